```python
import jax, jax.numpy as jnp
from jax import lax
import numpy as np

D_MODEL = 2048
BATCH = 8
SEQ = 4096
DEPTH = 2

CTX_LEN = 256
GRID_W = 64
HEAD_DIM = 128
N_HG = 4
N_NA = 8
N_CV = 4
HG_W = N_HG * HEAD_DIM
NA_W = N_NA * HEAD_DIM
CV_W = N_CV * HEAD_DIM
MIX_W = HG_W + NA_W + CV_W
IN_SPLITS = (HG_W, HG_W, HG_W, NA_W, NA_W, HG_W, HG_W, NA_W, CV_W, CV_W, CV_W)
IN_W = sum(IN_SPLITS)
CTX_PIECES = 5
HG_CHUNK = 64
NA_WIN_R = 8
NA_WIN_C = 16
NA_QBLK_C = 16
NA_KEY_C = 32
CONV_W = 3
D_FF = 5632
EPS = 1e-6
F_FLOOR = 1e-30
ATTN_SCALE = HEAD_DIM ** -0.5
NEG_INF = -1e30

kernel_name = 'hybrid_hgrn2_natten_shortconv_dit_block'


def _rms_norm(x, w):
    xf = x.astype(jnp.float32)
    y = xf * lax.rsqrt(jnp.mean(xf * xf, axis=-1, keepdims=True) + EPS)
    return (y * w.astype(jnp.float32)).astype(x.dtype)


def _modulate(h, shift, scale):
    return h * (1 + scale) + shift


def _heads(a):
    return a.reshape(a.shape[:-1] + (a.shape[-1] // HEAD_DIM, HEAD_DIM))


def _merge(a):
    return a.reshape(a.shape[:-2] + (a.shape[-2] * a.shape[-1],))


def _flip(a):
    return a[:, ::-1]


def _split_cols(u, n_pieces):
    cuts = [int(v) for v in np.cumsum(IN_SPLITS[:n_pieces])[:-1]]
    return jnp.split(u, cuts, axis=-1)


def _dwconv(x, w):
    l_ = x.shape[1]
    pad = CONV_W // 2
    xp = jnp.pad(x, ((0, 0), (pad, pad), (0, 0)))
    out = xp[:, :l_] * w[0]
    for t in range(1, CONV_W):
        out = out + xp[:, t:t + l_] * w[t]
    return out


def _hgrn2_forget(z, lb):
    zf = z.astype(jnp.float32)
    f = lb + (1.0 - lb) * jax.nn.sigmoid(zf)
    log_f = jnp.log(jnp.maximum(f, F_FLOOR))
    k = (1.0 - lb) * jax.nn.sigmoid(-zf)
    return _heads(log_f), _heads(k)


def _hgrn2_scan(log_f, k, v, s0, q=None):
    b_, l_, h_, _ = log_f.shape
    n_chunks = l_ // HG_CHUNK

    def chunks(a):
        return a.reshape(b_, n_chunks, HG_CHUNK, h_, a.shape[-1]).transpose(1, 0, 3, 2, 4)

    tri = jnp.tril(jnp.ones((HG_CHUNK, HG_CHUNK), dtype=bool))[:, :, None]
    with_out = q is not None
    xs = (chunks(log_f), chunks(k), chunks(v)) + ((chunks(q),) if with_out else ())

    def step(s, inp):
        g, kc, vc = inp[0], inp[1], inp[2]
        cum = jnp.cumsum(g, axis=2)
        tot = cum[:, :, -1]
        s_new = jnp.exp(tot)[..., None] * s + jnp.einsum('bhcd,bhce->bhde', kc * jnp.exp(tot[:, :, None] - cum), vc)
        if not with_out:
            return s_new, None
        qc = inp[3]
        o_inter = jnp.einsum('bhcd,bhde->bhce', qc * jnp.exp(cum), s)
        diff = cum[:, :, :, None, :] - cum[:, :, None, :, :]
        decay = jnp.exp(jnp.where(tri, diff, NEG_INF))
        att = jnp.einsum('bhid,bhjd,bhijd->bhij', qc, kc, decay)
        return s_new, o_inter + jnp.einsum('bhij,bhje->bhie', att, vc)

    s_fin, o = lax.scan(step, s0, xs)
    if not with_out:
        return s_fin, None
    return s_fin, o.transpose(1, 0, 3, 2, 4).reshape(b_, l_, h_, v.shape[-1])


def _hgrn2_readout(o, g, w):
    return _merge(_rms_norm(o, w)).astype(g.dtype) * jax.nn.silu(g)


def _neighbourhood_attention(q, k, v, k_ctx, v_ctx, rpb, rows):
    b_, n_, h_, d_ = q.shape
    win_r = min(NA_WIN_R, rows)
    n_cb = GRID_W // NA_QBLK_C
    qcol = np.arange(GRID_W).reshape(n_cb, NA_QBLK_C)
    kstart = np.clip(np.arange(n_cb) * NA_QBLK_C - NA_WIN_C // 2, 0, GRID_W - NA_KEY_C)
    kcol = kstart[:, None] + np.arange(NA_KEY_C)[None]
    wstart = np.clip(qcol - NA_WIN_C // 2, 0, GRID_W - NA_WIN_C)
    col_ok = (kcol[:, None, :] >= wstart[:, :, None]) & (kcol[:, None, :] < wstart[:, :, None] + NA_WIN_C)
    mask = jnp.asarray(col_ok[:, :, None, :])
    idx_c = jnp.asarray(np.clip(kcol[:, None, :] - qcol[:, :, None] + NA_WIN_C - 1, 0, 2 * NA_WIN_C - 2)[:, :, None, :])
    qg = q.reshape(b_, rows, GRID_W, h_, d_)
    kg = k.reshape(b_, rows, GRID_W, h_, d_)
    vg = v.reshape(b_, rows, GRID_W, h_, d_)
    n_loc = win_r * NA_KEY_C

    def row_block(r):
        rs = jnp.clip(r - win_r // 2, 0, rows - win_r)
        qr = lax.dynamic_index_in_dim(qg, r, axis=1, keepdims=False).reshape(b_, n_cb, NA_QBLK_C, h_, d_)
        kr = lax.dynamic_slice_in_dim(kg, rs, win_r, axis=1)[:, :, kcol]
        vr = lax.dynamic_slice_in_dim(vg, rs, win_r, axis=1)[:, :, kcol]
        idx_r = (rs + jnp.arange(win_r) - r + NA_WIN_R - 1)[None, None, :, None]
        bias = rpb[:, idx_r, idx_c].astype(jnp.float32)
        s_loc = jnp.einsum('bmqhd,brmkhd->bhmqrk', qr, kr).astype(jnp.float32) * ATTN_SCALE
        s_loc = jnp.where(mask, s_loc + bias, NEG_INF)
        s_ctx = jnp.einsum('bmqhd,bkhd->bhmqk', qr, k_ctx).astype(jnp.float32) * ATTN_SCALE
        logits = jnp.concatenate([s_loc.reshape(s_loc.shape[:4] + (n_loc,)), s_ctx], axis=-1)
        p = jax.nn.softmax(logits, axis=-1).astype(v.dtype)
        p_loc = p[..., :n_loc].reshape(s_loc.shape)
        o = jnp.einsum('bhmqrk,brmkhd->bmqhd', p_loc, vr) + jnp.einsum('bhmqk,bkhd->bmqhd', p[..., n_loc:], v_ctx)
        return o.reshape(b_, GRID_W, h_, d_)

    o = lax.map(row_block, jnp.arange(rows))
    return jnp.moveaxis(o, 0, 1).reshape(b_, n_, h_, d_)


def _context_attention(q, k, v):
    s = jnp.einsum('bqhd,bkhd->bhqk', q, k).astype(jnp.float32) * ATTN_SCALE
    p = jax.nn.softmax(s, axis=-1).astype(v.dtype)
    return jnp.einsum('bhqk,bkhd->bqhd', p, v)


def _short_conv(b_gate, c_gate, v, w):
    return b_gate * _dwconv(c_gate * v, w)


def _conv_ffn(h, w_up, cw, cb, w_down):
    u = _dwconv(h @ w_up, cw) + cb
    gate, val = jnp.split(u, 2, axis=-1)
    return (jax.nn.silu(gate) * val) @ w_down


def _layer(x, ctx, ada, ada_ctx, lb_fw, lb_bw, ln1, ln2, w_in, hg_norm, q_norm, k_norm, rpb, na_onorm,
           cv_w, cv_onorm, w_out, w_up, f_cw, f_cb, w_down, rows, last):
    sh1, sc1, g1, sh2, sc2, g2 = jnp.split(ada[:, None, :], 6, axis=-1)
    cm = jnp.split(ada_ctx, 2 if last else 6)
    h = _modulate(_rms_norm(x, ln1), sh1, sc1)
    hc = _modulate(_rms_norm(ctx, ln1), cm[0], cm[1])
    (f_fw, f_bw, i_v, na_k, na_v, hg_q, hg_g, na_q, cv_b, cv_c, cv_v) = _split_cols(h @ w_in, len(IN_SPLITS))
    n_c = CTX_PIECES if last else len(IN_SPLITS)
    uc = _split_cols(hc @ w_in[:, :sum(IN_SPLITS[:n_c])], n_c)

    s0 = jnp.zeros((x.shape[0], N_HG, HEAD_DIM, HEAD_DIM), jnp.float32)
    cf_fw, ck_fw = _hgrn2_forget(uc[0], lb_fw)
    cf_bw, ck_bw = _hgrn2_forget(uc[1], lb_bw)
    c_val = _heads(uc[2].astype(jnp.float32))
    c_q = None if last else _heads(jax.nn.silu(uc[5]).astype(jnp.float32))
    s_fw, co_fw = _hgrn2_scan(cf_fw, ck_fw, c_val, s0, c_q)
    s_bw, co_bw = _hgrn2_scan(_flip(cf_bw), _flip(ck_bw), _flip(c_val), s0, None if last else _flip(c_q))
    lf_fw, lk_fw = _hgrn2_forget(f_fw, lb_fw)
    lf_bw, lk_bw = _hgrn2_forget(f_bw, lb_bw)
    l_val = _heads(i_v.astype(jnp.float32))
    l_q = _heads(jax.nn.silu(hg_q).astype(jnp.float32))
    _, o_fw = _hgrn2_scan(lf_fw, lk_fw, l_val, s_fw, l_q)
    _, o_bw = _hgrn2_scan(_flip(lf_bw), _flip(lk_bw), _flip(l_val), s_bw, _flip(l_q))
    hg_out = _hgrn2_readout(o_fw + _flip(o_bw), hg_g, hg_norm)

    k_c = _rms_norm(_heads(uc[3]), k_norm)
    v_c = _heads(uc[4])
    na_o = _neighbourhood_attention(_rms_norm(_heads(na_q), q_norm), _rms_norm(_heads(na_k), k_norm),
                                    _heads(na_v), k_c, v_c, rpb, rows)
    na_onorm_h = na_onorm.reshape(N_NA, HEAD_DIM)
    na_out = _merge(_rms_norm(na_o, na_onorm_h))

    cv_onorm_h = cv_onorm.reshape(N_CV, HEAD_DIM)
    cv_out = _merge(_rms_norm(_heads(_short_conv(cv_b, cv_c, cv_v, cv_w)), cv_onorm_h))

    x = x + g1 * (jnp.concatenate([hg_out, na_out, cv_out], axis=-1) @ w_out)
    x = x + g2 * _conv_ffn(_modulate(_rms_norm(x, ln2), sh2, sc2), w_up, f_cw, f_cb, w_down)
    if last:
        return x, None

    hg_c = _hgrn2_readout(co_fw + _flip(co_bw), uc[6], hg_norm)
    na_c = _merge(_rms_norm(_context_attention(_rms_norm(_heads(uc[7]), q_norm), k_c, v_c), na_onorm_h))
    cv_co = _merge(_rms_norm(_heads(_short_conv(uc[8], uc[9], uc[10], cv_w)), cv_onorm_h))
    ctx = ctx + cm[2] * (jnp.concatenate([hg_c, na_c, cv_co], axis=-1) @ w_out)
    ctx = ctx + cm[5] * _conv_ffn(_modulate(_rms_norm(ctx, ln2), cm[3], cm[4]), w_up, f_cw, f_cb, w_down)
    return x, ctx


def _fwd_setup_inputs(seed: int = 0) -> dict:
    key = jax.random.key(seed)
    ks = jax.random.split(key, 24)
    d = D_MODEL

    def nrm(k, shape, scale):
        return jax.random.normal(k, shape, jnp.float32) * scale

    return {
        'x': nrm(ks[0], (BATCH, SEQ, d), 1.0),
        'c': nrm(ks[1], (BATCH, d), 1.0),
        'ctx': nrm(ks[2], (BATCH, CTX_LEN, d), 1.0),
        'c_ctx': nrm(ks[3], (d,), 1.0),
        'w_ada': nrm(ks[4], (DEPTH, d, 6 * d), 0.5 * d ** -0.5),
        'b_ada': nrm(ks[5], (DEPTH, 6 * d), 0.01),
        'ln1_w': 1.0 + nrm(ks[6], (DEPTH, d), 0.02),
        'ln2_w': 1.0 + nrm(ks[7], (DEPTH, d), 0.02),
        'w_in': nrm(ks[8], (DEPTH, d, IN_W), d ** -0.5),
        'hg_lb_logits': nrm(ks[9], (2, DEPTH, HG_W), 0.5),
        'hg_norm_w': 1.0 + nrm(ks[10], (DEPTH, HEAD_DIM), 0.02),
        'na_q_norm_w': 1.0 + nrm(ks[11], (DEPTH, HEAD_DIM), 0.02),
        'na_k_norm_w': 1.0 + nrm(ks[12], (DEPTH, HEAD_DIM), 0.02),
        'na_rpb': nrm(ks[13], (DEPTH, N_NA, 2 * NA_WIN_R - 1, 2 * NA_WIN_C - 1), 0.1),
        'na_out_norm_w': 1.0 + nrm(ks[14], (DEPTH, NA_W), 0.02),
        'cv_w': nrm(ks[15], (DEPTH, CONV_W, CV_W), CONV_W ** -0.5),
        'cv_out_norm_w': 1.0 + nrm(ks[16], (DEPTH, CV_W), 0.02),
        'w_out': nrm(ks[17], (DEPTH, MIX_W, d), MIX_W ** -0.5),
        'w_up': nrm(ks[18], (DEPTH, d, 2 * D_FF), d ** -0.5),
        'ffn_conv_w': nrm(ks[19], (DEPTH, CONV_W, 2 * D_FF), CONV_W ** -0.5),
        'ffn_conv_b': nrm(ks[20], (DEPTH, 2 * D_FF), 0.01),
        'w_down': nrm(ks[21], (DEPTH, D_FF, d), D_FF ** -0.5),
    }


def _fwd_reference(x, c, ctx, c_ctx, w_ada, b_ada, ln1_w, ln2_w, w_in, hg_lb_logits, hg_norm_w, na_q_norm_w,
              na_k_norm_w, na_rpb, na_out_norm_w, cv_w, cv_out_norm_w, w_out, w_up, ffn_conv_w, ffn_conv_b, w_down):
    rows = x.shape[1] // GRID_W
    lb_sm = jax.nn.softmax(hg_lb_logits.astype(jnp.float32), axis=1)
    lb_all = jnp.cumsum(lb_sm, axis=1) - lb_sm[:, :1]
    silu_c = jax.nn.silu(c)
    silu_cc = jax.nn.silu(c_ctx)
    for l in range(DEPTH):
        last = l == DEPTH - 1
        ada = silu_c @ w_ada[l] + b_ada[l]
        n_ada = (2 if last else 6) * D_MODEL
        ada_ctx = silu_cc @ w_ada[l][:, :n_ada] + b_ada[l][:n_ada]
        x, ctx = _layer(x, ctx, ada, ada_ctx, lb_all[0, l], lb_all[1, l], ln1_w[l], ln2_w[l], w_in[l],
                        hg_norm_w[l], na_q_norm_w[l], na_k_norm_w[l], na_rpb[l], na_out_norm_w[l], cv_w[l],
                        cv_out_norm_w[l], w_out[l], w_up[l], ffn_conv_w[l], ffn_conv_b[l], w_down[l], rows, last)
    return x


import jax as _jax
import jax.numpy as _jnp

TWIN_FORMAT = 'train_step'
FWD_PARAMS = ['x', 'c', 'ctx', 'c_ctx', 'w_ada', 'b_ada', 'ln1_w', 'ln2_w', 'w_in', 'hg_lb_logits', 'hg_norm_w', 'na_q_norm_w', 'na_k_norm_w', 'na_rpb', 'na_out_norm_w', 'cv_w', 'cv_out_norm_w', 'w_out', 'w_up', 'ffn_conv_w', 'ffn_conv_b', 'w_down']
TWIN_WEIGHTS = ['c_ctx', 'w_ada', 'b_ada', 'ln1_w', 'ln2_w', 'w_in', 'hg_lb_logits', 'hg_norm_w', 'na_q_norm_w', 'na_k_norm_w', 'na_rpb', 'na_out_norm_w', 'cv_w', 'cv_out_norm_w', 'w_out', 'w_up', 'ffn_conv_w', 'ffn_conv_b', 'w_down']
TWIN_DIFF_INPUT = 'x'
TWIN_INPUTS = ['x', 'c', 'ctx', 'c_ctx', 'w_ada', 'b_ada', 'ln1_w', 'ln2_w', 'w_in', 'hg_lb_logits', 'hg_norm_w', 'na_q_norm_w', 'na_k_norm_w', 'na_rpb', 'na_out_norm_w', 'cv_w', 'cv_out_norm_w', 'w_out', 'w_up', 'ffn_conv_w', 'ffn_conv_b', 'w_down', 'loss_target', 'm_c_ctx', 'm_w_ada', 'm_b_ada', 'm_ln1_w', 'm_ln2_w', 'm_w_in', 'm_hg_lb_logits', 'm_hg_norm_w', 'm_na_q_norm_w', 'm_na_k_norm_w', 'm_na_rpb', 'm_na_out_norm_w', 'm_cv_w', 'm_cv_out_norm_w', 'm_w_out', 'm_w_up', 'm_ffn_conv_w', 'm_ffn_conv_b', 'm_w_down', 'v_c_ctx', 'v_w_ada', 'v_b_ada', 'v_ln1_w', 'v_ln2_w', 'v_w_in', 'v_hg_lb_logits', 'v_hg_norm_w', 'v_na_q_norm_w', 'v_na_k_norm_w', 'v_na_rpb', 'v_na_out_norm_w', 'v_cv_w', 'v_cv_out_norm_w', 'v_w_out', 'v_w_up', 'v_ffn_conv_w', 'v_ffn_conv_b', 'v_w_down']
TWIN_OUTPUTS = ['loss', 'grad_x', 'grad_c_ctx', 'grad_w_ada', 'grad_b_ada', 'grad_ln1_w', 'grad_ln2_w', 'grad_w_in', 'grad_hg_lb_logits', 'grad_hg_norm_w', 'grad_na_q_norm_w', 'grad_na_k_norm_w', 'grad_na_rpb', 'grad_na_out_norm_w', 'grad_cv_w', 'grad_cv_out_norm_w', 'grad_w_out', 'grad_w_up', 'grad_ffn_conv_w', 'grad_ffn_conv_b', 'grad_w_down', 'delta_c_ctx', 'delta_w_ada', 'delta_b_ada', 'delta_ln1_w', 'delta_ln2_w', 'delta_w_in', 'delta_hg_lb_logits', 'delta_hg_norm_w', 'delta_na_q_norm_w', 'delta_na_k_norm_w', 'delta_na_rpb', 'delta_na_out_norm_w', 'delta_cv_w', 'delta_cv_out_norm_w', 'delta_w_out', 'delta_w_up', 'delta_ffn_conv_w', 'delta_ffn_conv_b', 'delta_w_down', 'new_m_c_ctx', 'new_m_w_ada', 'new_m_b_ada', 'new_m_ln1_w', 'new_m_ln2_w', 'new_m_w_in', 'new_m_hg_lb_logits', 'new_m_hg_norm_w', 'new_m_na_q_norm_w', 'new_m_na_k_norm_w', 'new_m_na_rpb', 'new_m_na_out_norm_w', 'new_m_cv_w', 'new_m_cv_out_norm_w', 'new_m_w_out', 'new_m_w_up', 'new_m_ffn_conv_w', 'new_m_ffn_conv_b', 'new_m_w_down', 'new_v_c_ctx', 'new_v_w_ada', 'new_v_b_ada', 'new_v_ln1_w', 'new_v_ln2_w', 'new_v_w_in', 'new_v_hg_lb_logits', 'new_v_hg_norm_w', 'new_v_na_q_norm_w', 'new_v_na_k_norm_w', 'new_v_na_rpb', 'new_v_na_out_norm_w', 'new_v_cv_w', 'new_v_cv_out_norm_w', 'new_v_w_out', 'new_v_w_up', 'new_v_ffn_conv_w', 'new_v_ffn_conv_b', 'new_v_w_down']
TWIN_LEAF_KINDS = {'loss': 'loss', 'grad_x': 'grad_x', 'grad_c_ctx': 'grad_w', 'grad_w_ada': 'grad_w', 'grad_b_ada': 'grad_w', 'grad_ln1_w': 'grad_w', 'grad_ln2_w': 'grad_w', 'grad_w_in': 'grad_w', 'grad_hg_lb_logits': 'grad_w', 'grad_hg_norm_w': 'grad_w', 'grad_na_q_norm_w': 'grad_w', 'grad_na_k_norm_w': 'grad_w', 'grad_na_rpb': 'grad_w', 'grad_na_out_norm_w': 'grad_w', 'grad_cv_w': 'grad_w', 'grad_cv_out_norm_w': 'grad_w', 'grad_w_out': 'grad_w', 'grad_w_up': 'grad_w', 'grad_ffn_conv_w': 'grad_w', 'grad_ffn_conv_b': 'grad_w', 'grad_w_down': 'grad_w', 'delta_c_ctx': 'delta_w', 'delta_w_ada': 'delta_w', 'delta_b_ada': 'delta_w', 'delta_ln1_w': 'delta_w', 'delta_ln2_w': 'delta_w', 'delta_w_in': 'delta_w', 'delta_hg_lb_logits': 'delta_w', 'delta_hg_norm_w': 'delta_w', 'delta_na_q_norm_w': 'delta_w', 'delta_na_k_norm_w': 'delta_w', 'delta_na_rpb': 'delta_w', 'delta_na_out_norm_w': 'delta_w', 'delta_cv_w': 'delta_w', 'delta_cv_out_norm_w': 'delta_w', 'delta_w_out': 'delta_w', 'delta_w_up': 'delta_w', 'delta_ffn_conv_w': 'delta_w', 'delta_ffn_conv_b': 'delta_w', 'delta_w_down': 'delta_w', 'new_m_c_ctx': 'new_m', 'new_m_w_ada': 'new_m', 'new_m_b_ada': 'new_m', 'new_m_ln1_w': 'new_m', 'new_m_ln2_w': 'new_m', 'new_m_w_in': 'new_m', 'new_m_hg_lb_logits': 'new_m', 'new_m_hg_norm_w': 'new_m', 'new_m_na_q_norm_w': 'new_m', 'new_m_na_k_norm_w': 'new_m', 'new_m_na_rpb': 'new_m', 'new_m_na_out_norm_w': 'new_m', 'new_m_cv_w': 'new_m', 'new_m_cv_out_norm_w': 'new_m', 'new_m_w_out': 'new_m', 'new_m_w_up': 'new_m', 'new_m_ffn_conv_w': 'new_m', 'new_m_ffn_conv_b': 'new_m', 'new_m_w_down': 'new_m', 'new_v_c_ctx': 'new_v', 'new_v_w_ada': 'new_v', 'new_v_b_ada': 'new_v', 'new_v_ln1_w': 'new_v', 'new_v_ln2_w': 'new_v', 'new_v_w_in': 'new_v', 'new_v_hg_lb_logits': 'new_v', 'new_v_hg_norm_w': 'new_v', 'new_v_na_q_norm_w': 'new_v', 'new_v_na_k_norm_w': 'new_v', 'new_v_na_rpb': 'new_v', 'new_v_na_out_norm_w': 'new_v', 'new_v_cv_w': 'new_v', 'new_v_cv_out_norm_w': 'new_v', 'new_v_w_out': 'new_v', 'new_v_w_up': 'new_v', 'new_v_ffn_conv_w': 'new_v', 'new_v_ffn_conv_b': 'new_v', 'new_v_w_down': 'new_v'}


def _forward(args):
    return _fwd_reference(*[args[k] for k in FWD_PARAMS])


def _output_shape():
    def fwd():
        inp = _fwd_setup_inputs(0)
        return _fwd_reference(*[inp[k] for k in FWD_PARAMS])
    out = _jax.eval_shape(fwd)
    return out.shape, out.dtype

N_MICROBATCH = 1
ADAM_LR = 0.001
ADAM_B1 = 0.9
ADAM_B2 = 0.999
ADAM_EPS = 1e-08
ADAM_WD = 0.01
ADAM_STEP = 10
PER_EXAMPLE_BATCH_AXIS = {'x': 0, 'c': 0, 'ctx': 0, 'loss_target': 0}
SHARED_INPUTS = []
_WEIGHT_DTYPES = {'c_ctx': _jnp.float32, 'w_ada': _jnp.float32, 'b_ada': _jnp.float32, 'ln1_w': _jnp.float32, 'ln2_w': _jnp.float32, 'w_in': _jnp.float32, 'hg_lb_logits': _jnp.float32, 'hg_norm_w': _jnp.float32, 'na_q_norm_w': _jnp.float32, 'na_k_norm_w': _jnp.float32, 'na_rpb': _jnp.float32, 'na_out_norm_w': _jnp.float32, 'cv_w': _jnp.float32, 'cv_out_norm_w': _jnp.float32, 'w_out': _jnp.float32, 'w_up': _jnp.float32, 'ffn_conv_w': _jnp.float32, 'ffn_conv_b': _jnp.float32, 'w_down': _jnp.float32}
MOMENT_SCALE = {'c_ctx': 8.852271e-01, 'w_ada': 8.121385e-01, 'b_ada': 1.936446e+00, 'ln1_w': 3.030034e-01, 'ln2_w': 1.576656e+00, 'w_in': 3.577244e-01, 'hg_lb_logits': 1.607722e-03, 'hg_norm_w': 2.502725e+00, 'na_q_norm_w': 1.583578e-01, 'na_k_norm_w': 1.582499e-01, 'na_rpb': 1.315406e-02, 'na_out_norm_w': 2.082741e+00, 'cv_w': 5.605284e-02, 'cv_out_norm_w': 1.758761e+00, 'w_out': 6.656385e-01, 'w_up': 6.575382e-02, 'ffn_conv_w': 2.501906e-01, 'ffn_conv_b': 2.225683e-01, 'w_down': 7.344018e-02}


def _to_microbatches(a, axis):
    t = _jnp.moveaxis(a, axis, 0)
    t = t.reshape((N_MICROBATCH, t.shape[0] // N_MICROBATCH) + t.shape[1:])
    return _jnp.moveaxis(t, 1, axis + 1)


def setup_inputs(seed: int = 0) -> dict:
    inp = _fwd_setup_inputs(seed)
    key = _jax.random.fold_in(_jax.random.key(seed), 7919)
    shape, _ = _output_shape()
    out = dict(inp)
    out["loss_target"] = _jax.random.normal(_jax.random.fold_in(key, 0), shape, _jnp.float32)
    for i, name in enumerate(TWIN_WEIGHTS):
        w = inp[name].astype(_jnp.float32)
        if MOMENT_SCALE is None:
            s = _jnp.sqrt(_jnp.mean(_jnp.square(w)) + 1e-30)
        else:
            s = MOMENT_SCALE[name]
        km, kv = _jax.random.split(_jax.random.fold_in(key, i + 1))
        out[name] = w
        out["m_" + name] = s * _jax.random.normal(km, w.shape, _jnp.float32)
        out["v_" + name] = (s * s) * _jax.random.uniform(kv, w.shape, _jnp.float32, 0.5, 1.5)
    if N_MICROBATCH > 1:
        for name, axis in PER_EXAMPLE_BATCH_AXIS.items():
            out[name] = _to_microbatches(out[name], axis)
    return {'x': out['x'], 'c': out['c'], 'ctx': out['ctx'], 'c_ctx': out['c_ctx'], 'w_ada': out['w_ada'], 'b_ada': out['b_ada'], 'ln1_w': out['ln1_w'], 'ln2_w': out['ln2_w'], 'w_in': out['w_in'], 'hg_lb_logits': out['hg_lb_logits'], 'hg_norm_w': out['hg_norm_w'], 'na_q_norm_w': out['na_q_norm_w'], 'na_k_norm_w': out['na_k_norm_w'], 'na_rpb': out['na_rpb'], 'na_out_norm_w': out['na_out_norm_w'], 'cv_w': out['cv_w'], 'cv_out_norm_w': out['cv_out_norm_w'], 'w_out': out['w_out'], 'w_up': out['w_up'], 'ffn_conv_w': out['ffn_conv_w'], 'ffn_conv_b': out['ffn_conv_b'], 'w_down': out['w_down'], 'loss_target': out['loss_target'], 'm_c_ctx': out['m_c_ctx'], 'm_w_ada': out['m_w_ada'], 'm_b_ada': out['m_b_ada'], 'm_ln1_w': out['m_ln1_w'], 'm_ln2_w': out['m_ln2_w'], 'm_w_in': out['m_w_in'], 'm_hg_lb_logits': out['m_hg_lb_logits'], 'm_hg_norm_w': out['m_hg_norm_w'], 'm_na_q_norm_w': out['m_na_q_norm_w'], 'm_na_k_norm_w': out['m_na_k_norm_w'], 'm_na_rpb': out['m_na_rpb'], 'm_na_out_norm_w': out['m_na_out_norm_w'], 'm_cv_w': out['m_cv_w'], 'm_cv_out_norm_w': out['m_cv_out_norm_w'], 'm_w_out': out['m_w_out'], 'm_w_up': out['m_w_up'], 'm_ffn_conv_w': out['m_ffn_conv_w'], 'm_ffn_conv_b': out['m_ffn_conv_b'], 'm_w_down': out['m_w_down'], 'v_c_ctx': out['v_c_ctx'], 'v_w_ada': out['v_w_ada'], 'v_b_ada': out['v_b_ada'], 'v_ln1_w': out['v_ln1_w'], 'v_ln2_w': out['v_ln2_w'], 'v_w_in': out['v_w_in'], 'v_hg_lb_logits': out['v_hg_lb_logits'], 'v_hg_norm_w': out['v_hg_norm_w'], 'v_na_q_norm_w': out['v_na_q_norm_w'], 'v_na_k_norm_w': out['v_na_k_norm_w'], 'v_na_rpb': out['v_na_rpb'], 'v_na_out_norm_w': out['v_na_out_norm_w'], 'v_cv_w': out['v_cv_w'], 'v_cv_out_norm_w': out['v_cv_out_norm_w'], 'v_w_out': out['v_w_out'], 'v_w_up': out['v_w_up'], 'v_ffn_conv_w': out['v_ffn_conv_w'], 'v_ffn_conv_b': out['v_ffn_conv_b'], 'v_w_down': out['v_w_down']}


def _loss(weights, diff, rest, loss_target):
    with _jax.named_scope("forward"):
        args = {**rest, TWIN_DIFF_INPUT: diff, **{k: w.astype(_WEIGHT_DTYPES[k]) for k, w in weights.items()}}
        y = _forward(args)
    with _jax.named_scope("loss_head"):
        err = _jnp.square(y.astype(_jnp.float32) - loss_target)
        return 0.5 * _jnp.sum(_jnp.mean(err, axis=-1)) if err.ndim else 0.5 * err


def _adamw(w, g, m, v):
    m = ADAM_B1 * m + (1.0 - ADAM_B1) * g
    v = ADAM_B2 * v + (1.0 - ADAM_B2) * _jnp.square(g)
    m_hat = m / (1.0 - ADAM_B1 ** ADAM_STEP)
    v_hat = v / (1.0 - ADAM_B2 ** ADAM_STEP)
    delta = -ADAM_LR * (m_hat / (_jnp.sqrt(v_hat) + ADAM_EPS) + ADAM_WD * w)
    return delta, m, v


def reference(x, c, ctx, c_ctx, w_ada, b_ada, ln1_w, ln2_w, w_in, hg_lb_logits, hg_norm_w, na_q_norm_w, na_k_norm_w, na_rpb, na_out_norm_w, cv_w, cv_out_norm_w, w_out, w_up, ffn_conv_w, ffn_conv_b, w_down, loss_target, m_c_ctx, m_w_ada, m_b_ada, m_ln1_w, m_ln2_w, m_w_in, m_hg_lb_logits, m_hg_norm_w, m_na_q_norm_w, m_na_k_norm_w, m_na_rpb, m_na_out_norm_w, m_cv_w, m_cv_out_norm_w, m_w_out, m_w_up, m_ffn_conv_w, m_ffn_conv_b, m_w_down, v_c_ctx, v_w_ada, v_b_ada, v_ln1_w, v_ln2_w, v_w_in, v_hg_lb_logits, v_hg_norm_w, v_na_q_norm_w, v_na_k_norm_w, v_na_rpb, v_na_out_norm_w, v_cv_w, v_cv_out_norm_w, v_w_out, v_w_up, v_ffn_conv_w, v_ffn_conv_b, v_w_down):
    given = dict(x=x, c=c, ctx=ctx, c_ctx=c_ctx, w_ada=w_ada, b_ada=b_ada, ln1_w=ln1_w, ln2_w=ln2_w, w_in=w_in, hg_lb_logits=hg_lb_logits, hg_norm_w=hg_norm_w, na_q_norm_w=na_q_norm_w, na_k_norm_w=na_k_norm_w, na_rpb=na_rpb, na_out_norm_w=na_out_norm_w, cv_w=cv_w, cv_out_norm_w=cv_out_norm_w, w_out=w_out, w_up=w_up, ffn_conv_w=ffn_conv_w, ffn_conv_b=ffn_conv_b, w_down=w_down, loss_target=loss_target, m_c_ctx=m_c_ctx, m_w_ada=m_w_ada, m_b_ada=m_b_ada, m_ln1_w=m_ln1_w, m_ln2_w=m_ln2_w, m_w_in=m_w_in, m_hg_lb_logits=m_hg_lb_logits, m_hg_norm_w=m_hg_norm_w, m_na_q_norm_w=m_na_q_norm_w, m_na_k_norm_w=m_na_k_norm_w, m_na_rpb=m_na_rpb, m_na_out_norm_w=m_na_out_norm_w, m_cv_w=m_cv_w, m_cv_out_norm_w=m_cv_out_norm_w, m_w_out=m_w_out, m_w_up=m_w_up, m_ffn_conv_w=m_ffn_conv_w, m_ffn_conv_b=m_ffn_conv_b, m_w_down=m_w_down, v_c_ctx=v_c_ctx, v_w_ada=v_w_ada, v_b_ada=v_b_ada, v_ln1_w=v_ln1_w, v_ln2_w=v_ln2_w, v_w_in=v_w_in, v_hg_lb_logits=v_hg_lb_logits, v_hg_norm_w=v_hg_norm_w, v_na_q_norm_w=v_na_q_norm_w, v_na_k_norm_w=v_na_k_norm_w, v_na_rpb=v_na_rpb, v_na_out_norm_w=v_na_out_norm_w, v_cv_w=v_cv_w, v_cv_out_norm_w=v_cv_out_norm_w, v_w_out=v_w_out, v_w_up=v_w_up, v_ffn_conv_w=v_ffn_conv_w, v_ffn_conv_b=v_ffn_conv_b, v_w_down=v_w_down)
    weights = {n: given[n] for n in TWIN_WEIGHTS}
    shared = {n: given[n] for n in SHARED_INPUTS}
    per_example = {n: given[n] for n in ['x', 'c', 'ctx']}
    grad_fn = _jax.value_and_grad(_loss, argnums=(0, 1))

    def one_microbatch(ex, loss_target):
        ex = dict(ex)
        diff = ex.pop(TWIN_DIFF_INPUT)
        return grad_fn(weights, diff, {**shared, **ex}, loss_target)

    if N_MICROBATCH == 1:
        loss, (grad_w, grad_x) = one_microbatch(per_example, given["loss_target"])
    else:
        def body(carry, xs):
            loss_sum, grad_sum = carry
            l_k, (gw_k, gx_k) = one_microbatch(xs[0], xs[1])
            with _jax.named_scope("update"):
                return (loss_sum + l_k, _jax.tree.map(_jnp.add, grad_sum, gw_k)), gx_k

        init = (_jnp.zeros((), _jnp.float32), _jax.tree.map(_jnp.zeros_like, weights))
        (loss, grad_w), grad_x = _jax.lax.scan(body, init, (per_example, given["loss_target"]))
    with _jax.named_scope("update"):
        delta_w, new_m, new_v = {}, {}, {}
        for n in TWIN_WEIGHTS:
            delta_w[n], new_m[n], new_v[n] = _adamw(weights[n], grad_w[n], given["m_" + n], given["v_" + n])
    return (loss, grad_x, *[grad_w[n] for n in TWIN_WEIGHTS], *[delta_w[n] for n in TWIN_WEIGHTS],
            *[new_m[n] for n in TWIN_WEIGHTS], *[new_v[n] for n in TWIN_WEIGHTS])
```

```python
import functools
import math

import numpy as np
import jax
import jax.numpy as jnp
from jax import lax
from jax.experimental import pallas as pl
from jax.experimental.pallas import tpu as pltpu

F32 = jnp.float32
BF16 = jnp.bfloat16
MESH = pl.DeviceIdType.MESH
ANY = pl.BlockSpec(memory_space=pl.ANY)
VMEM_SPEC = pl.BlockSpec(memory_space=pltpu.VMEM)

LANE = 128
CHUNK = 64
SUB = 16
GRID_W = 64
WIN_R = 8
WIN_C = 16
EPS = 1e-6
F_FLOOR = 1e-30
NEG_INF = -1e30
EXP_CLAMP = 80.0
ATTN_SCALE = LANE ** -0.5
VMEM_LIMIT = 56 * 1024 * 1024
ADAM_LR, ADAM_B1, ADAM_B2, ADAM_EPS, ADAM_WD, ADAM_STEP = 0.001, 0.9, 0.999, 1e-08, 0.01, 10


def _cp(*sem):
    return pltpu.CompilerParams(dimension_semantics=sem or None, vmem_limit_bytes=VMEM_LIMIT)


def _me():
    return lax.axis_index("x"), lax.axis_index("y"), lax.axis_index("c")


def allgather8(blocks, name, hbm=False):
    na = len(blocks)

    def body(*refs):
        x_refs, out_refs = refs[:na], refs[na:2 * na]
        send_sems, recv_sems, local_sems = refs[2 * na:]
        x, y, c = _me()
        me, sibling = (x, y, c), (x, y, 1 - c)
        chips = [(1 - x, y), (x, 1 - y), (1 - x, 1 - y)]

        def rows(a, px, py, pc):
            return out_refs[a].at[4 * px + 2 * py + pc]

        def copy(a, k, blk, to, src=None):
            return pltpu.make_async_remote_copy(
                src_ref=rows(a, *blk) if src is None else src, dst_ref=rows(a, *blk),
                send_sem=send_sems.at[a, k], recv_sem=recv_sems.at[a, k], device_id=to, device_id_type=MESH)

        mine = [pltpu.make_async_copy(x_refs[a], rows(a, *me), local_sems.at[a]) for a in range(na)]
        first, passed = [], []
        for a in range(na):
            mine[a].start()
            first.append(copy(a, 0, me, sibling, src=x_refs[a]))
            first += [copy(a, 1 + j, me, (*chip, c), src=x_refs[a]) for j, chip in enumerate(chips)]
        for cp in first:
            cp.start()
        for j, chip in enumerate(chips):
            for a in range(na):
                copy(a, 1 + j, (*chip, c), me).wait_recv()
                passed.append(copy(a, 4 + j, (*chip, c), sibling))
                passed[-1].start()
        for a in range(na):
            copy(a, 0, sibling, me).wait_recv()
            for j, chip in enumerate(chips):
                copy(a, 4 + j, (*chip, 1 - c), me).wait_recv()
        for cp in first + passed:
            cp.wait_send()
        for cp in mine:
            cp.wait()

    spec = ANY if hbm else VMEM_SPEC
    return pl.pallas_call(
        body, name=name, out_shape=[jax.ShapeDtypeStruct((8,) + b.shape, b.dtype) for b in blocks],
        in_specs=[spec] * na, out_specs=[spec] * na,
        scratch_shapes=[pltpu.SemaphoreType.DMA((na, 7)), pltpu.SemaphoreType.DMA((na, 7)),
                        pltpu.SemaphoreType.DMA((na,))],
        compiler_params=pltpu.CompilerParams(vmem_limit_bytes=VMEM_LIMIT),
    )(*blocks)


def swap_halves(gs, name):
    na = len(gs)
    hrs = [g.shape[1] // 2 for g in gs]

    def body(*refs):
        g_refs, o_refs = refs[:na], refs[na:2 * na]
        send_sems, recv_sems = refs[2 * na:]
        x, y, c = _me()
        cps = []
        for a in range(na):
            for s in range(4):
                src = g_refs[a].at[s, pl.ds(pl.multiple_of((1 - c) * hrs[a], 16), hrs[a]), :]
                cps.append(pltpu.make_async_remote_copy(
                    src_ref=src, dst_ref=o_refs[a].at[s], send_sem=send_sems.at[a, s], recv_sem=recv_sems.at[a, s],
                    device_id=(x, y, 1 - c), device_id_type=MESH))
        for cp in cps:
            cp.start()
        for cp in cps:
            cp.wait()

    return pl.pallas_call(
        body, name=name, out_shape=[jax.ShapeDtypeStruct((4, hrs[a], gs[a].shape[2]), gs[a].dtype) for a in range(na)],
        in_specs=[ANY] * na, out_specs=[ANY] * na,
        scratch_shapes=[pltpu.SemaphoreType.DMA((na, 4)), pltpu.SemaphoreType.DMA((na, 4))],
    )(*gs)


def chip_alltoall(gs, name):
    na = len(gs)

    def body(*refs):
        g_refs, o_refs = refs[:na], refs[na:2 * na]
        send_sems, recv_sems, local_sems = refs[2 * na:]
        x, y, c = _me()
        mine = 2 * x + y
        own, cps = [], []
        for a in range(na):
            own.append(pltpu.make_async_copy(g_refs[a].at[mine], o_refs[a].at[mine], local_sems.at[a]))
            own[-1].start()
            for k, (px, py) in enumerate([(1 - x, y), (x, 1 - y), (1 - x, 1 - y)]):
                cps.append(pltpu.make_async_remote_copy(
                    src_ref=g_refs[a].at[2 * px + py], dst_ref=o_refs[a].at[mine], send_sem=send_sems.at[a, k],
                    recv_sem=recv_sems.at[a, k], device_id=(px, py, c), device_id_type=MESH))
        for cp in cps:
            cp.start()
        for cp in cps:
            cp.wait()
        for cp in own:
            cp.wait()

    return pl.pallas_call(
        body, name=name, out_shape=[jax.ShapeDtypeStruct(g.shape, g.dtype) for g in gs],
        in_specs=[ANY] * na, out_specs=[ANY] * na,
        scratch_shapes=[pltpu.SemaphoreType.DMA((na, 3)), pltpu.SemaphoreType.DMA((na, 3)),
                        pltpu.SemaphoreType.DMA((na,))],
    )(*gs)


def share_halves(vs, name):
    na = len(vs)

    def body(*refs):
        v_refs, o_refs = refs[:na], refs[na:2 * na]
        send_sems, recv_sems, local_sems = refs[2 * na:]
        x, y, c = _me()
        own, cps = [], []
        for a in range(na):
            own.append(pltpu.make_async_copy(v_refs[a], o_refs[a].at[c], local_sems.at[a]))
            own[-1].start()
            cps.append(pltpu.make_async_remote_copy(
                src_ref=v_refs[a], dst_ref=o_refs[a].at[c], send_sem=send_sems.at[a], recv_sem=recv_sems.at[a],
                device_id=(x, y, 1 - c), device_id_type=MESH))
        for cp in cps:
            cp.start()
        for cp in cps:
            cp.wait()
        for cp in own:
            cp.wait()

    return pl.pallas_call(
        body, name=name, out_shape=[jax.ShapeDtypeStruct((2,) + v.shape, v.dtype) for v in vs],
        in_specs=[ANY] * na, out_specs=[ANY] * na,
        scratch_shapes=[pltpu.SemaphoreType.DMA((na,)), pltpu.SemaphoreType.DMA((na,)), pltpu.SemaphoreType.DMA((na,))],
    )(*vs)


def _row_block(rows, cap):
    rb = math.gcd(rows, cap)
    return rb if rb % 8 == 0 else rows


def sum_leading(x, out_dtype, name):
    n, r, c = x.shape
    rb = _row_block(r, 1024)

    def body(x_ref, o_ref):
        acc = x_ref[0].astype(F32)
        for k in range(1, n):
            acc = acc + x_ref[k].astype(F32)
        o_ref[...] = acc.astype(o_ref.dtype)

    return pl.pallas_call(
        body, name=name, grid=(r // rb,), out_shape=jax.ShapeDtypeStruct((r, c), out_dtype),
        in_specs=[pl.BlockSpec((n, rb, c), lambda i: (0, i, 0))], out_specs=pl.BlockSpec((rb, c), lambda i: (i, 0)),
        compiler_params=_cp("parallel"),
    )(x)


def pair_sum(g, got, core, name):
    _, r2, n = g.shape
    hr = r2 // 2
    rb = math.gcd(hr, 512)
    nb = hr // rb

    def body(c_ref, a_ref, b_ref, o_ref):
        o_ref[...] = (a_ref[...].astype(F32) + b_ref[...].astype(F32)).astype(o_ref.dtype)

    spec = pl.BlockSpec((None, rb, n), lambda s, i, c_ref: (s, i, 0))
    return pl.pallas_call(
        body, name=name, out_shape=jax.ShapeDtypeStruct((4, hr, n), g.dtype),
        grid_spec=pltpu.PrefetchScalarGridSpec(
            num_scalar_prefetch=1, grid=(4, nb),
            in_specs=[pl.BlockSpec((None, rb, n), lambda s, i, c_ref: (s, c_ref[0] * nb + i, 0)), spec],
            out_specs=spec),
        compiler_params=_cp("parallel", "parallel"),
    )(core, g, got)


def adamw(w, gs, m, v, name):
    rows, c = w.shape
    ng = len(gs)
    r = rows // ng
    rb = _row_block(r, 128 if c > 2048 else 256 if c > 1024 else 1024)
    nb = r // rb
    bc1 = 1.0 - ADAM_B1 ** ADAM_STEP
    bc2 = 1.0 - ADAM_B2 ** ADAM_STEP

    def body(w_ref, *refs):
        g_refs, (m_ref, v_ref, g_out, d_ref, nm_ref, nv_ref) = refs[:ng], refs[ng:]
        part = pl.program_id(0) // nb
        gg = g_refs[0][...]
        for k in range(1, ng):
            gg = jnp.where(part == k, g_refs[k][...], gg)
        nm = ADAM_B1 * m_ref[...] + (1.0 - ADAM_B1) * gg
        nv = ADAM_B2 * v_ref[...] + (1.0 - ADAM_B2) * (gg * gg)
        g_out[...] = gg
        d_ref[...] = -ADAM_LR * ((nm / bc1) / (jnp.sqrt(nv / bc2) + ADAM_EPS) + ADAM_WD * w_ref[...])
        nm_ref[...] = nm
        nv_ref[...] = nv

    spec = pl.BlockSpec((rb, c), lambda i: (i, 0))
    gspecs = [pl.BlockSpec((rb, c), functools.partial(lambda k, i: (jnp.clip(i - k * nb, 0, nb - 1), 0), k))
              for k in range(ng)]
    sds = jax.ShapeDtypeStruct((rows, c), F32)
    return pl.pallas_call(
        body, name=name, grid=(ng * nb,), out_shape=(sds,) * 4, in_specs=[spec] + gspecs + [spec, spec],
        out_specs=(spec,) * 4, compiler_params=_cp("parallel"),
    )(w, *gs, m, v)


def _pick(n, prefs):
    for p in prefs:
        if n % p == 0:
            return p
    return n


def _mm_body(dims, nk, out_dtype):
    def body(a_ref, b_ref, o_ref, acc):
        kk = pl.program_id(2)
        part = lax.dot_general(a_ref[...].astype(BF16), b_ref[...].astype(BF16), (dims, ((), ())),
                               preferred_element_type=F32)
        if nk == 1:
            o_ref[...] = part.astype(out_dtype)
        else:
            @pl.when(kk == 0)
            def _():
                acc[...] = part

            @pl.when(kk > 0)
            def _():
                acc[...] += part

            @pl.when(kk == nk - 1)
            def _():
                o_ref[...] = acc[...].astype(out_dtype)
    return body


def mm_nn(a, w, out_dtype, name):
    M, K = a.shape
    S, _, Ns = w.shape
    tm = _pick(M, (1088, 1024, 512, 256, 128))
    tn = _pick(Ns, (1024, 896, 1408, 512, 256, 128))
    tk = _pick(K, (2048, 1408, 1024, 512, 256, 128))
    nps, nk = Ns // tn, K // tk
    return pl.pallas_call(
        _mm_body(((1,), (0,)), nk, out_dtype), name=name, grid=(S * nps, M // tm, nk),
        out_shape=jax.ShapeDtypeStruct((M, S * Ns), out_dtype),
        in_specs=[pl.BlockSpec((tm, tk), lambda j, i, k: (i, k)),
                  pl.BlockSpec((None, tk, tn), lambda j, i, k: (j // nps, k, j % nps))],
        out_specs=pl.BlockSpec((tm, tn), lambda j, i, k: (i, j)),
        scratch_shapes=[pltpu.VMEM((tm, tn), F32)], compiler_params=_cp("parallel", "parallel", "arbitrary"),
    )(a, w)


def mm_nt(dy, w, out_dtype, name):
    M, N = dy.shape
    S, K, Ns = w.shape
    tm = _pick(M, (1088, 1024, 512, 256, 128))
    tn = _pick(K, (1024, 512, 256, 128))
    tk = _pick(Ns, (1024, 896, 1408, 512, 256, 128))
    kps, nk = Ns // tk, N // tk
    return pl.pallas_call(
        _mm_body(((1,), (1,)), nk, out_dtype), name=name, grid=(K // tn, M // tm, nk),
        out_shape=jax.ShapeDtypeStruct((M, K), out_dtype),
        in_specs=[pl.BlockSpec((tm, tk), lambda j, i, k: (i, k)),
                  pl.BlockSpec((None, tn, tk), lambda j, i, k: (k // kps, j, k % kps))],
        out_specs=pl.BlockSpec((tm, tn), lambda j, i, k: (i, j)),
        scratch_shapes=[pltpu.VMEM((tm, tn), F32)], compiler_params=_cp("parallel", "parallel", "arbitrary"),
    )(dy, w)


def mm_tn(a, dy, S, out_dtype, name):
    M, K = a.shape
    N = dy.shape[1]
    Ns = N // S
    to = _pick(K, (1024, 512, 256, 128))
    tn = _pick(Ns, (1024, 896, 1408, 512, 256, 128))
    tk = _pick(M, (1088, 1024, 512, 256, 128))
    nps, nk = Ns // tn, M // tk
    return pl.pallas_call(
        _mm_body(((0,), (0,)), nk, out_dtype), name=name, grid=(K // to, S * nps, nk),
        out_shape=jax.ShapeDtypeStruct((S, K, Ns), out_dtype),
        in_specs=[pl.BlockSpec((tk, to), lambda i, j, k: (k, i)),
                  pl.BlockSpec((tk, tn), lambda i, j, k: (k, j))],
        out_specs=pl.BlockSpec((None, to, tn), lambda i, j, k: (j // nps, i, j % nps)),
        scratch_shapes=[pltpu.VMEM((to, tn), F32)], compiler_params=_cp("parallel", "parallel", "arbitrary"),
    )(a, dy)


_DIMS = {"nn": ((1,), (0,)), "nt": ((1,), (1,)), "tn": ((0,), (0,))}


def _dot(a, b, mode):
    return lax.dot_general(a.astype(BF16), b.astype(BF16), (_DIMS[mode], ((), ())), preferred_element_type=F32)


@functools.partial(jax.custom_vjp, nondiff_argnums=(2,))
def mmf(a, b, mode):
    return _dot(a, b, mode)


def _mmf_fwd(a, b, mode):
    return _dot(a, b, mode), (a, b)


def _mmf_bwd(mode, res, ct):
    a, b = res
    if mode == "nn":
        return _dot(ct, b, "nt"), _dot(a, ct, "tn")
    if mode == "nt":
        return _dot(ct, b, "nn"), _dot(ct, a, "tn")
    return _dot(b, ct, "nt"), _dot(a, ct, "nn")


mmf.defvjp(_mmf_fwd, _mmf_bwd)


def _dot_hi(m, g):
    return jnp.dot(m, g, precision=lax.Precision.HIGHEST, preferred_element_type=F32)


@jax.custom_vjp
def cumdot(m, mt, g):
    return _dot_hi(m, g)


def _cumdot_fwd(m, mt, g):
    return _dot_hi(m, g), (m, mt)


def _cumdot_bwd(res, ct):
    m, mt = res
    return jnp.zeros_like(m), jnp.zeros_like(mt), _dot_hi(mt, ct)


cumdot.defvjp(_cumdot_fwd, _cumdot_bwd)


def _rms(x, w):
    return x * lax.rsqrt(jnp.mean(x * x, axis=-1, keepdims=True) + EPS) * w


def _silu(x):
    return x * jax.nn.sigmoid(x)


RT = 16


def _gn_math(has_gate, x, m, gate, lnw, shift, scale):
    xn = x + gate * m if has_gate else x
    h = _rms(xn, lnw) * (1.0 + scale) + shift
    return xn, h


def _seg_spec(width, ncb):
    return pl.BlockSpec((None, RT, width), lambda i: (jnp.minimum(i // ncb, 1), 0, 0))


def gate_norm(x, m, gate, lnw, shift, scale, nc, R, name):
    T, D = x.shape
    has_gate = m is not None
    ncb = nc // R

    def body(*refs):
        if has_gate:
            x_ref, m_ref, g_ref, w_ref, sh_ref, sc_ref, xn_ref, h_ref = refs
        else:
            x_ref, w_ref, sh_ref, sc_ref, h_ref = refs

        def step(t, carry):
            rows = pl.ds(pl.multiple_of(t * RT, RT), RT)
            xn, h = _gn_math(has_gate, x_ref[rows, :], m_ref[rows, :] if has_gate else None,
                             g_ref[...] if has_gate else None, w_ref[...], sh_ref[...], sc_ref[...])
            if has_gate:
                xn_ref[rows, :] = xn
            h_ref[rows, :] = h.astype(BF16)
            return carry

        lax.fori_loop(0, R // RT, step, 0)

    row = pl.BlockSpec((R, D), lambda i: (i, 0))
    seg = _seg_spec(D, ncb)
    shared = pl.BlockSpec((None, RT, D), lambda i: (0, 0, 0))
    if has_gate:
        ins, in_specs = (x, m, gate, lnw, shift, scale), [row, row, seg, shared, seg, seg]
        out_shape = (jax.ShapeDtypeStruct((T, D), F32), jax.ShapeDtypeStruct((T, D), BF16))
        out_specs = (row, row)
    else:
        ins, in_specs = (x, lnw, shift, scale), [row, shared, seg, seg]
        out_shape, out_specs = jax.ShapeDtypeStruct((T, D), BF16), row
    out = pl.pallas_call(body, name=name, grid=(T // R,), out_shape=out_shape, in_specs=in_specs,
                         out_specs=out_specs, compiler_params=_cp("parallel"))(*ins)
    return out if has_gate else (None, out)


def gate_norm_bwd(x, m, gate, lnw, shift, scale, dxn, dh, nc, R, name):
    T, D = x.shape
    has_gate = m is not None
    ncb = nc // R

    def body(*refs):
        if has_gate:
            (x_ref, m_ref, g_ref, w_ref, sh_ref, sc_ref, dxn_ref, dh_ref,
             dx_ref, dm_ref, dg_ref, dw_ref, dsh_ref, dsc_ref) = refs
        else:
            x_ref, w_ref, sh_ref, sc_ref, dxn_ref, dh_ref, dx_ref, dw_ref, dsh_ref, dsc_ref = refs
        i = pl.program_id(0)

        @pl.when(i == 0)
        def _():
            dw_ref[...] = jnp.zeros_like(dw_ref)

        @pl.when((i == 0) | (i == ncb))
        def _():
            dsh_ref[...] = jnp.zeros_like(dsh_ref)
            dsc_ref[...] = jnp.zeros_like(dsc_ref)
            if has_gate:
                dg_ref[...] = jnp.zeros_like(dg_ref)

        def step(t, carry):
            rows = pl.ds(pl.multiple_of(t * RT, RT), RT)
            ct = (dxn_ref[rows, :], dh_ref[rows, :])
            if has_gate:
                _, vjp = jax.vjp(functools.partial(_gn_math, True), x_ref[rows, :], m_ref[rows, :], g_ref[...],
                                 w_ref[...], sh_ref[...], sc_ref[...])
                dx, dm, dg, dw, dsh, dsc = vjp(ct)
                dm_ref[rows, :] = dm.astype(BF16)
                dg_ref[...] += dg
            else:
                f = lambda x_, w_, sh_, sc_: _gn_math(False, x_, None, None, w_, sh_, sc_)[1]
                _, vjp = jax.vjp(f, x_ref[rows, :], w_ref[...], sh_ref[...], sc_ref[...])
                dx, dw, dsh, dsc = vjp(ct[1])
                dx = dx + ct[0]
            dx_ref[rows, :] = dx
            dw_ref[...] += dw
            dsh_ref[...] += dsh
            dsc_ref[...] += dsc
            return carry

        lax.fori_loop(0, R // RT, step, 0)

    row = pl.BlockSpec((R, D), lambda i: (i, 0))
    seg = _seg_spec(D, ncb)
    shared = pl.BlockSpec((None, RT, D), lambda i: (0, 0, 0))
    full, segs, one = jax.ShapeDtypeStruct((T, D), F32), jax.ShapeDtypeStruct((2, RT, D), F32), \
        jax.ShapeDtypeStruct((1, RT, D), F32)
    if has_gate:
        ins = (x, m, gate, lnw, shift, scale, dxn, dh)
        in_specs = [row, row, seg, shared, seg, seg, row, row]
        out_shape = (full, jax.ShapeDtypeStruct((T, D), BF16), segs, one, segs, segs)
        out_specs = (row, row, seg, shared, seg, seg)
    else:
        ins = (x, lnw, shift, scale, dxn, dh)
        in_specs = [row, shared, seg, seg, row, row]
        out_shape = (full, one, segs, segs)
        out_specs = (row, shared, seg, seg)
    out = pl.pallas_call(body, name=name, grid=(T // R,), out_shape=out_shape, in_specs=in_specs,
                         out_specs=out_specs, compiler_params=_cp("arbitrary"))(*ins)
    if has_gate:
        return out
    dx, dw, dsh, dsc = out
    return dx, None, None, dw, dsh, dsc


def gate_loss(x, m, gate, target, nc, R, name):
    T, D = x.shape
    ncb = nc // R

    def body(x_ref, m_ref, g_ref, t_ref, loss_ref, dx_ref, dm_ref, dg_ref):
        i = pl.program_id(0)

        @pl.when(i == 0)
        def _():
            loss_ref[...] = jnp.zeros_like(loss_ref)

        @pl.when((i == 0) | (i == ncb))
        def _():
            dg_ref[...] = jnp.zeros_like(dg_ref)

        live = jnp.where(i >= ncb, 1.0, 0.0).astype(F32)

        def step(t, carry):
            rows = pl.ds(pl.multiple_of(t * RT, RT), RT)
            mm_ = m_ref[rows, :]
            g = g_ref[...]
            e = (x_ref[rows, :] + g * mm_ - t_ref[rows, :]) * live
            dy = e * (1.0 / D)
            loss_ref[...] += 0.5 * e * dy
            dx_ref[rows, :] = dy
            dm_ref[rows, :] = (dy * g).astype(BF16)
            dg_ref[...] += dy * mm_
            return carry

        lax.fori_loop(0, R // RT, step, 0)

    row = pl.BlockSpec((R, D), lambda i: (i, 0))
    seg = _seg_spec(D, ncb)
    return pl.pallas_call(
        body, name=name, grid=(T // R,),
        out_shape=(jax.ShapeDtypeStruct((RT, D), F32), jax.ShapeDtypeStruct((T, D), F32),
                   jax.ShapeDtypeStruct((T, D), BF16), jax.ShapeDtypeStruct((2, RT, D), F32)),
        in_specs=[row, row, seg, pl.BlockSpec((R, D), lambda i: (jnp.maximum(i - ncb, 0), 0))],
        out_specs=(pl.BlockSpec((RT, D), lambda i: (0, 0)), row, row, seg),
        compiler_params=_cp("arbitrary"),
    )(x, m, gate, target)


def _hg_chunk(rev, lb, z, iv, hq, st):
    f = lb + (1.0 - lb) * jax.nn.sigmoid(z)
    g = jnp.log(jnp.maximum(f, F_FLOOR))
    k = (1.0 - lb) * jax.nn.sigmoid(-z)
    q = _silu(hq)
    ri = lax.broadcasted_iota(jnp.int32, (CHUNK, CHUNK), 0)
    ci = lax.broadcasted_iota(jnp.int32, (CHUNK, CHUNK), 1)
    r1 = lax.broadcasted_iota(jnp.int32, (CHUNK, 1), 0)
    seen = (ci >= ri) if rev else (ci <= ri)
    seen_t = (ci <= ri) if rev else (ci >= ri)
    cum = cumdot(seen.astype(F32), seen_t.astype(F32), g)
    tot = jnp.sum(g, axis=0, keepdims=True)
    att = jnp.zeros((CHUNK, CHUNK), F32)
    ref_rows = jnp.zeros_like(g)
    refs = []
    for b in range(CHUNK // SUB):
        before = (r1 >= SUB * (b + 1)) if rev else (r1 < SUB * b)
        r_b = jnp.sum(jnp.where(before, g, 0.0), axis=0, keepdims=True)
        in_b = (r1 >= SUB * b) & (r1 < SUB * (b + 1))
        ref_rows = ref_rows + jnp.where(in_b, r_b, 0.0)
        refs.append(r_b)
    qd = q * jnp.exp(cum - ref_rows)
    for b in range(CHUNK // SUB):
        kd = k * jnp.exp(jnp.minimum(refs[b] - cum, EXP_CLAMP))
        in_b = (ri >= SUB * b) & (ri < SUB * (b + 1))
        att = att + jnp.where(in_b, mmf(qd, kd, "nt"), 0.0)
    att = jnp.where(seen, att, 0.0)
    o = mmf(att, iv, "nn") + mmf(q * jnp.exp(cum), st, "nt")
    st_new = st * jnp.exp(tot) + mmf(iv, k * jnp.exp(tot - cum), "tn")
    return st_new, o


def _hg_cid(rev, i, ncs, n):
    if not rev:
        return i
    return jnp.where(i < ncs, ncs - 1 - i, ncs + n - 1 - i)


def hgrn_fwd(u, lb, rev, zcol, nc, hgw, name):
    T = u.shape[0]
    n, ncs, nh = T // CHUNK, nc // CHUNK, hgw // LANE

    def body(z_ref, v_ref, q_ref, lb_ref, o_ref, s_ref, st):
        i = pl.program_id(0)

        @pl.when(i == 0)
        def _():
            st[...] = jnp.zeros_like(st)

        for h in range(nh):
            cols = slice(h * LANE, (h + 1) * LANE)
            s_ref[h] = st[h]
            s_new, o = _hg_chunk(rev, lb_ref[:, cols], z_ref[:, cols], v_ref[:, cols], q_ref[:, cols], st[h])
            st[h] = s_new
            o_ref[:, cols] = o

    def col(cb):
        return pl.BlockSpec((CHUNK, hgw), lambda i: (_hg_cid(rev, i, ncs, n), cb))

    return pl.pallas_call(
        body, name=name, grid=(n,),
        out_shape=(jax.ShapeDtypeStruct((T, hgw), F32), jax.ShapeDtypeStruct((n, nh, LANE, LANE), F32)),
        in_specs=[col(zcol), col(2), col(7), pl.BlockSpec((1, hgw), lambda i: (0, 0))],
        out_specs=(pl.BlockSpec((CHUNK, hgw), lambda i: (_hg_cid(rev, i, ncs, n), 0)),
                   pl.BlockSpec((None, nh, LANE, LANE), lambda i: (i, 0, 0, 0))),
        scratch_shapes=[pltpu.VMEM((nh, LANE, LANE), F32)], compiler_params=_cp("arbitrary"),
    )(u, u, u, lb)


def hgrn_bwd(u, lb, states, do, rev, zcol, nc, hgw, name):
    T = u.shape[0]
    n, ncs, nh = T // CHUNK, nc // CHUNK, hgw // LANE

    def body(z_ref, v_ref, q_ref, lb_ref, s_ref, do_ref, dz_ref, dv_ref, dq_ref, dlb_ref, dst):
        j = pl.program_id(0)

        @pl.when(j == 0)
        def _():
            dst[...] = jnp.zeros_like(dst)
            dlb_ref[...] = jnp.zeros_like(dlb_ref)

        for h in range(nh):
            cols = slice(h * LANE, (h + 1) * LANE)
            _, vjp = jax.vjp(functools.partial(_hg_chunk, rev), lb_ref[:, cols], z_ref[:, cols], v_ref[:, cols],
                             q_ref[:, cols], s_ref[h])
            dlb, dz, dv, dq, ds = vjp((dst[h], do_ref[:, cols]))
            dst[h] = ds
            dz_ref[:, cols] = dz
            dv_ref[:, cols] = dv
            dq_ref[:, cols] = dq
            dlb_ref[:, cols] += dlb

    def cid(j):
        return _hg_cid(rev, n - 1 - j, ncs, n)

    def col(cb):
        return pl.BlockSpec((CHUNK, hgw), lambda j: (cid(j), cb))

    out = pl.BlockSpec((CHUNK, hgw), lambda j: (cid(j), 0))
    full = jax.ShapeDtypeStruct((T, hgw), F32)
    return pl.pallas_call(
        body, name=name, grid=(n,),
        out_shape=(full, full, full, jax.ShapeDtypeStruct((1, hgw), F32)),
        in_specs=[col(zcol), col(2), col(7), pl.BlockSpec((1, hgw), lambda j: (0, 0)),
                  pl.BlockSpec((None, nh, LANE, LANE), lambda j: (n - 1 - j, 0, 0, 0)), out],
        out_specs=(out, out, out, pl.BlockSpec((1, hgw), lambda j: (0, 0))),
        scratch_shapes=[pltpu.VMEM((nh, LANE, LANE), F32)], compiler_params=_cp("arbitrary"),
    )(u, u, u, lb, states, do)


HT = 128


def _read_math(ofw, obw, g, w):
    return _rms(ofw + obw, w) * _silu(g)


def hg_read(ofw, obw, u, w, gcol, R, name):
    T, hgw = ofw.shape
    nh = hgw // LANE

    def body(a_ref, b_ref, g_ref, w_ref, o_ref):
        for t in range(R // HT):
            rows = slice(t * HT, (t + 1) * HT)
            o_ref[rows, :] = _read_math(a_ref[rows, :], b_ref[rows, :], g_ref[rows, :], w_ref[...]).astype(BF16)

    blk = pl.BlockSpec((R, LANE), lambda i, h: (i, h))
    return pl.pallas_call(
        body, name=name, grid=(T // R, nh), out_shape=jax.ShapeDtypeStruct((T, hgw), BF16),
        in_specs=[blk, blk, pl.BlockSpec((R, LANE), lambda i, h: (i, gcol + h)),
                  pl.BlockSpec((1, LANE), lambda i, h: (0, 0))],
        out_specs=blk, compiler_params=_cp("parallel", "parallel"),
    )(ofw, obw, u, w)


def hg_read_bwd(ofw, obw, u, w, dout, gcol, ocol, R, name):
    T, hgw = ofw.shape
    nh = hgw // LANE

    def body(a_ref, b_ref, g_ref, w_ref, d_ref, do_ref, dg_ref, dw_ref):
        @pl.when(pl.program_id(1) == 0)
        def _():
            dw_ref[...] = jnp.zeros_like(dw_ref)

        for t in range(R // HT):
            rows = slice(t * HT, (t + 1) * HT)
            _, vjp = jax.vjp(_read_math, a_ref[rows, :], b_ref[rows, :], g_ref[rows, :], w_ref[...])
            da, _, dg, dw = vjp(d_ref[rows, :])
            do_ref[rows, :] = da
            dg_ref[rows, :] = dg
            dw_ref[...] += dw

    blk = pl.BlockSpec((R, LANE), lambda h, i: (i, h))
    full = jax.ShapeDtypeStruct((T, hgw), F32)
    return pl.pallas_call(
        body, name=name, grid=(nh, T // R), out_shape=(full, full, jax.ShapeDtypeStruct((nh, 1, LANE), F32)),
        in_specs=[blk, blk, pl.BlockSpec((R, LANE), lambda h, i: (i, gcol + h)),
                  pl.BlockSpec((1, LANE), lambda h, i: (0, 0)), pl.BlockSpec((R, LANE), lambda h, i: (i, ocol + h))],
        out_specs=(blk, blk, pl.BlockSpec((None, 1, LANE), lambda h, i: (h, 0, 0))),
        compiler_params=_cp("parallel", "arbitrary"),
    )(ofw, obw, u, w, dout)


def _na_step(qw, kw, ow, bias, qraw, kl, vl, kc, vc):
    q = _rms(qraw, qw)
    s_loc = mmf(q, _rms(kl, kw), "nt") * ATTN_SCALE + bias
    s_ctx = mmf(q, _rms(kc, kw), "nt") * ATTN_SCALE
    m = lax.stop_gradient(jnp.maximum(jnp.max(s_loc, axis=-1, keepdims=True), jnp.max(s_ctx, axis=-1, keepdims=True)))
    p_loc = jnp.exp(s_loc - m)
    p_ctx = jnp.exp(s_ctx - m)
    inv = 1.0 / (jnp.sum(p_loc, axis=-1, keepdims=True) + jnp.sum(p_ctx, axis=-1, keepdims=True))
    return _rms(mmf(p_loc * inv, vl, "nn") + mmf(p_ctx * inv, vc, "nn"), ow)


def _na_geometry(nc, rows):
    ncs = nc // GRID_W
    win_r = min(WIN_R, rows)
    nloc = win_r * GRID_W

    def row_start(s):
        r = jnp.maximum(s - ncs, 0)
        return jnp.clip(r - win_r // 2, 0, rows - win_r)

    def bias_idx(s):
        r = s - ncs
        return jnp.where(s < ncs, win_r, r - jnp.clip(r - win_r // 2, 0, rows - win_r))

    return ncs, win_r, nloc, row_start, bias_idx


def na_bias_tables(rpb, rows):
    win_r = min(WIN_R, rows)
    nh = rpb.shape[0]
    sel_r = np.zeros((win_r, win_r, 2 * WIN_R - 1), np.float32)
    for off in range(win_r):
        for jr in range(win_r):
            sel_r[off, jr, jr - off + WIN_R - 1] = 1.0
    qc = np.arange(GRID_W)[:, None]
    kc = np.arange(GRID_W)[None, :]
    wstart = np.clip(qc - WIN_C // 2, 0, GRID_W - WIN_C)
    ok = (kc >= wstart) & (kc < wstart + WIN_C)
    sel_c = np.zeros((GRID_W, GRID_W, 2 * WIN_C - 1), np.float32)
    sel_c[np.broadcast_to(qc, ok.shape)[ok], np.broadcast_to(kc, ok.shape)[ok], (kc - qc + WIN_C - 1)[ok]] = 1.0
    hi = lax.Precision.HIGHEST
    t = jnp.einsum("hab,oja->hojb", rpb, sel_r, precision=hi)
    t = jnp.einsum("hojb,qkb->hoqjk", t, sel_c, precision=hi)
    t = jnp.where(ok[None, None, :, None, :], t, NEG_INF)
    t = jnp.concatenate([t, jnp.full((nh, 1, GRID_W, win_r, GRID_W), NEG_INF, F32)], axis=1)
    return t.reshape(nh, win_r + 1, GRID_W, win_r * GRID_W)


def na_fwd(u, qw, kw, ow, bias, kcol, vcol, qcol, nc, naw, name):
    T = u.shape[0]
    nh, rows = naw // LANE, (T - nc) // GRID_W
    ncs, win_r, nloc, row_start, bias_idx = _na_geometry(nc, rows)

    def body(q_ref, k_ref, v_ref, qw_ref, kw_ref, ow_ref, b_ref, o_ref):
        s = pl.program_id(1)
        loc = pl.ds(pl.multiple_of(nc + row_start(s) * GRID_W, GRID_W), nloc)
        o_ref[...] = _na_step(qw_ref[...], kw_ref[...], ow_ref[...], b_ref[...], q_ref[...], k_ref[loc, :],
                              v_ref[loc, :], k_ref[0:nc, :], v_ref[0:nc, :]).astype(BF16)

    vec = pl.BlockSpec((1, LANE), lambda h, s: (0, 0))
    return pl.pallas_call(
        body, name=name, grid=(nh, T // GRID_W), out_shape=jax.ShapeDtypeStruct((T, naw), BF16),
        in_specs=[pl.BlockSpec((GRID_W, LANE), lambda h, s: (s, qcol + h)),
                  pl.BlockSpec((T, LANE), lambda h, s: (0, kcol + h)),
                  pl.BlockSpec((T, LANE), lambda h, s: (0, vcol + h)),
                  vec, vec, pl.BlockSpec((1, LANE), lambda h, s: (0, h)),
                  pl.BlockSpec((None, None, GRID_W, nloc), lambda h, s: (h, bias_idx(s), 0, 0))],
        out_specs=pl.BlockSpec((GRID_W, LANE), lambda h, s: (s, h)),
        compiler_params=_cp("parallel", "arbitrary"),
    )(u, u, u, qw, kw, ow, bias)


def na_bwd(u, qw, kw, ow, bias, dout, kcol, vcol, qcol, ocol, nc, naw, name):
    T = u.shape[0]
    nh, rows = naw // LANE, (T - nc) // GRID_W
    ncs, win_r, nloc, row_start, bias_idx = _na_geometry(nc, rows)
    fresh = [0] + [ncs + r for r in range(rows) if r == 0 or r - np.clip(r - win_r // 2, 0, rows - win_r)
                   != (r - 1) - np.clip(r - 1 - win_r // 2, 0, rows - win_r)]

    def body(q_ref, k_ref, v_ref, qw_ref, kw_ref, ow_ref, b_ref, d_ref,
             dq_ref, dk_ref, dv_ref, db_ref, dqw_ref, dkw_ref, dow_ref):
        s = pl.program_id(1)

        @pl.when(s == 0)
        def _():
            dk_ref[...] = jnp.zeros_like(dk_ref)
            dv_ref[...] = jnp.zeros_like(dv_ref)
            dqw_ref[...] = jnp.zeros_like(dqw_ref)
            dkw_ref[...] = jnp.zeros_like(dkw_ref)
            dow_ref[...] = jnp.zeros_like(dow_ref)

        first = functools.reduce(lambda a, b: a | b, [s == f for f in fresh])

        @pl.when(first)
        def _():
            db_ref[...] = jnp.zeros_like(db_ref)

        loc = pl.ds(pl.multiple_of(nc + row_start(s) * GRID_W, GRID_W), nloc)
        _, vjp = jax.vjp(_na_step, qw_ref[...], kw_ref[...], ow_ref[...], b_ref[...], q_ref[...], k_ref[loc, :],
                         v_ref[loc, :], k_ref[0:nc, :], v_ref[0:nc, :])
        dqw, dkw, dow, db, dq, dkl, dvl, dkc, dvc = vjp(d_ref[...])
        dq_ref[...] = dq
        dk_ref[loc, :] += dkl
        dv_ref[loc, :] += dvl
        dk_ref[0:nc, :] += dkc
        dv_ref[0:nc, :] += dvc
        db_ref[...] += db
        dqw_ref[...] += dqw
        dkw_ref[...] += dkw
        dow_ref[...] += dow

    vec = pl.BlockSpec((1, LANE), lambda h, s: (0, 0))
    slab = pl.BlockSpec((T, LANE), lambda h, s: (0, h))
    hvec = pl.BlockSpec((None, 1, LANE), lambda h, s: (h, 0, 0))
    full = jax.ShapeDtypeStruct((T, naw), F32)
    hv = jax.ShapeDtypeStruct((nh, 1, LANE), F32)
    bspec = pl.BlockSpec((None, None, GRID_W, nloc), lambda h, s: (h, bias_idx(s), 0, 0))
    return pl.pallas_call(
        body, name=name, grid=(nh, T // GRID_W),
        out_shape=(full, full, full, jax.ShapeDtypeStruct(bias.shape, F32), hv, hv, hv),
        in_specs=[pl.BlockSpec((GRID_W, LANE), lambda h, s: (s, qcol + h)),
                  pl.BlockSpec((T, LANE), lambda h, s: (0, kcol + h)),
                  pl.BlockSpec((T, LANE), lambda h, s: (0, vcol + h)),
                  vec, vec, pl.BlockSpec((1, LANE), lambda h, s: (0, h)), bspec,
                  pl.BlockSpec((GRID_W, LANE), lambda h, s: (s, ocol + h))],
        out_specs=(pl.BlockSpec((GRID_W, LANE), lambda h, s: (s, h)), slab, slab, bspec, hvec, hvec, hvec),
        compiler_params=_cp("parallel", "arbitrary"),
    )(u, u, u, qw, kw, ow, bias, dout)


def _halo_specs(R, width, T, col):
    hb = R // 8
    prev = pl.BlockSpec((8, width), lambda j, i: (jnp.maximum(i * hb - 1, 0), col(j, i)))
    nxt = pl.BlockSpec((8, width), lambda j, i: (jnp.minimum((i + 1) * hb, T // 8 - 1), col(j, i)))
    return prev, nxt


def _edge_flags(i, ncb, nblk):
    has_prev = jnp.where((i == 0) | (i == ncb), 0.0, 1.0).astype(F32)
    has_next = jnp.where((i == ncb - 1) | (i == nblk - 1), 0.0, 1.0).astype(F32)
    return has_prev, has_next


def _shift_up(a, prev_row):
    r0 = lax.broadcasted_iota(jnp.int32, a.shape, 0) == 0
    return jnp.where(r0, prev_row, pltpu.roll(a, 1, 0))


def _shift_dn(a, next_row):
    n = a.shape[0]
    rl = lax.broadcasted_iota(jnp.int32, a.shape, 0) == n - 1
    return jnp.where(rl, next_row, pltpu.roll(a, n - 1, 0))


def _conv3(a, prev_row, next_row, w_ref):
    return w_ref[0:1, :] * _shift_up(a, prev_row) + w_ref[1:2, :] * a + w_ref[2:3, :] * _shift_dn(a, next_row)


def _cv_post(b, y, w):
    return _rms(b * y, w)


def short_conv(u, cw, ow, bcol, nc, cvw, R, bwd_dout=None, ocol=0, name=""):
    T = u.shape[0]
    nh, nblk, ncb = cvw // LANE, T // R, nc // R
    bwd = bwd_dout is not None

    def body(b_ref, c_ref, v_ref, cp_ref, vp_ref, cn_ref, vn_ref, cw_ref, ow_ref, *rest):
        i = pl.program_id(1)
        has_prev, has_next = _edge_flags(i, ncb, nblk)
        p = c_ref[...] * v_ref[...]
        y = _conv3(p, cp_ref[7:8, :] * vp_ref[7:8, :] * has_prev, cn_ref[0:1, :] * vn_ref[0:1, :] * has_next, cw_ref)
        if not bwd:
            rest[0][...] = _cv_post(b_ref[...], y, ow_ref[...]).astype(BF16)
            return
        d_ref, db_ref, dy_ref, dow_ref = rest

        @pl.when(i == 0)
        def _():
            dow_ref[...] = jnp.zeros_like(dow_ref)

        _, vjp = jax.vjp(_cv_post, b_ref[...], y, ow_ref[...])
        db, dy, dow = vjp(d_ref[...])
        db_ref[...] = db
        dy_ref[...] = dy
        dow_ref[...] += dow

    def main(k):
        return pl.BlockSpec((R, LANE), lambda h, i: (i, bcol + k * nh + h))

    cprev, cnext = _halo_specs(R, LANE, T, lambda h, i: bcol + nh + h)
    vprev, vnext = _halo_specs(R, LANE, T, lambda h, i: bcol + 2 * nh + h)
    in_specs = [main(0), main(1), main(2), cprev, vprev, cnext, vnext,
                pl.BlockSpec((3, LANE), lambda h, i: (0, h)), pl.BlockSpec((1, LANE), lambda h, i: (0, h))]
    ins = [u] * 7 + [cw, ow]
    blk = pl.BlockSpec((R, LANE), lambda h, i: (i, h))
    if not bwd:
        out_shape, out_specs = jax.ShapeDtypeStruct((T, cvw), BF16), blk
    else:
        in_specs.append(pl.BlockSpec((R, LANE), lambda h, i: (i, ocol + h)))
        ins.append(bwd_dout)
        full = jax.ShapeDtypeStruct((T, cvw), F32)
        out_shape = (full, full, jax.ShapeDtypeStruct((nh, 1, LANE), F32))
        out_specs = (blk, blk, pl.BlockSpec((None, 1, LANE), lambda h, i: (h, 0, 0)))
    return pl.pallas_call(body, name=name, grid=(nh, nblk), out_shape=out_shape, in_specs=in_specs,
                          out_specs=out_specs, compiler_params=_cp("parallel", "arbitrary"))(*ins)


def conv3_bwd(dy, src, cw, nc, R, W, prod_cols=None, col0=0, out_dtype=F32, name=""):
    T, C = dy.shape
    nblk, ncb = T // R, nc // R
    prod = prod_cols is not None

    def body(*refs):
        if prod:
            (d_ref, dp_ref, dn_ref, c_ref, v_ref, cp_ref, vp_ref, cn_ref, vn_ref, w_ref,
             dc_ref, dv_ref, dw_ref) = refs
        else:
            d_ref, dp_ref, dn_ref, p_ref, pp_ref, pn_ref, w_ref, o_ref, dw_ref = refs
        i = pl.program_id(1)
        has_prev, has_next = _edge_flags(i, ncb, nblk)

        @pl.when(i == 0)
        def _():
            dw_ref[...] = jnp.zeros_like(dw_ref)

        d = d_ref[...]
        d_up = _shift_up(d, dp_ref[7:8, :] * has_prev)
        d_dn = _shift_dn(d, dn_ref[0:1, :] * has_next)
        dp = w_ref[0:1, :] * d_dn + w_ref[1:2, :] * d + w_ref[2:3, :] * d_up
        if prod:
            c, v = c_ref[...], v_ref[...]
            p = c * v
            p_prev, p_next = cp_ref[7:8, :] * vp_ref[7:8, :] * has_prev, cn_ref[0:1, :] * vn_ref[0:1, :] * has_next
            dc_ref[...] = dp * v
            dv_ref[...] = dp * c
        else:
            p = p_ref[...]
            p_prev, p_next = pp_ref[7:8, :] * has_prev, pn_ref[0:1, :] * has_next
            o_ref[...] = dp.astype(out_dtype)
        dw_ref[0:1, :] += jnp.sum(_shift_up(p, p_prev) * d, axis=0, keepdims=True)
        dw_ref[1:2, :] += jnp.sum(p * d, axis=0, keepdims=True)
        dw_ref[2:3, :] += jnp.sum(_shift_dn(p, p_next) * d, axis=0, keepdims=True)

    blk = pl.BlockSpec((R, W), lambda j, i: (i, j))
    dprev, dnext = _halo_specs(R, W, T, lambda j, i: j)
    wspec = pl.BlockSpec((3, W), lambda j, i: (0, j))
    dwspec = pl.BlockSpec((8, W), lambda j, i: (0, j))
    dwshape = jax.ShapeDtypeStruct((8, C), F32)
    if prod:
        ccol, vcol = prod_cols
        cprev, cnext = _halo_specs(R, W, T, lambda j, i: ccol + j)
        vprev, vnext = _halo_specs(R, W, T, lambda j, i: vcol + j)
        in_specs = [blk, dprev, dnext, pl.BlockSpec((R, W), lambda j, i: (i, ccol + j)),
                    pl.BlockSpec((R, W), lambda j, i: (i, vcol + j)), cprev, vprev, cnext, vnext, wspec]
        ins = [dy, dy, dy] + [src] * 6 + [cw]
        full = jax.ShapeDtypeStruct((T, C), F32)
        out_shape, out_specs = (full, full, dwshape), (blk, blk, dwspec)
    else:
        sprev, snext = _halo_specs(R, W, T, lambda j, i: col0 + j)
        in_specs = [blk, dprev, dnext, pl.BlockSpec((R, W), lambda j, i: (i, col0 + j)), sprev, snext,
                    pl.BlockSpec((3, W), lambda j, i: (0, col0 + j))]
        ins = [dy, dy, dy, src, src, src, cw]
        out_shape, out_specs = (jax.ShapeDtypeStruct((T, C), out_dtype), dwshape), (blk, dwspec)
    return pl.pallas_call(body, name=name, grid=(C // W, nblk), out_shape=out_shape, in_specs=in_specs,
                          out_specs=out_specs, compiler_params=_cp("parallel", "arbitrary"))(*ins)


def ffn_mid(uf, cw, cb, nc, R, W, da=None, name=""):
    T, C = uf.shape
    F = C // 2
    nblk, ncb, nj = T // R, nc // R, F // W
    bwd = da is not None

    def body(g_ref, v_ref, gp_ref, vp_ref, gn_ref, vn_ref, wg_ref, wv_ref, bg_ref, bv_ref, *rest):
        i = pl.program_id(1)
        has_prev, has_next = _edge_flags(i, ncb, nblk)
        yg = _conv3(g_ref[...], gp_ref[7:8, :] * has_prev, gn_ref[0:1, :] * has_next, wg_ref) + bg_ref[...]
        yv = _conv3(v_ref[...], vp_ref[7:8, :] * has_prev, vn_ref[0:1, :] * has_next, wv_ref) + bv_ref[...]
        sg = jax.nn.sigmoid(yg)
        if not bwd:
            rest[0][...] = (yg * sg * yv).astype(BF16)
            return
        da_ref, dyg_ref, dyv_ref, dbg_ref, dbv_ref = rest

        @pl.when(i == 0)
        def _():
            dbg_ref[...] = jnp.zeros_like(dbg_ref)
            dbv_ref[...] = jnp.zeros_like(dbv_ref)

        d = da_ref[...]
        dyg = d * yv * (sg * (1.0 + yg * (1.0 - sg)))
        dyv = d * (yg * sg)
        dyg_ref[...] = dyg
        dyv_ref[...] = dyv
        dbg_ref[...] += jnp.sum(dyg, axis=0, keepdims=True)
        dbv_ref[...] += jnp.sum(dyv, axis=0, keepdims=True)

    gblk = pl.BlockSpec((R, W), lambda j, i: (i, j))
    vblk = pl.BlockSpec((R, W), lambda j, i: (i, nj + j))
    gprev, gnext = _halo_specs(R, W, T, lambda j, i: j)
    vprev, vnext = _halo_specs(R, W, T, lambda j, i: nj + j)
    in_specs = [gblk, vblk, gprev, vprev, gnext, vnext,
                pl.BlockSpec((3, W), lambda j, i: (0, j)), pl.BlockSpec((3, W), lambda j, i: (0, nj + j)),
                pl.BlockSpec((1, W), lambda j, i: (0, j)), pl.BlockSpec((1, W), lambda j, i: (0, nj + j))]
    ins = [uf] * 6 + [cw, cw, cb, cb]
    if not bwd:
        out_shape, out_specs = jax.ShapeDtypeStruct((T, F), BF16), gblk
    else:
        in_specs.append(gblk)
        ins.append(da)
        half, bias = jax.ShapeDtypeStruct((T, F), F32), jax.ShapeDtypeStruct((1, F), F32)
        bspec = pl.BlockSpec((1, W), lambda j, i: (0, j))
        out_shape, out_specs = (half, half, bias, bias), (gblk, gblk, bspec, bspec)
    return pl.pallas_call(body, name=name, grid=(nj, nblk), out_shape=out_shape, in_specs=in_specs,
                          out_specs=out_specs, compiler_params=_cp("parallel", "arbitrary"))(*ins)


WEIGHTS = ("c_ctx", "w_ada", "b_ada", "ln1_w", "ln2_w", "w_in", "hg_lb_logits", "hg_norm_w", "na_q_norm_w",
           "na_k_norm_w", "na_rpb", "na_out_norm_w", "cv_w", "cv_out_norm_w", "w_out", "w_up", "ffn_conv_w",
           "ffn_conv_b", "w_down")
BIG = ("w_ada", "w_in", "w_out", "w_up", "w_down")
SHARDED_SMALL = ("hg_lb_logits", "cv_w", "ffn_conv_w")


def _flat_rows(parts, dtype):
    flat, layout, off = [], [], 0
    for p in parts:
        layout.append((off, p.shape))
        flat.append(p.reshape(-1).astype(dtype))
        off += p.size
    pad = (-off) % (8 * LANE)
    if pad:
        flat.append(jnp.zeros((pad,), dtype))
    return jnp.concatenate(flat).reshape(-1, LANE), layout


def _unflat(buf, layout):
    v = buf.reshape(-1)
    return [v[off:off + int(np.prod(shape))].reshape(shape) for off, shape in layout]


def _lb_all(logits):
    sm = jax.nn.softmax(logits.astype(F32), axis=1)
    return jnp.cumsum(sm, axis=1) - sm[:, :1]


def _seg(ctx_vec, lat_vec):
    return jnp.broadcast_to(jnp.stack([ctx_vec, lat_vec])[:, None, :], (2, RT, ctx_vec.shape[0]))


def _shared(vec):
    return jnp.broadcast_to(vec[None, None, :], (1, RT, vec.shape[0]))


def kernel(x, c, ctx, c_ctx, w_ada, b_ada, ln1_w, ln2_w, w_in, hg_lb_logits, hg_norm_w, na_q_norm_w, na_k_norm_w, na_rpb, na_out_norm_w, cv_w, cv_out_norm_w, w_out, w_up, ffn_conv_w, ffn_conv_b, w_down, loss_target, m_c_ctx, m_w_ada, m_b_ada, m_ln1_w, m_ln2_w, m_w_in, m_hg_lb_logits, m_hg_norm_w, m_na_q_norm_w, m_na_k_norm_w, m_na_rpb, m_na_out_norm_w, m_cv_w, m_cv_out_norm_w, m_w_out, m_w_up, m_ffn_conv_w, m_ffn_conv_b, m_w_down, v_c_ctx, v_w_ada, v_b_ada, v_ln1_w, v_ln2_w, v_w_in, v_hg_lb_logits, v_hg_norm_w, v_na_q_norm_w, v_na_k_norm_w, v_na_rpb, v_na_out_norm_w, v_cv_w, v_cv_out_norm_w, v_w_out, v_w_up, v_ffn_conv_w, v_ffn_conv_b, v_w_down):
    W = dict(c_ctx=c_ctx, w_ada=w_ada, b_ada=b_ada, ln1_w=ln1_w, ln2_w=ln2_w, w_in=w_in, hg_lb_logits=hg_lb_logits,
             hg_norm_w=hg_norm_w, na_q_norm_w=na_q_norm_w, na_k_norm_w=na_k_norm_w, na_rpb=na_rpb,
             na_out_norm_w=na_out_norm_w, cv_w=cv_w, cv_out_norm_w=cv_out_norm_w, w_out=w_out, w_up=w_up,
             ffn_conv_w=ffn_conv_w, ffn_conv_b=ffn_conv_b, w_down=w_down)
    Mo = dict(c_ctx=m_c_ctx, w_ada=m_w_ada, b_ada=m_b_ada, ln1_w=m_ln1_w, ln2_w=m_ln2_w, w_in=m_w_in,
              hg_lb_logits=m_hg_lb_logits, hg_norm_w=m_hg_norm_w, na_q_norm_w=m_na_q_norm_w,
              na_k_norm_w=m_na_k_norm_w, na_rpb=m_na_rpb, na_out_norm_w=m_na_out_norm_w, cv_w=m_cv_w,
              cv_out_norm_w=m_cv_out_norm_w, w_out=m_w_out, w_up=m_w_up, ffn_conv_w=m_ffn_conv_w,
              ffn_conv_b=m_ffn_conv_b, w_down=m_w_down)
    Vo = dict(c_ctx=v_c_ctx, w_ada=v_w_ada, b_ada=v_b_ada, ln1_w=v_ln1_w, ln2_w=v_ln2_w, w_in=v_w_in,
              hg_lb_logits=v_hg_lb_logits, hg_norm_w=v_hg_norm_w, na_q_norm_w=v_na_q_norm_w,
              na_k_norm_w=v_na_k_norm_w, na_rpb=v_na_rpb, na_out_norm_w=v_na_out_norm_w, cv_w=v_cv_w,
              cv_out_norm_w=v_cv_out_norm_w, w_out=v_w_out, w_up=v_w_up, ffn_conv_w=v_ffn_conv_w,
              ffn_conv_b=v_ffn_conv_b, w_down=v_w_down)

    xi, yi, ci = _me()
    chip = 2 * xi + yi
    dev = 2 * chip + ci
    L, D = x.shape[1], x.shape[2]
    NC = ctx.shape[1]
    T = NC + L
    depth = w_in.shape[0]
    HGW, NAW, CVW = 4 * hg_lb_logits.shape[-1], na_out_norm_w.shape[-1], cv_out_norm_w.shape[-1]
    MIX = HGW + NAW + CVW
    INW, FF2 = 4 * w_in.shape[-1], 4 * w_up.shape[-1]
    F = FF2 // 2
    ADA = 4 * w_ada.shape[-1]
    assert NAW == 2 * HGW and INW == 5 * HGW + 3 * NAW + 3 * CVW and ADA == 6 * D and NC % 128 == 0
    assert L % GRID_W == 0 and T % CHUNK == 0 and depth == 2
    R = math.gcd(NC, 256)
    FW = 512 if F % 512 == 0 else LANE
    rows = L // GRID_W
    nh_hg, nh_na, nh_cv = HGW // LANE, NAW // LANE, CVW // LANE
    kcol = 3 * nh_hg
    vcol = kcol + nh_na
    gcol = vcol + nh_na + nh_hg
    qcol = gcol + nh_hg
    bcol = qcol + nh_na
    mix_na, mix_cv = nh_hg, nh_hg + nh_na

    small1, lay1 = _flat_rows([c[0], hg_lb_logits, cv_w, ffn_conv_w], F32)
    g1 = allgather8([small1], "gather_cond")[0]
    per_dev = [_unflat(g1[d], lay1) for d in range(8)]
    c_all = jnp.stack([p[0] for p in per_dev])
    lb_logits = jnp.concatenate([per_dev[2 * s][1] for s in range(4)], axis=-1)
    cvw_full = jnp.concatenate([per_dev[2 * s][2] for s in range(4)], axis=-1)
    fcw_full = jnp.concatenate([per_dev[2 * s][3] for s in range(4)], axis=-1)
    lb_all, lb_pull = jax.vjp(_lb_all, lb_logits)

    a16 = jnp.concatenate([c_all, c_ctx[None], jnp.zeros((7, D), F32)])
    s16 = _silu(a16)
    wcols = ADA // 4
    b_mine = lax.dynamic_slice_in_dim(b_ada, chip * wcols, wcols, axis=1)
    p_ada = jnp.stack([mm_nn(s16, w_ada[l][None], F32, f"ada_fwd_{l}") + b_mine[l][None] for l in range(depth)])
    g2 = allgather8([p_ada.reshape(depth * 16, wcols)], "gather_ada")[0].reshape(8, depth, 16, wcols)
    ada_rows = jnp.concatenate([g2[2 * s] for s in range(4)], axis=-1)
    ada = lax.dynamic_index_in_dim(ada_rows, dev, axis=1, keepdims=False)
    ada_c = ada_rows[:, 8]

    def half_rows(a):
        h = a.shape[0] // 2
        return lax.dynamic_slice_in_dim(a, ci * h, h, axis=0)

    proj = ("w_in", "w_out", "w_up", "w_down")
    wparts = [half_rows(W[n][l]).astype(BF16) for l in range(depth) for n in proj]
    wg = allgather8(wparts, "gather_weights", hbm=True)
    Wg = [{n: wg[l * 4 + j].reshape(4, -1, wg[l * 4 + j].shape[-1]) for j, n in enumerate(proj)} for l in range(depth)]
    for l in range(depth):
        Wg[l]["w_out"] = Wg[l]["w_out"].reshape(1, MIX, D)
        Wg[l]["w_down"] = Wg[l]["w_down"].reshape(1, F, D)

    xcat = jnp.concatenate([ctx[0], x[0]], axis=0)
    mods = []
    for l in range(depth):
        lat, con = jnp.split(ada[l], 6), jnp.split(ada_c[l], 6)
        mods.append(dict(sh1=_seg(con[0], lat[0]), sc1=_seg(con[1], lat[1]), g1=_seg(con[2], lat[2]),
                         sh2=_seg(con[3], lat[3]), sc2=_seg(con[4], lat[4]), g2=_seg(con[5], lat[5]),
                         ln1=_shared(ln1_w[l]), ln2=_shared(ln2_w[l])))
    bias_pull, saved = [], []
    x0 = xcat
    _, h = gate_norm(x0, None, None, mods[0]["ln1"], mods[0]["sh1"], mods[0]["sc1"], NC, R, "norm_in")
    for l in range(depth):
        md, wl = mods[l], Wg[l]
        u = mm_nn(h, wl["w_in"], F32, f"proj_in_{l}")
        lbf, lbb = lb_all[0, l][None], lb_all[1, l][None]
        o_fw, st_fw = hgrn_fwd(u, lbf, False, 0, NC, HGW, f"hgrn_fw_{l}")
        o_bw, st_bw = hgrn_fwd(u, lbb, True, 1, NC, HGW, f"hgrn_bw_{l}")
        hgn = hg_norm_w[l][None]
        hg = hg_read(o_fw, o_bw, u, hgn, gcol, R, f"hg_read_{l}")
        bias, pull = jax.vjp(lambda r: na_bias_tables(r, rows), na_rpb[l])
        bias_pull.append(pull)
        qn, kn, on = na_q_norm_w[l][None], na_k_norm_w[l][None], na_out_norm_w[l][None]
        na = na_fwd(u, qn, kn, on, bias, kcol, vcol, qcol, NC, NAW, f"na_fwd_{l}")
        cvw_l, cvo = cvw_full[l], cv_out_norm_w[l][None]
        cv = short_conv(u, cvw_l, cvo, bcol, NC, CVW, R, name=f"short_conv_{l}")
        mix = jnp.concatenate([hg, na, cv], axis=1)
        m1 = mm_nn(mix, wl["w_out"], F32, f"proj_out_{l}")
        x1, h2 = gate_norm(x0, m1, md["g1"], md["ln2"], md["sh2"], md["sc2"], NC, R, f"gate_norm_mid_{l}")
        uf = mm_nn(h2, wl["w_up"], F32, f"ffn_up_{l}")
        fcw_l, fcb_l = fcw_full[l], ffn_conv_b[l][None]
        a = ffn_mid(uf, fcw_l, fcb_l, NC, R, FW, name=f"ffn_mid_{l}")
        m2 = mm_nn(a, wl["w_down"], F32, f"ffn_down_{l}")
        saved.append(dict(x0=x0, h=h, u=u, o_fw=o_fw, o_bw=o_bw, st_fw=st_fw, st_bw=st_bw, bias=bias, mix=mix,
                          m1=m1, x1=x1, h2=h2, uf=uf, a=a, m2=m2, lbf=lbf, lbb=lbb))
        if l + 1 < depth:
            nx = mods[l + 1]
            x0, h = gate_norm(x1, m2, md["g2"], nx["ln1"], nx["sh1"], nx["sc1"], NC, R, f"gate_norm_end_{l}")
    sv, md = saved[-1], mods[-1]
    loss_terms, d_x1, d_m2, d_g2 = gate_loss(sv["x1"], sv["m2"], md["g2"], loss_target[0], NC, R, "gate_loss")
    loss = lax.psum(jnp.sum(loss_terms), ("x", "y", "c"))

    big_grads = [dict() for _ in range(depth)]
    small = [dict() for _ in range(depth)]
    d_ada = [None] * depth
    d_lb = [None] * depth
    for l in reversed(range(depth)):
        sv, md, wl = saved[l], mods[l], Wg[l]
        u, uf = sv["u"], sv["uf"]
        big_grads[l]["w_down"] = mm_tn(sv["a"], d_m2, 1, BF16, f"grad_w_down_{l}").reshape(4, F // 4, D)
        d_a = mm_nt(d_m2, wl["w_down"], F32, f"ffn_down_bwd_{l}")
        fcw_l, fcb_l = fcw_full[l], ffn_conv_b[l][None]
        dyg, dyv, dbg, dbv = ffn_mid(uf, fcw_l, fcb_l, NC, R, FW, da=d_a, name=f"ffn_mid_bwd_{l}")
        dug, dwg = conv3_bwd(dyg, uf, fcw_l, NC, R, FW, col0=0, out_dtype=BF16, name=f"ffn_conv_bwd_gate_{l}")
        duv, dwv = conv3_bwd(dyv, uf, fcw_l, NC, R, FW, col0=F // FW, out_dtype=BF16, name=f"ffn_conv_bwd_val_{l}")
        d_uf = jnp.concatenate([dug, duv], axis=1)
        small[l]["ffn_conv_w"] = jnp.concatenate([dwg[:3], dwv[:3]], axis=1)
        small[l]["ffn_conv_b"] = jnp.concatenate([dbg[0], dbv[0]])
        big_grads[l]["w_up"] = mm_tn(sv["h2"], d_uf, 4, BF16, f"grad_w_up_{l}")
        d_h2 = mm_nt(d_uf, wl["w_up"], F32, f"ffn_up_bwd_{l}")
        d_x0, d_m1, dg1, dln2, dsh2, dsc2 = gate_norm_bwd(sv["x0"], sv["m1"], md["g1"], md["ln2"], md["sh2"], md["sc2"],
                                                          d_x1, d_h2, NC, R, f"gate_norm_mid_bwd_{l}")
        big_grads[l]["w_out"] = mm_tn(sv["mix"], d_m1, 1, BF16, f"grad_w_out_{l}").reshape(4, MIX // 4, D)
        d_mix = mm_nt(d_m1, wl["w_out"], F32, f"proj_out_bwd_{l}")
        hgn = hg_norm_w[l][None]
        d_o, d_hgg, d_hgn = hg_read_bwd(sv["o_fw"], sv["o_bw"], u, hgn, d_mix, gcol, 0, R, f"hg_read_bwd_{l}")
        dzf, dvf, dqf, dlbf = hgrn_bwd(u, sv["lbf"], sv["st_fw"], d_o, False, 0, NC, HGW, f"hgrn_fw_bwd_{l}")
        dzb, dvb, dqb, dlbb = hgrn_bwd(u, sv["lbb"], sv["st_bw"], d_o, True, 1, NC, HGW, f"hgrn_bw_bwd_{l}")
        d_lb[l] = (dlbf[0], dlbb[0])
        qn, kn, on = na_q_norm_w[l][None], na_k_norm_w[l][None], na_out_norm_w[l][None]
        d_nq, d_nk, d_nv, d_bias, d_qn, d_kn, d_on = na_bwd(u, qn, kn, on, sv["bias"], d_mix, kcol, vcol, qcol, mix_na,
                                                           NC, NAW, f"na_bwd_{l}")
        cvw_l, cvo = cvw_full[l], cv_out_norm_w[l][None]
        d_cb, d_cy, d_cvo = short_conv(u, cvw_l, cvo, bcol, NC, CVW, R, bwd_dout=d_mix, ocol=mix_cv,
                                       name=f"short_conv_bwd_{l}")
        d_cc, d_cvv, d_cvw = conv3_bwd(d_cy, u, cvw_l, NC, R, LANE, prod_cols=(bcol + nh_cv, bcol + 2 * nh_cv),
                                       name=f"short_conv_taps_bwd_{l}")
        d_u = jnp.concatenate([dzf, dzb, dvf + dvb, d_nk, d_nv, dqf + dqb, d_hgg, d_nq, d_cb, d_cc, d_cvv],
                              axis=1).astype(BF16)
        big_grads[l]["w_in"] = mm_tn(sv["h"], d_u, 4, BF16, f"grad_w_in_{l}")
        d_h = mm_nt(d_u, wl["w_in"], F32, f"proj_in_bwd_{l}")
        small[l].update(hg_norm_w=d_hgn.sum(0)[0], na_q_norm_w=d_qn.sum(0)[0], na_k_norm_w=d_kn.sum(0)[0],
                        na_out_norm_w=d_on.reshape(-1), na_rpb=bias_pull[l](d_bias)[0], cv_w=d_cvw[:3],
                        cv_out_norm_w=d_cvo.reshape(-1), ln2_w=dln2.sum((0, 1)))
        if l > 0:
            pv, pm = saved[l - 1], mods[l - 1]
            d_x1, d_m2, dg2_prev, dln1, dsh1, dsc1 = gate_norm_bwd(pv["x1"], pv["m2"], pm["g2"], md["ln1"], md["sh1"],
                                                                   md["sc1"], d_x0, d_h, NC, R, f"gate_norm_end_bwd_{l - 1}")
        else:
            d_xin, _, _, dln1, dsh1, dsc1 = gate_norm_bwd(sv["x0"], None, None, md["ln1"], md["sh1"], md["sc1"], d_x0, d_h,
                                                          NC, R, "norm_in_bwd")
        small[l]["ln1_w"] = dln1.sum((0, 1))
        this_g2 = d_g2
        vecs = [v.sum(1) for v in (dsh1, dsc1, dg1, dsh2, dsc2, this_g2)]
        d_ada[l] = jnp.stack([jnp.concatenate([v[s] for v in vecs]) for s in (0, 1)])
        if l > 0:
            d_g2 = dg2_prev
    grad_x = d_xin[NC:][None]
    d_logits = lb_pull(jnp.stack([jnp.stack([d_lb[l][k] for l in range(depth)]) for k in (0, 1)]))[0]

    rep_names = ("ln1_w", "ln2_w", "hg_norm_w", "na_q_norm_w", "na_k_norm_w", "na_rpb", "na_out_norm_w",
                 "cv_out_norm_w", "ffn_conv_b", "cv_w", "ffn_conv_w")
    parts3 = [jnp.stack([small[l][n] for l in range(depth)]) for n in rep_names]
    parts3 += [d_logits, jnp.stack([d_ada[l][0] for l in range(depth)]), jnp.stack([d_ada[l][1] for l in range(depth)])]
    buf3, lay3 = _flat_rows(parts3, F32)
    g3 = allgather8([buf3], "gather_small_grads")[0]
    tot3 = _unflat(sum_leading(g3, F32, "sum_small_grads"), lay3)
    gsm = dict(zip(rep_names, tot3[:len(rep_names)]))
    gsm["hg_lb_logits"] = tot3[len(rep_names)]
    dctx_tot, dlat_tot = tot3[-2], tot3[-1]
    dlat_each = jnp.stack([_unflat(g3[d], lay3)[-1] for d in range(8)], axis=1)
    grads = {n: gsm[n].reshape(W[n].shape) for n in rep_names if n not in SHARDED_SMALL}
    for n in SHARDED_SMALL:
        wl_ = W[n].shape[-1]
        grads[n] = lax.dynamic_slice_in_dim(gsm[n], chip * wl_, wl_, axis=gsm[n].ndim - 1)
    grads["b_ada"] = dctx_tot + dlat_tot

    ds16 = jnp.zeros((16, D), F32)
    gw_ada = []
    for l in range(depth):
        dm = jnp.concatenate([dlat_each[l], dctx_tot[l][None], jnp.zeros((7, ADA), F32)])
        dm = lax.dynamic_slice_in_dim(dm, chip * wcols, wcols, axis=1)
        gw_ada.append(mm_tn(s16, dm, 1, F32, f"grad_w_ada_{l}")[0])
        ds16 = ds16 + mm_nt(dm, w_ada[l][None], F32, f"ada_bwd_{l}")
    g4 = allgather8([ds16[8:16]], "gather_cond_grad")[0]
    d_scc = g4[0, 0] + g4[2, 0] + g4[4, 0] + g4[6, 0]
    sg = jax.nn.sigmoid(c_ctx)
    grads["c_ctx"] = d_scc * (sg * (1.0 + c_ctx * (1.0 - sg)))

    keys = [(l, n) for l in range(depth) for n in proj]
    parts = [big_grads[l][n] for l, n in keys]
    got = swap_halves(parts, "reduce_sibling")
    core = ci.astype(jnp.int32).reshape(1)
    pairs = [pair_sum(p, g, core, f"reduce_pair_sum_{n}_{l}") for (l, n), p, g in zip(keys, parts, got)]
    quads = chip_alltoall(pairs, "reduce_chips")
    mine = [sum_leading(q, F32, f"reduce_chip_sum_{n}_{l}") for (l, n), q in zip(keys, quads)]
    both = share_halves(mine, "reduce_share")
    layer_grads = {n: [None] * depth for n in proj}
    for (l, n), b in zip(keys, both):
        layer_grads[n][l] = b.reshape(-1, b.shape[-1])
    layer_grads["w_ada"] = gw_ada

    delta, new_m, new_v = {}, {}, {}
    for n in BIG:
        shp = W[n].shape
        two = lambda a: a.reshape(-1, shp[-1])
        g_, d_, m_, v_ = adamw(two(W[n]), layer_grads[n], two(Mo[n]), two(Vo[n]), f"adamw_{n}")
        grads[n], delta[n], new_m[n], new_v[n] = g_.reshape(shp), d_.reshape(shp), m_.reshape(shp), v_.reshape(shp)
    smalls = [n for n in WEIGHTS if n not in BIG]
    pw, lay_s = _flat_rows([W[n] for n in smalls], F32)
    pg, _ = _flat_rows([grads[n] for n in smalls], F32)
    pm, _ = _flat_rows([Mo[n] for n in smalls], F32)
    pvv, _ = _flat_rows([Vo[n] for n in smalls], F32)
    _, d_, m_, v_ = adamw(pw, [pg], pm, pvv, "adamw_small")
    for n, dd, mm_, vv in zip(smalls, _unflat(d_, lay_s), _unflat(m_, lay_s), _unflat(v_, lay_s)):
        delta[n], new_m[n], new_v[n] = dd, mm_, vv

    return (loss, grad_x, *[grads[n] for n in WEIGHTS], *[delta[n] for n in WEIGHTS],
            *[new_m[n] for n in WEIGHTS], *[new_v[n] for n in WEIGHTS])
```

```python
import functools
import math

import numpy as np
import jax
import jax.numpy as jnp
from jax import lax
from jax.experimental import pallas as pl
from jax.experimental.pallas import tpu as pltpu

F32 = jnp.float32
BF16 = jnp.bfloat16
MESH = pl.DeviceIdType.MESH
ANY = pl.BlockSpec(memory_space=pl.ANY)
VMEM_SPEC = pl.BlockSpec(memory_space=pltpu.VMEM)

LANE = 128
CHUNK = 64
SUB = 16
GRID_W = 64
WIN_R = 8
WIN_C = 16
EPS = 1e-6
F_FLOOR = 1e-30
NEG_INF = -1e30
EXP_CLAMP = 80.0
ATTN_SCALE = LANE ** -0.5
VMEM_LIMIT = 56 * 1024 * 1024
ADAM_LR, ADAM_B1, ADAM_B2, ADAM_EPS, ADAM_WD, ADAM_STEP = 0.001, 0.9, 0.999, 1e-08, 0.01, 10


def _cp(*sem):
    return pltpu.CompilerParams(dimension_semantics=sem or None, vmem_limit_bytes=VMEM_LIMIT)


def _me():
    return lax.axis_index("x"), lax.axis_index("y"), lax.axis_index("c")


def allgather8(blocks, name, hbm=False):
    na = len(blocks)

    def body(*refs):
        x_refs, out_refs = refs[:na], refs[na:2 * na]
        send_sems, recv_sems, local_sems = refs[2 * na:]
        x, y, c = _me()
        me, sibling = (x, y, c), (x, y, 1 - c)
        chips = [(1 - x, y), (x, 1 - y), (1 - x, 1 - y)]

        def rows(a, px, py, pc):
            return out_refs[a].at[4 * px + 2 * py + pc]

        def copy(a, k, blk, to, src=None):
            return pltpu.make_async_remote_copy(
                src_ref=rows(a, *blk) if src is None else src, dst_ref=rows(a, *blk),
                send_sem=send_sems.at[a, k], recv_sem=recv_sems.at[a, k], device_id=to, device_id_type=MESH)

        mine = [pltpu.make_async_copy(x_refs[a], rows(a, *me), local_sems.at[a]) for a in range(na)]
        first, passed = [], []
        for a in range(na):
            mine[a].start()
            first.append(copy(a, 0, me, sibling, src=x_refs[a]))
            first += [copy(a, 1 + j, me, (*chip, c), src=x_refs[a]) for j, chip in enumerate(chips)]
        for cp in first:
            cp.start()
        for j, chip in enumerate(chips):
            for a in range(na):
                copy(a, 1 + j, (*chip, c), me).wait_recv()
                passed.append(copy(a, 4 + j, (*chip, c), sibling))
                passed[-1].start()
        for a in range(na):
            copy(a, 0, sibling, me).wait_recv()
            for j, chip in enumerate(chips):
                copy(a, 4 + j, (*chip, 1 - c), me).wait_recv()
        for cp in first + passed:
            cp.wait_send()
        for cp in mine:
            cp.wait()

    spec = ANY if hbm else VMEM_SPEC
    return pl.pallas_call(
        body, name=name, out_shape=[jax.ShapeDtypeStruct((8,) + b.shape, b.dtype) for b in blocks],
        in_specs=[spec] * na, out_specs=[spec] * na,
        scratch_shapes=[pltpu.SemaphoreType.DMA((na, 7)), pltpu.SemaphoreType.DMA((na, 7)),
                        pltpu.SemaphoreType.DMA((na,))],
        compiler_params=pltpu.CompilerParams(vmem_limit_bytes=VMEM_LIMIT),
    )(*blocks)


def swap_halves(gs, name):
    na = len(gs)
    hrs = [g.shape[1] // 2 for g in gs]

    def body(*refs):
        g_refs, o_refs = refs[:na], refs[na:2 * na]
        send_sems, recv_sems = refs[2 * na:]
        x, y, c = _me()
        cps = []
        for a in range(na):
            for s in range(4):
                src = g_refs[a].at[s, pl.ds(pl.multiple_of((1 - c) * hrs[a], 16), hrs[a]), :]
                cps.append(pltpu.make_async_remote_copy(
                    src_ref=src, dst_ref=o_refs[a].at[s], send_sem=send_sems.at[a, s], recv_sem=recv_sems.at[a, s],
                    device_id=(x, y, 1 - c), device_id_type=MESH))
        for cp in cps:
            cp.start()
        for cp in cps:
            cp.wait()

    return pl.pallas_call(
        body, name=name, out_shape=[jax.ShapeDtypeStruct((4, hrs[a], gs[a].shape[2]), gs[a].dtype) for a in range(na)],
        in_specs=[ANY] * na, out_specs=[ANY] * na,
        scratch_shapes=[pltpu.SemaphoreType.DMA((na, 4)), pltpu.SemaphoreType.DMA((na, 4))],
    )(*gs)


def chip_alltoall(gs, name):
    na = len(gs)

    def body(*refs):
        g_refs, o_refs = refs[:na], refs[na:2 * na]
        send_sems, recv_sems, local_sems = refs[2 * na:]
        x, y, c = _me()
        mine = 2 * x + y
        own, cps = [], []
        for a in range(na):
            own.append(pltpu.make_async_copy(g_refs[a].at[mine], o_refs[a].at[mine], local_sems.at[a]))
            own[-1].start()
            for k, (px, py) in enumerate([(1 - x, y), (x, 1 - y), (1 - x, 1 - y)]):
                cps.append(pltpu.make_async_remote_copy(
                    src_ref=g_refs[a].at[2 * px + py], dst_ref=o_refs[a].at[mine], send_sem=send_sems.at[a, k],
                    recv_sem=recv_sems.at[a, k], device_id=(px, py, c), device_id_type=MESH))
        for cp in cps:
            cp.start()
        for cp in cps:
            cp.wait()
        for cp in own:
            cp.wait()

    return pl.pallas_call(
        body, name=name, out_shape=[jax.ShapeDtypeStruct(g.shape, g.dtype) for g in gs],
        in_specs=[ANY] * na, out_specs=[ANY] * na,
        scratch_shapes=[pltpu.SemaphoreType.DMA((na, 3)), pltpu.SemaphoreType.DMA((na, 3)),
                        pltpu.SemaphoreType.DMA((na,))],
    )(*gs)


def share_halves(vs, name):
    na = len(vs)

    def body(*refs):
        v_refs, o_refs = refs[:na], refs[na:2 * na]
        send_sems, recv_sems = refs[2 * na:]
        x, y, c = _me()
        cps = [pltpu.make_async_remote_copy(
            src_ref=v_refs[a], dst_ref=o_refs[a], send_sem=send_sems.at[a], recv_sem=recv_sems.at[a],
            device_id=(x, y, 1 - c), device_id_type=MESH) for a in range(na)]
        for cp in cps:
            cp.start()
        for cp in cps:
            cp.wait()

    return pl.pallas_call(
        body, name=name, out_shape=[jax.ShapeDtypeStruct(v.shape, v.dtype) for v in vs],
        in_specs=[ANY] * na, out_specs=[ANY] * na,
        scratch_shapes=[pltpu.SemaphoreType.DMA((na,)), pltpu.SemaphoreType.DMA((na,))],
    )(*vs)


def _row_block(rows, cap):
    rb = math.gcd(rows, cap)
    return rb if rb % 8 == 0 else rows


def sum_leading(x, out_dtype, name):
    n, r, c = x.shape
    rb = _row_block(r, 1024)

    def body(x_ref, o_ref):
        acc = x_ref[0].astype(F32)
        for k in range(1, n):
            acc = acc + x_ref[k].astype(F32)
        o_ref[...] = acc.astype(o_ref.dtype)

    return pl.pallas_call(
        body, name=name, grid=(r // rb,), out_shape=jax.ShapeDtypeStruct((r, c), out_dtype),
        in_specs=[pl.BlockSpec((n, rb, c), lambda i: (0, i, 0))], out_specs=pl.BlockSpec((rb, c), lambda i: (i, 0)),
        compiler_params=_cp("parallel"),
    )(x)


def pair_sum(g, got, core, name):
    _, r2, n = g.shape
    hr = r2 // 2
    rb = math.gcd(hr, 512)
    nb = hr // rb

    def body(c_ref, a_ref, b_ref, o_ref):
        o_ref[...] = (a_ref[...].astype(F32) + b_ref[...].astype(F32)).astype(o_ref.dtype)

    spec = pl.BlockSpec((None, rb, n), lambda s, i, c_ref: (s, i, 0))
    return pl.pallas_call(
        body, name=name, out_shape=jax.ShapeDtypeStruct((4, hr, n), g.dtype),
        grid_spec=pltpu.PrefetchScalarGridSpec(
            num_scalar_prefetch=1, grid=(4, nb),
            in_specs=[pl.BlockSpec((None, rb, n), lambda s, i, c_ref: (s, c_ref[0] * nb + i, 0)), spec],
            out_specs=spec),
        compiler_params=_cp("parallel", "parallel"),
    )(core, g, got)


def adamw(w, gs, m, v, name):
    rows, c = w.shape
    ng = len(gs)
    r = rows // ng
    rb = _row_block(r, 128 if c > 2048 else 256 if c > 1024 else 1024)
    nb = r // rb
    bc1 = 1.0 - ADAM_B1 ** ADAM_STEP
    bc2 = 1.0 - ADAM_B2 ** ADAM_STEP

    def body(w_ref, *refs):
        g_refs, (m_ref, v_ref, g_out, d_ref, nm_ref, nv_ref) = refs[:ng], refs[ng:]
        part = pl.program_id(0) // nb
        gg = g_refs[0][...]
        for k in range(1, ng):
            gg = jnp.where(part == k, g_refs[k][...], gg)
        nm = ADAM_B1 * m_ref[...] + (1.0 - ADAM_B1) * gg
        nv = ADAM_B2 * v_ref[...] + (1.0 - ADAM_B2) * (gg * gg)
        g_out[...] = gg
        d_ref[...] = -ADAM_LR * ((nm / bc1) / (jnp.sqrt(nv / bc2) + ADAM_EPS) + ADAM_WD * w_ref[...])
        nm_ref[...] = nm
        nv_ref[...] = nv

    spec = pl.BlockSpec((rb, c), lambda i: (i, 0))
    gspecs = [pl.BlockSpec((rb, c), functools.partial(lambda k, i: (jnp.clip(i - k * nb, 0, nb - 1), 0), k))
              for k in range(ng)]
    sds = jax.ShapeDtypeStruct((rows, c), F32)
    return pl.pallas_call(
        body, name=name, grid=(ng * nb,), out_shape=(sds,) * 4, in_specs=[spec] + gspecs + [spec, spec],
        out_specs=(spec,) * 4, compiler_params=_cp("parallel"),
    )(w, *gs, m, v)


def adamw_halves(w, mine, other, m, v, core, name):
    rows, c = w.shape
    nl = len(mine)
    hr = rows // (2 * nl)
    rb = _row_block(hr, 128 if c > 2048 else 256 if c > 1024 else 1024)
    nb = hr // rb
    bc1 = 1.0 - ADAM_B1 ** ADAM_STEP
    bc2 = 1.0 - ADAM_B2 ** ADAM_STEP

    def body(c_ref, w_ref, *refs):
        mine_refs, other_refs = refs[:nl], refs[nl:2 * nl]
        m_ref, v_ref, g_out, d_ref, nm_ref, nv_ref = refs[2 * nl:]
        part = pl.program_id(0) // nb
        layer, half = part // 2, part % 2
        gg = jnp.where(half == c_ref[0], mine_refs[0][...], other_refs[0][...])
        for k in range(1, nl):
            gg = jnp.where(layer == k, jnp.where(half == c_ref[0], mine_refs[k][...], other_refs[k][...]), gg)
        nm = ADAM_B1 * m_ref[...] + (1.0 - ADAM_B1) * gg
        nv = ADAM_B2 * v_ref[...] + (1.0 - ADAM_B2) * (gg * gg)
        g_out[...] = gg
        d_ref[...] = -ADAM_LR * ((nm / bc1) / (jnp.sqrt(nv / bc2) + ADAM_EPS) + ADAM_WD * w_ref[...])
        nm_ref[...] = nm
        nv_ref[...] = nv

    spec = pl.BlockSpec((rb, c), lambda i, c_ref: (i, 0))
    gspecs = [pl.BlockSpec((rb, c), functools.partial(
        lambda k, i, c_ref: (jnp.clip(i - 2 * k * nb, 0, 2 * nb - 1) % nb, 0), k)) for k in range(nl)]
    sds = jax.ShapeDtypeStruct((rows, c), F32)
    return pl.pallas_call(
        body, name=name, out_shape=(sds,) * 4,
        grid_spec=pltpu.PrefetchScalarGridSpec(
            num_scalar_prefetch=1, grid=(2 * nl * nb,), in_specs=[spec] + gspecs + gspecs + [spec, spec],
            out_specs=(spec,) * 4),
        compiler_params=_cp("parallel"),
    )(core, w, *mine, *other, m, v)


def _pick(n, prefs):
    for p in prefs:
        if n % p == 0:
            return p
    return n


def _mm_body(dims, nk, out_dtype):
    def body(a_ref, b_ref, o_ref, acc):
        kk = pl.program_id(2)
        part = lax.dot_general(a_ref[...].astype(BF16), b_ref[...].astype(BF16), (dims, ((), ())),
                               preferred_element_type=F32)
        if nk == 1:
            o_ref[...] = part.astype(out_dtype)
        else:
            @pl.when(kk == 0)
            def _():
                acc[...] = part

            @pl.when(kk > 0)
            def _():
                acc[...] += part

            @pl.when(kk == nk - 1)
            def _():
                o_ref[...] = acc[...].astype(out_dtype)
    return body


def mm_nn(a, w, out_dtype, name):
    M, K = a.shape
    S, _, Ns = w.shape
    tm = _pick(M, (1088, 1024, 512, 256, 128))
    tn = _pick(Ns, (1024, 896, 1408, 512, 256, 128))
    tk = _pick(K, (2048, 1408, 1024, 512, 256, 128))
    nps, nk = Ns // tn, K // tk
    return pl.pallas_call(
        _mm_body(((1,), (0,)), nk, out_dtype), name=name, grid=(S * nps, M // tm, nk),
        out_shape=jax.ShapeDtypeStruct((M, S * Ns), out_dtype),
        in_specs=[pl.BlockSpec((tm, tk), lambda j, i, k: (i, k)),
                  pl.BlockSpec((None, tk, tn), lambda j, i, k: (j // nps, k, j % nps))],
        out_specs=pl.BlockSpec((tm, tn), lambda j, i, k: (i, j)),
        scratch_shapes=[pltpu.VMEM((tm, tn), F32)], compiler_params=_cp("parallel", "parallel", "arbitrary"),
    )(a, w)


def mm_nt(dy, w, out_dtype, name):
    M, N = dy.shape
    S, K, Ns = w.shape
    tm = _pick(M, (1088, 1024, 512, 256, 128))
    tn = _pick(K, (1024, 512, 256, 128))
    tk = _pick(Ns, (1024, 896, 1408, 512, 256, 128))
    kps, nk = Ns // tk, N // tk
    return pl.pallas_call(
        _mm_body(((1,), (1,)), nk, out_dtype), name=name, grid=(K // tn, M // tm, nk),
        out_shape=jax.ShapeDtypeStruct((M, K), out_dtype),
        in_specs=[pl.BlockSpec((tm, tk), lambda j, i, k: (i, k)),
                  pl.BlockSpec((None, tn, tk), lambda j, i, k: (k // kps, j, k % kps))],
        out_specs=pl.BlockSpec((tm, tn), lambda j, i, k: (i, j)),
        scratch_shapes=[pltpu.VMEM((tm, tn), F32)], compiler_params=_cp("parallel", "parallel", "arbitrary"),
    )(dy, w)


def mm_tn(a, dy, S, out_dtype, name):
    M, K = a.shape
    N = dy.shape[1]
    Ns = N // S
    to = _pick(K, (1024, 512, 256, 128))
    tn = _pick(Ns, (1024, 896, 1408, 512, 256, 128))
    tk = _pick(M, (1088, 1024, 512, 256, 128))
    nps, nk = Ns // tn, M // tk
    return pl.pallas_call(
        _mm_body(((0,), (0,)), nk, out_dtype), name=name, grid=(K // to, S * nps, nk),
        out_shape=jax.ShapeDtypeStruct((S, K, Ns), out_dtype),
        in_specs=[pl.BlockSpec((tk, to), lambda i, j, k: (k, i)),
                  pl.BlockSpec((tk, tn), lambda i, j, k: (k, j))],
        out_specs=pl.BlockSpec((None, to, tn), lambda i, j, k: (j // nps, i, j % nps)),
        scratch_shapes=[pltpu.VMEM((to, tn), F32)], compiler_params=_cp("parallel", "parallel", "arbitrary"),
    )(a, dy)


_DIMS = {"nn": ((1,), (0,)), "nt": ((1,), (1,)), "tn": ((0,), (0,))}


def _dot(a, b, mode):
    return lax.dot_general(a.astype(BF16), b.astype(BF16), (_DIMS[mode], ((), ())), preferred_element_type=F32)


@functools.partial(jax.custom_vjp, nondiff_argnums=(2,))
def mmf(a, b, mode):
    return _dot(a, b, mode)


def _mmf_fwd(a, b, mode):
    return _dot(a, b, mode), (a, b)


def _mmf_bwd(mode, res, ct):
    a, b = res
    if mode == "nn":
        return _dot(ct, b, "nt"), _dot(a, ct, "tn")
    if mode == "nt":
        return _dot(ct, b, "nn"), _dot(ct, a, "tn")
    return _dot(b, ct, "nt"), _dot(a, ct, "nn")


mmf.defvjp(_mmf_fwd, _mmf_bwd)


def _dot_hi(m, g):
    return jnp.dot(m, g, precision=lax.Precision.HIGHEST, preferred_element_type=F32)


@jax.custom_vjp
def cumdot(m, mt, g):
    return _dot_hi(m, g)


def _cumdot_fwd(m, mt, g):
    return _dot_hi(m, g), (m, mt)


def _cumdot_bwd(res, ct):
    m, mt = res
    return jnp.zeros_like(m), jnp.zeros_like(mt), _dot_hi(mt, ct)


cumdot.defvjp(_cumdot_fwd, _cumdot_bwd)


def _rms(x, w):
    return x * lax.rsqrt(jnp.mean(x * x, axis=-1, keepdims=True) + EPS) * w


def _silu(x):
    return x * jax.nn.sigmoid(x)


RT = 16


def _gn_math(has_gate, x, m, gate, lnw, shift, scale):
    xn = x + gate * m if has_gate else x
    h = _rms(xn, lnw) * (1.0 + scale) + shift
    return xn, h


def _seg_spec(width, ncb):
    return pl.BlockSpec((None, RT, width), lambda i: (jnp.minimum(i // ncb, 1), 0, 0))


def gate_norm(x, m, gate, lnw, shift, scale, nc, R, name):
    T, D = x.shape
    has_gate = m is not None
    ncb = nc // R

    def body(*refs):
        if has_gate:
            x_ref, m_ref, g_ref, w_ref, sh_ref, sc_ref, xn_ref, h_ref = refs
        else:
            x_ref, w_ref, sh_ref, sc_ref, h_ref = refs

        def step(t, carry):
            rows = pl.ds(pl.multiple_of(t * RT, RT), RT)
            xn, h = _gn_math(has_gate, x_ref[rows, :], m_ref[rows, :] if has_gate else None,
                             g_ref[...] if has_gate else None, w_ref[...], sh_ref[...], sc_ref[...])
            if has_gate:
                xn_ref[rows, :] = xn
            h_ref[rows, :] = h.astype(BF16)
            return carry

        lax.fori_loop(0, R // RT, step, 0)

    row = pl.BlockSpec((R, D), lambda i: (i, 0))
    seg = _seg_spec(D, ncb)
    shared = pl.BlockSpec((None, RT, D), lambda i: (0, 0, 0))
    if has_gate:
        ins, in_specs = (x, m, gate, lnw, shift, scale), [row, row, seg, shared, seg, seg]
        out_shape = (jax.ShapeDtypeStruct((T, D), F32), jax.ShapeDtypeStruct((T, D), BF16))
        out_specs = (row, row)
    else:
        ins, in_specs = (x, lnw, shift, scale), [row, shared, seg, seg]
        out_shape, out_specs = jax.ShapeDtypeStruct((T, D), BF16), row
    out = pl.pallas_call(body, name=name, grid=(T // R,), out_shape=out_shape, in_specs=in_specs,
                         out_specs=out_specs, compiler_params=_cp("parallel"))(*ins)
    return out if has_gate else (None, out)


def gate_norm_bwd(x, m, gate, lnw, shift, scale, dxn, dh, nc, R, name):
    T, D = x.shape
    has_gate = m is not None
    ncb = nc // R

    def body(*refs):
        if has_gate:
            (x_ref, m_ref, g_ref, w_ref, sh_ref, sc_ref, dxn_ref, dh_ref,
             dx_ref, dm_ref, dg_ref, dw_ref, dsh_ref, dsc_ref) = refs
        else:
            x_ref, w_ref, sh_ref, sc_ref, dxn_ref, dh_ref, dx_ref, dw_ref, dsh_ref, dsc_ref = refs
        i = pl.program_id(0)

        @pl.when(i == 0)
        def _():
            dw_ref[...] = jnp.zeros_like(dw_ref)

        @pl.when((i == 0) | (i == ncb))
        def _():
            dsh_ref[...] = jnp.zeros_like(dsh_ref)
            dsc_ref[...] = jnp.zeros_like(dsc_ref)
            if has_gate:
                dg_ref[...] = jnp.zeros_like(dg_ref)

        def step(t, carry):
            rows = pl.ds(pl.multiple_of(t * RT, RT), RT)
            ct = (dxn_ref[rows, :], dh_ref[rows, :])
            if has_gate:
                _, vjp = jax.vjp(functools.partial(_gn_math, True), x_ref[rows, :], m_ref[rows, :], g_ref[...],
                                 w_ref[...], sh_ref[...], sc_ref[...])
                dx, dm, dg, dw, dsh, dsc = vjp(ct)
                dm_ref[rows, :] = dm.astype(BF16)
                dg_ref[...] += dg
            else:
                f = lambda x_, w_, sh_, sc_: _gn_math(False, x_, None, None, w_, sh_, sc_)[1]
                _, vjp = jax.vjp(f, x_ref[rows, :], w_ref[...], sh_ref[...], sc_ref[...])
                dx, dw, dsh, dsc = vjp(ct[1])
                dx = dx + ct[0]
            dx_ref[rows, :] = dx
            dw_ref[...] += dw
            dsh_ref[...] += dsh
            dsc_ref[...] += dsc
            return carry

        lax.fori_loop(0, R // RT, step, 0)

    row = pl.BlockSpec((R, D), lambda i: (i, 0))
    seg = _seg_spec(D, ncb)
    shared = pl.BlockSpec((None, RT, D), lambda i: (0, 0, 0))
    full, segs, one = jax.ShapeDtypeStruct((T, D), F32), jax.ShapeDtypeStruct((2, RT, D), F32), \
        jax.ShapeDtypeStruct((1, RT, D), F32)
    if has_gate:
        ins = (x, m, gate, lnw, shift, scale, dxn, dh)
        in_specs = [row, row, seg, shared, seg, seg, row, row]
        out_shape = (full, jax.ShapeDtypeStruct((T, D), BF16), segs, one, segs, segs)
        out_specs = (row, row, seg, shared, seg, seg)
    else:
        ins = (x, lnw, shift, scale, dxn, dh)
        in_specs = [row, shared, seg, seg, row, row]
        out_shape = (full, one, segs, segs)
        out_specs = (row, shared, seg, seg)
    out = pl.pallas_call(body, name=name, grid=(T // R,), out_shape=out_shape, in_specs=in_specs,
                         out_specs=out_specs, compiler_params=_cp("arbitrary"))(*ins)
    if has_gate:
        return out
    dx, dw, dsh, dsc = out
    return dx, None, None, dw, dsh, dsc


def gate_loss(x, m, gate, target, nc, R, name):
    T, D = x.shape
    ncb = nc // R

    def body(x_ref, m_ref, g_ref, t_ref, loss_ref, dx_ref, dm_ref, dg_ref):
        i = pl.program_id(0)

        @pl.when(i == 0)
        def _():
            loss_ref[...] = jnp.zeros_like(loss_ref)

        @pl.when((i == 0) | (i == ncb))
        def _():
            dg_ref[...] = jnp.zeros_like(dg_ref)

        live = jnp.where(i >= ncb, 1.0, 0.0).astype(F32)

        def step(t, carry):
            rows = pl.ds(pl.multiple_of(t * RT, RT), RT)
            mm_ = m_ref[rows, :]
            g = g_ref[...]
            e = (x_ref[rows, :] + g * mm_ - t_ref[rows, :]) * live
            dy = e * (1.0 / D)
            loss_ref[...] += 0.5 * e * dy
            dx_ref[rows, :] = dy
            dm_ref[rows, :] = (dy * g).astype(BF16)
            dg_ref[...] += dy * mm_
            return carry

        lax.fori_loop(0, R // RT, step, 0)

    row = pl.BlockSpec((R, D), lambda i: (i, 0))
    seg = _seg_spec(D, ncb)
    return pl.pallas_call(
        body, name=name, grid=(T // R,),
        out_shape=(jax.ShapeDtypeStruct((RT, D), F32), jax.ShapeDtypeStruct((T, D), F32),
                   jax.ShapeDtypeStruct((T, D), BF16), jax.ShapeDtypeStruct((2, RT, D), F32)),
        in_specs=[row, row, seg, pl.BlockSpec((R, D), lambda i: (jnp.maximum(i - ncb, 0), 0))],
        out_specs=(pl.BlockSpec((RT, D), lambda i: (0, 0)), row, row, seg),
        compiler_params=_cp("arbitrary"),
    )(x, m, gate, target)


def _hg_chunk(rev, lb, z, iv, hq, st):
    f = lb + (1.0 - lb) * jax.nn.sigmoid(z)
    g = jnp.log(jnp.maximum(f, F_FLOOR))
    k = (1.0 - lb) * jax.nn.sigmoid(-z)
    q = _silu(hq)
    ri = lax.broadcasted_iota(jnp.int32, (CHUNK, CHUNK), 0)
    ci = lax.broadcasted_iota(jnp.int32, (CHUNK, CHUNK), 1)
    r1 = lax.broadcasted_iota(jnp.int32, (CHUNK, 1), 0)
    seen = (ci >= ri) if rev else (ci <= ri)
    seen_t = (ci <= ri) if rev else (ci >= ri)
    cum = cumdot(seen.astype(F32), seen_t.astype(F32), g)
    tot = jnp.sum(g, axis=0, keepdims=True)
    att = jnp.zeros((CHUNK, CHUNK), F32)
    ref_rows = jnp.zeros_like(g)
    refs = []
    for b in range(CHUNK // SUB):
        before = (r1 >= SUB * (b + 1)) if rev else (r1 < SUB * b)
        r_b = jnp.sum(jnp.where(before, g, 0.0), axis=0, keepdims=True)
        in_b = (r1 >= SUB * b) & (r1 < SUB * (b + 1))
        ref_rows = ref_rows + jnp.where(in_b, r_b, 0.0)
        refs.append(r_b)
    qd = q * jnp.exp(cum - ref_rows)
    for b in range(CHUNK // SUB):
        kd = k * jnp.exp(jnp.minimum(refs[b] - cum, EXP_CLAMP))
        in_b = (ri >= SUB * b) & (ri < SUB * (b + 1))
        att = att + jnp.where(in_b, mmf(qd, kd, "nt"), 0.0)
    att = jnp.where(seen, att, 0.0)
    o = mmf(att, iv, "nn") + mmf(q * jnp.exp(cum), st, "nt")
    st_new = st * jnp.exp(tot) + mmf(iv, k * jnp.exp(tot - cum), "tn")
    return st_new, o


def _hg_cid(rev, i, ncs, n):
    if not rev:
        return i
    return jnp.where(i < ncs, ncs - 1 - i, ncs + n - 1 - i)


def hgrn_fwd(u, lb, rev, zcol, nc, hgw, name):
    T = u.shape[0]
    n, ncs, nh = T // CHUNK, nc // CHUNK, hgw // LANE

    def body(z_ref, v_ref, q_ref, lb_ref, o_ref, s_ref, st):
        i = pl.program_id(0)

        @pl.when(i == 0)
        def _():
            st[...] = jnp.zeros_like(st)

        for h in range(nh):
            cols = slice(h * LANE, (h + 1) * LANE)
            s_ref[h] = st[h]
            s_new, o = _hg_chunk(rev, lb_ref[:, cols], z_ref[:, cols], v_ref[:, cols], q_ref[:, cols], st[h])
            st[h] = s_new
            o_ref[:, cols] = o

    def col(cb):
        return pl.BlockSpec((CHUNK, hgw), lambda i: (_hg_cid(rev, i, ncs, n), cb))

    return pl.pallas_call(
        body, name=name, grid=(n,),
        out_shape=(jax.ShapeDtypeStruct((T, hgw), F32), jax.ShapeDtypeStruct((n, nh, LANE, LANE), F32)),
        in_specs=[col(zcol), col(2), col(7), pl.BlockSpec((1, hgw), lambda i: (0, 0))],
        out_specs=(pl.BlockSpec((CHUNK, hgw), lambda i: (_hg_cid(rev, i, ncs, n), 0)),
                   pl.BlockSpec((None, nh, LANE, LANE), lambda i: (i, 0, 0, 0))),
        scratch_shapes=[pltpu.VMEM((nh, LANE, LANE), F32)], compiler_params=_cp("arbitrary"),
    )(u, u, u, lb)


def hgrn_bwd(u, lb, states, do, rev, zcol, nc, hgw, name):
    T = u.shape[0]
    n, ncs, nh = T // CHUNK, nc // CHUNK, hgw // LANE

    def body(z_ref, v_ref, q_ref, lb_ref, s_ref, do_ref, dz_ref, dv_ref, dq_ref, dlb_ref, dst):
        j = pl.program_id(0)

        @pl.when(j == 0)
        def _():
            dst[...] = jnp.zeros_like(dst)
            dlb_ref[...] = jnp.zeros_like(dlb_ref)

        for h in range(nh):
            cols = slice(h * LANE, (h + 1) * LANE)
            _, vjp = jax.vjp(functools.partial(_hg_chunk, rev), lb_ref[:, cols], z_ref[:, cols], v_ref[:, cols],
                             q_ref[:, cols], s_ref[h])
            dlb, dz, dv, dq, ds = vjp((dst[h], do_ref[:, cols]))
            dst[h] = ds
            dz_ref[:, cols] = dz
            dv_ref[:, cols] = dv
            dq_ref[:, cols] = dq
            dlb_ref[:, cols] += dlb

    def cid(j):
        return _hg_cid(rev, n - 1 - j, ncs, n)

    def col(cb):
        return pl.BlockSpec((CHUNK, hgw), lambda j: (cid(j), cb))

    out = pl.BlockSpec((CHUNK, hgw), lambda j: (cid(j), 0))
    full = jax.ShapeDtypeStruct((T, hgw), F32)
    return pl.pallas_call(
        body, name=name, grid=(n,),
        out_shape=(full, full, full, jax.ShapeDtypeStruct((1, hgw), F32)),
        in_specs=[col(zcol), col(2), col(7), pl.BlockSpec((1, hgw), lambda j: (0, 0)),
                  pl.BlockSpec((None, nh, LANE, LANE), lambda j: (n - 1 - j, 0, 0, 0)), out],
        out_specs=(out, out, out, pl.BlockSpec((1, hgw), lambda j: (0, 0))),
        scratch_shapes=[pltpu.VMEM((nh, LANE, LANE), F32)], compiler_params=_cp("arbitrary"),
    )(u, u, u, lb, states, do)


HT = 128


def _read_math(ofw, obw, g, w):
    return _rms(ofw + obw, w) * _silu(g)


def hg_read(ofw, obw, u, w, gcol, R, name):
    T, hgw = ofw.shape
    nh = hgw // LANE

    def body(a_ref, b_ref, g_ref, w_ref, o_ref):
        for t in range(R // HT):
            rows = slice(t * HT, (t + 1) * HT)
            o_ref[rows, :] = _read_math(a_ref[rows, :], b_ref[rows, :], g_ref[rows, :], w_ref[...]).astype(BF16)

    blk = pl.BlockSpec((R, LANE), lambda i, h: (i, h))
    return pl.pallas_call(
        body, name=name, grid=(T // R, nh), out_shape=jax.ShapeDtypeStruct((T, hgw), BF16),
        in_specs=[blk, blk, pl.BlockSpec((R, LANE), lambda i, h: (i, gcol + h)),
                  pl.BlockSpec((1, LANE), lambda i, h: (0, 0))],
        out_specs=blk, compiler_params=_cp("parallel", "parallel"),
    )(ofw, obw, u, w)


def hg_read_bwd(ofw, obw, u, w, dout, gcol, ocol, R, name):
    T, hgw = ofw.shape
    nh = hgw // LANE

    def body(a_ref, b_ref, g_ref, w_ref, d_ref, do_ref, dg_ref, dw_ref):
        @pl.when(pl.program_id(1) == 0)
        def _():
            dw_ref[...] = jnp.zeros_like(dw_ref)

        for t in range(R // HT):
            rows = slice(t * HT, (t + 1) * HT)
            _, vjp = jax.vjp(_read_math, a_ref[rows, :], b_ref[rows, :], g_ref[rows, :], w_ref[...])
            da, _, dg, dw = vjp(d_ref[rows, :])
            do_ref[rows, :] = da
            dg_ref[rows, :] = dg
            dw_ref[...] += dw

    blk = pl.BlockSpec((R, LANE), lambda h, i: (i, h))
    full = jax.ShapeDtypeStruct((T, hgw), F32)
    return pl.pallas_call(
        body, name=name, grid=(nh, T // R), out_shape=(full, full, jax.ShapeDtypeStruct((nh, 1, LANE), F32)),
        in_specs=[blk, blk, pl.BlockSpec((R, LANE), lambda h, i: (i, gcol + h)),
                  pl.BlockSpec((1, LANE), lambda h, i: (0, 0)), pl.BlockSpec((R, LANE), lambda h, i: (i, ocol + h))],
        out_specs=(blk, blk, pl.BlockSpec((None, 1, LANE), lambda h, i: (h, 0, 0))),
        compiler_params=_cp("parallel", "arbitrary"),
    )(ofw, obw, u, w, dout)


def _na_step(qw, ow, bias, qraw, kl, vl, kc, vc):
    q = _rms(qraw, qw)
    s_loc = mmf(q, kl, "nt") * ATTN_SCALE + bias
    s_ctx = mmf(q, kc, "nt") * ATTN_SCALE
    m = lax.stop_gradient(jnp.maximum(jnp.max(s_loc, axis=-1, keepdims=True), jnp.max(s_ctx, axis=-1, keepdims=True)))
    p_loc = jnp.exp(s_loc - m)
    p_ctx = jnp.exp(s_ctx - m)
    inv = 1.0 / (jnp.sum(p_loc, axis=-1, keepdims=True) + jnp.sum(p_ctx, axis=-1, keepdims=True))
    return _rms(mmf(p_loc * inv, vl, "nn") + mmf(p_ctx * inv, vc, "nn"), ow)


def _na_geometry(nc, rows):
    ncs = nc // GRID_W
    win_r = min(WIN_R, rows)
    nloc = win_r * GRID_W

    def row_start(s):
        r = jnp.maximum(s - ncs, 0)
        return jnp.clip(r - win_r // 2, 0, rows - win_r)

    def bias_idx(s):
        r = s - ncs
        return jnp.where(s < ncs, win_r, r - jnp.clip(r - win_r // 2, 0, rows - win_r))

    return ncs, win_r, nloc, row_start, bias_idx


def na_bias_tables(rpb, rows):
    win_r = min(WIN_R, rows)
    nh = rpb.shape[0]
    sel_r = np.zeros((win_r, win_r, 2 * WIN_R - 1), np.float32)
    for off in range(win_r):
        for jr in range(win_r):
            sel_r[off, jr, jr - off + WIN_R - 1] = 1.0
    qc = np.arange(GRID_W)[:, None]
    kc = np.arange(GRID_W)[None, :]
    wstart = np.clip(qc - WIN_C // 2, 0, GRID_W - WIN_C)
    ok = (kc >= wstart) & (kc < wstart + WIN_C)
    sel_c = np.zeros((GRID_W, GRID_W, 2 * WIN_C - 1), np.float32)
    sel_c[np.broadcast_to(qc, ok.shape)[ok], np.broadcast_to(kc, ok.shape)[ok], (kc - qc + WIN_C - 1)[ok]] = 1.0
    hi = lax.Precision.HIGHEST
    t = jnp.einsum("hab,oja->hojb", rpb, sel_r, precision=hi)
    t = jnp.einsum("hojb,qkb->hoqjk", t, sel_c, precision=hi)
    t = jnp.where(ok[None, None, :, None, :], t, NEG_INF)
    t = jnp.concatenate([t, jnp.full((nh, 1, GRID_W, win_r, GRID_W), NEG_INF, F32)], axis=1)
    return t.reshape(nh, win_r + 1, GRID_W, win_r * GRID_W)


def kv_prep(u, kw, kcol, vcol, naw, R, name):
    T = u.shape[0]

    def body(k_ref, v_ref, w_ref, kn_ref, vb_ref):
        kn_ref[...] = _rms(k_ref[...], w_ref[...]).astype(BF16)
        vb_ref[...] = v_ref[...].astype(BF16)

    blk = pl.BlockSpec((R, LANE), lambda i, h: (i, h))
    sds = jax.ShapeDtypeStruct((T, naw), BF16)
    return pl.pallas_call(
        body, name=name, grid=(T // R, naw // LANE), out_shape=(sds, sds),
        in_specs=[pl.BlockSpec((R, LANE), lambda i, h: (i, kcol + h)), pl.BlockSpec((R, LANE), lambda i, h: (i, vcol + h)),
                  pl.BlockSpec((1, LANE), lambda i, h: (0, 0))],
        out_specs=(blk, blk), compiler_params=_cp("parallel", "parallel"),
    )(u, u, kw)


def kv_prep_bwd(u, kw, dkn, kcol, naw, R, name):
    T = u.shape[0]
    nh = naw // LANE

    def body(k_ref, w_ref, d_ref, dk_ref, dw_ref):
        @pl.when(pl.program_id(1) == 0)
        def _():
            dw_ref[...] = jnp.zeros_like(dw_ref)

        for t in range(R // HT):
            rows = slice(t * HT, (t + 1) * HT)
            _, vjp = jax.vjp(_rms, k_ref[rows, :], w_ref[...])
            dk, dw = vjp(d_ref[rows, :])
            dk_ref[rows, :] = dk
            dw_ref[...] += dw

    blk = pl.BlockSpec((R, LANE), lambda h, i: (i, h))
    return pl.pallas_call(
        body, name=name, grid=(nh, T // R),
        out_shape=(jax.ShapeDtypeStruct((T, naw), F32), jax.ShapeDtypeStruct((nh, 1, LANE), F32)),
        in_specs=[pl.BlockSpec((R, LANE), lambda h, i: (i, kcol + h)), pl.BlockSpec((1, LANE), lambda h, i: (0, 0)), blk],
        out_specs=(blk, pl.BlockSpec((None, 1, LANE), lambda h, i: (h, 0, 0))),
        compiler_params=_cp("parallel", "arbitrary"),
    )(u, kw, dkn)


NA_HB = 2


def _na_operands(j, s, nc, nloc, row_start, q_refs, k_ref, v_ref, qw_ref, ow_ref, b_ref):
    cols = slice(j * LANE, (j + 1) * LANE)
    loc = pl.ds(pl.multiple_of(nc + row_start(s) * GRID_W, GRID_W), nloc)
    ops = (qw_ref[...], ow_ref[:, cols], b_ref[j], q_refs[j][...], k_ref[loc, cols].astype(F32),
           v_ref[loc, cols].astype(F32), k_ref[0:nc, cols].astype(F32), v_ref[0:nc, cols].astype(F32))
    return cols, loc, ops


def na_fwd(u, kn, vb, qw, ow, bias, qcol, nc, name):
    T, naw = kn.shape
    nh, rows = naw // LANE, (T - nc) // GRID_W
    hb = NA_HB if nh % NA_HB == 0 else 1
    ncs, win_r, nloc, row_start, bias_idx = _na_geometry(nc, rows)

    def body(*refs):
        q_refs, (k_ref, v_ref, qw_ref, ow_ref, b_ref, o_ref) = refs[:hb], refs[hb:]
        s = pl.program_id(1)
        for j in range(hb):
            cols, _, ops = _na_operands(j, s, nc, nloc, row_start, q_refs, k_ref, v_ref, qw_ref, ow_ref, b_ref)
            o_ref[:, cols] = _na_step(*ops).astype(BF16)

    wide = pl.BlockSpec((T, hb * LANE), lambda g, s: (0, g))
    return pl.pallas_call(
        body, name=name, grid=(nh // hb, T // GRID_W), out_shape=jax.ShapeDtypeStruct((T, naw), BF16),
        in_specs=[pl.BlockSpec((GRID_W, LANE), functools.partial(lambda j, g, s: (s, qcol + g * hb + j), j))
                  for j in range(hb)]
        + [wide, wide, pl.BlockSpec((1, LANE), lambda g, s: (0, 0)), pl.BlockSpec((1, hb * LANE), lambda g, s: (0, g)),
           pl.BlockSpec((hb, None, GRID_W, nloc), lambda g, s: (g, bias_idx(s), 0, 0))],
        out_specs=pl.BlockSpec((GRID_W, hb * LANE), lambda g, s: (s, g)),
        compiler_params=_cp("parallel", "arbitrary"),
    )(*([u] * hb), kn, vb, qw, ow, bias)


def na_bwd(u, kn, vb, qw, ow, bias, dout, qcol, ocol, nc, name):
    T, naw = kn.shape
    nh, rows = naw // LANE, (T - nc) // GRID_W
    hb = NA_HB if nh % NA_HB == 0 else 1
    ncs, win_r, nloc, row_start, bias_idx = _na_geometry(nc, rows)
    fresh = [0] + [ncs + r for r in range(rows) if r == 0 or r - np.clip(r - win_r // 2, 0, rows - win_r)
                   != (r - 1) - np.clip(r - 1 - win_r // 2, 0, rows - win_r)]

    def body(*refs):
        q_refs, d_refs = refs[:hb], refs[hb:2 * hb]
        k_ref, v_ref, qw_ref, ow_ref, b_ref, dq_ref, dk_ref, dv_ref, db_ref, dqw_ref, dow_ref = refs[2 * hb:]
        s = pl.program_id(1)

        @pl.when(s == 0)
        def _():
            dk_ref[...] = jnp.zeros_like(dk_ref)
            dv_ref[...] = jnp.zeros_like(dv_ref)
            dqw_ref[...] = jnp.zeros_like(dqw_ref)
            dow_ref[...] = jnp.zeros_like(dow_ref)

        first = functools.reduce(lambda a, b: a | b, [s == f for f in fresh])

        @pl.when(first)
        def _():
            db_ref[...] = jnp.zeros_like(db_ref)

        for j in range(hb):
            cols, loc, ops = _na_operands(j, s, nc, nloc, row_start, q_refs, k_ref, v_ref, qw_ref, ow_ref, b_ref)
            _, vjp = jax.vjp(_na_step, *ops)
            dqw, dow, db, dq, dkl, dvl, dkc, dvc = vjp(d_refs[j][...])
            dq_ref[:, cols] = dq
            dk_ref[loc, cols] += dkl
            dv_ref[loc, cols] += dvl
            dk_ref[0:nc, cols] += dkc
            dv_ref[0:nc, cols] += dvc
            db_ref[j] += db
            dqw_ref[j] += dqw
            dow_ref[j] += dow

    wide = pl.BlockSpec((T, hb * LANE), lambda g, s: (0, g))
    hvec = pl.BlockSpec((hb, 1, LANE), lambda g, s: (g, 0, 0))
    full = jax.ShapeDtypeStruct((T, naw), F32)
    hv = jax.ShapeDtypeStruct((nh, 1, LANE), F32)
    bspec = pl.BlockSpec((hb, None, GRID_W, nloc), lambda g, s: (g, bias_idx(s), 0, 0))
    return pl.pallas_call(
        body, name=name, grid=(nh // hb, T // GRID_W),
        out_shape=(full, full, full, jax.ShapeDtypeStruct(bias.shape, F32), hv, hv),
        in_specs=[pl.BlockSpec((GRID_W, LANE), functools.partial(lambda j, g, s: (s, qcol + g * hb + j), j))
                  for j in range(hb)]
        + [pl.BlockSpec((GRID_W, LANE), functools.partial(lambda j, g, s: (s, ocol + g * hb + j), j))
           for j in range(hb)]
        + [wide, wide, pl.BlockSpec((1, LANE), lambda g, s: (0, 0)), pl.BlockSpec((1, hb * LANE), lambda g, s: (0, g)),
           bspec],
        out_specs=(pl.BlockSpec((GRID_W, hb * LANE), lambda g, s: (s, g)), wide, wide, bspec, hvec, hvec),
        compiler_params=_cp("parallel", "arbitrary"),
    )(*([u] * hb), *([dout] * hb), kn, vb, qw, ow, bias)


def _halo_specs(R, width, T, col):
    hb = R // 8
    prev = pl.BlockSpec((8, width), lambda j, i: (jnp.maximum(i * hb - 1, 0), col(j, i)))
    nxt = pl.BlockSpec((8, width), lambda j, i: (jnp.minimum((i + 1) * hb, T // 8 - 1), col(j, i)))
    return prev, nxt


def _edge_flags(i, ncb, nblk):
    has_prev = jnp.where((i == 0) | (i == ncb), 0.0, 1.0).astype(F32)
    has_next = jnp.where((i == ncb - 1) | (i == nblk - 1), 0.0, 1.0).astype(F32)
    return has_prev, has_next


def _shift_up(a, prev_row):
    r0 = lax.broadcasted_iota(jnp.int32, a.shape, 0) == 0
    return jnp.where(r0, prev_row, pltpu.roll(a, 1, 0))


def _shift_dn(a, next_row):
    n = a.shape[0]
    rl = lax.broadcasted_iota(jnp.int32, a.shape, 0) == n - 1
    return jnp.where(rl, next_row, pltpu.roll(a, n - 1, 0))


def _conv3(a, prev_row, next_row, w_ref):
    return w_ref[0:1, :] * _shift_up(a, prev_row) + w_ref[1:2, :] * a + w_ref[2:3, :] * _shift_dn(a, next_row)


def _cv_post(b, y, w):
    return _rms(b * y, w)


def short_conv(u, cw, ow, bcol, nc, cvw, R, bwd_dout=None, ocol=0, name=""):
    T = u.shape[0]
    nh, nblk, ncb = cvw // LANE, T // R, nc // R
    bwd = bwd_dout is not None

    def body(b_ref, c_ref, v_ref, cp_ref, vp_ref, cn_ref, vn_ref, cw_ref, ow_ref, *rest):
        i = pl.program_id(1)
        has_prev, has_next = _edge_flags(i, ncb, nblk)
        p = c_ref[...] * v_ref[...]
        y = _conv3(p, cp_ref[7:8, :] * vp_ref[7:8, :] * has_prev, cn_ref[0:1, :] * vn_ref[0:1, :] * has_next, cw_ref)
        if not bwd:
            rest[0][...] = _cv_post(b_ref[...], y, ow_ref[...]).astype(BF16)
            return
        d_ref, db_ref, dy_ref, dow_ref = rest

        @pl.when(i == 0)
        def _():
            dow_ref[...] = jnp.zeros_like(dow_ref)

        _, vjp = jax.vjp(_cv_post, b_ref[...], y, ow_ref[...])
        db, dy, dow = vjp(d_ref[...])
        db_ref[...] = db
        dy_ref[...] = dy
        dow_ref[...] += dow

    def main(k):
        return pl.BlockSpec((R, LANE), lambda h, i: (i, bcol + k * nh + h))

    cprev, cnext = _halo_specs(R, LANE, T, lambda h, i: bcol + nh + h)
    vprev, vnext = _halo_specs(R, LANE, T, lambda h, i: bcol + 2 * nh + h)
    in_specs = [main(0), main(1), main(2), cprev, vprev, cnext, vnext,
                pl.BlockSpec((3, LANE), lambda h, i: (0, h)), pl.BlockSpec((1, LANE), lambda h, i: (0, h))]
    ins = [u] * 7 + [cw, ow]
    blk = pl.BlockSpec((R, LANE), lambda h, i: (i, h))
    if not bwd:
        out_shape, out_specs = jax.ShapeDtypeStruct((T, cvw), BF16), blk
    else:
        in_specs.append(pl.BlockSpec((R, LANE), lambda h, i: (i, ocol + h)))
        ins.append(bwd_dout)
        full = jax.ShapeDtypeStruct((T, cvw), F32)
        out_shape = (full, full, jax.ShapeDtypeStruct((nh, 1, LANE), F32))
        out_specs = (blk, blk, pl.BlockSpec((None, 1, LANE), lambda h, i: (h, 0, 0)))
    return pl.pallas_call(body, name=name, grid=(nh, nblk), out_shape=out_shape, in_specs=in_specs,
                          out_specs=out_specs, compiler_params=_cp("parallel", "arbitrary"))(*ins)


def conv3_bwd(dy, src, cw, nc, R, W, prod_cols=None, col0=0, out_dtype=F32, name=""):
    T, C = dy.shape
    nblk, ncb = T // R, nc // R
    prod = prod_cols is not None

    def body(*refs):
        if prod:
            (d_ref, dp_ref, dn_ref, c_ref, v_ref, cp_ref, vp_ref, cn_ref, vn_ref, w_ref,
             dc_ref, dv_ref, dw_ref) = refs
        else:
            d_ref, dp_ref, dn_ref, p_ref, pp_ref, pn_ref, w_ref, o_ref, dw_ref = refs
        i = pl.program_id(1)
        has_prev, has_next = _edge_flags(i, ncb, nblk)

        @pl.when(i == 0)
        def _():
            dw_ref[...] = jnp.zeros_like(dw_ref)

        d = d_ref[...]
        d_up = _shift_up(d, dp_ref[7:8, :] * has_prev)
        d_dn = _shift_dn(d, dn_ref[0:1, :] * has_next)
        dp = w_ref[0:1, :] * d_dn + w_ref[1:2, :] * d + w_ref[2:3, :] * d_up
        if prod:
            c, v = c_ref[...], v_ref[...]
            p = c * v
            p_prev, p_next = cp_ref[7:8, :] * vp_ref[7:8, :] * has_prev, cn_ref[0:1, :] * vn_ref[0:1, :] * has_next
            dc_ref[...] = dp * v
            dv_ref[...] = dp * c
        else:
            p = p_ref[...]
            p_prev, p_next = pp_ref[7:8, :] * has_prev, pn_ref[0:1, :] * has_next
            o_ref[...] = dp.astype(out_dtype)
        dw_ref[0:1, :] += jnp.sum(_shift_up(p, p_prev) * d, axis=0, keepdims=True)
        dw_ref[1:2, :] += jnp.sum(p * d, axis=0, keepdims=True)
        dw_ref[2:3, :] += jnp.sum(_shift_dn(p, p_next) * d, axis=0, keepdims=True)

    blk = pl.BlockSpec((R, W), lambda j, i: (i, j))
    dprev, dnext = _halo_specs(R, W, T, lambda j, i: j)
    wspec = pl.BlockSpec((3, W), lambda j, i: (0, j))
    dwspec = pl.BlockSpec((8, W), lambda j, i: (0, j))
    dwshape = jax.ShapeDtypeStruct((8, C), F32)
    if prod:
        ccol, vcol = prod_cols
        cprev, cnext = _halo_specs(R, W, T, lambda j, i: ccol + j)
        vprev, vnext = _halo_specs(R, W, T, lambda j, i: vcol + j)
        in_specs = [blk, dprev, dnext, pl.BlockSpec((R, W), lambda j, i: (i, ccol + j)),
                    pl.BlockSpec((R, W), lambda j, i: (i, vcol + j)), cprev, vprev, cnext, vnext, wspec]
        ins = [dy, dy, dy] + [src] * 6 + [cw]
        full = jax.ShapeDtypeStruct((T, C), F32)
        out_shape, out_specs = (full, full, dwshape), (blk, blk, dwspec)
    else:
        sprev, snext = _halo_specs(R, W, T, lambda j, i: col0 + j)
        in_specs = [blk, dprev, dnext, pl.BlockSpec((R, W), lambda j, i: (i, col0 + j)), sprev, snext,
                    pl.BlockSpec((3, W), lambda j, i: (0, col0 + j))]
        ins = [dy, dy, dy, src, src, src, cw]
        out_shape, out_specs = (jax.ShapeDtypeStruct((T, C), out_dtype), dwshape), (blk, dwspec)
    return pl.pallas_call(body, name=name, grid=(C // W, nblk), out_shape=out_shape, in_specs=in_specs,
                          out_specs=out_specs, compiler_params=_cp("parallel", "arbitrary"))(*ins)


def ffn_mid(uf, cw, cb, nc, R, W, da=None, name=""):
    T, C = uf.shape
    F = C // 2
    nblk, ncb, nj = T // R, nc // R, F // W
    bwd = da is not None

    def body(g_ref, v_ref, gp_ref, vp_ref, gn_ref, vn_ref, wg_ref, wv_ref, bg_ref, bv_ref, *rest):
        i = pl.program_id(1)
        has_prev, has_next = _edge_flags(i, ncb, nblk)
        yg = _conv3(g_ref[...], gp_ref[7:8, :] * has_prev, gn_ref[0:1, :] * has_next, wg_ref) + bg_ref[...]
        yv = _conv3(v_ref[...], vp_ref[7:8, :] * has_prev, vn_ref[0:1, :] * has_next, wv_ref) + bv_ref[...]
        sg = jax.nn.sigmoid(yg)
        if not bwd:
            rest[0][...] = (yg * sg * yv).astype(BF16)
            return
        da_ref, dyg_ref, dyv_ref, dbg_ref, dbv_ref = rest

        @pl.when(i == 0)
        def _():
            dbg_ref[...] = jnp.zeros_like(dbg_ref)
            dbv_ref[...] = jnp.zeros_like(dbv_ref)

        d = da_ref[...]
        dyg = d * yv * (sg * (1.0 + yg * (1.0 - sg)))
        dyv = d * (yg * sg)
        dyg_ref[...] = dyg
        dyv_ref[...] = dyv
        dbg_ref[...] += jnp.sum(dyg, axis=0, keepdims=True)
        dbv_ref[...] += jnp.sum(dyv, axis=0, keepdims=True)

    gblk = pl.BlockSpec((R, W), lambda j, i: (i, j))
    vblk = pl.BlockSpec((R, W), lambda j, i: (i, nj + j))
    gprev, gnext = _halo_specs(R, W, T, lambda j, i: j)
    vprev, vnext = _halo_specs(R, W, T, lambda j, i: nj + j)
    in_specs = [gblk, vblk, gprev, vprev, gnext, vnext,
                pl.BlockSpec((3, W), lambda j, i: (0, j)), pl.BlockSpec((3, W), lambda j, i: (0, nj + j)),
                pl.BlockSpec((1, W), lambda j, i: (0, j)), pl.BlockSpec((1, W), lambda j, i: (0, nj + j))]
    ins = [uf] * 6 + [cw, cw, cb, cb]
    if not bwd:
        out_shape, out_specs = jax.ShapeDtypeStruct((T, F), BF16), gblk
    else:
        in_specs.append(gblk)
        ins.append(da)
        half, bias = jax.ShapeDtypeStruct((T, F), F32), jax.ShapeDtypeStruct((1, F), F32)
        bspec = pl.BlockSpec((1, W), lambda j, i: (0, j))
        out_shape, out_specs = (half, half, bias, bias), (gblk, gblk, bspec, bspec)
    return pl.pallas_call(body, name=name, grid=(nj, nblk), out_shape=out_shape, in_specs=in_specs,
                          out_specs=out_specs, compiler_params=_cp("parallel", "arbitrary"))(*ins)


WEIGHTS = ("c_ctx", "w_ada", "b_ada", "ln1_w", "ln2_w", "w_in", "hg_lb_logits", "hg_norm_w", "na_q_norm_w",
           "na_k_norm_w", "na_rpb", "na_out_norm_w", "cv_w", "cv_out_norm_w", "w_out", "w_up", "ffn_conv_w",
           "ffn_conv_b", "w_down")
BIG = ("w_ada", "w_in", "w_out", "w_up", "w_down")
SHARDED_SMALL = ("hg_lb_logits", "cv_w", "ffn_conv_w")


def _flat_rows(parts, dtype):
    flat, layout, off = [], [], 0
    for p in parts:
        layout.append((off, p.shape))
        flat.append(p.reshape(-1).astype(dtype))
        off += p.size
    pad = (-off) % (8 * LANE)
    if pad:
        flat.append(jnp.zeros((pad,), dtype))
    return jnp.concatenate(flat).reshape(-1, LANE), layout


def _unflat(buf, layout):
    v = buf.reshape(-1)
    return [v[off:off + int(np.prod(shape))].reshape(shape) for off, shape in layout]


def _lb_all(logits):
    sm = jax.nn.softmax(logits.astype(F32), axis=1)
    return jnp.cumsum(sm, axis=1) - sm[:, :1]


def _seg(ctx_vec, lat_vec):
    return jnp.broadcast_to(jnp.stack([ctx_vec, lat_vec])[:, None, :], (2, RT, ctx_vec.shape[0]))


def _shared(vec):
    return jnp.broadcast_to(vec[None, None, :], (1, RT, vec.shape[0]))


def kernel(x, c, ctx, c_ctx, w_ada, b_ada, ln1_w, ln2_w, w_in, hg_lb_logits, hg_norm_w, na_q_norm_w, na_k_norm_w, na_rpb, na_out_norm_w, cv_w, cv_out_norm_w, w_out, w_up, ffn_conv_w, ffn_conv_b, w_down, loss_target, m_c_ctx, m_w_ada, m_b_ada, m_ln1_w, m_ln2_w, m_w_in, m_hg_lb_logits, m_hg_norm_w, m_na_q_norm_w, m_na_k_norm_w, m_na_rpb, m_na_out_norm_w, m_cv_w, m_cv_out_norm_w, m_w_out, m_w_up, m_ffn_conv_w, m_ffn_conv_b, m_w_down, v_c_ctx, v_w_ada, v_b_ada, v_ln1_w, v_ln2_w, v_w_in, v_hg_lb_logits, v_hg_norm_w, v_na_q_norm_w, v_na_k_norm_w, v_na_rpb, v_na_out_norm_w, v_cv_w, v_cv_out_norm_w, v_w_out, v_w_up, v_ffn_conv_w, v_ffn_conv_b, v_w_down):
    W = dict(c_ctx=c_ctx, w_ada=w_ada, b_ada=b_ada, ln1_w=ln1_w, ln2_w=ln2_w, w_in=w_in, hg_lb_logits=hg_lb_logits,
             hg_norm_w=hg_norm_w, na_q_norm_w=na_q_norm_w, na_k_norm_w=na_k_norm_w, na_rpb=na_rpb,
             na_out_norm_w=na_out_norm_w, cv_w=cv_w, cv_out_norm_w=cv_out_norm_w, w_out=w_out, w_up=w_up,
             ffn_conv_w=ffn_conv_w, ffn_conv_b=ffn_conv_b, w_down=w_down)
    Mo = dict(c_ctx=m_c_ctx, w_ada=m_w_ada, b_ada=m_b_ada, ln1_w=m_ln1_w, ln2_w=m_ln2_w, w_in=m_w_in,
              hg_lb_logits=m_hg_lb_logits, hg_norm_w=m_hg_norm_w, na_q_norm_w=m_na_q_norm_w,
              na_k_norm_w=m_na_k_norm_w, na_rpb=m_na_rpb, na_out_norm_w=m_na_out_norm_w, cv_w=m_cv_w,
              cv_out_norm_w=m_cv_out_norm_w, w_out=m_w_out, w_up=m_w_up, ffn_conv_w=m_ffn_conv_w,
              ffn_conv_b=m_ffn_conv_b, w_down=m_w_down)
    Vo = dict(c_ctx=v_c_ctx, w_ada=v_w_ada, b_ada=v_b_ada, ln1_w=v_ln1_w, ln2_w=v_ln2_w, w_in=v_w_in,
              hg_lb_logits=v_hg_lb_logits, hg_norm_w=v_hg_norm_w, na_q_norm_w=v_na_q_norm_w,
              na_k_norm_w=v_na_k_norm_w, na_rpb=v_na_rpb, na_out_norm_w=v_na_out_norm_w, cv_w=v_cv_w,
              cv_out_norm_w=v_cv_out_norm_w, w_out=v_w_out, w_up=v_w_up, ffn_conv_w=v_ffn_conv_w,
              ffn_conv_b=v_ffn_conv_b, w_down=v_w_down)

    xi, yi, ci = _me()
    chip = 2 * xi + yi
    dev = 2 * chip + ci
    L, D = x.shape[1], x.shape[2]
    NC = ctx.shape[1]
    T = NC + L
    depth = w_in.shape[0]
    HGW, NAW, CVW = 4 * hg_lb_logits.shape[-1], na_out_norm_w.shape[-1], cv_out_norm_w.shape[-1]
    MIX = HGW + NAW + CVW
    INW, FF2 = 4 * w_in.shape[-1], 4 * w_up.shape[-1]
    F = FF2 // 2
    ADA = 4 * w_ada.shape[-1]
    assert NAW == 2 * HGW and INW == 5 * HGW + 3 * NAW + 3 * CVW and ADA == 6 * D and NC % 128 == 0
    assert L % GRID_W == 0 and T % CHUNK == 0 and depth == 2
    R = math.gcd(NC, 256)
    FW = 512 if F % 512 == 0 else LANE
    rows = L // GRID_W
    nh_hg, nh_na, nh_cv = HGW // LANE, NAW // LANE, CVW // LANE
    kcol = 3 * nh_hg
    vcol = kcol + nh_na
    gcol = vcol + nh_na + nh_hg
    qcol = gcol + nh_hg
    bcol = qcol + nh_na
    mix_na, mix_cv = nh_hg, nh_hg + nh_na

    small1, lay1 = _flat_rows([c[0], hg_lb_logits, cv_w, ffn_conv_w], F32)
    g1 = allgather8([small1], "gather_cond")[0]
    per_dev = [_unflat(g1[d], lay1) for d in range(8)]
    c_all = jnp.stack([p[0] for p in per_dev])
    lb_logits = jnp.concatenate([per_dev[2 * s][1] for s in range(4)], axis=-1)
    cvw_full = jnp.concatenate([per_dev[2 * s][2] for s in range(4)], axis=-1)
    fcw_full = jnp.concatenate([per_dev[2 * s][3] for s in range(4)], axis=-1)
    lb_all, lb_pull = jax.vjp(_lb_all, lb_logits)

    a16 = jnp.concatenate([c_all, c_ctx[None], jnp.zeros((7, D), F32)])
    s16 = _silu(a16)
    wcols = ADA // 4
    b_mine = lax.dynamic_slice_in_dim(b_ada, chip * wcols, wcols, axis=1)
    p_ada = jnp.stack([mm_nn(s16, w_ada[l][None], F32, f"ada_fwd_{l}") + b_mine[l][None] for l in range(depth)])
    g2 = allgather8([p_ada.reshape(depth * 16, wcols)], "gather_ada")[0].reshape(8, depth, 16, wcols)
    ada_rows = jnp.concatenate([g2[2 * s] for s in range(4)], axis=-1)
    ada = lax.dynamic_index_in_dim(ada_rows, dev, axis=1, keepdims=False)
    ada_c = ada_rows[:, 8]

    def half_rows(a):
        h = a.shape[0] // 2
        return lax.dynamic_slice_in_dim(a, ci * h, h, axis=0)

    proj = ("w_in", "w_out", "w_up", "w_down")
    wparts = [half_rows(W[n][l]).astype(BF16) for l in range(depth) for n in proj]
    wg = allgather8(wparts, "gather_weights", hbm=True)
    Wg = [{n: wg[l * 4 + j].reshape(4, -1, wg[l * 4 + j].shape[-1]) for j, n in enumerate(proj)} for l in range(depth)]
    for l in range(depth):
        Wg[l]["w_out"] = Wg[l]["w_out"].reshape(1, MIX, D)
        Wg[l]["w_down"] = Wg[l]["w_down"].reshape(1, F, D)

    xcat = jnp.concatenate([ctx[0], x[0]], axis=0)
    mods = []
    for l in range(depth):
        lat, con = jnp.split(ada[l], 6), jnp.split(ada_c[l], 6)
        mods.append(dict(sh1=_seg(con[0], lat[0]), sc1=_seg(con[1], lat[1]), g1=_seg(con[2], lat[2]),
                         sh2=_seg(con[3], lat[3]), sc2=_seg(con[4], lat[4]), g2=_seg(con[5], lat[5]),
                         ln1=_shared(ln1_w[l]), ln2=_shared(ln2_w[l])))
    bias_pull, saved = [], []
    x0 = xcat
    _, h = gate_norm(x0, None, None, mods[0]["ln1"], mods[0]["sh1"], mods[0]["sc1"], NC, R, "norm_in")
    for l in range(depth):
        md, wl = mods[l], Wg[l]
        u = mm_nn(h, wl["w_in"], F32, f"proj_in_{l}")
        lbf, lbb = lb_all[0, l][None], lb_all[1, l][None]
        o_fw, st_fw = hgrn_fwd(u, lbf, False, 0, NC, HGW, f"hgrn_fw_{l}")
        o_bw, st_bw = hgrn_fwd(u, lbb, True, 1, NC, HGW, f"hgrn_bw_{l}")
        hgn = hg_norm_w[l][None]
        hg = hg_read(o_fw, o_bw, u, hgn, gcol, R, f"hg_read_{l}")
        bias, pull = jax.vjp(lambda r: na_bias_tables(r, rows), na_rpb[l])
        bias_pull.append(pull)
        qn, kn, on = na_q_norm_w[l][None], na_k_norm_w[l][None], na_out_norm_w[l][None]
        keys_n, vals_b = kv_prep(u, kn, kcol, vcol, NAW, R, f"kv_prep_{l}")
        na = na_fwd(u, keys_n, vals_b, qn, on, bias, qcol, NC, f"na_fwd_{l}")
        cvw_l, cvo = cvw_full[l], cv_out_norm_w[l][None]
        cv = short_conv(u, cvw_l, cvo, bcol, NC, CVW, R, name=f"short_conv_{l}")
        mix = jnp.concatenate([hg, na, cv], axis=1)
        m1 = mm_nn(mix, wl["w_out"], F32, f"proj_out_{l}")
        x1, h2 = gate_norm(x0, m1, md["g1"], md["ln2"], md["sh2"], md["sc2"], NC, R, f"gate_norm_mid_{l}")
        uf = mm_nn(h2, wl["w_up"], F32, f"ffn_up_{l}")
        fcw_l, fcb_l = fcw_full[l], ffn_conv_b[l][None]
        a = ffn_mid(uf, fcw_l, fcb_l, NC, R, FW, name=f"ffn_mid_{l}")
        m2 = mm_nn(a, wl["w_down"], F32, f"ffn_down_{l}")
        saved.append(dict(x0=x0, h=h, u=u, o_fw=o_fw, o_bw=o_bw, st_fw=st_fw, st_bw=st_bw, bias=bias, mix=mix,
                          m1=m1, x1=x1, h2=h2, uf=uf, a=a, m2=m2, lbf=lbf, lbb=lbb, keys_n=keys_n, vals_b=vals_b))
        if l + 1 < depth:
            nx = mods[l + 1]
            x0, h = gate_norm(x1, m2, md["g2"], nx["ln1"], nx["sh1"], nx["sc1"], NC, R, f"gate_norm_end_{l}")
    sv, md = saved[-1], mods[-1]
    loss_terms, d_x1, d_m2, d_g2 = gate_loss(sv["x1"], sv["m2"], md["g2"], loss_target[0], NC, R, "gate_loss")
    loss = lax.psum(jnp.sum(loss_terms), ("x", "y", "c"))

    big_grads = [dict() for _ in range(depth)]
    small = [dict() for _ in range(depth)]
    d_ada = [None] * depth
    d_lb = [None] * depth
    for l in reversed(range(depth)):
        sv, md, wl = saved[l], mods[l], Wg[l]
        u, uf = sv["u"], sv["uf"]
        big_grads[l]["w_down"] = mm_tn(sv["a"], d_m2, 1, BF16, f"grad_w_down_{l}").reshape(4, F // 4, D)
        d_a = mm_nt(d_m2, wl["w_down"], F32, f"ffn_down_bwd_{l}")
        fcw_l, fcb_l = fcw_full[l], ffn_conv_b[l][None]
        dyg, dyv, dbg, dbv = ffn_mid(uf, fcw_l, fcb_l, NC, R, FW, da=d_a, name=f"ffn_mid_bwd_{l}")
        dug, dwg = conv3_bwd(dyg, uf, fcw_l, NC, R, FW, col0=0, out_dtype=BF16, name=f"ffn_conv_bwd_gate_{l}")
        duv, dwv = conv3_bwd(dyv, uf, fcw_l, NC, R, FW, col0=F // FW, out_dtype=BF16, name=f"ffn_conv_bwd_val_{l}")
        d_uf = jnp.concatenate([dug, duv], axis=1)
        small[l]["ffn_conv_w"] = jnp.concatenate([dwg[:3], dwv[:3]], axis=1)
        small[l]["ffn_conv_b"] = jnp.concatenate([dbg[0], dbv[0]])
        big_grads[l]["w_up"] = mm_tn(sv["h2"], d_uf, 4, BF16, f"grad_w_up_{l}")
        d_h2 = mm_nt(d_uf, wl["w_up"], F32, f"ffn_up_bwd_{l}")
        d_x0, d_m1, dg1, dln2, dsh2, dsc2 = gate_norm_bwd(sv["x0"], sv["m1"], md["g1"], md["ln2"], md["sh2"], md["sc2"],
                                                          d_x1, d_h2, NC, R, f"gate_norm_mid_bwd_{l}")
        big_grads[l]["w_out"] = mm_tn(sv["mix"], d_m1, 1, BF16, f"grad_w_out_{l}").reshape(4, MIX // 4, D)
        d_mix = mm_nt(d_m1, wl["w_out"], F32, f"proj_out_bwd_{l}")
        hgn = hg_norm_w[l][None]
        d_o, d_hgg, d_hgn = hg_read_bwd(sv["o_fw"], sv["o_bw"], u, hgn, d_mix, gcol, 0, R, f"hg_read_bwd_{l}")
        dzf, dvf, dqf, dlbf = hgrn_bwd(u, sv["lbf"], sv["st_fw"], d_o, False, 0, NC, HGW, f"hgrn_fw_bwd_{l}")
        dzb, dvb, dqb, dlbb = hgrn_bwd(u, sv["lbb"], sv["st_bw"], d_o, True, 1, NC, HGW, f"hgrn_bw_bwd_{l}")
        d_lb[l] = (dlbf[0], dlbb[0])
        qn, kn, on = na_q_norm_w[l][None], na_k_norm_w[l][None], na_out_norm_w[l][None]
        d_nq, d_keys_n, d_nv, d_bias, d_qn, d_on = na_bwd(u, sv["keys_n"], sv["vals_b"], qn, on, sv["bias"], d_mix, qcol,
                                                         mix_na, NC, f"na_bwd_{l}")
        d_nk, d_kn = kv_prep_bwd(u, kn, d_keys_n, kcol, NAW, R, f"kv_prep_bwd_{l}")
        cvw_l, cvo = cvw_full[l], cv_out_norm_w[l][None]
        d_cb, d_cy, d_cvo = short_conv(u, cvw_l, cvo, bcol, NC, CVW, R, bwd_dout=d_mix, ocol=mix_cv,
                                       name=f"short_conv_bwd_{l}")
        d_cc, d_cvv, d_cvw = conv3_bwd(d_cy, u, cvw_l, NC, R, LANE, prod_cols=(bcol + nh_cv, bcol + 2 * nh_cv),
                                       name=f"short_conv_taps_bwd_{l}")
        d_u = jnp.concatenate([dzf, dzb, dvf + dvb, d_nk, d_nv, dqf + dqb, d_hgg, d_nq, d_cb, d_cc, d_cvv],
                              axis=1).astype(BF16)
        big_grads[l]["w_in"] = mm_tn(sv["h"], d_u, 4, BF16, f"grad_w_in_{l}")
        d_h = mm_nt(d_u, wl["w_in"], F32, f"proj_in_bwd_{l}")
        small[l].update(hg_norm_w=d_hgn.sum(0)[0], na_q_norm_w=d_qn.sum(0)[0], na_k_norm_w=d_kn.sum(0)[0],
                        na_out_norm_w=d_on.reshape(-1), na_rpb=bias_pull[l](d_bias)[0], cv_w=d_cvw[:3],
                        cv_out_norm_w=d_cvo.reshape(-1), ln2_w=dln2.sum((0, 1)))
        if l > 0:
            pv, pm = saved[l - 1], mods[l - 1]
            d_x1, d_m2, dg2_prev, dln1, dsh1, dsc1 = gate_norm_bwd(pv["x1"], pv["m2"], pm["g2"], md["ln1"], md["sh1"],
                                                                   md["sc1"], d_x0, d_h, NC, R, f"gate_norm_end_bwd_{l - 1}")
        else:
            d_xin, _, _, dln1, dsh1, dsc1 = gate_norm_bwd(sv["x0"], None, None, md["ln1"], md["sh1"], md["sc1"], d_x0, d_h,
                                                          NC, R, "norm_in_bwd")
        small[l]["ln1_w"] = dln1.sum((0, 1))
        this_g2 = d_g2
        vecs = [v.sum(1) for v in (dsh1, dsc1, dg1, dsh2, dsc2, this_g2)]
        d_ada[l] = jnp.stack([jnp.concatenate([v[s] for v in vecs]) for s in (0, 1)])
        if l > 0:
            d_g2 = dg2_prev
    grad_x = d_xin[NC:][None]
    d_logits = lb_pull(jnp.stack([jnp.stack([d_lb[l][k] for l in range(depth)]) for k in (0, 1)]))[0]

    rep_names = ("ln1_w", "ln2_w", "hg_norm_w", "na_q_norm_w", "na_k_norm_w", "na_rpb", "na_out_norm_w",
                 "cv_out_norm_w", "ffn_conv_b", "cv_w", "ffn_conv_w")
    parts3 = [jnp.stack([small[l][n] for l in range(depth)]) for n in rep_names]
    parts3 += [d_logits, jnp.stack([d_ada[l][0] for l in range(depth)]), jnp.stack([d_ada[l][1] for l in range(depth)])]
    buf3, lay3 = _flat_rows(parts3, F32)
    g3 = allgather8([buf3], "gather_small_grads")[0]
    tot3 = _unflat(sum_leading(g3, F32, "sum_small_grads"), lay3)
    gsm = dict(zip(rep_names, tot3[:len(rep_names)]))
    gsm["hg_lb_logits"] = tot3[len(rep_names)]
    dctx_tot, dlat_tot = tot3[-2], tot3[-1]
    dlat_each = jnp.stack([_unflat(g3[d], lay3)[-1] for d in range(8)], axis=1)
    grads = {n: gsm[n].reshape(W[n].shape) for n in rep_names if n not in SHARDED_SMALL}
    for n in SHARDED_SMALL:
        wl_ = W[n].shape[-1]
        grads[n] = lax.dynamic_slice_in_dim(gsm[n], chip * wl_, wl_, axis=gsm[n].ndim - 1)
    grads["b_ada"] = dctx_tot + dlat_tot

    ds16 = jnp.zeros((16, D), F32)
    gw_ada = []
    for l in range(depth):
        dm = jnp.concatenate([dlat_each[l], dctx_tot[l][None], jnp.zeros((7, ADA), F32)])
        dm = lax.dynamic_slice_in_dim(dm, chip * wcols, wcols, axis=1)
        gw_ada.append(mm_tn(s16, dm, 1, F32, f"grad_w_ada_{l}")[0])
        ds16 = ds16 + mm_nt(dm, w_ada[l][None], F32, f"ada_bwd_{l}")
    g4 = allgather8([ds16[8:16]], "gather_cond_grad")[0]
    d_scc = g4[0, 0] + g4[2, 0] + g4[4, 0] + g4[6, 0]
    sg = jax.nn.sigmoid(c_ctx)
    grads["c_ctx"] = d_scc * (sg * (1.0 + c_ctx * (1.0 - sg)))

    keys = [(l, n) for l in range(depth) for n in proj]
    parts = [big_grads[l][n] for l, n in keys]
    got = swap_halves(parts, "reduce_sibling")
    core = ci.astype(jnp.int32).reshape(1)
    pairs = [pair_sum(p, g, core, f"reduce_pair_sum_{n}_{l}") for (l, n), p, g in zip(keys, parts, got)]
    quads = chip_alltoall(pairs, "reduce_chips")
    mine = [sum_leading(q, F32, f"reduce_chip_sum_{n}_{l}") for (l, n), q in zip(keys, quads)]
    other = share_halves(mine, "reduce_share")
    mine_by, other_by = {n: [None] * depth for n in proj}, {n: [None] * depth for n in proj}
    for (l, n), a, b in zip(keys, mine, other):
        mine_by[n][l], other_by[n][l] = a, b

    delta, new_m, new_v = {}, {}, {}
    for n in BIG:
        shp = W[n].shape
        two = lambda a: a.reshape(-1, shp[-1])
        if n == "w_ada":
            g_, d_, m_, v_ = adamw(two(W[n]), gw_ada, two(Mo[n]), two(Vo[n]), f"adamw_{n}")
        else:
            g_, d_, m_, v_ = adamw_halves(two(W[n]), mine_by[n], other_by[n], two(Mo[n]), two(Vo[n]), core, f"adamw_{n}")
        grads[n], delta[n], new_m[n], new_v[n] = g_.reshape(shp), d_.reshape(shp), m_.reshape(shp), v_.reshape(shp)
    smalls = [n for n in WEIGHTS if n not in BIG]
    pw, lay_s = _flat_rows([W[n] for n in smalls], F32)
    pg, _ = _flat_rows([grads[n] for n in smalls], F32)
    pm, _ = _flat_rows([Mo[n] for n in smalls], F32)
    pvv, _ = _flat_rows([Vo[n] for n in smalls], F32)
    _, d_, m_, v_ = adamw(pw, [pg], pm, pvv, "adamw_small")
    for n, dd, mm_, vv in zip(smalls, _unflat(d_, lay_s), _unflat(m_, lay_s), _unflat(v_, lay_s)):
        delta[n], new_m[n], new_v[n] = dd, mm_, vv

    return (loss, grad_x, *[grads[n] for n in WEIGHTS], *[delta[n] for n in WEIGHTS],
            *[new_m[n] for n in WEIGHTS], *[new_v[n] for n in WEIGHTS])
```

```python
import functools
import math

import numpy as np
import jax
import jax.numpy as jnp
from jax import lax
from jax.experimental import pallas as pl
from jax.experimental.pallas import tpu as pltpu

F32 = jnp.float32
BF16 = jnp.bfloat16
MESH = pl.DeviceIdType.MESH
ANY = pl.BlockSpec(memory_space=pl.ANY)
VMEM_SPEC = pl.BlockSpec(memory_space=pltpu.VMEM)

LANE = 128
CHUNK = 64
SUB = 16
GRID_W = 64
WIN_R = 8
WIN_C = 16
EPS = 1e-6
F_FLOOR = 1e-30
NEG_INF = -1e30
EXP_CLAMP = 80.0
ATTN_SCALE = LANE ** -0.5
VMEM_LIMIT = 56 * 1024 * 1024
ADAM_LR, ADAM_B1, ADAM_B2, ADAM_EPS, ADAM_WD, ADAM_STEP = 0.001, 0.9, 0.999, 1e-08, 0.01, 10


def _cp(*sem):
    return pltpu.CompilerParams(dimension_semantics=sem or None, vmem_limit_bytes=VMEM_LIMIT)


def _me():
    return lax.axis_index("x"), lax.axis_index("y"), lax.axis_index("c")


def allgather8(blocks, name, hbm=False):
    na = len(blocks)
    comm = allgather8_comm(blocks)

    def body(*refs):
        comm["start"](refs[:na], refs[na:2 * na], refs[2 * na:])
        comm["finish"](refs[:na], refs[na:2 * na], refs[2 * na:])

    spec = ANY if hbm else VMEM_SPEC
    return pl.pallas_call(
        body, name=name, out_shape=comm["outs"], in_specs=[spec] * na, out_specs=[spec] * na,
        scratch_shapes=comm["scratch"], compiler_params=pltpu.CompilerParams(vmem_limit_bytes=VMEM_LIMIT),
    )(*blocks)


def allgather8_comm(blocks):
    na = len(blocks)

    def parts(x_refs, out_refs, sems):
        send_sems, recv_sems, local_sems = sems
        x, y, c = _me()
        me, sibling = (x, y, c), (x, y, 1 - c)
        chips = [(1 - x, y), (x, 1 - y), (1 - x, 1 - y)]

        def rows(a, px, py, pc):
            return out_refs[a].at[4 * px + 2 * py + pc]

        def copy(a, k, blk, to, src=None):
            return pltpu.make_async_remote_copy(
                src_ref=rows(a, *blk) if src is None else src, dst_ref=rows(a, *blk),
                send_sem=send_sems.at[a, k], recv_sem=recv_sems.at[a, k], device_id=to, device_id_type=MESH)

        mine = [pltpu.make_async_copy(x_refs[a], rows(a, *me), local_sems.at[a]) for a in range(na)]
        first = []
        for a in range(na):
            first.append(copy(a, 0, me, sibling, src=x_refs[a]))
            first += [copy(a, 1 + j, me, (*chip, c), src=x_refs[a]) for j, chip in enumerate(chips)]
        return c, me, sibling, chips, copy, mine, first

    def start(x_refs, out_refs, sems):
        _, _, _, _, _, mine, first = parts(x_refs, out_refs, sems)
        for cp in mine + first:
            cp.start()

    def finish(x_refs, out_refs, sems):
        c, me, sibling, chips, copy, mine, first = parts(x_refs, out_refs, sems)
        passed = []
        for j, chip in enumerate(chips):
            for a in range(na):
                copy(a, 1 + j, (*chip, c), me).wait_recv()
                passed.append(copy(a, 4 + j, (*chip, c), sibling))
                passed[-1].start()
        for a in range(na):
            copy(a, 0, sibling, me).wait_recv()
            for j, chip in enumerate(chips):
                copy(a, 4 + j, (*chip, 1 - c), me).wait_recv()
        for cp in first + passed:
            cp.wait_send()
        for cp in mine:
            cp.wait()

    return dict(ins=list(blocks), outs=[jax.ShapeDtypeStruct((8,) + b.shape, b.dtype) for b in blocks],
                scratch=[pltpu.SemaphoreType.DMA((na, 7)), pltpu.SemaphoreType.DMA((na, 7)),
                         pltpu.SemaphoreType.DMA((na,))], start=start, finish=finish)


def swap_halves(gs, name):
    na = len(gs)
    hrs = [g.shape[1] // 2 for g in gs]

    def body(*refs):
        g_refs, o_refs = refs[:na], refs[na:2 * na]
        send_sems, recv_sems = refs[2 * na:]
        x, y, c = _me()
        cps = []
        for a in range(na):
            for s in range(4):
                src = g_refs[a].at[s, pl.ds(pl.multiple_of((1 - c) * hrs[a], 16), hrs[a]), :]
                cps.append(pltpu.make_async_remote_copy(
                    src_ref=src, dst_ref=o_refs[a].at[s], send_sem=send_sems.at[a, s], recv_sem=recv_sems.at[a, s],
                    device_id=(x, y, 1 - c), device_id_type=MESH))
        for cp in cps:
            cp.start()
        for cp in cps:
            cp.wait()

    return pl.pallas_call(
        body, name=name, out_shape=[jax.ShapeDtypeStruct((4, hrs[a], gs[a].shape[2]), gs[a].dtype) for a in range(na)],
        in_specs=[ANY] * na, out_specs=[ANY] * na,
        scratch_shapes=[pltpu.SemaphoreType.DMA((na, 4)), pltpu.SemaphoreType.DMA((na, 4))],
    )(*gs)


def chip_alltoall(gs, name):
    na = len(gs)
    comm = chip_alltoall_comm(gs)

    def body(*refs):
        comm["start"](refs[:na], refs[na:2 * na], refs[2 * na:])
        comm["finish"](refs[:na], refs[na:2 * na], refs[2 * na:])

    return pl.pallas_call(body, name=name, out_shape=comm["outs"], in_specs=[ANY] * na, out_specs=[ANY] * na,
                          scratch_shapes=comm["scratch"])(*gs)


def chip_alltoall_comm(gs):
    na = len(gs)

    def copies(g_refs, o_refs, sems):
        send_sems, recv_sems, local_sems = sems
        x, y, c = _me()
        mine = 2 * x + y
        cps = []
        for a in range(na):
            cps.append(pltpu.make_async_copy(g_refs[a].at[mine], o_refs[a].at[mine], local_sems.at[a]))
            for k, (px, py) in enumerate([(1 - x, y), (x, 1 - y), (1 - x, 1 - y)]):
                cps.append(pltpu.make_async_remote_copy(
                    src_ref=g_refs[a].at[2 * px + py], dst_ref=o_refs[a].at[mine], send_sem=send_sems.at[a, k],
                    recv_sem=recv_sems.at[a, k], device_id=(px, py, c), device_id_type=MESH))
        return cps

    def start(g_refs, o_refs, sems):
        for cp in copies(g_refs, o_refs, sems):
            cp.start()

    def finish(g_refs, o_refs, sems):
        for cp in copies(g_refs, o_refs, sems):
            cp.wait()

    return dict(ins=list(gs), outs=[jax.ShapeDtypeStruct(g.shape, g.dtype) for g in gs],
                scratch=[pltpu.SemaphoreType.DMA((na, 3)), pltpu.SemaphoreType.DMA((na, 3)),
                         pltpu.SemaphoreType.DMA((na,))], start=start, finish=finish)


def share_halves(vs, name):
    na = len(vs)

    def body(*refs):
        v_refs, o_refs = refs[:na], refs[na:2 * na]
        send_sems, recv_sems = refs[2 * na:]
        x, y, c = _me()
        cps = [pltpu.make_async_remote_copy(
            src_ref=v_refs[a], dst_ref=o_refs[a], send_sem=send_sems.at[a], recv_sem=recv_sems.at[a],
            device_id=(x, y, 1 - c), device_id_type=MESH) for a in range(na)]
        for cp in cps:
            cp.start()
        for cp in cps:
            cp.wait()

    return pl.pallas_call(
        body, name=name, out_shape=[jax.ShapeDtypeStruct(v.shape, v.dtype) for v in vs],
        in_specs=[ANY] * na, out_specs=[ANY] * na,
        scratch_shapes=[pltpu.SemaphoreType.DMA((na,)), pltpu.SemaphoreType.DMA((na,))],
    )(*vs)


def _row_block(rows, cap):
    rb = math.gcd(rows, cap)
    return rb if rb % 8 == 0 else rows


def sum_leading(x, out_dtype, name):
    n, r, c = x.shape
    rb = _row_block(r, 1024)

    def body(x_ref, o_ref):
        acc = x_ref[0].astype(F32)
        for k in range(1, n):
            acc = acc + x_ref[k].astype(F32)
        o_ref[...] = acc.astype(o_ref.dtype)

    return pl.pallas_call(
        body, name=name, grid=(r // rb,), out_shape=jax.ShapeDtypeStruct((r, c), out_dtype),
        in_specs=[pl.BlockSpec((n, rb, c), lambda i: (0, i, 0))], out_specs=pl.BlockSpec((rb, c), lambda i: (i, 0)),
        compiler_params=_cp("parallel"),
    )(x)


def pair_sum(g, got, core, name):
    _, r2, n = g.shape
    hr = r2 // 2
    rb = math.gcd(hr, 512)
    nb = hr // rb

    def body(c_ref, a_ref, b_ref, o_ref):
        o_ref[...] = (a_ref[...].astype(F32) + b_ref[...].astype(F32)).astype(o_ref.dtype)

    spec = pl.BlockSpec((None, rb, n), lambda s, i, c_ref: (s, i, 0))
    return pl.pallas_call(
        body, name=name, out_shape=jax.ShapeDtypeStruct((4, hr, n), g.dtype),
        grid_spec=pltpu.PrefetchScalarGridSpec(
            num_scalar_prefetch=1, grid=(4, nb),
            in_specs=[pl.BlockSpec((None, rb, n), lambda s, i, c_ref: (s, c_ref[0] * nb + i, 0)), spec],
            out_specs=spec),
        compiler_params=_cp("parallel", "parallel"),
    )(core, g, got)


def adamw(w, gs, m, v, name):
    rows, c = w.shape
    ng = len(gs)
    r = rows // ng
    rb = _row_block(r, 128 if c > 2048 else 256 if c > 1024 else 1024)
    nb = r // rb
    bc1 = 1.0 - ADAM_B1 ** ADAM_STEP
    bc2 = 1.0 - ADAM_B2 ** ADAM_STEP

    def body(w_ref, *refs):
        g_refs, (m_ref, v_ref, g_out, d_ref, nm_ref, nv_ref) = refs[:ng], refs[ng:]
        part = pl.program_id(0) // nb
        gg = g_refs[0][...]
        for k in range(1, ng):
            gg = jnp.where(part == k, g_refs[k][...], gg)
        nm = ADAM_B1 * m_ref[...] + (1.0 - ADAM_B1) * gg
        nv = ADAM_B2 * v_ref[...] + (1.0 - ADAM_B2) * (gg * gg)
        g_out[...] = gg
        d_ref[...] = -ADAM_LR * ((nm / bc1) / (jnp.sqrt(nv / bc2) + ADAM_EPS) + ADAM_WD * w_ref[...])
        nm_ref[...] = nm
        nv_ref[...] = nv

    spec = pl.BlockSpec((rb, c), lambda i: (i, 0))
    gspecs = [pl.BlockSpec((rb, c), functools.partial(lambda k, i: (jnp.clip(i - k * nb, 0, nb - 1), 0), k))
              for k in range(ng)]
    sds = jax.ShapeDtypeStruct((rows, c), F32)
    return pl.pallas_call(
        body, name=name, grid=(ng * nb,), out_shape=(sds,) * 4, in_specs=[spec] + gspecs + [spec, spec],
        out_specs=(spec,) * 4, compiler_params=_cp("parallel"),
    )(w, *gs, m, v)


def adamw_halves(w, mine, other, m, v, core, name):
    rows, c = w.shape
    nl = len(mine)
    hr = rows // (2 * nl)
    rb = _row_block(hr, 128 if c > 2048 else 256 if c > 1024 else 1024)
    nb = hr // rb
    bc1 = 1.0 - ADAM_B1 ** ADAM_STEP
    bc2 = 1.0 - ADAM_B2 ** ADAM_STEP

    def body(c_ref, w_ref, *refs):
        mine_refs, other_refs = refs[:nl], refs[nl:2 * nl]
        m_ref, v_ref, g_out, d_ref, nm_ref, nv_ref = refs[2 * nl:]
        part = pl.program_id(0) // nb
        layer, half = part // 2, part % 2
        gg = jnp.where(half == c_ref[0], mine_refs[0][...], other_refs[0][...])
        for k in range(1, nl):
            gg = jnp.where(layer == k, jnp.where(half == c_ref[0], mine_refs[k][...], other_refs[k][...]), gg)
        nm = ADAM_B1 * m_ref[...] + (1.0 - ADAM_B1) * gg
        nv = ADAM_B2 * v_ref[...] + (1.0 - ADAM_B2) * (gg * gg)
        g_out[...] = gg
        d_ref[...] = -ADAM_LR * ((nm / bc1) / (jnp.sqrt(nv / bc2) + ADAM_EPS) + ADAM_WD * w_ref[...])
        nm_ref[...] = nm
        nv_ref[...] = nv

    spec = pl.BlockSpec((rb, c), lambda i, c_ref: (i, 0))
    gspecs = [pl.BlockSpec((rb, c), functools.partial(
        lambda k, i, c_ref: (jnp.clip(i - 2 * k * nb, 0, 2 * nb - 1) % nb, 0), k)) for k in range(nl)]
    sds = jax.ShapeDtypeStruct((rows, c), F32)
    return pl.pallas_call(
        body, name=name, out_shape=(sds,) * 4,
        grid_spec=pltpu.PrefetchScalarGridSpec(
            num_scalar_prefetch=1, grid=(2 * nl * nb,), in_specs=[spec] + gspecs + gspecs + [spec, spec],
            out_specs=(spec,) * 4),
        compiler_params=_cp("parallel"),
    )(core, w, *mine, *other, m, v)


def _pick(n, prefs):
    for p in prefs:
        if n % p == 0:
            return p
    return n


def _hosted(body, n_in, n_out, comm, first, last):
    if comm is None:
        return body
    k, ns = len(comm["ins"]), len(comm["scratch"])

    def wrapped(*refs):
        ins, cins = refs[:n_in], refs[n_in:n_in + k]
        outs, couts = refs[n_in + k:n_in + k + n_out], refs[n_in + k + n_out:n_in + 2 * k + n_out]
        rest = refs[n_in + 2 * k + n_out:]
        scratch, sems = rest[:len(rest) - ns], rest[len(rest) - ns:]

        @pl.when(first())
        def _():
            comm["start"](cins, couts, sems)

        body(*ins, *outs, *scratch)

        @pl.when(last())
        def _():
            comm["finish"](cins, couts, sems)

    return wrapped


def _comm_extras(comm):
    if comm is None:
        return [], [], [], []
    return list(comm["ins"]), [ANY] * len(comm["ins"]), list(comm["outs"]), list(comm["scratch"])


def _mm_body(dims, nk, out_dtype):
    def body(a_ref, b_ref, o_ref, acc=None):
        kk = pl.program_id(2)
        part = lax.dot_general(a_ref[...].astype(BF16), b_ref[...].astype(BF16), (dims, ((), ())),
                               preferred_element_type=F32)
        if nk == 1:
            o_ref[...] = part.astype(out_dtype)
        else:
            @pl.when(kk == 0)
            def _():
                acc[...] = part

            @pl.when(kk > 0)
            def _():
                acc[...] += part

            @pl.when(kk == nk - 1)
            def _():
                o_ref[...] = acc[...].astype(out_dtype)
    return body


def _acc(nk, shape):
    return [pltpu.VMEM(shape, F32)] if nk > 1 else []


def mm_nn(a, w, out_dtype, name, comm=None):
    M, K = a.shape
    S, _, Ns = w.shape
    tm = _pick(M, (1088, 1024, 512, 256, 128))
    tn = _pick(Ns, (1024, 896, 1408, 512, 256, 128))
    tk = _pick(K, (2816, 2048, 1408, 1024, 512, 256, 128))
    nps, nk = Ns // tn, K // tk
    grid = (S * nps, M // tm, nk)
    ids = lambda: [pl.program_id(d) for d in range(3)]
    first = lambda: functools.reduce(jnp.logical_and, [p == 0 for p in ids()])
    last = lambda: functools.reduce(jnp.logical_and, [p == g - 1 for p, g in zip(ids(), grid)])
    cin, cspec, cout, csem = _comm_extras(comm)
    out = pl.pallas_call(
        _hosted(_mm_body(((1,), (0,)), nk, out_dtype), 2, 1, comm, first, last), name=name, grid=grid,
        out_shape=[jax.ShapeDtypeStruct((M, S * Ns), out_dtype)] + cout,
        in_specs=[pl.BlockSpec((tm, tk), lambda j, i, k: (i, k)),
                  pl.BlockSpec((None, tk, tn), lambda j, i, k: (j // nps, k, j % nps))] + cspec,
        out_specs=[pl.BlockSpec((tm, tn), lambda j, i, k: (i, j))] + cspec,
        scratch_shapes=_acc(nk, (tm, tn)) + csem,
        compiler_params=_cp(*(("arbitrary",) * 3 if comm else ("parallel", "parallel", "arbitrary"))),
    )(a, w, *cin)
    return out[0] if comm is None else out


def mm_nt(dy, w, out_dtype, name):
    M, N = dy.shape
    S, K, Ns = w.shape
    tm = _pick(M, (1088, 1024, 512, 256, 128))
    tn = _pick(K, (1408, 1024, 512, 256, 128))
    tk = _pick(Ns, (2816, 2048, 1792, 1408, 1024, 896, 512, 256, 128))
    kps, nk = Ns // tk, N // tk
    return pl.pallas_call(
        _mm_body(((1,), (1,)), nk, out_dtype), name=name, grid=(K // tn, M // tm, nk),
        out_shape=jax.ShapeDtypeStruct((M, K), out_dtype),
        in_specs=[pl.BlockSpec((tm, tk), lambda j, i, k: (i, k)),
                  pl.BlockSpec((None, tn, tk), lambda j, i, k: (k // kps, j, k % kps))],
        out_specs=pl.BlockSpec((tm, tn), lambda j, i, k: (i, j)),
        scratch_shapes=_acc(nk, (tm, tn)), compiler_params=_cp("parallel", "parallel", "arbitrary"),
    )(dy, w)


def mm_tn(a, dy, S, out_dtype, name):
    M, K = a.shape
    N = dy.shape[1]
    Ns = N // S
    to = _pick(K, (1024, 512, 256, 128))
    tn = _pick(Ns, (2816, 2048, 1792, 1408, 1024, 896, 512, 256, 128))
    tk = _pick(M, (1088, 1024, 512, 256, 128))
    nps, nk = Ns // tn, M // tk
    return pl.pallas_call(
        _mm_body(((0,), (0,)), nk, out_dtype), name=name, grid=(K // to, S * nps, nk),
        out_shape=jax.ShapeDtypeStruct((S, K, Ns), out_dtype),
        in_specs=[pl.BlockSpec((tk, to), lambda i, j, k: (k, i)),
                  pl.BlockSpec((tk, tn), lambda i, j, k: (k, j))],
        out_specs=pl.BlockSpec((None, to, tn), lambda i, j, k: (j // nps, i, j % nps)),
        scratch_shapes=_acc(nk, (to, tn)), compiler_params=_cp("parallel", "parallel", "arbitrary"),
    )(a, dy)


_DIMS = {"nn": ((1,), (0,)), "nt": ((1,), (1,)), "tn": ((0,), (0,))}


def _dot(a, b, mode):
    return lax.dot_general(a.astype(BF16), b.astype(BF16), (_DIMS[mode], ((), ())), preferred_element_type=F32)


@functools.partial(jax.custom_vjp, nondiff_argnums=(2,))
def mmf(a, b, mode):
    return _dot(a, b, mode)


def _mmf_fwd(a, b, mode):
    return _dot(a, b, mode), (a, b)


def _mmf_bwd(mode, res, ct):
    a, b = res
    if mode == "nn":
        return _dot(ct, b, "nt"), _dot(a, ct, "tn")
    if mode == "nt":
        return _dot(ct, b, "nn"), _dot(ct, a, "tn")
    return _dot(b, ct, "nt"), _dot(a, ct, "nn")


mmf.defvjp(_mmf_fwd, _mmf_bwd)


def _dot_hi(m, g):
    return jnp.dot(m, g, precision=lax.Precision.HIGHEST, preferred_element_type=F32)


@jax.custom_vjp
def cumdot(m, mt, g):
    return _dot_hi(m, g)


def _cumdot_fwd(m, mt, g):
    return _dot_hi(m, g), (m, mt)


def _cumdot_bwd(res, ct):
    m, mt = res
    return jnp.zeros_like(m), jnp.zeros_like(mt), _dot_hi(mt, ct)


cumdot.defvjp(_cumdot_fwd, _cumdot_bwd)


def _rms(x, w):
    return x * lax.rsqrt(jnp.mean(x * x, axis=-1, keepdims=True) + EPS) * w


def _silu(x):
    return x * jax.nn.sigmoid(x)


RT = 16


def _gn_math(has_gate, x, m, gate, lnw, shift, scale):
    xn = x + gate * m if has_gate else x
    h = _rms(xn, lnw) * (1.0 + scale) + shift
    return xn, h


def _seg_spec(width, ncb):
    return pl.BlockSpec((None, RT, width), lambda i: (jnp.minimum(i // ncb, 1), 0, 0))


def gate_norm(x, m, gate, lnw, shift, scale, nc, R, name):
    T, D = x.shape
    has_gate = m is not None
    ncb = nc // R

    def body(*refs):
        if has_gate:
            x_ref, m_ref, g_ref, w_ref, sh_ref, sc_ref, xn_ref, h_ref = refs
        else:
            x_ref, w_ref, sh_ref, sc_ref, h_ref = refs

        def step(t, carry):
            rows = pl.ds(pl.multiple_of(t * RT, RT), RT)
            xn, h = _gn_math(has_gate, x_ref[rows, :], m_ref[rows, :] if has_gate else None,
                             g_ref[...] if has_gate else None, w_ref[...], sh_ref[...], sc_ref[...])
            if has_gate:
                xn_ref[rows, :] = xn
            h_ref[rows, :] = h.astype(BF16)
            return carry

        lax.fori_loop(0, R // RT, step, 0)

    row = pl.BlockSpec((R, D), lambda i: (i, 0))
    seg = _seg_spec(D, ncb)
    shared = pl.BlockSpec((None, RT, D), lambda i: (0, 0, 0))
    if has_gate:
        ins, in_specs = (x, m, gate, lnw, shift, scale), [row, row, seg, shared, seg, seg]
        out_shape = (jax.ShapeDtypeStruct((T, D), F32), jax.ShapeDtypeStruct((T, D), BF16))
        out_specs = (row, row)
    else:
        ins, in_specs = (x, lnw, shift, scale), [row, shared, seg, seg]
        out_shape, out_specs = jax.ShapeDtypeStruct((T, D), BF16), row
    out = pl.pallas_call(body, name=name, grid=(T // R,), out_shape=out_shape, in_specs=in_specs,
                         out_specs=out_specs, compiler_params=_cp("parallel"))(*ins)
    return out if has_gate else (None, out)


def gate_norm_bwd(x, m, gate, lnw, shift, scale, dxn, dh, nc, R, name):
    T, D = x.shape
    has_gate = m is not None
    ncb = nc // R

    def body(*refs):
        if has_gate:
            (x_ref, m_ref, g_ref, w_ref, sh_ref, sc_ref, dxn_ref, dh_ref,
             dx_ref, dm_ref, dg_ref, dw_ref, dsh_ref, dsc_ref) = refs
        else:
            x_ref, w_ref, sh_ref, sc_ref, dxn_ref, dh_ref, dx_ref, dw_ref, dsh_ref, dsc_ref = refs
        i = pl.program_id(0)

        @pl.when(i == 0)
        def _():
            dw_ref[...] = jnp.zeros_like(dw_ref)

        @pl.when((i == 0) | (i == ncb))
        def _():
            dsh_ref[...] = jnp.zeros_like(dsh_ref)
            dsc_ref[...] = jnp.zeros_like(dsc_ref)
            if has_gate:
                dg_ref[...] = jnp.zeros_like(dg_ref)

        def step(t, carry):
            rows = pl.ds(pl.multiple_of(t * RT, RT), RT)
            ct = (dxn_ref[rows, :], dh_ref[rows, :])
            if has_gate:
                _, vjp = jax.vjp(functools.partial(_gn_math, True), x_ref[rows, :], m_ref[rows, :], g_ref[...],
                                 w_ref[...], sh_ref[...], sc_ref[...])
                dx, dm, dg, dw, dsh, dsc = vjp(ct)
                dm_ref[rows, :] = dm.astype(BF16)
                dg_ref[...] += dg
            else:
                f = lambda x_, w_, sh_, sc_: _gn_math(False, x_, None, None, w_, sh_, sc_)[1]
                _, vjp = jax.vjp(f, x_ref[rows, :], w_ref[...], sh_ref[...], sc_ref[...])
                dx, dw, dsh, dsc = vjp(ct[1])
                dx = dx + ct[0]
            dx_ref[rows, :] = dx
            dw_ref[...] += dw
            dsh_ref[...] += dsh
            dsc_ref[...] += dsc
            return carry

        lax.fori_loop(0, R // RT, step, 0)

    row = pl.BlockSpec((R, D), lambda i: (i, 0))
    seg = _seg_spec(D, ncb)
    shared = pl.BlockSpec((None, RT, D), lambda i: (0, 0, 0))
    full, segs, one = jax.ShapeDtypeStruct((T, D), F32), jax.ShapeDtypeStruct((2, RT, D), F32), \
        jax.ShapeDtypeStruct((1, RT, D), F32)
    if has_gate:
        ins = (x, m, gate, lnw, shift, scale, dxn, dh)
        in_specs = [row, row, seg, shared, seg, seg, row, row]
        out_shape = (full, jax.ShapeDtypeStruct((T, D), BF16), segs, one, segs, segs)
        out_specs = (row, row, seg, shared, seg, seg)
    else:
        ins = (x, lnw, shift, scale, dxn, dh)
        in_specs = [row, shared, seg, seg, row, row]
        out_shape = (full, one, segs, segs)
        out_specs = (row, shared, seg, seg)
    out = pl.pallas_call(body, name=name, grid=(T // R,), out_shape=out_shape, in_specs=in_specs,
                         out_specs=out_specs, compiler_params=_cp("arbitrary"))(*ins)
    if has_gate:
        return out
    dx, dw, dsh, dsc = out
    return dx, None, None, dw, dsh, dsc


def gate_loss(x, m, gate, target, nc, R, name):
    T, D = x.shape
    ncb = nc // R

    def body(x_ref, m_ref, g_ref, t_ref, loss_ref, dx_ref, dm_ref, dg_ref):
        i = pl.program_id(0)

        @pl.when(i == 0)
        def _():
            loss_ref[...] = jnp.zeros_like(loss_ref)

        @pl.when((i == 0) | (i == ncb))
        def _():
            dg_ref[...] = jnp.zeros_like(dg_ref)

        live = jnp.where(i >= ncb, 1.0, 0.0).astype(F32)

        def step(t, carry):
            rows = pl.ds(pl.multiple_of(t * RT, RT), RT)
            mm_ = m_ref[rows, :]
            g = g_ref[...]
            e = (x_ref[rows, :] + g * mm_ - t_ref[rows, :]) * live
            dy = e * (1.0 / D)
            loss_ref[...] += 0.5 * e * dy
            dx_ref[rows, :] = dy
            dm_ref[rows, :] = (dy * g).astype(BF16)
            dg_ref[...] += dy * mm_
            return carry

        lax.fori_loop(0, R // RT, step, 0)

    row = pl.BlockSpec((R, D), lambda i: (i, 0))
    seg = _seg_spec(D, ncb)
    return pl.pallas_call(
        body, name=name, grid=(T // R,),
        out_shape=(jax.ShapeDtypeStruct((RT, D), F32), jax.ShapeDtypeStruct((T, D), F32),
                   jax.ShapeDtypeStruct((T, D), BF16), jax.ShapeDtypeStruct((2, RT, D), F32)),
        in_specs=[row, row, seg, pl.BlockSpec((R, D), lambda i: (jnp.maximum(i - ncb, 0), 0))],
        out_specs=(pl.BlockSpec((RT, D), lambda i: (0, 0)), row, row, seg),
        compiler_params=_cp("arbitrary"),
    )(x, m, gate, target)


def _hg_chunk(rev, lb, z, iv, hq, st):
    f = lb + (1.0 - lb) * jax.nn.sigmoid(z)
    g = jnp.log(jnp.maximum(f, F_FLOOR))
    k = (1.0 - lb) * jax.nn.sigmoid(-z)
    q = _silu(hq)
    ri = lax.broadcasted_iota(jnp.int32, (CHUNK, CHUNK), 0)
    ci = lax.broadcasted_iota(jnp.int32, (CHUNK, CHUNK), 1)
    r1 = lax.broadcasted_iota(jnp.int32, (CHUNK, 1), 0)
    seen = (ci >= ri) if rev else (ci <= ri)
    seen_t = (ci <= ri) if rev else (ci >= ri)
    cum = cumdot(seen.astype(F32), seen_t.astype(F32), g)
    tot = jnp.sum(g, axis=0, keepdims=True)
    att = jnp.zeros((CHUNK, CHUNK), F32)
    ref_rows = jnp.zeros_like(g)
    refs = []
    for b in range(CHUNK // SUB):
        before = (r1 >= SUB * (b + 1)) if rev else (r1 < SUB * b)
        r_b = jnp.sum(jnp.where(before, g, 0.0), axis=0, keepdims=True)
        in_b = (r1 >= SUB * b) & (r1 < SUB * (b + 1))
        ref_rows = ref_rows + jnp.where(in_b, r_b, 0.0)
        refs.append(r_b)
    qd = q * jnp.exp(cum - ref_rows)
    for b in range(CHUNK // SUB):
        kd = k * jnp.exp(jnp.minimum(refs[b] - cum, EXP_CLAMP))
        in_b = (ri >= SUB * b) & (ri < SUB * (b + 1))
        att = att + jnp.where(in_b, mmf(qd, kd, "nt"), 0.0)
    att = jnp.where(seen, att, 0.0)
    o = mmf(att, iv, "nn") + mmf(q * jnp.exp(cum), st, "nt")
    st_new = st * jnp.exp(tot) + mmf(iv, k * jnp.exp(tot - cum), "tn")
    return st_new, o


def _hg_cid(rev, i, ncs, n):
    if not rev:
        return i
    return jnp.where(i < ncs, ncs - 1 - i, ncs + n - 1 - i)


def hgrn_fwd(u, lb, rev, zcol, nc, hgw, name):
    T = u.shape[0]
    n, ncs, nh = T // CHUNK, nc // CHUNK, hgw // LANE

    def body(z_ref, v_ref, q_ref, lb_ref, o_ref, s_ref, st):
        i = pl.program_id(0)

        @pl.when(i == 0)
        def _():
            st[...] = jnp.zeros_like(st)

        for h in range(nh):
            cols = slice(h * LANE, (h + 1) * LANE)
            s_ref[h] = st[h]
            s_new, o = _hg_chunk(rev, lb_ref[:, cols], z_ref[:, cols], v_ref[:, cols], q_ref[:, cols], st[h])
            st[h] = s_new
            o_ref[:, cols] = o

    def col(cb):
        return pl.BlockSpec((CHUNK, hgw), lambda i: (_hg_cid(rev, i, ncs, n), cb))

    return pl.pallas_call(
        body, name=name, grid=(n,),
        out_shape=(jax.ShapeDtypeStruct((T, hgw), F32), jax.ShapeDtypeStruct((n, nh, LANE, LANE), F32)),
        in_specs=[col(zcol), col(2), col(7), pl.BlockSpec((1, hgw), lambda i: (0, 0))],
        out_specs=(pl.BlockSpec((CHUNK, hgw), lambda i: (_hg_cid(rev, i, ncs, n), 0)),
                   pl.BlockSpec((None, nh, LANE, LANE), lambda i: (i, 0, 0, 0))),
        scratch_shapes=[pltpu.VMEM((nh, LANE, LANE), F32)], compiler_params=_cp("arbitrary"),
    )(u, u, u, lb)


def hgrn_bwd(u, lb, states, do, rev, zcol, nc, hgw, name):
    T = u.shape[0]
    n, ncs, nh = T // CHUNK, nc // CHUNK, hgw // LANE

    def body(z_ref, v_ref, q_ref, lb_ref, s_ref, do_ref, dz_ref, dv_ref, dq_ref, dlb_ref, dst):
        j = pl.program_id(0)

        @pl.when(j == 0)
        def _():
            dst[...] = jnp.zeros_like(dst)
            dlb_ref[...] = jnp.zeros_like(dlb_ref)

        for h in range(nh):
            cols = slice(h * LANE, (h + 1) * LANE)
            _, vjp = jax.vjp(functools.partial(_hg_chunk, rev), lb_ref[:, cols], z_ref[:, cols], v_ref[:, cols],
                             q_ref[:, cols], s_ref[h])
            dlb, dz, dv, dq, ds = vjp((dst[h], do_ref[:, cols]))
            dst[h] = ds
            dz_ref[:, cols] = dz
            dv_ref[:, cols] = dv
            dq_ref[:, cols] = dq
            dlb_ref[:, cols] += dlb

    def cid(j):
        return _hg_cid(rev, n - 1 - j, ncs, n)

    def col(cb):
        return pl.BlockSpec((CHUNK, hgw), lambda j: (cid(j), cb))

    out = pl.BlockSpec((CHUNK, hgw), lambda j: (cid(j), 0))
    full = jax.ShapeDtypeStruct((T, hgw), F32)
    return pl.pallas_call(
        body, name=name, grid=(n,),
        out_shape=(full, full, full, jax.ShapeDtypeStruct((1, hgw), F32)),
        in_specs=[col(zcol), col(2), col(7), pl.BlockSpec((1, hgw), lambda j: (0, 0)),
                  pl.BlockSpec((None, nh, LANE, LANE), lambda j: (n - 1 - j, 0, 0, 0)), out],
        out_specs=(out, out, out, pl.BlockSpec((1, hgw), lambda j: (0, 0))),
        scratch_shapes=[pltpu.VMEM((nh, LANE, LANE), F32)], compiler_params=_cp("arbitrary"),
    )(u, u, u, lb, states, do)


HT = 128


def _read_math(ofw, obw, g, w):
    return _rms(ofw + obw, w) * _silu(g)


def hg_read(ofw, obw, u, w, gcol, R, name):
    T, hgw = ofw.shape
    nh = hgw // LANE

    def body(a_ref, b_ref, g_ref, w_ref, o_ref):
        for t in range(R // HT):
            rows = slice(t * HT, (t + 1) * HT)
            o_ref[rows, :] = _read_math(a_ref[rows, :], b_ref[rows, :], g_ref[rows, :], w_ref[...]).astype(BF16)

    blk = pl.BlockSpec((R, LANE), lambda i, h: (i, h))
    return pl.pallas_call(
        body, name=name, grid=(T // R, nh), out_shape=jax.ShapeDtypeStruct((T, hgw), BF16),
        in_specs=[blk, blk, pl.BlockSpec((R, LANE), lambda i, h: (i, gcol + h)),
                  pl.BlockSpec((1, LANE), lambda i, h: (0, 0))],
        out_specs=blk, compiler_params=_cp("parallel", "parallel"),
    )(ofw, obw, u, w)


def hg_read_bwd(ofw, obw, u, w, dout, gcol, ocol, R, name):
    T, hgw = ofw.shape
    nh = hgw // LANE

    def body(a_ref, b_ref, g_ref, w_ref, d_ref, do_ref, dg_ref, dw_ref):
        @pl.when(pl.program_id(1) == 0)
        def _():
            dw_ref[...] = jnp.zeros_like(dw_ref)

        for t in range(R // HT):
            rows = slice(t * HT, (t + 1) * HT)
            _, vjp = jax.vjp(_read_math, a_ref[rows, :], b_ref[rows, :], g_ref[rows, :], w_ref[...])
            da, _, dg, dw = vjp(d_ref[rows, :])
            do_ref[rows, :] = da
            dg_ref[rows, :] = dg
            dw_ref[...] += dw

    blk = pl.BlockSpec((R, LANE), lambda h, i: (i, h))
    full = jax.ShapeDtypeStruct((T, hgw), F32)
    return pl.pallas_call(
        body, name=name, grid=(nh, T // R), out_shape=(full, full, jax.ShapeDtypeStruct((nh, 1, LANE), F32)),
        in_specs=[blk, blk, pl.BlockSpec((R, LANE), lambda h, i: (i, gcol + h)),
                  pl.BlockSpec((1, LANE), lambda h, i: (0, 0)), pl.BlockSpec((R, LANE), lambda h, i: (i, ocol + h))],
        out_specs=(blk, blk, pl.BlockSpec((None, 1, LANE), lambda h, i: (h, 0, 0))),
        compiler_params=_cp("parallel", "arbitrary"),
    )(ofw, obw, u, w, dout)


def _na_step(qw, ow, bias, qraw, kl, vl, kc, vc):
    q = _rms(qraw, qw)
    s_loc = mmf(q, kl, "nt") * ATTN_SCALE + bias
    s_ctx = mmf(q, kc, "nt") * ATTN_SCALE
    m = lax.stop_gradient(jnp.maximum(jnp.max(s_loc, axis=-1, keepdims=True), jnp.max(s_ctx, axis=-1, keepdims=True)))
    p_loc = jnp.exp(s_loc - m)
    p_ctx = jnp.exp(s_ctx - m)
    inv = 1.0 / (jnp.sum(p_loc, axis=-1, keepdims=True) + jnp.sum(p_ctx, axis=-1, keepdims=True))
    return _rms(mmf(p_loc * inv, vl, "nn") + mmf(p_ctx * inv, vc, "nn"), ow)


def _na_geometry(nc, rows):
    ncs = nc // GRID_W
    win_r = min(WIN_R, rows)
    nloc = win_r * GRID_W

    def row_start(s):
        r = jnp.maximum(s - ncs, 0)
        return jnp.clip(r - win_r // 2, 0, rows - win_r)

    def bias_idx(s):
        r = s - ncs
        return jnp.where(s < ncs, win_r, r - jnp.clip(r - win_r // 2, 0, rows - win_r))

    return ncs, win_r, nloc, row_start, bias_idx


def na_bias_tables(rpb, rows):
    win_r = min(WIN_R, rows)
    nh = rpb.shape[0]
    sel_r = np.zeros((win_r, win_r, 2 * WIN_R - 1), np.float32)
    for off in range(win_r):
        for jr in range(win_r):
            sel_r[off, jr, jr - off + WIN_R - 1] = 1.0
    qc = np.arange(GRID_W)[:, None]
    kc = np.arange(GRID_W)[None, :]
    wstart = np.clip(qc - WIN_C // 2, 0, GRID_W - WIN_C)
    ok = (kc >= wstart) & (kc < wstart + WIN_C)
    sel_c = np.zeros((GRID_W, GRID_W, 2 * WIN_C - 1), np.float32)
    sel_c[np.broadcast_to(qc, ok.shape)[ok], np.broadcast_to(kc, ok.shape)[ok], (kc - qc + WIN_C - 1)[ok]] = 1.0
    hi = lax.Precision.HIGHEST
    t = jnp.einsum("hab,oja->hojb", rpb, sel_r, precision=hi)
    t = jnp.einsum("hojb,qkb->hoqjk", t, sel_c, precision=hi)
    t = jnp.where(ok[None, None, :, None, :], t, NEG_INF)
    t = jnp.concatenate([t, jnp.full((nh, 1, GRID_W, win_r, GRID_W), NEG_INF, F32)], axis=1)
    return t.reshape(nh, win_r + 1, GRID_W, win_r * GRID_W)


def kv_prep(u, kw, kcol, vcol, naw, R, name):
    T = u.shape[0]

    def body(k_ref, v_ref, w_ref, kn_ref, vb_ref):
        kn_ref[...] = _rms(k_ref[...], w_ref[...]).astype(BF16)
        vb_ref[...] = v_ref[...].astype(BF16)

    blk = pl.BlockSpec((R, LANE), lambda i, h: (i, h))
    sds = jax.ShapeDtypeStruct((T, naw), BF16)
    return pl.pallas_call(
        body, name=name, grid=(T // R, naw // LANE), out_shape=(sds, sds),
        in_specs=[pl.BlockSpec((R, LANE), lambda i, h: (i, kcol + h)), pl.BlockSpec((R, LANE), lambda i, h: (i, vcol + h)),
                  pl.BlockSpec((1, LANE), lambda i, h: (0, 0))],
        out_specs=(blk, blk), compiler_params=_cp("parallel", "parallel"),
    )(u, u, kw)


def kv_prep_bwd(u, kw, dkn, kcol, naw, R, name):
    T = u.shape[0]
    nh = naw // LANE

    def body(k_ref, w_ref, d_ref, dk_ref, dw_ref):
        @pl.when(pl.program_id(1) == 0)
        def _():
            dw_ref[...] = jnp.zeros_like(dw_ref)

        for t in range(R // HT):
            rows = slice(t * HT, (t + 1) * HT)
            _, vjp = jax.vjp(_rms, k_ref[rows, :], w_ref[...])
            dk, dw = vjp(d_ref[rows, :])
            dk_ref[rows, :] = dk
            dw_ref[...] += dw

    blk = pl.BlockSpec((R, LANE), lambda h, i: (i, h))
    return pl.pallas_call(
        body, name=name, grid=(nh, T // R),
        out_shape=(jax.ShapeDtypeStruct((T, naw), F32), jax.ShapeDtypeStruct((nh, 1, LANE), F32)),
        in_specs=[pl.BlockSpec((R, LANE), lambda h, i: (i, kcol + h)), pl.BlockSpec((1, LANE), lambda h, i: (0, 0)), blk],
        out_specs=(blk, pl.BlockSpec((None, 1, LANE), lambda h, i: (h, 0, 0))),
        compiler_params=_cp("parallel", "arbitrary"),
    )(u, kw, dkn)


NA_HB = 2


def _na_operands(j, s, nc, nloc, row_start, q_refs, k_ref, v_ref, qw_ref, ow_ref, b_ref):
    cols = slice(j * LANE, (j + 1) * LANE)
    loc = pl.ds(pl.multiple_of(nc + row_start(s) * GRID_W, GRID_W), nloc)
    ops = (qw_ref[...], ow_ref[:, cols], b_ref[j], q_refs[j][...], k_ref[loc, cols].astype(F32),
           v_ref[loc, cols].astype(F32), k_ref[0:nc, cols].astype(F32), v_ref[0:nc, cols].astype(F32))
    return cols, loc, ops


def _grid_ends(grid):
    ids = lambda: [pl.program_id(d) for d in range(len(grid))]
    first = lambda: functools.reduce(jnp.logical_and, [p == 0 for p in ids()])
    last = lambda: functools.reduce(jnp.logical_and, [p == g - 1 for p, g in zip(ids(), grid)])
    return first, last


def na_fwd(u, kn, vb, qw, ow, bias, qcol, nc, name, comm=None):
    T, naw = kn.shape
    nh, rows = naw // LANE, (T - nc) // GRID_W
    hb = NA_HB if nh % NA_HB == 0 else 1
    ncs, win_r, nloc, row_start, bias_idx = _na_geometry(nc, rows)

    def body(*refs):
        q_refs, (k_ref, v_ref, qw_ref, ow_ref, b_ref, o_ref) = refs[:hb], refs[hb:]
        s = pl.program_id(1)
        for j in range(hb):
            cols, _, ops = _na_operands(j, s, nc, nloc, row_start, q_refs, k_ref, v_ref, qw_ref, ow_ref, b_ref)
            o_ref[:, cols] = _na_step(*ops).astype(BF16)

    wide = pl.BlockSpec((T, hb * LANE), lambda g, s: (0, g))
    grid = (nh // hb, T // GRID_W)
    cin, cspec, cout, csem = _comm_extras(comm)
    out = pl.pallas_call(
        _hosted(body, hb + 5, 1, comm, *_grid_ends(grid)), name=name, grid=grid,
        out_shape=[jax.ShapeDtypeStruct((T, naw), BF16)] + cout,
        in_specs=[pl.BlockSpec((GRID_W, LANE), functools.partial(lambda j, g, s: (s, qcol + g * hb + j), j))
                  for j in range(hb)]
        + [wide, wide, pl.BlockSpec((1, LANE), lambda g, s: (0, 0)), pl.BlockSpec((1, hb * LANE), lambda g, s: (0, g)),
           pl.BlockSpec((hb, None, GRID_W, nloc), lambda g, s: (g, bias_idx(s), 0, 0))] + cspec,
        out_specs=[pl.BlockSpec((GRID_W, hb * LANE), lambda g, s: (s, g))] + cspec, scratch_shapes=csem,
        compiler_params=_cp("arbitrary", "arbitrary"),
    )(*([u] * hb), kn, vb, qw, ow, bias, *cin)
    return out[0] if comm is None else out


def na_bwd(u, kn, vb, qw, ow, bias, dout, qcol, ocol, nc, name, comm=None):
    T, naw = kn.shape
    nh, rows = naw // LANE, (T - nc) // GRID_W
    hb = NA_HB if nh % NA_HB == 0 else 1
    ncs, win_r, nloc, row_start, bias_idx = _na_geometry(nc, rows)
    fresh = [0] + [ncs + r for r in range(rows) if r == 0 or r - np.clip(r - win_r // 2, 0, rows - win_r)
                   != (r - 1) - np.clip(r - 1 - win_r // 2, 0, rows - win_r)]

    def body(*refs):
        q_refs, d_refs = refs[:hb], refs[hb:2 * hb]
        k_ref, v_ref, qw_ref, ow_ref, b_ref, dq_ref, dk_ref, dv_ref, db_ref, dqw_ref, dow_ref = refs[2 * hb:]
        s = pl.program_id(1)

        @pl.when(s == 0)
        def _():
            dk_ref[...] = jnp.zeros_like(dk_ref)
            dv_ref[...] = jnp.zeros_like(dv_ref)
            dqw_ref[...] = jnp.zeros_like(dqw_ref)
            dow_ref[...] = jnp.zeros_like(dow_ref)

        first = functools.reduce(lambda a, b: a | b, [s == f for f in fresh])

        @pl.when(first)
        def _():
            db_ref[...] = jnp.zeros_like(db_ref)

        for j in range(hb):
            cols, loc, ops = _na_operands(j, s, nc, nloc, row_start, q_refs, k_ref, v_ref, qw_ref, ow_ref, b_ref)
            _, vjp = jax.vjp(_na_step, *ops)
            dqw, dow, db, dq, dkl, dvl, dkc, dvc = vjp(d_refs[j][...])
            dq_ref[:, cols] = dq
            dk_ref[loc, cols] += dkl
            dv_ref[loc, cols] += dvl
            dk_ref[0:nc, cols] += dkc
            dv_ref[0:nc, cols] += dvc
            db_ref[j] += db
            dqw_ref[j] += dqw
            dow_ref[j] += dow

    wide = pl.BlockSpec((T, hb * LANE), lambda g, s: (0, g))
    hvec = pl.BlockSpec((hb, 1, LANE), lambda g, s: (g, 0, 0))
    full = jax.ShapeDtypeStruct((T, naw), F32)
    hv = jax.ShapeDtypeStruct((nh, 1, LANE), F32)
    bspec = pl.BlockSpec((hb, None, GRID_W, nloc), lambda g, s: (g, bias_idx(s), 0, 0))
    grid = (nh // hb, T // GRID_W)
    cin, cspec, cout, csem = _comm_extras(comm)
    return pl.pallas_call(
        _hosted(body, 2 * hb + 5, 6, comm, *_grid_ends(grid)), name=name, grid=grid,
        out_shape=[full, full, full, jax.ShapeDtypeStruct(bias.shape, F32), hv, hv] + cout,
        in_specs=[pl.BlockSpec((GRID_W, LANE), functools.partial(lambda j, g, s: (s, qcol + g * hb + j), j))
                  for j in range(hb)]
        + [pl.BlockSpec((GRID_W, LANE), functools.partial(lambda j, g, s: (s, ocol + g * hb + j), j))
           for j in range(hb)]
        + [wide, wide, pl.BlockSpec((1, LANE), lambda g, s: (0, 0)), pl.BlockSpec((1, hb * LANE), lambda g, s: (0, g)),
           bspec] + cspec,
        out_specs=[pl.BlockSpec((GRID_W, hb * LANE), lambda g, s: (s, g)), wide, wide, bspec, hvec, hvec] + cspec,
        scratch_shapes=csem, compiler_params=_cp("arbitrary", "arbitrary"),
    )(*([u] * hb), *([dout] * hb), kn, vb, qw, ow, bias, *cin)


def _halo_specs(R, width, T, col):
    hb = R // 8
    prev = pl.BlockSpec((8, width), lambda j, i: (jnp.maximum(i * hb - 1, 0), col(j, i)))
    nxt = pl.BlockSpec((8, width), lambda j, i: (jnp.minimum((i + 1) * hb, T // 8 - 1), col(j, i)))
    return prev, nxt


def _edge_flags(i, ncb, nblk):
    has_prev = jnp.where((i == 0) | (i == ncb), 0.0, 1.0).astype(F32)
    has_next = jnp.where((i == ncb - 1) | (i == nblk - 1), 0.0, 1.0).astype(F32)
    return has_prev, has_next


def _shift_up(a, prev_row):
    r0 = lax.broadcasted_iota(jnp.int32, a.shape, 0) == 0
    return jnp.where(r0, prev_row, pltpu.roll(a, 1, 0))


def _shift_dn(a, next_row):
    n = a.shape[0]
    rl = lax.broadcasted_iota(jnp.int32, a.shape, 0) == n - 1
    return jnp.where(rl, next_row, pltpu.roll(a, n - 1, 0))


def _conv3(a, prev_row, next_row, w_ref):
    return w_ref[0:1, :] * _shift_up(a, prev_row) + w_ref[1:2, :] * a + w_ref[2:3, :] * _shift_dn(a, next_row)


def _cv_post(b, y, w):
    return _rms(b * y, w)


def short_conv(u, cw, ow, bcol, nc, cvw, R, bwd_dout=None, ocol=0, name=""):
    T = u.shape[0]
    nh, nblk, ncb = cvw // LANE, T // R, nc // R
    bwd = bwd_dout is not None

    def body(b_ref, c_ref, v_ref, cp_ref, vp_ref, cn_ref, vn_ref, cw_ref, ow_ref, *rest):
        i = pl.program_id(1)
        has_prev, has_next = _edge_flags(i, ncb, nblk)
        p = c_ref[...] * v_ref[...]
        y = _conv3(p, cp_ref[7:8, :] * vp_ref[7:8, :] * has_prev, cn_ref[0:1, :] * vn_ref[0:1, :] * has_next, cw_ref)
        if not bwd:
            rest[0][...] = _cv_post(b_ref[...], y, ow_ref[...]).astype(BF16)
            return
        d_ref, db_ref, dy_ref, dow_ref = rest

        @pl.when(i == 0)
        def _():
            dow_ref[...] = jnp.zeros_like(dow_ref)

        _, vjp = jax.vjp(_cv_post, b_ref[...], y, ow_ref[...])
        db, dy, dow = vjp(d_ref[...])
        db_ref[...] = db
        dy_ref[...] = dy
        dow_ref[...] += dow

    def main(k):
        return pl.BlockSpec((R, LANE), lambda h, i: (i, bcol + k * nh + h))

    cprev, cnext = _halo_specs(R, LANE, T, lambda h, i: bcol + nh + h)
    vprev, vnext = _halo_specs(R, LANE, T, lambda h, i: bcol + 2 * nh + h)
    in_specs = [main(0), main(1), main(2), cprev, vprev, cnext, vnext,
                pl.BlockSpec((3, LANE), lambda h, i: (0, h)), pl.BlockSpec((1, LANE), lambda h, i: (0, h))]
    ins = [u] * 7 + [cw, ow]
    blk = pl.BlockSpec((R, LANE), lambda h, i: (i, h))
    if not bwd:
        out_shape, out_specs = jax.ShapeDtypeStruct((T, cvw), BF16), blk
    else:
        in_specs.append(pl.BlockSpec((R, LANE), lambda h, i: (i, ocol + h)))
        ins.append(bwd_dout)
        full = jax.ShapeDtypeStruct((T, cvw), F32)
        out_shape = (full, full, jax.ShapeDtypeStruct((nh, 1, LANE), F32))
        out_specs = (blk, blk, pl.BlockSpec((None, 1, LANE), lambda h, i: (h, 0, 0)))
    return pl.pallas_call(body, name=name, grid=(nh, nblk), out_shape=out_shape, in_specs=in_specs,
                          out_specs=out_specs, compiler_params=_cp("parallel", "arbitrary"))(*ins)


def conv3_bwd(dy, src, cw, nc, R, W, prod_cols=None, col0=0, out_dtype=F32, name=""):
    T, C = dy.shape
    nblk, ncb = T // R, nc // R
    prod = prod_cols is not None

    def body(*refs):
        if prod:
            (d_ref, dp_ref, dn_ref, c_ref, v_ref, cp_ref, vp_ref, cn_ref, vn_ref, w_ref,
             dc_ref, dv_ref, dw_ref) = refs
        else:
            d_ref, dp_ref, dn_ref, p_ref, pp_ref, pn_ref, w_ref, o_ref, dw_ref = refs
        i = pl.program_id(1)
        has_prev, has_next = _edge_flags(i, ncb, nblk)

        @pl.when(i == 0)
        def _():
            dw_ref[...] = jnp.zeros_like(dw_ref)

        d = d_ref[...]
        d_up = _shift_up(d, dp_ref[7:8, :] * has_prev)
        d_dn = _shift_dn(d, dn_ref[0:1, :] * has_next)
        dp = w_ref[0:1, :] * d_dn + w_ref[1:2, :] * d + w_ref[2:3, :] * d_up
        if prod:
            c, v = c_ref[...], v_ref[...]
            p = c * v
            p_prev, p_next = cp_ref[7:8, :] * vp_ref[7:8, :] * has_prev, cn_ref[0:1, :] * vn_ref[0:1, :] * has_next
            dc_ref[...] = dp * v
            dv_ref[...] = dp * c
        else:
            p = p_ref[...]
            p_prev, p_next = pp_ref[7:8, :] * has_prev, pn_ref[0:1, :] * has_next
            o_ref[...] = dp.astype(out_dtype)
        dw_ref[0:1, :] += jnp.sum(_shift_up(p, p_prev) * d, axis=0, keepdims=True)
        dw_ref[1:2, :] += jnp.sum(p * d, axis=0, keepdims=True)
        dw_ref[2:3, :] += jnp.sum(_shift_dn(p, p_next) * d, axis=0, keepdims=True)

    blk = pl.BlockSpec((R, W), lambda j, i: (i, j))
    dprev, dnext = _halo_specs(R, W, T, lambda j, i: j)
    wspec = pl.BlockSpec((3, W), lambda j, i: (0, j))
    dwspec = pl.BlockSpec((8, W), lambda j, i: (0, j))
    dwshape = jax.ShapeDtypeStruct((8, C), F32)
    if prod:
        ccol, vcol = prod_cols
        cprev, cnext = _halo_specs(R, W, T, lambda j, i: ccol + j)
        vprev, vnext = _halo_specs(R, W, T, lambda j, i: vcol + j)
        in_specs = [blk, dprev, dnext, pl.BlockSpec((R, W), lambda j, i: (i, ccol + j)),
                    pl.BlockSpec((R, W), lambda j, i: (i, vcol + j)), cprev, vprev, cnext, vnext, wspec]
        ins = [dy, dy, dy] + [src] * 6 + [cw]
        full = jax.ShapeDtypeStruct((T, C), F32)
        out_shape, out_specs = (full, full, dwshape), (blk, blk, dwspec)
    else:
        sprev, snext = _halo_specs(R, W, T, lambda j, i: col0 + j)
        in_specs = [blk, dprev, dnext, pl.BlockSpec((R, W), lambda j, i: (i, col0 + j)), sprev, snext,
                    pl.BlockSpec((3, W), lambda j, i: (0, col0 + j))]
        ins = [dy, dy, dy, src, src, src, cw]
        out_shape, out_specs = (jax.ShapeDtypeStruct((T, C), out_dtype), dwshape), (blk, dwspec)
    return pl.pallas_call(body, name=name, grid=(C // W, nblk), out_shape=out_shape, in_specs=in_specs,
                          out_specs=out_specs, compiler_params=_cp("parallel", "arbitrary"))(*ins)


def ffn_mid(uf, cw, cb, nc, R, W, da=None, name=""):
    T, C = uf.shape
    F = C // 2
    nblk, ncb, nj = T // R, nc // R, F // W
    bwd = da is not None

    def body(g_ref, v_ref, gp_ref, vp_ref, gn_ref, vn_ref, wg_ref, wv_ref, bg_ref, bv_ref, *rest):
        i = pl.program_id(1)
        has_prev, has_next = _edge_flags(i, ncb, nblk)
        yg = _conv3(g_ref[...], gp_ref[7:8, :] * has_prev, gn_ref[0:1, :] * has_next, wg_ref) + bg_ref[...]
        yv = _conv3(v_ref[...], vp_ref[7:8, :] * has_prev, vn_ref[0:1, :] * has_next, wv_ref) + bv_ref[...]
        sg = jax.nn.sigmoid(yg)
        if not bwd:
            rest[0][...] = (yg * sg * yv).astype(BF16)
            return
        da_ref, dyg_ref, dyv_ref, dbg_ref, dbv_ref = rest

        @pl.when(i == 0)
        def _():
            dbg_ref[...] = jnp.zeros_like(dbg_ref)
            dbv_ref[...] = jnp.zeros_like(dbv_ref)

        d = da_ref[...]
        dyg = d * yv * (sg * (1.0 + yg * (1.0 - sg)))
        dyv = d * (yg * sg)
        dyg_ref[...] = dyg
        dyv_ref[...] = dyv
        dbg_ref[...] += jnp.sum(dyg, axis=0, keepdims=True)
        dbv_ref[...] += jnp.sum(dyv, axis=0, keepdims=True)

    gblk = pl.BlockSpec((R, W), lambda j, i: (i, j))
    vblk = pl.BlockSpec((R, W), lambda j, i: (i, nj + j))
    gprev, gnext = _halo_specs(R, W, T, lambda j, i: j)
    vprev, vnext = _halo_specs(R, W, T, lambda j, i: nj + j)
    in_specs = [gblk, vblk, gprev, vprev, gnext, vnext,
                pl.BlockSpec((3, W), lambda j, i: (0, j)), pl.BlockSpec((3, W), lambda j, i: (0, nj + j)),
                pl.BlockSpec((1, W), lambda j, i: (0, j)), pl.BlockSpec((1, W), lambda j, i: (0, nj + j))]
    ins = [uf] * 6 + [cw, cw, cb, cb]
    if not bwd:
        out_shape, out_specs = jax.ShapeDtypeStruct((T, F), BF16), gblk
    else:
        in_specs.append(gblk)
        ins.append(da)
        half, bias = jax.ShapeDtypeStruct((T, F), F32), jax.ShapeDtypeStruct((1, F), F32)
        bspec = pl.BlockSpec((1, W), lambda j, i: (0, j))
        out_shape, out_specs = (half, half, bias, bias), (gblk, gblk, bspec, bspec)
    return pl.pallas_call(body, name=name, grid=(nj, nblk), out_shape=out_shape, in_specs=in_specs,
                          out_specs=out_specs, compiler_params=_cp("parallel", "arbitrary"))(*ins)


WEIGHTS = ("c_ctx", "w_ada", "b_ada", "ln1_w", "ln2_w", "w_in", "hg_lb_logits", "hg_norm_w", "na_q_norm_w",
           "na_k_norm_w", "na_rpb", "na_out_norm_w", "cv_w", "cv_out_norm_w", "w_out", "w_up", "ffn_conv_w",
           "ffn_conv_b", "w_down")
BIG = ("w_ada", "w_in", "w_out", "w_up", "w_down")
SHARDED_SMALL = ("hg_lb_logits", "cv_w", "ffn_conv_w")


def _flat_rows(parts, dtype):
    flat, layout, off = [], [], 0
    for p in parts:
        layout.append((off, p.shape))
        flat.append(p.reshape(-1).astype(dtype))
        off += p.size
    pad = (-off) % (8 * LANE)
    if pad:
        flat.append(jnp.zeros((pad,), dtype))
    return jnp.concatenate(flat).reshape(-1, LANE), layout


def _unflat(buf, layout):
    v = buf.reshape(-1)
    return [v[off:off + int(np.prod(shape))].reshape(shape) for off, shape in layout]


def _lb_all(logits):
    sm = jax.nn.softmax(logits.astype(F32), axis=1)
    return jnp.cumsum(sm, axis=1) - sm[:, :1]


def _seg(ctx_vec, lat_vec):
    return jnp.broadcast_to(jnp.stack([ctx_vec, lat_vec])[:, None, :], (2, RT, ctx_vec.shape[0]))


def _shared(vec):
    return jnp.broadcast_to(vec[None, None, :], (1, RT, vec.shape[0]))


def kernel(x, c, ctx, c_ctx, w_ada, b_ada, ln1_w, ln2_w, w_in, hg_lb_logits, hg_norm_w, na_q_norm_w, na_k_norm_w, na_rpb, na_out_norm_w, cv_w, cv_out_norm_w, w_out, w_up, ffn_conv_w, ffn_conv_b, w_down, loss_target, m_c_ctx, m_w_ada, m_b_ada, m_ln1_w, m_ln2_w, m_w_in, m_hg_lb_logits, m_hg_norm_w, m_na_q_norm_w, m_na_k_norm_w, m_na_rpb, m_na_out_norm_w, m_cv_w, m_cv_out_norm_w, m_w_out, m_w_up, m_ffn_conv_w, m_ffn_conv_b, m_w_down, v_c_ctx, v_w_ada, v_b_ada, v_ln1_w, v_ln2_w, v_w_in, v_hg_lb_logits, v_hg_norm_w, v_na_q_norm_w, v_na_k_norm_w, v_na_rpb, v_na_out_norm_w, v_cv_w, v_cv_out_norm_w, v_w_out, v_w_up, v_ffn_conv_w, v_ffn_conv_b, v_w_down):
    W = dict(c_ctx=c_ctx, w_ada=w_ada, b_ada=b_ada, ln1_w=ln1_w, ln2_w=ln2_w, w_in=w_in, hg_lb_logits=hg_lb_logits,
             hg_norm_w=hg_norm_w, na_q_norm_w=na_q_norm_w, na_k_norm_w=na_k_norm_w, na_rpb=na_rpb,
             na_out_norm_w=na_out_norm_w, cv_w=cv_w, cv_out_norm_w=cv_out_norm_w, w_out=w_out, w_up=w_up,
             ffn_conv_w=ffn_conv_w, ffn_conv_b=ffn_conv_b, w_down=w_down)
    Mo = dict(c_ctx=m_c_ctx, w_ada=m_w_ada, b_ada=m_b_ada, ln1_w=m_ln1_w, ln2_w=m_ln2_w, w_in=m_w_in,
              hg_lb_logits=m_hg_lb_logits, hg_norm_w=m_hg_norm_w, na_q_norm_w=m_na_q_norm_w,
              na_k_norm_w=m_na_k_norm_w, na_rpb=m_na_rpb, na_out_norm_w=m_na_out_norm_w, cv_w=m_cv_w,
              cv_out_norm_w=m_cv_out_norm_w, w_out=m_w_out, w_up=m_w_up, ffn_conv_w=m_ffn_conv_w,
              ffn_conv_b=m_ffn_conv_b, w_down=m_w_down)
    Vo = dict(c_ctx=v_c_ctx, w_ada=v_w_ada, b_ada=v_b_ada, ln1_w=v_ln1_w, ln2_w=v_ln2_w, w_in=v_w_in,
              hg_lb_logits=v_hg_lb_logits, hg_norm_w=v_hg_norm_w, na_q_norm_w=v_na_q_norm_w,
              na_k_norm_w=v_na_k_norm_w, na_rpb=v_na_rpb, na_out_norm_w=v_na_out_norm_w, cv_w=v_cv_w,
              cv_out_norm_w=v_cv_out_norm_w, w_out=v_w_out, w_up=v_w_up, ffn_conv_w=v_ffn_conv_w,
              ffn_conv_b=v_ffn_conv_b, w_down=v_w_down)

    xi, yi, ci = _me()
    chip = 2 * xi + yi
    dev = 2 * chip + ci
    L, D = x.shape[1], x.shape[2]
    NC = ctx.shape[1]
    T = NC + L
    depth = w_in.shape[0]
    HGW, NAW, CVW = 4 * hg_lb_logits.shape[-1], na_out_norm_w.shape[-1], cv_out_norm_w.shape[-1]
    MIX = HGW + NAW + CVW
    INW, FF2 = 4 * w_in.shape[-1], 4 * w_up.shape[-1]
    F = FF2 // 2
    ADA = 4 * w_ada.shape[-1]
    assert NAW == 2 * HGW and INW == 5 * HGW + 3 * NAW + 3 * CVW and ADA == 6 * D and NC % 128 == 0
    assert L % GRID_W == 0 and T % CHUNK == 0 and depth == 2
    R = math.gcd(NC, 256)
    FW = 512 if F % 512 == 0 else LANE
    rows = L // GRID_W
    nh_hg, nh_na, nh_cv = HGW // LANE, NAW // LANE, CVW // LANE
    kcol = 3 * nh_hg
    vcol = kcol + nh_na
    gcol = vcol + nh_na + nh_hg
    qcol = gcol + nh_hg
    bcol = qcol + nh_na
    mix_na, mix_cv = nh_hg, nh_hg + nh_na

    small1, lay1 = _flat_rows([c[0], hg_lb_logits, cv_w, ffn_conv_w], F32)
    g1 = allgather8([small1], "gather_cond")[0]
    per_dev = [_unflat(g1[d], lay1) for d in range(8)]
    c_all = jnp.stack([p[0] for p in per_dev])
    lb_logits = jnp.concatenate([per_dev[2 * s][1] for s in range(4)], axis=-1)
    cvw_full = jnp.concatenate([per_dev[2 * s][2] for s in range(4)], axis=-1)
    fcw_full = jnp.concatenate([per_dev[2 * s][3] for s in range(4)], axis=-1)
    lb_all, lb_pull = jax.vjp(_lb_all, lb_logits)

    a16 = jnp.concatenate([c_all, c_ctx[None], jnp.zeros((7, D), F32)])
    s16 = _silu(a16)
    wcols = ADA // 4
    b_mine = lax.dynamic_slice_in_dim(b_ada, chip * wcols, wcols, axis=1)
    p_ada = jnp.stack([mm_nn(s16, w_ada[l][None], F32, f"ada_fwd_{l}") + b_mine[l][None] for l in range(depth)])
    g2 = allgather8([p_ada.reshape(depth * 16, wcols)], "gather_ada")[0].reshape(8, depth, 16, wcols)
    ada_rows = jnp.concatenate([g2[2 * s] for s in range(4)], axis=-1)
    ada = lax.dynamic_index_in_dim(ada_rows, dev, axis=1, keepdims=False)
    ada_c = ada_rows[:, 8]

    def half_rows(a):
        h = a.shape[0] // 2
        return lax.dynamic_slice_in_dim(a, ci * h, h, axis=0)

    proj = ("w_in", "w_out", "w_up", "w_down")
    wparts = [{n: half_rows(W[n][l]).astype(BF16) for n in proj} for l in range(depth)]

    def stacked(n, g):
        g = g.reshape(4, -1, g.shape[-1])
        return g.reshape(1, -1, g.shape[-1]) if n in ("w_out", "w_down") else g

    Wg = [dict(zip(proj, [stacked(n, g) for n, g in
                          zip(proj, allgather8([wparts[0][n] for n in proj], "gather_weights_0", hbm=True))])), {}]
    late_a, late_b = ("w_in", "w_out", "w_down"), ("w_up",)

    xcat = jnp.concatenate([ctx[0], x[0]], axis=0)
    mods = []
    for l in range(depth):
        lat, con = jnp.split(ada[l], 6), jnp.split(ada_c[l], 6)
        mods.append(dict(sh1=_seg(con[0], lat[0]), sc1=_seg(con[1], lat[1]), g1=_seg(con[2], lat[2]),
                         sh2=_seg(con[3], lat[3]), sc2=_seg(con[4], lat[4]), g2=_seg(con[5], lat[5]),
                         ln1=_shared(ln1_w[l]), ln2=_shared(ln2_w[l])))
    bias_pull, saved = [], []
    x0 = xcat
    _, h = gate_norm(x0, None, None, mods[0]["ln1"], mods[0]["sh1"], mods[0]["sc1"], NC, R, "norm_in")
    for l in range(depth):
        md, wl = mods[l], Wg[l]
        u = mm_nn(h, wl["w_in"], F32, f"proj_in_{l}")
        lbf, lbb = lb_all[0, l][None], lb_all[1, l][None]
        o_fw, st_fw = hgrn_fwd(u, lbf, False, 0, NC, HGW, f"hgrn_fw_{l}")
        o_bw, st_bw = hgrn_fwd(u, lbb, True, 1, NC, HGW, f"hgrn_bw_{l}")
        hgn = hg_norm_w[l][None]
        hg = hg_read(o_fw, o_bw, u, hgn, gcol, R, f"hg_read_{l}")
        bias, pull = jax.vjp(lambda r: na_bias_tables(r, rows), na_rpb[l])
        bias_pull.append(pull)
        qn, kn, on = na_q_norm_w[l][None], na_k_norm_w[l][None], na_out_norm_w[l][None]
        keys_n, vals_b = kv_prep(u, kn, kcol, vcol, NAW, R, f"kv_prep_{l}")
        if l == 0:
            na, *late = na_fwd(u, keys_n, vals_b, qn, on, bias, qcol, NC, f"na_fwd_{l}",
                               comm=allgather8_comm([wparts[1][n] for n in late_a]))
            Wg[1].update({n: stacked(n, g) for n, g in zip(late_a, late)})
        else:
            na = na_fwd(u, keys_n, vals_b, qn, on, bias, qcol, NC, f"na_fwd_{l}")
        cvw_l, cvo = cvw_full[l], cv_out_norm_w[l][None]
        cv = short_conv(u, cvw_l, cvo, bcol, NC, CVW, R, name=f"short_conv_{l}")
        mix = jnp.concatenate([hg, na, cv], axis=1)
        m1 = mm_nn(mix, wl["w_out"], F32, f"proj_out_{l}")
        x1, h2 = gate_norm(x0, m1, md["g1"], md["ln2"], md["sh2"], md["sc2"], NC, R, f"gate_norm_mid_{l}")
        if l == 0:
            uf, *late = mm_nn(h2, wl["w_up"], F32, f"ffn_up_{l}", comm=allgather8_comm([wparts[1][n] for n in late_b]))
            Wg[1].update({n: stacked(n, g) for n, g in zip(late_b, late)})
        else:
            uf = mm_nn(h2, wl["w_up"], F32, f"ffn_up_{l}")
        fcw_l, fcb_l = fcw_full[l], ffn_conv_b[l][None]
        a = ffn_mid(uf, fcw_l, fcb_l, NC, R, FW, name=f"ffn_mid_{l}")
        m2 = mm_nn(a, wl["w_down"], F32, f"ffn_down_{l}")
        saved.append(dict(x0=x0, h=h, u=u, o_fw=o_fw, o_bw=o_bw, st_fw=st_fw, st_bw=st_bw, bias=bias, mix=mix,
                          m1=m1, x1=x1, h2=h2, uf=uf, a=a, m2=m2, lbf=lbf, lbb=lbb, keys_n=keys_n, vals_b=vals_b))
        if l + 1 < depth:
            nx = mods[l + 1]
            x0, h = gate_norm(x1, m2, md["g2"], nx["ln1"], nx["sh1"], nx["sc1"], NC, R, f"gate_norm_end_{l}")
    sv, md = saved[-1], mods[-1]
    loss_terms, d_x1, d_m2, d_g2 = gate_loss(sv["x1"], sv["m2"], md["g2"], loss_target[0], NC, R, "gate_loss")
    loss = lax.psum(jnp.sum(loss_terms), ("x", "y", "c"))

    big_grads = [dict() for _ in range(depth)]
    core = ci.astype(jnp.int32).reshape(1)
    pairs, quads = [None] * depth, [None] * depth
    small = [dict() for _ in range(depth)]
    d_ada = [None] * depth
    d_lb = [None] * depth
    for l in reversed(range(depth)):
        sv, md, wl = saved[l], mods[l], Wg[l]
        u, uf = sv["u"], sv["uf"]
        big_grads[l]["w_down"] = mm_tn(sv["a"], d_m2, 1, BF16, f"grad_w_down_{l}").reshape(4, F // 4, D)
        d_a = mm_nt(d_m2, wl["w_down"], F32, f"ffn_down_bwd_{l}")
        fcw_l, fcb_l = fcw_full[l], ffn_conv_b[l][None]
        dyg, dyv, dbg, dbv = ffn_mid(uf, fcw_l, fcb_l, NC, R, FW, da=d_a, name=f"ffn_mid_bwd_{l}")
        dug, dwg = conv3_bwd(dyg, uf, fcw_l, NC, R, FW, col0=0, out_dtype=BF16, name=f"ffn_conv_bwd_gate_{l}")
        duv, dwv = conv3_bwd(dyv, uf, fcw_l, NC, R, FW, col0=F // FW, out_dtype=BF16, name=f"ffn_conv_bwd_val_{l}")
        d_uf = jnp.concatenate([dug, duv], axis=1)
        small[l]["ffn_conv_w"] = jnp.concatenate([dwg[:3], dwv[:3]], axis=1)
        small[l]["ffn_conv_b"] = jnp.concatenate([dbg[0], dbv[0]])
        big_grads[l]["w_up"] = mm_tn(sv["h2"], d_uf, 4, BF16, f"grad_w_up_{l}")
        d_h2 = mm_nt(d_uf, wl["w_up"], F32, f"ffn_up_bwd_{l}")
        d_x0, d_m1, dg1, dln2, dsh2, dsc2 = gate_norm_bwd(sv["x0"], sv["m1"], md["g1"], md["ln2"], md["sh2"], md["sc2"],
                                                          d_x1, d_h2, NC, R, f"gate_norm_mid_bwd_{l}")
        big_grads[l]["w_out"] = mm_tn(sv["mix"], d_m1, 1, BF16, f"grad_w_out_{l}").reshape(4, MIX // 4, D)
        d_mix = mm_nt(d_m1, wl["w_out"], F32, f"proj_out_bwd_{l}")
        hgn = hg_norm_w[l][None]
        d_o, d_hgg, d_hgn = hg_read_bwd(sv["o_fw"], sv["o_bw"], u, hgn, d_mix, gcol, 0, R, f"hg_read_bwd_{l}")
        dzf, dvf, dqf, dlbf = hgrn_bwd(u, sv["lbf"], sv["st_fw"], d_o, False, 0, NC, HGW, f"hgrn_fw_bwd_{l}")
        dzb, dvb, dqb, dlbb = hgrn_bwd(u, sv["lbb"], sv["st_bw"], d_o, True, 1, NC, HGW, f"hgrn_bw_bwd_{l}")
        d_lb[l] = (dlbf[0], dlbb[0])
        qn, kn, on = na_q_norm_w[l][None], na_k_norm_w[l][None], na_out_norm_w[l][None]
        na_out = na_bwd(u, sv["keys_n"], sv["vals_b"], qn, on, sv["bias"], d_mix, qcol, mix_na, NC, f"na_bwd_{l}",
                        comm=chip_alltoall_comm(pairs[1]) if l == 0 else None)
        d_nq, d_keys_n, d_nv, d_bias, d_qn, d_on = na_out[:6]
        if l == 0:
            quads[1] = na_out[6:]
        d_nk, d_kn = kv_prep_bwd(u, kn, d_keys_n, kcol, NAW, R, f"kv_prep_bwd_{l}")
        cvw_l, cvo = cvw_full[l], cv_out_norm_w[l][None]
        d_cb, d_cy, d_cvo = short_conv(u, cvw_l, cvo, bcol, NC, CVW, R, bwd_dout=d_mix, ocol=mix_cv,
                                       name=f"short_conv_bwd_{l}")
        d_cc, d_cvv, d_cvw = conv3_bwd(d_cy, u, cvw_l, NC, R, LANE, prod_cols=(bcol + nh_cv, bcol + 2 * nh_cv),
                                       name=f"short_conv_taps_bwd_{l}")
        d_u = jnp.concatenate([dzf, dzb, dvf + dvb, d_nk, d_nv, dqf + dqb, d_hgg, d_nq, d_cb, d_cc, d_cvv],
                              axis=1).astype(BF16)
        big_grads[l]["w_in"] = mm_tn(sv["h"], d_u, 4, BF16, f"grad_w_in_{l}")
        d_h = mm_nt(d_u, wl["w_in"], F32, f"proj_in_bwd_{l}")
        small[l].update(hg_norm_w=d_hgn.sum(0)[0], na_q_norm_w=d_qn.sum(0)[0], na_k_norm_w=d_kn.sum(0)[0],
                        na_out_norm_w=d_on.reshape(-1), na_rpb=bias_pull[l](d_bias)[0], cv_w=d_cvw[:3],
                        cv_out_norm_w=d_cvo.reshape(-1), ln2_w=dln2.sum((0, 1)))
        if l > 0:
            pv, pm = saved[l - 1], mods[l - 1]
            d_x1, d_m2, dg2_prev, dln1, dsh1, dsc1 = gate_norm_bwd(pv["x1"], pv["m2"], pm["g2"], md["ln1"], md["sh1"],
                                                                   md["sc1"], d_x0, d_h, NC, R, f"gate_norm_end_bwd_{l - 1}")
        else:
            d_xin, _, _, dln1, dsh1, dsc1 = gate_norm_bwd(sv["x0"], None, None, md["ln1"], md["sh1"], md["sc1"], d_x0, d_h,
                                                          NC, R, "norm_in_bwd")
        small[l]["ln1_w"] = dln1.sum((0, 1))
        this_g2 = d_g2
        vecs = [v.sum(1) for v in (dsh1, dsc1, dg1, dsh2, dsc2, this_g2)]
        d_ada[l] = jnp.stack([jnp.concatenate([v[s] for v in vecs]) for s in (0, 1)])
        if l > 0:
            d_g2 = dg2_prev
        parts = [big_grads[l][n] for n in proj]
        got = swap_halves(parts, f"reduce_sibling_{l}")
        pairs[l] = [pair_sum(p, g, core, f"reduce_pair_sum_{n}_{l}") for n, p, g in zip(proj, parts, got)]
    grad_x = d_xin[NC:][None]
    d_logits = lb_pull(jnp.stack([jnp.stack([d_lb[l][k] for l in range(depth)]) for k in (0, 1)]))[0]

    rep_names = ("ln1_w", "ln2_w", "hg_norm_w", "na_q_norm_w", "na_k_norm_w", "na_rpb", "na_out_norm_w",
                 "cv_out_norm_w", "ffn_conv_b", "cv_w", "ffn_conv_w")
    parts3 = [jnp.stack([small[l][n] for l in range(depth)]) for n in rep_names]
    parts3 += [d_logits, jnp.stack([d_ada[l][0] for l in range(depth)]), jnp.stack([d_ada[l][1] for l in range(depth)])]
    buf3, lay3 = _flat_rows(parts3, F32)
    g3 = allgather8([buf3], "gather_small_grads")[0]
    tot3 = _unflat(sum_leading(g3, F32, "sum_small_grads"), lay3)
    gsm = dict(zip(rep_names, tot3[:len(rep_names)]))
    gsm["hg_lb_logits"] = tot3[len(rep_names)]
    dctx_tot, dlat_tot = tot3[-2], tot3[-1]
    dlat_each = jnp.stack([_unflat(g3[d], lay3)[-1] for d in range(8)], axis=1)
    grads = {n: gsm[n].reshape(W[n].shape) for n in rep_names if n not in SHARDED_SMALL}
    for n in SHARDED_SMALL:
        wl_ = W[n].shape[-1]
        grads[n] = lax.dynamic_slice_in_dim(gsm[n], chip * wl_, wl_, axis=gsm[n].ndim - 1)
    grads["b_ada"] = dctx_tot + dlat_tot

    ds16 = jnp.zeros((16, D), F32)
    gw_ada = []
    for l in range(depth):
        dm = jnp.concatenate([dlat_each[l], dctx_tot[l][None], jnp.zeros((7, ADA), F32)])
        dm = lax.dynamic_slice_in_dim(dm, chip * wcols, wcols, axis=1)
        gw_ada.append(mm_tn(s16, dm, 1, F32, f"grad_w_ada_{l}")[0])
        ds16 = ds16 + mm_nt(dm, w_ada[l][None], F32, f"ada_bwd_{l}")
    g4 = allgather8([ds16[8:16]], "gather_cond_grad")[0]
    d_scc = g4[0, 0] + g4[2, 0] + g4[4, 0] + g4[6, 0]
    sg = jax.nn.sigmoid(c_ctx)
    grads["c_ctx"] = d_scc * (sg * (1.0 + c_ctx * (1.0 - sg)))

    keys = [(l, n) for l in range(depth) for n in proj]
    quads[0] = chip_alltoall(pairs[0], "reduce_chips_0")
    mine = [sum_leading(q, F32, f"reduce_chip_sum_{n}_{l}") for (l, n), q in zip(keys, list(quads[0]) + list(quads[1]))]
    other = share_halves(mine, "reduce_share")
    mine_by, other_by = {n: [None] * depth for n in proj}, {n: [None] * depth for n in proj}
    for (l, n), a, b in zip(keys, mine, other):
        mine_by[n][l], other_by[n][l] = a, b

    delta, new_m, new_v = {}, {}, {}
    for n in BIG:
        shp = W[n].shape
        two = lambda a: a.reshape(-1, shp[-1])
        if n == "w_ada":
            g_, d_, m_, v_ = adamw(two(W[n]), gw_ada, two(Mo[n]), two(Vo[n]), f"adamw_{n}")
        else:
            g_, d_, m_, v_ = adamw_halves(two(W[n]), mine_by[n], other_by[n], two(Mo[n]), two(Vo[n]), core, f"adamw_{n}")
        grads[n], delta[n], new_m[n], new_v[n] = g_.reshape(shp), d_.reshape(shp), m_.reshape(shp), v_.reshape(shp)
    smalls = [n for n in WEIGHTS if n not in BIG]
    pw, lay_s = _flat_rows([W[n] for n in smalls], F32)
    pg, _ = _flat_rows([grads[n] for n in smalls], F32)
    pm, _ = _flat_rows([Mo[n] for n in smalls], F32)
    pvv, _ = _flat_rows([Vo[n] for n in smalls], F32)
    _, d_, m_, v_ = adamw(pw, [pg], pm, pvv, "adamw_small")
    for n, dd, mm_, vv in zip(smalls, _unflat(d_, lay_s), _unflat(m_, lay_s), _unflat(v_, lay_s)):
        delta[n], new_m[n], new_v[n] = dd, mm_, vv

    return (loss, grad_x, *[grads[n] for n in WEIGHTS], *[delta[n] for n in WEIGHTS],
            *[new_m[n] for n in WEIGHTS], *[new_v[n] for n in WEIGHTS])
```

```python
import functools
import math

import numpy as np
import jax
import jax.numpy as jnp
from jax import lax
from jax.experimental import pallas as pl
from jax.experimental.pallas import tpu as pltpu

F32 = jnp.float32
BF16 = jnp.bfloat16
MESH = pl.DeviceIdType.MESH
ANY = pl.BlockSpec(memory_space=pl.ANY)
VMEM_SPEC = pl.BlockSpec(memory_space=pltpu.VMEM)

LANE = 128
CHUNK = 64
SUB = 16
GRID_W = 64
WIN_R = 8
WIN_C = 16
EPS = 1e-6
F_FLOOR = 1e-30
NEG_INF = -1e30
EXP_CLAMP = 80.0
ATTN_SCALE = LANE ** -0.5
VMEM_LIMIT = 56 * 1024 * 1024
ADAM_LR, ADAM_B1, ADAM_B2, ADAM_EPS, ADAM_WD, ADAM_STEP = 0.001, 0.9, 0.999, 1e-08, 0.01, 10


def _cp(*sem):
    return pltpu.CompilerParams(dimension_semantics=sem or None, vmem_limit_bytes=VMEM_LIMIT)


def _me():
    return lax.axis_index("x"), lax.axis_index("y"), lax.axis_index("c")


def allgather8(blocks, name, hbm=False):
    na = len(blocks)
    comm = allgather8_comm(blocks)

    def body(*refs):
        comm["start"](refs[:na], refs[na:2 * na], refs[2 * na:])
        comm["finish"](refs[:na], refs[na:2 * na], refs[2 * na:])

    spec = ANY if hbm else VMEM_SPEC
    return pl.pallas_call(
        body, name=name, out_shape=comm["outs"], in_specs=[spec] * na, out_specs=[spec] * na,
        scratch_shapes=comm["scratch"], compiler_params=pltpu.CompilerParams(vmem_limit_bytes=VMEM_LIMIT),
    )(*blocks)


def allgather8_comm(blocks):
    na = len(blocks)

    def parts(x_refs, out_refs, sems):
        send_sems, recv_sems, local_sems = sems
        x, y, c = _me()
        me, sibling = (x, y, c), (x, y, 1 - c)
        chips = [(1 - x, y), (x, 1 - y), (1 - x, 1 - y)]

        def rows(a, px, py, pc):
            return out_refs[a].at[4 * px + 2 * py + pc]

        def copy(a, k, blk, to, src=None):
            return pltpu.make_async_remote_copy(
                src_ref=rows(a, *blk) if src is None else src, dst_ref=rows(a, *blk),
                send_sem=send_sems.at[a, k], recv_sem=recv_sems.at[a, k], device_id=to, device_id_type=MESH)

        mine = [pltpu.make_async_copy(x_refs[a], rows(a, *me), local_sems.at[a]) for a in range(na)]
        first = []
        for a in range(na):
            first.append(copy(a, 0, me, sibling, src=x_refs[a]))
            first += [copy(a, 1 + j, me, (*chip, c), src=x_refs[a]) for j, chip in enumerate(chips)]
        return c, me, sibling, chips, copy, mine, first

    def start(x_refs, out_refs, sems):
        _, _, _, _, _, mine, first = parts(x_refs, out_refs, sems)
        for cp in mine + first:
            cp.start()

    def finish(x_refs, out_refs, sems):
        c, me, sibling, chips, copy, mine, first = parts(x_refs, out_refs, sems)
        passed = []
        for j, chip in enumerate(chips):
            for a in range(na):
                copy(a, 1 + j, (*chip, c), me).wait_recv()
                passed.append(copy(a, 4 + j, (*chip, c), sibling))
                passed[-1].start()
        for a in range(na):
            copy(a, 0, sibling, me).wait_recv()
            for j, chip in enumerate(chips):
                copy(a, 4 + j, (*chip, 1 - c), me).wait_recv()
        for cp in first + passed:
            cp.wait_send()
        for cp in mine:
            cp.wait()

    return dict(ins=list(blocks), outs=[jax.ShapeDtypeStruct((8,) + b.shape, b.dtype) for b in blocks],
                scratch=[pltpu.SemaphoreType.DMA((na, 7)), pltpu.SemaphoreType.DMA((na, 7)),
                         pltpu.SemaphoreType.DMA((na,))], start=start, finish=finish)


def swap_halves(gs, name):
    na = len(gs)
    hrs = [g.shape[1] // 2 for g in gs]

    def body(*refs):
        g_refs, o_refs = refs[:na], refs[na:2 * na]
        send_sems, recv_sems = refs[2 * na:]
        x, y, c = _me()
        cps = []
        for a in range(na):
            for s in range(4):
                src = g_refs[a].at[s, pl.ds(pl.multiple_of((1 - c) * hrs[a], 16), hrs[a]), :]
                cps.append(pltpu.make_async_remote_copy(
                    src_ref=src, dst_ref=o_refs[a].at[s], send_sem=send_sems.at[a, s], recv_sem=recv_sems.at[a, s],
                    device_id=(x, y, 1 - c), device_id_type=MESH))
        for cp in cps:
            cp.start()
        for cp in cps:
            cp.wait()

    return pl.pallas_call(
        body, name=name, out_shape=[jax.ShapeDtypeStruct((4, hrs[a], gs[a].shape[2]), gs[a].dtype) for a in range(na)],
        in_specs=[ANY] * na, out_specs=[ANY] * na,
        scratch_shapes=[pltpu.SemaphoreType.DMA((na, 4)), pltpu.SemaphoreType.DMA((na, 4))],
    )(*gs)


def chip_alltoall(gs, name):
    na = len(gs)
    comm = chip_alltoall_comm(gs)

    def body(*refs):
        comm["start"](refs[:na], refs[na:2 * na], refs[2 * na:])
        comm["finish"](refs[:na], refs[na:2 * na], refs[2 * na:])

    return pl.pallas_call(body, name=name, out_shape=comm["outs"], in_specs=[ANY] * na, out_specs=[ANY] * na,
                          scratch_shapes=comm["scratch"])(*gs)


def chip_alltoall_comm(gs):
    na = len(gs)

    def copies(g_refs, o_refs, sems):
        send_sems, recv_sems, local_sems = sems
        x, y, c = _me()
        mine = 2 * x + y
        cps = []
        for a in range(na):
            cps.append(pltpu.make_async_copy(g_refs[a].at[mine], o_refs[a].at[mine], local_sems.at[a]))
            for k, (px, py) in enumerate([(1 - x, y), (x, 1 - y), (1 - x, 1 - y)]):
                cps.append(pltpu.make_async_remote_copy(
                    src_ref=g_refs[a].at[2 * px + py], dst_ref=o_refs[a].at[mine], send_sem=send_sems.at[a, k],
                    recv_sem=recv_sems.at[a, k], device_id=(px, py, c), device_id_type=MESH))
        return cps

    def start(g_refs, o_refs, sems):
        for cp in copies(g_refs, o_refs, sems):
            cp.start()

    def finish(g_refs, o_refs, sems):
        for cp in copies(g_refs, o_refs, sems):
            cp.wait()

    return dict(ins=list(gs), outs=[jax.ShapeDtypeStruct(g.shape, g.dtype) for g in gs],
                scratch=[pltpu.SemaphoreType.DMA((na, 3)), pltpu.SemaphoreType.DMA((na, 3)),
                         pltpu.SemaphoreType.DMA((na,))], start=start, finish=finish)


def share_halves(vs, name):
    na = len(vs)

    def body(*refs):
        v_refs, o_refs = refs[:na], refs[na:2 * na]
        send_sems, recv_sems = refs[2 * na:]
        x, y, c = _me()
        cps = [pltpu.make_async_remote_copy(
            src_ref=v_refs[a], dst_ref=o_refs[a], send_sem=send_sems.at[a], recv_sem=recv_sems.at[a],
            device_id=(x, y, 1 - c), device_id_type=MESH) for a in range(na)]
        for cp in cps:
            cp.start()
        for cp in cps:
            cp.wait()

    return pl.pallas_call(
        body, name=name, out_shape=[jax.ShapeDtypeStruct(v.shape, v.dtype) for v in vs],
        in_specs=[ANY] * na, out_specs=[ANY] * na,
        scratch_shapes=[pltpu.SemaphoreType.DMA((na,)), pltpu.SemaphoreType.DMA((na,))],
    )(*vs)


def _row_block(rows, cap):
    rb = math.gcd(rows, cap)
    return rb if rb % 8 == 0 else rows


def sum_leading(x, out_dtype, name):
    n, r, c = x.shape
    rb = _row_block(r, 1024)

    def body(x_ref, o_ref):
        acc = x_ref[0].astype(F32)
        for k in range(1, n):
            acc = acc + x_ref[k].astype(F32)
        o_ref[...] = acc.astype(o_ref.dtype)

    return pl.pallas_call(
        body, name=name, grid=(r // rb,), out_shape=jax.ShapeDtypeStruct((r, c), out_dtype),
        in_specs=[pl.BlockSpec((n, rb, c), lambda i: (0, i, 0))], out_specs=pl.BlockSpec((rb, c), lambda i: (i, 0)),
        compiler_params=_cp("parallel"),
    )(x)


def pair_sum(g, got, core, name):
    _, r2, n = g.shape
    hr = r2 // 2
    rb = math.gcd(hr, 512)
    nb = hr // rb

    def body(c_ref, a_ref, b_ref, o_ref):
        o_ref[...] = (a_ref[...].astype(F32) + b_ref[...].astype(F32)).astype(o_ref.dtype)

    spec = pl.BlockSpec((None, rb, n), lambda s, i, c_ref: (s, i, 0))
    return pl.pallas_call(
        body, name=name, out_shape=jax.ShapeDtypeStruct((4, hr, n), g.dtype),
        grid_spec=pltpu.PrefetchScalarGridSpec(
            num_scalar_prefetch=1, grid=(4, nb),
            in_specs=[pl.BlockSpec((None, rb, n), lambda s, i, c_ref: (s, c_ref[0] * nb + i, 0)), spec],
            out_specs=spec),
        compiler_params=_cp("parallel", "parallel"),
    )(core, g, got)


def adamw(w, gs, m, v, name):
    rows, c = w.shape
    ng = len(gs)
    r = rows // ng
    rb = _row_block(r, 128 if c > 2048 else 256 if c > 1024 else 1024)
    nb = r // rb
    bc1 = 1.0 - ADAM_B1 ** ADAM_STEP
    bc2 = 1.0 - ADAM_B2 ** ADAM_STEP

    def body(w_ref, *refs):
        g_refs, (m_ref, v_ref, g_out, d_ref, nm_ref, nv_ref) = refs[:ng], refs[ng:]
        part = pl.program_id(0) // nb
        gg = g_refs[0][...]
        for k in range(1, ng):
            gg = jnp.where(part == k, g_refs[k][...], gg)
        nm = ADAM_B1 * m_ref[...] + (1.0 - ADAM_B1) * gg
        nv = ADAM_B2 * v_ref[...] + (1.0 - ADAM_B2) * (gg * gg)
        g_out[...] = gg
        d_ref[...] = -ADAM_LR * ((nm / bc1) / (jnp.sqrt(nv / bc2) + ADAM_EPS) + ADAM_WD * w_ref[...])
        nm_ref[...] = nm
        nv_ref[...] = nv

    spec = pl.BlockSpec((rb, c), lambda i: (i, 0))
    gspecs = [pl.BlockSpec((rb, c), functools.partial(lambda k, i: (jnp.clip(i - k * nb, 0, nb - 1), 0), k))
              for k in range(ng)]
    sds = jax.ShapeDtypeStruct((rows, c), F32)
    return pl.pallas_call(
        body, name=name, grid=(ng * nb,), out_shape=(sds,) * 4, in_specs=[spec] + gspecs + [spec, spec],
        out_specs=(spec,) * 4, compiler_params=_cp("parallel"),
    )(w, *gs, m, v)


def adamw_halves(w, mine, other, m, v, core, name):
    rows, c = w.shape
    nl = len(mine)
    hr = rows // (2 * nl)
    rb = _row_block(hr, 128 if c > 2048 else 256 if c > 1024 else 1024)
    nb = hr // rb
    bc1 = 1.0 - ADAM_B1 ** ADAM_STEP
    bc2 = 1.0 - ADAM_B2 ** ADAM_STEP

    def body(c_ref, w_ref, *refs):
        mine_refs, other_refs = refs[:nl], refs[nl:2 * nl]
        m_ref, v_ref, g_out, d_ref, nm_ref, nv_ref = refs[2 * nl:]
        part = pl.program_id(0) // nb
        layer, half = part // 2, part % 2
        gg = jnp.where(half == c_ref[0], mine_refs[0][...], other_refs[0][...])
        for k in range(1, nl):
            gg = jnp.where(layer == k, jnp.where(half == c_ref[0], mine_refs[k][...], other_refs[k][...]), gg)
        nm = ADAM_B1 * m_ref[...] + (1.0 - ADAM_B1) * gg
        nv = ADAM_B2 * v_ref[...] + (1.0 - ADAM_B2) * (gg * gg)
        g_out[...] = gg
        d_ref[...] = -ADAM_LR * ((nm / bc1) / (jnp.sqrt(nv / bc2) + ADAM_EPS) + ADAM_WD * w_ref[...])
        nm_ref[...] = nm
        nv_ref[...] = nv

    spec = pl.BlockSpec((rb, c), lambda i, c_ref: (i, 0))
    gspecs = [pl.BlockSpec((rb, c), functools.partial(
        lambda k, i, c_ref: (jnp.clip(i - 2 * k * nb, 0, 2 * nb - 1) % nb, 0), k)) for k in range(nl)]
    sds = jax.ShapeDtypeStruct((rows, c), F32)
    return pl.pallas_call(
        body, name=name, out_shape=(sds,) * 4,
        grid_spec=pltpu.PrefetchScalarGridSpec(
            num_scalar_prefetch=1, grid=(2 * nl * nb,), in_specs=[spec] + gspecs + gspecs + [spec, spec],
            out_specs=(spec,) * 4),
        compiler_params=_cp("parallel"),
    )(core, w, *mine, *other, m, v)


def _pick(n, prefs):
    for p in prefs:
        if n % p == 0:
            return p
    return n


def _hosted(body, n_in, n_out, comm, first, last):
    if comm is None:
        return body
    k, ns = len(comm["ins"]), len(comm["scratch"])

    def wrapped(*refs):
        ins, cins = refs[:n_in], refs[n_in:n_in + k]
        outs, couts = refs[n_in + k:n_in + k + n_out], refs[n_in + k + n_out:n_in + 2 * k + n_out]
        rest = refs[n_in + 2 * k + n_out:]
        scratch, sems = rest[:len(rest) - ns], rest[len(rest) - ns:]

        @pl.when(first())
        def _():
            comm["start"](cins, couts, sems)

        body(*ins, *outs, *scratch)

        @pl.when(last())
        def _():
            comm["finish"](cins, couts, sems)

    return wrapped


def _comm_extras(comm):
    if comm is None:
        return [], [], [], []
    return list(comm["ins"]), [ANY] * len(comm["ins"]), list(comm["outs"]), list(comm["scratch"])


def _mm_body(dims, nk, out_dtype):
    def body(a_ref, b_ref, o_ref, acc=None):
        kk = pl.program_id(2)
        part = lax.dot_general(a_ref[...].astype(BF16), b_ref[...].astype(BF16), (dims, ((), ())),
                               preferred_element_type=F32)
        if nk == 1:
            o_ref[...] = part.astype(out_dtype)
        else:
            @pl.when(kk == 0)
            def _():
                acc[...] = part

            @pl.when(kk > 0)
            def _():
                acc[...] += part

            @pl.when(kk == nk - 1)
            def _():
                o_ref[...] = acc[...].astype(out_dtype)
    return body


def _acc(nk, shape):
    return [pltpu.VMEM(shape, F32)] if nk > 1 else []


def mm_nn(a, w, out_dtype, name, comm=None):
    M, K = a.shape
    S, _, Ns = w.shape
    tm = _pick(M, (1088, 1024, 512, 256, 128))
    tn = _pick(Ns, (1024, 896, 1408, 512, 256, 128))
    tk = _pick(K, (2816, 2048, 1408, 1024, 512, 256, 128))
    nps, nk = Ns // tn, K // tk
    grid = (S * nps, M // tm, nk)
    ids = lambda: [pl.program_id(d) for d in range(3)]
    first = lambda: functools.reduce(jnp.logical_and, [p == 0 for p in ids()])
    last = lambda: functools.reduce(jnp.logical_and, [p == g - 1 for p, g in zip(ids(), grid)])
    cin, cspec, cout, csem = _comm_extras(comm)
    out = pl.pallas_call(
        _hosted(_mm_body(((1,), (0,)), nk, out_dtype), 2, 1, comm, first, last), name=name, grid=grid,
        out_shape=[jax.ShapeDtypeStruct((M, S * Ns), out_dtype)] + cout,
        in_specs=[pl.BlockSpec((tm, tk), lambda j, i, k: (i, k)),
                  pl.BlockSpec((None, tk, tn), lambda j, i, k: (j // nps, k, j % nps))] + cspec,
        out_specs=[pl.BlockSpec((tm, tn), lambda j, i, k: (i, j))] + cspec,
        scratch_shapes=_acc(nk, (tm, tn)) + csem,
        compiler_params=_cp(*(("arbitrary",) * 3 if comm else ("parallel", "parallel", "arbitrary"))),
    )(a, w, *cin)
    return out[0] if comm is None else out


def mm_nt(dy, w, out_dtype, name):
    M, N = dy.shape
    S, K, Ns = w.shape
    tm = _pick(M, (1088, 1024, 512, 256, 128))
    tn = _pick(K, (1408, 1024, 512, 256, 128))
    tk = _pick(Ns, (2816, 2048, 1792, 1408, 1024, 896, 512, 256, 128))
    kps, nk = Ns // tk, N // tk
    return pl.pallas_call(
        _mm_body(((1,), (1,)), nk, out_dtype), name=name, grid=(K // tn, M // tm, nk),
        out_shape=jax.ShapeDtypeStruct((M, K), out_dtype),
        in_specs=[pl.BlockSpec((tm, tk), lambda j, i, k: (i, k)),
                  pl.BlockSpec((None, tn, tk), lambda j, i, k: (k // kps, j, k % kps))],
        out_specs=pl.BlockSpec((tm, tn), lambda j, i, k: (i, j)),
        scratch_shapes=_acc(nk, (tm, tn)), compiler_params=_cp("parallel", "parallel", "arbitrary"),
    )(dy, w)


def mm_tn(a, dy, S, out_dtype, name):
    M, K = a.shape
    N = dy.shape[1]
    Ns = N // S
    to = _pick(K, (1024, 512, 256, 128))
    tn = _pick(Ns, (2816, 2048, 1792, 1408, 1024, 896, 512, 256, 128))
    tk = _pick(M, (1088, 1024, 512, 256, 128))
    nps, nk = Ns // tn, M // tk
    return pl.pallas_call(
        _mm_body(((0,), (0,)), nk, out_dtype), name=name, grid=(K // to, S * nps, nk),
        out_shape=jax.ShapeDtypeStruct((S, K, Ns), out_dtype),
        in_specs=[pl.BlockSpec((tk, to), lambda i, j, k: (k, i)),
                  pl.BlockSpec((tk, tn), lambda i, j, k: (k, j))],
        out_specs=pl.BlockSpec((None, to, tn), lambda i, j, k: (j // nps, i, j % nps)),
        scratch_shapes=_acc(nk, (to, tn)), compiler_params=_cp("parallel", "parallel", "arbitrary"),
    )(a, dy)


_DIMS = {"nn": ((1,), (0,)), "nt": ((1,), (1,)), "tn": ((0,), (0,))}


def _dot(a, b, mode):
    return lax.dot_general(a.astype(BF16), b.astype(BF16), (_DIMS[mode], ((), ())), preferred_element_type=F32)


@functools.partial(jax.custom_vjp, nondiff_argnums=(2,))
def mmf(a, b, mode):
    return _dot(a, b, mode)


def _mmf_fwd(a, b, mode):
    return _dot(a, b, mode), (a, b)


def _mmf_bwd(mode, res, ct):
    a, b = res
    if mode == "nn":
        return _dot(ct, b, "nt"), _dot(a, ct, "tn")
    if mode == "nt":
        return _dot(ct, b, "nn"), _dot(ct, a, "tn")
    return _dot(b, ct, "nt"), _dot(a, ct, "nn")


mmf.defvjp(_mmf_fwd, _mmf_bwd)


def _dot_hi(m, g):
    return jnp.dot(m, g, precision=lax.Precision.HIGHEST, preferred_element_type=F32)


@jax.custom_vjp
def cumdot(m, mt, g):
    return _dot_hi(m, g)


def _cumdot_fwd(m, mt, g):
    return _dot_hi(m, g), (m, mt)


def _cumdot_bwd(res, ct):
    m, mt = res
    return jnp.zeros_like(m), jnp.zeros_like(mt), _dot_hi(mt, ct)


cumdot.defvjp(_cumdot_fwd, _cumdot_bwd)


def _rms(x, w):
    return x * lax.rsqrt(jnp.mean(x * x, axis=-1, keepdims=True) + EPS) * w


def _silu(x):
    return x * jax.nn.sigmoid(x)


RT = 16


def _gn_math(has_gate, x, m, gate, lnw, shift, scale):
    xn = x + gate * m if has_gate else x
    h = _rms(xn, lnw) * (1.0 + scale) + shift
    return xn, h


def _seg_spec(width, ncb):
    return pl.BlockSpec((None, RT, width), lambda i: (jnp.minimum(i // ncb, 1), 0, 0))


def gate_norm(x, m, gate, lnw, shift, scale, nc, R, name):
    T, D = x.shape
    has_gate = m is not None
    ncb = nc // R

    def body(*refs):
        if has_gate:
            x_ref, m_ref, g_ref, w_ref, sh_ref, sc_ref, xn_ref, h_ref = refs
        else:
            x_ref, w_ref, sh_ref, sc_ref, h_ref = refs

        def step(t, carry):
            rows = pl.ds(pl.multiple_of(t * RT, RT), RT)
            xn, h = _gn_math(has_gate, x_ref[rows, :], m_ref[rows, :] if has_gate else None,
                             g_ref[...] if has_gate else None, w_ref[...], sh_ref[...], sc_ref[...])
            if has_gate:
                xn_ref[rows, :] = xn
            h_ref[rows, :] = h.astype(BF16)
            return carry

        lax.fori_loop(0, R // RT, step, 0)

    row = pl.BlockSpec((R, D), lambda i: (i, 0))
    seg = _seg_spec(D, ncb)
    shared = pl.BlockSpec((None, RT, D), lambda i: (0, 0, 0))
    if has_gate:
        ins, in_specs = (x, m, gate, lnw, shift, scale), [row, row, seg, shared, seg, seg]
        out_shape = (jax.ShapeDtypeStruct((T, D), F32), jax.ShapeDtypeStruct((T, D), BF16))
        out_specs = (row, row)
    else:
        ins, in_specs = (x, lnw, shift, scale), [row, shared, seg, seg]
        out_shape, out_specs = jax.ShapeDtypeStruct((T, D), BF16), row
    out = pl.pallas_call(body, name=name, grid=(T // R,), out_shape=out_shape, in_specs=in_specs,
                         out_specs=out_specs, compiler_params=_cp("parallel"))(*ins)
    return out if has_gate else (None, out)


def gate_norm_bwd(x, m, gate, lnw, shift, scale, dxn, dh, nc, R, name):
    T, D = x.shape
    has_gate = m is not None
    ncb = nc // R

    def body(*refs):
        if has_gate:
            (x_ref, m_ref, g_ref, w_ref, sh_ref, sc_ref, dxn_ref, dh_ref,
             dx_ref, dm_ref, dg_ref, dw_ref, dsh_ref, dsc_ref) = refs
        else:
            x_ref, w_ref, sh_ref, sc_ref, dxn_ref, dh_ref, dx_ref, dw_ref, dsh_ref, dsc_ref = refs
        i = pl.program_id(0)

        @pl.when(i == 0)
        def _():
            dw_ref[...] = jnp.zeros_like(dw_ref)

        @pl.when((i == 0) | (i == ncb))
        def _():
            dsh_ref[...] = jnp.zeros_like(dsh_ref)
            dsc_ref[...] = jnp.zeros_like(dsc_ref)
            if has_gate:
                dg_ref[...] = jnp.zeros_like(dg_ref)

        def step(t, carry):
            rows = pl.ds(pl.multiple_of(t * RT, RT), RT)
            ct = (dxn_ref[rows, :], dh_ref[rows, :])
            if has_gate:
                _, vjp = jax.vjp(functools.partial(_gn_math, True), x_ref[rows, :], m_ref[rows, :], g_ref[...],
                                 w_ref[...], sh_ref[...], sc_ref[...])
                dx, dm, dg, dw, dsh, dsc = vjp(ct)
                dm_ref[rows, :] = dm.astype(BF16)
                dg_ref[...] += dg
            else:
                f = lambda x_, w_, sh_, sc_: _gn_math(False, x_, None, None, w_, sh_, sc_)[1]
                _, vjp = jax.vjp(f, x_ref[rows, :], w_ref[...], sh_ref[...], sc_ref[...])
                dx, dw, dsh, dsc = vjp(ct[1])
                dx = dx + ct[0]
            dx_ref[rows, :] = dx
            dw_ref[...] += dw
            dsh_ref[...] += dsh
            dsc_ref[...] += dsc
            return carry

        lax.fori_loop(0, R // RT, step, 0)

    row = pl.BlockSpec((R, D), lambda i: (i, 0))
    seg = _seg_spec(D, ncb)
    shared = pl.BlockSpec((None, RT, D), lambda i: (0, 0, 0))
    full, segs, one = jax.ShapeDtypeStruct((T, D), F32), jax.ShapeDtypeStruct((2, RT, D), F32), \
        jax.ShapeDtypeStruct((1, RT, D), F32)
    if has_gate:
        ins = (x, m, gate, lnw, shift, scale, dxn, dh)
        in_specs = [row, row, seg, shared, seg, seg, row, row]
        out_shape = (full, jax.ShapeDtypeStruct((T, D), BF16), segs, one, segs, segs)
        out_specs = (row, row, seg, shared, seg, seg)
    else:
        ins = (x, lnw, shift, scale, dxn, dh)
        in_specs = [row, shared, seg, seg, row, row]
        out_shape = (full, one, segs, segs)
        out_specs = (row, shared, seg, seg)
    out = pl.pallas_call(body, name=name, grid=(T // R,), out_shape=out_shape, in_specs=in_specs,
                         out_specs=out_specs, compiler_params=_cp("arbitrary"))(*ins)
    if has_gate:
        return out
    dx, dw, dsh, dsc = out
    return dx, None, None, dw, dsh, dsc


def gate_loss(x, m, gate, target, nc, R, name):
    T, D = x.shape
    ncb = nc // R

    def body(x_ref, m_ref, g_ref, t_ref, loss_ref, dx_ref, dm_ref, dg_ref):
        i = pl.program_id(0)

        @pl.when(i == 0)
        def _():
            loss_ref[...] = jnp.zeros_like(loss_ref)

        @pl.when((i == 0) | (i == ncb))
        def _():
            dg_ref[...] = jnp.zeros_like(dg_ref)

        live = jnp.where(i >= ncb, 1.0, 0.0).astype(F32)

        def step(t, carry):
            rows = pl.ds(pl.multiple_of(t * RT, RT), RT)
            mm_ = m_ref[rows, :]
            g = g_ref[...]
            e = (x_ref[rows, :] + g * mm_ - t_ref[rows, :]) * live
            dy = e * (1.0 / D)
            loss_ref[...] += 0.5 * e * dy
            dx_ref[rows, :] = dy
            dm_ref[rows, :] = (dy * g).astype(BF16)
            dg_ref[...] += dy * mm_
            return carry

        lax.fori_loop(0, R // RT, step, 0)

    row = pl.BlockSpec((R, D), lambda i: (i, 0))
    seg = _seg_spec(D, ncb)
    return pl.pallas_call(
        body, name=name, grid=(T // R,),
        out_shape=(jax.ShapeDtypeStruct((RT, D), F32), jax.ShapeDtypeStruct((T, D), F32),
                   jax.ShapeDtypeStruct((T, D), BF16), jax.ShapeDtypeStruct((2, RT, D), F32)),
        in_specs=[row, row, seg, pl.BlockSpec((R, D), lambda i: (jnp.maximum(i - ncb, 0), 0))],
        out_specs=(pl.BlockSpec((RT, D), lambda i: (0, 0)), row, row, seg),
        compiler_params=_cp("arbitrary"),
    )(x, m, gate, target)


def _hg_chunk(rev, lb, z, iv, hq, st):
    f = lb + (1.0 - lb) * jax.nn.sigmoid(z)
    g = jnp.log(jnp.maximum(f, F_FLOOR))
    k = (1.0 - lb) * jax.nn.sigmoid(-z)
    q = _silu(hq)
    ri = lax.broadcasted_iota(jnp.int32, (CHUNK, CHUNK), 0)
    ci = lax.broadcasted_iota(jnp.int32, (CHUNK, CHUNK), 1)
    r1 = lax.broadcasted_iota(jnp.int32, (CHUNK, 1), 0)
    seen = (ci >= ri) if rev else (ci <= ri)
    seen_t = (ci <= ri) if rev else (ci >= ri)
    cum = cumdot(seen.astype(F32), seen_t.astype(F32), g)
    tot = jnp.sum(g, axis=0, keepdims=True)
    att = jnp.zeros((CHUNK, CHUNK), F32)
    ref_rows = jnp.zeros_like(g)
    refs = []
    for b in range(CHUNK // SUB):
        before = (r1 >= SUB * (b + 1)) if rev else (r1 < SUB * b)
        r_b = jnp.sum(jnp.where(before, g, 0.0), axis=0, keepdims=True)
        in_b = (r1 >= SUB * b) & (r1 < SUB * (b + 1))
        ref_rows = ref_rows + jnp.where(in_b, r_b, 0.0)
        refs.append(r_b)
    qd = q * jnp.exp(cum - ref_rows)
    for b in range(CHUNK // SUB):
        kd = k * jnp.exp(jnp.minimum(refs[b] - cum, EXP_CLAMP))
        in_b = (ri >= SUB * b) & (ri < SUB * (b + 1))
        att = att + jnp.where(in_b, mmf(qd, kd, "nt"), 0.0)
    att = jnp.where(seen, att, 0.0)
    o = mmf(att, iv, "nn") + mmf(q * jnp.exp(cum), st, "nt")
    st_new = st * jnp.exp(tot) + mmf(iv, k * jnp.exp(tot - cum), "tn")
    return st_new, o


def _hg_cid(rev, i, ncs, n):
    if not rev:
        return i
    return jnp.where(i < ncs, ncs - 1 - i, ncs + n - 1 - i)


def hgrn_fwd(u, lb, rev, zcol, nc, hgw, name):
    T = u.shape[0]
    n, ncs, nh = T // CHUNK, nc // CHUNK, hgw // LANE

    def body(z_ref, v_ref, q_ref, lb_ref, o_ref, s_ref, st):
        i = pl.program_id(0)

        @pl.when(i == 0)
        def _():
            st[...] = jnp.zeros_like(st)

        for h in range(nh):
            cols = slice(h * LANE, (h + 1) * LANE)
            s_ref[h] = st[h]
            s_new, o = _hg_chunk(rev, lb_ref[:, cols], z_ref[:, cols], v_ref[:, cols], q_ref[:, cols], st[h])
            st[h] = s_new
            o_ref[:, cols] = o

    def col(cb):
        return pl.BlockSpec((CHUNK, hgw), lambda i: (_hg_cid(rev, i, ncs, n), cb))

    return pl.pallas_call(
        body, name=name, grid=(n,),
        out_shape=(jax.ShapeDtypeStruct((T, hgw), F32), jax.ShapeDtypeStruct((n, nh, LANE, LANE), F32)),
        in_specs=[col(zcol), col(2), col(7), pl.BlockSpec((1, hgw), lambda i: (0, 0))],
        out_specs=(pl.BlockSpec((CHUNK, hgw), lambda i: (_hg_cid(rev, i, ncs, n), 0)),
                   pl.BlockSpec((None, nh, LANE, LANE), lambda i: (i, 0, 0, 0))),
        scratch_shapes=[pltpu.VMEM((nh, LANE, LANE), F32)], compiler_params=_cp("arbitrary"),
    )(u, u, u, lb)


def hgrn_bwd(u, lb, states, do, rev, zcol, nc, hgw, name):
    T = u.shape[0]
    n, ncs, nh = T // CHUNK, nc // CHUNK, hgw // LANE

    def body(z_ref, v_ref, q_ref, lb_ref, s_ref, do_ref, dz_ref, dv_ref, dq_ref, dlb_ref, dst):
        j = pl.program_id(0)

        @pl.when(j == 0)
        def _():
            dst[...] = jnp.zeros_like(dst)
            dlb_ref[...] = jnp.zeros_like(dlb_ref)

        for h in range(nh):
            cols = slice(h * LANE, (h + 1) * LANE)
            _, vjp = jax.vjp(functools.partial(_hg_chunk, rev), lb_ref[:, cols], z_ref[:, cols], v_ref[:, cols],
                             q_ref[:, cols], s_ref[h])
            dlb, dz, dv, dq, ds = vjp((dst[h], do_ref[:, cols]))
            dst[h] = ds
            dz_ref[:, cols] = dz
            dv_ref[:, cols] = dv
            dq_ref[:, cols] = dq
            dlb_ref[:, cols] += dlb

    def cid(j):
        return _hg_cid(rev, n - 1 - j, ncs, n)

    def col(cb):
        return pl.BlockSpec((CHUNK, hgw), lambda j: (cid(j), cb))

    out = pl.BlockSpec((CHUNK, hgw), lambda j: (cid(j), 0))
    full = jax.ShapeDtypeStruct((T, hgw), F32)
    return pl.pallas_call(
        body, name=name, grid=(n,),
        out_shape=(full, full, full, jax.ShapeDtypeStruct((1, hgw), F32)),
        in_specs=[col(zcol), col(2), col(7), pl.BlockSpec((1, hgw), lambda j: (0, 0)),
                  pl.BlockSpec((None, nh, LANE, LANE), lambda j: (n - 1 - j, 0, 0, 0)), out],
        out_specs=(out, out, out, pl.BlockSpec((1, hgw), lambda j: (0, 0))),
        scratch_shapes=[pltpu.VMEM((nh, LANE, LANE), F32)], compiler_params=_cp("arbitrary"),
    )(u, u, u, lb, states, do)


HT = 128


def _read_math(ofw, obw, g, w):
    return _rms(ofw + obw, w) * _silu(g)


def hg_read(ofw, obw, u, w, gcol, R, name):
    T, hgw = ofw.shape
    nh = hgw // LANE

    def body(a_ref, b_ref, g_ref, w_ref, o_ref):
        for t in range(R // HT):
            rows = slice(t * HT, (t + 1) * HT)
            o_ref[rows, :] = _read_math(a_ref[rows, :], b_ref[rows, :], g_ref[rows, :], w_ref[...]).astype(BF16)

    blk = pl.BlockSpec((R, LANE), lambda i, h: (i, h))
    return pl.pallas_call(
        body, name=name, grid=(T // R, nh), out_shape=jax.ShapeDtypeStruct((T, hgw), BF16),
        in_specs=[blk, blk, pl.BlockSpec((R, LANE), lambda i, h: (i, gcol + h)),
                  pl.BlockSpec((1, LANE), lambda i, h: (0, 0))],
        out_specs=blk, compiler_params=_cp("parallel", "parallel"),
    )(ofw, obw, u, w)


def hg_read_bwd(ofw, obw, u, w, dout, gcol, ocol, R, name):
    T, hgw = ofw.shape
    nh = hgw // LANE

    def body(a_ref, b_ref, g_ref, w_ref, d_ref, do_ref, dg_ref, dw_ref):
        @pl.when(pl.program_id(1) == 0)
        def _():
            dw_ref[...] = jnp.zeros_like(dw_ref)

        for t in range(R // HT):
            rows = slice(t * HT, (t + 1) * HT)
            _, vjp = jax.vjp(_read_math, a_ref[rows, :], b_ref[rows, :], g_ref[rows, :], w_ref[...])
            da, _, dg, dw = vjp(d_ref[rows, :])
            do_ref[rows, :] = da
            dg_ref[rows, :] = dg
            dw_ref[...] += dw

    blk = pl.BlockSpec((R, LANE), lambda h, i: (i, h))
    full = jax.ShapeDtypeStruct((T, hgw), F32)
    return pl.pallas_call(
        body, name=name, grid=(nh, T // R), out_shape=(full, full, jax.ShapeDtypeStruct((nh, 1, LANE), F32)),
        in_specs=[blk, blk, pl.BlockSpec((R, LANE), lambda h, i: (i, gcol + h)),
                  pl.BlockSpec((1, LANE), lambda h, i: (0, 0)), pl.BlockSpec((R, LANE), lambda h, i: (i, ocol + h))],
        out_specs=(blk, blk, pl.BlockSpec((None, 1, LANE), lambda h, i: (h, 0, 0))),
        compiler_params=_cp("parallel", "arbitrary"),
    )(ofw, obw, u, w, dout)


def _na_step(qw, ow, bias, qraw, kl, vl, kc, vc):
    q = _rms(qraw, qw)
    s_loc = mmf(q, kl, "nt") * ATTN_SCALE + bias
    s_ctx = mmf(q, kc, "nt") * ATTN_SCALE
    m = lax.stop_gradient(jnp.maximum(jnp.max(s_loc, axis=-1, keepdims=True), jnp.max(s_ctx, axis=-1, keepdims=True)))
    p_loc = jnp.exp(s_loc - m)
    p_ctx = jnp.exp(s_ctx - m)
    inv = 1.0 / (jnp.sum(p_loc, axis=-1, keepdims=True) + jnp.sum(p_ctx, axis=-1, keepdims=True))
    return _rms(mmf(p_loc * inv, vl, "nn") + mmf(p_ctx * inv, vc, "nn"), ow)


def _na_geometry(nc, rows):
    ncs = nc // GRID_W
    win_r = min(WIN_R, rows)
    nloc = win_r * GRID_W

    def row_start(s):
        r = jnp.maximum(s - ncs, 0)
        return jnp.clip(r - win_r // 2, 0, rows - win_r)

    def bias_idx(s):
        r = s - ncs
        return jnp.where(s < ncs, win_r, r - jnp.clip(r - win_r // 2, 0, rows - win_r))

    return ncs, win_r, nloc, row_start, bias_idx


def na_bias_tables(rpb, rows):
    win_r = min(WIN_R, rows)
    nh = rpb.shape[0]
    sel_r = np.zeros((win_r, win_r, 2 * WIN_R - 1), np.float32)
    for off in range(win_r):
        for jr in range(win_r):
            sel_r[off, jr, jr - off + WIN_R - 1] = 1.0
    qc = np.arange(GRID_W)[:, None]
    kc = np.arange(GRID_W)[None, :]
    wstart = np.clip(qc - WIN_C // 2, 0, GRID_W - WIN_C)
    ok = (kc >= wstart) & (kc < wstart + WIN_C)
    sel_c = np.zeros((GRID_W, GRID_W, 2 * WIN_C - 1), np.float32)
    sel_c[np.broadcast_to(qc, ok.shape)[ok], np.broadcast_to(kc, ok.shape)[ok], (kc - qc + WIN_C - 1)[ok]] = 1.0
    hi = lax.Precision.HIGHEST
    t = jnp.einsum("hab,oja->hojb", rpb, sel_r, precision=hi)
    t = jnp.einsum("hojb,qkb->hoqjk", t, sel_c, precision=hi)
    t = jnp.where(ok[None, None, :, None, :], t, NEG_INF)
    t = jnp.concatenate([t, jnp.full((nh, 1, GRID_W, win_r, GRID_W), NEG_INF, F32)], axis=1)
    return t.reshape(nh, win_r + 1, GRID_W, win_r * GRID_W)


def kv_prep(u, kw, kcol, vcol, naw, R, name):
    T = u.shape[0]

    def body(k_ref, v_ref, w_ref, kn_ref, vb_ref):
        kn_ref[...] = _rms(k_ref[...], w_ref[...]).astype(BF16)
        vb_ref[...] = v_ref[...].astype(BF16)

    blk = pl.BlockSpec((R, LANE), lambda i, h: (i, h))
    sds = jax.ShapeDtypeStruct((T, naw), BF16)
    return pl.pallas_call(
        body, name=name, grid=(T // R, naw // LANE), out_shape=(sds, sds),
        in_specs=[pl.BlockSpec((R, LANE), lambda i, h: (i, kcol + h)), pl.BlockSpec((R, LANE), lambda i, h: (i, vcol + h)),
                  pl.BlockSpec((1, LANE), lambda i, h: (0, 0))],
        out_specs=(blk, blk), compiler_params=_cp("parallel", "parallel"),
    )(u, u, kw)


def kv_prep_bwd(u, kw, dkn, kcol, naw, R, name):
    T = u.shape[0]
    nh = naw // LANE

    def body(k_ref, w_ref, d_ref, dk_ref, dw_ref):
        @pl.when(pl.program_id(1) == 0)
        def _():
            dw_ref[...] = jnp.zeros_like(dw_ref)

        for t in range(R // HT):
            rows = slice(t * HT, (t + 1) * HT)
            _, vjp = jax.vjp(_rms, k_ref[rows, :], w_ref[...])
            dk, dw = vjp(d_ref[rows, :])
            dk_ref[rows, :] = dk
            dw_ref[...] += dw

    blk = pl.BlockSpec((R, LANE), lambda h, i: (i, h))
    return pl.pallas_call(
        body, name=name, grid=(nh, T // R),
        out_shape=(jax.ShapeDtypeStruct((T, naw), F32), jax.ShapeDtypeStruct((nh, 1, LANE), F32)),
        in_specs=[pl.BlockSpec((R, LANE), lambda h, i: (i, kcol + h)), pl.BlockSpec((1, LANE), lambda h, i: (0, 0)), blk],
        out_specs=(blk, pl.BlockSpec((None, 1, LANE), lambda h, i: (h, 0, 0))),
        compiler_params=_cp("parallel", "arbitrary"),
    )(u, kw, dkn)


NA_HB = 4


def _na_operands(j, s, nc, nloc, row_start, q_refs, k_ref, v_ref, qw_ref, ow_ref, b_ref):
    cols = slice(j * LANE, (j + 1) * LANE)
    loc = pl.ds(pl.multiple_of(nc + row_start(s) * GRID_W, GRID_W), nloc)
    ops = (qw_ref[...], ow_ref[:, cols], b_ref[j], q_refs[j][...], k_ref[loc, cols].astype(F32),
           v_ref[loc, cols].astype(F32), k_ref[0:nc, cols].astype(F32), v_ref[0:nc, cols].astype(F32))
    return cols, loc, ops


def _grid_ends(grid):
    ids = lambda: [pl.program_id(d) for d in range(len(grid))]
    first = lambda: functools.reduce(jnp.logical_and, [p == 0 for p in ids()])
    last = lambda: functools.reduce(jnp.logical_and, [p == g - 1 for p, g in zip(ids(), grid)])
    return first, last


def na_fwd(u, kn, vb, qw, ow, bias, qcol, nc, name, comm=None):
    T, naw = kn.shape
    nh, rows = naw // LANE, (T - nc) // GRID_W
    hb = NA_HB if nh % NA_HB == 0 else 1
    ncs, win_r, nloc, row_start, bias_idx = _na_geometry(nc, rows)

    def body(*refs):
        q_refs, (k_ref, v_ref, qw_ref, ow_ref, b_ref, o_ref) = refs[:hb], refs[hb:]
        s = pl.program_id(1)
        for j in range(hb):
            cols, _, ops = _na_operands(j, s, nc, nloc, row_start, q_refs, k_ref, v_ref, qw_ref, ow_ref, b_ref)
            o_ref[:, cols] = _na_step(*ops).astype(BF16)

    wide = pl.BlockSpec((T, hb * LANE), lambda g, s: (0, g), pipeline_mode=pl.Buffered(1))
    grid = (nh // hb, T // GRID_W)
    cin, cspec, cout, csem = _comm_extras(comm)
    out = pl.pallas_call(
        _hosted(body, hb + 5, 1, comm, *_grid_ends(grid)), name=name, grid=grid,
        out_shape=[jax.ShapeDtypeStruct((T, naw), BF16)] + cout,
        in_specs=[pl.BlockSpec((GRID_W, LANE), functools.partial(lambda j, g, s: (s, qcol + g * hb + j), j))
                  for j in range(hb)]
        + [wide, wide, pl.BlockSpec((1, LANE), lambda g, s: (0, 0)), pl.BlockSpec((1, hb * LANE), lambda g, s: (0, g)),
           pl.BlockSpec((hb, None, GRID_W, nloc), lambda g, s: (g, bias_idx(s), 0, 0))] + cspec,
        out_specs=[pl.BlockSpec((GRID_W, hb * LANE), lambda g, s: (s, g))] + cspec, scratch_shapes=csem,
        compiler_params=_cp("arbitrary", "arbitrary"),
    )(*([u] * hb), kn, vb, qw, ow, bias, *cin)
    return out[0] if comm is None else out


def na_bwd(u, kn, vb, qw, ow, bias, dout, qcol, ocol, nc, name, comm=None):
    T, naw = kn.shape
    nh, rows = naw // LANE, (T - nc) // GRID_W
    hb = NA_HB if nh % NA_HB == 0 else 1
    ncs, win_r, nloc, row_start, bias_idx = _na_geometry(nc, rows)
    fresh = [0] + [ncs + r for r in range(rows) if r == 0 or r - np.clip(r - win_r // 2, 0, rows - win_r)
                   != (r - 1) - np.clip(r - 1 - win_r // 2, 0, rows - win_r)]

    def body(*refs):
        q_refs, d_refs = refs[:hb], refs[hb:2 * hb]
        k_ref, v_ref, qw_ref, ow_ref, b_ref, dq_ref, dk_ref, dv_ref, db_ref, dqw_ref, dow_ref = refs[2 * hb:]
        s = pl.program_id(1)

        @pl.when(s == 0)
        def _():
            dk_ref[...] = jnp.zeros_like(dk_ref)
            dv_ref[...] = jnp.zeros_like(dv_ref)
            dqw_ref[...] = jnp.zeros_like(dqw_ref)
            dow_ref[...] = jnp.zeros_like(dow_ref)

        first = functools.reduce(lambda a, b: a | b, [s == f for f in fresh])

        @pl.when(first)
        def _():
            db_ref[...] = jnp.zeros_like(db_ref)

        for j in range(hb):
            cols, loc, ops = _na_operands(j, s, nc, nloc, row_start, q_refs, k_ref, v_ref, qw_ref, ow_ref, b_ref)
            _, vjp = jax.vjp(_na_step, *ops)
            dqw, dow, db, dq, dkl, dvl, dkc, dvc = vjp(d_refs[j][...])
            dq_ref[:, cols] = dq
            dk_ref[loc, cols] += dkl
            dv_ref[loc, cols] += dvl
            dk_ref[0:nc, cols] += dkc
            dv_ref[0:nc, cols] += dvc
            db_ref[j] += db
            dqw_ref[j] += dqw
            dow_ref[j] += dow

    wide = pl.BlockSpec((T, hb * LANE), lambda g, s: (0, g), pipeline_mode=pl.Buffered(1))
    hvec = pl.BlockSpec((hb, 1, LANE), lambda g, s: (g, 0, 0))
    full = jax.ShapeDtypeStruct((T, naw), F32)
    hv = jax.ShapeDtypeStruct((nh, 1, LANE), F32)
    bspec = pl.BlockSpec((hb, None, GRID_W, nloc), lambda g, s: (g, bias_idx(s), 0, 0))
    grid = (nh // hb, T // GRID_W)
    cin, cspec, cout, csem = _comm_extras(comm)
    return pl.pallas_call(
        _hosted(body, 2 * hb + 5, 6, comm, *_grid_ends(grid)), name=name, grid=grid,
        out_shape=[full, full, full, jax.ShapeDtypeStruct(bias.shape, F32), hv, hv] + cout,
        in_specs=[pl.BlockSpec((GRID_W, LANE), functools.partial(lambda j, g, s: (s, qcol + g * hb + j), j))
                  for j in range(hb)]
        + [pl.BlockSpec((GRID_W, LANE), functools.partial(lambda j, g, s: (s, ocol + g * hb + j), j))
           for j in range(hb)]
        + [wide, wide, pl.BlockSpec((1, LANE), lambda g, s: (0, 0)), pl.BlockSpec((1, hb * LANE), lambda g, s: (0, g)),
           bspec] + cspec,
        out_specs=[pl.BlockSpec((GRID_W, hb * LANE), lambda g, s: (s, g)), wide, wide, bspec, hvec, hvec] + cspec,
        scratch_shapes=csem, compiler_params=_cp("arbitrary", "arbitrary"),
    )(*([u] * hb), *([dout] * hb), kn, vb, qw, ow, bias, *cin)


def _halo_specs(R, width, T, col):
    hb = R // 8
    prev = pl.BlockSpec((8, width), lambda j, i: (jnp.maximum(i * hb - 1, 0), col(j, i)))
    nxt = pl.BlockSpec((8, width), lambda j, i: (jnp.minimum((i + 1) * hb, T // 8 - 1), col(j, i)))
    return prev, nxt


def _edge_flags(i, ncb, nblk):
    has_prev = jnp.where((i == 0) | (i == ncb), 0.0, 1.0).astype(F32)
    has_next = jnp.where((i == ncb - 1) | (i == nblk - 1), 0.0, 1.0).astype(F32)
    return has_prev, has_next


def _shift_up(a, prev_row):
    r0 = lax.broadcasted_iota(jnp.int32, a.shape, 0) == 0
    return jnp.where(r0, prev_row, pltpu.roll(a, 1, 0))


def _shift_dn(a, next_row):
    n = a.shape[0]
    rl = lax.broadcasted_iota(jnp.int32, a.shape, 0) == n - 1
    return jnp.where(rl, next_row, pltpu.roll(a, n - 1, 0))


def _conv3(a, prev_row, next_row, w_ref):
    return w_ref[0:1, :] * _shift_up(a, prev_row) + w_ref[1:2, :] * a + w_ref[2:3, :] * _shift_dn(a, next_row)


def _cv_post(b, y, w):
    return _rms(b * y, w)


def short_conv(u, cw, ow, bcol, nc, cvw, R, bwd_dout=None, ocol=0, name=""):
    T = u.shape[0]
    nh, nblk, ncb = cvw // LANE, T // R, nc // R
    bwd = bwd_dout is not None

    def body(b_ref, c_ref, v_ref, cp_ref, vp_ref, cn_ref, vn_ref, cw_ref, ow_ref, *rest):
        i = pl.program_id(1)
        has_prev, has_next = _edge_flags(i, ncb, nblk)
        p = c_ref[...] * v_ref[...]
        y = _conv3(p, cp_ref[7:8, :] * vp_ref[7:8, :] * has_prev, cn_ref[0:1, :] * vn_ref[0:1, :] * has_next, cw_ref)
        if not bwd:
            rest[0][...] = _cv_post(b_ref[...], y, ow_ref[...]).astype(BF16)
            return
        d_ref, db_ref, dy_ref, dow_ref = rest

        @pl.when(i == 0)
        def _():
            dow_ref[...] = jnp.zeros_like(dow_ref)

        _, vjp = jax.vjp(_cv_post, b_ref[...], y, ow_ref[...])
        db, dy, dow = vjp(d_ref[...])
        db_ref[...] = db
        dy_ref[...] = dy
        dow_ref[...] += dow

    def main(k):
        return pl.BlockSpec((R, LANE), lambda h, i: (i, bcol + k * nh + h))

    cprev, cnext = _halo_specs(R, LANE, T, lambda h, i: bcol + nh + h)
    vprev, vnext = _halo_specs(R, LANE, T, lambda h, i: bcol + 2 * nh + h)
    in_specs = [main(0), main(1), main(2), cprev, vprev, cnext, vnext,
                pl.BlockSpec((3, LANE), lambda h, i: (0, h)), pl.BlockSpec((1, LANE), lambda h, i: (0, h))]
    ins = [u] * 7 + [cw, ow]
    blk = pl.BlockSpec((R, LANE), lambda h, i: (i, h))
    if not bwd:
        out_shape, out_specs = jax.ShapeDtypeStruct((T, cvw), BF16), blk
    else:
        in_specs.append(pl.BlockSpec((R, LANE), lambda h, i: (i, ocol + h)))
        ins.append(bwd_dout)
        full = jax.ShapeDtypeStruct((T, cvw), F32)
        out_shape = (full, full, jax.ShapeDtypeStruct((nh, 1, LANE), F32))
        out_specs = (blk, blk, pl.BlockSpec((None, 1, LANE), lambda h, i: (h, 0, 0)))
    return pl.pallas_call(body, name=name, grid=(nh, nblk), out_shape=out_shape, in_specs=in_specs,
                          out_specs=out_specs, compiler_params=_cp("parallel", "arbitrary"))(*ins)


def conv3_bwd(dy, src, cw, nc, R, W, prod_cols=None, col0=0, out_dtype=F32, name=""):
    T, C = dy.shape
    nblk, ncb = T // R, nc // R
    prod = prod_cols is not None

    def body(*refs):
        if prod:
            (d_ref, dp_ref, dn_ref, c_ref, v_ref, cp_ref, vp_ref, cn_ref, vn_ref, w_ref,
             dc_ref, dv_ref, dw_ref) = refs
        else:
            d_ref, dp_ref, dn_ref, p_ref, pp_ref, pn_ref, w_ref, o_ref, dw_ref = refs
        i = pl.program_id(1)
        has_prev, has_next = _edge_flags(i, ncb, nblk)

        @pl.when(i == 0)
        def _():
            dw_ref[...] = jnp.zeros_like(dw_ref)

        d = d_ref[...]
        d_up = _shift_up(d, dp_ref[7:8, :] * has_prev)
        d_dn = _shift_dn(d, dn_ref[0:1, :] * has_next)
        dp = w_ref[0:1, :] * d_dn + w_ref[1:2, :] * d + w_ref[2:3, :] * d_up
        if prod:
            c, v = c_ref[...], v_ref[...]
            p = c * v
            p_prev, p_next = cp_ref[7:8, :] * vp_ref[7:8, :] * has_prev, cn_ref[0:1, :] * vn_ref[0:1, :] * has_next
            dc_ref[...] = dp * v
            dv_ref[...] = dp * c
        else:
            p = p_ref[...]
            p_prev, p_next = pp_ref[7:8, :] * has_prev, pn_ref[0:1, :] * has_next
            o_ref[...] = dp.astype(out_dtype)
        dw_ref[0:1, :] += jnp.sum(_shift_up(p, p_prev) * d, axis=0, keepdims=True)
        dw_ref[1:2, :] += jnp.sum(p * d, axis=0, keepdims=True)
        dw_ref[2:3, :] += jnp.sum(_shift_dn(p, p_next) * d, axis=0, keepdims=True)

    blk = pl.BlockSpec((R, W), lambda j, i: (i, j))
    dprev, dnext = _halo_specs(R, W, T, lambda j, i: j)
    wspec = pl.BlockSpec((3, W), lambda j, i: (0, j))
    dwspec = pl.BlockSpec((8, W), lambda j, i: (0, j))
    dwshape = jax.ShapeDtypeStruct((8, C), F32)
    if prod:
        ccol, vcol = prod_cols
        cprev, cnext = _halo_specs(R, W, T, lambda j, i: ccol + j)
        vprev, vnext = _halo_specs(R, W, T, lambda j, i: vcol + j)
        in_specs = [blk, dprev, dnext, pl.BlockSpec((R, W), lambda j, i: (i, ccol + j)),
                    pl.BlockSpec((R, W), lambda j, i: (i, vcol + j)), cprev, vprev, cnext, vnext, wspec]
        ins = [dy, dy, dy] + [src] * 6 + [cw]
        full = jax.ShapeDtypeStruct((T, C), F32)
        out_shape, out_specs = (full, full, dwshape), (blk, blk, dwspec)
    else:
        sprev, snext = _halo_specs(R, W, T, lambda j, i: col0 + j)
        in_specs = [blk, dprev, dnext, pl.BlockSpec((R, W), lambda j, i: (i, col0 + j)), sprev, snext,
                    pl.BlockSpec((3, W), lambda j, i: (0, col0 + j))]
        ins = [dy, dy, dy, src, src, src, cw]
        out_shape, out_specs = (jax.ShapeDtypeStruct((T, C), out_dtype), dwshape), (blk, dwspec)
    return pl.pallas_call(body, name=name, grid=(C // W, nblk), out_shape=out_shape, in_specs=in_specs,
                          out_specs=out_specs, compiler_params=_cp("parallel", "arbitrary"))(*ins)


def ffn_mid(uf, cw, cb, nc, R, W, da=None, name=""):
    T, C = uf.shape
    F = C // 2
    nblk, ncb, nj = T // R, nc // R, F // W
    bwd = da is not None

    def body(g_ref, v_ref, gp_ref, vp_ref, gn_ref, vn_ref, wg_ref, wv_ref, bg_ref, bv_ref, *rest):
        i = pl.program_id(1)
        has_prev, has_next = _edge_flags(i, ncb, nblk)
        yg = _conv3(g_ref[...], gp_ref[7:8, :] * has_prev, gn_ref[0:1, :] * has_next, wg_ref) + bg_ref[...]
        yv = _conv3(v_ref[...], vp_ref[7:8, :] * has_prev, vn_ref[0:1, :] * has_next, wv_ref) + bv_ref[...]
        sg = jax.nn.sigmoid(yg)
        if not bwd:
            rest[0][...] = (yg * sg * yv).astype(BF16)
            return
        da_ref, dyg_ref, dyv_ref, dbg_ref, dbv_ref = rest

        @pl.when(i == 0)
        def _():
            dbg_ref[...] = jnp.zeros_like(dbg_ref)
            dbv_ref[...] = jnp.zeros_like(dbv_ref)

        d = da_ref[...]
        dyg = d * yv * (sg * (1.0 + yg * (1.0 - sg)))
        dyv = d * (yg * sg)
        dyg_ref[...] = dyg
        dyv_ref[...] = dyv
        dbg_ref[...] += jnp.sum(dyg, axis=0, keepdims=True)
        dbv_ref[...] += jnp.sum(dyv, axis=0, keepdims=True)

    gblk = pl.BlockSpec((R, W), lambda j, i: (i, j))
    vblk = pl.BlockSpec((R, W), lambda j, i: (i, nj + j))
    gprev, gnext = _halo_specs(R, W, T, lambda j, i: j)
    vprev, vnext = _halo_specs(R, W, T, lambda j, i: nj + j)
    in_specs = [gblk, vblk, gprev, vprev, gnext, vnext,
                pl.BlockSpec((3, W), lambda j, i: (0, j)), pl.BlockSpec((3, W), lambda j, i: (0, nj + j)),
                pl.BlockSpec((1, W), lambda j, i: (0, j)), pl.BlockSpec((1, W), lambda j, i: (0, nj + j))]
    ins = [uf] * 6 + [cw, cw, cb, cb]
    if not bwd:
        out_shape, out_specs = jax.ShapeDtypeStruct((T, F), BF16), gblk
    else:
        in_specs.append(gblk)
        ins.append(da)
        half, bias = jax.ShapeDtypeStruct((T, F), F32), jax.ShapeDtypeStruct((1, F), F32)
        bspec = pl.BlockSpec((1, W), lambda j, i: (0, j))
        out_shape, out_specs = (half, half, bias, bias), (gblk, gblk, bspec, bspec)
    return pl.pallas_call(body, name=name, grid=(nj, nblk), out_shape=out_shape, in_specs=in_specs,
                          out_specs=out_specs, compiler_params=_cp("parallel", "arbitrary"))(*ins)


def ffn_mid_bwd(uf, cw, cb, da, nc, R, W, name):
    T, C = uf.shape
    F = C // 2
    nblk, ncb, nj = T // R, nc // R, F // W

    def body(g_ref, v_ref, gp_ref, vp_ref, gn_ref, vn_ref, d_ref, dp_ref, dn_ref, wg_ref, wv_ref, bg_ref, bv_ref,
             dug_ref, duv_ref, dwg_ref, dwv_ref, dbg_ref, dbv_ref):
        i = pl.program_id(1)
        has_prev, has_next = _edge_flags(i, ncb, nblk)

        @pl.when(i == 0)
        def _():
            for r in (dwg_ref, dwv_ref, dbg_ref, dbv_ref):
                r[...] = jnp.zeros_like(r)

        def taps(w_ref):
            return w_ref[0:1, :], w_ref[1:2, :], w_ref[2:3, :]

        def dy_of(yg, yv, d):
            sg = jax.nn.sigmoid(yg)
            return d * yv * (sg * (1.0 + yg * (1.0 - sg))), d * (yg * sg)

        g, v, d = g_ref[...], v_ref[...], d_ref[...]
        (wg0, wg1, wg2), (wv0, wv1, wv2) = taps(wg_ref), taps(wv_ref)
        bg, bv = bg_ref[...], bv_ref[...]
        g_up, g_dn = _shift_up(g, gp_ref[7:8, :] * has_prev), _shift_dn(g, gn_ref[0:1, :] * has_next)
        v_up, v_dn = _shift_up(v, vp_ref[7:8, :] * has_prev), _shift_dn(v, vn_ref[0:1, :] * has_next)
        dyg, dyv = dy_of(wg0 * g_up + wg1 * g + wg2 * g_dn + bg, wv0 * v_up + wv1 * v + wv2 * v_dn + bv, d)
        dyg_p, dyv_p = dy_of(wg0 * gp_ref[6:7, :] + wg1 * gp_ref[7:8, :] + wg2 * g_ref[0:1, :] + bg,
                             wv0 * vp_ref[6:7, :] + wv1 * vp_ref[7:8, :] + wv2 * v_ref[0:1, :] + bv, dp_ref[7:8, :])
        dyg_n, dyv_n = dy_of(wg0 * g_ref[R - 1:R, :] + wg1 * gn_ref[0:1, :] + wg2 * gn_ref[1:2, :] + bg,
                             wv0 * v_ref[R - 1:R, :] + wv1 * vn_ref[0:1, :] + wv2 * vn_ref[1:2, :] + bv, dn_ref[0:1, :])
        dug_ref[...] = (wg0 * _shift_dn(dyg, dyg_n * has_next) + wg1 * dyg
                        + wg2 * _shift_up(dyg, dyg_p * has_prev)).astype(BF16)
        duv_ref[...] = (wv0 * _shift_dn(dyv, dyv_n * has_next) + wv1 * dyv
                        + wv2 * _shift_up(dyv, dyv_p * has_prev)).astype(BF16)
        for ref, ups, mid, dns, dy in ((dwg_ref, g_up, g, g_dn, dyg), (dwv_ref, v_up, v, v_dn, dyv)):
            ref[0:1, :] += jnp.sum(ups * dy, axis=0, keepdims=True)
            ref[1:2, :] += jnp.sum(mid * dy, axis=0, keepdims=True)
            ref[2:3, :] += jnp.sum(dns * dy, axis=0, keepdims=True)
        dbg_ref[...] += jnp.sum(dyg, axis=0, keepdims=True)
        dbv_ref[...] += jnp.sum(dyv, axis=0, keepdims=True)

    gblk = pl.BlockSpec((R, W), lambda j, i: (i, j))
    vblk = pl.BlockSpec((R, W), lambda j, i: (i, nj + j))
    gprev, gnext = _halo_specs(R, W, T, lambda j, i: j)
    vprev, vnext = _halo_specs(R, W, T, lambda j, i: nj + j)
    half = jax.ShapeDtypeStruct((T, F), BF16)
    taps8, bias1 = jax.ShapeDtypeStruct((8, F), F32), jax.ShapeDtypeStruct((1, F), F32)
    return pl.pallas_call(
        body, name=name, grid=(nj, nblk), out_shape=(half, half, taps8, taps8, bias1, bias1),
        in_specs=[gblk, vblk, gprev, vprev, gnext, vnext, gblk, gprev, gnext,
                  pl.BlockSpec((3, W), lambda j, i: (0, j)), pl.BlockSpec((3, W), lambda j, i: (0, nj + j)),
                  pl.BlockSpec((1, W), lambda j, i: (0, j)), pl.BlockSpec((1, W), lambda j, i: (0, nj + j))],
        out_specs=(gblk, gblk, pl.BlockSpec((8, W), lambda j, i: (0, j)), pl.BlockSpec((8, W), lambda j, i: (0, j)),
                   pl.BlockSpec((1, W), lambda j, i: (0, j)), pl.BlockSpec((1, W), lambda j, i: (0, j))),
        compiler_params=_cp("parallel", "arbitrary"),
    )(uf, uf, uf, uf, uf, uf, da, da, da, cw, cw, cb, cb)


WEIGHTS = ("c_ctx", "w_ada", "b_ada", "ln1_w", "ln2_w", "w_in", "hg_lb_logits", "hg_norm_w", "na_q_norm_w",
           "na_k_norm_w", "na_rpb", "na_out_norm_w", "cv_w", "cv_out_norm_w", "w_out", "w_up", "ffn_conv_w",
           "ffn_conv_b", "w_down")
BIG = ("w_ada", "w_in", "w_out", "w_up", "w_down")
SHARDED_SMALL = ("hg_lb_logits", "cv_w", "ffn_conv_w")


def _flat_rows(parts, dtype):
    flat, layout, off = [], [], 0
    for p in parts:
        layout.append((off, p.shape))
        flat.append(p.reshape(-1).astype(dtype))
        off += p.size
    pad = (-off) % (8 * LANE)
    if pad:
        flat.append(jnp.zeros((pad,), dtype))
    return jnp.concatenate(flat).reshape(-1, LANE), layout


def _unflat(buf, layout):
    v = buf.reshape(-1)
    return [v[off:off + int(np.prod(shape))].reshape(shape) for off, shape in layout]


def _lb_all(logits):
    sm = jax.nn.softmax(logits.astype(F32), axis=1)
    return jnp.cumsum(sm, axis=1) - sm[:, :1]


def _seg(ctx_vec, lat_vec):
    return jnp.broadcast_to(jnp.stack([ctx_vec, lat_vec])[:, None, :], (2, RT, ctx_vec.shape[0]))


def _shared(vec):
    return jnp.broadcast_to(vec[None, None, :], (1, RT, vec.shape[0]))


def kernel(x, c, ctx, c_ctx, w_ada, b_ada, ln1_w, ln2_w, w_in, hg_lb_logits, hg_norm_w, na_q_norm_w, na_k_norm_w, na_rpb, na_out_norm_w, cv_w, cv_out_norm_w, w_out, w_up, ffn_conv_w, ffn_conv_b, w_down, loss_target, m_c_ctx, m_w_ada, m_b_ada, m_ln1_w, m_ln2_w, m_w_in, m_hg_lb_logits, m_hg_norm_w, m_na_q_norm_w, m_na_k_norm_w, m_na_rpb, m_na_out_norm_w, m_cv_w, m_cv_out_norm_w, m_w_out, m_w_up, m_ffn_conv_w, m_ffn_conv_b, m_w_down, v_c_ctx, v_w_ada, v_b_ada, v_ln1_w, v_ln2_w, v_w_in, v_hg_lb_logits, v_hg_norm_w, v_na_q_norm_w, v_na_k_norm_w, v_na_rpb, v_na_out_norm_w, v_cv_w, v_cv_out_norm_w, v_w_out, v_w_up, v_ffn_conv_w, v_ffn_conv_b, v_w_down):
    W = dict(c_ctx=c_ctx, w_ada=w_ada, b_ada=b_ada, ln1_w=ln1_w, ln2_w=ln2_w, w_in=w_in, hg_lb_logits=hg_lb_logits,
             hg_norm_w=hg_norm_w, na_q_norm_w=na_q_norm_w, na_k_norm_w=na_k_norm_w, na_rpb=na_rpb,
             na_out_norm_w=na_out_norm_w, cv_w=cv_w, cv_out_norm_w=cv_out_norm_w, w_out=w_out, w_up=w_up,
             ffn_conv_w=ffn_conv_w, ffn_conv_b=ffn_conv_b, w_down=w_down)
    Mo = dict(c_ctx=m_c_ctx, w_ada=m_w_ada, b_ada=m_b_ada, ln1_w=m_ln1_w, ln2_w=m_ln2_w, w_in=m_w_in,
              hg_lb_logits=m_hg_lb_logits, hg_norm_w=m_hg_norm_w, na_q_norm_w=m_na_q_norm_w,
              na_k_norm_w=m_na_k_norm_w, na_rpb=m_na_rpb, na_out_norm_w=m_na_out_norm_w, cv_w=m_cv_w,
              cv_out_norm_w=m_cv_out_norm_w, w_out=m_w_out, w_up=m_w_up, ffn_conv_w=m_ffn_conv_w,
              ffn_conv_b=m_ffn_conv_b, w_down=m_w_down)
    Vo = dict(c_ctx=v_c_ctx, w_ada=v_w_ada, b_ada=v_b_ada, ln1_w=v_ln1_w, ln2_w=v_ln2_w, w_in=v_w_in,
              hg_lb_logits=v_hg_lb_logits, hg_norm_w=v_hg_norm_w, na_q_norm_w=v_na_q_norm_w,
              na_k_norm_w=v_na_k_norm_w, na_rpb=v_na_rpb, na_out_norm_w=v_na_out_norm_w, cv_w=v_cv_w,
              cv_out_norm_w=v_cv_out_norm_w, w_out=v_w_out, w_up=v_w_up, ffn_conv_w=v_ffn_conv_w,
              ffn_conv_b=v_ffn_conv_b, w_down=v_w_down)

    xi, yi, ci = _me()
    chip = 2 * xi + yi
    dev = 2 * chip + ci
    L, D = x.shape[1], x.shape[2]
    NC = ctx.shape[1]
    T = NC + L
    depth = w_in.shape[0]
    HGW, NAW, CVW = 4 * hg_lb_logits.shape[-1], na_out_norm_w.shape[-1], cv_out_norm_w.shape[-1]
    MIX = HGW + NAW + CVW
    INW, FF2 = 4 * w_in.shape[-1], 4 * w_up.shape[-1]
    F = FF2 // 2
    ADA = 4 * w_ada.shape[-1]
    assert NAW == 2 * HGW and INW == 5 * HGW + 3 * NAW + 3 * CVW and ADA == 6 * D and NC % 128 == 0
    assert L % GRID_W == 0 and T % CHUNK == 0 and depth == 2
    R = math.gcd(NC, 256)
    FW = 512 if F % 512 == 0 else LANE
    rows = L // GRID_W
    nh_hg, nh_na, nh_cv = HGW // LANE, NAW // LANE, CVW // LANE
    kcol = 3 * nh_hg
    vcol = kcol + nh_na
    gcol = vcol + nh_na + nh_hg
    qcol = gcol + nh_hg
    bcol = qcol + nh_na
    mix_na, mix_cv = nh_hg, nh_hg + nh_na

    small1, lay1 = _flat_rows([c[0], hg_lb_logits, cv_w, ffn_conv_w], F32)
    g1 = allgather8([small1], "gather_cond")[0]
    per_dev = [_unflat(g1[d], lay1) for d in range(8)]
    c_all = jnp.stack([p[0] for p in per_dev])
    lb_logits = jnp.concatenate([per_dev[2 * s][1] for s in range(4)], axis=-1)
    cvw_full = jnp.concatenate([per_dev[2 * s][2] for s in range(4)], axis=-1)
    fcw_full = jnp.concatenate([per_dev[2 * s][3] for s in range(4)], axis=-1)
    lb_all, lb_pull = jax.vjp(_lb_all, lb_logits)

    a16 = jnp.concatenate([c_all, c_ctx[None], jnp.zeros((7, D), F32)])
    s16 = _silu(a16)
    wcols = ADA // 4
    b_mine = lax.dynamic_slice_in_dim(b_ada, chip * wcols, wcols, axis=1)
    p_ada = jnp.stack([mm_nn(s16, w_ada[l][None], F32, f"ada_fwd_{l}") + b_mine[l][None] for l in range(depth)])
    g2 = allgather8([p_ada.reshape(depth * 16, wcols)], "gather_ada")[0].reshape(8, depth, 16, wcols)
    ada_rows = jnp.concatenate([g2[2 * s] for s in range(4)], axis=-1)
    ada = lax.dynamic_index_in_dim(ada_rows, dev, axis=1, keepdims=False)
    ada_c = ada_rows[:, 8]

    def half_rows(a):
        h = a.shape[0] // 2
        return lax.dynamic_slice_in_dim(a, ci * h, h, axis=0)

    proj = ("w_in", "w_out", "w_up", "w_down")
    wparts = [{n: half_rows(W[n][l]).astype(BF16) for n in proj} for l in range(depth)]

    def stacked(n, g):
        g = g.reshape(4, -1, g.shape[-1])
        return g.reshape(1, -1, g.shape[-1]) if n in ("w_out", "w_down") else g

    Wg = [dict(zip(proj, [stacked(n, g) for n, g in
                          zip(proj, allgather8([wparts[0][n] for n in proj], "gather_weights_0", hbm=True))])), {}]
    late_a, late_b = ("w_in", "w_out", "w_down"), ("w_up",)

    xcat = jnp.concatenate([ctx[0], x[0]], axis=0)
    mods = []
    for l in range(depth):
        lat, con = jnp.split(ada[l], 6), jnp.split(ada_c[l], 6)
        mods.append(dict(sh1=_seg(con[0], lat[0]), sc1=_seg(con[1], lat[1]), g1=_seg(con[2], lat[2]),
                         sh2=_seg(con[3], lat[3]), sc2=_seg(con[4], lat[4]), g2=_seg(con[5], lat[5]),
                         ln1=_shared(ln1_w[l]), ln2=_shared(ln2_w[l])))
    bias_pull, saved = [], []
    x0 = xcat
    _, h = gate_norm(x0, None, None, mods[0]["ln1"], mods[0]["sh1"], mods[0]["sc1"], NC, R, "norm_in")
    for l in range(depth):
        md, wl = mods[l], Wg[l]
        u = mm_nn(h, wl["w_in"], F32, f"proj_in_{l}")
        lbf, lbb = lb_all[0, l][None], lb_all[1, l][None]
        o_fw, st_fw = hgrn_fwd(u, lbf, False, 0, NC, HGW, f"hgrn_fw_{l}")
        o_bw, st_bw = hgrn_fwd(u, lbb, True, 1, NC, HGW, f"hgrn_bw_{l}")
        hgn = hg_norm_w[l][None]
        hg = hg_read(o_fw, o_bw, u, hgn, gcol, R, f"hg_read_{l}")
        bias, pull = jax.vjp(lambda r: na_bias_tables(r, rows), na_rpb[l])
        bias_pull.append(pull)
        qn, kn, on = na_q_norm_w[l][None], na_k_norm_w[l][None], na_out_norm_w[l][None]
        keys_n, vals_b = kv_prep(u, kn, kcol, vcol, NAW, R, f"kv_prep_{l}")
        if l == 0:
            na, *late = na_fwd(u, keys_n, vals_b, qn, on, bias, qcol, NC, f"na_fwd_{l}",
                               comm=allgather8_comm([wparts[1][n] for n in late_a]))
            Wg[1].update({n: stacked(n, g) for n, g in zip(late_a, late)})
        else:
            na = na_fwd(u, keys_n, vals_b, qn, on, bias, qcol, NC, f"na_fwd_{l}")
        cvw_l, cvo = cvw_full[l], cv_out_norm_w[l][None]
        cv = short_conv(u, cvw_l, cvo, bcol, NC, CVW, R, name=f"short_conv_{l}")
        mix = jnp.concatenate([hg, na, cv], axis=1)
        m1 = mm_nn(mix, wl["w_out"], F32, f"proj_out_{l}")
        x1, h2 = gate_norm(x0, m1, md["g1"], md["ln2"], md["sh2"], md["sc2"], NC, R, f"gate_norm_mid_{l}")
        if l == 0:
            uf, *late = mm_nn(h2, wl["w_up"], F32, f"ffn_up_{l}", comm=allgather8_comm([wparts[1][n] for n in late_b]))
            Wg[1].update({n: stacked(n, g) for n, g in zip(late_b, late)})
        else:
            uf = mm_nn(h2, wl["w_up"], F32, f"ffn_up_{l}")
        fcw_l, fcb_l = fcw_full[l], ffn_conv_b[l][None]
        a = ffn_mid(uf, fcw_l, fcb_l, NC, R, FW, name=f"ffn_mid_{l}")
        m2 = mm_nn(a, wl["w_down"], F32, f"ffn_down_{l}")
        saved.append(dict(x0=x0, h=h, u=u, o_fw=o_fw, o_bw=o_bw, st_fw=st_fw, st_bw=st_bw, bias=bias, mix=mix,
                          m1=m1, x1=x1, h2=h2, uf=uf, a=a, m2=m2, lbf=lbf, lbb=lbb, keys_n=keys_n, vals_b=vals_b))
        if l + 1 < depth:
            nx = mods[l + 1]
            x0, h = gate_norm(x1, m2, md["g2"], nx["ln1"], nx["sh1"], nx["sc1"], NC, R, f"gate_norm_end_{l}")
    sv, md = saved[-1], mods[-1]
    loss_terms, d_x1, d_m2, d_g2 = gate_loss(sv["x1"], sv["m2"], md["g2"], loss_target[0], NC, R, "gate_loss")
    loss = lax.psum(jnp.sum(loss_terms), ("x", "y", "c"))

    big_grads = [dict() for _ in range(depth)]
    core = ci.astype(jnp.int32).reshape(1)
    pairs, quads = [None] * depth, [None] * depth
    small = [dict() for _ in range(depth)]
    d_ada = [None] * depth
    d_lb = [None] * depth
    for l in reversed(range(depth)):
        sv, md, wl = saved[l], mods[l], Wg[l]
        u, uf = sv["u"], sv["uf"]
        big_grads[l]["w_down"] = mm_tn(sv["a"], d_m2, 1, BF16, f"grad_w_down_{l}").reshape(4, F // 4, D)
        d_a = mm_nt(d_m2, wl["w_down"], F32, f"ffn_down_bwd_{l}")
        fcw_l, fcb_l = fcw_full[l], ffn_conv_b[l][None]
        dug, duv, dwg, dwv, dbg, dbv = ffn_mid_bwd(uf, fcw_l, fcb_l, d_a, NC, R, FW, f"ffn_mid_bwd_{l}")
        d_uf = jnp.concatenate([dug, duv], axis=1)
        small[l]["ffn_conv_w"] = jnp.concatenate([dwg[:3], dwv[:3]], axis=1)
        small[l]["ffn_conv_b"] = jnp.concatenate([dbg[0], dbv[0]])
        big_grads[l]["w_up"] = mm_tn(sv["h2"], d_uf, 4, BF16, f"grad_w_up_{l}")
        d_h2 = mm_nt(d_uf, wl["w_up"], F32, f"ffn_up_bwd_{l}")
        d_x0, d_m1, dg1, dln2, dsh2, dsc2 = gate_norm_bwd(sv["x0"], sv["m1"], md["g1"], md["ln2"], md["sh2"], md["sc2"],
                                                          d_x1, d_h2, NC, R, f"gate_norm_mid_bwd_{l}")
        big_grads[l]["w_out"] = mm_tn(sv["mix"], d_m1, 1, BF16, f"grad_w_out_{l}").reshape(4, MIX // 4, D)
        d_mix = mm_nt(d_m1, wl["w_out"], F32, f"proj_out_bwd_{l}")
        hgn = hg_norm_w[l][None]
        d_o, d_hgg, d_hgn = hg_read_bwd(sv["o_fw"], sv["o_bw"], u, hgn, d_mix, gcol, 0, R, f"hg_read_bwd_{l}")
        dzf, dvf, dqf, dlbf = hgrn_bwd(u, sv["lbf"], sv["st_fw"], d_o, False, 0, NC, HGW, f"hgrn_fw_bwd_{l}")
        dzb, dvb, dqb, dlbb = hgrn_bwd(u, sv["lbb"], sv["st_bw"], d_o, True, 1, NC, HGW, f"hgrn_bw_bwd_{l}")
        d_lb[l] = (dlbf[0], dlbb[0])
        qn, kn, on = na_q_norm_w[l][None], na_k_norm_w[l][None], na_out_norm_w[l][None]
        na_out = na_bwd(u, sv["keys_n"], sv["vals_b"], qn, on, sv["bias"], d_mix, qcol, mix_na, NC, f"na_bwd_{l}",
                        comm=chip_alltoall_comm(pairs[1]) if l == 0 else None)
        d_nq, d_keys_n, d_nv, d_bias, d_qn, d_on = na_out[:6]
        if l == 0:
            quads[1] = na_out[6:]
        d_nk, d_kn = kv_prep_bwd(u, kn, d_keys_n, kcol, NAW, R, f"kv_prep_bwd_{l}")
        cvw_l, cvo = cvw_full[l], cv_out_norm_w[l][None]
        d_cb, d_cy, d_cvo = short_conv(u, cvw_l, cvo, bcol, NC, CVW, R, bwd_dout=d_mix, ocol=mix_cv,
                                       name=f"short_conv_bwd_{l}")
        d_cc, d_cvv, d_cvw = conv3_bwd(d_cy, u, cvw_l, NC, R, LANE, prod_cols=(bcol + nh_cv, bcol + 2 * nh_cv),
                                       name=f"short_conv_taps_bwd_{l}")
        d_u = jnp.concatenate([dzf, dzb, dvf + dvb, d_nk, d_nv, dqf + dqb, d_hgg, d_nq, d_cb, d_cc, d_cvv],
                              axis=1).astype(BF16)
        big_grads[l]["w_in"] = mm_tn(sv["h"], d_u, 4, BF16, f"grad_w_in_{l}")
        d_h = mm_nt(d_u, wl["w_in"], F32, f"proj_in_bwd_{l}")
        small[l].update(hg_norm_w=d_hgn.sum(0)[0], na_q_norm_w=d_qn.sum(0)[0], na_k_norm_w=d_kn.sum(0)[0],
                        na_out_norm_w=d_on.reshape(-1), na_rpb=bias_pull[l](d_bias)[0], cv_w=d_cvw[:3],
                        cv_out_norm_w=d_cvo.reshape(-1), ln2_w=dln2.sum((0, 1)))
        if l > 0:
            pv, pm = saved[l - 1], mods[l - 1]
            d_x1, d_m2, dg2_prev, dln1, dsh1, dsc1 = gate_norm_bwd(pv["x1"], pv["m2"], pm["g2"], md["ln1"], md["sh1"],
                                                                   md["sc1"], d_x0, d_h, NC, R, f"gate_norm_end_bwd_{l - 1}")
        else:
            d_xin, _, _, dln1, dsh1, dsc1 = gate_norm_bwd(sv["x0"], None, None, md["ln1"], md["sh1"], md["sc1"], d_x0, d_h,
                                                          NC, R, "norm_in_bwd")
        small[l]["ln1_w"] = dln1.sum((0, 1))
        this_g2 = d_g2
        vecs = [v.sum(1) for v in (dsh1, dsc1, dg1, dsh2, dsc2, this_g2)]
        d_ada[l] = jnp.stack([jnp.concatenate([v[s] for v in vecs]) for s in (0, 1)])
        if l > 0:
            d_g2 = dg2_prev
        parts = [big_grads[l][n] for n in proj]
        got = swap_halves(parts, f"reduce_sibling_{l}")
        pairs[l] = [pair_sum(p, g, core, f"reduce_pair_sum_{n}_{l}") for n, p, g in zip(proj, parts, got)]
    grad_x = d_xin[NC:][None]
    d_logits = lb_pull(jnp.stack([jnp.stack([d_lb[l][k] for l in range(depth)]) for k in (0, 1)]))[0]

    rep_names = ("ln1_w", "ln2_w", "hg_norm_w", "na_q_norm_w", "na_k_norm_w", "na_rpb", "na_out_norm_w",
                 "cv_out_norm_w", "ffn_conv_b", "cv_w", "ffn_conv_w")
    parts3 = [jnp.stack([small[l][n] for l in range(depth)]) for n in rep_names]
    parts3 += [d_logits, jnp.stack([d_ada[l][0] for l in range(depth)]), jnp.stack([d_ada[l][1] for l in range(depth)])]
    buf3, lay3 = _flat_rows(parts3, F32)
    g3 = allgather8([buf3], "gather_small_grads")[0]
    tot3 = _unflat(sum_leading(g3, F32, "sum_small_grads"), lay3)
    gsm = dict(zip(rep_names, tot3[:len(rep_names)]))
    gsm["hg_lb_logits"] = tot3[len(rep_names)]
    dctx_tot, dlat_tot = tot3[-2], tot3[-1]
    dlat_each = jnp.stack([_unflat(g3[d], lay3)[-1] for d in range(8)], axis=1)
    grads = {n: gsm[n].reshape(W[n].shape) for n in rep_names if n not in SHARDED_SMALL}
    for n in SHARDED_SMALL:
        wl_ = W[n].shape[-1]
        grads[n] = lax.dynamic_slice_in_dim(gsm[n], chip * wl_, wl_, axis=gsm[n].ndim - 1)
    grads["b_ada"] = dctx_tot + dlat_tot

    ds16 = jnp.zeros((16, D), F32)
    gw_ada = []
    for l in range(depth):
        dm = jnp.concatenate([dlat_each[l], dctx_tot[l][None], jnp.zeros((7, ADA), F32)])
        dm = lax.dynamic_slice_in_dim(dm, chip * wcols, wcols, axis=1)
        gw_ada.append(mm_tn(s16, dm, 1, F32, f"grad_w_ada_{l}")[0])
        ds16 = ds16 + mm_nt(dm, w_ada[l][None], F32, f"ada_bwd_{l}")
    g4 = allgather8([ds16[8:16]], "gather_cond_grad")[0]
    d_scc = g4[0, 0] + g4[2, 0] + g4[4, 0] + g4[6, 0]
    sg = jax.nn.sigmoid(c_ctx)
    grads["c_ctx"] = d_scc * (sg * (1.0 + c_ctx * (1.0 - sg)))

    keys = [(l, n) for l in range(depth) for n in proj]
    quads[0] = chip_alltoall(pairs[0], "reduce_chips_0")
    mine = [sum_leading(q, F32, f"reduce_chip_sum_{n}_{l}") for (l, n), q in zip(keys, list(quads[0]) + list(quads[1]))]
    other = share_halves(mine, "reduce_share")
    mine_by, other_by = {n: [None] * depth for n in proj}, {n: [None] * depth for n in proj}
    for (l, n), a, b in zip(keys, mine, other):
        mine_by[n][l], other_by[n][l] = a, b

    delta, new_m, new_v = {}, {}, {}
    for n in BIG:
        shp = W[n].shape
        two = lambda a: a.reshape(-1, shp[-1])
        if n == "w_ada":
            g_, d_, m_, v_ = adamw(two(W[n]), gw_ada, two(Mo[n]), two(Vo[n]), f"adamw_{n}")
        else:
            g_, d_, m_, v_ = adamw_halves(two(W[n]), mine_by[n], other_by[n], two(Mo[n]), two(Vo[n]), core, f"adamw_{n}")
        grads[n], delta[n], new_m[n], new_v[n] = g_.reshape(shp), d_.reshape(shp), m_.reshape(shp), v_.reshape(shp)
    smalls = [n for n in WEIGHTS if n not in BIG]
    pw, lay_s = _flat_rows([W[n] for n in smalls], F32)
    pg, _ = _flat_rows([grads[n] for n in smalls], F32)
    pm, _ = _flat_rows([Mo[n] for n in smalls], F32)
    pvv, _ = _flat_rows([Vo[n] for n in smalls], F32)
    _, d_, m_, v_ = adamw(pw, [pg], pm, pvv, "adamw_small")
    for n, dd, mm_, vv in zip(smalls, _unflat(d_, lay_s), _unflat(m_, lay_s), _unflat(v_, lay_s)):
        delta[n], new_m[n], new_v[n] = dd, mm_, vv

    return (loss, grad_x, *[grads[n] for n in WEIGHTS], *[delta[n] for n in WEIGHTS],
            *[new_m[n] for n in WEIGHTS], *[new_v[n] for n in WEIGHTS])
```

```python
import functools
import math

import numpy as np
import jax
import jax.numpy as jnp
from jax import lax
from jax.experimental import pallas as pl
from jax.experimental.pallas import tpu as pltpu

F32 = jnp.float32
BF16 = jnp.bfloat16
MESH = pl.DeviceIdType.MESH
ANY = pl.BlockSpec(memory_space=pl.ANY)
VMEM_SPEC = pl.BlockSpec(memory_space=pltpu.VMEM)

LANE = 128
CHUNK = 64
SUB = 16
GRID_W = 64
WIN_R = 8
WIN_C = 16
EPS = 1e-6
F_FLOOR = 1e-30
NEG_INF = -1e30
EXP_CLAMP = 80.0
ATTN_SCALE = LANE ** -0.5
VMEM_LIMIT = 56 * 1024 * 1024
ADAM_LR, ADAM_B1, ADAM_B2, ADAM_EPS, ADAM_WD, ADAM_STEP = 0.001, 0.9, 0.999, 1e-08, 0.01, 10


def _cp(*sem):
    return pltpu.CompilerParams(dimension_semantics=sem or None, vmem_limit_bytes=VMEM_LIMIT)


def _me():
    return lax.axis_index("x"), lax.axis_index("y"), lax.axis_index("c")


def allgather8(blocks, name, hbm=False):
    na = len(blocks)
    comm = allgather8_comm(blocks)

    def body(*refs):
        comm["start"](refs[:na], refs[na:2 * na], refs[2 * na:])
        comm["finish"](refs[:na], refs[na:2 * na], refs[2 * na:])

    spec = ANY if hbm else VMEM_SPEC
    return pl.pallas_call(
        body, name=name, out_shape=comm["outs"], in_specs=[spec] * na, out_specs=[spec] * na,
        scratch_shapes=comm["scratch"], compiler_params=pltpu.CompilerParams(vmem_limit_bytes=VMEM_LIMIT),
    )(*blocks)


def allgather8_comm(blocks):
    na = len(blocks)

    def parts(x_refs, out_refs, sems):
        send_sems, recv_sems, local_sems = sems
        x, y, c = _me()
        me, sibling = (x, y, c), (x, y, 1 - c)
        chips = [(1 - x, y), (x, 1 - y), (1 - x, 1 - y)]

        def rows(a, px, py, pc):
            return out_refs[a].at[4 * px + 2 * py + pc]

        def copy(a, k, blk, to, src=None):
            return pltpu.make_async_remote_copy(
                src_ref=rows(a, *blk) if src is None else src, dst_ref=rows(a, *blk),
                send_sem=send_sems.at[a, k], recv_sem=recv_sems.at[a, k], device_id=to, device_id_type=MESH)

        mine = [pltpu.make_async_copy(x_refs[a], rows(a, *me), local_sems.at[a]) for a in range(na)]
        first = []
        for a in range(na):
            first.append(copy(a, 0, me, sibling, src=x_refs[a]))
            first += [copy(a, 1 + j, me, (*chip, c), src=x_refs[a]) for j, chip in enumerate(chips)]
        return c, me, sibling, chips, copy, mine, first

    def start(x_refs, out_refs, sems):
        _, _, _, _, _, mine, first = parts(x_refs, out_refs, sems)
        for cp in mine + first:
            cp.start()

    def finish(x_refs, out_refs, sems):
        c, me, sibling, chips, copy, mine, first = parts(x_refs, out_refs, sems)
        passed = []
        for j, chip in enumerate(chips):
            for a in range(na):
                copy(a, 1 + j, (*chip, c), me).wait_recv()
                passed.append(copy(a, 4 + j, (*chip, c), sibling))
                passed[-1].start()
        for a in range(na):
            copy(a, 0, sibling, me).wait_recv()
            for j, chip in enumerate(chips):
                copy(a, 4 + j, (*chip, 1 - c), me).wait_recv()
        for cp in first + passed:
            cp.wait_send()
        for cp in mine:
            cp.wait()

    return dict(ins=list(blocks), outs=[jax.ShapeDtypeStruct((8,) + b.shape, b.dtype) for b in blocks],
                scratch=[pltpu.SemaphoreType.DMA((na, 7)), pltpu.SemaphoreType.DMA((na, 7)),
                         pltpu.SemaphoreType.DMA((na,))], start=start, finish=finish)


def swap_halves(gs, name):
    na = len(gs)
    hrs = [g.shape[1] // 2 for g in gs]

    def body(*refs):
        g_refs, o_refs = refs[:na], refs[na:2 * na]
        send_sems, recv_sems = refs[2 * na:]
        x, y, c = _me()
        cps = []
        for a in range(na):
            for s in range(4):
                src = g_refs[a].at[s, pl.ds(pl.multiple_of((1 - c) * hrs[a], 16), hrs[a]), :]
                cps.append(pltpu.make_async_remote_copy(
                    src_ref=src, dst_ref=o_refs[a].at[s], send_sem=send_sems.at[a, s], recv_sem=recv_sems.at[a, s],
                    device_id=(x, y, 1 - c), device_id_type=MESH))
        for cp in cps:
            cp.start()
        for cp in cps:
            cp.wait()

    return pl.pallas_call(
        body, name=name, out_shape=[jax.ShapeDtypeStruct((4, hrs[a], gs[a].shape[2]), gs[a].dtype) for a in range(na)],
        in_specs=[ANY] * na, out_specs=[ANY] * na,
        scratch_shapes=[pltpu.SemaphoreType.DMA((na, 4)), pltpu.SemaphoreType.DMA((na, 4))],
    )(*gs)


def chip_alltoall(gs, name):
    na = len(gs)
    comm = chip_alltoall_comm(gs)

    def body(*refs):
        comm["start"](refs[:na], refs[na:2 * na], refs[2 * na:])
        comm["finish"](refs[:na], refs[na:2 * na], refs[2 * na:])

    return pl.pallas_call(body, name=name, out_shape=comm["outs"], in_specs=[ANY] * na, out_specs=[ANY] * na,
                          scratch_shapes=comm["scratch"])(*gs)


def chip_alltoall_comm(gs):
    na = len(gs)

    def copies(g_refs, o_refs, sems):
        send_sems, recv_sems, local_sems = sems
        x, y, c = _me()
        mine = 2 * x + y
        cps = []
        for a in range(na):
            cps.append(pltpu.make_async_copy(g_refs[a].at[mine], o_refs[a].at[mine], local_sems.at[a]))
            for k, (px, py) in enumerate([(1 - x, y), (x, 1 - y), (1 - x, 1 - y)]):
                cps.append(pltpu.make_async_remote_copy(
                    src_ref=g_refs[a].at[2 * px + py], dst_ref=o_refs[a].at[mine], send_sem=send_sems.at[a, k],
                    recv_sem=recv_sems.at[a, k], device_id=(px, py, c), device_id_type=MESH))
        return cps

    def start(g_refs, o_refs, sems):
        for cp in copies(g_refs, o_refs, sems):
            cp.start()

    def finish(g_refs, o_refs, sems):
        for cp in copies(g_refs, o_refs, sems):
            cp.wait()

    return dict(ins=list(gs), outs=[jax.ShapeDtypeStruct(g.shape, g.dtype) for g in gs],
                scratch=[pltpu.SemaphoreType.DMA((na, 3)), pltpu.SemaphoreType.DMA((na, 3)),
                         pltpu.SemaphoreType.DMA((na,))], start=start, finish=finish)


def share_halves(vs, name):
    na = len(vs)

    def body(*refs):
        v_refs, o_refs = refs[:na], refs[na:2 * na]
        send_sems, recv_sems = refs[2 * na:]
        x, y, c = _me()
        cps = [pltpu.make_async_remote_copy(
            src_ref=v_refs[a], dst_ref=o_refs[a], send_sem=send_sems.at[a], recv_sem=recv_sems.at[a],
            device_id=(x, y, 1 - c), device_id_type=MESH) for a in range(na)]
        for cp in cps:
            cp.start()
        for cp in cps:
            cp.wait()

    return pl.pallas_call(
        body, name=name, out_shape=[jax.ShapeDtypeStruct(v.shape, v.dtype) for v in vs],
        in_specs=[ANY] * na, out_specs=[ANY] * na,
        scratch_shapes=[pltpu.SemaphoreType.DMA((na,)), pltpu.SemaphoreType.DMA((na,))],
    )(*vs)


def _row_block(rows, cap):
    rb = math.gcd(rows, cap)
    return rb if rb % 8 == 0 else rows


def sum_leading(x, out_dtype, name):
    n, r, c = x.shape
    rb = _row_block(r, 1024)

    def body(x_ref, o_ref):
        acc = x_ref[0].astype(F32)
        for k in range(1, n):
            acc = acc + x_ref[k].astype(F32)
        o_ref[...] = acc.astype(o_ref.dtype)

    return pl.pallas_call(
        body, name=name, grid=(r // rb,), out_shape=jax.ShapeDtypeStruct((r, c), out_dtype),
        in_specs=[pl.BlockSpec((n, rb, c), lambda i: (0, i, 0))], out_specs=pl.BlockSpec((rb, c), lambda i: (i, 0)),
        compiler_params=_cp("parallel"),
    )(x)


def pair_sum(g, got, core, name):
    _, r2, n = g.shape
    hr = r2 // 2
    rb = math.gcd(hr, 512)
    nb = hr // rb

    def body(c_ref, a_ref, b_ref, o_ref):
        o_ref[...] = (a_ref[...].astype(F32) + b_ref[...].astype(F32)).astype(o_ref.dtype)

    spec = pl.BlockSpec((None, rb, n), lambda s, i, c_ref: (s, i, 0))
    return pl.pallas_call(
        body, name=name, out_shape=jax.ShapeDtypeStruct((4, hr, n), g.dtype),
        grid_spec=pltpu.PrefetchScalarGridSpec(
            num_scalar_prefetch=1, grid=(4, nb),
            in_specs=[pl.BlockSpec((None, rb, n), lambda s, i, c_ref: (s, c_ref[0] * nb + i, 0)), spec],
            out_specs=spec),
        compiler_params=_cp("parallel", "parallel"),
    )(core, g, got)


def adamw(w, gs, m, v, name):
    rows, c = w.shape
    ng = len(gs)
    r = rows // ng
    rb = _row_block(r, 128 if c > 2048 else 256 if c > 1024 else 1024)
    nb = r // rb
    bc1 = 1.0 - ADAM_B1 ** ADAM_STEP
    bc2 = 1.0 - ADAM_B2 ** ADAM_STEP

    def body(w_ref, *refs):
        g_refs, (m_ref, v_ref, g_out, d_ref, nm_ref, nv_ref) = refs[:ng], refs[ng:]
        part = pl.program_id(0) // nb
        gg = g_refs[0][...]
        for k in range(1, ng):
            gg = jnp.where(part == k, g_refs[k][...], gg)
        nm = ADAM_B1 * m_ref[...] + (1.0 - ADAM_B1) * gg
        nv = ADAM_B2 * v_ref[...] + (1.0 - ADAM_B2) * (gg * gg)
        g_out[...] = gg
        d_ref[...] = -ADAM_LR * ((nm / bc1) / (jnp.sqrt(nv / bc2) + ADAM_EPS) + ADAM_WD * w_ref[...])
        nm_ref[...] = nm
        nv_ref[...] = nv

    spec = pl.BlockSpec((rb, c), lambda i: (i, 0))
    gspecs = [pl.BlockSpec((rb, c), functools.partial(lambda k, i: (jnp.clip(i - k * nb, 0, nb - 1), 0), k))
              for k in range(ng)]
    sds = jax.ShapeDtypeStruct((rows, c), F32)
    return pl.pallas_call(
        body, name=name, grid=(ng * nb,), out_shape=(sds,) * 4, in_specs=[spec] + gspecs + [spec, spec],
        out_specs=(spec,) * 4, compiler_params=_cp("parallel"),
    )(w, *gs, m, v)


def adamw_halves(w, mine, other, m, v, core, name):
    rows, c = w.shape
    nl = len(mine)
    hr = rows // (2 * nl)
    rb = _row_block(hr, 128 if c > 2048 else 256 if c > 1024 else 1024)
    nb = hr // rb
    bc1 = 1.0 - ADAM_B1 ** ADAM_STEP
    bc2 = 1.0 - ADAM_B2 ** ADAM_STEP

    def body(c_ref, w_ref, *refs):
        mine_refs, other_refs = refs[:nl], refs[nl:2 * nl]
        m_ref, v_ref, g_out, d_ref, nm_ref, nv_ref = refs[2 * nl:]
        part = pl.program_id(0) // nb
        layer, half = part // 2, part % 2
        gg = jnp.where(half == c_ref[0], mine_refs[0][...], other_refs[0][...])
        for k in range(1, nl):
            gg = jnp.where(layer == k, jnp.where(half == c_ref[0], mine_refs[k][...], other_refs[k][...]), gg)
        nm = ADAM_B1 * m_ref[...] + (1.0 - ADAM_B1) * gg
        nv = ADAM_B2 * v_ref[...] + (1.0 - ADAM_B2) * (gg * gg)
        g_out[...] = gg
        d_ref[...] = -ADAM_LR * ((nm / bc1) / (jnp.sqrt(nv / bc2) + ADAM_EPS) + ADAM_WD * w_ref[...])
        nm_ref[...] = nm
        nv_ref[...] = nv

    spec = pl.BlockSpec((rb, c), lambda i, c_ref: (i, 0))
    gspecs = [pl.BlockSpec((rb, c), functools.partial(
        lambda k, i, c_ref: (jnp.clip(i - 2 * k * nb, 0, 2 * nb - 1) % nb, 0), k)) for k in range(nl)]
    sds = jax.ShapeDtypeStruct((rows, c), F32)
    return pl.pallas_call(
        body, name=name, out_shape=(sds,) * 4,
        grid_spec=pltpu.PrefetchScalarGridSpec(
            num_scalar_prefetch=1, grid=(2 * nl * nb,), in_specs=[spec] + gspecs + gspecs + [spec, spec],
            out_specs=(spec,) * 4),
        compiler_params=_cp("parallel"),
    )(core, w, *mine, *other, m, v)


def _pick(n, prefs):
    for p in prefs:
        if n % p == 0:
            return p
    return n


def _hosted(body, n_in, n_out, comm, first, last):
    if comm is None:
        return body
    k, ns = len(comm["ins"]), len(comm["scratch"])

    def wrapped(*refs):
        ins, cins = refs[:n_in], refs[n_in:n_in + k]
        outs, couts = refs[n_in + k:n_in + k + n_out], refs[n_in + k + n_out:n_in + 2 * k + n_out]
        rest = refs[n_in + 2 * k + n_out:]
        scratch, sems = rest[:len(rest) - ns], rest[len(rest) - ns:]

        @pl.when(first())
        def _():
            comm["start"](cins, couts, sems)

        body(*ins, *outs, *scratch)

        @pl.when(last())
        def _():
            comm["finish"](cins, couts, sems)

    return wrapped


def _comm_extras(comm):
    if comm is None:
        return [], [], [], []
    return list(comm["ins"]), [ANY] * len(comm["ins"]), list(comm["outs"]), list(comm["scratch"])


def _mm_body(dims, nk, out_dtype):
    def body(a_ref, b_ref, o_ref, acc=None):
        kk = pl.program_id(2)
        part = lax.dot_general(a_ref[...].astype(BF16), b_ref[...].astype(BF16), (dims, ((), ())),
                               preferred_element_type=F32)
        if nk == 1:
            o_ref[...] = part.astype(out_dtype)
        else:
            @pl.when(kk == 0)
            def _():
                acc[...] = part

            @pl.when(kk > 0)
            def _():
                acc[...] += part

            @pl.when(kk == nk - 1)
            def _():
                o_ref[...] = acc[...].astype(out_dtype)
    return body


def _acc(nk, shape):
    return [pltpu.VMEM(shape, F32)] if nk > 1 else []


def mm_nn(a, w, out_dtype, name, comm=None):
    M, K = a.shape
    S, _, Ns = w.shape
    tm = _pick(M, (1088, 1024, 512, 256, 128))
    tn = _pick(Ns, (1024, 896, 1408, 512, 256, 128))
    tk = _pick(K, (2816, 2048, 1408, 1024, 512, 256, 128))
    nps, nk = Ns // tn, K // tk
    grid = (S * nps, M // tm, nk)
    ids = lambda: [pl.program_id(d) for d in range(3)]
    first = lambda: functools.reduce(jnp.logical_and, [p == 0 for p in ids()])
    last = lambda: functools.reduce(jnp.logical_and, [p == g - 1 for p, g in zip(ids(), grid)])
    cin, cspec, cout, csem = _comm_extras(comm)
    out = pl.pallas_call(
        _hosted(_mm_body(((1,), (0,)), nk, out_dtype), 2, 1, comm, first, last), name=name, grid=grid,
        out_shape=[jax.ShapeDtypeStruct((M, S * Ns), out_dtype)] + cout,
        in_specs=[pl.BlockSpec((tm, tk), lambda j, i, k: (i, k)),
                  pl.BlockSpec((None, tk, tn), lambda j, i, k: (j // nps, k, j % nps))] + cspec,
        out_specs=[pl.BlockSpec((tm, tn), lambda j, i, k: (i, j))] + cspec,
        scratch_shapes=_acc(nk, (tm, tn)) + csem,
        compiler_params=_cp(*(("arbitrary",) * 3 if comm else ("parallel", "parallel", "arbitrary"))),
    )(a, w, *cin)
    return out[0] if comm is None else out


def mm_nt(dy, w, out_dtype, name, comm=None):
    M, N = dy.shape
    S, K, Ns = w.shape
    tm = _pick(M, (1088, 1024, 512, 256, 128))
    tn = _pick(K, (1408, 1024, 512, 256, 128))
    tk = _pick(Ns, (2816, 2048, 1792, 1408, 1024, 896, 512, 256, 128))
    kps, nk = Ns // tk, N // tk
    out = _pcall(
        _mm_body(((1,), (1,)), nk, out_dtype), name=name, grid=(K // tn, M // tm, nk),
        out_shape=[jax.ShapeDtypeStruct((M, K), out_dtype)],
        in_specs=[pl.BlockSpec((tm, tk), lambda j, i, k: (i, k)),
                  pl.BlockSpec((None, tn, tk), lambda j, i, k: (k // kps, j, k % kps))],
        out_specs=[pl.BlockSpec((tm, tn), lambda j, i, k: (i, j))], args=(dy, w),
        scratch=_acc(nk, (tm, tn)), sem=("parallel", "parallel", "arbitrary"), comm=comm)
    return out[0] if comm is None else out


def mm_tn(a, dy, S, out_dtype, name):
    M, K = a.shape
    N = dy.shape[1]
    Ns = N // S
    to = _pick(K, (1024, 512, 256, 128))
    tn = _pick(Ns, (2816, 2048, 1792, 1408, 1024, 896, 512, 256, 128))
    tk = _pick(M, (1088, 1024, 512, 256, 128))
    nps, nk = Ns // tn, M // tk
    return pl.pallas_call(
        _mm_body(((0,), (0,)), nk, out_dtype), name=name, grid=(K // to, S * nps, nk),
        out_shape=jax.ShapeDtypeStruct((S, K, Ns), out_dtype),
        in_specs=[pl.BlockSpec((tk, to), lambda i, j, k: (k, i)),
                  pl.BlockSpec((tk, tn), lambda i, j, k: (k, j))],
        out_specs=pl.BlockSpec((None, to, tn), lambda i, j, k: (j // nps, i, j % nps)),
        scratch_shapes=_acc(nk, (to, tn)), compiler_params=_cp("parallel", "parallel", "arbitrary"),
    )(a, dy)


_DIMS = {"nn": ((1,), (0,)), "nt": ((1,), (1,)), "tn": ((0,), (0,))}


def _dot(a, b, mode):
    return lax.dot_general(a.astype(BF16), b.astype(BF16), (_DIMS[mode], ((), ())), preferred_element_type=F32)


@functools.partial(jax.custom_vjp, nondiff_argnums=(2,))
def mmf(a, b, mode):
    return _dot(a, b, mode)


def _mmf_fwd(a, b, mode):
    return _dot(a, b, mode), (a, b)


def _mmf_bwd(mode, res, ct):
    a, b = res
    if mode == "nn":
        return _dot(ct, b, "nt"), _dot(a, ct, "tn")
    if mode == "nt":
        return _dot(ct, b, "nn"), _dot(ct, a, "tn")
    return _dot(b, ct, "nt"), _dot(a, ct, "nn")


mmf.defvjp(_mmf_fwd, _mmf_bwd)


def _dot_hi(m, g):
    return jnp.dot(m, g, precision=lax.Precision.HIGHEST, preferred_element_type=F32)


@jax.custom_vjp
def cumdot(m, mt, g):
    return _dot_hi(m, g)


def _cumdot_fwd(m, mt, g):
    return _dot_hi(m, g), (m, mt)


def _cumdot_bwd(res, ct):
    m, mt = res
    return jnp.zeros_like(m), jnp.zeros_like(mt), _dot_hi(mt, ct)


cumdot.defvjp(_cumdot_fwd, _cumdot_bwd)


def _rms(x, w):
    return x * lax.rsqrt(jnp.mean(x * x, axis=-1, keepdims=True) + EPS) * w


def _silu(x):
    return x * jax.nn.sigmoid(x)


RT = 16


def _gn_math(has_gate, x, m, gate, lnw, shift, scale):
    xn = x + gate * m if has_gate else x
    h = _rms(xn, lnw) * (1.0 + scale) + shift
    return xn, h


def _seg_spec(width, ncb):
    return pl.BlockSpec((None, RT, width), lambda i: (jnp.minimum(i // ncb, 1), 0, 0))


def gate_norm(x, m, gate, lnw, shift, scale, nc, R, name):
    T, D = x.shape
    has_gate = m is not None
    ncb = nc // R

    def body(*refs):
        if has_gate:
            x_ref, m_ref, g_ref, w_ref, sh_ref, sc_ref, xn_ref, h_ref = refs
        else:
            x_ref, w_ref, sh_ref, sc_ref, h_ref = refs

        def step(t, carry):
            rows = pl.ds(pl.multiple_of(t * RT, RT), RT)
            xn, h = _gn_math(has_gate, x_ref[rows, :], m_ref[rows, :] if has_gate else None,
                             g_ref[...] if has_gate else None, w_ref[...], sh_ref[...], sc_ref[...])
            if has_gate:
                xn_ref[rows, :] = xn
            h_ref[rows, :] = h.astype(BF16)
            return carry

        lax.fori_loop(0, R // RT, step, 0)

    row = pl.BlockSpec((R, D), lambda i: (i, 0))
    seg = _seg_spec(D, ncb)
    shared = pl.BlockSpec((None, RT, D), lambda i: (0, 0, 0))
    if has_gate:
        ins, in_specs = (x, m, gate, lnw, shift, scale), [row, row, seg, shared, seg, seg]
        out_shape = (jax.ShapeDtypeStruct((T, D), F32), jax.ShapeDtypeStruct((T, D), BF16))
        out_specs = (row, row)
    else:
        ins, in_specs = (x, lnw, shift, scale), [row, shared, seg, seg]
        out_shape, out_specs = jax.ShapeDtypeStruct((T, D), BF16), row
    out = pl.pallas_call(body, name=name, grid=(T // R,), out_shape=out_shape, in_specs=in_specs,
                         out_specs=out_specs, compiler_params=_cp("parallel"))(*ins)
    return out if has_gate else (None, out)


def gate_norm_bwd(x, m, gate, lnw, shift, scale, dxn, dh, nc, R, name):
    T, D = x.shape
    has_gate = m is not None
    ncb = nc // R

    def body(*refs):
        if has_gate:
            (x_ref, m_ref, g_ref, w_ref, sh_ref, sc_ref, dxn_ref, dh_ref,
             dx_ref, dm_ref, dg_ref, dw_ref, dsh_ref, dsc_ref) = refs
        else:
            x_ref, w_ref, sh_ref, sc_ref, dxn_ref, dh_ref, dx_ref, dw_ref, dsh_ref, dsc_ref = refs
        i = pl.program_id(0)

        @pl.when(i == 0)
        def _():
            dw_ref[...] = jnp.zeros_like(dw_ref)

        @pl.when((i == 0) | (i == ncb))
        def _():
            dsh_ref[...] = jnp.zeros_like(dsh_ref)
            dsc_ref[...] = jnp.zeros_like(dsc_ref)
            if has_gate:
                dg_ref[...] = jnp.zeros_like(dg_ref)

        def step(t, carry):
            rows = pl.ds(pl.multiple_of(t * RT, RT), RT)
            ct = (dxn_ref[rows, :], dh_ref[rows, :])
            if has_gate:
                _, vjp = jax.vjp(functools.partial(_gn_math, True), x_ref[rows, :], m_ref[rows, :], g_ref[...],
                                 w_ref[...], sh_ref[...], sc_ref[...])
                dx, dm, dg, dw, dsh, dsc = vjp(ct)
                dm_ref[rows, :] = dm.astype(BF16)
                dg_ref[...] += dg
            else:
                f = lambda x_, w_, sh_, sc_: _gn_math(False, x_, None, None, w_, sh_, sc_)[1]
                _, vjp = jax.vjp(f, x_ref[rows, :], w_ref[...], sh_ref[...], sc_ref[...])
                dx, dw, dsh, dsc = vjp(ct[1])
                dx = dx + ct[0]
            dx_ref[rows, :] = dx
            dw_ref[...] += dw
            dsh_ref[...] += dsh
            dsc_ref[...] += dsc
            return carry

        lax.fori_loop(0, R // RT, step, 0)

    row = pl.BlockSpec((R, D), lambda i: (i, 0))
    seg = _seg_spec(D, ncb)
    shared = pl.BlockSpec((None, RT, D), lambda i: (0, 0, 0))
    full, segs, one = jax.ShapeDtypeStruct((T, D), F32), jax.ShapeDtypeStruct((2, RT, D), F32), \
        jax.ShapeDtypeStruct((1, RT, D), F32)
    if has_gate:
        ins = (x, m, gate, lnw, shift, scale, dxn, dh)
        in_specs = [row, row, seg, shared, seg, seg, row, row]
        out_shape = (full, jax.ShapeDtypeStruct((T, D), BF16), segs, one, segs, segs)
        out_specs = (row, row, seg, shared, seg, seg)
    else:
        ins = (x, lnw, shift, scale, dxn, dh)
        in_specs = [row, shared, seg, seg, row, row]
        out_shape = (full, one, segs, segs)
        out_specs = (row, shared, seg, seg)
    out = pl.pallas_call(body, name=name, grid=(T // R,), out_shape=out_shape, in_specs=in_specs,
                         out_specs=out_specs, compiler_params=_cp("arbitrary"))(*ins)
    if has_gate:
        return out
    dx, dw, dsh, dsc = out
    return dx, None, None, dw, dsh, dsc


def gate_loss(x, m, gate, target, nc, R, name):
    T, D = x.shape
    ncb = nc // R

    def body(x_ref, m_ref, g_ref, t_ref, loss_ref, dx_ref, dm_ref, dg_ref):
        i = pl.program_id(0)

        @pl.when(i == 0)
        def _():
            loss_ref[...] = jnp.zeros_like(loss_ref)

        @pl.when((i == 0) | (i == ncb))
        def _():
            dg_ref[...] = jnp.zeros_like(dg_ref)

        live = jnp.where(i >= ncb, 1.0, 0.0).astype(F32)

        def step(t, carry):
            rows = pl.ds(pl.multiple_of(t * RT, RT), RT)
            mm_ = m_ref[rows, :]
            g = g_ref[...]
            e = (x_ref[rows, :] + g * mm_ - t_ref[rows, :]) * live
            dy = e * (1.0 / D)
            loss_ref[...] += 0.5 * e * dy
            dx_ref[rows, :] = dy
            dm_ref[rows, :] = (dy * g).astype(BF16)
            dg_ref[...] += dy * mm_
            return carry

        lax.fori_loop(0, R // RT, step, 0)

    row = pl.BlockSpec((R, D), lambda i: (i, 0))
    seg = _seg_spec(D, ncb)
    return pl.pallas_call(
        body, name=name, grid=(T // R,),
        out_shape=(jax.ShapeDtypeStruct((RT, D), F32), jax.ShapeDtypeStruct((T, D), F32),
                   jax.ShapeDtypeStruct((T, D), BF16), jax.ShapeDtypeStruct((2, RT, D), F32)),
        in_specs=[row, row, seg, pl.BlockSpec((R, D), lambda i: (jnp.maximum(i - ncb, 0), 0))],
        out_specs=(pl.BlockSpec((RT, D), lambda i: (0, 0)), row, row, seg),
        compiler_params=_cp("arbitrary"),
    )(x, m, gate, target)


def _hg_chunk(rev, lb, z, iv, hq, st):
    f = lb + (1.0 - lb) * jax.nn.sigmoid(z)
    g = jnp.log(jnp.maximum(f, F_FLOOR))
    k = (1.0 - lb) * jax.nn.sigmoid(-z)
    q = _silu(hq)
    ri = lax.broadcasted_iota(jnp.int32, (CHUNK, CHUNK), 0)
    ci = lax.broadcasted_iota(jnp.int32, (CHUNK, CHUNK), 1)
    r1 = lax.broadcasted_iota(jnp.int32, (CHUNK, 1), 0)
    seen = (ci >= ri) if rev else (ci <= ri)
    seen_t = (ci <= ri) if rev else (ci >= ri)
    cum = cumdot(seen.astype(F32), seen_t.astype(F32), g)
    tot = jnp.sum(g, axis=0, keepdims=True)
    att = jnp.zeros((CHUNK, CHUNK), F32)
    ref_rows = jnp.zeros_like(g)
    refs = []
    for b in range(CHUNK // SUB):
        before = (r1 >= SUB * (b + 1)) if rev else (r1 < SUB * b)
        r_b = jnp.sum(jnp.where(before, g, 0.0), axis=0, keepdims=True)
        in_b = (r1 >= SUB * b) & (r1 < SUB * (b + 1))
        ref_rows = ref_rows + jnp.where(in_b, r_b, 0.0)
        refs.append(r_b)
    qd = q * jnp.exp(cum - ref_rows)
    for b in range(CHUNK // SUB):
        kd = k * jnp.exp(jnp.minimum(refs[b] - cum, EXP_CLAMP))
        in_b = (ri >= SUB * b) & (ri < SUB * (b + 1))
        att = att + jnp.where(in_b, mmf(qd, kd, "nt"), 0.0)
    att = jnp.where(seen, att, 0.0)
    o = mmf(att, iv, "nn") + mmf(q * jnp.exp(cum), st, "nt")
    st_new = st * jnp.exp(tot) + mmf(iv, k * jnp.exp(tot - cum), "tn")
    return st_new, o


def _hg_cid(rev, i, ncs, n):
    if not rev:
        return i
    return jnp.where(i < ncs, ncs - 1 - i, ncs + n - 1 - i)


def hgrn_fwd(u, lb, rev, zcol, nc, hgw, name):
    T = u.shape[0]
    n, ncs, nh = T // CHUNK, nc // CHUNK, hgw // LANE

    def body(z_ref, v_ref, q_ref, lb_ref, o_ref, s_ref, st):
        i = pl.program_id(0)

        @pl.when(i == 0)
        def _():
            st[...] = jnp.zeros_like(st)

        for h in range(nh):
            cols = slice(h * LANE, (h + 1) * LANE)
            s_ref[h] = st[h]
            s_new, o = _hg_chunk(rev, lb_ref[:, cols], z_ref[:, cols], v_ref[:, cols], q_ref[:, cols], st[h])
            st[h] = s_new
            o_ref[:, cols] = o

    def col(cb):
        return pl.BlockSpec((CHUNK, hgw), lambda i: (_hg_cid(rev, i, ncs, n), cb))

    return pl.pallas_call(
        body, name=name, grid=(n,),
        out_shape=(jax.ShapeDtypeStruct((T, hgw), F32), jax.ShapeDtypeStruct((n, nh, LANE, LANE), F32)),
        in_specs=[col(zcol), col(2), col(7), pl.BlockSpec((1, hgw), lambda i: (0, 0))],
        out_specs=(pl.BlockSpec((CHUNK, hgw), lambda i: (_hg_cid(rev, i, ncs, n), 0)),
                   pl.BlockSpec((None, nh, LANE, LANE), lambda i: (i, 0, 0, 0))),
        scratch_shapes=[pltpu.VMEM((nh, LANE, LANE), F32)], compiler_params=_cp("arbitrary"),
    )(u, u, u, lb)


def hgrn_bwd(u, lb, states, do, rev, zcol, nc, hgw, name):
    T = u.shape[0]
    n, ncs, nh = T // CHUNK, nc // CHUNK, hgw // LANE

    def body(z_ref, v_ref, q_ref, lb_ref, s_ref, do_ref, dz_ref, dv_ref, dq_ref, dlb_ref, dst):
        j = pl.program_id(0)

        @pl.when(j == 0)
        def _():
            dst[...] = jnp.zeros_like(dst)
            dlb_ref[...] = jnp.zeros_like(dlb_ref)

        for h in range(nh):
            cols = slice(h * LANE, (h + 1) * LANE)
            _, vjp = jax.vjp(functools.partial(_hg_chunk, rev), lb_ref[:, cols], z_ref[:, cols], v_ref[:, cols],
                             q_ref[:, cols], s_ref[h])
            dlb, dz, dv, dq, ds = vjp((dst[h], do_ref[:, cols]))
            dst[h] = ds
            dz_ref[:, cols] = dz
            dv_ref[:, cols] = dv
            dq_ref[:, cols] = dq
            dlb_ref[:, cols] += dlb

    def cid(j):
        return _hg_cid(rev, n - 1 - j, ncs, n)

    def col(cb):
        return pl.BlockSpec((CHUNK, hgw), lambda j: (cid(j), cb))

    out = pl.BlockSpec((CHUNK, hgw), lambda j: (cid(j), 0))
    full = jax.ShapeDtypeStruct((T, hgw), F32)
    return pl.pallas_call(
        body, name=name, grid=(n,),
        out_shape=(full, full, full, jax.ShapeDtypeStruct((1, hgw), F32)),
        in_specs=[col(zcol), col(2), col(7), pl.BlockSpec((1, hgw), lambda j: (0, 0)),
                  pl.BlockSpec((None, nh, LANE, LANE), lambda j: (n - 1 - j, 0, 0, 0)), out],
        out_specs=(out, out, out, pl.BlockSpec((1, hgw), lambda j: (0, 0))),
        scratch_shapes=[pltpu.VMEM((nh, LANE, LANE), F32)], compiler_params=_cp("arbitrary"),
    )(u, u, u, lb, states, do)


HT = 128


def _read_math(ofw, obw, g, w):
    return _rms(ofw + obw, w) * _silu(g)


def hg_read(ofw, obw, u, w, gcol, R, name):
    T, hgw = ofw.shape
    nh = hgw // LANE

    def body(a_ref, b_ref, g_ref, w_ref, o_ref):
        for t in range(R // HT):
            rows = slice(t * HT, (t + 1) * HT)
            o_ref[rows, :] = _read_math(a_ref[rows, :], b_ref[rows, :], g_ref[rows, :], w_ref[...]).astype(BF16)

    blk = pl.BlockSpec((R, LANE), lambda i, h: (i, h))
    return pl.pallas_call(
        body, name=name, grid=(T // R, nh), out_shape=jax.ShapeDtypeStruct((T, hgw), BF16),
        in_specs=[blk, blk, pl.BlockSpec((R, LANE), lambda i, h: (i, gcol + h)),
                  pl.BlockSpec((1, LANE), lambda i, h: (0, 0))],
        out_specs=blk, compiler_params=_cp("parallel", "parallel"),
    )(ofw, obw, u, w)


def hg_read_bwd(ofw, obw, u, w, dout, gcol, ocol, R, name):
    T, hgw = ofw.shape
    nh = hgw // LANE

    def body(a_ref, b_ref, g_ref, w_ref, d_ref, do_ref, dg_ref, dw_ref):
        @pl.when(pl.program_id(1) == 0)
        def _():
            dw_ref[...] = jnp.zeros_like(dw_ref)

        for t in range(R // HT):
            rows = slice(t * HT, (t + 1) * HT)
            _, vjp = jax.vjp(_read_math, a_ref[rows, :], b_ref[rows, :], g_ref[rows, :], w_ref[...])
            da, _, dg, dw = vjp(d_ref[rows, :])
            do_ref[rows, :] = da
            dg_ref[rows, :] = dg
            dw_ref[...] += dw

    blk = pl.BlockSpec((R, LANE), lambda h, i: (i, h))
    full = jax.ShapeDtypeStruct((T, hgw), F32)
    return pl.pallas_call(
        body, name=name, grid=(nh, T // R), out_shape=(full, full, jax.ShapeDtypeStruct((nh, 1, LANE), F32)),
        in_specs=[blk, blk, pl.BlockSpec((R, LANE), lambda h, i: (i, gcol + h)),
                  pl.BlockSpec((1, LANE), lambda h, i: (0, 0)), pl.BlockSpec((R, LANE), lambda h, i: (i, ocol + h))],
        out_specs=(blk, blk, pl.BlockSpec((None, 1, LANE), lambda h, i: (h, 0, 0))),
        compiler_params=_cp("parallel", "arbitrary"),
    )(ofw, obw, u, w, dout)


def _na_step(qw, ow, bias, qraw, kl, vl, kc, vc):
    q = _rms(qraw, qw)
    s_loc = mmf(q, kl, "nt") * ATTN_SCALE + bias
    s_ctx = mmf(q, kc, "nt") * ATTN_SCALE
    m = lax.stop_gradient(jnp.maximum(jnp.max(s_loc, axis=-1, keepdims=True), jnp.max(s_ctx, axis=-1, keepdims=True)))
    p_loc = jnp.exp(s_loc - m)
    p_ctx = jnp.exp(s_ctx - m)
    inv = 1.0 / (jnp.sum(p_loc, axis=-1, keepdims=True) + jnp.sum(p_ctx, axis=-1, keepdims=True))
    return _rms(mmf(p_loc * inv, vl, "nn") + mmf(p_ctx * inv, vc, "nn"), ow)


def _na_geometry(nc, rows):
    ncs = nc // GRID_W
    win_r = min(WIN_R, rows)
    nloc = win_r * GRID_W

    def row_start(s):
        r = jnp.maximum(s - ncs, 0)
        return jnp.clip(r - win_r // 2, 0, rows - win_r)

    def bias_idx(s):
        r = s - ncs
        return jnp.where(s < ncs, win_r, r - jnp.clip(r - win_r // 2, 0, rows - win_r))

    return ncs, win_r, nloc, row_start, bias_idx


def na_bias_tables(rpb, rows):
    win_r = min(WIN_R, rows)
    nh = rpb.shape[0]
    sel_r = np.zeros((win_r, win_r, 2 * WIN_R - 1), np.float32)
    for off in range(win_r):
        for jr in range(win_r):
            sel_r[off, jr, jr - off + WIN_R - 1] = 1.0
    qc = np.arange(GRID_W)[:, None]
    kc = np.arange(GRID_W)[None, :]
    wstart = np.clip(qc - WIN_C // 2, 0, GRID_W - WIN_C)
    ok = (kc >= wstart) & (kc < wstart + WIN_C)
    sel_c = np.zeros((GRID_W, GRID_W, 2 * WIN_C - 1), np.float32)
    sel_c[np.broadcast_to(qc, ok.shape)[ok], np.broadcast_to(kc, ok.shape)[ok], (kc - qc + WIN_C - 1)[ok]] = 1.0
    hi = lax.Precision.HIGHEST
    t = jnp.einsum("hab,oja->hojb", rpb, sel_r, precision=hi)
    t = jnp.einsum("hojb,qkb->hoqjk", t, sel_c, precision=hi)
    t = jnp.where(ok[None, None, :, None, :], t, NEG_INF)
    t = jnp.concatenate([t, jnp.full((nh, 1, GRID_W, win_r, GRID_W), NEG_INF, F32)], axis=1)
    return t.reshape(nh, win_r + 1, GRID_W, win_r * GRID_W)


def kv_prep(u, kw, kcol, vcol, naw, R, name):
    T = u.shape[0]

    def body(k_ref, v_ref, w_ref, kn_ref, vb_ref):
        kn_ref[...] = _rms(k_ref[...], w_ref[...]).astype(BF16)
        vb_ref[...] = v_ref[...].astype(BF16)

    blk = pl.BlockSpec((R, LANE), lambda i, h: (i, h))
    sds = jax.ShapeDtypeStruct((T, naw), BF16)
    return pl.pallas_call(
        body, name=name, grid=(T // R, naw // LANE), out_shape=(sds, sds),
        in_specs=[pl.BlockSpec((R, LANE), lambda i, h: (i, kcol + h)), pl.BlockSpec((R, LANE), lambda i, h: (i, vcol + h)),
                  pl.BlockSpec((1, LANE), lambda i, h: (0, 0))],
        out_specs=(blk, blk), compiler_params=_cp("parallel", "parallel"),
    )(u, u, kw)


def kv_prep_bwd(u, kw, dkn, kcol, naw, R, name):
    T = u.shape[0]
    nh = naw // LANE

    def body(k_ref, w_ref, d_ref, dk_ref, dw_ref):
        @pl.when(pl.program_id(1) == 0)
        def _():
            dw_ref[...] = jnp.zeros_like(dw_ref)

        for t in range(R // HT):
            rows = slice(t * HT, (t + 1) * HT)
            _, vjp = jax.vjp(_rms, k_ref[rows, :], w_ref[...])
            dk, dw = vjp(d_ref[rows, :])
            dk_ref[rows, :] = dk
            dw_ref[...] += dw

    blk = pl.BlockSpec((R, LANE), lambda h, i: (i, h))
    return pl.pallas_call(
        body, name=name, grid=(nh, T // R),
        out_shape=(jax.ShapeDtypeStruct((T, naw), F32), jax.ShapeDtypeStruct((nh, 1, LANE), F32)),
        in_specs=[pl.BlockSpec((R, LANE), lambda h, i: (i, kcol + h)), pl.BlockSpec((1, LANE), lambda h, i: (0, 0)), blk],
        out_specs=(blk, pl.BlockSpec((None, 1, LANE), lambda h, i: (h, 0, 0))),
        compiler_params=_cp("parallel", "arbitrary"),
    )(u, kw, dkn)


NA_HB = 4


def _na_operands(j, s, nc, nloc, row_start, q_refs, k_ref, v_ref, qw_ref, ow_ref, b_ref):
    cols = slice(j * LANE, (j + 1) * LANE)
    loc = pl.ds(pl.multiple_of(nc + row_start(s) * GRID_W, GRID_W), nloc)
    ops = (qw_ref[...], ow_ref[:, cols], b_ref[j], q_refs[j][...], k_ref[loc, cols].astype(F32),
           v_ref[loc, cols].astype(F32), k_ref[0:nc, cols].astype(F32), v_ref[0:nc, cols].astype(F32))
    return cols, loc, ops


def _grid_ends(grid):
    ids = lambda: [pl.program_id(d) for d in range(len(grid))]
    first = lambda: functools.reduce(jnp.logical_and, [p == 0 for p in ids()])
    last = lambda: functools.reduce(jnp.logical_and, [p == g - 1 for p, g in zip(ids(), grid)])
    return first, last


def na_fwd(u, kn, vb, qw, ow, bias, qcol, nc, name, comm=None):
    T, naw = kn.shape
    nh, rows = naw // LANE, (T - nc) // GRID_W
    hb = NA_HB if nh % NA_HB == 0 else 1
    ncs, win_r, nloc, row_start, bias_idx = _na_geometry(nc, rows)

    def body(*refs):
        q_refs, (k_ref, v_ref, qw_ref, ow_ref, b_ref, o_ref) = refs[:hb], refs[hb:]
        s = pl.program_id(1)
        for j in range(hb):
            cols, _, ops = _na_operands(j, s, nc, nloc, row_start, q_refs, k_ref, v_ref, qw_ref, ow_ref, b_ref)
            o_ref[:, cols] = _na_step(*ops).astype(BF16)

    wide = pl.BlockSpec((T, hb * LANE), lambda g, s: (0, g), pipeline_mode=pl.Buffered(1))
    grid = (nh // hb, T // GRID_W)
    cin, cspec, cout, csem = _comm_extras(comm)
    out = pl.pallas_call(
        _hosted(body, hb + 5, 1, comm, *_grid_ends(grid)), name=name, grid=grid,
        out_shape=[jax.ShapeDtypeStruct((T, naw), BF16)] + cout,
        in_specs=[pl.BlockSpec((GRID_W, LANE), functools.partial(lambda j, g, s: (s, qcol + g * hb + j), j))
                  for j in range(hb)]
        + [wide, wide, pl.BlockSpec((1, LANE), lambda g, s: (0, 0)), pl.BlockSpec((1, hb * LANE), lambda g, s: (0, g)),
           pl.BlockSpec((hb, None, GRID_W, nloc), lambda g, s: (g, bias_idx(s), 0, 0))] + cspec,
        out_specs=[pl.BlockSpec((GRID_W, hb * LANE), lambda g, s: (s, g))] + cspec, scratch_shapes=csem,
        compiler_params=_cp("arbitrary", "arbitrary"),
    )(*([u] * hb), kn, vb, qw, ow, bias, *cin)
    return out[0] if comm is None else out


def na_bwd(u, kn, vb, qw, ow, bias, dout, qcol, ocol, nc, name, comm=None):
    T, naw = kn.shape
    nh, rows = naw // LANE, (T - nc) // GRID_W
    hb = NA_HB if nh % NA_HB == 0 else 1
    ncs, win_r, nloc, row_start, bias_idx = _na_geometry(nc, rows)
    fresh = [0] + [ncs + r for r in range(rows) if r == 0 or r - np.clip(r - win_r // 2, 0, rows - win_r)
                   != (r - 1) - np.clip(r - 1 - win_r // 2, 0, rows - win_r)]

    def body(*refs):
        q_refs, d_refs = refs[:hb], refs[hb:2 * hb]
        k_ref, v_ref, qw_ref, ow_ref, b_ref, dq_ref, dk_ref, dv_ref, db_ref, dqw_ref, dow_ref = refs[2 * hb:]
        s = pl.program_id(1)

        @pl.when(s == 0)
        def _():
            dk_ref[...] = jnp.zeros_like(dk_ref)
            dv_ref[...] = jnp.zeros_like(dv_ref)
            dqw_ref[...] = jnp.zeros_like(dqw_ref)
            dow_ref[...] = jnp.zeros_like(dow_ref)

        first = functools.reduce(lambda a, b: a | b, [s == f for f in fresh])

        @pl.when(first)
        def _():
            db_ref[...] = jnp.zeros_like(db_ref)

        for j in range(hb):
            cols, loc, ops = _na_operands(j, s, nc, nloc, row_start, q_refs, k_ref, v_ref, qw_ref, ow_ref, b_ref)
            _, vjp = jax.vjp(_na_step, *ops)
            dqw, dow, db, dq, dkl, dvl, dkc, dvc = vjp(d_refs[j][...])
            dq_ref[:, cols] = dq
            dk_ref[loc, cols] += dkl
            dv_ref[loc, cols] += dvl
            dk_ref[0:nc, cols] += dkc
            dv_ref[0:nc, cols] += dvc
            db_ref[j] += db
            dqw_ref[j] += dqw
            dow_ref[j] += dow

    wide = pl.BlockSpec((T, hb * LANE), lambda g, s: (0, g), pipeline_mode=pl.Buffered(1))
    hvec = pl.BlockSpec((hb, 1, LANE), lambda g, s: (g, 0, 0))
    full = jax.ShapeDtypeStruct((T, naw), F32)
    hv = jax.ShapeDtypeStruct((nh, 1, LANE), F32)
    bspec = pl.BlockSpec((hb, None, GRID_W, nloc), lambda g, s: (g, bias_idx(s), 0, 0))
    grid = (nh // hb, T // GRID_W)
    cin, cspec, cout, csem = _comm_extras(comm)
    return pl.pallas_call(
        _hosted(body, 2 * hb + 5, 6, comm, *_grid_ends(grid)), name=name, grid=grid,
        out_shape=[full, full, full, jax.ShapeDtypeStruct(bias.shape, F32), hv, hv] + cout,
        in_specs=[pl.BlockSpec((GRID_W, LANE), functools.partial(lambda j, g, s: (s, qcol + g * hb + j), j))
                  for j in range(hb)]
        + [pl.BlockSpec((GRID_W, LANE), functools.partial(lambda j, g, s: (s, ocol + g * hb + j), j))
           for j in range(hb)]
        + [wide, wide, pl.BlockSpec((1, LANE), lambda g, s: (0, 0)), pl.BlockSpec((1, hb * LANE), lambda g, s: (0, g)),
           bspec] + cspec,
        out_specs=[pl.BlockSpec((GRID_W, hb * LANE), lambda g, s: (s, g)), wide, wide, bspec, hvec, hvec] + cspec,
        scratch_shapes=csem, compiler_params=_cp("arbitrary", "arbitrary"),
    )(*([u] * hb), *([dout] * hb), kn, vb, qw, ow, bias, *cin)


def _halo_specs(R, width, T, col):
    hb = R // 8
    prev = pl.BlockSpec((8, width), lambda j, i: (jnp.maximum(i * hb - 1, 0), col(j, i)))
    nxt = pl.BlockSpec((8, width), lambda j, i: (jnp.minimum((i + 1) * hb, T // 8 - 1), col(j, i)))
    return prev, nxt


def _edge_flags(i, ncb, nblk):
    has_prev = jnp.where((i == 0) | (i == ncb), 0.0, 1.0).astype(F32)
    has_next = jnp.where((i == ncb - 1) | (i == nblk - 1), 0.0, 1.0).astype(F32)
    return has_prev, has_next


def _shift_up(a, prev_row):
    r0 = lax.broadcasted_iota(jnp.int32, a.shape, 0) == 0
    return jnp.where(r0, prev_row, pltpu.roll(a, 1, 0))


def _shift_dn(a, next_row):
    n = a.shape[0]
    rl = lax.broadcasted_iota(jnp.int32, a.shape, 0) == n - 1
    return jnp.where(rl, next_row, pltpu.roll(a, n - 1, 0))


def _conv3(a, prev_row, next_row, w_ref):
    return w_ref[0:1, :] * _shift_up(a, prev_row) + w_ref[1:2, :] * a + w_ref[2:3, :] * _shift_dn(a, next_row)


def _cv_post(b, y, w):
    return _rms(b * y, w)


def short_conv(u, cw, ow, bcol, nc, cvw, R, bwd_dout=None, ocol=0, name=""):
    T = u.shape[0]
    nh, nblk, ncb = cvw // LANE, T // R, nc // R
    bwd = bwd_dout is not None

    def body(b_ref, c_ref, v_ref, cp_ref, vp_ref, cn_ref, vn_ref, cw_ref, ow_ref, *rest):
        i = pl.program_id(1)
        has_prev, has_next = _edge_flags(i, ncb, nblk)
        p = c_ref[...] * v_ref[...]
        y = _conv3(p, cp_ref[7:8, :] * vp_ref[7:8, :] * has_prev, cn_ref[0:1, :] * vn_ref[0:1, :] * has_next, cw_ref)
        if not bwd:
            rest[0][...] = _cv_post(b_ref[...], y, ow_ref[...]).astype(BF16)
            return
        d_ref, db_ref, dy_ref, dow_ref = rest

        @pl.when(i == 0)
        def _():
            dow_ref[...] = jnp.zeros_like(dow_ref)

        _, vjp = jax.vjp(_cv_post, b_ref[...], y, ow_ref[...])
        db, dy, dow = vjp(d_ref[...])
        db_ref[...] = db
        dy_ref[...] = dy
        dow_ref[...] += dow

    def main(k):
        return pl.BlockSpec((R, LANE), lambda h, i: (i, bcol + k * nh + h))

    cprev, cnext = _halo_specs(R, LANE, T, lambda h, i: bcol + nh + h)
    vprev, vnext = _halo_specs(R, LANE, T, lambda h, i: bcol + 2 * nh + h)
    in_specs = [main(0), main(1), main(2), cprev, vprev, cnext, vnext,
                pl.BlockSpec((3, LANE), lambda h, i: (0, h)), pl.BlockSpec((1, LANE), lambda h, i: (0, h))]
    ins = [u] * 7 + [cw, ow]
    blk = pl.BlockSpec((R, LANE), lambda h, i: (i, h))
    if not bwd:
        out_shape, out_specs = jax.ShapeDtypeStruct((T, cvw), BF16), blk
    else:
        in_specs.append(pl.BlockSpec((R, LANE), lambda h, i: (i, ocol + h)))
        ins.append(bwd_dout)
        full = jax.ShapeDtypeStruct((T, cvw), F32)
        out_shape = (full, full, jax.ShapeDtypeStruct((nh, 1, LANE), F32))
        out_specs = (blk, blk, pl.BlockSpec((None, 1, LANE), lambda h, i: (h, 0, 0)))
    return pl.pallas_call(body, name=name, grid=(nh, nblk), out_shape=out_shape, in_specs=in_specs,
                          out_specs=out_specs, compiler_params=_cp("parallel", "arbitrary"))(*ins)


def conv3_bwd(dy, src, cw, nc, R, W, prod_cols=None, col0=0, out_dtype=F32, name=""):
    T, C = dy.shape
    nblk, ncb = T // R, nc // R
    prod = prod_cols is not None

    def body(*refs):
        if prod:
            (d_ref, dp_ref, dn_ref, c_ref, v_ref, cp_ref, vp_ref, cn_ref, vn_ref, w_ref,
             dc_ref, dv_ref, dw_ref) = refs
        else:
            d_ref, dp_ref, dn_ref, p_ref, pp_ref, pn_ref, w_ref, o_ref, dw_ref = refs
        i = pl.program_id(1)
        has_prev, has_next = _edge_flags(i, ncb, nblk)

        @pl.when(i == 0)
        def _():
            dw_ref[...] = jnp.zeros_like(dw_ref)

        d = d_ref[...]
        d_up = _shift_up(d, dp_ref[7:8, :] * has_prev)
        d_dn = _shift_dn(d, dn_ref[0:1, :] * has_next)
        dp = w_ref[0:1, :] * d_dn + w_ref[1:2, :] * d + w_ref[2:3, :] * d_up
        if prod:
            c, v = c_ref[...], v_ref[...]
            p = c * v
            p_prev, p_next = cp_ref[7:8, :] * vp_ref[7:8, :] * has_prev, cn_ref[0:1, :] * vn_ref[0:1, :] * has_next
            dc_ref[...] = dp * v
            dv_ref[...] = dp * c
        else:
            p = p_ref[...]
            p_prev, p_next = pp_ref[7:8, :] * has_prev, pn_ref[0:1, :] * has_next
            o_ref[...] = dp.astype(out_dtype)
        dw_ref[0:1, :] += jnp.sum(_shift_up(p, p_prev) * d, axis=0, keepdims=True)
        dw_ref[1:2, :] += jnp.sum(p * d, axis=0, keepdims=True)
        dw_ref[2:3, :] += jnp.sum(_shift_dn(p, p_next) * d, axis=0, keepdims=True)

    blk = pl.BlockSpec((R, W), lambda j, i: (i, j))
    dprev, dnext = _halo_specs(R, W, T, lambda j, i: j)
    wspec = pl.BlockSpec((3, W), lambda j, i: (0, j))
    dwspec = pl.BlockSpec((8, W), lambda j, i: (0, j))
    dwshape = jax.ShapeDtypeStruct((8, C), F32)
    if prod:
        ccol, vcol = prod_cols
        cprev, cnext = _halo_specs(R, W, T, lambda j, i: ccol + j)
        vprev, vnext = _halo_specs(R, W, T, lambda j, i: vcol + j)
        in_specs = [blk, dprev, dnext, pl.BlockSpec((R, W), lambda j, i: (i, ccol + j)),
                    pl.BlockSpec((R, W), lambda j, i: (i, vcol + j)), cprev, vprev, cnext, vnext, wspec]
        ins = [dy, dy, dy] + [src] * 6 + [cw]
        full = jax.ShapeDtypeStruct((T, C), F32)
        out_shape, out_specs = (full, full, dwshape), (blk, blk, dwspec)
    else:
        sprev, snext = _halo_specs(R, W, T, lambda j, i: col0 + j)
        in_specs = [blk, dprev, dnext, pl.BlockSpec((R, W), lambda j, i: (i, col0 + j)), sprev, snext,
                    pl.BlockSpec((3, W), lambda j, i: (0, col0 + j))]
        ins = [dy, dy, dy, src, src, src, cw]
        out_shape, out_specs = (jax.ShapeDtypeStruct((T, C), out_dtype), dwshape), (blk, dwspec)
    return pl.pallas_call(body, name=name, grid=(C // W, nblk), out_shape=out_shape, in_specs=in_specs,
                          out_specs=out_specs, compiler_params=_cp("parallel", "arbitrary"))(*ins)


def _pcall(body, *, name, grid, in_specs, out_specs, out_shape, args, scratch=(), sem=None, comm=None):
    cin, cspec, cout, csem = _comm_extras(comm)
    if comm is not None:
        sem = ("arbitrary",) * len(grid)
    return pl.pallas_call(
        _hosted(body, len(in_specs), len(out_specs), comm, *_grid_ends(grid)), name=name, grid=grid,
        out_shape=list(out_shape) + cout, in_specs=list(in_specs) + cspec, out_specs=list(out_specs) + cspec,
        scratch_shapes=list(scratch) + csem, compiler_params=_cp(*sem))(*args, *cin)


def ffn_mid(uf, cw, cb, nc, R, W, name, comm=None):
    T, C = uf.shape
    F = C // 2
    nblk, ncb, nj = T // R, nc // R, F // W

    def body(g_ref, v_ref, gp_ref, vp_ref, gn_ref, vn_ref, wg_ref, wv_ref, bg_ref, bv_ref, a_ref):
        i = pl.program_id(1)
        has_prev, has_next = _edge_flags(i, ncb, nblk)
        yg = _conv3(g_ref[...], gp_ref[7:8, :] * has_prev, gn_ref[0:1, :] * has_next, wg_ref) + bg_ref[...]
        yv = _conv3(v_ref[...], vp_ref[7:8, :] * has_prev, vn_ref[0:1, :] * has_next, wv_ref) + bv_ref[...]
        a_ref[...] = (yg * jax.nn.sigmoid(yg) * yv).astype(BF16)

    gblk = pl.BlockSpec((R, W), lambda j, i: (i, j))
    vblk = pl.BlockSpec((R, W), lambda j, i: (i, nj + j))
    gprev, gnext = _halo_specs(R, W, T, lambda j, i: j)
    vprev, vnext = _halo_specs(R, W, T, lambda j, i: nj + j)
    in_specs = [gblk, vblk, gprev, vprev, gnext, vnext,
                pl.BlockSpec((3, W), lambda j, i: (0, j)), pl.BlockSpec((3, W), lambda j, i: (0, nj + j)),
                pl.BlockSpec((1, W), lambda j, i: (0, j)), pl.BlockSpec((1, W), lambda j, i: (0, nj + j))]
    return _pcall(body, name=name, grid=(nj, nblk), in_specs=in_specs, out_specs=[gblk],
                  out_shape=[jax.ShapeDtypeStruct((T, F), BF16)], args=[uf] * 6 + [cw, cw, cb, cb],
                  sem=("parallel", "arbitrary"), comm=comm)


def ffn_mid_bwd(uf, cw, cb, da, nc, R, W, name, comm=None):
    T, C = uf.shape
    F = C // 2
    nblk, ncb, nj = T // R, nc // R, F // W

    def body(g_ref, v_ref, gp_ref, vp_ref, gn_ref, vn_ref, d_ref, dp_ref, dn_ref, wg_ref, wv_ref, bg_ref, bv_ref,
             dug_ref, duv_ref, dwg_ref, dwv_ref, dbg_ref, dbv_ref):
        i = pl.program_id(1)
        has_prev, has_next = _edge_flags(i, ncb, nblk)

        @pl.when(i == 0)
        def _():
            for r in (dwg_ref, dwv_ref, dbg_ref, dbv_ref):
                r[...] = jnp.zeros_like(r)

        def taps(w_ref):
            return w_ref[0:1, :], w_ref[1:2, :], w_ref[2:3, :]

        def dy_of(yg, yv, d):
            sg = jax.nn.sigmoid(yg)
            return d * yv * (sg * (1.0 + yg * (1.0 - sg))), d * (yg * sg)

        g, v, d = g_ref[...], v_ref[...], d_ref[...]
        (wg0, wg1, wg2), (wv0, wv1, wv2) = taps(wg_ref), taps(wv_ref)
        bg, bv = bg_ref[...], bv_ref[...]
        g_up, g_dn = _shift_up(g, gp_ref[7:8, :] * has_prev), _shift_dn(g, gn_ref[0:1, :] * has_next)
        v_up, v_dn = _shift_up(v, vp_ref[7:8, :] * has_prev), _shift_dn(v, vn_ref[0:1, :] * has_next)
        dyg, dyv = dy_of(wg0 * g_up + wg1 * g + wg2 * g_dn + bg, wv0 * v_up + wv1 * v + wv2 * v_dn + bv, d)
        dyg_p, dyv_p = dy_of(wg0 * gp_ref[6:7, :] + wg1 * gp_ref[7:8, :] + wg2 * g_ref[0:1, :] + bg,
                             wv0 * vp_ref[6:7, :] + wv1 * vp_ref[7:8, :] + wv2 * v_ref[0:1, :] + bv, dp_ref[7:8, :])
        dyg_n, dyv_n = dy_of(wg0 * g_ref[R - 1:R, :] + wg1 * gn_ref[0:1, :] + wg2 * gn_ref[1:2, :] + bg,
                             wv0 * v_ref[R - 1:R, :] + wv1 * vn_ref[0:1, :] + wv2 * vn_ref[1:2, :] + bv, dn_ref[0:1, :])
        dug_ref[...] = (wg0 * _shift_dn(dyg, dyg_n * has_next) + wg1 * dyg
                        + wg2 * _shift_up(dyg, dyg_p * has_prev)).astype(BF16)
        duv_ref[...] = (wv0 * _shift_dn(dyv, dyv_n * has_next) + wv1 * dyv
                        + wv2 * _shift_up(dyv, dyv_p * has_prev)).astype(BF16)
        for ref, ups, mid, dns, dy in ((dwg_ref, g_up, g, g_dn, dyg), (dwv_ref, v_up, v, v_dn, dyv)):
            ref[0:1, :] += jnp.sum(ups * dy, axis=0, keepdims=True)
            ref[1:2, :] += jnp.sum(mid * dy, axis=0, keepdims=True)
            ref[2:3, :] += jnp.sum(dns * dy, axis=0, keepdims=True)
        dbg_ref[...] += jnp.sum(dyg, axis=0, keepdims=True)
        dbv_ref[...] += jnp.sum(dyv, axis=0, keepdims=True)

    gblk = pl.BlockSpec((R, W), lambda j, i: (i, j))
    vblk = pl.BlockSpec((R, W), lambda j, i: (i, nj + j))
    gprev, gnext = _halo_specs(R, W, T, lambda j, i: j)
    vprev, vnext = _halo_specs(R, W, T, lambda j, i: nj + j)
    half = jax.ShapeDtypeStruct((T, F), BF16)
    taps8, bias1 = jax.ShapeDtypeStruct((8, F), F32), jax.ShapeDtypeStruct((1, F), F32)
    return _pcall(
        body, name=name, grid=(nj, nblk), out_shape=(half, half, taps8, taps8, bias1, bias1),
        in_specs=[gblk, vblk, gprev, vprev, gnext, vnext, gblk, gprev, gnext,
                  pl.BlockSpec((3, W), lambda j, i: (0, j)), pl.BlockSpec((3, W), lambda j, i: (0, nj + j)),
                  pl.BlockSpec((1, W), lambda j, i: (0, j)), pl.BlockSpec((1, W), lambda j, i: (0, nj + j))],
        out_specs=(gblk, gblk, pl.BlockSpec((8, W), lambda j, i: (0, j)), pl.BlockSpec((8, W), lambda j, i: (0, j)),
                   pl.BlockSpec((1, W), lambda j, i: (0, j)), pl.BlockSpec((1, W), lambda j, i: (0, j))),
        args=(uf, uf, uf, uf, uf, uf, da, da, da, cw, cw, cb, cb), sem=("parallel", "arbitrary"), comm=comm)


WEIGHTS = ("c_ctx", "w_ada", "b_ada", "ln1_w", "ln2_w", "w_in", "hg_lb_logits", "hg_norm_w", "na_q_norm_w",
           "na_k_norm_w", "na_rpb", "na_out_norm_w", "cv_w", "cv_out_norm_w", "w_out", "w_up", "ffn_conv_w",
           "ffn_conv_b", "w_down")
BIG = ("w_ada", "w_in", "w_out", "w_up", "w_down")
SHARDED_SMALL = ("hg_lb_logits", "cv_w", "ffn_conv_w")


def _flat_rows(parts, dtype):
    flat, layout, off = [], [], 0
    for p in parts:
        layout.append((off, p.shape))
        flat.append(p.reshape(-1).astype(dtype))
        off += p.size
    pad = (-off) % (8 * LANE)
    if pad:
        flat.append(jnp.zeros((pad,), dtype))
    return jnp.concatenate(flat).reshape(-1, LANE), layout


def _unflat(buf, layout):
    v = buf.reshape(-1)
    return [v[off:off + int(np.prod(shape))].reshape(shape) for off, shape in layout]


def _lb_all(logits):
    sm = jax.nn.softmax(logits.astype(F32), axis=1)
    return jnp.cumsum(sm, axis=1) - sm[:, :1]


def _seg(ctx_vec, lat_vec):
    return jnp.broadcast_to(jnp.stack([ctx_vec, lat_vec])[:, None, :], (2, RT, ctx_vec.shape[0]))


def _shared(vec):
    return jnp.broadcast_to(vec[None, None, :], (1, RT, vec.shape[0]))


def kernel(x, c, ctx, c_ctx, w_ada, b_ada, ln1_w, ln2_w, w_in, hg_lb_logits, hg_norm_w, na_q_norm_w, na_k_norm_w, na_rpb, na_out_norm_w, cv_w, cv_out_norm_w, w_out, w_up, ffn_conv_w, ffn_conv_b, w_down, loss_target, m_c_ctx, m_w_ada, m_b_ada, m_ln1_w, m_ln2_w, m_w_in, m_hg_lb_logits, m_hg_norm_w, m_na_q_norm_w, m_na_k_norm_w, m_na_rpb, m_na_out_norm_w, m_cv_w, m_cv_out_norm_w, m_w_out, m_w_up, m_ffn_conv_w, m_ffn_conv_b, m_w_down, v_c_ctx, v_w_ada, v_b_ada, v_ln1_w, v_ln2_w, v_w_in, v_hg_lb_logits, v_hg_norm_w, v_na_q_norm_w, v_na_k_norm_w, v_na_rpb, v_na_out_norm_w, v_cv_w, v_cv_out_norm_w, v_w_out, v_w_up, v_ffn_conv_w, v_ffn_conv_b, v_w_down):
    W = dict(c_ctx=c_ctx, w_ada=w_ada, b_ada=b_ada, ln1_w=ln1_w, ln2_w=ln2_w, w_in=w_in, hg_lb_logits=hg_lb_logits,
             hg_norm_w=hg_norm_w, na_q_norm_w=na_q_norm_w, na_k_norm_w=na_k_norm_w, na_rpb=na_rpb,
             na_out_norm_w=na_out_norm_w, cv_w=cv_w, cv_out_norm_w=cv_out_norm_w, w_out=w_out, w_up=w_up,
             ffn_conv_w=ffn_conv_w, ffn_conv_b=ffn_conv_b, w_down=w_down)
    Mo = dict(c_ctx=m_c_ctx, w_ada=m_w_ada, b_ada=m_b_ada, ln1_w=m_ln1_w, ln2_w=m_ln2_w, w_in=m_w_in,
              hg_lb_logits=m_hg_lb_logits, hg_norm_w=m_hg_norm_w, na_q_norm_w=m_na_q_norm_w,
              na_k_norm_w=m_na_k_norm_w, na_rpb=m_na_rpb, na_out_norm_w=m_na_out_norm_w, cv_w=m_cv_w,
              cv_out_norm_w=m_cv_out_norm_w, w_out=m_w_out, w_up=m_w_up, ffn_conv_w=m_ffn_conv_w,
              ffn_conv_b=m_ffn_conv_b, w_down=m_w_down)
    Vo = dict(c_ctx=v_c_ctx, w_ada=v_w_ada, b_ada=v_b_ada, ln1_w=v_ln1_w, ln2_w=v_ln2_w, w_in=v_w_in,
              hg_lb_logits=v_hg_lb_logits, hg_norm_w=v_hg_norm_w, na_q_norm_w=v_na_q_norm_w,
              na_k_norm_w=v_na_k_norm_w, na_rpb=v_na_rpb, na_out_norm_w=v_na_out_norm_w, cv_w=v_cv_w,
              cv_out_norm_w=v_cv_out_norm_w, w_out=v_w_out, w_up=v_w_up, ffn_conv_w=v_ffn_conv_w,
              ffn_conv_b=v_ffn_conv_b, w_down=v_w_down)

    xi, yi, ci = _me()
    chip = 2 * xi + yi
    dev = 2 * chip + ci
    L, D = x.shape[1], x.shape[2]
    NC = ctx.shape[1]
    T = NC + L
    depth = w_in.shape[0]
    HGW, NAW, CVW = 4 * hg_lb_logits.shape[-1], na_out_norm_w.shape[-1], cv_out_norm_w.shape[-1]
    MIX = HGW + NAW + CVW
    INW, FF2 = 4 * w_in.shape[-1], 4 * w_up.shape[-1]
    F = FF2 // 2
    ADA = 4 * w_ada.shape[-1]
    assert NAW == 2 * HGW and INW == 5 * HGW + 3 * NAW + 3 * CVW and ADA == 6 * D and NC % 128 == 0
    assert L % GRID_W == 0 and T % CHUNK == 0 and depth == 2
    R = math.gcd(NC, 256)
    FW = 512 if F % 512 == 0 else LANE
    rows = L // GRID_W
    nh_hg, nh_na, nh_cv = HGW // LANE, NAW // LANE, CVW // LANE
    kcol = 3 * nh_hg
    vcol = kcol + nh_na
    gcol = vcol + nh_na + nh_hg
    qcol = gcol + nh_hg
    bcol = qcol + nh_na
    mix_na, mix_cv = nh_hg, nh_hg + nh_na

    small1, lay1 = _flat_rows([c[0], hg_lb_logits, cv_w, ffn_conv_w], F32)
    g1 = allgather8([small1], "gather_cond")[0]
    per_dev = [_unflat(g1[d], lay1) for d in range(8)]
    c_all = jnp.stack([p[0] for p in per_dev])
    lb_logits = jnp.concatenate([per_dev[2 * s][1] for s in range(4)], axis=-1)
    cvw_full = jnp.concatenate([per_dev[2 * s][2] for s in range(4)], axis=-1)
    fcw_full = jnp.concatenate([per_dev[2 * s][3] for s in range(4)], axis=-1)
    lb_all, lb_pull = jax.vjp(_lb_all, lb_logits)

    a16 = jnp.concatenate([c_all, c_ctx[None], jnp.zeros((7, D), F32)])
    s16 = _silu(a16)
    wcols = ADA // 4
    b_mine = lax.dynamic_slice_in_dim(b_ada, chip * wcols, wcols, axis=1)
    p_ada = jnp.stack([mm_nn(s16, w_ada[l][None], F32, f"ada_fwd_{l}") + b_mine[l][None] for l in range(depth)])
    g2 = allgather8([p_ada.reshape(depth * 16, wcols)], "gather_ada")[0].reshape(8, depth, 16, wcols)
    ada_rows = jnp.concatenate([g2[2 * s] for s in range(4)], axis=-1)
    ada = lax.dynamic_index_in_dim(ada_rows, dev, axis=1, keepdims=False)
    ada_c = ada_rows[:, 8]

    def half_rows(a):
        h = a.shape[0] // 2
        return lax.dynamic_slice_in_dim(a, ci * h, h, axis=0)

    proj = ("w_in", "w_out", "w_up", "w_down")
    wparts = [{n: half_rows(W[n][l]).astype(BF16) for n in proj} for l in range(depth)]

    def stacked(n, g):
        g = g.reshape(4, -1, g.shape[-1])
        return g.reshape(1, -1, g.shape[-1]) if n in ("w_out", "w_down") else g

    Wg = [{"w_in": stacked("w_in", allgather8([wparts[0]["w_in"]], "gather_w_in_0", hbm=True)[0])}, {}]
    fetch_plan = {"proj_in_0": [(0, "w_out"), (0, "w_down")], "na_fwd_0": [(0, "w_up")],
                  "ffn_up_0": [(1, "w_in"), (1, "w_out")], "ffn_mid_0": [(1, "w_up")], "ffn_down_0": [(1, "w_down")]}

    def fetching(name, call):
        items = fetch_plan.get(name, [])
        res = call(allgather8_comm([wparts[l][n] for l, n in items]) if items else None)
        res = list(res) if isinstance(res, (list, tuple)) else [res]
        for (l, n), g in zip(items, res[len(res) - len(items):]):
            Wg[l][n] = stacked(n, g)
        own = res[:len(res) - len(items)]
        return own[0] if len(own) == 1 else own

    xcat = jnp.concatenate([ctx[0], x[0]], axis=0)
    mods = []
    for l in range(depth):
        lat, con = jnp.split(ada[l], 6), jnp.split(ada_c[l], 6)
        mods.append(dict(sh1=_seg(con[0], lat[0]), sc1=_seg(con[1], lat[1]), g1=_seg(con[2], lat[2]),
                         sh2=_seg(con[3], lat[3]), sc2=_seg(con[4], lat[4]), g2=_seg(con[5], lat[5]),
                         ln1=_shared(ln1_w[l]), ln2=_shared(ln2_w[l])))
    bias_pull, saved = [], []
    x0 = xcat
    _, h = gate_norm(x0, None, None, mods[0]["ln1"], mods[0]["sh1"], mods[0]["sc1"], NC, R, "norm_in")
    for l in range(depth):
        md, wl = mods[l], Wg[l]
        u = fetching(f"proj_in_{l}", lambda cm: mm_nn(h, wl["w_in"], F32, f"proj_in_{l}", comm=cm))
        lbf, lbb = lb_all[0, l][None], lb_all[1, l][None]
        o_fw, st_fw = hgrn_fwd(u, lbf, False, 0, NC, HGW, f"hgrn_fw_{l}")
        o_bw, st_bw = hgrn_fwd(u, lbb, True, 1, NC, HGW, f"hgrn_bw_{l}")
        hgn = hg_norm_w[l][None]
        hg = hg_read(o_fw, o_bw, u, hgn, gcol, R, f"hg_read_{l}")
        bias, pull = jax.vjp(lambda r: na_bias_tables(r, rows), na_rpb[l])
        bias_pull.append(pull)
        qn, kn, on = na_q_norm_w[l][None], na_k_norm_w[l][None], na_out_norm_w[l][None]
        keys_n, vals_b = kv_prep(u, kn, kcol, vcol, NAW, R, f"kv_prep_{l}")
        na = fetching(f"na_fwd_{l}", lambda cm: na_fwd(u, keys_n, vals_b, qn, on, bias, qcol, NC, f"na_fwd_{l}", comm=cm))
        cvw_l, cvo = cvw_full[l], cv_out_norm_w[l][None]
        cv = short_conv(u, cvw_l, cvo, bcol, NC, CVW, R, name=f"short_conv_{l}")
        mix = jnp.concatenate([hg, na, cv], axis=1)
        m1 = mm_nn(mix, wl["w_out"], F32, f"proj_out_{l}")
        x1, h2 = gate_norm(x0, m1, md["g1"], md["ln2"], md["sh2"], md["sc2"], NC, R, f"gate_norm_mid_{l}")
        uf = fetching(f"ffn_up_{l}", lambda cm: mm_nn(h2, wl["w_up"], F32, f"ffn_up_{l}", comm=cm))
        fcw_l, fcb_l = fcw_full[l], ffn_conv_b[l][None]
        a = fetching(f"ffn_mid_{l}", lambda cm: ffn_mid(uf, fcw_l, fcb_l, NC, R, FW, f"ffn_mid_{l}", comm=cm))
        m2 = fetching(f"ffn_down_{l}", lambda cm: mm_nn(a, wl["w_down"], F32, f"ffn_down_{l}", comm=cm))
        saved.append(dict(x0=x0, h=h, u=u, o_fw=o_fw, o_bw=o_bw, st_fw=st_fw, st_bw=st_bw, bias=bias, mix=mix,
                          m1=m1, x1=x1, h2=h2, uf=uf, a=a, m2=m2, lbf=lbf, lbb=lbb, keys_n=keys_n, vals_b=vals_b))
        if l + 1 < depth:
            nx = mods[l + 1]
            x0, h = gate_norm(x1, m2, md["g2"], nx["ln1"], nx["sh1"], nx["sc1"], NC, R, f"gate_norm_end_{l}")
    sv, md = saved[-1], mods[-1]
    loss_terms, d_x1, d_m2, d_g2 = gate_loss(sv["x1"], sv["m2"], md["g2"], loss_target[0], NC, R, "gate_loss")
    loss = lax.psum(jnp.sum(loss_terms), ("x", "y", "c"))

    big_grads = [dict() for _ in range(depth)]
    core = ci.astype(jnp.int32).reshape(1)
    pairs, quads = {}, {}
    reduce_plan = {"ffn_down_bwd_0": [(1, "w_down")], "ffn_mid_bwd_0": [(1, "w_up")],
                   "ffn_up_bwd_0": [(1, "w_in"), (1, "w_out")], "na_bwd_0": [(0, "w_down"), (0, "w_up"), (0, "w_out")]}

    def stage(l, names):
        parts = [big_grads[l][n] for n in names]
        got = swap_halves(parts, f"reduce_sibling_{l}_{names[0]}")
        for n, p, g in zip(names, parts, got):
            pairs[(l, n)] = pair_sum(p, g, core, f"reduce_pair_sum_{n}_{l}")

    def reducing(name, call):
        items = reduce_plan.get(name, [])
        res = call(chip_alltoall_comm([pairs[k] for k in items]) if items else None)
        res = list(res) if isinstance(res, (list, tuple)) else [res]
        for k, q in zip(items, res[len(res) - len(items):]):
            quads[k] = q
        own = res[:len(res) - len(items)]
        return own[0] if len(own) == 1 else own

    small = [dict() for _ in range(depth)]
    d_ada = [None] * depth
    d_lb = [None] * depth
    for l in reversed(range(depth)):
        sv, md, wl = saved[l], mods[l], Wg[l]
        u, uf = sv["u"], sv["uf"]
        big_grads[l]["w_down"] = mm_tn(sv["a"], d_m2, 1, BF16, f"grad_w_down_{l}").reshape(4, F // 4, D)
        d_a = reducing(f"ffn_down_bwd_{l}", lambda cm: mm_nt(d_m2, wl["w_down"], F32, f"ffn_down_bwd_{l}", comm=cm))
        fcw_l, fcb_l = fcw_full[l], ffn_conv_b[l][None]
        dug, duv, dwg, dwv, dbg, dbv = reducing(
            f"ffn_mid_bwd_{l}", lambda cm: ffn_mid_bwd(uf, fcw_l, fcb_l, d_a, NC, R, FW, f"ffn_mid_bwd_{l}", comm=cm))
        d_uf = jnp.concatenate([dug, duv], axis=1)
        small[l]["ffn_conv_w"] = jnp.concatenate([dwg[:3], dwv[:3]], axis=1)
        small[l]["ffn_conv_b"] = jnp.concatenate([dbg[0], dbv[0]])
        big_grads[l]["w_up"] = mm_tn(sv["h2"], d_uf, 4, BF16, f"grad_w_up_{l}")
        d_h2 = reducing(f"ffn_up_bwd_{l}", lambda cm: mm_nt(d_uf, wl["w_up"], F32, f"ffn_up_bwd_{l}", comm=cm))
        d_x0, d_m1, dg1, dln2, dsh2, dsc2 = gate_norm_bwd(sv["x0"], sv["m1"], md["g1"], md["ln2"], md["sh2"], md["sc2"],
                                                          d_x1, d_h2, NC, R, f"gate_norm_mid_bwd_{l}")
        big_grads[l]["w_out"] = mm_tn(sv["mix"], d_m1, 1, BF16, f"grad_w_out_{l}").reshape(4, MIX // 4, D)
        if l == 0:
            stage(0, ("w_down", "w_up", "w_out"))
        d_mix = mm_nt(d_m1, wl["w_out"], F32, f"proj_out_bwd_{l}")
        hgn = hg_norm_w[l][None]
        d_o, d_hgg, d_hgn = hg_read_bwd(sv["o_fw"], sv["o_bw"], u, hgn, d_mix, gcol, 0, R, f"hg_read_bwd_{l}")
        dzf, dvf, dqf, dlbf = hgrn_bwd(u, sv["lbf"], sv["st_fw"], d_o, False, 0, NC, HGW, f"hgrn_fw_bwd_{l}")
        dzb, dvb, dqb, dlbb = hgrn_bwd(u, sv["lbb"], sv["st_bw"], d_o, True, 1, NC, HGW, f"hgrn_bw_bwd_{l}")
        d_lb[l] = (dlbf[0], dlbb[0])
        qn, kn, on = na_q_norm_w[l][None], na_k_norm_w[l][None], na_out_norm_w[l][None]
        d_nq, d_keys_n, d_nv, d_bias, d_qn, d_on = reducing(
            f"na_bwd_{l}", lambda cm: na_bwd(u, sv["keys_n"], sv["vals_b"], qn, on, sv["bias"], d_mix, qcol, mix_na, NC,
                                            f"na_bwd_{l}", comm=cm))
        d_nk, d_kn = kv_prep_bwd(u, kn, d_keys_n, kcol, NAW, R, f"kv_prep_bwd_{l}")
        cvw_l, cvo = cvw_full[l], cv_out_norm_w[l][None]
        d_cb, d_cy, d_cvo = short_conv(u, cvw_l, cvo, bcol, NC, CVW, R, bwd_dout=d_mix, ocol=mix_cv,
                                       name=f"short_conv_bwd_{l}")
        d_cc, d_cvv, d_cvw = conv3_bwd(d_cy, u, cvw_l, NC, R, LANE, prod_cols=(bcol + nh_cv, bcol + 2 * nh_cv),
                                       name=f"short_conv_taps_bwd_{l}")
        d_u = jnp.concatenate([dzf, dzb, dvf + dvb, d_nk, d_nv, dqf + dqb, d_hgg, d_nq, d_cb, d_cc, d_cvv],
                              axis=1).astype(BF16)
        big_grads[l]["w_in"] = mm_tn(sv["h"], d_u, 4, BF16, f"grad_w_in_{l}")
        d_h = mm_nt(d_u, wl["w_in"], F32, f"proj_in_bwd_{l}")
        small[l].update(hg_norm_w=d_hgn.sum(0)[0], na_q_norm_w=d_qn.sum(0)[0], na_k_norm_w=d_kn.sum(0)[0],
                        na_out_norm_w=d_on.reshape(-1), na_rpb=bias_pull[l](d_bias)[0], cv_w=d_cvw[:3],
                        cv_out_norm_w=d_cvo.reshape(-1), ln2_w=dln2.sum((0, 1)))
        if l > 0:
            pv, pm = saved[l - 1], mods[l - 1]
            d_x1, d_m2, dg2_prev, dln1, dsh1, dsc1 = gate_norm_bwd(pv["x1"], pv["m2"], pm["g2"], md["ln1"], md["sh1"],
                                                                   md["sc1"], d_x0, d_h, NC, R, f"gate_norm_end_bwd_{l - 1}")
        else:
            d_xin, _, _, dln1, dsh1, dsc1 = gate_norm_bwd(sv["x0"], None, None, md["ln1"], md["sh1"], md["sc1"], d_x0, d_h,
                                                          NC, R, "norm_in_bwd")
        small[l]["ln1_w"] = dln1.sum((0, 1))
        this_g2 = d_g2
        vecs = [v.sum(1) for v in (dsh1, dsc1, dg1, dsh2, dsc2, this_g2)]
        d_ada[l] = jnp.stack([jnp.concatenate([v[s] for v in vecs]) for s in (0, 1)])
        if l > 0:
            d_g2 = dg2_prev
        stage(l, proj if l > 0 else ("w_in",))
    grad_x = d_xin[NC:][None]
    d_logits = lb_pull(jnp.stack([jnp.stack([d_lb[l][k] for l in range(depth)]) for k in (0, 1)]))[0]

    rep_names = ("ln1_w", "ln2_w", "hg_norm_w", "na_q_norm_w", "na_k_norm_w", "na_rpb", "na_out_norm_w",
                 "cv_out_norm_w", "ffn_conv_b", "cv_w", "ffn_conv_w")
    parts3 = [jnp.stack([small[l][n] for l in range(depth)]) for n in rep_names]
    parts3 += [d_logits, jnp.stack([d_ada[l][0] for l in range(depth)]), jnp.stack([d_ada[l][1] for l in range(depth)])]
    buf3, lay3 = _flat_rows(parts3, F32)
    g3 = allgather8([buf3], "gather_small_grads")[0]
    tot3 = _unflat(sum_leading(g3, F32, "sum_small_grads"), lay3)
    gsm = dict(zip(rep_names, tot3[:len(rep_names)]))
    gsm["hg_lb_logits"] = tot3[len(rep_names)]
    dctx_tot, dlat_tot = tot3[-2], tot3[-1]
    dlat_each = jnp.stack([_unflat(g3[d], lay3)[-1] for d in range(8)], axis=1)
    grads = {n: gsm[n].reshape(W[n].shape) for n in rep_names if n not in SHARDED_SMALL}
    for n in SHARDED_SMALL:
        wl_ = W[n].shape[-1]
        grads[n] = lax.dynamic_slice_in_dim(gsm[n], chip * wl_, wl_, axis=gsm[n].ndim - 1)
    grads["b_ada"] = dctx_tot + dlat_tot

    ds16 = jnp.zeros((16, D), F32)
    gw_ada = []
    for l in range(depth):
        dm = jnp.concatenate([dlat_each[l], dctx_tot[l][None], jnp.zeros((7, ADA), F32)])
        dm = lax.dynamic_slice_in_dim(dm, chip * wcols, wcols, axis=1)
        gw_ada.append(mm_tn(s16, dm, 1, F32, f"grad_w_ada_{l}")[0])
        ds16 = ds16 + mm_nt(dm, w_ada[l][None], F32, f"ada_bwd_{l}")
    g4 = allgather8([ds16[8:16]], "gather_cond_grad")[0]
    d_scc = g4[0, 0] + g4[2, 0] + g4[4, 0] + g4[6, 0]
    sg = jax.nn.sigmoid(c_ctx)
    grads["c_ctx"] = d_scc * (sg * (1.0 + c_ctx * (1.0 - sg)))

    keys = [(l, n) for l in range(depth) for n in proj]
    quads[(0, "w_in")] = chip_alltoall([pairs[(0, "w_in")]], "reduce_chips_w_in_0")[0]
    mine = [sum_leading(quads[(l, n)], F32, f"reduce_chip_sum_{n}_{l}") for l, n in keys]
    other = share_halves(mine, "reduce_share")
    mine_by, other_by = {n: [None] * depth for n in proj}, {n: [None] * depth for n in proj}
    for (l, n), a, b in zip(keys, mine, other):
        mine_by[n][l], other_by[n][l] = a, b

    delta, new_m, new_v = {}, {}, {}
    for n in BIG:
        shp = W[n].shape
        two = lambda a: a.reshape(-1, shp[-1])
        if n == "w_ada":
            g_, d_, m_, v_ = adamw(two(W[n]), gw_ada, two(Mo[n]), two(Vo[n]), f"adamw_{n}")
        else:
            g_, d_, m_, v_ = adamw_halves(two(W[n]), mine_by[n], other_by[n], two(Mo[n]), two(Vo[n]), core, f"adamw_{n}")
        grads[n], delta[n], new_m[n], new_v[n] = g_.reshape(shp), d_.reshape(shp), m_.reshape(shp), v_.reshape(shp)
    smalls = [n for n in WEIGHTS if n not in BIG]
    pw, lay_s = _flat_rows([W[n] for n in smalls], F32)
    pg, _ = _flat_rows([grads[n] for n in smalls], F32)
    pm, _ = _flat_rows([Mo[n] for n in smalls], F32)
    pvv, _ = _flat_rows([Vo[n] for n in smalls], F32)
    _, d_, m_, v_ = adamw(pw, [pg], pm, pvv, "adamw_small")
    for n, dd, mm_, vv in zip(smalls, _unflat(d_, lay_s), _unflat(m_, lay_s), _unflat(v_, lay_s)):
        delta[n], new_m[n], new_v[n] = dd, mm_, vv

    return (loss, grad_x, *[grads[n] for n in WEIGHTS], *[delta[n] for n in WEIGHTS],
            *[new_m[n] for n in WEIGHTS], *[new_v[n] for n in WEIGHTS])
```

```python
import functools
import math

import numpy as np
import jax
import jax.numpy as jnp
from jax import lax
from jax.experimental import pallas as pl
from jax.experimental.pallas import tpu as pltpu

F32 = jnp.float32
BF16 = jnp.bfloat16
MESH = pl.DeviceIdType.MESH
ANY = pl.BlockSpec(memory_space=pl.ANY)
VMEM_SPEC = pl.BlockSpec(memory_space=pltpu.VMEM)

LANE = 128
CHUNK = 64
SUB = 16
GRID_W = 64
WIN_R = 8
WIN_C = 16
EPS = 1e-6
F_FLOOR = 1e-30
NEG_INF = -1e30
EXP_CLAMP = 80.0
ATTN_SCALE = LANE ** -0.5
VMEM_LIMIT = 56 * 1024 * 1024
ADAM_LR, ADAM_B1, ADAM_B2, ADAM_EPS, ADAM_WD, ADAM_STEP = 0.001, 0.9, 0.999, 1e-08, 0.01, 10


def _cp(*sem):
    return pltpu.CompilerParams(dimension_semantics=sem or None, vmem_limit_bytes=VMEM_LIMIT)


def _me():
    return lax.axis_index("x"), lax.axis_index("y"), lax.axis_index("c")


def allgather8(blocks, name, hbm=False):
    na = len(blocks)
    comm = allgather8_comm(blocks)

    def body(*refs):
        comm["start"](refs[:na], refs[na:2 * na], refs[2 * na:])
        comm["finish"](refs[:na], refs[na:2 * na], refs[2 * na:])

    spec = ANY if hbm else VMEM_SPEC
    return pl.pallas_call(
        body, name=name, out_shape=comm["outs"], in_specs=[spec] * na, out_specs=[spec] * na,
        scratch_shapes=comm["scratch"], compiler_params=pltpu.CompilerParams(vmem_limit_bytes=VMEM_LIMIT),
    )(*blocks)


def allgather8_comm(blocks):
    na = len(blocks)

    def parts(x_refs, out_refs, sems):
        send_sems, recv_sems, local_sems = sems
        x, y, c = _me()
        me, sibling = (x, y, c), (x, y, 1 - c)
        chips = [(1 - x, y), (x, 1 - y), (1 - x, 1 - y)]

        def rows(a, px, py, pc):
            return out_refs[a].at[4 * px + 2 * py + pc]

        def copy(a, k, blk, to, src=None):
            return pltpu.make_async_remote_copy(
                src_ref=rows(a, *blk) if src is None else src, dst_ref=rows(a, *blk),
                send_sem=send_sems.at[a, k], recv_sem=recv_sems.at[a, k], device_id=to, device_id_type=MESH)

        mine = [pltpu.make_async_copy(x_refs[a], rows(a, *me), local_sems.at[a]) for a in range(na)]
        first = []
        for a in range(na):
            first.append(copy(a, 0, me, sibling, src=x_refs[a]))
            first += [copy(a, 1 + j, me, (*chip, c), src=x_refs[a]) for j, chip in enumerate(chips)]
        return c, me, sibling, chips, copy, mine, first

    def start(x_refs, out_refs, sems):
        _, _, _, _, _, mine, first = parts(x_refs, out_refs, sems)
        for cp in mine + first:
            cp.start()

    def finish(x_refs, out_refs, sems):
        c, me, sibling, chips, copy, mine, first = parts(x_refs, out_refs, sems)
        passed = []
        for j, chip in enumerate(chips):
            for a in range(na):
                copy(a, 1 + j, (*chip, c), me).wait_recv()
                passed.append(copy(a, 4 + j, (*chip, c), sibling))
                passed[-1].start()
        for a in range(na):
            copy(a, 0, sibling, me).wait_recv()
            for j, chip in enumerate(chips):
                copy(a, 4 + j, (*chip, 1 - c), me).wait_recv()
        for cp in first + passed:
            cp.wait_send()
        for cp in mine:
            cp.wait()

    return dict(ins=list(blocks), outs=[jax.ShapeDtypeStruct((8,) + b.shape, b.dtype) for b in blocks],
                scratch=[pltpu.SemaphoreType.DMA((na, 7)), pltpu.SemaphoreType.DMA((na, 7)),
                         pltpu.SemaphoreType.DMA((na,))], start=start, finish=finish)


def swap_halves(gs, name):
    na = len(gs)
    hrs = [g.shape[1] // 2 for g in gs]

    def body(*refs):
        g_refs, o_refs = refs[:na], refs[na:2 * na]
        send_sems, recv_sems = refs[2 * na:]
        x, y, c = _me()
        cps = []
        for a in range(na):
            for s in range(4):
                src = g_refs[a].at[s, pl.ds(pl.multiple_of((1 - c) * hrs[a], 16), hrs[a]), :]
                cps.append(pltpu.make_async_remote_copy(
                    src_ref=src, dst_ref=o_refs[a].at[s], send_sem=send_sems.at[a, s], recv_sem=recv_sems.at[a, s],
                    device_id=(x, y, 1 - c), device_id_type=MESH))
        for cp in cps:
            cp.start()
        for cp in cps:
            cp.wait()

    return pl.pallas_call(
        body, name=name, out_shape=[jax.ShapeDtypeStruct((4, hrs[a], gs[a].shape[2]), gs[a].dtype) for a in range(na)],
        in_specs=[ANY] * na, out_specs=[ANY] * na,
        scratch_shapes=[pltpu.SemaphoreType.DMA((na, 4)), pltpu.SemaphoreType.DMA((na, 4))],
    )(*gs)


def chip_alltoall(gs, name):
    na = len(gs)
    comm = chip_alltoall_comm(gs)

    def body(*refs):
        comm["start"](refs[:na], refs[na:2 * na], refs[2 * na:])
        comm["finish"](refs[:na], refs[na:2 * na], refs[2 * na:])

    return pl.pallas_call(body, name=name, out_shape=comm["outs"], in_specs=[ANY] * na, out_specs=[ANY] * na,
                          scratch_shapes=comm["scratch"])(*gs)


def chip_alltoall_comm(gs):
    na = len(gs)

    def copies(g_refs, o_refs, sems):
        send_sems, recv_sems, local_sems = sems
        x, y, c = _me()
        mine = 2 * x + y
        cps = []
        for a in range(na):
            cps.append(pltpu.make_async_copy(g_refs[a].at[mine], o_refs[a].at[mine], local_sems.at[a]))
            for k, (px, py) in enumerate([(1 - x, y), (x, 1 - y), (1 - x, 1 - y)]):
                cps.append(pltpu.make_async_remote_copy(
                    src_ref=g_refs[a].at[2 * px + py], dst_ref=o_refs[a].at[mine], send_sem=send_sems.at[a, k],
                    recv_sem=recv_sems.at[a, k], device_id=(px, py, c), device_id_type=MESH))
        return cps

    def start(g_refs, o_refs, sems):
        for cp in copies(g_refs, o_refs, sems):
            cp.start()

    def finish(g_refs, o_refs, sems):
        for cp in copies(g_refs, o_refs, sems):
            cp.wait()

    return dict(ins=list(gs), outs=[jax.ShapeDtypeStruct(g.shape, g.dtype) for g in gs],
                scratch=[pltpu.SemaphoreType.DMA((na, 3)), pltpu.SemaphoreType.DMA((na, 3)),
                         pltpu.SemaphoreType.DMA((na,))], start=start, finish=finish)


def share_halves(vs, name):
    na = len(vs)

    def body(*refs):
        v_refs, o_refs = refs[:na], refs[na:2 * na]
        send_sems, recv_sems = refs[2 * na:]
        x, y, c = _me()
        cps = [pltpu.make_async_remote_copy(
            src_ref=v_refs[a], dst_ref=o_refs[a], send_sem=send_sems.at[a], recv_sem=recv_sems.at[a],
            device_id=(x, y, 1 - c), device_id_type=MESH) for a in range(na)]
        for cp in cps:
            cp.start()
        for cp in cps:
            cp.wait()

    return pl.pallas_call(
        body, name=name, out_shape=[jax.ShapeDtypeStruct(v.shape, v.dtype) for v in vs],
        in_specs=[ANY] * na, out_specs=[ANY] * na,
        scratch_shapes=[pltpu.SemaphoreType.DMA((na,)), pltpu.SemaphoreType.DMA((na,))],
    )(*vs)


def _row_block(rows, cap):
    rb = math.gcd(rows, cap)
    return rb if rb % 8 == 0 else rows


def sum_leading(x, out_dtype, name):
    n, r, c = x.shape
    rb = _row_block(r, 1024)

    def body(x_ref, o_ref):
        acc = x_ref[0].astype(F32)
        for k in range(1, n):
            acc = acc + x_ref[k].astype(F32)
        o_ref[...] = acc.astype(o_ref.dtype)

    return pl.pallas_call(
        body, name=name, grid=(r // rb,), out_shape=jax.ShapeDtypeStruct((r, c), out_dtype),
        in_specs=[pl.BlockSpec((n, rb, c), lambda i: (0, i, 0))], out_specs=pl.BlockSpec((rb, c), lambda i: (i, 0)),
        compiler_params=_cp("parallel"),
    )(x)


def pair_sum(g, got, core, name):
    _, r2, n = g.shape
    hr = r2 // 2
    rb = math.gcd(hr, 512)
    nb = hr // rb

    def body(c_ref, a_ref, b_ref, o_ref):
        o_ref[...] = (a_ref[...].astype(F32) + b_ref[...].astype(F32)).astype(o_ref.dtype)

    spec = pl.BlockSpec((None, rb, n), lambda s, i, c_ref: (s, i, 0))
    return pl.pallas_call(
        body, name=name, out_shape=jax.ShapeDtypeStruct((4, hr, n), g.dtype),
        grid_spec=pltpu.PrefetchScalarGridSpec(
            num_scalar_prefetch=1, grid=(4, nb),
            in_specs=[pl.BlockSpec((None, rb, n), lambda s, i, c_ref: (s, c_ref[0] * nb + i, 0)), spec],
            out_specs=spec),
        compiler_params=_cp("parallel", "parallel"),
    )(core, g, got)


def adamw(w, gs, m, v, name):
    rows, c = w.shape
    ng = len(gs)
    r = rows // ng
    rb = _row_block(r, 128 if c > 2048 else 256 if c > 1024 else 1024)
    nb = r // rb
    bc1 = 1.0 - ADAM_B1 ** ADAM_STEP
    bc2 = 1.0 - ADAM_B2 ** ADAM_STEP

    def body(w_ref, *refs):
        g_refs, (m_ref, v_ref, g_out, d_ref, nm_ref, nv_ref) = refs[:ng], refs[ng:]
        part = pl.program_id(0) // nb
        gg = g_refs[0][...]
        for k in range(1, ng):
            gg = jnp.where(part == k, g_refs[k][...], gg)
        nm = ADAM_B1 * m_ref[...] + (1.0 - ADAM_B1) * gg
        nv = ADAM_B2 * v_ref[...] + (1.0 - ADAM_B2) * (gg * gg)
        g_out[...] = gg
        d_ref[...] = -ADAM_LR * ((nm / bc1) / (jnp.sqrt(nv / bc2) + ADAM_EPS) + ADAM_WD * w_ref[...])
        nm_ref[...] = nm
        nv_ref[...] = nv

    spec = pl.BlockSpec((rb, c), lambda i: (i, 0))
    gspecs = [pl.BlockSpec((rb, c), functools.partial(lambda k, i: (jnp.clip(i - k * nb, 0, nb - 1), 0), k))
              for k in range(ng)]
    sds = jax.ShapeDtypeStruct((rows, c), F32)
    return pl.pallas_call(
        body, name=name, grid=(ng * nb,), out_shape=(sds,) * 4, in_specs=[spec] + gspecs + [spec, spec],
        out_specs=(spec,) * 4, compiler_params=_cp("parallel"),
    )(w, *gs, m, v)


def adamw_halves(w, mine, other, m, v, core, name):
    rows, c = w.shape
    nl = len(mine)
    hr = rows // (2 * nl)
    rb = _row_block(hr, 128 if c > 2048 else 256 if c > 1024 else 1024)
    nb = hr // rb
    bc1 = 1.0 - ADAM_B1 ** ADAM_STEP
    bc2 = 1.0 - ADAM_B2 ** ADAM_STEP

    def body(c_ref, w_ref, *refs):
        mine_refs, other_refs = refs[:nl], refs[nl:2 * nl]
        m_ref, v_ref, g_out, d_ref, nm_ref, nv_ref = refs[2 * nl:]
        part = pl.program_id(0) // nb
        layer, half = part // 2, part % 2
        gg = jnp.where(half == c_ref[0], mine_refs[0][...], other_refs[0][...])
        for k in range(1, nl):
            gg = jnp.where(layer == k, jnp.where(half == c_ref[0], mine_refs[k][...], other_refs[k][...]), gg)
        nm = ADAM_B1 * m_ref[...] + (1.0 - ADAM_B1) * gg
        nv = ADAM_B2 * v_ref[...] + (1.0 - ADAM_B2) * (gg * gg)
        g_out[...] = gg
        d_ref[...] = -ADAM_LR * ((nm / bc1) / (jnp.sqrt(nv / bc2) + ADAM_EPS) + ADAM_WD * w_ref[...])
        nm_ref[...] = nm
        nv_ref[...] = nv

    spec = pl.BlockSpec((rb, c), lambda i, c_ref: (i, 0))
    gspecs = [pl.BlockSpec((rb, c), functools.partial(
        lambda k, i, c_ref: (jnp.clip(i - 2 * k * nb, 0, 2 * nb - 1) % nb, 0), k)) for k in range(nl)]
    sds = jax.ShapeDtypeStruct((rows, c), F32)
    return pl.pallas_call(
        body, name=name, out_shape=(sds,) * 4,
        grid_spec=pltpu.PrefetchScalarGridSpec(
            num_scalar_prefetch=1, grid=(2 * nl * nb,), in_specs=[spec] + gspecs + gspecs + [spec, spec],
            out_specs=(spec,) * 4),
        compiler_params=_cp("parallel"),
    )(core, w, *mine, *other, m, v)


def _pick(n, prefs):
    for p in prefs:
        if n % p == 0:
            return p
    return n


def _hosted(body, n_in, n_out, comm, first, last):
    if comm is None:
        return body
    k, ns = len(comm["ins"]), len(comm["scratch"])

    def wrapped(*refs):
        ins, cins = refs[:n_in], refs[n_in:n_in + k]
        outs, couts = refs[n_in + k:n_in + k + n_out], refs[n_in + k + n_out:n_in + 2 * k + n_out]
        rest = refs[n_in + 2 * k + n_out:]
        scratch, sems = rest[:len(rest) - ns], rest[len(rest) - ns:]

        @pl.when(first())
        def _():
            comm["start"](cins, couts, sems)

        body(*ins, *outs, *scratch)

        @pl.when(last())
        def _():
            comm["finish"](cins, couts, sems)

    return wrapped


def _comm_extras(comm):
    if comm is None:
        return [], [], [], []
    return list(comm["ins"]), [ANY] * len(comm["ins"]), list(comm["outs"]), list(comm["scratch"])


def _mm_body(dims, nk, out_dtype):
    def body(a_ref, b_ref, o_ref, acc=None):
        kk = pl.program_id(2)
        part = lax.dot_general(a_ref[...].astype(BF16), b_ref[...].astype(BF16), (dims, ((), ())),
                               preferred_element_type=F32)
        if nk == 1:
            o_ref[...] = part.astype(out_dtype)
        else:
            @pl.when(kk == 0)
            def _():
                acc[...] = part

            @pl.when(kk > 0)
            def _():
                acc[...] += part

            @pl.when(kk == nk - 1)
            def _():
                o_ref[...] = acc[...].astype(out_dtype)
    return body


def _acc(nk, shape):
    return [pltpu.VMEM(shape, F32)] if nk > 1 else []


def mm_nn(a, w, out_dtype, name, comm=None):
    M, K = a.shape
    S, _, Ns = w.shape
    tm = _pick(M, (1088, 1024, 512, 256, 128))
    tn = _pick(Ns, (1024, 896, 1408, 512, 256, 128))
    tk = _pick(K, (2816, 2048, 1408, 1024, 512, 256, 128))
    nps, nk = Ns // tn, K // tk
    grid = (S * nps, M // tm, nk)
    ids = lambda: [pl.program_id(d) for d in range(3)]
    first = lambda: functools.reduce(jnp.logical_and, [p == 0 for p in ids()])
    last = lambda: functools.reduce(jnp.logical_and, [p == g - 1 for p, g in zip(ids(), grid)])
    cin, cspec, cout, csem = _comm_extras(comm)
    out = pl.pallas_call(
        _hosted(_mm_body(((1,), (0,)), nk, out_dtype), 2, 1, comm, first, last), name=name, grid=grid,
        out_shape=[jax.ShapeDtypeStruct((M, S * Ns), out_dtype)] + cout,
        in_specs=[pl.BlockSpec((tm, tk), lambda j, i, k: (i, k)),
                  pl.BlockSpec((None, tk, tn), lambda j, i, k: (j // nps, k, j % nps))] + cspec,
        out_specs=[pl.BlockSpec((tm, tn), lambda j, i, k: (i, j))] + cspec,
        scratch_shapes=_acc(nk, (tm, tn)) + csem,
        compiler_params=_cp(*(("arbitrary",) * 3 if comm else ("parallel", "parallel", "arbitrary"))),
    )(a, w, *cin)
    return out[0] if comm is None else out


def mm_nt(dy, w, out_dtype, name, comm=None):
    M, N = dy.shape
    S, K, Ns = w.shape
    tm = _pick(M, (1088, 1024, 512, 256, 128))
    tn = _pick(K, (1408, 1024, 512, 256, 128))
    tk = _pick(Ns, (2816, 2048, 1792, 1408, 1024, 896, 512, 256, 128))
    kps, nk = Ns // tk, N // tk
    out = _pcall(
        _mm_body(((1,), (1,)), nk, out_dtype), name=name, grid=(K // tn, M // tm, nk),
        out_shape=[jax.ShapeDtypeStruct((M, K), out_dtype)],
        in_specs=[pl.BlockSpec((tm, tk), lambda j, i, k: (i, k)),
                  pl.BlockSpec((None, tn, tk), lambda j, i, k: (k // kps, j, k % kps))],
        out_specs=[pl.BlockSpec((tm, tn), lambda j, i, k: (i, j))], args=(dy, w),
        scratch=_acc(nk, (tm, tn)), sem=("parallel", "parallel", "arbitrary"), comm=comm)
    return out[0] if comm is None else out


def mm_tn(a, dy, S, out_dtype, name):
    M, K = a.shape
    N = dy.shape[1]
    Ns = N // S
    to = _pick(K, (1024, 512, 256, 128))
    tn = _pick(Ns, (2816, 2048, 1792, 1408, 1024, 896, 512, 256, 128))
    tk = _pick(M, (1088, 1024, 512, 256, 128))
    nps, nk = Ns // tn, M // tk
    return pl.pallas_call(
        _mm_body(((0,), (0,)), nk, out_dtype), name=name, grid=(K // to, S * nps, nk),
        out_shape=jax.ShapeDtypeStruct((S, K, Ns), out_dtype),
        in_specs=[pl.BlockSpec((tk, to), lambda i, j, k: (k, i)),
                  pl.BlockSpec((tk, tn), lambda i, j, k: (k, j))],
        out_specs=pl.BlockSpec((None, to, tn), lambda i, j, k: (j // nps, i, j % nps)),
        scratch_shapes=_acc(nk, (to, tn)), compiler_params=_cp("parallel", "parallel", "arbitrary"),
    )(a, dy)


_DIMS = {"nn": ((1,), (0,)), "nt": ((1,), (1,)), "tn": ((0,), (0,))}


def _dot(a, b, mode):
    return lax.dot_general(a.astype(BF16), b.astype(BF16), (_DIMS[mode], ((), ())), preferred_element_type=F32)


@functools.partial(jax.custom_vjp, nondiff_argnums=(2,))
def mmf(a, b, mode):
    return _dot(a, b, mode)


def _mmf_fwd(a, b, mode):
    return _dot(a, b, mode), (a, b)


def _mmf_bwd(mode, res, ct):
    a, b = res
    if mode == "nn":
        return _dot(ct, b, "nt"), _dot(a, ct, "tn")
    if mode == "nt":
        return _dot(ct, b, "nn"), _dot(ct, a, "tn")
    return _dot(b, ct, "nt"), _dot(a, ct, "nn")


mmf.defvjp(_mmf_fwd, _mmf_bwd)


def _dot_hi(m, g):
    return jnp.dot(m, g, precision=lax.Precision.HIGHEST, preferred_element_type=F32)


@jax.custom_vjp
def cumdot(m, mt, g):
    return _dot_hi(m, g)


def _cumdot_fwd(m, mt, g):
    return _dot_hi(m, g), (m, mt)


def _cumdot_bwd(res, ct):
    m, mt = res
    return jnp.zeros_like(m), jnp.zeros_like(mt), _dot_hi(mt, ct)


cumdot.defvjp(_cumdot_fwd, _cumdot_bwd)


def _rms(x, w):
    return x * lax.rsqrt(jnp.mean(x * x, axis=-1, keepdims=True) + EPS) * w


def _silu(x):
    return x * jax.nn.sigmoid(x)


RT = 16


def _gn_math(has_gate, x, m, gate, lnw, shift, scale):
    xn = x + gate * m if has_gate else x
    h = _rms(xn, lnw) * (1.0 + scale) + shift
    return xn, h


def _seg_spec(width, ncb):
    return pl.BlockSpec((None, RT, width), lambda i: (jnp.minimum(i // ncb, 1), 0, 0))


def gate_norm(x, m, gate, lnw, shift, scale, nc, R, name):
    T, D = x.shape
    has_gate = m is not None
    ncb = nc // R

    def body(*refs):
        if has_gate:
            x_ref, m_ref, g_ref, w_ref, sh_ref, sc_ref, xn_ref, h_ref = refs
        else:
            x_ref, w_ref, sh_ref, sc_ref, h_ref = refs

        def step(t, carry):
            rows = pl.ds(pl.multiple_of(t * RT, RT), RT)
            xn, h = _gn_math(has_gate, x_ref[rows, :], m_ref[rows, :] if has_gate else None,
                             g_ref[...] if has_gate else None, w_ref[...], sh_ref[...], sc_ref[...])
            if has_gate:
                xn_ref[rows, :] = xn
            h_ref[rows, :] = h.astype(BF16)
            return carry

        lax.fori_loop(0, R // RT, step, 0)

    row = pl.BlockSpec((R, D), lambda i: (i, 0))
    seg = _seg_spec(D, ncb)
    shared = pl.BlockSpec((None, RT, D), lambda i: (0, 0, 0))
    if has_gate:
        ins, in_specs = (x, m, gate, lnw, shift, scale), [row, row, seg, shared, seg, seg]
        out_shape = (jax.ShapeDtypeStruct((T, D), F32), jax.ShapeDtypeStruct((T, D), BF16))
        out_specs = (row, row)
    else:
        ins, in_specs = (x, lnw, shift, scale), [row, shared, seg, seg]
        out_shape, out_specs = jax.ShapeDtypeStruct((T, D), BF16), row
    out = pl.pallas_call(body, name=name, grid=(T // R,), out_shape=out_shape, in_specs=in_specs,
                         out_specs=out_specs, compiler_params=_cp("parallel"))(*ins)
    return out if has_gate else (None, out)


def gate_norm_bwd(x, m, gate, lnw, shift, scale, dxn, dh, nc, R, name):
    T, D = x.shape
    has_gate = m is not None
    ncb = nc // R

    def body(*refs):
        if has_gate:
            (x_ref, m_ref, g_ref, w_ref, sh_ref, sc_ref, dxn_ref, dh_ref,
             dx_ref, dm_ref, dg_ref, dw_ref, dsh_ref, dsc_ref) = refs
        else:
            x_ref, w_ref, sh_ref, sc_ref, dxn_ref, dh_ref, dx_ref, dw_ref, dsh_ref, dsc_ref = refs
        i = pl.program_id(0)

        @pl.when(i == 0)
        def _():
            dw_ref[...] = jnp.zeros_like(dw_ref)

        @pl.when((i == 0) | (i == ncb))
        def _():
            dsh_ref[...] = jnp.zeros_like(dsh_ref)
            dsc_ref[...] = jnp.zeros_like(dsc_ref)
            if has_gate:
                dg_ref[...] = jnp.zeros_like(dg_ref)

        def step(t, carry):
            rows = pl.ds(pl.multiple_of(t * RT, RT), RT)
            ct = (dxn_ref[rows, :], dh_ref[rows, :])
            if has_gate:
                _, vjp = jax.vjp(functools.partial(_gn_math, True), x_ref[rows, :], m_ref[rows, :], g_ref[...],
                                 w_ref[...], sh_ref[...], sc_ref[...])
                dx, dm, dg, dw, dsh, dsc = vjp(ct)
                dm_ref[rows, :] = dm.astype(BF16)
                dg_ref[...] += dg
            else:
                f = lambda x_, w_, sh_, sc_: _gn_math(False, x_, None, None, w_, sh_, sc_)[1]
                _, vjp = jax.vjp(f, x_ref[rows, :], w_ref[...], sh_ref[...], sc_ref[...])
                dx, dw, dsh, dsc = vjp(ct[1])
                dx = dx + ct[0]
            dx_ref[rows, :] = dx
            dw_ref[...] += dw
            dsh_ref[...] += dsh
            dsc_ref[...] += dsc
            return carry

        lax.fori_loop(0, R // RT, step, 0)

    row = pl.BlockSpec((R, D), lambda i: (i, 0))
    seg = _seg_spec(D, ncb)
    shared = pl.BlockSpec((None, RT, D), lambda i: (0, 0, 0))
    full, segs, one = jax.ShapeDtypeStruct((T, D), F32), jax.ShapeDtypeStruct((2, RT, D), F32), \
        jax.ShapeDtypeStruct((1, RT, D), F32)
    if has_gate:
        ins = (x, m, gate, lnw, shift, scale, dxn, dh)
        in_specs = [row, row, seg, shared, seg, seg, row, row]
        out_shape = (full, jax.ShapeDtypeStruct((T, D), BF16), segs, one, segs, segs)
        out_specs = (row, row, seg, shared, seg, seg)
    else:
        ins = (x, lnw, shift, scale, dxn, dh)
        in_specs = [row, shared, seg, seg, row, row]
        out_shape = (full, one, segs, segs)
        out_specs = (row, shared, seg, seg)
    out = pl.pallas_call(body, name=name, grid=(T // R,), out_shape=out_shape, in_specs=in_specs,
                         out_specs=out_specs, compiler_params=_cp("arbitrary"))(*ins)
    if has_gate:
        return out
    dx, dw, dsh, dsc = out
    return dx, None, None, dw, dsh, dsc


def gate_loss(x, m, gate, target, nc, R, name):
    T, D = x.shape
    ncb = nc // R

    def body(x_ref, m_ref, g_ref, t_ref, loss_ref, dx_ref, dm_ref, dg_ref):
        i = pl.program_id(0)

        @pl.when(i == 0)
        def _():
            loss_ref[...] = jnp.zeros_like(loss_ref)

        @pl.when((i == 0) | (i == ncb))
        def _():
            dg_ref[...] = jnp.zeros_like(dg_ref)

        live = jnp.where(i >= ncb, 1.0, 0.0).astype(F32)

        def step(t, carry):
            rows = pl.ds(pl.multiple_of(t * RT, RT), RT)
            mm_ = m_ref[rows, :]
            g = g_ref[...]
            e = (x_ref[rows, :] + g * mm_ - t_ref[rows, :]) * live
            dy = e * (1.0 / D)
            loss_ref[...] += 0.5 * e * dy
            dx_ref[rows, :] = dy
            dm_ref[rows, :] = (dy * g).astype(BF16)
            dg_ref[...] += dy * mm_
            return carry

        lax.fori_loop(0, R // RT, step, 0)

    row = pl.BlockSpec((R, D), lambda i: (i, 0))
    seg = _seg_spec(D, ncb)
    return pl.pallas_call(
        body, name=name, grid=(T // R,),
        out_shape=(jax.ShapeDtypeStruct((RT, D), F32), jax.ShapeDtypeStruct((T, D), F32),
                   jax.ShapeDtypeStruct((T, D), BF16), jax.ShapeDtypeStruct((2, RT, D), F32)),
        in_specs=[row, row, seg, pl.BlockSpec((R, D), lambda i: (jnp.maximum(i - ncb, 0), 0))],
        out_specs=(pl.BlockSpec((RT, D), lambda i: (0, 0)), row, row, seg),
        compiler_params=_cp("arbitrary"),
    )(x, m, gate, target)


def _hg_chunk(rev, lb, z, iv, hq, st):
    f = lb + (1.0 - lb) * jax.nn.sigmoid(z)
    g = jnp.log(jnp.maximum(f, F_FLOOR))
    k = (1.0 - lb) * jax.nn.sigmoid(-z)
    q = _silu(hq)
    ri = lax.broadcasted_iota(jnp.int32, (CHUNK, CHUNK), 0)
    ci = lax.broadcasted_iota(jnp.int32, (CHUNK, CHUNK), 1)
    r1 = lax.broadcasted_iota(jnp.int32, (CHUNK, 1), 0)
    seen = (ci >= ri) if rev else (ci <= ri)
    seen_t = (ci <= ri) if rev else (ci >= ri)
    cum = cumdot(seen.astype(F32), seen_t.astype(F32), g)
    tot = jnp.sum(g, axis=0, keepdims=True)
    att = jnp.zeros((CHUNK, CHUNK), F32)
    ref_rows = jnp.zeros_like(g)
    refs = []
    for b in range(CHUNK // SUB):
        before = (r1 >= SUB * (b + 1)) if rev else (r1 < SUB * b)
        r_b = jnp.sum(jnp.where(before, g, 0.0), axis=0, keepdims=True)
        in_b = (r1 >= SUB * b) & (r1 < SUB * (b + 1))
        ref_rows = ref_rows + jnp.where(in_b, r_b, 0.0)
        refs.append(r_b)
    qd = q * jnp.exp(cum - ref_rows)
    for b in range(CHUNK // SUB):
        kd = k * jnp.exp(jnp.minimum(refs[b] - cum, EXP_CLAMP))
        in_b = (ri >= SUB * b) & (ri < SUB * (b + 1))
        att = att + jnp.where(in_b, mmf(qd, kd, "nt"), 0.0)
    att = jnp.where(seen, att, 0.0)
    o = mmf(att, iv, "nn") + mmf(q * jnp.exp(cum), st, "nt")
    st_new = st * jnp.exp(tot) + mmf(iv, k * jnp.exp(tot - cum), "tn")
    return st_new, o


def _hg_cid(rev, i, ncs, n):
    if not rev:
        return i
    return jnp.where(i < ncs, ncs - 1 - i, ncs + n - 1 - i)


def hgrn_fwd(u, lb, rev, zcol, nc, hgw, name):
    T = u.shape[0]
    n, ncs, nh = T // CHUNK, nc // CHUNK, hgw // LANE

    def body(z_ref, v_ref, q_ref, lb_ref, o_ref, s_ref, st):
        i = pl.program_id(0)

        @pl.when(i == 0)
        def _():
            st[...] = jnp.zeros_like(st)

        for h in range(nh):
            cols = slice(h * LANE, (h + 1) * LANE)
            s_ref[h] = st[h]
            s_new, o = _hg_chunk(rev, lb_ref[:, cols], z_ref[:, cols], v_ref[:, cols], q_ref[:, cols], st[h])
            st[h] = s_new
            o_ref[:, cols] = o

    def col(cb):
        return pl.BlockSpec((CHUNK, hgw), lambda i: (_hg_cid(rev, i, ncs, n), cb))

    return pl.pallas_call(
        body, name=name, grid=(n,),
        out_shape=(jax.ShapeDtypeStruct((T, hgw), F32), jax.ShapeDtypeStruct((n, nh, LANE, LANE), F32)),
        in_specs=[col(zcol), col(2), col(7), pl.BlockSpec((1, hgw), lambda i: (0, 0))],
        out_specs=(pl.BlockSpec((CHUNK, hgw), lambda i: (_hg_cid(rev, i, ncs, n), 0)),
                   pl.BlockSpec((None, nh, LANE, LANE), lambda i: (i, 0, 0, 0))),
        scratch_shapes=[pltpu.VMEM((nh, LANE, LANE), F32)], compiler_params=_cp("arbitrary"),
    )(u, u, u, lb)


def hgrn_bwd(u, lb, states, do, rev, zcol, nc, hgw, name):
    T = u.shape[0]
    n, ncs, nh = T // CHUNK, nc // CHUNK, hgw // LANE

    def body(z_ref, v_ref, q_ref, lb_ref, s_ref, do_ref, dz_ref, dv_ref, dq_ref, dlb_ref, dst):
        j = pl.program_id(0)

        @pl.when(j == 0)
        def _():
            dst[...] = jnp.zeros_like(dst)
            dlb_ref[...] = jnp.zeros_like(dlb_ref)

        for h in range(nh):
            cols = slice(h * LANE, (h + 1) * LANE)
            _, vjp = jax.vjp(functools.partial(_hg_chunk, rev), lb_ref[:, cols], z_ref[:, cols], v_ref[:, cols],
                             q_ref[:, cols], s_ref[h])
            dlb, dz, dv, dq, ds = vjp((dst[h], do_ref[:, cols]))
            dst[h] = ds
            dz_ref[:, cols] = dz
            dv_ref[:, cols] = dv
            dq_ref[:, cols] = dq
            dlb_ref[:, cols] += dlb

    def cid(j):
        return _hg_cid(rev, n - 1 - j, ncs, n)

    def col(cb):
        return pl.BlockSpec((CHUNK, hgw), lambda j: (cid(j), cb))

    out = pl.BlockSpec((CHUNK, hgw), lambda j: (cid(j), 0))
    full = jax.ShapeDtypeStruct((T, hgw), F32)
    return pl.pallas_call(
        body, name=name, grid=(n,),
        out_shape=(full, full, full, jax.ShapeDtypeStruct((1, hgw), F32)),
        in_specs=[col(zcol), col(2), col(7), pl.BlockSpec((1, hgw), lambda j: (0, 0)),
                  pl.BlockSpec((None, nh, LANE, LANE), lambda j: (n - 1 - j, 0, 0, 0)), out],
        out_specs=(out, out, out, pl.BlockSpec((1, hgw), lambda j: (0, 0))),
        scratch_shapes=[pltpu.VMEM((nh, LANE, LANE), F32)], compiler_params=_cp("arbitrary"),
    )(u, u, u, lb, states, do)


HT = 128


def _head_group(nh, *col_offsets):
    for g in (4, 2):
        if nh % g == 0 and all(c % g == 0 for c in col_offsets):
            return g
    return 1


def _read_math(ofw, obw, g, w):
    return _rms(ofw + obw, w) * _silu(g)


def hg_read(ofw, obw, u, w, gcol, R, name):
    T, hgw = ofw.shape
    nh = hgw // LANE
    g = _head_group(nh, gcol)

    def body(a_ref, b_ref, g_ref, w_ref, o_ref):
        for j in range(g):
            cols = slice(j * LANE, (j + 1) * LANE)
            for t in range(R // HT):
                rows = slice(t * HT, (t + 1) * HT)
                o_ref[rows, cols] = _read_math(a_ref[rows, cols], b_ref[rows, cols], g_ref[rows, cols],
                                               w_ref[...]).astype(BF16)

    blk = pl.BlockSpec((R, g * LANE), lambda i, h: (i, h))
    return pl.pallas_call(
        body, name=name, grid=(T // R, nh // g), out_shape=jax.ShapeDtypeStruct((T, hgw), BF16),
        in_specs=[blk, blk, pl.BlockSpec((R, g * LANE), lambda i, h: (i, gcol // g + h)),
                  pl.BlockSpec((1, LANE), lambda i, h: (0, 0))],
        out_specs=blk, compiler_params=_cp("parallel", "parallel"),
    )(ofw, obw, u, w)


def hg_read_bwd(ofw, obw, u, w, dout, gcol, ocol, R, name):
    T, hgw = ofw.shape
    nh = hgw // LANE
    g = _head_group(nh, gcol, ocol)

    def body(a_ref, b_ref, g_ref, w_ref, d_ref, do_ref, dg_ref, dw_ref):
        @pl.when(pl.program_id(1) == 0)
        def _():
            dw_ref[...] = jnp.zeros_like(dw_ref)

        for j in range(g):
            cols = slice(j * LANE, (j + 1) * LANE)
            for t in range(R // HT):
                rows = slice(t * HT, (t + 1) * HT)
                _, vjp = jax.vjp(_read_math, a_ref[rows, cols], b_ref[rows, cols], g_ref[rows, cols], w_ref[...])
                da, _, dg, dw = vjp(d_ref[rows, cols])
                do_ref[rows, cols] = da
                dg_ref[rows, cols] = dg
                dw_ref[j] += dw

    blk = pl.BlockSpec((R, g * LANE), lambda h, i: (i, h))
    full = jax.ShapeDtypeStruct((T, hgw), F32)
    return pl.pallas_call(
        body, name=name, grid=(nh // g, T // R), out_shape=(full, full, jax.ShapeDtypeStruct((nh, 1, LANE), F32)),
        in_specs=[blk, blk, pl.BlockSpec((R, g * LANE), lambda h, i: (i, gcol // g + h)),
                  pl.BlockSpec((1, LANE), lambda h, i: (0, 0)),
                  pl.BlockSpec((R, g * LANE), lambda h, i: (i, ocol // g + h))],
        out_specs=(blk, blk, pl.BlockSpec((g, 1, LANE), lambda h, i: (h, 0, 0))),
        compiler_params=_cp("parallel", "arbitrary"),
    )(ofw, obw, u, w, dout)


def _na_step(qw, ow, bias, qraw, kl, vl, kc, vc):
    q = _rms(qraw, qw)
    s_loc = mmf(q, kl, "nt") * ATTN_SCALE + bias
    s_ctx = mmf(q, kc, "nt") * ATTN_SCALE
    m = lax.stop_gradient(jnp.maximum(jnp.max(s_loc, axis=-1, keepdims=True), jnp.max(s_ctx, axis=-1, keepdims=True)))
    p_loc = jnp.exp(s_loc - m)
    p_ctx = jnp.exp(s_ctx - m)
    inv = 1.0 / (jnp.sum(p_loc, axis=-1, keepdims=True) + jnp.sum(p_ctx, axis=-1, keepdims=True))
    return _rms(mmf(p_loc * inv, vl, "nn") + mmf(p_ctx * inv, vc, "nn"), ow)


def _na_geometry(nc, rows):
    ncs = nc // GRID_W
    win_r = min(WIN_R, rows)
    nloc = win_r * GRID_W

    def row_start(s):
        r = jnp.maximum(s - ncs, 0)
        return jnp.clip(r - win_r // 2, 0, rows - win_r)

    def bias_idx(s):
        r = s - ncs
        return jnp.where(s < ncs, win_r, r - jnp.clip(r - win_r // 2, 0, rows - win_r))

    return ncs, win_r, nloc, row_start, bias_idx


def na_bias_tables(rpb, rows):
    win_r = min(WIN_R, rows)
    nh = rpb.shape[0]
    sel_r = np.zeros((win_r, win_r, 2 * WIN_R - 1), np.float32)
    for off in range(win_r):
        for jr in range(win_r):
            sel_r[off, jr, jr - off + WIN_R - 1] = 1.0
    qc = np.arange(GRID_W)[:, None]
    kc = np.arange(GRID_W)[None, :]
    wstart = np.clip(qc - WIN_C // 2, 0, GRID_W - WIN_C)
    ok = (kc >= wstart) & (kc < wstart + WIN_C)
    sel_c = np.zeros((GRID_W, GRID_W, 2 * WIN_C - 1), np.float32)
    sel_c[np.broadcast_to(qc, ok.shape)[ok], np.broadcast_to(kc, ok.shape)[ok], (kc - qc + WIN_C - 1)[ok]] = 1.0
    hi = lax.Precision.HIGHEST
    t = jnp.einsum("hab,oja->hojb", rpb, sel_r, precision=hi)
    t = jnp.einsum("hojb,qkb->hoqjk", t, sel_c, precision=hi)
    t = jnp.where(ok[None, None, :, None, :], t, NEG_INF)
    t = jnp.concatenate([t, jnp.full((nh, 1, GRID_W, win_r, GRID_W), NEG_INF, F32)], axis=1)
    return t.reshape(nh, win_r + 1, GRID_W, win_r * GRID_W)


def kv_prep(u, kw, kcol, vcol, naw, R, name):
    T = u.shape[0]
    g = _head_group(naw // LANE, kcol, vcol)

    def body(k_ref, v_ref, w_ref, kn_ref, vb_ref):
        for j in range(g):
            cols = slice(j * LANE, (j + 1) * LANE)
            kn_ref[:, cols] = _rms(k_ref[:, cols], w_ref[...]).astype(BF16)
        vb_ref[...] = v_ref[...].astype(BF16)

    blk = pl.BlockSpec((R, g * LANE), lambda i, h: (i, h))
    sds = jax.ShapeDtypeStruct((T, naw), BF16)
    return pl.pallas_call(
        body, name=name, grid=(T // R, naw // LANE // g), out_shape=(sds, sds),
        in_specs=[pl.BlockSpec((R, g * LANE), lambda i, h: (i, kcol // g + h)),
                  pl.BlockSpec((R, g * LANE), lambda i, h: (i, vcol // g + h)),
                  pl.BlockSpec((1, LANE), lambda i, h: (0, 0))],
        out_specs=(blk, blk), compiler_params=_cp("parallel", "parallel"),
    )(u, u, kw)


def kv_prep_bwd(u, kw, dkn, kcol, naw, R, name):
    T = u.shape[0]
    nh = naw // LANE
    g = _head_group(nh, kcol)

    def body(k_ref, w_ref, d_ref, dk_ref, dw_ref):
        @pl.when(pl.program_id(1) == 0)
        def _():
            dw_ref[...] = jnp.zeros_like(dw_ref)

        for j in range(g):
            cols = slice(j * LANE, (j + 1) * LANE)
            for t in range(R // HT):
                rows = slice(t * HT, (t + 1) * HT)
                _, vjp = jax.vjp(_rms, k_ref[rows, cols], w_ref[...])
                dk, dw = vjp(d_ref[rows, cols])
                dk_ref[rows, cols] = dk
                dw_ref[j] += dw

    blk = pl.BlockSpec((R, g * LANE), lambda h, i: (i, h))
    return pl.pallas_call(
        body, name=name, grid=(nh // g, T // R),
        out_shape=(jax.ShapeDtypeStruct((T, naw), F32), jax.ShapeDtypeStruct((nh, 1, LANE), F32)),
        in_specs=[pl.BlockSpec((R, g * LANE), lambda h, i: (i, kcol // g + h)),
                  pl.BlockSpec((1, LANE), lambda h, i: (0, 0)), blk],
        out_specs=(blk, pl.BlockSpec((g, 1, LANE), lambda h, i: (h, 0, 0))),
        compiler_params=_cp("parallel", "arbitrary"),
    )(u, kw, dkn)


NA_HB = 4


def _na_operands(j, s, nc, nloc, row_start, q_refs, k_ref, v_ref, qw_ref, ow_ref, b_ref):
    cols = slice(j * LANE, (j + 1) * LANE)
    loc = pl.ds(pl.multiple_of(nc + row_start(s) * GRID_W, GRID_W), nloc)
    ops = (qw_ref[...], ow_ref[:, cols], b_ref[j], q_refs[j][...], k_ref[loc, cols].astype(F32),
           v_ref[loc, cols].astype(F32), k_ref[0:nc, cols].astype(F32), v_ref[0:nc, cols].astype(F32))
    return cols, loc, ops


def _grid_ends(grid):
    ids = lambda: [pl.program_id(d) for d in range(len(grid))]
    first = lambda: functools.reduce(jnp.logical_and, [p == 0 for p in ids()])
    last = lambda: functools.reduce(jnp.logical_and, [p == g - 1 for p, g in zip(ids(), grid)])
    return first, last


def na_fwd(u, kn, vb, qw, ow, bias, qcol, nc, name, comm=None):
    T, naw = kn.shape
    nh, rows = naw // LANE, (T - nc) // GRID_W
    hb = NA_HB if nh % NA_HB == 0 else 1
    ncs, win_r, nloc, row_start, bias_idx = _na_geometry(nc, rows)

    def body(*refs):
        q_refs, (k_ref, v_ref, qw_ref, ow_ref, b_ref, o_ref) = refs[:hb], refs[hb:]
        s = pl.program_id(1)
        for j in range(hb):
            cols, _, ops = _na_operands(j, s, nc, nloc, row_start, q_refs, k_ref, v_ref, qw_ref, ow_ref, b_ref)
            o_ref[:, cols] = _na_step(*ops).astype(BF16)

    wide = pl.BlockSpec((T, hb * LANE), lambda g, s: (0, g), pipeline_mode=pl.Buffered(1))
    grid = (nh // hb, T // GRID_W)
    cin, cspec, cout, csem = _comm_extras(comm)
    out = pl.pallas_call(
        _hosted(body, hb + 5, 1, comm, *_grid_ends(grid)), name=name, grid=grid,
        out_shape=[jax.ShapeDtypeStruct((T, naw), BF16)] + cout,
        in_specs=[pl.BlockSpec((GRID_W, LANE), functools.partial(lambda j, g, s: (s, qcol + g * hb + j), j))
                  for j in range(hb)]
        + [wide, wide, pl.BlockSpec((1, LANE), lambda g, s: (0, 0)), pl.BlockSpec((1, hb * LANE), lambda g, s: (0, g)),
           pl.BlockSpec((hb, None, GRID_W, nloc), lambda g, s: (g, bias_idx(s), 0, 0))] + cspec,
        out_specs=[pl.BlockSpec((GRID_W, hb * LANE), lambda g, s: (s, g))] + cspec, scratch_shapes=csem,
        compiler_params=_cp("arbitrary", "arbitrary"),
    )(*([u] * hb), kn, vb, qw, ow, bias, *cin)
    return out[0] if comm is None else out


def na_bwd(u, kn, vb, qw, ow, bias, dout, qcol, ocol, nc, name, comm=None):
    T, naw = kn.shape
    nh, rows = naw // LANE, (T - nc) // GRID_W
    hb = NA_HB if nh % NA_HB == 0 else 1
    ncs, win_r, nloc, row_start, bias_idx = _na_geometry(nc, rows)
    fresh = [0] + [ncs + r for r in range(rows) if r == 0 or r - np.clip(r - win_r // 2, 0, rows - win_r)
                   != (r - 1) - np.clip(r - 1 - win_r // 2, 0, rows - win_r)]

    def body(*refs):
        q_refs, d_refs = refs[:hb], refs[hb:2 * hb]
        k_ref, v_ref, qw_ref, ow_ref, b_ref, dq_ref, dk_ref, dv_ref, db_ref, dqw_ref, dow_ref = refs[2 * hb:]
        s = pl.program_id(1)

        @pl.when(s == 0)
        def _():
            dk_ref[...] = jnp.zeros_like(dk_ref)
            dv_ref[...] = jnp.zeros_like(dv_ref)
            dqw_ref[...] = jnp.zeros_like(dqw_ref)
            dow_ref[...] = jnp.zeros_like(dow_ref)

        first = functools.reduce(lambda a, b: a | b, [s == f for f in fresh])

        @pl.when(first)
        def _():
            db_ref[...] = jnp.zeros_like(db_ref)

        for j in range(hb):
            cols, loc, ops = _na_operands(j, s, nc, nloc, row_start, q_refs, k_ref, v_ref, qw_ref, ow_ref, b_ref)
            _, vjp = jax.vjp(_na_step, *ops)
            dqw, dow, db, dq, dkl, dvl, dkc, dvc = vjp(d_refs[j][...])
            dq_ref[:, cols] = dq
            dk_ref[loc, cols] += dkl
            dv_ref[loc, cols] += dvl
            dk_ref[0:nc, cols] += dkc
            dv_ref[0:nc, cols] += dvc
            db_ref[j] += db
            dqw_ref[j] += dqw
            dow_ref[j] += dow

    wide = pl.BlockSpec((T, hb * LANE), lambda g, s: (0, g), pipeline_mode=pl.Buffered(1))
    hvec = pl.BlockSpec((hb, 1, LANE), lambda g, s: (g, 0, 0))
    full = jax.ShapeDtypeStruct((T, naw), F32)
    hv = jax.ShapeDtypeStruct((nh, 1, LANE), F32)
    bspec = pl.BlockSpec((hb, None, GRID_W, nloc), lambda g, s: (g, bias_idx(s), 0, 0))
    grid = (nh // hb, T // GRID_W)
    cin, cspec, cout, csem = _comm_extras(comm)
    return pl.pallas_call(
        _hosted(body, 2 * hb + 5, 6, comm, *_grid_ends(grid)), name=name, grid=grid,
        out_shape=[full, full, full, jax.ShapeDtypeStruct(bias.shape, F32), hv, hv] + cout,
        in_specs=[pl.BlockSpec((GRID_W, LANE), functools.partial(lambda j, g, s: (s, qcol + g * hb + j), j))
                  for j in range(hb)]
        + [pl.BlockSpec((GRID_W, LANE), functools.partial(lambda j, g, s: (s, ocol + g * hb + j), j))
           for j in range(hb)]
        + [wide, wide, pl.BlockSpec((1, LANE), lambda g, s: (0, 0)), pl.BlockSpec((1, hb * LANE), lambda g, s: (0, g)),
           bspec] + cspec,
        out_specs=[pl.BlockSpec((GRID_W, hb * LANE), lambda g, s: (s, g)), wide, wide, bspec, hvec, hvec] + cspec,
        scratch_shapes=csem, compiler_params=_cp("arbitrary", "arbitrary"),
    )(*([u] * hb), *([dout] * hb), kn, vb, qw, ow, bias, *cin)


def _halo_specs(R, width, T, col):
    hb = R // 8
    prev = pl.BlockSpec((8, width), lambda j, i: (jnp.maximum(i * hb - 1, 0), col(j, i)))
    nxt = pl.BlockSpec((8, width), lambda j, i: (jnp.minimum((i + 1) * hb, T // 8 - 1), col(j, i)))
    return prev, nxt


def _edge_flags(i, ncb, nblk):
    has_prev = jnp.where((i == 0) | (i == ncb), 0.0, 1.0).astype(F32)
    has_next = jnp.where((i == ncb - 1) | (i == nblk - 1), 0.0, 1.0).astype(F32)
    return has_prev, has_next


def _shift_up(a, prev_row):
    r0 = lax.broadcasted_iota(jnp.int32, a.shape, 0) == 0
    return jnp.where(r0, prev_row, pltpu.roll(a, 1, 0))


def _shift_dn(a, next_row):
    n = a.shape[0]
    rl = lax.broadcasted_iota(jnp.int32, a.shape, 0) == n - 1
    return jnp.where(rl, next_row, pltpu.roll(a, n - 1, 0))


def _conv3(a, prev_row, next_row, w_ref):
    return w_ref[0:1, :] * _shift_up(a, prev_row) + w_ref[1:2, :] * a + w_ref[2:3, :] * _shift_dn(a, next_row)


def _cv_post(b, y, w):
    return _rms(b * y, w)


def short_conv(u, cw, ow, bcol, nc, cvw, R, bwd_dout=None, ocol=0, name=""):
    T = u.shape[0]
    nh, nblk, ncb = cvw // LANE, T // R, nc // R
    bwd = bwd_dout is not None
    g = _head_group(nh, bcol, bcol + nh, bcol + 2 * nh, ocol)
    gw = g * LANE

    def body(b_ref, c_ref, v_ref, cp_ref, vp_ref, cn_ref, vn_ref, cw_ref, ow_ref, *rest):
        i = pl.program_id(1)
        has_prev, has_next = _edge_flags(i, ncb, nblk)
        p = c_ref[...] * v_ref[...]
        y = _conv3(p, cp_ref[7:8, :] * vp_ref[7:8, :] * has_prev, cn_ref[0:1, :] * vn_ref[0:1, :] * has_next, cw_ref)
        if bwd:
            d_ref, db_ref, dy_ref, dow_ref = rest

            @pl.when(i == 0)
            def _():
                dow_ref[...] = jnp.zeros_like(dow_ref)

        for j in range(g):
            cols = slice(j * LANE, (j + 1) * LANE)
            if not bwd:
                rest[0][:, cols] = _cv_post(b_ref[:, cols], y[:, cols], ow_ref[:, cols]).astype(BF16)
            else:
                _, vjp = jax.vjp(_cv_post, b_ref[:, cols], y[:, cols], ow_ref[:, cols])
                db, dy, dow = vjp(d_ref[:, cols])
                db_ref[:, cols] = db
                dy_ref[:, cols] = dy
                dow_ref[j] += dow

    def main(k):
        return pl.BlockSpec((R, gw), lambda h, i: (i, (bcol + k * nh) // g + h))

    cprev, cnext = _halo_specs(R, gw, T, lambda h, i: (bcol + nh) // g + h)
    vprev, vnext = _halo_specs(R, gw, T, lambda h, i: (bcol + 2 * nh) // g + h)
    in_specs = [main(0), main(1), main(2), cprev, vprev, cnext, vnext,
                pl.BlockSpec((3, gw), lambda h, i: (0, h)), pl.BlockSpec((1, gw), lambda h, i: (0, h))]
    ins = [u] * 7 + [cw, ow]
    blk = pl.BlockSpec((R, gw), lambda h, i: (i, h))
    if not bwd:
        out_shape, out_specs = jax.ShapeDtypeStruct((T, cvw), BF16), blk
    else:
        in_specs.append(pl.BlockSpec((R, gw), lambda h, i: (i, ocol // g + h)))
        ins.append(bwd_dout)
        full = jax.ShapeDtypeStruct((T, cvw), F32)
        out_shape = (full, full, jax.ShapeDtypeStruct((nh, 1, LANE), F32))
        out_specs = (blk, blk, pl.BlockSpec((g, 1, LANE), lambda h, i: (h, 0, 0)))
    return pl.pallas_call(body, name=name, grid=(nh // g, nblk), out_shape=out_shape, in_specs=in_specs,
                          out_specs=out_specs, compiler_params=_cp("parallel", "arbitrary"))(*ins)


def conv3_bwd(dy, src, cw, nc, R, W, prod_cols=None, col0=0, out_dtype=F32, name=""):
    T, C = dy.shape
    nblk, ncb = T // R, nc // R
    prod = prod_cols is not None

    def body(*refs):
        if prod:
            (d_ref, dp_ref, dn_ref, c_ref, v_ref, cp_ref, vp_ref, cn_ref, vn_ref, w_ref,
             dc_ref, dv_ref, dw_ref) = refs
        else:
            d_ref, dp_ref, dn_ref, p_ref, pp_ref, pn_ref, w_ref, o_ref, dw_ref = refs
        i = pl.program_id(1)
        has_prev, has_next = _edge_flags(i, ncb, nblk)

        @pl.when(i == 0)
        def _():
            dw_ref[...] = jnp.zeros_like(dw_ref)

        d = d_ref[...]
        d_up = _shift_up(d, dp_ref[7:8, :] * has_prev)
        d_dn = _shift_dn(d, dn_ref[0:1, :] * has_next)
        dp = w_ref[0:1, :] * d_dn + w_ref[1:2, :] * d + w_ref[2:3, :] * d_up
        if prod:
            c, v = c_ref[...], v_ref[...]
            p = c * v
            p_prev, p_next = cp_ref[7:8, :] * vp_ref[7:8, :] * has_prev, cn_ref[0:1, :] * vn_ref[0:1, :] * has_next
            dc_ref[...] = dp * v
            dv_ref[...] = dp * c
        else:
            p = p_ref[...]
            p_prev, p_next = pp_ref[7:8, :] * has_prev, pn_ref[0:1, :] * has_next
            o_ref[...] = dp.astype(out_dtype)
        dw_ref[0:1, :] += jnp.sum(_shift_up(p, p_prev) * d, axis=0, keepdims=True)
        dw_ref[1:2, :] += jnp.sum(p * d, axis=0, keepdims=True)
        dw_ref[2:3, :] += jnp.sum(_shift_dn(p, p_next) * d, axis=0, keepdims=True)

    blk = pl.BlockSpec((R, W), lambda j, i: (i, j))
    dprev, dnext = _halo_specs(R, W, T, lambda j, i: j)
    wspec = pl.BlockSpec((3, W), lambda j, i: (0, j))
    dwspec = pl.BlockSpec((8, W), lambda j, i: (0, j))
    dwshape = jax.ShapeDtypeStruct((8, C), F32)
    if prod:
        ccol, vcol = prod_cols
        cprev, cnext = _halo_specs(R, W, T, lambda j, i: ccol + j)
        vprev, vnext = _halo_specs(R, W, T, lambda j, i: vcol + j)
        in_specs = [blk, dprev, dnext, pl.BlockSpec((R, W), lambda j, i: (i, ccol + j)),
                    pl.BlockSpec((R, W), lambda j, i: (i, vcol + j)), cprev, vprev, cnext, vnext, wspec]
        ins = [dy, dy, dy] + [src] * 6 + [cw]
        full = jax.ShapeDtypeStruct((T, C), F32)
        out_shape, out_specs = (full, full, dwshape), (blk, blk, dwspec)
    else:
        sprev, snext = _halo_specs(R, W, T, lambda j, i: col0 + j)
        in_specs = [blk, dprev, dnext, pl.BlockSpec((R, W), lambda j, i: (i, col0 + j)), sprev, snext,
                    pl.BlockSpec((3, W), lambda j, i: (0, col0 + j))]
        ins = [dy, dy, dy, src, src, src, cw]
        out_shape, out_specs = (jax.ShapeDtypeStruct((T, C), out_dtype), dwshape), (blk, dwspec)
    return pl.pallas_call(body, name=name, grid=(C // W, nblk), out_shape=out_shape, in_specs=in_specs,
                          out_specs=out_specs, compiler_params=_cp("parallel", "arbitrary"))(*ins)


def _pcall(body, *, name, grid, in_specs, out_specs, out_shape, args, scratch=(), sem=None, comm=None):
    cin, cspec, cout, csem = _comm_extras(comm)
    if comm is not None:
        sem = ("arbitrary",) * len(grid)
    return pl.pallas_call(
        _hosted(body, len(in_specs), len(out_specs), comm, *_grid_ends(grid)), name=name, grid=grid,
        out_shape=list(out_shape) + cout, in_specs=list(in_specs) + cspec, out_specs=list(out_specs) + cspec,
        scratch_shapes=list(scratch) + csem, compiler_params=_cp(*sem))(*args, *cin)


def ffn_mid(uf, cw, cb, nc, R, W, name, comm=None):
    T, C = uf.shape
    F = C // 2
    nblk, ncb, nj = T // R, nc // R, F // W

    def body(g_ref, v_ref, gp_ref, vp_ref, gn_ref, vn_ref, wg_ref, wv_ref, bg_ref, bv_ref, a_ref):
        i = pl.program_id(1)
        has_prev, has_next = _edge_flags(i, ncb, nblk)
        yg = _conv3(g_ref[...], gp_ref[7:8, :] * has_prev, gn_ref[0:1, :] * has_next, wg_ref) + bg_ref[...]
        yv = _conv3(v_ref[...], vp_ref[7:8, :] * has_prev, vn_ref[0:1, :] * has_next, wv_ref) + bv_ref[...]
        a_ref[...] = (yg * jax.nn.sigmoid(yg) * yv).astype(BF16)

    gblk = pl.BlockSpec((R, W), lambda j, i: (i, j))
    vblk = pl.BlockSpec((R, W), lambda j, i: (i, nj + j))
    gprev, gnext = _halo_specs(R, W, T, lambda j, i: j)
    vprev, vnext = _halo_specs(R, W, T, lambda j, i: nj + j)
    in_specs = [gblk, vblk, gprev, vprev, gnext, vnext,
                pl.BlockSpec((3, W), lambda j, i: (0, j)), pl.BlockSpec((3, W), lambda j, i: (0, nj + j)),
                pl.BlockSpec((1, W), lambda j, i: (0, j)), pl.BlockSpec((1, W), lambda j, i: (0, nj + j))]
    return _pcall(body, name=name, grid=(nj, nblk), in_specs=in_specs, out_specs=[gblk],
                  out_shape=[jax.ShapeDtypeStruct((T, F), BF16)], args=[uf] * 6 + [cw, cw, cb, cb],
                  sem=("parallel", "arbitrary"), comm=comm)


def ffn_mid_bwd(uf, cw, cb, da, nc, R, W, name, comm=None):
    T, C = uf.shape
    F = C // 2
    nblk, ncb, nj = T // R, nc // R, F // W

    def body(g_ref, v_ref, gp_ref, vp_ref, gn_ref, vn_ref, d_ref, dp_ref, dn_ref, wg_ref, wv_ref, bg_ref, bv_ref,
             dug_ref, duv_ref, dwg_ref, dwv_ref, dbg_ref, dbv_ref):
        i = pl.program_id(1)
        has_prev, has_next = _edge_flags(i, ncb, nblk)

        @pl.when(i == 0)
        def _():
            for r in (dwg_ref, dwv_ref, dbg_ref, dbv_ref):
                r[...] = jnp.zeros_like(r)

        def taps(w_ref):
            return w_ref[0:1, :], w_ref[1:2, :], w_ref[2:3, :]

        def dy_of(yg, yv, d):
            sg = jax.nn.sigmoid(yg)
            return d * yv * (sg * (1.0 + yg * (1.0 - sg))), d * (yg * sg)

        g, v, d = g_ref[...], v_ref[...], d_ref[...]
        (wg0, wg1, wg2), (wv0, wv1, wv2) = taps(wg_ref), taps(wv_ref)
        bg, bv = bg_ref[...], bv_ref[...]
        g_up, g_dn = _shift_up(g, gp_ref[7:8, :] * has_prev), _shift_dn(g, gn_ref[0:1, :] * has_next)
        v_up, v_dn = _shift_up(v, vp_ref[7:8, :] * has_prev), _shift_dn(v, vn_ref[0:1, :] * has_next)
        dyg, dyv = dy_of(wg0 * g_up + wg1 * g + wg2 * g_dn + bg, wv0 * v_up + wv1 * v + wv2 * v_dn + bv, d)
        dyg_p, dyv_p = dy_of(wg0 * gp_ref[6:7, :] + wg1 * gp_ref[7:8, :] + wg2 * g_ref[0:1, :] + bg,
                             wv0 * vp_ref[6:7, :] + wv1 * vp_ref[7:8, :] + wv2 * v_ref[0:1, :] + bv, dp_ref[7:8, :])
        dyg_n, dyv_n = dy_of(wg0 * g_ref[R - 1:R, :] + wg1 * gn_ref[0:1, :] + wg2 * gn_ref[1:2, :] + bg,
                             wv0 * v_ref[R - 1:R, :] + wv1 * vn_ref[0:1, :] + wv2 * vn_ref[1:2, :] + bv, dn_ref[0:1, :])
        dug_ref[...] = (wg0 * _shift_dn(dyg, dyg_n * has_next) + wg1 * dyg
                        + wg2 * _shift_up(dyg, dyg_p * has_prev)).astype(BF16)
        duv_ref[...] = (wv0 * _shift_dn(dyv, dyv_n * has_next) + wv1 * dyv
                        + wv2 * _shift_up(dyv, dyv_p * has_prev)).astype(BF16)
        for ref, ups, mid, dns, dy in ((dwg_ref, g_up, g, g_dn, dyg), (dwv_ref, v_up, v, v_dn, dyv)):
            ref[0:1, :] += jnp.sum(ups * dy, axis=0, keepdims=True)
            ref[1:2, :] += jnp.sum(mid * dy, axis=0, keepdims=True)
            ref[2:3, :] += jnp.sum(dns * dy, axis=0, keepdims=True)
        dbg_ref[...] += jnp.sum(dyg, axis=0, keepdims=True)
        dbv_ref[...] += jnp.sum(dyv, axis=0, keepdims=True)

    gblk = pl.BlockSpec((R, W), lambda j, i: (i, j))
    vblk = pl.BlockSpec((R, W), lambda j, i: (i, nj + j))
    gprev, gnext = _halo_specs(R, W, T, lambda j, i: j)
    vprev, vnext = _halo_specs(R, W, T, lambda j, i: nj + j)
    half = jax.ShapeDtypeStruct((T, F), BF16)
    taps8, bias1 = jax.ShapeDtypeStruct((8, F), F32), jax.ShapeDtypeStruct((1, F), F32)
    return _pcall(
        body, name=name, grid=(nj, nblk), out_shape=(half, half, taps8, taps8, bias1, bias1),
        in_specs=[gblk, vblk, gprev, vprev, gnext, vnext, gblk, gprev, gnext,
                  pl.BlockSpec((3, W), lambda j, i: (0, j)), pl.BlockSpec((3, W), lambda j, i: (0, nj + j)),
                  pl.BlockSpec((1, W), lambda j, i: (0, j)), pl.BlockSpec((1, W), lambda j, i: (0, nj + j))],
        out_specs=(gblk, gblk, pl.BlockSpec((8, W), lambda j, i: (0, j)), pl.BlockSpec((8, W), lambda j, i: (0, j)),
                   pl.BlockSpec((1, W), lambda j, i: (0, j)), pl.BlockSpec((1, W), lambda j, i: (0, j))),
        args=(uf, uf, uf, uf, uf, uf, da, da, da, cw, cw, cb, cb), sem=("parallel", "arbitrary"), comm=comm)


WEIGHTS = ("c_ctx", "w_ada", "b_ada", "ln1_w", "ln2_w", "w_in", "hg_lb_logits", "hg_norm_w", "na_q_norm_w",
           "na_k_norm_w", "na_rpb", "na_out_norm_w", "cv_w", "cv_out_norm_w", "w_out", "w_up", "ffn_conv_w",
           "ffn_conv_b", "w_down")
BIG = ("w_ada", "w_in", "w_out", "w_up", "w_down")
SHARDED_SMALL = ("hg_lb_logits", "cv_w", "ffn_conv_w")


def _flat_rows(parts, dtype):
    flat, layout, off = [], [], 0
    for p in parts:
        layout.append((off, p.shape))
        flat.append(p.reshape(-1).astype(dtype))
        off += p.size
    pad = (-off) % (8 * LANE)
    if pad:
        flat.append(jnp.zeros((pad,), dtype))
    return jnp.concatenate(flat).reshape(-1, LANE), layout


def _unflat(buf, layout):
    v = buf.reshape(-1)
    return [v[off:off + int(np.prod(shape))].reshape(shape) for off, shape in layout]


def _lb_all(logits):
    sm = jax.nn.softmax(logits.astype(F32), axis=1)
    return jnp.cumsum(sm, axis=1) - sm[:, :1]


def _seg(ctx_vec, lat_vec):
    return jnp.broadcast_to(jnp.stack([ctx_vec, lat_vec])[:, None, :], (2, RT, ctx_vec.shape[0]))


def _shared(vec):
    return jnp.broadcast_to(vec[None, None, :], (1, RT, vec.shape[0]))


def kernel(x, c, ctx, c_ctx, w_ada, b_ada, ln1_w, ln2_w, w_in, hg_lb_logits, hg_norm_w, na_q_norm_w, na_k_norm_w, na_rpb, na_out_norm_w, cv_w, cv_out_norm_w, w_out, w_up, ffn_conv_w, ffn_conv_b, w_down, loss_target, m_c_ctx, m_w_ada, m_b_ada, m_ln1_w, m_ln2_w, m_w_in, m_hg_lb_logits, m_hg_norm_w, m_na_q_norm_w, m_na_k_norm_w, m_na_rpb, m_na_out_norm_w, m_cv_w, m_cv_out_norm_w, m_w_out, m_w_up, m_ffn_conv_w, m_ffn_conv_b, m_w_down, v_c_ctx, v_w_ada, v_b_ada, v_ln1_w, v_ln2_w, v_w_in, v_hg_lb_logits, v_hg_norm_w, v_na_q_norm_w, v_na_k_norm_w, v_na_rpb, v_na_out_norm_w, v_cv_w, v_cv_out_norm_w, v_w_out, v_w_up, v_ffn_conv_w, v_ffn_conv_b, v_w_down):
    W = dict(c_ctx=c_ctx, w_ada=w_ada, b_ada=b_ada, ln1_w=ln1_w, ln2_w=ln2_w, w_in=w_in, hg_lb_logits=hg_lb_logits,
             hg_norm_w=hg_norm_w, na_q_norm_w=na_q_norm_w, na_k_norm_w=na_k_norm_w, na_rpb=na_rpb,
             na_out_norm_w=na_out_norm_w, cv_w=cv_w, cv_out_norm_w=cv_out_norm_w, w_out=w_out, w_up=w_up,
             ffn_conv_w=ffn_conv_w, ffn_conv_b=ffn_conv_b, w_down=w_down)
    Mo = dict(c_ctx=m_c_ctx, w_ada=m_w_ada, b_ada=m_b_ada, ln1_w=m_ln1_w, ln2_w=m_ln2_w, w_in=m_w_in,
              hg_lb_logits=m_hg_lb_logits, hg_norm_w=m_hg_norm_w, na_q_norm_w=m_na_q_norm_w,
              na_k_norm_w=m_na_k_norm_w, na_rpb=m_na_rpb, na_out_norm_w=m_na_out_norm_w, cv_w=m_cv_w,
              cv_out_norm_w=m_cv_out_norm_w, w_out=m_w_out, w_up=m_w_up, ffn_conv_w=m_ffn_conv_w,
              ffn_conv_b=m_ffn_conv_b, w_down=m_w_down)
    Vo = dict(c_ctx=v_c_ctx, w_ada=v_w_ada, b_ada=v_b_ada, ln1_w=v_ln1_w, ln2_w=v_ln2_w, w_in=v_w_in,
              hg_lb_logits=v_hg_lb_logits, hg_norm_w=v_hg_norm_w, na_q_norm_w=v_na_q_norm_w,
              na_k_norm_w=v_na_k_norm_w, na_rpb=v_na_rpb, na_out_norm_w=v_na_out_norm_w, cv_w=v_cv_w,
              cv_out_norm_w=v_cv_out_norm_w, w_out=v_w_out, w_up=v_w_up, ffn_conv_w=v_ffn_conv_w,
              ffn_conv_b=v_ffn_conv_b, w_down=v_w_down)

    xi, yi, ci = _me()
    chip = 2 * xi + yi
    dev = 2 * chip + ci
    L, D = x.shape[1], x.shape[2]
    NC = ctx.shape[1]
    T = NC + L
    depth = w_in.shape[0]
    HGW, NAW, CVW = 4 * hg_lb_logits.shape[-1], na_out_norm_w.shape[-1], cv_out_norm_w.shape[-1]
    MIX = HGW + NAW + CVW
    INW, FF2 = 4 * w_in.shape[-1], 4 * w_up.shape[-1]
    F = FF2 // 2
    ADA = 4 * w_ada.shape[-1]
    assert NAW == 2 * HGW and INW == 5 * HGW + 3 * NAW + 3 * CVW and ADA == 6 * D and NC % 128 == 0
    assert L % GRID_W == 0 and T % CHUNK == 0 and depth == 2
    R = math.gcd(NC, 256)
    FW = 512 if F % 512 == 0 else LANE
    rows = L // GRID_W
    nh_hg, nh_na, nh_cv = HGW // LANE, NAW // LANE, CVW // LANE
    kcol = 3 * nh_hg
    vcol = kcol + nh_na
    gcol = vcol + nh_na + nh_hg
    qcol = gcol + nh_hg
    bcol = qcol + nh_na
    mix_na, mix_cv = nh_hg, nh_hg + nh_na

    small1, lay1 = _flat_rows([c[0], hg_lb_logits, cv_w, ffn_conv_w], F32)
    g1 = allgather8([small1], "gather_cond")[0]
    per_dev = [_unflat(g1[d], lay1) for d in range(8)]
    c_all = jnp.stack([p[0] for p in per_dev])
    lb_logits = jnp.concatenate([per_dev[2 * s][1] for s in range(4)], axis=-1)
    cvw_full = jnp.concatenate([per_dev[2 * s][2] for s in range(4)], axis=-1)
    fcw_full = jnp.concatenate([per_dev[2 * s][3] for s in range(4)], axis=-1)
    lb_all, lb_pull = jax.vjp(_lb_all, lb_logits)

    a16 = jnp.concatenate([c_all, c_ctx[None], jnp.zeros((7, D), F32)])
    s16 = _silu(a16)
    wcols = ADA // 4
    b_mine = lax.dynamic_slice_in_dim(b_ada, chip * wcols, wcols, axis=1)
    p_ada = jnp.stack([mm_nn(s16, w_ada[l][None], F32, f"ada_fwd_{l}") + b_mine[l][None] for l in range(depth)])
    g2 = allgather8([p_ada.reshape(depth * 16, wcols)], "gather_ada")[0].reshape(8, depth, 16, wcols)
    ada_rows = jnp.concatenate([g2[2 * s] for s in range(4)], axis=-1)
    ada = lax.dynamic_index_in_dim(ada_rows, dev, axis=1, keepdims=False)
    ada_c = ada_rows[:, 8]

    def half_rows(a):
        h = a.shape[0] // 2
        return lax.dynamic_slice_in_dim(a, ci * h, h, axis=0)

    proj = ("w_in", "w_out", "w_up", "w_down")
    wparts = [{n: half_rows(W[n][l]).astype(BF16) for n in proj} for l in range(depth)]

    def stacked(n, g):
        g = g.reshape(4, -1, g.shape[-1])
        return g.reshape(1, -1, g.shape[-1]) if n in ("w_out", "w_down") else g

    Wg = [{"w_in": stacked("w_in", allgather8([wparts[0]["w_in"]], "gather_w_in_0", hbm=True)[0])}, {}]
    fetch_plan = {"proj_in_0": [(0, "w_out"), (0, "w_down")], "na_fwd_0": [(0, "w_up")],
                  "ffn_up_0": [(1, "w_in"), (1, "w_out")], "ffn_mid_0": [(1, "w_up")], "ffn_down_0": [(1, "w_down")]}

    def fetching(name, call):
        items = fetch_plan.get(name, [])
        res = call(allgather8_comm([wparts[l][n] for l, n in items]) if items else None)
        res = list(res) if isinstance(res, (list, tuple)) else [res]
        for (l, n), g in zip(items, res[len(res) - len(items):]):
            Wg[l][n] = stacked(n, g)
        own = res[:len(res) - len(items)]
        return own[0] if len(own) == 1 else own

    xcat = jnp.concatenate([ctx[0], x[0]], axis=0)
    mods = []
    for l in range(depth):
        lat, con = jnp.split(ada[l], 6), jnp.split(ada_c[l], 6)
        mods.append(dict(sh1=_seg(con[0], lat[0]), sc1=_seg(con[1], lat[1]), g1=_seg(con[2], lat[2]),
                         sh2=_seg(con[3], lat[3]), sc2=_seg(con[4], lat[4]), g2=_seg(con[5], lat[5]),
                         ln1=_shared(ln1_w[l]), ln2=_shared(ln2_w[l])))
    bias_pull, saved = [], []
    x0 = xcat
    _, h = gate_norm(x0, None, None, mods[0]["ln1"], mods[0]["sh1"], mods[0]["sc1"], NC, R, "norm_in")
    for l in range(depth):
        md, wl = mods[l], Wg[l]
        u = fetching(f"proj_in_{l}", lambda cm: mm_nn(h, wl["w_in"], F32, f"proj_in_{l}", comm=cm))
        lbf, lbb = lb_all[0, l][None], lb_all[1, l][None]
        o_fw, st_fw = hgrn_fwd(u, lbf, False, 0, NC, HGW, f"hgrn_fw_{l}")
        o_bw, st_bw = hgrn_fwd(u, lbb, True, 1, NC, HGW, f"hgrn_bw_{l}")
        hgn = hg_norm_w[l][None]
        hg = hg_read(o_fw, o_bw, u, hgn, gcol, R, f"hg_read_{l}")
        bias, pull = jax.vjp(lambda r: na_bias_tables(r, rows), na_rpb[l])
        bias_pull.append(pull)
        qn, kn, on = na_q_norm_w[l][None], na_k_norm_w[l][None], na_out_norm_w[l][None]
        keys_n, vals_b = kv_prep(u, kn, kcol, vcol, NAW, R, f"kv_prep_{l}")
        na = fetching(f"na_fwd_{l}", lambda cm: na_fwd(u, keys_n, vals_b, qn, on, bias, qcol, NC, f"na_fwd_{l}", comm=cm))
        cvw_l, cvo = cvw_full[l], cv_out_norm_w[l][None]
        cv = short_conv(u, cvw_l, cvo, bcol, NC, CVW, R, name=f"short_conv_{l}")
        mix = jnp.concatenate([hg, na, cv], axis=1)
        m1 = mm_nn(mix, wl["w_out"], F32, f"proj_out_{l}")
        x1, h2 = gate_norm(x0, m1, md["g1"], md["ln2"], md["sh2"], md["sc2"], NC, R, f"gate_norm_mid_{l}")
        uf = fetching(f"ffn_up_{l}", lambda cm: mm_nn(h2, wl["w_up"], F32, f"ffn_up_{l}", comm=cm))
        fcw_l, fcb_l = fcw_full[l], ffn_conv_b[l][None]
        a = fetching(f"ffn_mid_{l}", lambda cm: ffn_mid(uf, fcw_l, fcb_l, NC, R, FW, f"ffn_mid_{l}", comm=cm))
        m2 = fetching(f"ffn_down_{l}", lambda cm: mm_nn(a, wl["w_down"], F32, f"ffn_down_{l}", comm=cm))
        saved.append(dict(x0=x0, h=h, u=u, o_fw=o_fw, o_bw=o_bw, st_fw=st_fw, st_bw=st_bw, bias=bias, mix=mix,
                          m1=m1, x1=x1, h2=h2, uf=uf, a=a, m2=m2, lbf=lbf, lbb=lbb, keys_n=keys_n, vals_b=vals_b))
        if l + 1 < depth:
            nx = mods[l + 1]
            x0, h = gate_norm(x1, m2, md["g2"], nx["ln1"], nx["sh1"], nx["sc1"], NC, R, f"gate_norm_end_{l}")
    sv, md = saved[-1], mods[-1]
    loss_terms, d_x1, d_m2, d_g2 = gate_loss(sv["x1"], sv["m2"], md["g2"], loss_target[0], NC, R, "gate_loss")
    loss = lax.psum(jnp.sum(loss_terms), ("x", "y", "c"))

    big_grads = [dict() for _ in range(depth)]
    core = ci.astype(jnp.int32).reshape(1)
    pairs, quads = {}, {}
    reduce_plan = {"ffn_down_bwd_0": [(1, "w_down")], "ffn_mid_bwd_0": [(1, "w_up")],
                   "ffn_up_bwd_0": [(1, "w_in"), (1, "w_out")], "na_bwd_0": [(0, "w_down"), (0, "w_up"), (0, "w_out")]}

    def stage(l, names):
        parts = [big_grads[l][n] for n in names]
        got = swap_halves(parts, f"reduce_sibling_{l}_{names[0]}")
        for n, p, g in zip(names, parts, got):
            pairs[(l, n)] = pair_sum(p, g, core, f"reduce_pair_sum_{n}_{l}")

    def reducing(name, call):
        items = reduce_plan.get(name, [])
        res = call(chip_alltoall_comm([pairs[k] for k in items]) if items else None)
        res = list(res) if isinstance(res, (list, tuple)) else [res]
        for k, q in zip(items, res[len(res) - len(items):]):
            quads[k] = q
        own = res[:len(res) - len(items)]
        return own[0] if len(own) == 1 else own

    small = [dict() for _ in range(depth)]
    d_ada = [None] * depth
    d_lb = [None] * depth
    for l in reversed(range(depth)):
        sv, md, wl = saved[l], mods[l], Wg[l]
        u, uf = sv["u"], sv["uf"]
        big_grads[l]["w_down"] = mm_tn(sv["a"], d_m2, 1, BF16, f"grad_w_down_{l}").reshape(4, F // 4, D)
        d_a = reducing(f"ffn_down_bwd_{l}", lambda cm: mm_nt(d_m2, wl["w_down"], F32, f"ffn_down_bwd_{l}", comm=cm))
        fcw_l, fcb_l = fcw_full[l], ffn_conv_b[l][None]
        dug, duv, dwg, dwv, dbg, dbv = reducing(
            f"ffn_mid_bwd_{l}", lambda cm: ffn_mid_bwd(uf, fcw_l, fcb_l, d_a, NC, R, FW, f"ffn_mid_bwd_{l}", comm=cm))
        d_uf = jnp.concatenate([dug, duv], axis=1)
        small[l]["ffn_conv_w"] = jnp.concatenate([dwg[:3], dwv[:3]], axis=1)
        small[l]["ffn_conv_b"] = jnp.concatenate([dbg[0], dbv[0]])
        big_grads[l]["w_up"] = mm_tn(sv["h2"], d_uf, 4, BF16, f"grad_w_up_{l}")
        d_h2 = reducing(f"ffn_up_bwd_{l}", lambda cm: mm_nt(d_uf, wl["w_up"], F32, f"ffn_up_bwd_{l}", comm=cm))
        d_x0, d_m1, dg1, dln2, dsh2, dsc2 = gate_norm_bwd(sv["x0"], sv["m1"], md["g1"], md["ln2"], md["sh2"], md["sc2"],
                                                          d_x1, d_h2, NC, R, f"gate_norm_mid_bwd_{l}")
        big_grads[l]["w_out"] = mm_tn(sv["mix"], d_m1, 1, BF16, f"grad_w_out_{l}").reshape(4, MIX // 4, D)
        if l == 0:
            stage(0, ("w_down", "w_up", "w_out"))
        d_mix = mm_nt(d_m1, wl["w_out"], F32, f"proj_out_bwd_{l}")
        hgn = hg_norm_w[l][None]
        d_o, d_hgg, d_hgn = hg_read_bwd(sv["o_fw"], sv["o_bw"], u, hgn, d_mix, gcol, 0, R, f"hg_read_bwd_{l}")
        dzf, dvf, dqf, dlbf = hgrn_bwd(u, sv["lbf"], sv["st_fw"], d_o, False, 0, NC, HGW, f"hgrn_fw_bwd_{l}")
        dzb, dvb, dqb, dlbb = hgrn_bwd(u, sv["lbb"], sv["st_bw"], d_o, True, 1, NC, HGW, f"hgrn_bw_bwd_{l}")
        d_lb[l] = (dlbf[0], dlbb[0])
        qn, kn, on = na_q_norm_w[l][None], na_k_norm_w[l][None], na_out_norm_w[l][None]
        d_nq, d_keys_n, d_nv, d_bias, d_qn, d_on = reducing(
            f"na_bwd_{l}", lambda cm: na_bwd(u, sv["keys_n"], sv["vals_b"], qn, on, sv["bias"], d_mix, qcol, mix_na, NC,
                                            f"na_bwd_{l}", comm=cm))
        d_nk, d_kn = kv_prep_bwd(u, kn, d_keys_n, kcol, NAW, R, f"kv_prep_bwd_{l}")
        cvw_l, cvo = cvw_full[l], cv_out_norm_w[l][None]
        d_cb, d_cy, d_cvo = short_conv(u, cvw_l, cvo, bcol, NC, CVW, R, bwd_dout=d_mix, ocol=mix_cv,
                                       name=f"short_conv_bwd_{l}")
        gcv = _head_group(nh_cv, bcol + nh_cv, bcol + 2 * nh_cv)
        d_cc, d_cvv, d_cvw = conv3_bwd(d_cy, u, cvw_l, NC, R, gcv * LANE,
                                       prod_cols=((bcol + nh_cv) // gcv, (bcol + 2 * nh_cv) // gcv),
                                       name=f"short_conv_taps_bwd_{l}")
        d_u = jnp.concatenate([dzf, dzb, dvf + dvb, d_nk, d_nv, dqf + dqb, d_hgg, d_nq, d_cb, d_cc, d_cvv],
                              axis=1).astype(BF16)
        big_grads[l]["w_in"] = mm_tn(sv["h"], d_u, 4, BF16, f"grad_w_in_{l}")
        d_h = mm_nt(d_u, wl["w_in"], F32, f"proj_in_bwd_{l}")
        small[l].update(hg_norm_w=d_hgn.sum(0)[0], na_q_norm_w=d_qn.sum(0)[0], na_k_norm_w=d_kn.sum(0)[0],
                        na_out_norm_w=d_on.reshape(-1), na_rpb=bias_pull[l](d_bias)[0], cv_w=d_cvw[:3],
                        cv_out_norm_w=d_cvo.reshape(-1), ln2_w=dln2.sum((0, 1)))
        if l > 0:
            pv, pm = saved[l - 1], mods[l - 1]
            d_x1, d_m2, dg2_prev, dln1, dsh1, dsc1 = gate_norm_bwd(pv["x1"], pv["m2"], pm["g2"], md["ln1"], md["sh1"],
                                                                   md["sc1"], d_x0, d_h, NC, R, f"gate_norm_end_bwd_{l - 1}")
        else:
            d_xin, _, _, dln1, dsh1, dsc1 = gate_norm_bwd(sv["x0"], None, None, md["ln1"], md["sh1"], md["sc1"], d_x0, d_h,
                                                          NC, R, "norm_in_bwd")
        small[l]["ln1_w"] = dln1.sum((0, 1))
        this_g2 = d_g2
        vecs = [v.sum(1) for v in (dsh1, dsc1, dg1, dsh2, dsc2, this_g2)]
        d_ada[l] = jnp.stack([jnp.concatenate([v[s] for v in vecs]) for s in (0, 1)])
        if l > 0:
            d_g2 = dg2_prev
        stage(l, proj if l > 0 else ("w_in",))
    grad_x = d_xin[NC:][None]
    d_logits = lb_pull(jnp.stack([jnp.stack([d_lb[l][k] for l in range(depth)]) for k in (0, 1)]))[0]

    rep_names = ("ln1_w", "ln2_w", "hg_norm_w", "na_q_norm_w", "na_k_norm_w", "na_rpb", "na_out_norm_w",
                 "cv_out_norm_w", "ffn_conv_b", "cv_w", "ffn_conv_w")
    parts3 = [jnp.stack([small[l][n] for l in range(depth)]) for n in rep_names]
    parts3 += [d_logits, jnp.stack([d_ada[l][0] for l in range(depth)]), jnp.stack([d_ada[l][1] for l in range(depth)])]
    buf3, lay3 = _flat_rows(parts3, F32)
    g3 = allgather8([buf3], "gather_small_grads")[0]
    tot3 = _unflat(sum_leading(g3, F32, "sum_small_grads"), lay3)
    gsm = dict(zip(rep_names, tot3[:len(rep_names)]))
    gsm["hg_lb_logits"] = tot3[len(rep_names)]
    dctx_tot, dlat_tot = tot3[-2], tot3[-1]
    dlat_each = jnp.stack([_unflat(g3[d], lay3)[-1] for d in range(8)], axis=1)
    grads = {n: gsm[n].reshape(W[n].shape) for n in rep_names if n not in SHARDED_SMALL}
    for n in SHARDED_SMALL:
        wl_ = W[n].shape[-1]
        grads[n] = lax.dynamic_slice_in_dim(gsm[n], chip * wl_, wl_, axis=gsm[n].ndim - 1)
    grads["b_ada"] = dctx_tot + dlat_tot

    ds16 = jnp.zeros((16, D), F32)
    gw_ada = []
    for l in range(depth):
        dm = jnp.concatenate([dlat_each[l], dctx_tot[l][None], jnp.zeros((7, ADA), F32)])
        dm = lax.dynamic_slice_in_dim(dm, chip * wcols, wcols, axis=1)
        gw_ada.append(mm_tn(s16, dm, 1, F32, f"grad_w_ada_{l}")[0])
        ds16 = ds16 + mm_nt(dm, w_ada[l][None], F32, f"ada_bwd_{l}")
    g4 = allgather8([ds16[8:16]], "gather_cond_grad")[0]
    d_scc = g4[0, 0] + g4[2, 0] + g4[4, 0] + g4[6, 0]
    sg = jax.nn.sigmoid(c_ctx)
    grads["c_ctx"] = d_scc * (sg * (1.0 + c_ctx * (1.0 - sg)))

    keys = [(l, n) for l in range(depth) for n in proj]
    quads[(0, "w_in")] = chip_alltoall([pairs[(0, "w_in")]], "reduce_chips_w_in_0")[0]
    mine = [sum_leading(quads[(l, n)], F32, f"reduce_chip_sum_{n}_{l}") for l, n in keys]
    other = share_halves(mine, "reduce_share")
    mine_by, other_by = {n: [None] * depth for n in proj}, {n: [None] * depth for n in proj}
    for (l, n), a, b in zip(keys, mine, other):
        mine_by[n][l], other_by[n][l] = a, b

    delta, new_m, new_v = {}, {}, {}
    for n in BIG:
        shp = W[n].shape
        two = lambda a: a.reshape(-1, shp[-1])
        if n == "w_ada":
            g_, d_, m_, v_ = adamw(two(W[n]), gw_ada, two(Mo[n]), two(Vo[n]), f"adamw_{n}")
        else:
            g_, d_, m_, v_ = adamw_halves(two(W[n]), mine_by[n], other_by[n], two(Mo[n]), two(Vo[n]), core, f"adamw_{n}")
        grads[n], delta[n], new_m[n], new_v[n] = g_.reshape(shp), d_.reshape(shp), m_.reshape(shp), v_.reshape(shp)
    smalls = [n for n in WEIGHTS if n not in BIG]
    pw, lay_s = _flat_rows([W[n] for n in smalls], F32)
    pg, _ = _flat_rows([grads[n] for n in smalls], F32)
    pm, _ = _flat_rows([Mo[n] for n in smalls], F32)
    pvv, _ = _flat_rows([Vo[n] for n in smalls], F32)
    _, d_, m_, v_ = adamw(pw, [pg], pm, pvv, "adamw_small")
    for n, dd, mm_, vv in zip(smalls, _unflat(d_, lay_s), _unflat(m_, lay_s), _unflat(v_, lay_s)):
        delta[n], new_m[n], new_v[n] = dd, mm_, vv

    return (loss, grad_x, *[grads[n] for n in WEIGHTS], *[delta[n] for n in WEIGHTS],
            *[new_m[n] for n in WEIGHTS], *[new_v[n] for n in WEIGHTS])
```

```python
import functools
import math

import numpy as np
import jax
import jax.numpy as jnp
from jax import lax
from jax.experimental import pallas as pl
from jax.experimental.pallas import tpu as pltpu

F32 = jnp.float32
BF16 = jnp.bfloat16
MESH = pl.DeviceIdType.MESH
ANY = pl.BlockSpec(memory_space=pl.ANY)
VMEM_SPEC = pl.BlockSpec(memory_space=pltpu.VMEM)

LANE = 128
CHUNK = 64
SUB = 16
GRID_W = 64
WIN_R = 8
WIN_C = 16
EPS = 1e-6
F_FLOOR = 1e-30
NEG_INF = -1e30
EXP_CLAMP = 80.0
ATTN_SCALE = LANE ** -0.5
VMEM_LIMIT = 56 * 1024 * 1024
ADAM_LR, ADAM_B1, ADAM_B2, ADAM_EPS, ADAM_WD, ADAM_STEP = 0.001, 0.9, 0.999, 1e-08, 0.01, 10


def _cp(*sem):
    return pltpu.CompilerParams(dimension_semantics=sem or None, vmem_limit_bytes=VMEM_LIMIT)


def _me():
    return lax.axis_index("x"), lax.axis_index("y"), lax.axis_index("c")


def allgather8(blocks, name, hbm=False):
    na = len(blocks)
    comm = allgather8_comm(blocks)

    def body(*refs):
        comm["start"](refs[:na], refs[na:2 * na], refs[2 * na:])
        comm["finish"](refs[:na], refs[na:2 * na], refs[2 * na:])

    spec = ANY if hbm else VMEM_SPEC
    return pl.pallas_call(
        body, name=name, out_shape=comm["outs"], in_specs=[spec] * na, out_specs=[spec] * na,
        scratch_shapes=comm["scratch"], compiler_params=pltpu.CompilerParams(vmem_limit_bytes=VMEM_LIMIT),
    )(*blocks)


def allgather8_comm(blocks):
    na = len(blocks)

    def parts(x_refs, out_refs, sems):
        send_sems, recv_sems, local_sems = sems
        x, y, c = _me()
        me, sibling = (x, y, c), (x, y, 1 - c)
        chips = [(1 - x, y), (x, 1 - y), (1 - x, 1 - y)]

        def rows(a, px, py, pc):
            return out_refs[a].at[4 * px + 2 * py + pc]

        def copy(a, k, blk, to, src=None):
            return pltpu.make_async_remote_copy(
                src_ref=rows(a, *blk) if src is None else src, dst_ref=rows(a, *blk),
                send_sem=send_sems.at[a, k], recv_sem=recv_sems.at[a, k], device_id=to, device_id_type=MESH)

        mine = [pltpu.make_async_copy(x_refs[a], rows(a, *me), local_sems.at[a]) for a in range(na)]
        first = []
        for a in range(na):
            first.append(copy(a, 0, me, sibling, src=x_refs[a]))
            first += [copy(a, 1 + j, me, (*chip, c), src=x_refs[a]) for j, chip in enumerate(chips)]
        return c, me, sibling, chips, copy, mine, first

    def start(x_refs, out_refs, sems):
        _, _, _, _, _, mine, first = parts(x_refs, out_refs, sems)
        for cp in mine + first:
            cp.start()

    def finish(x_refs, out_refs, sems):
        c, me, sibling, chips, copy, mine, first = parts(x_refs, out_refs, sems)
        passed = []
        for j, chip in enumerate(chips):
            for a in range(na):
                copy(a, 1 + j, (*chip, c), me).wait_recv()
                passed.append(copy(a, 4 + j, (*chip, c), sibling))
                passed[-1].start()
        for a in range(na):
            copy(a, 0, sibling, me).wait_recv()
            for j, chip in enumerate(chips):
                copy(a, 4 + j, (*chip, 1 - c), me).wait_recv()
        for cp in first + passed:
            cp.wait_send()
        for cp in mine:
            cp.wait()

    return dict(ins=list(blocks), outs=[jax.ShapeDtypeStruct((8,) + b.shape, b.dtype) for b in blocks],
                scratch=[pltpu.SemaphoreType.DMA((na, 7)), pltpu.SemaphoreType.DMA((na, 7)),
                         pltpu.SemaphoreType.DMA((na,))], start=start, finish=finish)


def swap_halves(gs, name):
    return _run_comm(swap_halves_comm(gs), name)


def _run_comm(comm, name):
    na = len(comm["ins"])

    def body(*refs):
        comm["start"](refs[:na], refs[na:2 * na], refs[2 * na:])
        comm["finish"](refs[:na], refs[na:2 * na], refs[2 * na:])

    return pl.pallas_call(body, name=name, out_shape=comm["outs"], in_specs=[ANY] * na, out_specs=[ANY] * na,
                          scratch_shapes=comm["scratch"])(*comm["ins"])


def swap_halves_comm(gs):
    na = len(gs)
    hrs = [g.shape[1] // 2 for g in gs]

    def copies(g_refs, o_refs, sems):
        send_sems, recv_sems = sems
        x, y, c = _me()
        cps = []
        for a in range(na):
            for s in range(4):
                src = g_refs[a].at[s, pl.ds(pl.multiple_of((1 - c) * hrs[a], 16), hrs[a]), :]
                cps.append(pltpu.make_async_remote_copy(
                    src_ref=src, dst_ref=o_refs[a].at[s], send_sem=send_sems.at[a, s], recv_sem=recv_sems.at[a, s],
                    device_id=(x, y, 1 - c), device_id_type=MESH))
        return cps

    def start(g_refs, o_refs, sems):
        for cp in copies(g_refs, o_refs, sems):
            cp.start()

    def finish(g_refs, o_refs, sems):
        for cp in copies(g_refs, o_refs, sems):
            cp.wait()

    return dict(ins=list(gs), outs=[jax.ShapeDtypeStruct((4, hrs[a], gs[a].shape[2]), gs[a].dtype) for a in range(na)],
                scratch=[pltpu.SemaphoreType.DMA((na, 4)), pltpu.SemaphoreType.DMA((na, 4))], start=start, finish=finish)


def chip_alltoall(gs, name):
    na = len(gs)
    comm = chip_alltoall_comm(gs)

    def body(*refs):
        comm["start"](refs[:na], refs[na:2 * na], refs[2 * na:])
        comm["finish"](refs[:na], refs[na:2 * na], refs[2 * na:])

    return pl.pallas_call(body, name=name, out_shape=comm["outs"], in_specs=[ANY] * na, out_specs=[ANY] * na,
                          scratch_shapes=comm["scratch"])(*gs)


def chip_alltoall_comm(gs):
    na = len(gs)

    def copies(g_refs, o_refs, sems):
        send_sems, recv_sems, local_sems = sems
        x, y, c = _me()
        mine = 2 * x + y
        cps = []
        for a in range(na):
            cps.append(pltpu.make_async_copy(g_refs[a].at[mine], o_refs[a].at[mine], local_sems.at[a]))
            for k, (px, py) in enumerate([(1 - x, y), (x, 1 - y), (1 - x, 1 - y)]):
                cps.append(pltpu.make_async_remote_copy(
                    src_ref=g_refs[a].at[2 * px + py], dst_ref=o_refs[a].at[mine], send_sem=send_sems.at[a, k],
                    recv_sem=recv_sems.at[a, k], device_id=(px, py, c), device_id_type=MESH))
        return cps

    def start(g_refs, o_refs, sems):
        for cp in copies(g_refs, o_refs, sems):
            cp.start()

    def finish(g_refs, o_refs, sems):
        for cp in copies(g_refs, o_refs, sems):
            cp.wait()

    return dict(ins=list(gs), outs=[jax.ShapeDtypeStruct(g.shape, g.dtype) for g in gs],
                scratch=[pltpu.SemaphoreType.DMA((na, 3)), pltpu.SemaphoreType.DMA((na, 3)),
                         pltpu.SemaphoreType.DMA((na,))], start=start, finish=finish)


def share_halves_comm(vs):
    na = len(vs)

    def copies(v_refs, o_refs, sems):
        send_sems, recv_sems = sems
        x, y, c = _me()
        return [pltpu.make_async_remote_copy(
            src_ref=v_refs[a], dst_ref=o_refs[a], send_sem=send_sems.at[a], recv_sem=recv_sems.at[a],
            device_id=(x, y, 1 - c), device_id_type=MESH) for a in range(na)]

    def start(v_refs, o_refs, sems):
        for cp in copies(v_refs, o_refs, sems):
            cp.start()

    def finish(v_refs, o_refs, sems):
        for cp in copies(v_refs, o_refs, sems):
            cp.wait()

    return dict(ins=list(vs), outs=[jax.ShapeDtypeStruct(v.shape, v.dtype) for v in vs],
                scratch=[pltpu.SemaphoreType.DMA((na,)), pltpu.SemaphoreType.DMA((na,))], start=start, finish=finish)


def _row_block(rows, cap):
    rb = math.gcd(rows, cap)
    return rb if rb % 8 == 0 else rows


def sum_leading(x, out_dtype, name):
    n, r, c = x.shape
    rb = _row_block(r, 1024)

    def body(x_ref, o_ref):
        acc = x_ref[0].astype(F32)
        for k in range(1, n):
            acc = acc + x_ref[k].astype(F32)
        o_ref[...] = acc.astype(o_ref.dtype)

    return pl.pallas_call(
        body, name=name, grid=(r // rb,), out_shape=jax.ShapeDtypeStruct((r, c), out_dtype),
        in_specs=[pl.BlockSpec((n, rb, c), lambda i: (0, i, 0))], out_specs=pl.BlockSpec((rb, c), lambda i: (i, 0)),
        compiler_params=_cp("parallel"),
    )(x)


def pair_sum(g, got, core, name):
    _, r2, n = g.shape
    hr = r2 // 2
    rb = math.gcd(hr, 512)
    nb = hr // rb

    def body(c_ref, a_ref, b_ref, o_ref):
        o_ref[...] = (a_ref[...].astype(F32) + b_ref[...].astype(F32)).astype(o_ref.dtype)

    spec = pl.BlockSpec((None, rb, n), lambda s, i, c_ref: (s, i, 0))
    return pl.pallas_call(
        body, name=name, out_shape=jax.ShapeDtypeStruct((4, hr, n), g.dtype),
        grid_spec=pltpu.PrefetchScalarGridSpec(
            num_scalar_prefetch=1, grid=(4, nb),
            in_specs=[pl.BlockSpec((None, rb, n), lambda s, i, c_ref: (s, c_ref[0] * nb + i, 0)), spec],
            out_specs=spec),
        compiler_params=_cp("parallel", "parallel"),
    )(core, g, got)


def adamw(w, gs, m, v, name, comm=None):
    rows, c = w.shape
    ng = len(gs)
    r = rows // ng
    rb = _row_block(r, 128 if c > 2048 else 256 if c > 1024 else 1024)
    nb = r // rb
    bc1 = 1.0 - ADAM_B1 ** ADAM_STEP
    bc2 = 1.0 - ADAM_B2 ** ADAM_STEP

    def body(w_ref, *refs):
        g_refs, (m_ref, v_ref, g_out, d_ref, nm_ref, nv_ref) = refs[:ng], refs[ng:]
        part = pl.program_id(0) // nb
        gg = g_refs[0][...]
        for k in range(1, ng):
            gg = jnp.where(part == k, g_refs[k][...], gg)
        nm = ADAM_B1 * m_ref[...] + (1.0 - ADAM_B1) * gg
        nv = ADAM_B2 * v_ref[...] + (1.0 - ADAM_B2) * (gg * gg)
        g_out[...] = gg
        d_ref[...] = -ADAM_LR * ((nm / bc1) / (jnp.sqrt(nv / bc2) + ADAM_EPS) + ADAM_WD * w_ref[...])
        nm_ref[...] = nm
        nv_ref[...] = nv

    spec = pl.BlockSpec((rb, c), lambda i: (i, 0))
    gspecs = [pl.BlockSpec((rb, c), functools.partial(lambda k, i: (jnp.clip(i - k * nb, 0, nb - 1), 0), k))
              for k in range(ng)]
    sds = jax.ShapeDtypeStruct((rows, c), F32)
    return _pcall(body, name=name, grid=(ng * nb,), out_shape=(sds,) * 4, in_specs=[spec] + gspecs + [spec, spec],
                  out_specs=(spec,) * 4, args=(w, *gs, m, v), sem=("parallel",), comm=comm)


def adamw_halves(w, mine, other, m, v, core, name):
    rows, c = w.shape
    nl = len(mine)
    hr = rows // (2 * nl)
    rb = _row_block(hr, 128 if c > 2048 else 256 if c > 1024 else 1024)
    nb = hr // rb
    bc1 = 1.0 - ADAM_B1 ** ADAM_STEP
    bc2 = 1.0 - ADAM_B2 ** ADAM_STEP

    def body(c_ref, w_ref, *refs):
        mine_refs, other_refs = refs[:nl], refs[nl:2 * nl]
        m_ref, v_ref, g_out, d_ref, nm_ref, nv_ref = refs[2 * nl:]
        part = pl.program_id(0) // nb
        layer, half = part // 2, part % 2
        gg = jnp.where(half == c_ref[0], mine_refs[0][...], other_refs[0][...])
        for k in range(1, nl):
            gg = jnp.where(layer == k, jnp.where(half == c_ref[0], mine_refs[k][...], other_refs[k][...]), gg)
        nm = ADAM_B1 * m_ref[...] + (1.0 - ADAM_B1) * gg
        nv = ADAM_B2 * v_ref[...] + (1.0 - ADAM_B2) * (gg * gg)
        g_out[...] = gg
        d_ref[...] = -ADAM_LR * ((nm / bc1) / (jnp.sqrt(nv / bc2) + ADAM_EPS) + ADAM_WD * w_ref[...])
        nm_ref[...] = nm
        nv_ref[...] = nv

    spec = pl.BlockSpec((rb, c), lambda i, c_ref: (i, 0))
    gspecs = [pl.BlockSpec((rb, c), functools.partial(
        lambda k, i, c_ref: (jnp.clip(i - 2 * k * nb, 0, 2 * nb - 1) % nb, 0), k)) for k in range(nl)]
    sds = jax.ShapeDtypeStruct((rows, c), F32)
    return pl.pallas_call(
        body, name=name, out_shape=(sds,) * 4,
        grid_spec=pltpu.PrefetchScalarGridSpec(
            num_scalar_prefetch=1, grid=(2 * nl * nb,), in_specs=[spec] + gspecs + gspecs + [spec, spec],
            out_specs=(spec,) * 4),
        compiler_params=_cp("parallel"),
    )(core, w, *mine, *other, m, v)


def _pick(n, prefs):
    for p in prefs:
        if n % p == 0:
            return p
    return n


def _hosted(body, n_in, n_out, comm, first, last):
    if comm is None:
        return body
    k, ns = len(comm["ins"]), len(comm["scratch"])

    def wrapped(*refs):
        ins, cins = refs[:n_in], refs[n_in:n_in + k]
        outs, couts = refs[n_in + k:n_in + k + n_out], refs[n_in + k + n_out:n_in + 2 * k + n_out]
        rest = refs[n_in + 2 * k + n_out:]
        scratch, sems = rest[:len(rest) - ns], rest[len(rest) - ns:]

        @pl.when(first())
        def _():
            comm["start"](cins, couts, sems)

        body(*ins, *outs, *scratch)

        @pl.when(last())
        def _():
            comm["finish"](cins, couts, sems)

    return wrapped


def _comm_extras(comm):
    if comm is None:
        return [], [], [], []
    return list(comm["ins"]), [ANY] * len(comm["ins"]), list(comm["outs"]), list(comm["scratch"])


def _mm_body(dims, nk, out_dtype):
    def body(a_ref, b_ref, o_ref, acc=None):
        kk = pl.program_id(2)
        part = lax.dot_general(a_ref[...].astype(BF16), b_ref[...].astype(BF16), (dims, ((), ())),
                               preferred_element_type=F32)
        if nk == 1:
            o_ref[...] = part.astype(out_dtype)
        else:
            @pl.when(kk == 0)
            def _():
                acc[...] = part

            @pl.when(kk > 0)
            def _():
                acc[...] += part

            @pl.when(kk == nk - 1)
            def _():
                o_ref[...] = acc[...].astype(out_dtype)
    return body


def _acc(nk, shape):
    return [pltpu.VMEM(shape, F32)] if nk > 1 else []


def mm_nn(a, w, out_dtype, name, comm=None):
    M, K = a.shape
    S, _, Ns = w.shape
    tm = _pick(M, (1088, 1024, 512, 256, 128))
    tn = _pick(Ns, (1024, 896, 1408, 512, 256, 128))
    tk = _pick(K, (2816, 2048, 1408, 1024, 512, 256, 128))
    nps, nk = Ns // tn, K // tk
    grid = (S * nps, M // tm, nk)
    ids = lambda: [pl.program_id(d) for d in range(3)]
    first = lambda: functools.reduce(jnp.logical_and, [p == 0 for p in ids()])
    last = lambda: functools.reduce(jnp.logical_and, [p == g - 1 for p, g in zip(ids(), grid)])
    cin, cspec, cout, csem = _comm_extras(comm)
    out = pl.pallas_call(
        _hosted(_mm_body(((1,), (0,)), nk, out_dtype), 2, 1, comm, first, last), name=name, grid=grid,
        out_shape=[jax.ShapeDtypeStruct((M, S * Ns), out_dtype)] + cout,
        in_specs=[pl.BlockSpec((tm, tk), lambda j, i, k: (i, k)),
                  pl.BlockSpec((None, tk, tn), lambda j, i, k: (j // nps, k, j % nps))] + cspec,
        out_specs=[pl.BlockSpec((tm, tn), lambda j, i, k: (i, j))] + cspec,
        scratch_shapes=_acc(nk, (tm, tn)) + csem,
        compiler_params=_cp(*(("arbitrary",) * 3 if comm else ("parallel", "parallel", "arbitrary"))),
    )(a, w, *cin)
    return out[0] if comm is None else out


def mm_nt(dy, w, out_dtype, name, comm=None):
    M, N = dy.shape
    S, K, Ns = w.shape
    tm = _pick(M, (1088, 1024, 512, 256, 128))
    tn = _pick(K, (1408, 1024, 512, 256, 128))
    tk = _pick(Ns, (2816, 2048, 1792, 1408, 1024, 896, 512, 256, 128))
    kps, nk = Ns // tk, N // tk
    out = _pcall(
        _mm_body(((1,), (1,)), nk, out_dtype), name=name, grid=(K // tn, M // tm, nk),
        out_shape=[jax.ShapeDtypeStruct((M, K), out_dtype)],
        in_specs=[pl.BlockSpec((tm, tk), lambda j, i, k: (i, k)),
                  pl.BlockSpec((None, tn, tk), lambda j, i, k: (k // kps, j, k % kps))],
        out_specs=[pl.BlockSpec((tm, tn), lambda j, i, k: (i, j))], args=(dy, w),
        scratch=_acc(nk, (tm, tn)), sem=("parallel", "parallel", "arbitrary"), comm=comm)
    return out[0] if comm is None else out


def mm_tn(a, dy, S, out_dtype, name):
    M, K = a.shape
    N = dy.shape[1]
    Ns = N // S
    to = _pick(K, (1024, 512, 256, 128))
    tn = _pick(Ns, (2816, 2048, 1792, 1408, 1024, 896, 512, 256, 128))
    tk = _pick(M, (1088, 1024, 512, 256, 128))
    nps, nk = Ns // tn, M // tk
    return pl.pallas_call(
        _mm_body(((0,), (0,)), nk, out_dtype), name=name, grid=(K // to, S * nps, nk),
        out_shape=jax.ShapeDtypeStruct((S, K, Ns), out_dtype),
        in_specs=[pl.BlockSpec((tk, to), lambda i, j, k: (k, i)),
                  pl.BlockSpec((tk, tn), lambda i, j, k: (k, j))],
        out_specs=pl.BlockSpec((None, to, tn), lambda i, j, k: (j // nps, i, j % nps)),
        scratch_shapes=_acc(nk, (to, tn)), compiler_params=_cp("parallel", "parallel", "arbitrary"),
    )(a, dy)


_DIMS = {"nn": ((1,), (0,)), "nt": ((1,), (1,)), "tn": ((0,), (0,))}


def _dot(a, b, mode):
    return lax.dot_general(a.astype(BF16), b.astype(BF16), (_DIMS[mode], ((), ())), preferred_element_type=F32)


@functools.partial(jax.custom_vjp, nondiff_argnums=(2,))
def mmf(a, b, mode):
    return _dot(a, b, mode)


def _mmf_fwd(a, b, mode):
    return _dot(a, b, mode), (a, b)


def _mmf_bwd(mode, res, ct):
    a, b = res
    if mode == "nn":
        return _dot(ct, b, "nt"), _dot(a, ct, "tn")
    if mode == "nt":
        return _dot(ct, b, "nn"), _dot(ct, a, "tn")
    return _dot(b, ct, "nt"), _dot(a, ct, "nn")


mmf.defvjp(_mmf_fwd, _mmf_bwd)


def _dot_hi(m, g):
    return jnp.dot(m, g, precision=lax.Precision.HIGHEST, preferred_element_type=F32)


@jax.custom_vjp
def cumdot(m, mt, g):
    return _dot_hi(m, g)


def _cumdot_fwd(m, mt, g):
    return _dot_hi(m, g), (m, mt)


def _cumdot_bwd(res, ct):
    m, mt = res
    return jnp.zeros_like(m), jnp.zeros_like(mt), _dot_hi(mt, ct)


cumdot.defvjp(_cumdot_fwd, _cumdot_bwd)


def _rms(x, w):
    return x * lax.rsqrt(jnp.mean(x * x, axis=-1, keepdims=True) + EPS) * w


def _silu(x):
    return x * jax.nn.sigmoid(x)


RT = 16


def _gn_math(has_gate, x, m, gate, lnw, shift, scale):
    xn = x + gate * m if has_gate else x
    h = _rms(xn, lnw) * (1.0 + scale) + shift
    return xn, h


def _seg_spec(width, ncb):
    return pl.BlockSpec((None, RT, width), lambda i: (jnp.minimum(i // ncb, 1), 0, 0))


def gate_norm(x, m, gate, lnw, shift, scale, nc, R, name):
    T, D = x.shape
    has_gate = m is not None
    ncb = nc // R

    def body(*refs):
        if has_gate:
            x_ref, m_ref, g_ref, w_ref, sh_ref, sc_ref, xn_ref, h_ref = refs
        else:
            x_ref, w_ref, sh_ref, sc_ref, h_ref = refs

        def step(t, carry):
            rows = pl.ds(pl.multiple_of(t * RT, RT), RT)
            xn, h = _gn_math(has_gate, x_ref[rows, :], m_ref[rows, :] if has_gate else None,
                             g_ref[...] if has_gate else None, w_ref[...], sh_ref[...], sc_ref[...])
            if has_gate:
                xn_ref[rows, :] = xn
            h_ref[rows, :] = h.astype(BF16)
            return carry

        lax.fori_loop(0, R // RT, step, 0)

    row = pl.BlockSpec((R, D), lambda i: (i, 0))
    seg = _seg_spec(D, ncb)
    shared = pl.BlockSpec((None, RT, D), lambda i: (0, 0, 0))
    if has_gate:
        ins, in_specs = (x, m, gate, lnw, shift, scale), [row, row, seg, shared, seg, seg]
        out_shape = (jax.ShapeDtypeStruct((T, D), F32), jax.ShapeDtypeStruct((T, D), BF16))
        out_specs = (row, row)
    else:
        ins, in_specs = (x, lnw, shift, scale), [row, shared, seg, seg]
        out_shape, out_specs = jax.ShapeDtypeStruct((T, D), BF16), row
    out = pl.pallas_call(body, name=name, grid=(T // R,), out_shape=out_shape, in_specs=in_specs,
                         out_specs=out_specs, compiler_params=_cp("parallel"))(*ins)
    return out if has_gate else (None, out)


def gate_norm_bwd(x, m, gate, lnw, shift, scale, dxn, dh, nc, R, name):
    T, D = x.shape
    has_gate = m is not None
    ncb = nc // R

    def body(*refs):
        if has_gate:
            (x_ref, m_ref, g_ref, w_ref, sh_ref, sc_ref, dxn_ref, dh_ref,
             dx_ref, dm_ref, dg_ref, dw_ref, dsh_ref, dsc_ref) = refs
        else:
            x_ref, w_ref, sh_ref, sc_ref, dxn_ref, dh_ref, dx_ref, dw_ref, dsh_ref, dsc_ref = refs
        i = pl.program_id(0)

        @pl.when(i == 0)
        def _():
            dw_ref[...] = jnp.zeros_like(dw_ref)

        @pl.when((i == 0) | (i == ncb))
        def _():
            dsh_ref[...] = jnp.zeros_like(dsh_ref)
            dsc_ref[...] = jnp.zeros_like(dsc_ref)
            if has_gate:
                dg_ref[...] = jnp.zeros_like(dg_ref)

        def step(t, carry):
            rows = pl.ds(pl.multiple_of(t * RT, RT), RT)
            ct = (dxn_ref[rows, :], dh_ref[rows, :])
            if has_gate:
                _, vjp = jax.vjp(functools.partial(_gn_math, True), x_ref[rows, :], m_ref[rows, :], g_ref[...],
                                 w_ref[...], sh_ref[...], sc_ref[...])
                dx, dm, dg, dw, dsh, dsc = vjp(ct)
                dm_ref[rows, :] = dm.astype(BF16)
                dg_ref[...] += dg
            else:
                f = lambda x_, w_, sh_, sc_: _gn_math(False, x_, None, None, w_, sh_, sc_)[1]
                _, vjp = jax.vjp(f, x_ref[rows, :], w_ref[...], sh_ref[...], sc_ref[...])
                dx, dw, dsh, dsc = vjp(ct[1])
                dx = dx + ct[0]
            dx_ref[rows, :] = dx
            dw_ref[...] += dw
            dsh_ref[...] += dsh
            dsc_ref[...] += dsc
            return carry

        lax.fori_loop(0, R // RT, step, 0)

    row = pl.BlockSpec((R, D), lambda i: (i, 0))
    seg = _seg_spec(D, ncb)
    shared = pl.BlockSpec((None, RT, D), lambda i: (0, 0, 0))
    full, segs, one = jax.ShapeDtypeStruct((T, D), F32), jax.ShapeDtypeStruct((2, RT, D), F32), \
        jax.ShapeDtypeStruct((1, RT, D), F32)
    if has_gate:
        ins = (x, m, gate, lnw, shift, scale, dxn, dh)
        in_specs = [row, row, seg, shared, seg, seg, row, row]
        out_shape = (full, jax.ShapeDtypeStruct((T, D), BF16), segs, one, segs, segs)
        out_specs = (row, row, seg, shared, seg, seg)
    else:
        ins = (x, lnw, shift, scale, dxn, dh)
        in_specs = [row, shared, seg, seg, row, row]
        out_shape = (full, one, segs, segs)
        out_specs = (row, shared, seg, seg)
    out = pl.pallas_call(body, name=name, grid=(T // R,), out_shape=out_shape, in_specs=in_specs,
                         out_specs=out_specs, compiler_params=_cp("arbitrary"))(*ins)
    if has_gate:
        return out
    dx, dw, dsh, dsc = out
    return dx, None, None, dw, dsh, dsc


def gate_loss(x, m, gate, target, nc, R, name):
    T, D = x.shape
    ncb = nc // R

    def body(x_ref, m_ref, g_ref, t_ref, loss_ref, dx_ref, dm_ref, dg_ref):
        i = pl.program_id(0)

        @pl.when(i == 0)
        def _():
            loss_ref[...] = jnp.zeros_like(loss_ref)

        @pl.when((i == 0) | (i == ncb))
        def _():
            dg_ref[...] = jnp.zeros_like(dg_ref)

        live = jnp.where(i >= ncb, 1.0, 0.0).astype(F32)

        def step(t, carry):
            rows = pl.ds(pl.multiple_of(t * RT, RT), RT)
            mm_ = m_ref[rows, :]
            g = g_ref[...]
            e = (x_ref[rows, :] + g * mm_ - t_ref[rows, :]) * live
            dy = e * (1.0 / D)
            loss_ref[...] += 0.5 * e * dy
            dx_ref[rows, :] = dy
            dm_ref[rows, :] = (dy * g).astype(BF16)
            dg_ref[...] += dy * mm_
            return carry

        lax.fori_loop(0, R // RT, step, 0)

    row = pl.BlockSpec((R, D), lambda i: (i, 0))
    seg = _seg_spec(D, ncb)
    return pl.pallas_call(
        body, name=name, grid=(T // R,),
        out_shape=(jax.ShapeDtypeStruct((RT, D), F32), jax.ShapeDtypeStruct((T, D), F32),
                   jax.ShapeDtypeStruct((T, D), BF16), jax.ShapeDtypeStruct((2, RT, D), F32)),
        in_specs=[row, row, seg, pl.BlockSpec((R, D), lambda i: (jnp.maximum(i - ncb, 0), 0))],
        out_specs=(pl.BlockSpec((RT, D), lambda i: (0, 0)), row, row, seg),
        compiler_params=_cp("arbitrary"),
    )(x, m, gate, target)


def _hg_chunk(rev, lb, z, iv, hq, st):
    f = lb + (1.0 - lb) * jax.nn.sigmoid(z)
    g = jnp.log(jnp.maximum(f, F_FLOOR))
    k = (1.0 - lb) * jax.nn.sigmoid(-z)
    q = _silu(hq)
    ri = lax.broadcasted_iota(jnp.int32, (CHUNK, CHUNK), 0)
    ci = lax.broadcasted_iota(jnp.int32, (CHUNK, CHUNK), 1)
    r1 = lax.broadcasted_iota(jnp.int32, (CHUNK, 1), 0)
    seen = (ci >= ri) if rev else (ci <= ri)
    seen_t = (ci <= ri) if rev else (ci >= ri)
    cum = cumdot(seen.astype(F32), seen_t.astype(F32), g)
    tot = jnp.sum(g, axis=0, keepdims=True)
    att = jnp.zeros((CHUNK, CHUNK), F32)
    ref_rows = jnp.zeros_like(g)
    refs = []
    for b in range(CHUNK // SUB):
        before = (r1 >= SUB * (b + 1)) if rev else (r1 < SUB * b)
        r_b = jnp.sum(jnp.where(before, g, 0.0), axis=0, keepdims=True)
        in_b = (r1 >= SUB * b) & (r1 < SUB * (b + 1))
        ref_rows = ref_rows + jnp.where(in_b, r_b, 0.0)
        refs.append(r_b)
    qd = q * jnp.exp(cum - ref_rows)
    for b in range(CHUNK // SUB):
        kd = k * jnp.exp(jnp.minimum(refs[b] - cum, EXP_CLAMP))
        in_b = (ri >= SUB * b) & (ri < SUB * (b + 1))
        att = att + jnp.where(in_b, mmf(qd, kd, "nt"), 0.0)
    att = jnp.where(seen, att, 0.0)
    o = mmf(att, iv, "nn") + mmf(q * jnp.exp(cum), st, "nt")
    st_new = st * jnp.exp(tot) + mmf(iv, k * jnp.exp(tot - cum), "tn")
    return st_new, o


def _hg_cid(rev, i, ncs, n):
    if not rev:
        return i
    return jnp.where(i < ncs, ncs - 1 - i, ncs + n - 1 - i)


def hgrn_fwd(u, lb, rev, zcol, nc, hgw, name):
    T = u.shape[0]
    n, ncs, nh = T // CHUNK, nc // CHUNK, hgw // LANE

    def body(z_ref, v_ref, q_ref, lb_ref, o_ref, s_ref, st):
        i = pl.program_id(0)

        @pl.when(i == 0)
        def _():
            st[...] = jnp.zeros_like(st)

        for h in range(nh):
            cols = slice(h * LANE, (h + 1) * LANE)
            s_ref[h] = st[h]
            s_new, o = _hg_chunk(rev, lb_ref[:, cols], z_ref[:, cols], v_ref[:, cols], q_ref[:, cols], st[h])
            st[h] = s_new
            o_ref[:, cols] = o

    def col(cb):
        return pl.BlockSpec((CHUNK, hgw), lambda i: (_hg_cid(rev, i, ncs, n), cb))

    return pl.pallas_call(
        body, name=name, grid=(n,),
        out_shape=(jax.ShapeDtypeStruct((T, hgw), F32), jax.ShapeDtypeStruct((n, nh, LANE, LANE), F32)),
        in_specs=[col(zcol), col(2), col(7), pl.BlockSpec((1, hgw), lambda i: (0, 0))],
        out_specs=(pl.BlockSpec((CHUNK, hgw), lambda i: (_hg_cid(rev, i, ncs, n), 0)),
                   pl.BlockSpec((None, nh, LANE, LANE), lambda i: (i, 0, 0, 0))),
        scratch_shapes=[pltpu.VMEM((nh, LANE, LANE), F32)], compiler_params=_cp("arbitrary"),
    )(u, u, u, lb)


def hgrn_bwd(u, lb, states, do, rev, zcol, nc, hgw, name):
    T = u.shape[0]
    n, ncs, nh = T // CHUNK, nc // CHUNK, hgw // LANE

    def body(z_ref, v_ref, q_ref, lb_ref, s_ref, do_ref, dz_ref, dv_ref, dq_ref, dlb_ref, dst):
        j = pl.program_id(0)

        @pl.when(j == 0)
        def _():
            dst[...] = jnp.zeros_like(dst)
            dlb_ref[...] = jnp.zeros_like(dlb_ref)

        for h in range(nh):
            cols = slice(h * LANE, (h + 1) * LANE)
            _, vjp = jax.vjp(functools.partial(_hg_chunk, rev), lb_ref[:, cols], z_ref[:, cols], v_ref[:, cols],
                             q_ref[:, cols], s_ref[h])
            dlb, dz, dv, dq, ds = vjp((dst[h], do_ref[:, cols]))
            dst[h] = ds
            dz_ref[:, cols] = dz
            dv_ref[:, cols] = dv
            dq_ref[:, cols] = dq
            dlb_ref[:, cols] += dlb

    def cid(j):
        return _hg_cid(rev, n - 1 - j, ncs, n)

    def col(cb):
        return pl.BlockSpec((CHUNK, hgw), lambda j: (cid(j), cb))

    out = pl.BlockSpec((CHUNK, hgw), lambda j: (cid(j), 0))
    full = jax.ShapeDtypeStruct((T, hgw), F32)
    return pl.pallas_call(
        body, name=name, grid=(n,),
        out_shape=(full, full, full, jax.ShapeDtypeStruct((1, hgw), F32)),
        in_specs=[col(zcol), col(2), col(7), pl.BlockSpec((1, hgw), lambda j: (0, 0)),
                  pl.BlockSpec((None, nh, LANE, LANE), lambda j: (n - 1 - j, 0, 0, 0)), out],
        out_specs=(out, out, out, pl.BlockSpec((1, hgw), lambda j: (0, 0))),
        scratch_shapes=[pltpu.VMEM((nh, LANE, LANE), F32)], compiler_params=_cp("arbitrary"),
    )(u, u, u, lb, states, do)


HT = 128


def _head_group(nh, *col_offsets):
    for g in (4, 2):
        if nh % g == 0 and all(c % g == 0 for c in col_offsets):
            return g
    return 1


def _read_math(ofw, obw, g, w):
    return _rms(ofw + obw, w) * _silu(g)


def hg_read(ofw, obw, u, w, gcol, R, name):
    T, hgw = ofw.shape
    nh = hgw // LANE
    g = _head_group(nh, gcol)

    def body(a_ref, b_ref, g_ref, w_ref, o_ref):
        for j in range(g):
            cols = slice(j * LANE, (j + 1) * LANE)
            for t in range(R // HT):
                rows = slice(t * HT, (t + 1) * HT)
                o_ref[rows, cols] = _read_math(a_ref[rows, cols], b_ref[rows, cols], g_ref[rows, cols],
                                               w_ref[...]).astype(BF16)

    blk = pl.BlockSpec((R, g * LANE), lambda i, h: (i, h))
    return pl.pallas_call(
        body, name=name, grid=(T // R, nh // g), out_shape=jax.ShapeDtypeStruct((T, hgw), BF16),
        in_specs=[blk, blk, pl.BlockSpec((R, g * LANE), lambda i, h: (i, gcol // g + h)),
                  pl.BlockSpec((1, LANE), lambda i, h: (0, 0))],
        out_specs=blk, compiler_params=_cp("parallel", "parallel"),
    )(ofw, obw, u, w)


def hg_read_bwd(ofw, obw, u, w, dout, gcol, ocol, R, name):
    T, hgw = ofw.shape
    nh = hgw // LANE
    g = _head_group(nh, gcol, ocol)

    def body(a_ref, b_ref, g_ref, w_ref, d_ref, do_ref, dg_ref, dw_ref):
        @pl.when(pl.program_id(1) == 0)
        def _():
            dw_ref[...] = jnp.zeros_like(dw_ref)

        for j in range(g):
            cols = slice(j * LANE, (j + 1) * LANE)
            for t in range(R // HT):
                rows = slice(t * HT, (t + 1) * HT)
                _, vjp = jax.vjp(_read_math, a_ref[rows, cols], b_ref[rows, cols], g_ref[rows, cols], w_ref[...])
                da, _, dg, dw = vjp(d_ref[rows, cols])
                do_ref[rows, cols] = da
                dg_ref[rows, cols] = dg
                dw_ref[j] += dw

    blk = pl.BlockSpec((R, g * LANE), lambda h, i: (i, h))
    full = jax.ShapeDtypeStruct((T, hgw), F32)
    return pl.pallas_call(
        body, name=name, grid=(nh // g, T // R), out_shape=(full, full, jax.ShapeDtypeStruct((nh, 1, LANE), F32)),
        in_specs=[blk, blk, pl.BlockSpec((R, g * LANE), lambda h, i: (i, gcol // g + h)),
                  pl.BlockSpec((1, LANE), lambda h, i: (0, 0)),
                  pl.BlockSpec((R, g * LANE), lambda h, i: (i, ocol // g + h))],
        out_specs=(blk, blk, pl.BlockSpec((g, 1, LANE), lambda h, i: (h, 0, 0))),
        compiler_params=_cp("parallel", "arbitrary"),
    )(ofw, obw, u, w, dout)


def _na_step(qw, ow, bias, qraw, kl, vl, kc, vc):
    q = _rms(qraw, qw)
    s_loc = mmf(q, kl, "nt") * ATTN_SCALE + bias
    s_ctx = mmf(q, kc, "nt") * ATTN_SCALE
    m = lax.stop_gradient(jnp.maximum(jnp.max(s_loc, axis=-1, keepdims=True), jnp.max(s_ctx, axis=-1, keepdims=True)))
    p_loc = jnp.exp(s_loc - m)
    p_ctx = jnp.exp(s_ctx - m)
    inv = 1.0 / (jnp.sum(p_loc, axis=-1, keepdims=True) + jnp.sum(p_ctx, axis=-1, keepdims=True))
    return _rms(mmf(p_loc * inv, vl, "nn") + mmf(p_ctx * inv, vc, "nn"), ow)


def _na_geometry(nc, rows):
    ncs = nc // GRID_W
    win_r = min(WIN_R, rows)
    nloc = win_r * GRID_W

    def row_start(s):
        r = jnp.maximum(s - ncs, 0)
        return jnp.clip(r - win_r // 2, 0, rows - win_r)

    def bias_idx(s):
        r = s - ncs
        return jnp.where(s < ncs, win_r, r - jnp.clip(r - win_r // 2, 0, rows - win_r))

    return ncs, win_r, nloc, row_start, bias_idx


def na_bias_tables(rpb, rows):
    win_r = min(WIN_R, rows)
    nh = rpb.shape[0]
    sel_r = np.zeros((win_r, win_r, 2 * WIN_R - 1), np.float32)
    for off in range(win_r):
        for jr in range(win_r):
            sel_r[off, jr, jr - off + WIN_R - 1] = 1.0
    qc = np.arange(GRID_W)[:, None]
    kc = np.arange(GRID_W)[None, :]
    wstart = np.clip(qc - WIN_C // 2, 0, GRID_W - WIN_C)
    ok = (kc >= wstart) & (kc < wstart + WIN_C)
    sel_c = np.zeros((GRID_W, GRID_W, 2 * WIN_C - 1), np.float32)
    sel_c[np.broadcast_to(qc, ok.shape)[ok], np.broadcast_to(kc, ok.shape)[ok], (kc - qc + WIN_C - 1)[ok]] = 1.0
    hi = lax.Precision.HIGHEST
    t = jnp.einsum("hab,oja->hojb", rpb, sel_r, precision=hi)
    t = jnp.einsum("hojb,qkb->hoqjk", t, sel_c, precision=hi)
    t = jnp.where(ok[None, None, :, None, :], t, NEG_INF)
    t = jnp.concatenate([t, jnp.full((nh, 1, GRID_W, win_r, GRID_W), NEG_INF, F32)], axis=1)
    return t.reshape(nh, win_r + 1, GRID_W, win_r * GRID_W)


def kv_prep(u, kw, kcol, vcol, naw, R, name):
    T = u.shape[0]
    g = _head_group(naw // LANE, kcol, vcol)

    def body(k_ref, v_ref, w_ref, kn_ref, vb_ref):
        for j in range(g):
            cols = slice(j * LANE, (j + 1) * LANE)
            kn_ref[:, cols] = _rms(k_ref[:, cols], w_ref[...]).astype(BF16)
        vb_ref[...] = v_ref[...].astype(BF16)

    blk = pl.BlockSpec((R, g * LANE), lambda i, h: (i, h))
    sds = jax.ShapeDtypeStruct((T, naw), BF16)
    return pl.pallas_call(
        body, name=name, grid=(T // R, naw // LANE // g), out_shape=(sds, sds),
        in_specs=[pl.BlockSpec((R, g * LANE), lambda i, h: (i, kcol // g + h)),
                  pl.BlockSpec((R, g * LANE), lambda i, h: (i, vcol // g + h)),
                  pl.BlockSpec((1, LANE), lambda i, h: (0, 0))],
        out_specs=(blk, blk), compiler_params=_cp("parallel", "parallel"),
    )(u, u, kw)


def kv_prep_bwd(u, kw, dkn, kcol, naw, R, name):
    T = u.shape[0]
    nh = naw // LANE
    g = _head_group(nh, kcol)

    def body(k_ref, w_ref, d_ref, dk_ref, dw_ref):
        @pl.when(pl.program_id(1) == 0)
        def _():
            dw_ref[...] = jnp.zeros_like(dw_ref)

        for j in range(g):
            cols = slice(j * LANE, (j + 1) * LANE)
            for t in range(R // HT):
                rows = slice(t * HT, (t + 1) * HT)
                _, vjp = jax.vjp(_rms, k_ref[rows, cols], w_ref[...])
                dk, dw = vjp(d_ref[rows, cols])
                dk_ref[rows, cols] = dk
                dw_ref[j] += dw

    blk = pl.BlockSpec((R, g * LANE), lambda h, i: (i, h))
    return pl.pallas_call(
        body, name=name, grid=(nh // g, T // R),
        out_shape=(jax.ShapeDtypeStruct((T, naw), F32), jax.ShapeDtypeStruct((nh, 1, LANE), F32)),
        in_specs=[pl.BlockSpec((R, g * LANE), lambda h, i: (i, kcol // g + h)),
                  pl.BlockSpec((1, LANE), lambda h, i: (0, 0)), blk],
        out_specs=(blk, pl.BlockSpec((g, 1, LANE), lambda h, i: (h, 0, 0))),
        compiler_params=_cp("parallel", "arbitrary"),
    )(u, kw, dkn)


NA_HB = 4


def _na_operands(j, s, nc, nloc, row_start, q_refs, k_ref, v_ref, qw_ref, ow_ref, b_ref):
    cols = slice(j * LANE, (j + 1) * LANE)
    loc = pl.ds(pl.multiple_of(nc + row_start(s) * GRID_W, GRID_W), nloc)
    ops = (qw_ref[...], ow_ref[:, cols], b_ref[j], q_refs[j][...], k_ref[loc, cols].astype(F32),
           v_ref[loc, cols].astype(F32), k_ref[0:nc, cols].astype(F32), v_ref[0:nc, cols].astype(F32))
    return cols, loc, ops


def _grid_ends(grid):
    ids = lambda: [pl.program_id(d) for d in range(len(grid))]
    first = lambda: functools.reduce(jnp.logical_and, [p == 0 for p in ids()])
    last = lambda: functools.reduce(jnp.logical_and, [p == g - 1 for p, g in zip(ids(), grid)])
    return first, last


def na_fwd(u, kn, vb, qw, ow, bias, qcol, nc, name, comm=None):
    T, naw = kn.shape
    nh, rows = naw // LANE, (T - nc) // GRID_W
    hb = NA_HB if nh % NA_HB == 0 else 1
    ncs, win_r, nloc, row_start, bias_idx = _na_geometry(nc, rows)

    def body(*refs):
        q_refs, (k_ref, v_ref, qw_ref, ow_ref, b_ref, o_ref) = refs[:hb], refs[hb:]
        s = pl.program_id(1)
        for j in range(hb):
            cols, _, ops = _na_operands(j, s, nc, nloc, row_start, q_refs, k_ref, v_ref, qw_ref, ow_ref, b_ref)
            o_ref[:, cols] = _na_step(*ops).astype(BF16)

    wide = pl.BlockSpec((T, hb * LANE), lambda g, s: (0, g), pipeline_mode=pl.Buffered(1))
    grid = (nh // hb, T // GRID_W)
    cin, cspec, cout, csem = _comm_extras(comm)
    out = pl.pallas_call(
        _hosted(body, hb + 5, 1, comm, *_grid_ends(grid)), name=name, grid=grid,
        out_shape=[jax.ShapeDtypeStruct((T, naw), BF16)] + cout,
        in_specs=[pl.BlockSpec((GRID_W, LANE), functools.partial(lambda j, g, s: (s, qcol + g * hb + j), j))
                  for j in range(hb)]
        + [wide, wide, pl.BlockSpec((1, LANE), lambda g, s: (0, 0)), pl.BlockSpec((1, hb * LANE), lambda g, s: (0, g)),
           pl.BlockSpec((hb, None, GRID_W, nloc), lambda g, s: (g, bias_idx(s), 0, 0))] + cspec,
        out_specs=[pl.BlockSpec((GRID_W, hb * LANE), lambda g, s: (s, g))] + cspec, scratch_shapes=csem,
        compiler_params=_cp("arbitrary", "arbitrary"),
    )(*([u] * hb), kn, vb, qw, ow, bias, *cin)
    return out[0] if comm is None else out


def na_bwd(u, kn, vb, qw, ow, bias, dout, qcol, ocol, nc, name, comm=None):
    T, naw = kn.shape
    nh, rows = naw // LANE, (T - nc) // GRID_W
    hb = NA_HB if nh % NA_HB == 0 else 1
    ncs, win_r, nloc, row_start, bias_idx = _na_geometry(nc, rows)
    fresh = [0] + [ncs + r for r in range(rows) if r == 0 or r - np.clip(r - win_r // 2, 0, rows - win_r)
                   != (r - 1) - np.clip(r - 1 - win_r // 2, 0, rows - win_r)]

    def body(*refs):
        q_refs, d_refs = refs[:hb], refs[hb:2 * hb]
        k_ref, v_ref, qw_ref, ow_ref, b_ref, dq_ref, dk_ref, dv_ref, db_ref, dqw_ref, dow_ref = refs[2 * hb:]
        s = pl.program_id(1)

        @pl.when(s == 0)
        def _():
            dk_ref[...] = jnp.zeros_like(dk_ref)
            dv_ref[...] = jnp.zeros_like(dv_ref)
            dqw_ref[...] = jnp.zeros_like(dqw_ref)
            dow_ref[...] = jnp.zeros_like(dow_ref)

        first = functools.reduce(lambda a, b: a | b, [s == f for f in fresh])

        @pl.when(first)
        def _():
            db_ref[...] = jnp.zeros_like(db_ref)

        for j in range(hb):
            cols, loc, ops = _na_operands(j, s, nc, nloc, row_start, q_refs, k_ref, v_ref, qw_ref, ow_ref, b_ref)
            _, vjp = jax.vjp(_na_step, *ops)
            dqw, dow, db, dq, dkl, dvl, dkc, dvc = vjp(d_refs[j][...])
            dq_ref[:, cols] = dq
            dk_ref[loc, cols] += dkl
            dv_ref[loc, cols] += dvl
            dk_ref[0:nc, cols] += dkc
            dv_ref[0:nc, cols] += dvc
            db_ref[j] += db
            dqw_ref[j] += dqw
            dow_ref[j] += dow

    wide = pl.BlockSpec((T, hb * LANE), lambda g, s: (0, g), pipeline_mode=pl.Buffered(1))
    hvec = pl.BlockSpec((hb, 1, LANE), lambda g, s: (g, 0, 0))
    full = jax.ShapeDtypeStruct((T, naw), F32)
    hv = jax.ShapeDtypeStruct((nh, 1, LANE), F32)
    bspec = pl.BlockSpec((hb, None, GRID_W, nloc), lambda g, s: (g, bias_idx(s), 0, 0))
    grid = (nh // hb, T // GRID_W)
    cin, cspec, cout, csem = _comm_extras(comm)
    return pl.pallas_call(
        _hosted(body, 2 * hb + 5, 6, comm, *_grid_ends(grid)), name=name, grid=grid,
        out_shape=[full, full, full, jax.ShapeDtypeStruct(bias.shape, F32), hv, hv] + cout,
        in_specs=[pl.BlockSpec((GRID_W, LANE), functools.partial(lambda j, g, s: (s, qcol + g * hb + j), j))
                  for j in range(hb)]
        + [pl.BlockSpec((GRID_W, LANE), functools.partial(lambda j, g, s: (s, ocol + g * hb + j), j))
           for j in range(hb)]
        + [wide, wide, pl.BlockSpec((1, LANE), lambda g, s: (0, 0)), pl.BlockSpec((1, hb * LANE), lambda g, s: (0, g)),
           bspec] + cspec,
        out_specs=[pl.BlockSpec((GRID_W, hb * LANE), lambda g, s: (s, g)), wide, wide, bspec, hvec, hvec] + cspec,
        scratch_shapes=csem, compiler_params=_cp("arbitrary", "arbitrary"),
    )(*([u] * hb), *([dout] * hb), kn, vb, qw, ow, bias, *cin)


def _halo_specs(R, width, T, col):
    hb = R // 8
    prev = pl.BlockSpec((8, width), lambda j, i: (jnp.maximum(i * hb - 1, 0), col(j, i)))
    nxt = pl.BlockSpec((8, width), lambda j, i: (jnp.minimum((i + 1) * hb, T // 8 - 1), col(j, i)))
    return prev, nxt


def _edge_flags(i, ncb, nblk):
    has_prev = jnp.where((i == 0) | (i == ncb), 0.0, 1.0).astype(F32)
    has_next = jnp.where((i == ncb - 1) | (i == nblk - 1), 0.0, 1.0).astype(F32)
    return has_prev, has_next


def _shift_up(a, prev_row):
    r0 = lax.broadcasted_iota(jnp.int32, a.shape, 0) == 0
    return jnp.where(r0, prev_row, pltpu.roll(a, 1, 0))


def _shift_dn(a, next_row):
    n = a.shape[0]
    rl = lax.broadcasted_iota(jnp.int32, a.shape, 0) == n - 1
    return jnp.where(rl, next_row, pltpu.roll(a, n - 1, 0))


def _conv3(a, prev_row, next_row, w_ref):
    return w_ref[0:1, :] * _shift_up(a, prev_row) + w_ref[1:2, :] * a + w_ref[2:3, :] * _shift_dn(a, next_row)


def _cv_post(b, y, w):
    return _rms(b * y, w)


def short_conv(u, cw, ow, bcol, nc, cvw, R, bwd_dout=None, ocol=0, name=""):
    T = u.shape[0]
    nh, nblk, ncb = cvw // LANE, T // R, nc // R
    bwd = bwd_dout is not None
    g = _head_group(nh, bcol, bcol + nh, bcol + 2 * nh, ocol)
    gw = g * LANE

    def body(b_ref, c_ref, v_ref, cp_ref, vp_ref, cn_ref, vn_ref, cw_ref, ow_ref, *rest):
        i = pl.program_id(1)
        has_prev, has_next = _edge_flags(i, ncb, nblk)
        p = c_ref[...] * v_ref[...]
        y = _conv3(p, cp_ref[7:8, :] * vp_ref[7:8, :] * has_prev, cn_ref[0:1, :] * vn_ref[0:1, :] * has_next, cw_ref)
        if bwd:
            d_ref, db_ref, dy_ref, dow_ref = rest

            @pl.when(i == 0)
            def _():
                dow_ref[...] = jnp.zeros_like(dow_ref)

        for j in range(g):
            cols = slice(j * LANE, (j + 1) * LANE)
            if not bwd:
                rest[0][:, cols] = _cv_post(b_ref[:, cols], y[:, cols], ow_ref[:, cols]).astype(BF16)
            else:
                _, vjp = jax.vjp(_cv_post, b_ref[:, cols], y[:, cols], ow_ref[:, cols])
                db, dy, dow = vjp(d_ref[:, cols])
                db_ref[:, cols] = db
                dy_ref[:, cols] = dy
                dow_ref[j] += dow

    def main(k):
        return pl.BlockSpec((R, gw), lambda h, i: (i, (bcol + k * nh) // g + h))

    cprev, cnext = _halo_specs(R, gw, T, lambda h, i: (bcol + nh) // g + h)
    vprev, vnext = _halo_specs(R, gw, T, lambda h, i: (bcol + 2 * nh) // g + h)
    in_specs = [main(0), main(1), main(2), cprev, vprev, cnext, vnext,
                pl.BlockSpec((3, gw), lambda h, i: (0, h)), pl.BlockSpec((1, gw), lambda h, i: (0, h))]
    ins = [u] * 7 + [cw, ow]
    blk = pl.BlockSpec((R, gw), lambda h, i: (i, h))
    if not bwd:
        out_shape, out_specs = jax.ShapeDtypeStruct((T, cvw), BF16), blk
    else:
        in_specs.append(pl.BlockSpec((R, gw), lambda h, i: (i, ocol // g + h)))
        ins.append(bwd_dout)
        full = jax.ShapeDtypeStruct((T, cvw), F32)
        out_shape = (full, full, jax.ShapeDtypeStruct((nh, 1, LANE), F32))
        out_specs = (blk, blk, pl.BlockSpec((g, 1, LANE), lambda h, i: (h, 0, 0)))
    return pl.pallas_call(body, name=name, grid=(nh // g, nblk), out_shape=out_shape, in_specs=in_specs,
                          out_specs=out_specs, compiler_params=_cp("parallel", "arbitrary"))(*ins)


def conv3_bwd(dy, src, cw, nc, R, W, prod_cols=None, col0=0, out_dtype=F32, name=""):
    T, C = dy.shape
    nblk, ncb = T // R, nc // R
    prod = prod_cols is not None

    def body(*refs):
        if prod:
            (d_ref, dp_ref, dn_ref, c_ref, v_ref, cp_ref, vp_ref, cn_ref, vn_ref, w_ref,
             dc_ref, dv_ref, dw_ref) = refs
        else:
            d_ref, dp_ref, dn_ref, p_ref, pp_ref, pn_ref, w_ref, o_ref, dw_ref = refs
        i = pl.program_id(1)
        has_prev, has_next = _edge_flags(i, ncb, nblk)

        @pl.when(i == 0)
        def _():
            dw_ref[...] = jnp.zeros_like(dw_ref)

        d = d_ref[...]
        d_up = _shift_up(d, dp_ref[7:8, :] * has_prev)
        d_dn = _shift_dn(d, dn_ref[0:1, :] * has_next)
        dp = w_ref[0:1, :] * d_dn + w_ref[1:2, :] * d + w_ref[2:3, :] * d_up
        if prod:
            c, v = c_ref[...], v_ref[...]
            p = c * v
            p_prev, p_next = cp_ref[7:8, :] * vp_ref[7:8, :] * has_prev, cn_ref[0:1, :] * vn_ref[0:1, :] * has_next
            dc_ref[...] = dp * v
            dv_ref[...] = dp * c
        else:
            p = p_ref[...]
            p_prev, p_next = pp_ref[7:8, :] * has_prev, pn_ref[0:1, :] * has_next
            o_ref[...] = dp.astype(out_dtype)
        dw_ref[0:1, :] += jnp.sum(_shift_up(p, p_prev) * d, axis=0, keepdims=True)
        dw_ref[1:2, :] += jnp.sum(p * d, axis=0, keepdims=True)
        dw_ref[2:3, :] += jnp.sum(_shift_dn(p, p_next) * d, axis=0, keepdims=True)

    blk = pl.BlockSpec((R, W), lambda j, i: (i, j))
    dprev, dnext = _halo_specs(R, W, T, lambda j, i: j)
    wspec = pl.BlockSpec((3, W), lambda j, i: (0, j))
    dwspec = pl.BlockSpec((8, W), lambda j, i: (0, j))
    dwshape = jax.ShapeDtypeStruct((8, C), F32)
    if prod:
        ccol, vcol = prod_cols
        cprev, cnext = _halo_specs(R, W, T, lambda j, i: ccol + j)
        vprev, vnext = _halo_specs(R, W, T, lambda j, i: vcol + j)
        in_specs = [blk, dprev, dnext, pl.BlockSpec((R, W), lambda j, i: (i, ccol + j)),
                    pl.BlockSpec((R, W), lambda j, i: (i, vcol + j)), cprev, vprev, cnext, vnext, wspec]
        ins = [dy, dy, dy] + [src] * 6 + [cw]
        full = jax.ShapeDtypeStruct((T, C), F32)
        out_shape, out_specs = (full, full, dwshape), (blk, blk, dwspec)
    else:
        sprev, snext = _halo_specs(R, W, T, lambda j, i: col0 + j)
        in_specs = [blk, dprev, dnext, pl.BlockSpec((R, W), lambda j, i: (i, col0 + j)), sprev, snext,
                    pl.BlockSpec((3, W), lambda j, i: (0, col0 + j))]
        ins = [dy, dy, dy, src, src, src, cw]
        out_shape, out_specs = (jax.ShapeDtypeStruct((T, C), out_dtype), dwshape), (blk, dwspec)
    return pl.pallas_call(body, name=name, grid=(C // W, nblk), out_shape=out_shape, in_specs=in_specs,
                          out_specs=out_specs, compiler_params=_cp("parallel", "arbitrary"))(*ins)


def _pcall(body, *, name, grid, in_specs, out_specs, out_shape, args, scratch=(), sem=None, comm=None):
    cin, cspec, cout, csem = _comm_extras(comm)
    if comm is not None:
        sem = ("arbitrary",) * len(grid)
    return pl.pallas_call(
        _hosted(body, len(in_specs), len(out_specs), comm, *_grid_ends(grid)), name=name, grid=grid,
        out_shape=list(out_shape) + cout, in_specs=list(in_specs) + cspec, out_specs=list(out_specs) + cspec,
        scratch_shapes=list(scratch) + csem, compiler_params=_cp(*sem))(*args, *cin)


def ffn_mid(uf, cw, cb, nc, R, W, name, comm=None):
    T, C = uf.shape
    F = C // 2
    nblk, ncb, nj = T // R, nc // R, F // W

    def body(g_ref, v_ref, gp_ref, vp_ref, gn_ref, vn_ref, wg_ref, wv_ref, bg_ref, bv_ref, a_ref):
        i = pl.program_id(1)
        has_prev, has_next = _edge_flags(i, ncb, nblk)
        yg = _conv3(g_ref[...], gp_ref[7:8, :] * has_prev, gn_ref[0:1, :] * has_next, wg_ref) + bg_ref[...]
        yv = _conv3(v_ref[...], vp_ref[7:8, :] * has_prev, vn_ref[0:1, :] * has_next, wv_ref) + bv_ref[...]
        a_ref[...] = (yg * jax.nn.sigmoid(yg) * yv).astype(BF16)

    gblk = pl.BlockSpec((R, W), lambda j, i: (i, j))
    vblk = pl.BlockSpec((R, W), lambda j, i: (i, nj + j))
    gprev, gnext = _halo_specs(R, W, T, lambda j, i: j)
    vprev, vnext = _halo_specs(R, W, T, lambda j, i: nj + j)
    in_specs = [gblk, vblk, gprev, vprev, gnext, vnext,
                pl.BlockSpec((3, W), lambda j, i: (0, j)), pl.BlockSpec((3, W), lambda j, i: (0, nj + j)),
                pl.BlockSpec((1, W), lambda j, i: (0, j)), pl.BlockSpec((1, W), lambda j, i: (0, nj + j))]
    return _pcall(body, name=name, grid=(nj, nblk), in_specs=in_specs, out_specs=[gblk],
                  out_shape=[jax.ShapeDtypeStruct((T, F), BF16)], args=[uf] * 6 + [cw, cw, cb, cb],
                  sem=("parallel", "arbitrary"), comm=comm)


def ffn_mid_bwd(uf, cw, cb, da, nc, R, W, name, comm=None):
    T, C = uf.shape
    F = C // 2
    nblk, ncb, nj = T // R, nc // R, F // W

    def body(g_ref, v_ref, gp_ref, vp_ref, gn_ref, vn_ref, d_ref, dp_ref, dn_ref, wg_ref, wv_ref, bg_ref, bv_ref,
             dug_ref, duv_ref, dwg_ref, dwv_ref, dbg_ref, dbv_ref):
        i = pl.program_id(1)
        has_prev, has_next = _edge_flags(i, ncb, nblk)

        @pl.when(i == 0)
        def _():
            for r in (dwg_ref, dwv_ref, dbg_ref, dbv_ref):
                r[...] = jnp.zeros_like(r)

        def taps(w_ref):
            return w_ref[0:1, :], w_ref[1:2, :], w_ref[2:3, :]

        def dy_of(yg, yv, d):
            sg = jax.nn.sigmoid(yg)
            return d * yv * (sg * (1.0 + yg * (1.0 - sg))), d * (yg * sg)

        g, v, d = g_ref[...], v_ref[...], d_ref[...]
        (wg0, wg1, wg2), (wv0, wv1, wv2) = taps(wg_ref), taps(wv_ref)
        bg, bv = bg_ref[...], bv_ref[...]
        g_up, g_dn = _shift_up(g, gp_ref[7:8, :] * has_prev), _shift_dn(g, gn_ref[0:1, :] * has_next)
        v_up, v_dn = _shift_up(v, vp_ref[7:8, :] * has_prev), _shift_dn(v, vn_ref[0:1, :] * has_next)
        dyg, dyv = dy_of(wg0 * g_up + wg1 * g + wg2 * g_dn + bg, wv0 * v_up + wv1 * v + wv2 * v_dn + bv, d)
        dyg_p, dyv_p = dy_of(wg0 * gp_ref[6:7, :] + wg1 * gp_ref[7:8, :] + wg2 * g_ref[0:1, :] + bg,
                             wv0 * vp_ref[6:7, :] + wv1 * vp_ref[7:8, :] + wv2 * v_ref[0:1, :] + bv, dp_ref[7:8, :])
        dyg_n, dyv_n = dy_of(wg0 * g_ref[R - 1:R, :] + wg1 * gn_ref[0:1, :] + wg2 * gn_ref[1:2, :] + bg,
                             wv0 * v_ref[R - 1:R, :] + wv1 * vn_ref[0:1, :] + wv2 * vn_ref[1:2, :] + bv, dn_ref[0:1, :])
        dug_ref[...] = (wg0 * _shift_dn(dyg, dyg_n * has_next) + wg1 * dyg
                        + wg2 * _shift_up(dyg, dyg_p * has_prev)).astype(BF16)
        duv_ref[...] = (wv0 * _shift_dn(dyv, dyv_n * has_next) + wv1 * dyv
                        + wv2 * _shift_up(dyv, dyv_p * has_prev)).astype(BF16)
        for ref, ups, mid, dns, dy in ((dwg_ref, g_up, g, g_dn, dyg), (dwv_ref, v_up, v, v_dn, dyv)):
            ref[0:1, :] += jnp.sum(ups * dy, axis=0, keepdims=True)
            ref[1:2, :] += jnp.sum(mid * dy, axis=0, keepdims=True)
            ref[2:3, :] += jnp.sum(dns * dy, axis=0, keepdims=True)
        dbg_ref[...] += jnp.sum(dyg, axis=0, keepdims=True)
        dbv_ref[...] += jnp.sum(dyv, axis=0, keepdims=True)

    gblk = pl.BlockSpec((R, W), lambda j, i: (i, j))
    vblk = pl.BlockSpec((R, W), lambda j, i: (i, nj + j))
    gprev, gnext = _halo_specs(R, W, T, lambda j, i: j)
    vprev, vnext = _halo_specs(R, W, T, lambda j, i: nj + j)
    half = jax.ShapeDtypeStruct((T, F), BF16)
    taps8, bias1 = jax.ShapeDtypeStruct((8, F), F32), jax.ShapeDtypeStruct((1, F), F32)
    return _pcall(
        body, name=name, grid=(nj, nblk), out_shape=(half, half, taps8, taps8, bias1, bias1),
        in_specs=[gblk, vblk, gprev, vprev, gnext, vnext, gblk, gprev, gnext,
                  pl.BlockSpec((3, W), lambda j, i: (0, j)), pl.BlockSpec((3, W), lambda j, i: (0, nj + j)),
                  pl.BlockSpec((1, W), lambda j, i: (0, j)), pl.BlockSpec((1, W), lambda j, i: (0, nj + j))],
        out_specs=(gblk, gblk, pl.BlockSpec((8, W), lambda j, i: (0, j)), pl.BlockSpec((8, W), lambda j, i: (0, j)),
                   pl.BlockSpec((1, W), lambda j, i: (0, j)), pl.BlockSpec((1, W), lambda j, i: (0, j))),
        args=(uf, uf, uf, uf, uf, uf, da, da, da, cw, cw, cb, cb), sem=("parallel", "arbitrary"), comm=comm)


WEIGHTS = ("c_ctx", "w_ada", "b_ada", "ln1_w", "ln2_w", "w_in", "hg_lb_logits", "hg_norm_w", "na_q_norm_w",
           "na_k_norm_w", "na_rpb", "na_out_norm_w", "cv_w", "cv_out_norm_w", "w_out", "w_up", "ffn_conv_w",
           "ffn_conv_b", "w_down")
BIG = ("w_ada", "w_in", "w_out", "w_up", "w_down")
SHARDED_SMALL = ("hg_lb_logits", "cv_w", "ffn_conv_w")


def _flat_rows(parts, dtype):
    flat, layout, off = [], [], 0
    for p in parts:
        layout.append((off, p.shape))
        flat.append(p.reshape(-1).astype(dtype))
        off += p.size
    pad = (-off) % (8 * LANE)
    if pad:
        flat.append(jnp.zeros((pad,), dtype))
    return jnp.concatenate(flat).reshape(-1, LANE), layout


def _unflat(buf, layout):
    v = buf.reshape(-1)
    return [v[off:off + int(np.prod(shape))].reshape(shape) for off, shape in layout]


def _lb_all(logits):
    sm = jax.nn.softmax(logits.astype(F32), axis=1)
    return jnp.cumsum(sm, axis=1) - sm[:, :1]


def _seg(ctx_vec, lat_vec):
    return jnp.broadcast_to(jnp.stack([ctx_vec, lat_vec])[:, None, :], (2, RT, ctx_vec.shape[0]))


def _shared(vec):
    return jnp.broadcast_to(vec[None, None, :], (1, RT, vec.shape[0]))


def kernel(x, c, ctx, c_ctx, w_ada, b_ada, ln1_w, ln2_w, w_in, hg_lb_logits, hg_norm_w, na_q_norm_w, na_k_norm_w, na_rpb, na_out_norm_w, cv_w, cv_out_norm_w, w_out, w_up, ffn_conv_w, ffn_conv_b, w_down, loss_target, m_c_ctx, m_w_ada, m_b_ada, m_ln1_w, m_ln2_w, m_w_in, m_hg_lb_logits, m_hg_norm_w, m_na_q_norm_w, m_na_k_norm_w, m_na_rpb, m_na_out_norm_w, m_cv_w, m_cv_out_norm_w, m_w_out, m_w_up, m_ffn_conv_w, m_ffn_conv_b, m_w_down, v_c_ctx, v_w_ada, v_b_ada, v_ln1_w, v_ln2_w, v_w_in, v_hg_lb_logits, v_hg_norm_w, v_na_q_norm_w, v_na_k_norm_w, v_na_rpb, v_na_out_norm_w, v_cv_w, v_cv_out_norm_w, v_w_out, v_w_up, v_ffn_conv_w, v_ffn_conv_b, v_w_down):
    W = dict(c_ctx=c_ctx, w_ada=w_ada, b_ada=b_ada, ln1_w=ln1_w, ln2_w=ln2_w, w_in=w_in, hg_lb_logits=hg_lb_logits,
             hg_norm_w=hg_norm_w, na_q_norm_w=na_q_norm_w, na_k_norm_w=na_k_norm_w, na_rpb=na_rpb,
             na_out_norm_w=na_out_norm_w, cv_w=cv_w, cv_out_norm_w=cv_out_norm_w, w_out=w_out, w_up=w_up,
             ffn_conv_w=ffn_conv_w, ffn_conv_b=ffn_conv_b, w_down=w_down)
    Mo = dict(c_ctx=m_c_ctx, w_ada=m_w_ada, b_ada=m_b_ada, ln1_w=m_ln1_w, ln2_w=m_ln2_w, w_in=m_w_in,
              hg_lb_logits=m_hg_lb_logits, hg_norm_w=m_hg_norm_w, na_q_norm_w=m_na_q_norm_w,
              na_k_norm_w=m_na_k_norm_w, na_rpb=m_na_rpb, na_out_norm_w=m_na_out_norm_w, cv_w=m_cv_w,
              cv_out_norm_w=m_cv_out_norm_w, w_out=m_w_out, w_up=m_w_up, ffn_conv_w=m_ffn_conv_w,
              ffn_conv_b=m_ffn_conv_b, w_down=m_w_down)
    Vo = dict(c_ctx=v_c_ctx, w_ada=v_w_ada, b_ada=v_b_ada, ln1_w=v_ln1_w, ln2_w=v_ln2_w, w_in=v_w_in,
              hg_lb_logits=v_hg_lb_logits, hg_norm_w=v_hg_norm_w, na_q_norm_w=v_na_q_norm_w,
              na_k_norm_w=v_na_k_norm_w, na_rpb=v_na_rpb, na_out_norm_w=v_na_out_norm_w, cv_w=v_cv_w,
              cv_out_norm_w=v_cv_out_norm_w, w_out=v_w_out, w_up=v_w_up, ffn_conv_w=v_ffn_conv_w,
              ffn_conv_b=v_ffn_conv_b, w_down=v_w_down)

    xi, yi, ci = _me()
    chip = 2 * xi + yi
    dev = 2 * chip + ci
    L, D = x.shape[1], x.shape[2]
    NC = ctx.shape[1]
    T = NC + L
    depth = w_in.shape[0]
    HGW, NAW, CVW = 4 * hg_lb_logits.shape[-1], na_out_norm_w.shape[-1], cv_out_norm_w.shape[-1]
    MIX = HGW + NAW + CVW
    INW, FF2 = 4 * w_in.shape[-1], 4 * w_up.shape[-1]
    F = FF2 // 2
    ADA = 4 * w_ada.shape[-1]
    assert NAW == 2 * HGW and INW == 5 * HGW + 3 * NAW + 3 * CVW and ADA == 6 * D and NC % 128 == 0
    assert L % GRID_W == 0 and T % CHUNK == 0 and depth == 2
    R = math.gcd(NC, 256)
    FW = 512 if F % 512 == 0 else LANE
    rows = L // GRID_W
    nh_hg, nh_na, nh_cv = HGW // LANE, NAW // LANE, CVW // LANE
    kcol = 3 * nh_hg
    vcol = kcol + nh_na
    gcol = vcol + nh_na + nh_hg
    qcol = gcol + nh_hg
    bcol = qcol + nh_na
    mix_na, mix_cv = nh_hg, nh_hg + nh_na

    small1, lay1 = _flat_rows([c[0], hg_lb_logits, cv_w, ffn_conv_w], F32)
    g1 = allgather8([small1], "gather_cond")[0]
    per_dev = [_unflat(g1[d], lay1) for d in range(8)]
    c_all = jnp.stack([p[0] for p in per_dev])
    lb_logits = jnp.concatenate([per_dev[2 * s][1] for s in range(4)], axis=-1)
    cvw_full = jnp.concatenate([per_dev[2 * s][2] for s in range(4)], axis=-1)
    fcw_full = jnp.concatenate([per_dev[2 * s][3] for s in range(4)], axis=-1)
    lb_all, lb_pull = jax.vjp(_lb_all, lb_logits)

    a16 = jnp.concatenate([c_all, c_ctx[None], jnp.zeros((7, D), F32)])
    s16 = _silu(a16)
    wcols = ADA // 4
    b_mine = lax.dynamic_slice_in_dim(b_ada, chip * wcols, wcols, axis=1)
    p_ada = jnp.stack([mm_nn(s16, w_ada[l][None], F32, f"ada_fwd_{l}") + b_mine[l][None] for l in range(depth)])
    g2 = allgather8([p_ada.reshape(depth * 16, wcols)], "gather_ada")[0].reshape(8, depth, 16, wcols)
    ada_rows = jnp.concatenate([g2[2 * s] for s in range(4)], axis=-1)
    ada = lax.dynamic_index_in_dim(ada_rows, dev, axis=1, keepdims=False)
    ada_c = ada_rows[:, 8]

    def half_rows(a):
        h = a.shape[0] // 2
        return lax.dynamic_slice_in_dim(a, ci * h, h, axis=0)

    proj = ("w_in", "w_out", "w_up", "w_down")
    wparts = [{n: half_rows(W[n][l]).astype(BF16) for n in proj} for l in range(depth)]

    def stacked(n, g):
        g = g.reshape(4, -1, g.shape[-1])
        return g.reshape(1, -1, g.shape[-1]) if n in ("w_out", "w_down") else g

    Wg = [{"w_in": stacked("w_in", allgather8([wparts[0]["w_in"]], "gather_w_in_0", hbm=True)[0])}, {}]
    fetch_plan = {"proj_in_0": [(0, "w_out"), (0, "w_down")], "na_fwd_0": [(0, "w_up")],
                  "ffn_up_0": [(1, "w_in"), (1, "w_out")], "ffn_mid_0": [(1, "w_up")], "ffn_down_0": [(1, "w_down")]}

    def fetching(name, call):
        items = fetch_plan.get(name, [])
        res = call(allgather8_comm([wparts[l][n] for l, n in items]) if items else None)
        res = list(res) if isinstance(res, (list, tuple)) else [res]
        for (l, n), g in zip(items, res[len(res) - len(items):]):
            Wg[l][n] = stacked(n, g)
        own = res[:len(res) - len(items)]
        return own[0] if len(own) == 1 else own

    xcat = jnp.concatenate([ctx[0], x[0]], axis=0)
    mods = []
    for l in range(depth):
        lat, con = jnp.split(ada[l], 6), jnp.split(ada_c[l], 6)
        mods.append(dict(sh1=_seg(con[0], lat[0]), sc1=_seg(con[1], lat[1]), g1=_seg(con[2], lat[2]),
                         sh2=_seg(con[3], lat[3]), sc2=_seg(con[4], lat[4]), g2=_seg(con[5], lat[5]),
                         ln1=_shared(ln1_w[l]), ln2=_shared(ln2_w[l])))
    bias_pull, saved = [], []
    x0 = xcat
    _, h = gate_norm(x0, None, None, mods[0]["ln1"], mods[0]["sh1"], mods[0]["sc1"], NC, R, "norm_in")
    for l in range(depth):
        md, wl = mods[l], Wg[l]
        u = fetching(f"proj_in_{l}", lambda cm: mm_nn(h, wl["w_in"], F32, f"proj_in_{l}", comm=cm))
        lbf, lbb = lb_all[0, l][None], lb_all[1, l][None]
        o_fw, st_fw = hgrn_fwd(u, lbf, False, 0, NC, HGW, f"hgrn_fw_{l}")
        o_bw, st_bw = hgrn_fwd(u, lbb, True, 1, NC, HGW, f"hgrn_bw_{l}")
        hgn = hg_norm_w[l][None]
        hg = hg_read(o_fw, o_bw, u, hgn, gcol, R, f"hg_read_{l}")
        bias, pull = jax.vjp(lambda r: na_bias_tables(r, rows), na_rpb[l])
        bias_pull.append(pull)
        qn, kn, on = na_q_norm_w[l][None], na_k_norm_w[l][None], na_out_norm_w[l][None]
        keys_n, vals_b = kv_prep(u, kn, kcol, vcol, NAW, R, f"kv_prep_{l}")
        na = fetching(f"na_fwd_{l}", lambda cm: na_fwd(u, keys_n, vals_b, qn, on, bias, qcol, NC, f"na_fwd_{l}", comm=cm))
        cvw_l, cvo = cvw_full[l], cv_out_norm_w[l][None]
        cv = short_conv(u, cvw_l, cvo, bcol, NC, CVW, R, name=f"short_conv_{l}")
        mix = jnp.concatenate([hg, na, cv], axis=1)
        m1 = mm_nn(mix, wl["w_out"], F32, f"proj_out_{l}")
        x1, h2 = gate_norm(x0, m1, md["g1"], md["ln2"], md["sh2"], md["sc2"], NC, R, f"gate_norm_mid_{l}")
        uf = fetching(f"ffn_up_{l}", lambda cm: mm_nn(h2, wl["w_up"], F32, f"ffn_up_{l}", comm=cm))
        fcw_l, fcb_l = fcw_full[l], ffn_conv_b[l][None]
        a = fetching(f"ffn_mid_{l}", lambda cm: ffn_mid(uf, fcw_l, fcb_l, NC, R, FW, f"ffn_mid_{l}", comm=cm))
        m2 = fetching(f"ffn_down_{l}", lambda cm: mm_nn(a, wl["w_down"], F32, f"ffn_down_{l}", comm=cm))
        saved.append(dict(x0=x0, h=h, u=u, o_fw=o_fw, o_bw=o_bw, st_fw=st_fw, st_bw=st_bw, bias=bias, mix=mix,
                          m1=m1, x1=x1, h2=h2, uf=uf, a=a, m2=m2, lbf=lbf, lbb=lbb, keys_n=keys_n, vals_b=vals_b))
        if l + 1 < depth:
            nx = mods[l + 1]
            x0, h = gate_norm(x1, m2, md["g2"], nx["ln1"], nx["sh1"], nx["sc1"], NC, R, f"gate_norm_end_{l}")
    sv, md = saved[-1], mods[-1]
    loss_terms, d_x1, d_m2, d_g2 = gate_loss(sv["x1"], sv["m2"], md["g2"], loss_target[0], NC, R, "gate_loss")
    loss = lax.psum(jnp.sum(loss_terms), ("x", "y", "c"))

    big_grads = [dict() for _ in range(depth)]
    core = ci.astype(jnp.int32).reshape(1)
    pairs, quads = {}, {}
    reduce_plan = {"ffn_down_bwd_0": [(1, "w_down")], "ffn_mid_bwd_0": [(1, "w_up")],
                   "ffn_up_bwd_0": [(1, "w_in"), (1, "w_out")], "na_bwd_0": [(0, "w_down"), (0, "w_up"), (0, "w_out")],
                   "proj_in_bwd_0": [(0, "w_in")]}

    def swapping(l, names, call):
        parts = [big_grads[l][n] for n in names]
        res = list(call(swap_halves_comm(parts)))
        for n, p, g in zip(names, parts, res[len(res) - len(names):]):
            pairs[(l, n)] = pair_sum(p, g, core, f"reduce_pair_sum_{n}_{l}")
        own = res[:len(res) - len(names)]
        return own[0] if len(own) == 1 else own

    def stage(l, names):
        swapping(l, names, lambda cm: _run_comm(cm, f"reduce_sibling_{l}_{names[0]}"))

    def reducing(name, call):
        items = reduce_plan.get(name, [])
        res = call(chip_alltoall_comm([pairs[k] for k in items]) if items else None)
        res = list(res) if isinstance(res, (list, tuple)) else [res]
        for k, q in zip(items, res[len(res) - len(items):]):
            quads[k] = q
        own = res[:len(res) - len(items)]
        return own[0] if len(own) == 1 else own

    small = [dict() for _ in range(depth)]
    d_ada = [None] * depth
    d_lb = [None] * depth
    for l in reversed(range(depth)):
        sv, md, wl = saved[l], mods[l], Wg[l]
        u, uf = sv["u"], sv["uf"]
        big_grads[l]["w_down"] = mm_tn(sv["a"], d_m2, 1, BF16, f"grad_w_down_{l}").reshape(4, F // 4, D)
        d_a = reducing(f"ffn_down_bwd_{l}", lambda cm: mm_nt(d_m2, wl["w_down"], F32, f"ffn_down_bwd_{l}", comm=cm))
        fcw_l, fcb_l = fcw_full[l], ffn_conv_b[l][None]
        dug, duv, dwg, dwv, dbg, dbv = reducing(
            f"ffn_mid_bwd_{l}", lambda cm: ffn_mid_bwd(uf, fcw_l, fcb_l, d_a, NC, R, FW, f"ffn_mid_bwd_{l}", comm=cm))
        d_uf = jnp.concatenate([dug, duv], axis=1)
        small[l]["ffn_conv_w"] = jnp.concatenate([dwg[:3], dwv[:3]], axis=1)
        small[l]["ffn_conv_b"] = jnp.concatenate([dbg[0], dbv[0]])
        big_grads[l]["w_up"] = mm_tn(sv["h2"], d_uf, 4, BF16, f"grad_w_up_{l}")
        d_h2 = reducing(f"ffn_up_bwd_{l}", lambda cm: mm_nt(d_uf, wl["w_up"], F32, f"ffn_up_bwd_{l}", comm=cm))
        d_x0, d_m1, dg1, dln2, dsh2, dsc2 = gate_norm_bwd(sv["x0"], sv["m1"], md["g1"], md["ln2"], md["sh2"], md["sc2"],
                                                          d_x1, d_h2, NC, R, f"gate_norm_mid_bwd_{l}")
        big_grads[l]["w_out"] = mm_tn(sv["mix"], d_m1, 1, BF16, f"grad_w_out_{l}").reshape(4, MIX // 4, D)
        if l == 0:
            d_mix = swapping(0, ("w_down", "w_up", "w_out"),
                             lambda cm: mm_nt(d_m1, wl["w_out"], F32, f"proj_out_bwd_{l}", comm=cm))
        else:
            d_mix = mm_nt(d_m1, wl["w_out"], F32, f"proj_out_bwd_{l}")
        hgn = hg_norm_w[l][None]
        d_o, d_hgg, d_hgn = hg_read_bwd(sv["o_fw"], sv["o_bw"], u, hgn, d_mix, gcol, 0, R, f"hg_read_bwd_{l}")
        dzf, dvf, dqf, dlbf = hgrn_bwd(u, sv["lbf"], sv["st_fw"], d_o, False, 0, NC, HGW, f"hgrn_fw_bwd_{l}")
        dzb, dvb, dqb, dlbb = hgrn_bwd(u, sv["lbb"], sv["st_bw"], d_o, True, 1, NC, HGW, f"hgrn_bw_bwd_{l}")
        d_lb[l] = (dlbf[0], dlbb[0])
        qn, kn, on = na_q_norm_w[l][None], na_k_norm_w[l][None], na_out_norm_w[l][None]
        d_nq, d_keys_n, d_nv, d_bias, d_qn, d_on = reducing(
            f"na_bwd_{l}", lambda cm: na_bwd(u, sv["keys_n"], sv["vals_b"], qn, on, sv["bias"], d_mix, qcol, mix_na, NC,
                                            f"na_bwd_{l}", comm=cm))
        d_nk, d_kn = kv_prep_bwd(u, kn, d_keys_n, kcol, NAW, R, f"kv_prep_bwd_{l}")
        cvw_l, cvo = cvw_full[l], cv_out_norm_w[l][None]
        d_cb, d_cy, d_cvo = short_conv(u, cvw_l, cvo, bcol, NC, CVW, R, bwd_dout=d_mix, ocol=mix_cv,
                                       name=f"short_conv_bwd_{l}")
        gcv = _head_group(nh_cv, bcol + nh_cv, bcol + 2 * nh_cv)
        d_cc, d_cvv, d_cvw = conv3_bwd(d_cy, u, cvw_l, NC, R, gcv * LANE,
                                       prod_cols=((bcol + nh_cv) // gcv, (bcol + 2 * nh_cv) // gcv),
                                       name=f"short_conv_taps_bwd_{l}")
        d_u = jnp.concatenate([dzf, dzb, dvf + dvb, d_nk, d_nv, dqf + dqb, d_hgg, d_nq, d_cb, d_cc, d_cvv],
                              axis=1).astype(BF16)
        big_grads[l]["w_in"] = mm_tn(sv["h"], d_u, 4, BF16, f"grad_w_in_{l}")
        if l > 0:
            d_h = swapping(l, proj, lambda cm: mm_nt(d_u, wl["w_in"], F32, f"proj_in_bwd_{l}", comm=cm))
        else:
            stage(0, ("w_in",))
            d_h = reducing(f"proj_in_bwd_{l}", lambda cm: mm_nt(d_u, wl["w_in"], F32, f"proj_in_bwd_{l}", comm=cm))
        small[l].update(hg_norm_w=d_hgn.sum(0)[0], na_q_norm_w=d_qn.sum(0)[0], na_k_norm_w=d_kn.sum(0)[0],
                        na_out_norm_w=d_on.reshape(-1), na_rpb=bias_pull[l](d_bias)[0], cv_w=d_cvw[:3],
                        cv_out_norm_w=d_cvo.reshape(-1), ln2_w=dln2.sum((0, 1)))
        if l > 0:
            pv, pm = saved[l - 1], mods[l - 1]
            d_x1, d_m2, dg2_prev, dln1, dsh1, dsc1 = gate_norm_bwd(pv["x1"], pv["m2"], pm["g2"], md["ln1"], md["sh1"],
                                                                   md["sc1"], d_x0, d_h, NC, R, f"gate_norm_end_bwd_{l - 1}")
        else:
            d_xin, _, _, dln1, dsh1, dsc1 = gate_norm_bwd(sv["x0"], None, None, md["ln1"], md["sh1"], md["sc1"], d_x0, d_h,
                                                          NC, R, "norm_in_bwd")
        small[l]["ln1_w"] = dln1.sum((0, 1))
        this_g2 = d_g2
        vecs = [v.sum(1) for v in (dsh1, dsc1, dg1, dsh2, dsc2, this_g2)]
        d_ada[l] = jnp.stack([jnp.concatenate([v[s] for v in vecs]) for s in (0, 1)])
        if l > 0:
            d_g2 = dg2_prev
    grad_x = d_xin[NC:][None]
    d_logits = lb_pull(jnp.stack([jnp.stack([d_lb[l][k] for l in range(depth)]) for k in (0, 1)]))[0]

    rep_names = ("ln1_w", "ln2_w", "hg_norm_w", "na_q_norm_w", "na_k_norm_w", "na_rpb", "na_out_norm_w",
                 "cv_out_norm_w", "ffn_conv_b", "cv_w", "ffn_conv_w")
    parts3 = [jnp.stack([small[l][n] for l in range(depth)]) for n in rep_names]
    parts3 += [d_logits, jnp.stack([d_ada[l][0] for l in range(depth)]), jnp.stack([d_ada[l][1] for l in range(depth)])]
    buf3, lay3 = _flat_rows(parts3, F32)
    g3 = allgather8([buf3], "gather_small_grads")[0]
    tot3 = _unflat(sum_leading(g3, F32, "sum_small_grads"), lay3)
    gsm = dict(zip(rep_names, tot3[:len(rep_names)]))
    gsm["hg_lb_logits"] = tot3[len(rep_names)]
    dctx_tot, dlat_tot = tot3[-2], tot3[-1]
    dlat_each = jnp.stack([_unflat(g3[d], lay3)[-1] for d in range(8)], axis=1)
    grads = {n: gsm[n].reshape(W[n].shape) for n in rep_names if n not in SHARDED_SMALL}
    for n in SHARDED_SMALL:
        wl_ = W[n].shape[-1]
        grads[n] = lax.dynamic_slice_in_dim(gsm[n], chip * wl_, wl_, axis=gsm[n].ndim - 1)
    grads["b_ada"] = dctx_tot + dlat_tot

    ds16 = jnp.zeros((16, D), F32)
    gw_ada = []
    for l in range(depth):
        dm = jnp.concatenate([dlat_each[l], dctx_tot[l][None], jnp.zeros((7, ADA), F32)])
        dm = lax.dynamic_slice_in_dim(dm, chip * wcols, wcols, axis=1)
        gw_ada.append(mm_tn(s16, dm, 1, F32, f"grad_w_ada_{l}")[0])
        ds16 = ds16 + mm_nt(dm, w_ada[l][None], F32, f"ada_bwd_{l}")
    g4 = allgather8([ds16[8:16]], "gather_cond_grad")[0]
    d_scc = g4[0, 0] + g4[2, 0] + g4[4, 0] + g4[6, 0]
    sg = jax.nn.sigmoid(c_ctx)
    grads["c_ctx"] = d_scc * (sg * (1.0 + c_ctx * (1.0 - sg)))

    keys = [(l, n) for l in range(depth) for n in proj]
    mine = [sum_leading(quads[(l, n)], F32, f"reduce_chip_sum_{n}_{l}") for l, n in keys]

    delta, new_m, new_v = {}, {}, {}
    for n in BIG:
        shp = W[n].shape
        two = lambda a: a.reshape(-1, shp[-1])
        if n == "w_ada":
            g_, d_, m_, v_, *other = adamw(two(W[n]), gw_ada, two(Mo[n]), two(Vo[n]), f"adamw_{n}",
                                           comm=share_halves_comm(mine))
            mine_by, other_by = {k: [None] * depth for k in proj}, {k: [None] * depth for k in proj}
            for (l, k), a, b in zip(keys, mine, other):
                mine_by[k][l], other_by[k][l] = a, b
        else:
            g_, d_, m_, v_ = adamw_halves(two(W[n]), mine_by[n], other_by[n], two(Mo[n]), two(Vo[n]), core, f"adamw_{n}")
        grads[n], delta[n], new_m[n], new_v[n] = g_.reshape(shp), d_.reshape(shp), m_.reshape(shp), v_.reshape(shp)
    smalls = [n for n in WEIGHTS if n not in BIG]
    pw, lay_s = _flat_rows([W[n] for n in smalls], F32)
    pg, _ = _flat_rows([grads[n] for n in smalls], F32)
    pm, _ = _flat_rows([Mo[n] for n in smalls], F32)
    pvv, _ = _flat_rows([Vo[n] for n in smalls], F32)
    _, d_, m_, v_ = adamw(pw, [pg], pm, pvv, "adamw_small")
    for n, dd, mm_, vv in zip(smalls, _unflat(d_, lay_s), _unflat(m_, lay_s), _unflat(v_, lay_s)):
        delta[n], new_m[n], new_v[n] = dd, mm_, vv

    return (loss, grad_x, *[grads[n] for n in WEIGHTS], *[delta[n] for n in WEIGHTS],
            *[new_m[n] for n in WEIGHTS], *[new_v[n] for n in WEIGHTS])
```

```python
import functools
import math

import numpy as np
import jax
import jax.numpy as jnp
from jax import lax
from jax.experimental import pallas as pl
from jax.experimental.pallas import tpu as pltpu

F32 = jnp.float32
BF16 = jnp.bfloat16
MESH = pl.DeviceIdType.MESH
ANY = pl.BlockSpec(memory_space=pl.ANY)
VMEM_SPEC = pl.BlockSpec(memory_space=pltpu.VMEM)

LANE = 128
CHUNK = 64
SUB = 16
GRID_W = 64
WIN_R = 8
WIN_C = 16
EPS = 1e-6
F_FLOOR = 1e-30
NEG_INF = -1e30
EXP_CLAMP = 80.0
ATTN_SCALE = LANE ** -0.5
VMEM_LIMIT = 56 * 1024 * 1024
ADAM_LR, ADAM_B1, ADAM_B2, ADAM_EPS, ADAM_WD, ADAM_STEP = 0.001, 0.9, 0.999, 1e-08, 0.01, 10


def _cp(*sem):
    return pltpu.CompilerParams(dimension_semantics=sem or None, vmem_limit_bytes=VMEM_LIMIT)


def _me():
    return lax.axis_index("x"), lax.axis_index("y"), lax.axis_index("c")


def allgather8(blocks, name, hbm=False):
    na = len(blocks)
    comm = allgather8_comm(blocks)

    def body(*refs):
        comm["start"](refs[:na], refs[na:2 * na], refs[2 * na:])
        comm["finish"](refs[:na], refs[na:2 * na], refs[2 * na:])

    spec = ANY if hbm else VMEM_SPEC
    return pl.pallas_call(
        body, name=name, out_shape=comm["outs"], in_specs=[spec] * na, out_specs=[spec] * na,
        scratch_shapes=comm["scratch"], compiler_params=pltpu.CompilerParams(vmem_limit_bytes=VMEM_LIMIT),
    )(*blocks)


def allgather8_comm(blocks):
    na = len(blocks)

    def parts(x_refs, out_refs, sems):
        send_sems, recv_sems, local_sems = sems
        x, y, c = _me()
        me, sibling = (x, y, c), (x, y, 1 - c)
        chips = [(1 - x, y), (x, 1 - y), (1 - x, 1 - y)]

        def rows(a, px, py, pc):
            return out_refs[a].at[4 * px + 2 * py + pc]

        def copy(a, k, blk, to, src=None):
            return pltpu.make_async_remote_copy(
                src_ref=rows(a, *blk) if src is None else src, dst_ref=rows(a, *blk),
                send_sem=send_sems.at[a, k], recv_sem=recv_sems.at[a, k], device_id=to, device_id_type=MESH)

        mine = [pltpu.make_async_copy(x_refs[a], rows(a, *me), local_sems.at[a]) for a in range(na)]
        first = []
        for a in range(na):
            first.append(copy(a, 0, me, sibling, src=x_refs[a]))
            first += [copy(a, 1 + j, me, (*chip, c), src=x_refs[a]) for j, chip in enumerate(chips)]
        return c, me, sibling, chips, copy, mine, first

    def start(x_refs, out_refs, sems):
        _, _, _, _, _, mine, first = parts(x_refs, out_refs, sems)
        for cp in mine + first:
            cp.start()

    def finish(x_refs, out_refs, sems):
        c, me, sibling, chips, copy, mine, first = parts(x_refs, out_refs, sems)
        passed = []
        for j, chip in enumerate(chips):
            for a in range(na):
                copy(a, 1 + j, (*chip, c), me).wait_recv()
                passed.append(copy(a, 4 + j, (*chip, c), sibling))
                passed[-1].start()
        for a in range(na):
            copy(a, 0, sibling, me).wait_recv()
            for j, chip in enumerate(chips):
                copy(a, 4 + j, (*chip, 1 - c), me).wait_recv()
        for cp in first + passed:
            cp.wait_send()
        for cp in mine:
            cp.wait()

    return dict(ins=list(blocks), outs=[jax.ShapeDtypeStruct((8,) + b.shape, b.dtype) for b in blocks],
                scratch=[pltpu.SemaphoreType.DMA((na, 7)), pltpu.SemaphoreType.DMA((na, 7)),
                         pltpu.SemaphoreType.DMA((na,))], start=start, finish=finish)


def swap_halves(gs, name):
    return _run_comm(swap_halves_comm(gs), name)


def _run_comm(comm, name):
    na = len(comm["ins"])

    def body(*refs):
        comm["start"](refs[:na], refs[na:2 * na], refs[2 * na:])
        comm["finish"](refs[:na], refs[na:2 * na], refs[2 * na:])

    return pl.pallas_call(body, name=name, out_shape=comm["outs"], in_specs=[ANY] * na, out_specs=[ANY] * na,
                          scratch_shapes=comm["scratch"])(*comm["ins"])


def swap_halves_comm(gs):
    na = len(gs)
    hrs = [g.shape[1] // 2 for g in gs]

    def copies(g_refs, o_refs, sems):
        send_sems, recv_sems = sems
        x, y, c = _me()
        cps = []
        for a in range(na):
            for s in range(4):
                src = g_refs[a].at[s, pl.ds(pl.multiple_of((1 - c) * hrs[a], 16), hrs[a]), :]
                cps.append(pltpu.make_async_remote_copy(
                    src_ref=src, dst_ref=o_refs[a].at[s], send_sem=send_sems.at[a, s], recv_sem=recv_sems.at[a, s],
                    device_id=(x, y, 1 - c), device_id_type=MESH))
        return cps

    def start(g_refs, o_refs, sems):
        for cp in copies(g_refs, o_refs, sems):
            cp.start()

    def finish(g_refs, o_refs, sems):
        for cp in copies(g_refs, o_refs, sems):
            cp.wait()

    return dict(ins=list(gs), outs=[jax.ShapeDtypeStruct((4, hrs[a], gs[a].shape[2]), gs[a].dtype) for a in range(na)],
                scratch=[pltpu.SemaphoreType.DMA((na, 4)), pltpu.SemaphoreType.DMA((na, 4))], start=start, finish=finish)


def chip_alltoall(gs, name):
    na = len(gs)
    comm = chip_alltoall_comm(gs)

    def body(*refs):
        comm["start"](refs[:na], refs[na:2 * na], refs[2 * na:])
        comm["finish"](refs[:na], refs[na:2 * na], refs[2 * na:])

    return pl.pallas_call(body, name=name, out_shape=comm["outs"], in_specs=[ANY] * na, out_specs=[ANY] * na,
                          scratch_shapes=comm["scratch"])(*gs)


def chip_alltoall_comm(gs):
    na = len(gs)

    def copies(g_refs, o_refs, sems):
        send_sems, recv_sems, local_sems = sems
        x, y, c = _me()
        mine = 2 * x + y
        cps = []
        for a in range(na):
            cps.append(pltpu.make_async_copy(g_refs[a].at[mine], o_refs[a].at[mine], local_sems.at[a]))
            for k, (px, py) in enumerate([(1 - x, y), (x, 1 - y), (1 - x, 1 - y)]):
                cps.append(pltpu.make_async_remote_copy(
                    src_ref=g_refs[a].at[2 * px + py], dst_ref=o_refs[a].at[mine], send_sem=send_sems.at[a, k],
                    recv_sem=recv_sems.at[a, k], device_id=(px, py, c), device_id_type=MESH))
        return cps

    def start(g_refs, o_refs, sems):
        for cp in copies(g_refs, o_refs, sems):
            cp.start()

    def finish(g_refs, o_refs, sems):
        for cp in copies(g_refs, o_refs, sems):
            cp.wait()

    return dict(ins=list(gs), outs=[jax.ShapeDtypeStruct(g.shape, g.dtype) for g in gs],
                scratch=[pltpu.SemaphoreType.DMA((na, 3)), pltpu.SemaphoreType.DMA((na, 3)),
                         pltpu.SemaphoreType.DMA((na,))], start=start, finish=finish)


def share_halves_comm(vs):
    na = len(vs)

    def copies(v_refs, o_refs, sems):
        send_sems, recv_sems = sems
        x, y, c = _me()
        return [pltpu.make_async_remote_copy(
            src_ref=v_refs[a], dst_ref=o_refs[a], send_sem=send_sems.at[a], recv_sem=recv_sems.at[a],
            device_id=(x, y, 1 - c), device_id_type=MESH) for a in range(na)]

    def start(v_refs, o_refs, sems):
        for cp in copies(v_refs, o_refs, sems):
            cp.start()

    def finish(v_refs, o_refs, sems):
        for cp in copies(v_refs, o_refs, sems):
            cp.wait()

    return dict(ins=list(vs), outs=[jax.ShapeDtypeStruct(v.shape, v.dtype) for v in vs],
                scratch=[pltpu.SemaphoreType.DMA((na,)), pltpu.SemaphoreType.DMA((na,))], start=start, finish=finish)


def _row_block(rows, cap):
    rb = math.gcd(rows, cap)
    return rb if rb % 8 == 0 else rows


def sum_leading(x, out_dtype, name):
    n, r, c = x.shape
    rb = _row_block(r, 1024)

    def body(x_ref, o_ref):
        acc = x_ref[0].astype(F32)
        for k in range(1, n):
            acc = acc + x_ref[k].astype(F32)
        o_ref[...] = acc.astype(o_ref.dtype)

    return pl.pallas_call(
        body, name=name, grid=(r // rb,), out_shape=jax.ShapeDtypeStruct((r, c), out_dtype),
        in_specs=[pl.BlockSpec((n, rb, c), lambda i: (0, i, 0))], out_specs=pl.BlockSpec((rb, c), lambda i: (i, 0)),
        compiler_params=_cp("parallel"),
    )(x)


def pair_sum(g, got, core, name):
    _, r2, n = g.shape
    hr = r2 // 2
    rb = math.gcd(hr, 512)
    nb = hr // rb

    def body(c_ref, a_ref, b_ref, o_ref):
        o_ref[...] = (a_ref[...].astype(F32) + b_ref[...].astype(F32)).astype(o_ref.dtype)

    spec = pl.BlockSpec((None, rb, n), lambda s, i, c_ref: (s, i, 0))
    return pl.pallas_call(
        body, name=name, out_shape=jax.ShapeDtypeStruct((4, hr, n), g.dtype),
        grid_spec=pltpu.PrefetchScalarGridSpec(
            num_scalar_prefetch=1, grid=(4, nb),
            in_specs=[pl.BlockSpec((None, rb, n), lambda s, i, c_ref: (s, c_ref[0] * nb + i, 0)), spec],
            out_specs=spec),
        compiler_params=_cp("parallel", "parallel"),
    )(core, g, got)


def adamw(w, gs, m, v, name, comm=None):
    rows, c = w.shape
    ng = len(gs)
    r = rows // ng
    rb = _row_block(r, 128 if c > 2048 else 256 if c > 1024 else 1024)
    nb = r // rb
    bc1 = 1.0 - ADAM_B1 ** ADAM_STEP
    bc2 = 1.0 - ADAM_B2 ** ADAM_STEP

    def body(w_ref, *refs):
        g_refs, (m_ref, v_ref, g_out, d_ref, nm_ref, nv_ref) = refs[:ng], refs[ng:]
        part = pl.program_id(0) // nb
        gg = g_refs[0][...]
        for k in range(1, ng):
            gg = jnp.where(part == k, g_refs[k][...], gg)
        nm = ADAM_B1 * m_ref[...] + (1.0 - ADAM_B1) * gg
        nv = ADAM_B2 * v_ref[...] + (1.0 - ADAM_B2) * (gg * gg)
        g_out[...] = gg
        d_ref[...] = -ADAM_LR * ((nm / bc1) / (jnp.sqrt(nv / bc2) + ADAM_EPS) + ADAM_WD * w_ref[...])
        nm_ref[...] = nm
        nv_ref[...] = nv

    spec = pl.BlockSpec((rb, c), lambda i: (i, 0))
    gspecs = [pl.BlockSpec((rb, c), functools.partial(lambda k, i: (jnp.clip(i - k * nb, 0, nb - 1), 0), k))
              for k in range(ng)]
    sds = jax.ShapeDtypeStruct((rows, c), F32)
    return _pcall(body, name=name, grid=(ng * nb,), out_shape=(sds,) * 4, in_specs=[spec] + gspecs + [spec, spec],
                  out_specs=(spec,) * 4, args=(w, *gs, m, v), sem=("parallel",), comm=comm)


def adamw_halves(w, mine, other, m, v, core, name):
    rows, c = w.shape
    nl = len(mine)
    hr = rows // (2 * nl)
    rb = _row_block(hr, 128 if c > 2048 else 256 if c > 1024 else 1024)
    nb = hr // rb
    bc1 = 1.0 - ADAM_B1 ** ADAM_STEP
    bc2 = 1.0 - ADAM_B2 ** ADAM_STEP

    def body(c_ref, w_ref, *refs):
        mine_refs, other_refs = refs[:nl], refs[nl:2 * nl]
        m_ref, v_ref, g_out, d_ref, nm_ref, nv_ref = refs[2 * nl:]
        part = pl.program_id(0) // nb
        layer, half = part // 2, part % 2
        gg = jnp.where(half == c_ref[0], mine_refs[0][...], other_refs[0][...])
        for k in range(1, nl):
            gg = jnp.where(layer == k, jnp.where(half == c_ref[0], mine_refs[k][...], other_refs[k][...]), gg)
        nm = ADAM_B1 * m_ref[...] + (1.0 - ADAM_B1) * gg
        nv = ADAM_B2 * v_ref[...] + (1.0 - ADAM_B2) * (gg * gg)
        g_out[...] = gg
        d_ref[...] = -ADAM_LR * ((nm / bc1) / (jnp.sqrt(nv / bc2) + ADAM_EPS) + ADAM_WD * w_ref[...])
        nm_ref[...] = nm
        nv_ref[...] = nv

    spec = pl.BlockSpec((rb, c), lambda i, c_ref: (i, 0))
    gspecs = [pl.BlockSpec((rb, c), functools.partial(
        lambda k, i, c_ref: (jnp.clip(i - 2 * k * nb, 0, 2 * nb - 1) % nb, 0), k)) for k in range(nl)]
    sds = jax.ShapeDtypeStruct((rows, c), F32)
    return pl.pallas_call(
        body, name=name, out_shape=(sds,) * 4,
        grid_spec=pltpu.PrefetchScalarGridSpec(
            num_scalar_prefetch=1, grid=(2 * nl * nb,), in_specs=[spec] + gspecs + gspecs + [spec, spec],
            out_specs=(spec,) * 4),
        compiler_params=_cp("parallel"),
    )(core, w, *mine, *other, m, v)


def _pick(n, prefs):
    for p in prefs:
        if n % p == 0:
            return p
    return n


def _hosted(body, n_in, n_out, comm, first, last):
    if comm is None:
        return body
    k, ns = len(comm["ins"]), len(comm["scratch"])

    def wrapped(*refs):
        ins, cins = refs[:n_in], refs[n_in:n_in + k]
        outs, couts = refs[n_in + k:n_in + k + n_out], refs[n_in + k + n_out:n_in + 2 * k + n_out]
        rest = refs[n_in + 2 * k + n_out:]
        scratch, sems = rest[:len(rest) - ns], rest[len(rest) - ns:]

        @pl.when(first())
        def _():
            comm["start"](cins, couts, sems)

        body(*ins, *outs, *scratch)

        @pl.when(last())
        def _():
            comm["finish"](cins, couts, sems)

    return wrapped


def _comm_extras(comm):
    if comm is None:
        return [], [], [], []
    return list(comm["ins"]), [ANY] * len(comm["ins"]), list(comm["outs"]), list(comm["scratch"])


def _mm_body(dims, nk, out_dtype):
    def body(a_ref, b_ref, o_ref, acc=None):
        kk = pl.program_id(2)
        part = lax.dot_general(a_ref[...].astype(BF16), b_ref[...].astype(BF16), (dims, ((), ())),
                               preferred_element_type=F32)
        if nk == 1:
            o_ref[...] = part.astype(out_dtype)
        else:
            @pl.when(kk == 0)
            def _():
                acc[...] = part

            @pl.when(kk > 0)
            def _():
                acc[...] += part

            @pl.when(kk == nk - 1)
            def _():
                o_ref[...] = acc[...].astype(out_dtype)
    return body


def _acc(nk, shape):
    return [pltpu.VMEM(shape, F32)] if nk > 1 else []


def mm_nn(a, w, out_dtype, name, comm=None):
    M, K = a.shape
    S, _, Ns = w.shape
    tm = _pick(M, (1088, 1024, 512, 256, 128))
    tn = _pick(Ns, (1024, 896, 1408, 512, 256, 128))
    tk = _pick(K, (2816, 2048, 1408, 1024, 512, 256, 128))
    nps, nk = Ns // tn, K // tk
    grid = (S * nps, M // tm, nk)
    ids = lambda: [pl.program_id(d) for d in range(3)]
    first = lambda: functools.reduce(jnp.logical_and, [p == 0 for p in ids()])
    last = lambda: functools.reduce(jnp.logical_and, [p == g - 1 for p, g in zip(ids(), grid)])
    cin, cspec, cout, csem = _comm_extras(comm)
    out = pl.pallas_call(
        _hosted(_mm_body(((1,), (0,)), nk, out_dtype), 2, 1, comm, first, last), name=name, grid=grid,
        out_shape=[jax.ShapeDtypeStruct((M, S * Ns), out_dtype)] + cout,
        in_specs=[pl.BlockSpec((tm, tk), lambda j, i, k: (i, k)),
                  pl.BlockSpec((None, tk, tn), lambda j, i, k: (j // nps, k, j % nps))] + cspec,
        out_specs=[pl.BlockSpec((tm, tn), lambda j, i, k: (i, j))] + cspec,
        scratch_shapes=_acc(nk, (tm, tn)) + csem,
        compiler_params=_cp(*(("arbitrary",) * 3 if comm else ("parallel", "parallel", "arbitrary"))),
    )(a, w, *cin)
    return out[0] if comm is None else out


def mm_nt(dy, w, out_dtype, name, comm=None):
    M, N = dy.shape
    S, K, Ns = w.shape
    tm = _pick(M, (1088, 1024, 512, 256, 128))
    tn = _pick(K, (1408, 1024, 512, 256, 128))
    tk = _pick(Ns, (2816, 2048, 1792, 1408, 1024, 896, 512, 256, 128))
    kps, nk = Ns // tk, N // tk
    out = _pcall(
        _mm_body(((1,), (1,)), nk, out_dtype), name=name, grid=(K // tn, M // tm, nk),
        out_shape=[jax.ShapeDtypeStruct((M, K), out_dtype)],
        in_specs=[pl.BlockSpec((tm, tk), lambda j, i, k: (i, k)),
                  pl.BlockSpec((None, tn, tk), lambda j, i, k: (k // kps, j, k % kps))],
        out_specs=[pl.BlockSpec((tm, tn), lambda j, i, k: (i, j))], args=(dy, w),
        scratch=_acc(nk, (tm, tn)), sem=("parallel", "parallel", "arbitrary"), comm=comm)
    return out[0] if comm is None else out


def mm_tn(a, dy, S, out_dtype, name):
    M, K = a.shape
    N = dy.shape[1]
    Ns = N // S
    to = _pick(K, (1024, 512, 256, 128))
    tn = _pick(Ns, (2816, 2048, 1792, 1408, 1024, 896, 512, 256, 128))
    tk = _pick(M, (1088, 1024, 512, 256, 128))
    nps, nk = Ns // tn, M // tk
    return pl.pallas_call(
        _mm_body(((0,), (0,)), nk, out_dtype), name=name, grid=(K // to, S * nps, nk),
        out_shape=jax.ShapeDtypeStruct((S, K, Ns), out_dtype),
        in_specs=[pl.BlockSpec((tk, to), lambda i, j, k: (k, i)),
                  pl.BlockSpec((tk, tn), lambda i, j, k: (k, j))],
        out_specs=pl.BlockSpec((None, to, tn), lambda i, j, k: (j // nps, i, j % nps)),
        scratch_shapes=_acc(nk, (to, tn)), compiler_params=_cp("parallel", "parallel", "arbitrary"),
    )(a, dy)


_DIMS = {"nn": ((1,), (0,)), "nt": ((1,), (1,)), "tn": ((0,), (0,))}


def _dot(a, b, mode):
    return lax.dot_general(a.astype(BF16), b.astype(BF16), (_DIMS[mode], ((), ())), preferred_element_type=F32)


@functools.partial(jax.custom_vjp, nondiff_argnums=(2,))
def mmf(a, b, mode):
    return _dot(a, b, mode)


def _mmf_fwd(a, b, mode):
    return _dot(a, b, mode), (a, b)


def _mmf_bwd(mode, res, ct):
    a, b = res
    if mode == "nn":
        return _dot(ct, b, "nt"), _dot(a, ct, "tn")
    if mode == "nt":
        return _dot(ct, b, "nn"), _dot(ct, a, "tn")
    return _dot(b, ct, "nt"), _dot(a, ct, "nn")


mmf.defvjp(_mmf_fwd, _mmf_bwd)


def _dot_hi(m, g):
    return jnp.dot(m, g, precision=lax.Precision.HIGHEST, preferred_element_type=F32)


@jax.custom_vjp
def cumdot(m, mt, g):
    return _dot_hi(m, g)


def _cumdot_fwd(m, mt, g):
    return _dot_hi(m, g), (m, mt)


def _cumdot_bwd(res, ct):
    m, mt = res
    return jnp.zeros_like(m), jnp.zeros_like(mt), _dot_hi(mt, ct)


cumdot.defvjp(_cumdot_fwd, _cumdot_bwd)


def _rms(x, w):
    return x * lax.rsqrt(jnp.mean(x * x, axis=-1, keepdims=True) + EPS) * w


def _silu(x):
    return x * jax.nn.sigmoid(x)


RT = 16


def _gn_math(has_gate, x, m, gate, lnw, shift, scale):
    xn = x + gate * m if has_gate else x
    h = _rms(xn, lnw) * (1.0 + scale) + shift
    return xn, h


def _seg_spec(width, ncb):
    return pl.BlockSpec((None, RT, width), lambda i: (jnp.minimum(i // ncb, 1), 0, 0))


def gate_norm(x, m, gate, lnw, shift, scale, nc, R, name, comm=None):
    T, D = x.shape
    has_gate = m is not None
    ncb = nc // R

    def body(*refs):
        if has_gate:
            x_ref, m_ref, g_ref, w_ref, sh_ref, sc_ref, xn_ref, h_ref = refs
        else:
            x_ref, w_ref, sh_ref, sc_ref, h_ref = refs

        def step(t, carry):
            rows = pl.ds(pl.multiple_of(t * RT, RT), RT)
            xn, h = _gn_math(has_gate, x_ref[rows, :], m_ref[rows, :] if has_gate else None,
                             g_ref[...] if has_gate else None, w_ref[...], sh_ref[...], sc_ref[...])
            if has_gate:
                xn_ref[rows, :] = xn
            h_ref[rows, :] = h.astype(BF16)
            return carry

        lax.fori_loop(0, R // RT, step, 0)

    row = pl.BlockSpec((R, D), lambda i: (i, 0))
    seg = _seg_spec(D, ncb)
    shared = pl.BlockSpec((None, RT, D), lambda i: (0, 0, 0))
    if has_gate:
        ins, in_specs = (x, m, gate, lnw, shift, scale), [row, row, seg, shared, seg, seg]
        out_shape = (jax.ShapeDtypeStruct((T, D), F32), jax.ShapeDtypeStruct((T, D), BF16))
        out_specs = (row, row)
    else:
        ins, in_specs = (x, lnw, shift, scale), [row, shared, seg, seg]
        out_shape, out_specs = (jax.ShapeDtypeStruct((T, D), BF16),), (row,)
    out = _pcall(body, name=name, grid=(T // R,), out_shape=out_shape, in_specs=in_specs, out_specs=out_specs,
                 args=ins, sem=("parallel",), comm=comm)
    own = tuple(out[:2]) if has_gate else (None, out[0])
    return own if comm is None else own + tuple(out[2 if has_gate else 1:])


def gate_norm_bwd(x, m, gate, lnw, shift, scale, dxn, dh, nc, R, name):
    T, D = x.shape
    has_gate = m is not None
    ncb = nc // R

    def body(*refs):
        if has_gate:
            (x_ref, m_ref, g_ref, w_ref, sh_ref, sc_ref, dxn_ref, dh_ref,
             dx_ref, dm_ref, dg_ref, dw_ref, dsh_ref, dsc_ref) = refs
        else:
            x_ref, w_ref, sh_ref, sc_ref, dxn_ref, dh_ref, dx_ref, dw_ref, dsh_ref, dsc_ref = refs
        i = pl.program_id(0)

        @pl.when(i == 0)
        def _():
            dw_ref[...] = jnp.zeros_like(dw_ref)

        @pl.when((i == 0) | (i == ncb))
        def _():
            dsh_ref[...] = jnp.zeros_like(dsh_ref)
            dsc_ref[...] = jnp.zeros_like(dsc_ref)
            if has_gate:
                dg_ref[...] = jnp.zeros_like(dg_ref)

        def step(t, carry):
            rows = pl.ds(pl.multiple_of(t * RT, RT), RT)
            ct = (dxn_ref[rows, :], dh_ref[rows, :])
            if has_gate:
                _, vjp = jax.vjp(functools.partial(_gn_math, True), x_ref[rows, :], m_ref[rows, :], g_ref[...],
                                 w_ref[...], sh_ref[...], sc_ref[...])
                dx, dm, dg, dw, dsh, dsc = vjp(ct)
                dm_ref[rows, :] = dm.astype(BF16)
                dg_ref[...] += dg
            else:
                f = lambda x_, w_, sh_, sc_: _gn_math(False, x_, None, None, w_, sh_, sc_)[1]
                _, vjp = jax.vjp(f, x_ref[rows, :], w_ref[...], sh_ref[...], sc_ref[...])
                dx, dw, dsh, dsc = vjp(ct[1])
                dx = dx + ct[0]
            dx_ref[rows, :] = dx
            dw_ref[...] += dw
            dsh_ref[...] += dsh
            dsc_ref[...] += dsc
            return carry

        lax.fori_loop(0, R // RT, step, 0)

    row = pl.BlockSpec((R, D), lambda i: (i, 0))
    seg = _seg_spec(D, ncb)
    shared = pl.BlockSpec((None, RT, D), lambda i: (0, 0, 0))
    full, segs, one = jax.ShapeDtypeStruct((T, D), F32), jax.ShapeDtypeStruct((2, RT, D), F32), \
        jax.ShapeDtypeStruct((1, RT, D), F32)
    if has_gate:
        ins = (x, m, gate, lnw, shift, scale, dxn, dh)
        in_specs = [row, row, seg, shared, seg, seg, row, row]
        out_shape = (full, jax.ShapeDtypeStruct((T, D), BF16), segs, one, segs, segs)
        out_specs = (row, row, seg, shared, seg, seg)
    else:
        ins = (x, lnw, shift, scale, dxn, dh)
        in_specs = [row, shared, seg, seg, row, row]
        out_shape = (full, one, segs, segs)
        out_specs = (row, shared, seg, seg)
    out = pl.pallas_call(body, name=name, grid=(T // R,), out_shape=out_shape, in_specs=in_specs,
                         out_specs=out_specs, compiler_params=_cp("arbitrary"))(*ins)
    if has_gate:
        return out
    dx, dw, dsh, dsc = out
    return dx, None, None, dw, dsh, dsc


def gate_loss(x, m, gate, target, nc, R, name):
    T, D = x.shape
    ncb = nc // R

    def body(x_ref, m_ref, g_ref, t_ref, loss_ref, dx_ref, dm_ref, dg_ref):
        i = pl.program_id(0)

        @pl.when(i == 0)
        def _():
            loss_ref[...] = jnp.zeros_like(loss_ref)

        @pl.when((i == 0) | (i == ncb))
        def _():
            dg_ref[...] = jnp.zeros_like(dg_ref)

        live = jnp.where(i >= ncb, 1.0, 0.0).astype(F32)

        def step(t, carry):
            rows = pl.ds(pl.multiple_of(t * RT, RT), RT)
            mm_ = m_ref[rows, :]
            g = g_ref[...]
            e = (x_ref[rows, :] + g * mm_ - t_ref[rows, :]) * live
            dy = e * (1.0 / D)
            loss_ref[...] += 0.5 * e * dy
            dx_ref[rows, :] = dy
            dm_ref[rows, :] = (dy * g).astype(BF16)
            dg_ref[...] += dy * mm_
            return carry

        lax.fori_loop(0, R // RT, step, 0)

    row = pl.BlockSpec((R, D), lambda i: (i, 0))
    seg = _seg_spec(D, ncb)
    return pl.pallas_call(
        body, name=name, grid=(T // R,),
        out_shape=(jax.ShapeDtypeStruct((RT, D), F32), jax.ShapeDtypeStruct((T, D), F32),
                   jax.ShapeDtypeStruct((T, D), BF16), jax.ShapeDtypeStruct((2, RT, D), F32)),
        in_specs=[row, row, seg, pl.BlockSpec((R, D), lambda i: (jnp.maximum(i - ncb, 0), 0))],
        out_specs=(pl.BlockSpec((RT, D), lambda i: (0, 0)), row, row, seg),
        compiler_params=_cp("arbitrary"),
    )(x, m, gate, target)


def _hg_chunk(rev, lb, z, iv, hq, st):
    f = lb + (1.0 - lb) * jax.nn.sigmoid(z)
    g = jnp.log(jnp.maximum(f, F_FLOOR))
    k = (1.0 - lb) * jax.nn.sigmoid(-z)
    q = _silu(hq)
    ri = lax.broadcasted_iota(jnp.int32, (CHUNK, CHUNK), 0)
    ci = lax.broadcasted_iota(jnp.int32, (CHUNK, CHUNK), 1)
    r1 = lax.broadcasted_iota(jnp.int32, (CHUNK, 1), 0)
    seen = (ci >= ri) if rev else (ci <= ri)
    seen_t = (ci <= ri) if rev else (ci >= ri)
    cum = cumdot(seen.astype(F32), seen_t.astype(F32), g)
    tot = jnp.sum(g, axis=0, keepdims=True)
    att = jnp.zeros((CHUNK, CHUNK), F32)
    ref_rows = jnp.zeros_like(g)
    refs = []
    for b in range(CHUNK // SUB):
        before = (r1 >= SUB * (b + 1)) if rev else (r1 < SUB * b)
        r_b = jnp.sum(jnp.where(before, g, 0.0), axis=0, keepdims=True)
        in_b = (r1 >= SUB * b) & (r1 < SUB * (b + 1))
        ref_rows = ref_rows + jnp.where(in_b, r_b, 0.0)
        refs.append(r_b)
    qd = q * jnp.exp(cum - ref_rows)
    for b in range(CHUNK // SUB):
        kd = k * jnp.exp(jnp.minimum(refs[b] - cum, EXP_CLAMP))
        in_b = (ri >= SUB * b) & (ri < SUB * (b + 1))
        att = att + jnp.where(in_b, mmf(qd, kd, "nt"), 0.0)
    att = jnp.where(seen, att, 0.0)
    o = mmf(att, iv, "nn") + mmf(q * jnp.exp(cum), st, "nt")
    st_new = st * jnp.exp(tot) + mmf(iv, k * jnp.exp(tot - cum), "tn")
    return st_new, o


def _hg_cid(rev, i, ncs, n):
    if not rev:
        return i
    return jnp.where(i < ncs, ncs - 1 - i, ncs + n - 1 - i)


def hgrn_fwd(u, lb, rev, zcol, nc, hgw, name):
    T = u.shape[0]
    n, ncs, nh = T // CHUNK, nc // CHUNK, hgw // LANE

    def body(z_ref, v_ref, q_ref, lb_ref, o_ref, s_ref, st):
        i = pl.program_id(0)

        @pl.when(i == 0)
        def _():
            st[...] = jnp.zeros_like(st)

        for h in range(nh):
            cols = slice(h * LANE, (h + 1) * LANE)
            s_ref[h] = st[h]
            s_new, o = _hg_chunk(rev, lb_ref[:, cols], z_ref[:, cols], v_ref[:, cols], q_ref[:, cols], st[h])
            st[h] = s_new
            o_ref[:, cols] = o

    def col(cb):
        return pl.BlockSpec((CHUNK, hgw), lambda i: (_hg_cid(rev, i, ncs, n), cb))

    return pl.pallas_call(
        body, name=name, grid=(n,),
        out_shape=(jax.ShapeDtypeStruct((T, hgw), F32), jax.ShapeDtypeStruct((n, nh, LANE, LANE), F32)),
        in_specs=[col(zcol), col(2), col(7), pl.BlockSpec((1, hgw), lambda i: (0, 0))],
        out_specs=(pl.BlockSpec((CHUNK, hgw), lambda i: (_hg_cid(rev, i, ncs, n), 0)),
                   pl.BlockSpec((None, nh, LANE, LANE), lambda i: (i, 0, 0, 0))),
        scratch_shapes=[pltpu.VMEM((nh, LANE, LANE), F32)], compiler_params=_cp("arbitrary"),
    )(u, u, u, lb)


def hgrn_bwd(u, lb, states, do, rev, zcol, nc, hgw, name):
    T = u.shape[0]
    n, ncs, nh = T // CHUNK, nc // CHUNK, hgw // LANE

    def body(z_ref, v_ref, q_ref, lb_ref, s_ref, do_ref, dz_ref, dv_ref, dq_ref, dlb_ref, dst):
        j = pl.program_id(0)

        @pl.when(j == 0)
        def _():
            dst[...] = jnp.zeros_like(dst)
            dlb_ref[...] = jnp.zeros_like(dlb_ref)

        for h in range(nh):
            cols = slice(h * LANE, (h + 1) * LANE)
            _, vjp = jax.vjp(functools.partial(_hg_chunk, rev), lb_ref[:, cols], z_ref[:, cols], v_ref[:, cols],
                             q_ref[:, cols], s_ref[h])
            dlb, dz, dv, dq, ds = vjp((dst[h], do_ref[:, cols]))
            dst[h] = ds
            dz_ref[:, cols] = dz
            dv_ref[:, cols] = dv
            dq_ref[:, cols] = dq
            dlb_ref[:, cols] += dlb

    def cid(j):
        return _hg_cid(rev, n - 1 - j, ncs, n)

    def col(cb):
        return pl.BlockSpec((CHUNK, hgw), lambda j: (cid(j), cb))

    out = pl.BlockSpec((CHUNK, hgw), lambda j: (cid(j), 0))
    full = jax.ShapeDtypeStruct((T, hgw), F32)
    return pl.pallas_call(
        body, name=name, grid=(n,),
        out_shape=(full, full, full, jax.ShapeDtypeStruct((1, hgw), F32)),
        in_specs=[col(zcol), col(2), col(7), pl.BlockSpec((1, hgw), lambda j: (0, 0)),
                  pl.BlockSpec((None, nh, LANE, LANE), lambda j: (n - 1 - j, 0, 0, 0)), out],
        out_specs=(out, out, out, pl.BlockSpec((1, hgw), lambda j: (0, 0))),
        scratch_shapes=[pltpu.VMEM((nh, LANE, LANE), F32)], compiler_params=_cp("arbitrary"),
    )(u, u, u, lb, states, do)


HT = 128


def _head_group(nh, *col_offsets):
    for g in (4, 2):
        if nh % g == 0 and all(c % g == 0 for c in col_offsets):
            return g
    return 1


def _read_math(ofw, obw, g, w):
    return _rms(ofw + obw, w) * _silu(g)


def hg_read(ofw, obw, u, w, gcol, R, name):
    T, hgw = ofw.shape
    nh = hgw // LANE
    g = _head_group(nh, gcol)

    def body(a_ref, b_ref, g_ref, w_ref, o_ref):
        for j in range(g):
            cols = slice(j * LANE, (j + 1) * LANE)
            for t in range(R // HT):
                rows = slice(t * HT, (t + 1) * HT)
                o_ref[rows, cols] = _read_math(a_ref[rows, cols], b_ref[rows, cols], g_ref[rows, cols],
                                               w_ref[...]).astype(BF16)

    blk = pl.BlockSpec((R, g * LANE), lambda i, h: (i, h))
    return pl.pallas_call(
        body, name=name, grid=(T // R, nh // g), out_shape=jax.ShapeDtypeStruct((T, hgw), BF16),
        in_specs=[blk, blk, pl.BlockSpec((R, g * LANE), lambda i, h: (i, gcol // g + h)),
                  pl.BlockSpec((1, LANE), lambda i, h: (0, 0))],
        out_specs=blk, compiler_params=_cp("parallel", "parallel"),
    )(ofw, obw, u, w)


def hg_read_bwd(ofw, obw, u, w, dout, gcol, ocol, R, name):
    T, hgw = ofw.shape
    nh = hgw // LANE
    g = _head_group(nh, gcol, ocol)

    def body(a_ref, b_ref, g_ref, w_ref, d_ref, do_ref, dg_ref, dw_ref):
        @pl.when(pl.program_id(1) == 0)
        def _():
            dw_ref[...] = jnp.zeros_like(dw_ref)

        for j in range(g):
            cols = slice(j * LANE, (j + 1) * LANE)
            for t in range(R // HT):
                rows = slice(t * HT, (t + 1) * HT)
                _, vjp = jax.vjp(_read_math, a_ref[rows, cols], b_ref[rows, cols], g_ref[rows, cols], w_ref[...])
                da, _, dg, dw = vjp(d_ref[rows, cols])
                do_ref[rows, cols] = da
                dg_ref[rows, cols] = dg
                dw_ref[j] += dw

    blk = pl.BlockSpec((R, g * LANE), lambda h, i: (i, h))
    full = jax.ShapeDtypeStruct((T, hgw), F32)
    return pl.pallas_call(
        body, name=name, grid=(nh // g, T // R), out_shape=(full, full, jax.ShapeDtypeStruct((nh, 1, LANE), F32)),
        in_specs=[blk, blk, pl.BlockSpec((R, g * LANE), lambda h, i: (i, gcol // g + h)),
                  pl.BlockSpec((1, LANE), lambda h, i: (0, 0)),
                  pl.BlockSpec((R, g * LANE), lambda h, i: (i, ocol // g + h))],
        out_specs=(blk, blk, pl.BlockSpec((g, 1, LANE), lambda h, i: (h, 0, 0))),
        compiler_params=_cp("parallel", "arbitrary"),
    )(ofw, obw, u, w, dout)


def _na_step(qw, ow, bias, qraw, kl, vl, kc, vc):
    q = _rms(qraw, qw)
    s_loc = mmf(q, kl, "nt") * ATTN_SCALE + bias
    s_ctx = mmf(q, kc, "nt") * ATTN_SCALE
    m = lax.stop_gradient(jnp.maximum(jnp.max(s_loc, axis=-1, keepdims=True), jnp.max(s_ctx, axis=-1, keepdims=True)))
    p_loc = jnp.exp(s_loc - m)
    p_ctx = jnp.exp(s_ctx - m)
    inv = 1.0 / (jnp.sum(p_loc, axis=-1, keepdims=True) + jnp.sum(p_ctx, axis=-1, keepdims=True))
    return _rms(mmf(p_loc * inv, vl, "nn") + mmf(p_ctx * inv, vc, "nn"), ow)


def _na_geometry(nc, rows):
    ncs = nc // GRID_W
    win_r = min(WIN_R, rows)
    nloc = win_r * GRID_W

    def row_start(s):
        r = jnp.maximum(s - ncs, 0)
        return jnp.clip(r - win_r // 2, 0, rows - win_r)

    def bias_idx(s):
        r = s - ncs
        return jnp.where(s < ncs, win_r, r - jnp.clip(r - win_r // 2, 0, rows - win_r))

    return ncs, win_r, nloc, row_start, bias_idx


def na_bias_tables(rpb, rows):
    win_r = min(WIN_R, rows)
    nh = rpb.shape[0]
    sel_r = np.zeros((win_r, win_r, 2 * WIN_R - 1), np.float32)
    for off in range(win_r):
        for jr in range(win_r):
            sel_r[off, jr, jr - off + WIN_R - 1] = 1.0
    qc = np.arange(GRID_W)[:, None]
    kc = np.arange(GRID_W)[None, :]
    wstart = np.clip(qc - WIN_C // 2, 0, GRID_W - WIN_C)
    ok = (kc >= wstart) & (kc < wstart + WIN_C)
    sel_c = np.zeros((GRID_W, GRID_W, 2 * WIN_C - 1), np.float32)
    sel_c[np.broadcast_to(qc, ok.shape)[ok], np.broadcast_to(kc, ok.shape)[ok], (kc - qc + WIN_C - 1)[ok]] = 1.0
    hi = lax.Precision.HIGHEST
    t = jnp.einsum("hab,oja->hojb", rpb, sel_r, precision=hi)
    t = jnp.einsum("hojb,qkb->hoqjk", t, sel_c, precision=hi)
    t = jnp.where(ok[None, None, :, None, :], t, NEG_INF)
    t = jnp.concatenate([t, jnp.full((nh, 1, GRID_W, win_r, GRID_W), NEG_INF, F32)], axis=1)
    return t.reshape(nh, win_r + 1, GRID_W, win_r * GRID_W)


def kv_prep(u, kw, kcol, vcol, naw, R, name):
    T = u.shape[0]
    g = _head_group(naw // LANE, kcol, vcol)

    def body(k_ref, v_ref, w_ref, kn_ref, vb_ref):
        for j in range(g):
            cols = slice(j * LANE, (j + 1) * LANE)
            kn_ref[:, cols] = _rms(k_ref[:, cols], w_ref[...]).astype(BF16)
        vb_ref[...] = v_ref[...].astype(BF16)

    blk = pl.BlockSpec((R, g * LANE), lambda i, h: (i, h))
    sds = jax.ShapeDtypeStruct((T, naw), BF16)
    return pl.pallas_call(
        body, name=name, grid=(T // R, naw // LANE // g), out_shape=(sds, sds),
        in_specs=[pl.BlockSpec((R, g * LANE), lambda i, h: (i, kcol // g + h)),
                  pl.BlockSpec((R, g * LANE), lambda i, h: (i, vcol // g + h)),
                  pl.BlockSpec((1, LANE), lambda i, h: (0, 0))],
        out_specs=(blk, blk), compiler_params=_cp("parallel", "parallel"),
    )(u, u, kw)


def kv_prep_bwd(u, kw, dkn, kcol, naw, R, name):
    T = u.shape[0]
    nh = naw // LANE
    g = _head_group(nh, kcol)

    def body(k_ref, w_ref, d_ref, dk_ref, dw_ref):
        @pl.when(pl.program_id(1) == 0)
        def _():
            dw_ref[...] = jnp.zeros_like(dw_ref)

        for j in range(g):
            cols = slice(j * LANE, (j + 1) * LANE)
            for t in range(R // HT):
                rows = slice(t * HT, (t + 1) * HT)
                _, vjp = jax.vjp(_rms, k_ref[rows, cols], w_ref[...])
                dk, dw = vjp(d_ref[rows, cols])
                dk_ref[rows, cols] = dk
                dw_ref[j] += dw

    blk = pl.BlockSpec((R, g * LANE), lambda h, i: (i, h))
    return pl.pallas_call(
        body, name=name, grid=(nh // g, T // R),
        out_shape=(jax.ShapeDtypeStruct((T, naw), F32), jax.ShapeDtypeStruct((nh, 1, LANE), F32)),
        in_specs=[pl.BlockSpec((R, g * LANE), lambda h, i: (i, kcol // g + h)),
                  pl.BlockSpec((1, LANE), lambda h, i: (0, 0)), blk],
        out_specs=(blk, pl.BlockSpec((g, 1, LANE), lambda h, i: (h, 0, 0))),
        compiler_params=_cp("parallel", "arbitrary"),
    )(u, kw, dkn)


NA_HB = 4


def _na_operands(j, s, nc, nloc, row_start, q_refs, k_ref, v_ref, qw_ref, ow_ref, b_ref):
    cols = slice(j * LANE, (j + 1) * LANE)
    loc = pl.ds(pl.multiple_of(nc + row_start(s) * GRID_W, GRID_W), nloc)
    ops = (qw_ref[...], ow_ref[:, cols], b_ref[j], q_refs[j][...], k_ref[loc, cols].astype(F32),
           v_ref[loc, cols].astype(F32), k_ref[0:nc, cols].astype(F32), v_ref[0:nc, cols].astype(F32))
    return cols, loc, ops


def _grid_ends(grid):
    ids = lambda: [pl.program_id(d) for d in range(len(grid))]
    first = lambda: functools.reduce(jnp.logical_and, [p == 0 for p in ids()])
    last = lambda: functools.reduce(jnp.logical_and, [p == g - 1 for p, g in zip(ids(), grid)])
    return first, last


def na_fwd(u, kn, vb, qw, ow, bias, qcol, nc, name, comm=None):
    T, naw = kn.shape
    nh, rows = naw // LANE, (T - nc) // GRID_W
    hb = NA_HB if nh % NA_HB == 0 else 1
    ncs, win_r, nloc, row_start, bias_idx = _na_geometry(nc, rows)

    def body(*refs):
        q_refs, (k_ref, v_ref, qw_ref, ow_ref, b_ref, o_ref) = refs[:hb], refs[hb:]
        s = pl.program_id(1)
        for j in range(hb):
            cols, _, ops = _na_operands(j, s, nc, nloc, row_start, q_refs, k_ref, v_ref, qw_ref, ow_ref, b_ref)
            o_ref[:, cols] = _na_step(*ops).astype(BF16)

    wide = pl.BlockSpec((T, hb * LANE), lambda g, s: (0, g), pipeline_mode=pl.Buffered(1))
    grid = (nh // hb, T // GRID_W)
    cin, cspec, cout, csem = _comm_extras(comm)
    out = pl.pallas_call(
        _hosted(body, hb + 5, 1, comm, *_grid_ends(grid)), name=name, grid=grid,
        out_shape=[jax.ShapeDtypeStruct((T, naw), BF16)] + cout,
        in_specs=[pl.BlockSpec((GRID_W, LANE), functools.partial(lambda j, g, s: (s, qcol + g * hb + j), j))
                  for j in range(hb)]
        + [wide, wide, pl.BlockSpec((1, LANE), lambda g, s: (0, 0)), pl.BlockSpec((1, hb * LANE), lambda g, s: (0, g)),
           pl.BlockSpec((hb, None, GRID_W, nloc), lambda g, s: (g, bias_idx(s), 0, 0))] + cspec,
        out_specs=[pl.BlockSpec((GRID_W, hb * LANE), lambda g, s: (s, g))] + cspec, scratch_shapes=csem,
        compiler_params=_cp("arbitrary", "arbitrary"),
    )(*([u] * hb), kn, vb, qw, ow, bias, *cin)
    return out[0] if comm is None else out


def na_bwd(u, kn, vb, qw, ow, bias, dout, qcol, ocol, nc, name, comm=None):
    T, naw = kn.shape
    nh, rows = naw // LANE, (T - nc) // GRID_W
    hb = NA_HB if nh % NA_HB == 0 else 1
    ncs, win_r, nloc, row_start, bias_idx = _na_geometry(nc, rows)
    fresh = [0] + [ncs + r for r in range(rows) if r == 0 or r - np.clip(r - win_r // 2, 0, rows - win_r)
                   != (r - 1) - np.clip(r - 1 - win_r // 2, 0, rows - win_r)]

    def body(*refs):
        q_refs, d_refs = refs[:hb], refs[hb:2 * hb]
        k_ref, v_ref, qw_ref, ow_ref, b_ref, dq_ref, dk_ref, dv_ref, db_ref, dqw_ref, dow_ref = refs[2 * hb:]
        s = pl.program_id(1)

        @pl.when(s == 0)
        def _():
            dk_ref[...] = jnp.zeros_like(dk_ref)
            dv_ref[...] = jnp.zeros_like(dv_ref)
            dqw_ref[...] = jnp.zeros_like(dqw_ref)
            dow_ref[...] = jnp.zeros_like(dow_ref)

        first = functools.reduce(lambda a, b: a | b, [s == f for f in fresh])

        @pl.when(first)
        def _():
            db_ref[...] = jnp.zeros_like(db_ref)

        for j in range(hb):
            cols, loc, ops = _na_operands(j, s, nc, nloc, row_start, q_refs, k_ref, v_ref, qw_ref, ow_ref, b_ref)
            _, vjp = jax.vjp(_na_step, *ops)
            dqw, dow, db, dq, dkl, dvl, dkc, dvc = vjp(d_refs[j][...])
            dq_ref[:, cols] = dq
            dk_ref[loc, cols] += dkl
            dv_ref[loc, cols] += dvl
            dk_ref[0:nc, cols] += dkc
            dv_ref[0:nc, cols] += dvc
            db_ref[j] += db
            dqw_ref[j] += dqw
            dow_ref[j] += dow

    wide = pl.BlockSpec((T, hb * LANE), lambda g, s: (0, g), pipeline_mode=pl.Buffered(1))
    hvec = pl.BlockSpec((hb, 1, LANE), lambda g, s: (g, 0, 0))
    full = jax.ShapeDtypeStruct((T, naw), F32)
    hv = jax.ShapeDtypeStruct((nh, 1, LANE), F32)
    bspec = pl.BlockSpec((hb, None, GRID_W, nloc), lambda g, s: (g, bias_idx(s), 0, 0))
    grid = (nh // hb, T // GRID_W)
    cin, cspec, cout, csem = _comm_extras(comm)
    return pl.pallas_call(
        _hosted(body, 2 * hb + 5, 6, comm, *_grid_ends(grid)), name=name, grid=grid,
        out_shape=[full, full, full, jax.ShapeDtypeStruct(bias.shape, F32), hv, hv] + cout,
        in_specs=[pl.BlockSpec((GRID_W, LANE), functools.partial(lambda j, g, s: (s, qcol + g * hb + j), j))
                  for j in range(hb)]
        + [pl.BlockSpec((GRID_W, LANE), functools.partial(lambda j, g, s: (s, ocol + g * hb + j), j))
           for j in range(hb)]
        + [wide, wide, pl.BlockSpec((1, LANE), lambda g, s: (0, 0)), pl.BlockSpec((1, hb * LANE), lambda g, s: (0, g)),
           bspec] + cspec,
        out_specs=[pl.BlockSpec((GRID_W, hb * LANE), lambda g, s: (s, g)), wide, wide, bspec, hvec, hvec] + cspec,
        scratch_shapes=csem, compiler_params=_cp("arbitrary", "arbitrary"),
    )(*([u] * hb), *([dout] * hb), kn, vb, qw, ow, bias, *cin)


def _halo_specs(R, width, T, col):
    hb = R // 8
    prev = pl.BlockSpec((8, width), lambda j, i: (jnp.maximum(i * hb - 1, 0), col(j, i)))
    nxt = pl.BlockSpec((8, width), lambda j, i: (jnp.minimum((i + 1) * hb, T // 8 - 1), col(j, i)))
    return prev, nxt


def _edge_flags(i, ncb, nblk):
    has_prev = jnp.where((i == 0) | (i == ncb), 0.0, 1.0).astype(F32)
    has_next = jnp.where((i == ncb - 1) | (i == nblk - 1), 0.0, 1.0).astype(F32)
    return has_prev, has_next


def _shift_up(a, prev_row):
    r0 = lax.broadcasted_iota(jnp.int32, a.shape, 0) == 0
    return jnp.where(r0, prev_row, pltpu.roll(a, 1, 0))


def _shift_dn(a, next_row):
    n = a.shape[0]
    rl = lax.broadcasted_iota(jnp.int32, a.shape, 0) == n - 1
    return jnp.where(rl, next_row, pltpu.roll(a, n - 1, 0))


def _conv3(a, prev_row, next_row, w_ref):
    return w_ref[0:1, :] * _shift_up(a, prev_row) + w_ref[1:2, :] * a + w_ref[2:3, :] * _shift_dn(a, next_row)


def _cv_post(b, y, w):
    return _rms(b * y, w)


def short_conv(u, cw, ow, bcol, nc, cvw, R, bwd_dout=None, ocol=0, name=""):
    T = u.shape[0]
    nh, nblk, ncb = cvw // LANE, T // R, nc // R
    bwd = bwd_dout is not None
    g = _head_group(nh, bcol, bcol + nh, bcol + 2 * nh, ocol)
    gw = g * LANE

    def body(b_ref, c_ref, v_ref, cp_ref, vp_ref, cn_ref, vn_ref, cw_ref, ow_ref, *rest):
        i = pl.program_id(1)
        has_prev, has_next = _edge_flags(i, ncb, nblk)
        p = c_ref[...] * v_ref[...]
        y = _conv3(p, cp_ref[7:8, :] * vp_ref[7:8, :] * has_prev, cn_ref[0:1, :] * vn_ref[0:1, :] * has_next, cw_ref)
        if bwd:
            d_ref, db_ref, dy_ref, dow_ref = rest

            @pl.when(i == 0)
            def _():
                dow_ref[...] = jnp.zeros_like(dow_ref)

        for j in range(g):
            cols = slice(j * LANE, (j + 1) * LANE)
            if not bwd:
                rest[0][:, cols] = _cv_post(b_ref[:, cols], y[:, cols], ow_ref[:, cols]).astype(BF16)
            else:
                _, vjp = jax.vjp(_cv_post, b_ref[:, cols], y[:, cols], ow_ref[:, cols])
                db, dy, dow = vjp(d_ref[:, cols])
                db_ref[:, cols] = db
                dy_ref[:, cols] = dy
                dow_ref[j] += dow

    def main(k):
        return pl.BlockSpec((R, gw), lambda h, i: (i, (bcol + k * nh) // g + h))

    cprev, cnext = _halo_specs(R, gw, T, lambda h, i: (bcol + nh) // g + h)
    vprev, vnext = _halo_specs(R, gw, T, lambda h, i: (bcol + 2 * nh) // g + h)
    in_specs = [main(0), main(1), main(2), cprev, vprev, cnext, vnext,
                pl.BlockSpec((3, gw), lambda h, i: (0, h)), pl.BlockSpec((1, gw), lambda h, i: (0, h))]
    ins = [u] * 7 + [cw, ow]
    blk = pl.BlockSpec((R, gw), lambda h, i: (i, h))
    if not bwd:
        out_shape, out_specs = jax.ShapeDtypeStruct((T, cvw), BF16), blk
    else:
        in_specs.append(pl.BlockSpec((R, gw), lambda h, i: (i, ocol // g + h)))
        ins.append(bwd_dout)
        full = jax.ShapeDtypeStruct((T, cvw), F32)
        out_shape = (full, full, jax.ShapeDtypeStruct((nh, 1, LANE), F32))
        out_specs = (blk, blk, pl.BlockSpec((g, 1, LANE), lambda h, i: (h, 0, 0)))
    return pl.pallas_call(body, name=name, grid=(nh // g, nblk), out_shape=out_shape, in_specs=in_specs,
                          out_specs=out_specs, compiler_params=_cp("parallel", "arbitrary"))(*ins)


def conv3_bwd(dy, src, cw, nc, R, W, prod_cols=None, col0=0, out_dtype=F32, name=""):
    T, C = dy.shape
    nblk, ncb = T // R, nc // R
    prod = prod_cols is not None

    def body(*refs):
        if prod:
            (d_ref, dp_ref, dn_ref, c_ref, v_ref, cp_ref, vp_ref, cn_ref, vn_ref, w_ref,
             dc_ref, dv_ref, dw_ref) = refs
        else:
            d_ref, dp_ref, dn_ref, p_ref, pp_ref, pn_ref, w_ref, o_ref, dw_ref = refs
        i = pl.program_id(1)
        has_prev, has_next = _edge_flags(i, ncb, nblk)

        @pl.when(i == 0)
        def _():
            dw_ref[...] = jnp.zeros_like(dw_ref)

        d = d_ref[...]
        d_up = _shift_up(d, dp_ref[7:8, :] * has_prev)
        d_dn = _shift_dn(d, dn_ref[0:1, :] * has_next)
        dp = w_ref[0:1, :] * d_dn + w_ref[1:2, :] * d + w_ref[2:3, :] * d_up
        if prod:
            c, v = c_ref[...], v_ref[...]
            p = c * v
            p_prev, p_next = cp_ref[7:8, :] * vp_ref[7:8, :] * has_prev, cn_ref[0:1, :] * vn_ref[0:1, :] * has_next
            dc_ref[...] = dp * v
            dv_ref[...] = dp * c
        else:
            p = p_ref[...]
            p_prev, p_next = pp_ref[7:8, :] * has_prev, pn_ref[0:1, :] * has_next
            o_ref[...] = dp.astype(out_dtype)
        dw_ref[0:1, :] += jnp.sum(_shift_up(p, p_prev) * d, axis=0, keepdims=True)
        dw_ref[1:2, :] += jnp.sum(p * d, axis=0, keepdims=True)
        dw_ref[2:3, :] += jnp.sum(_shift_dn(p, p_next) * d, axis=0, keepdims=True)

    blk = pl.BlockSpec((R, W), lambda j, i: (i, j))
    dprev, dnext = _halo_specs(R, W, T, lambda j, i: j)
    wspec = pl.BlockSpec((3, W), lambda j, i: (0, j))
    dwspec = pl.BlockSpec((8, W), lambda j, i: (0, j))
    dwshape = jax.ShapeDtypeStruct((8, C), F32)
    if prod:
        ccol, vcol = prod_cols
        cprev, cnext = _halo_specs(R, W, T, lambda j, i: ccol + j)
        vprev, vnext = _halo_specs(R, W, T, lambda j, i: vcol + j)
        in_specs = [blk, dprev, dnext, pl.BlockSpec((R, W), lambda j, i: (i, ccol + j)),
                    pl.BlockSpec((R, W), lambda j, i: (i, vcol + j)), cprev, vprev, cnext, vnext, wspec]
        ins = [dy, dy, dy] + [src] * 6 + [cw]
        full = jax.ShapeDtypeStruct((T, C), F32)
        out_shape, out_specs = (full, full, dwshape), (blk, blk, dwspec)
    else:
        sprev, snext = _halo_specs(R, W, T, lambda j, i: col0 + j)
        in_specs = [blk, dprev, dnext, pl.BlockSpec((R, W), lambda j, i: (i, col0 + j)), sprev, snext,
                    pl.BlockSpec((3, W), lambda j, i: (0, col0 + j))]
        ins = [dy, dy, dy, src, src, src, cw]
        out_shape, out_specs = (jax.ShapeDtypeStruct((T, C), out_dtype), dwshape), (blk, dwspec)
    return pl.pallas_call(body, name=name, grid=(C // W, nblk), out_shape=out_shape, in_specs=in_specs,
                          out_specs=out_specs, compiler_params=_cp("parallel", "arbitrary"))(*ins)


def _pcall(body, *, name, grid, in_specs, out_specs, out_shape, args, scratch=(), sem=None, comm=None):
    cin, cspec, cout, csem = _comm_extras(comm)
    if comm is not None:
        sem = ("arbitrary",) * len(grid)
    return pl.pallas_call(
        _hosted(body, len(in_specs), len(out_specs), comm, *_grid_ends(grid)), name=name, grid=grid,
        out_shape=list(out_shape) + cout, in_specs=list(in_specs) + cspec, out_specs=list(out_specs) + cspec,
        scratch_shapes=list(scratch) + csem, compiler_params=_cp(*sem))(*args, *cin)


def ffn_mid(uf, cw, cb, nc, R, W, name, comm=None):
    T, C = uf.shape
    F = C // 2
    nblk, ncb, nj = T // R, nc // R, F // W

    def body(g_ref, v_ref, gp_ref, vp_ref, gn_ref, vn_ref, wg_ref, wv_ref, bg_ref, bv_ref, a_ref):
        i = pl.program_id(1)
        has_prev, has_next = _edge_flags(i, ncb, nblk)
        yg = _conv3(g_ref[...], gp_ref[7:8, :] * has_prev, gn_ref[0:1, :] * has_next, wg_ref) + bg_ref[...]
        yv = _conv3(v_ref[...], vp_ref[7:8, :] * has_prev, vn_ref[0:1, :] * has_next, wv_ref) + bv_ref[...]
        a_ref[...] = (yg * jax.nn.sigmoid(yg) * yv).astype(BF16)

    gblk = pl.BlockSpec((R, W), lambda j, i: (i, j))
    vblk = pl.BlockSpec((R, W), lambda j, i: (i, nj + j))
    gprev, gnext = _halo_specs(R, W, T, lambda j, i: j)
    vprev, vnext = _halo_specs(R, W, T, lambda j, i: nj + j)
    in_specs = [gblk, vblk, gprev, vprev, gnext, vnext,
                pl.BlockSpec((3, W), lambda j, i: (0, j)), pl.BlockSpec((3, W), lambda j, i: (0, nj + j)),
                pl.BlockSpec((1, W), lambda j, i: (0, j)), pl.BlockSpec((1, W), lambda j, i: (0, nj + j))]
    return _pcall(body, name=name, grid=(nj, nblk), in_specs=in_specs, out_specs=[gblk],
                  out_shape=[jax.ShapeDtypeStruct((T, F), BF16)], args=[uf] * 6 + [cw, cw, cb, cb],
                  sem=("parallel", "arbitrary"), comm=comm)


def ffn_mid_bwd(uf, cw, cb, da, nc, R, W, name, comm=None):
    T, C = uf.shape
    F = C // 2
    nblk, ncb, nj = T // R, nc // R, F // W

    def body(g_ref, v_ref, gp_ref, vp_ref, gn_ref, vn_ref, d_ref, dp_ref, dn_ref, wg_ref, wv_ref, bg_ref, bv_ref,
             dug_ref, duv_ref, dwg_ref, dwv_ref, dbg_ref, dbv_ref):
        i = pl.program_id(1)
        has_prev, has_next = _edge_flags(i, ncb, nblk)

        @pl.when(i == 0)
        def _():
            for r in (dwg_ref, dwv_ref, dbg_ref, dbv_ref):
                r[...] = jnp.zeros_like(r)

        def taps(w_ref):
            return w_ref[0:1, :], w_ref[1:2, :], w_ref[2:3, :]

        def dy_of(yg, yv, d):
            sg = jax.nn.sigmoid(yg)
            return d * yv * (sg * (1.0 + yg * (1.0 - sg))), d * (yg * sg)

        g, v, d = g_ref[...], v_ref[...], d_ref[...]
        (wg0, wg1, wg2), (wv0, wv1, wv2) = taps(wg_ref), taps(wv_ref)
        bg, bv = bg_ref[...], bv_ref[...]
        g_up, g_dn = _shift_up(g, gp_ref[7:8, :] * has_prev), _shift_dn(g, gn_ref[0:1, :] * has_next)
        v_up, v_dn = _shift_up(v, vp_ref[7:8, :] * has_prev), _shift_dn(v, vn_ref[0:1, :] * has_next)
        dyg, dyv = dy_of(wg0 * g_up + wg1 * g + wg2 * g_dn + bg, wv0 * v_up + wv1 * v + wv2 * v_dn + bv, d)
        dyg_p, dyv_p = dy_of(wg0 * gp_ref[6:7, :] + wg1 * gp_ref[7:8, :] + wg2 * g_ref[0:1, :] + bg,
                             wv0 * vp_ref[6:7, :] + wv1 * vp_ref[7:8, :] + wv2 * v_ref[0:1, :] + bv, dp_ref[7:8, :])
        dyg_n, dyv_n = dy_of(wg0 * g_ref[R - 1:R, :] + wg1 * gn_ref[0:1, :] + wg2 * gn_ref[1:2, :] + bg,
                             wv0 * v_ref[R - 1:R, :] + wv1 * vn_ref[0:1, :] + wv2 * vn_ref[1:2, :] + bv, dn_ref[0:1, :])
        dug_ref[...] = (wg0 * _shift_dn(dyg, dyg_n * has_next) + wg1 * dyg
                        + wg2 * _shift_up(dyg, dyg_p * has_prev)).astype(BF16)
        duv_ref[...] = (wv0 * _shift_dn(dyv, dyv_n * has_next) + wv1 * dyv
                        + wv2 * _shift_up(dyv, dyv_p * has_prev)).astype(BF16)
        for ref, ups, mid, dns, dy in ((dwg_ref, g_up, g, g_dn, dyg), (dwv_ref, v_up, v, v_dn, dyv)):
            ref[0:1, :] += jnp.sum(ups * dy, axis=0, keepdims=True)
            ref[1:2, :] += jnp.sum(mid * dy, axis=0, keepdims=True)
            ref[2:3, :] += jnp.sum(dns * dy, axis=0, keepdims=True)
        dbg_ref[...] += jnp.sum(dyg, axis=0, keepdims=True)
        dbv_ref[...] += jnp.sum(dyv, axis=0, keepdims=True)

    gblk = pl.BlockSpec((R, W), lambda j, i: (i, j))
    vblk = pl.BlockSpec((R, W), lambda j, i: (i, nj + j))
    gprev, gnext = _halo_specs(R, W, T, lambda j, i: j)
    vprev, vnext = _halo_specs(R, W, T, lambda j, i: nj + j)
    half = jax.ShapeDtypeStruct((T, F), BF16)
    taps8, bias1 = jax.ShapeDtypeStruct((8, F), F32), jax.ShapeDtypeStruct((1, F), F32)
    return _pcall(
        body, name=name, grid=(nj, nblk), out_shape=(half, half, taps8, taps8, bias1, bias1),
        in_specs=[gblk, vblk, gprev, vprev, gnext, vnext, gblk, gprev, gnext,
                  pl.BlockSpec((3, W), lambda j, i: (0, j)), pl.BlockSpec((3, W), lambda j, i: (0, nj + j)),
                  pl.BlockSpec((1, W), lambda j, i: (0, j)), pl.BlockSpec((1, W), lambda j, i: (0, nj + j))],
        out_specs=(gblk, gblk, pl.BlockSpec((8, W), lambda j, i: (0, j)), pl.BlockSpec((8, W), lambda j, i: (0, j)),
                   pl.BlockSpec((1, W), lambda j, i: (0, j)), pl.BlockSpec((1, W), lambda j, i: (0, j))),
        args=(uf, uf, uf, uf, uf, uf, da, da, da, cw, cw, cb, cb), sem=("parallel", "arbitrary"), comm=comm)


WEIGHTS = ("c_ctx", "w_ada", "b_ada", "ln1_w", "ln2_w", "w_in", "hg_lb_logits", "hg_norm_w", "na_q_norm_w",
           "na_k_norm_w", "na_rpb", "na_out_norm_w", "cv_w", "cv_out_norm_w", "w_out", "w_up", "ffn_conv_w",
           "ffn_conv_b", "w_down")
BIG = ("w_ada", "w_in", "w_out", "w_up", "w_down")
SHARDED_SMALL = ("hg_lb_logits", "cv_w", "ffn_conv_w")


def _flat_rows(parts, dtype):
    flat, layout, off = [], [], 0
    for p in parts:
        layout.append((off, p.shape))
        flat.append(p.reshape(-1).astype(dtype))
        off += p.size
    pad = (-off) % (8 * LANE)
    if pad:
        flat.append(jnp.zeros((pad,), dtype))
    return jnp.concatenate(flat).reshape(-1, LANE), layout


def _unflat(buf, layout):
    v = buf.reshape(-1)
    return [v[off:off + int(np.prod(shape))].reshape(shape) for off, shape in layout]


def _lb_all(logits):
    sm = jax.nn.softmax(logits.astype(F32), axis=1)
    return jnp.cumsum(sm, axis=1) - sm[:, :1]


def _seg(ctx_vec, lat_vec):
    return jnp.broadcast_to(jnp.stack([ctx_vec, lat_vec])[:, None, :], (2, RT, ctx_vec.shape[0]))


def _shared(vec):
    return jnp.broadcast_to(vec[None, None, :], (1, RT, vec.shape[0]))


def kernel(x, c, ctx, c_ctx, w_ada, b_ada, ln1_w, ln2_w, w_in, hg_lb_logits, hg_norm_w, na_q_norm_w, na_k_norm_w, na_rpb, na_out_norm_w, cv_w, cv_out_norm_w, w_out, w_up, ffn_conv_w, ffn_conv_b, w_down, loss_target, m_c_ctx, m_w_ada, m_b_ada, m_ln1_w, m_ln2_w, m_w_in, m_hg_lb_logits, m_hg_norm_w, m_na_q_norm_w, m_na_k_norm_w, m_na_rpb, m_na_out_norm_w, m_cv_w, m_cv_out_norm_w, m_w_out, m_w_up, m_ffn_conv_w, m_ffn_conv_b, m_w_down, v_c_ctx, v_w_ada, v_b_ada, v_ln1_w, v_ln2_w, v_w_in, v_hg_lb_logits, v_hg_norm_w, v_na_q_norm_w, v_na_k_norm_w, v_na_rpb, v_na_out_norm_w, v_cv_w, v_cv_out_norm_w, v_w_out, v_w_up, v_ffn_conv_w, v_ffn_conv_b, v_w_down):
    W = dict(c_ctx=c_ctx, w_ada=w_ada, b_ada=b_ada, ln1_w=ln1_w, ln2_w=ln2_w, w_in=w_in, hg_lb_logits=hg_lb_logits,
             hg_norm_w=hg_norm_w, na_q_norm_w=na_q_norm_w, na_k_norm_w=na_k_norm_w, na_rpb=na_rpb,
             na_out_norm_w=na_out_norm_w, cv_w=cv_w, cv_out_norm_w=cv_out_norm_w, w_out=w_out, w_up=w_up,
             ffn_conv_w=ffn_conv_w, ffn_conv_b=ffn_conv_b, w_down=w_down)
    Mo = dict(c_ctx=m_c_ctx, w_ada=m_w_ada, b_ada=m_b_ada, ln1_w=m_ln1_w, ln2_w=m_ln2_w, w_in=m_w_in,
              hg_lb_logits=m_hg_lb_logits, hg_norm_w=m_hg_norm_w, na_q_norm_w=m_na_q_norm_w,
              na_k_norm_w=m_na_k_norm_w, na_rpb=m_na_rpb, na_out_norm_w=m_na_out_norm_w, cv_w=m_cv_w,
              cv_out_norm_w=m_cv_out_norm_w, w_out=m_w_out, w_up=m_w_up, ffn_conv_w=m_ffn_conv_w,
              ffn_conv_b=m_ffn_conv_b, w_down=m_w_down)
    Vo = dict(c_ctx=v_c_ctx, w_ada=v_w_ada, b_ada=v_b_ada, ln1_w=v_ln1_w, ln2_w=v_ln2_w, w_in=v_w_in,
              hg_lb_logits=v_hg_lb_logits, hg_norm_w=v_hg_norm_w, na_q_norm_w=v_na_q_norm_w,
              na_k_norm_w=v_na_k_norm_w, na_rpb=v_na_rpb, na_out_norm_w=v_na_out_norm_w, cv_w=v_cv_w,
              cv_out_norm_w=v_cv_out_norm_w, w_out=v_w_out, w_up=v_w_up, ffn_conv_w=v_ffn_conv_w,
              ffn_conv_b=v_ffn_conv_b, w_down=v_w_down)

    xi, yi, ci = _me()
    chip = 2 * xi + yi
    dev = 2 * chip + ci
    L, D = x.shape[1], x.shape[2]
    NC = ctx.shape[1]
    T = NC + L
    depth = w_in.shape[0]
    HGW, NAW, CVW = 4 * hg_lb_logits.shape[-1], na_out_norm_w.shape[-1], cv_out_norm_w.shape[-1]
    MIX = HGW + NAW + CVW
    INW, FF2 = 4 * w_in.shape[-1], 4 * w_up.shape[-1]
    F = FF2 // 2
    ADA = 4 * w_ada.shape[-1]
    assert NAW == 2 * HGW and INW == 5 * HGW + 3 * NAW + 3 * CVW and ADA == 6 * D and NC % 128 == 0
    assert L % GRID_W == 0 and T % CHUNK == 0 and depth == 2
    R = math.gcd(NC, 256)
    FW = 512 if F % 512 == 0 else LANE
    rows = L // GRID_W
    nh_hg, nh_na, nh_cv = HGW // LANE, NAW // LANE, CVW // LANE
    kcol = 3 * nh_hg
    vcol = kcol + nh_na
    gcol = vcol + nh_na + nh_hg
    qcol = gcol + nh_hg
    bcol = qcol + nh_na
    mix_na, mix_cv = nh_hg, nh_hg + nh_na

    small1, lay1 = _flat_rows([c[0], hg_lb_logits, cv_w, ffn_conv_w], F32)
    g1 = allgather8([small1], "gather_cond")[0]
    per_dev = [_unflat(g1[d], lay1) for d in range(8)]
    c_all = jnp.stack([p[0] for p in per_dev])
    lb_logits = jnp.concatenate([per_dev[2 * s][1] for s in range(4)], axis=-1)
    cvw_full = jnp.concatenate([per_dev[2 * s][2] for s in range(4)], axis=-1)
    fcw_full = jnp.concatenate([per_dev[2 * s][3] for s in range(4)], axis=-1)
    lb_all, lb_pull = jax.vjp(_lb_all, lb_logits)

    a16 = jnp.concatenate([c_all, c_ctx[None], jnp.zeros((7, D), F32)])
    s16 = _silu(a16)
    wcols = ADA // 4
    b_mine = lax.dynamic_slice_in_dim(b_ada, chip * wcols, wcols, axis=1)
    p_ada = mm_nn(s16, w_ada, F32, "ada_fwd").reshape(16, depth, wcols).transpose(1, 0, 2) + b_mine[:, None, :]
    g2 = allgather8([p_ada.reshape(depth * 16, wcols)], "gather_ada")[0].reshape(8, depth, 16, wcols)
    ada_rows = jnp.concatenate([g2[2 * s] for s in range(4)], axis=-1)
    ada = lax.dynamic_index_in_dim(ada_rows, dev, axis=1, keepdims=False)
    ada_c = ada_rows[:, 8]

    def half_rows(a):
        h = a.shape[0] // 2
        return lax.dynamic_slice_in_dim(a, ci * h, h, axis=0)

    proj = ("w_in", "w_out", "w_up", "w_down")
    wparts = [{n: half_rows(W[n][l]).astype(BF16) for n in proj} for l in range(depth)]

    def stacked(n, g):
        g = g.reshape(4, -1, g.shape[-1])
        return g.reshape(1, -1, g.shape[-1]) if n in ("w_out", "w_down") else g

    Wg = [{}, {}]
    fetch_plan = {"norm_in": [(0, "w_in")], "proj_in_0": [(0, "w_out"), (0, "w_down")], "na_fwd_0": [(0, "w_up")],
                  "ffn_up_0": [(1, "w_in"), (1, "w_out")], "ffn_mid_0": [(1, "w_up")], "ffn_down_0": [(1, "w_down")]}

    def fetching(name, call):
        items = fetch_plan.get(name, [])
        res = call(allgather8_comm([wparts[l][n] for l, n in items]) if items else None)
        res = list(res) if isinstance(res, (list, tuple)) else [res]
        for (l, n), g in zip(items, res[len(res) - len(items):]):
            Wg[l][n] = stacked(n, g)
        own = res[:len(res) - len(items)]
        return own[0] if len(own) == 1 else own

    xcat = jnp.concatenate([ctx[0], x[0]], axis=0)
    mods = []
    for l in range(depth):
        lat, con = jnp.split(ada[l], 6), jnp.split(ada_c[l], 6)
        mods.append(dict(sh1=_seg(con[0], lat[0]), sc1=_seg(con[1], lat[1]), g1=_seg(con[2], lat[2]),
                         sh2=_seg(con[3], lat[3]), sc2=_seg(con[4], lat[4]), g2=_seg(con[5], lat[5]),
                         ln1=_shared(ln1_w[l]), ln2=_shared(ln2_w[l])))
    bias_pull, saved = [], []
    x0 = xcat
    h = fetching("norm_in", lambda cm: gate_norm(x0, None, None, mods[0]["ln1"], mods[0]["sh1"], mods[0]["sc1"], NC, R,
                                                 "norm_in", comm=cm)[1:])
    for l in range(depth):
        md, wl = mods[l], Wg[l]
        u = fetching(f"proj_in_{l}", lambda cm: mm_nn(h, wl["w_in"], F32, f"proj_in_{l}", comm=cm))
        lbf, lbb = lb_all[0, l][None], lb_all[1, l][None]
        o_fw, st_fw = hgrn_fwd(u, lbf, False, 0, NC, HGW, f"hgrn_fw_{l}")
        o_bw, st_bw = hgrn_fwd(u, lbb, True, 1, NC, HGW, f"hgrn_bw_{l}")
        hgn = hg_norm_w[l][None]
        hg = hg_read(o_fw, o_bw, u, hgn, gcol, R, f"hg_read_{l}")
        bias, pull = jax.vjp(lambda r: na_bias_tables(r, rows), na_rpb[l])
        bias_pull.append(pull)
        qn, kn, on = na_q_norm_w[l][None], na_k_norm_w[l][None], na_out_norm_w[l][None]
        keys_n, vals_b = kv_prep(u, kn, kcol, vcol, NAW, R, f"kv_prep_{l}")
        na = fetching(f"na_fwd_{l}", lambda cm: na_fwd(u, keys_n, vals_b, qn, on, bias, qcol, NC, f"na_fwd_{l}", comm=cm))
        cvw_l, cvo = cvw_full[l], cv_out_norm_w[l][None]
        cv = short_conv(u, cvw_l, cvo, bcol, NC, CVW, R, name=f"short_conv_{l}")
        mix = jnp.concatenate([hg, na, cv], axis=1)
        m1 = mm_nn(mix, wl["w_out"], F32, f"proj_out_{l}")
        x1, h2 = gate_norm(x0, m1, md["g1"], md["ln2"], md["sh2"], md["sc2"], NC, R, f"gate_norm_mid_{l}")
        uf = fetching(f"ffn_up_{l}", lambda cm: mm_nn(h2, wl["w_up"], F32, f"ffn_up_{l}", comm=cm))
        fcw_l, fcb_l = fcw_full[l], ffn_conv_b[l][None]
        a = fetching(f"ffn_mid_{l}", lambda cm: ffn_mid(uf, fcw_l, fcb_l, NC, R, FW, f"ffn_mid_{l}", comm=cm))
        m2 = fetching(f"ffn_down_{l}", lambda cm: mm_nn(a, wl["w_down"], F32, f"ffn_down_{l}", comm=cm))
        saved.append(dict(x0=x0, h=h, u=u, o_fw=o_fw, o_bw=o_bw, st_fw=st_fw, st_bw=st_bw, bias=bias, mix=mix,
                          m1=m1, x1=x1, h2=h2, uf=uf, a=a, m2=m2, lbf=lbf, lbb=lbb, keys_n=keys_n, vals_b=vals_b))
        if l + 1 < depth:
            nx = mods[l + 1]
            x0, h = gate_norm(x1, m2, md["g2"], nx["ln1"], nx["sh1"], nx["sc1"], NC, R, f"gate_norm_end_{l}")
    sv, md = saved[-1], mods[-1]
    loss_terms, d_x1, d_m2, d_g2 = gate_loss(sv["x1"], sv["m2"], md["g2"], loss_target[0], NC, R, "gate_loss")
    loss = lax.psum(jnp.sum(loss_terms), ("x", "y", "c"))

    big_grads = [dict() for _ in range(depth)]
    core = ci.astype(jnp.int32).reshape(1)
    pairs, quads = {}, {}
    reduce_plan = {"ffn_down_bwd_0": [(1, "w_down")], "ffn_mid_bwd_0": [(1, "w_up")],
                   "ffn_up_bwd_0": [(1, "w_in"), (1, "w_out")], "na_bwd_0": [(0, "w_down"), (0, "w_up"), (0, "w_out")],
                   "proj_in_bwd_0": [(0, "w_in")]}

    def swapping(l, names, call):
        parts = [big_grads[l][n] for n in names]
        res = list(call(swap_halves_comm(parts)))
        for n, p, g in zip(names, parts, res[len(res) - len(names):]):
            pairs[(l, n)] = pair_sum(p, g, core, f"reduce_pair_sum_{n}_{l}")
        own = res[:len(res) - len(names)]
        return own[0] if len(own) == 1 else own

    def stage(l, names):
        swapping(l, names, lambda cm: _run_comm(cm, f"reduce_sibling_{l}_{names[0]}"))

    def reducing(name, call):
        items = reduce_plan.get(name, [])
        res = call(chip_alltoall_comm([pairs[k] for k in items]) if items else None)
        res = list(res) if isinstance(res, (list, tuple)) else [res]
        for k, q in zip(items, res[len(res) - len(items):]):
            quads[k] = q
        own = res[:len(res) - len(items)]
        return own[0] if len(own) == 1 else own

    small = [dict() for _ in range(depth)]
    d_ada = [None] * depth
    d_lb = [None] * depth
    for l in reversed(range(depth)):
        sv, md, wl = saved[l], mods[l], Wg[l]
        u, uf = sv["u"], sv["uf"]
        big_grads[l]["w_down"] = mm_tn(sv["a"], d_m2, 1, BF16, f"grad_w_down_{l}").reshape(4, F // 4, D)
        d_a = reducing(f"ffn_down_bwd_{l}", lambda cm: mm_nt(d_m2, wl["w_down"], F32, f"ffn_down_bwd_{l}", comm=cm))
        fcw_l, fcb_l = fcw_full[l], ffn_conv_b[l][None]
        dug, duv, dwg, dwv, dbg, dbv = reducing(
            f"ffn_mid_bwd_{l}", lambda cm: ffn_mid_bwd(uf, fcw_l, fcb_l, d_a, NC, R, FW, f"ffn_mid_bwd_{l}", comm=cm))
        d_uf = jnp.concatenate([dug, duv], axis=1)
        small[l]["ffn_conv_w"] = jnp.concatenate([dwg[:3], dwv[:3]], axis=1)
        small[l]["ffn_conv_b"] = jnp.concatenate([dbg[0], dbv[0]])
        big_grads[l]["w_up"] = mm_tn(sv["h2"], d_uf, 4, BF16, f"grad_w_up_{l}")
        d_h2 = reducing(f"ffn_up_bwd_{l}", lambda cm: mm_nt(d_uf, wl["w_up"], F32, f"ffn_up_bwd_{l}", comm=cm))
        d_x0, d_m1, dg1, dln2, dsh2, dsc2 = gate_norm_bwd(sv["x0"], sv["m1"], md["g1"], md["ln2"], md["sh2"], md["sc2"],
                                                          d_x1, d_h2, NC, R, f"gate_norm_mid_bwd_{l}")
        big_grads[l]["w_out"] = mm_tn(sv["mix"], d_m1, 1, BF16, f"grad_w_out_{l}").reshape(4, MIX // 4, D)
        if l == 0:
            d_mix = swapping(0, ("w_down", "w_up", "w_out"),
                             lambda cm: mm_nt(d_m1, wl["w_out"], F32, f"proj_out_bwd_{l}", comm=cm))
        else:
            d_mix = mm_nt(d_m1, wl["w_out"], F32, f"proj_out_bwd_{l}")
        hgn = hg_norm_w[l][None]
        d_o, d_hgg, d_hgn = hg_read_bwd(sv["o_fw"], sv["o_bw"], u, hgn, d_mix, gcol, 0, R, f"hg_read_bwd_{l}")
        dzf, dvf, dqf, dlbf = hgrn_bwd(u, sv["lbf"], sv["st_fw"], d_o, False, 0, NC, HGW, f"hgrn_fw_bwd_{l}")
        dzb, dvb, dqb, dlbb = hgrn_bwd(u, sv["lbb"], sv["st_bw"], d_o, True, 1, NC, HGW, f"hgrn_bw_bwd_{l}")
        d_lb[l] = (dlbf[0], dlbb[0])
        qn, kn, on = na_q_norm_w[l][None], na_k_norm_w[l][None], na_out_norm_w[l][None]
        d_nq, d_keys_n, d_nv, d_bias, d_qn, d_on = reducing(
            f"na_bwd_{l}", lambda cm: na_bwd(u, sv["keys_n"], sv["vals_b"], qn, on, sv["bias"], d_mix, qcol, mix_na, NC,
                                            f"na_bwd_{l}", comm=cm))
        d_nk, d_kn = kv_prep_bwd(u, kn, d_keys_n, kcol, NAW, R, f"kv_prep_bwd_{l}")
        cvw_l, cvo = cvw_full[l], cv_out_norm_w[l][None]
        d_cb, d_cy, d_cvo = short_conv(u, cvw_l, cvo, bcol, NC, CVW, R, bwd_dout=d_mix, ocol=mix_cv,
                                       name=f"short_conv_bwd_{l}")
        gcv = _head_group(nh_cv, bcol + nh_cv, bcol + 2 * nh_cv)
        d_cc, d_cvv, d_cvw = conv3_bwd(d_cy, u, cvw_l, NC, R, gcv * LANE,
                                       prod_cols=((bcol + nh_cv) // gcv, (bcol + 2 * nh_cv) // gcv),
                                       name=f"short_conv_taps_bwd_{l}")
        d_u = jnp.concatenate([dzf, dzb, dvf + dvb, d_nk, d_nv, dqf + dqb, d_hgg, d_nq, d_cb, d_cc, d_cvv],
                              axis=1).astype(BF16)
        big_grads[l]["w_in"] = mm_tn(sv["h"], d_u, 4, BF16, f"grad_w_in_{l}")
        if l > 0:
            d_h = swapping(l, proj, lambda cm: mm_nt(d_u, wl["w_in"], F32, f"proj_in_bwd_{l}", comm=cm))
        else:
            stage(0, ("w_in",))
            d_h = reducing(f"proj_in_bwd_{l}", lambda cm: mm_nt(d_u, wl["w_in"], F32, f"proj_in_bwd_{l}", comm=cm))
        small[l].update(hg_norm_w=d_hgn.sum(0)[0], na_q_norm_w=d_qn.sum(0)[0], na_k_norm_w=d_kn.sum(0)[0],
                        na_out_norm_w=d_on.reshape(-1), na_rpb=bias_pull[l](d_bias)[0], cv_w=d_cvw[:3],
                        cv_out_norm_w=d_cvo.reshape(-1), ln2_w=dln2.sum((0, 1)))
        if l > 0:
            pv, pm = saved[l - 1], mods[l - 1]
            d_x1, d_m2, dg2_prev, dln1, dsh1, dsc1 = gate_norm_bwd(pv["x1"], pv["m2"], pm["g2"], md["ln1"], md["sh1"],
                                                                   md["sc1"], d_x0, d_h, NC, R, f"gate_norm_end_bwd_{l - 1}")
        else:
            d_xin, _, _, dln1, dsh1, dsc1 = gate_norm_bwd(sv["x0"], None, None, md["ln1"], md["sh1"], md["sc1"], d_x0, d_h,
                                                          NC, R, "norm_in_bwd")
        small[l]["ln1_w"] = dln1.sum((0, 1))
        this_g2 = d_g2
        vecs = [v.sum(1) for v in (dsh1, dsc1, dg1, dsh2, dsc2, this_g2)]
        d_ada[l] = jnp.stack([jnp.concatenate([v[s] for v in vecs]) for s in (0, 1)])
        if l > 0:
            d_g2 = dg2_prev
    grad_x = d_xin[NC:][None]
    d_logits = lb_pull(jnp.stack([jnp.stack([d_lb[l][k] for l in range(depth)]) for k in (0, 1)]))[0]

    rep_names = ("ln1_w", "ln2_w", "hg_norm_w", "na_q_norm_w", "na_k_norm_w", "na_rpb", "na_out_norm_w",
                 "cv_out_norm_w", "ffn_conv_b", "cv_w", "ffn_conv_w")
    parts3 = [jnp.stack([small[l][n] for l in range(depth)]) for n in rep_names]
    parts3 += [d_logits, jnp.stack([d_ada[l][0] for l in range(depth)]), jnp.stack([d_ada[l][1] for l in range(depth)])]
    buf3, lay3 = _flat_rows(parts3, F32)
    g3 = allgather8([buf3], "gather_small_grads")[0]
    tot3 = _unflat(sum_leading(g3, F32, "sum_small_grads"), lay3)
    gsm = dict(zip(rep_names, tot3[:len(rep_names)]))
    gsm["hg_lb_logits"] = tot3[len(rep_names)]
    dctx_tot, dlat_tot = tot3[-2], tot3[-1]
    dlat_each = jnp.stack([_unflat(g3[d], lay3)[-1] for d in range(8)], axis=1)
    grads = {n: gsm[n].reshape(W[n].shape) for n in rep_names if n not in SHARDED_SMALL}
    for n in SHARDED_SMALL:
        wl_ = W[n].shape[-1]
        grads[n] = lax.dynamic_slice_in_dim(gsm[n], chip * wl_, wl_, axis=gsm[n].ndim - 1)
    grads["b_ada"] = dctx_tot + dlat_tot

    gw_ada, dms = [], []
    for l in range(depth):
        dm = jnp.concatenate([dlat_each[l], dctx_tot[l][None], jnp.zeros((7, ADA), F32)])
        dms.append(lax.dynamic_slice_in_dim(dm, chip * wcols, wcols, axis=1))
        gw_ada.append(mm_tn(s16, dms[l], 1, F32, f"grad_w_ada_{l}")[0])
    ds16 = mm_nt(jnp.concatenate(dms, axis=1), w_ada, F32, "ada_bwd")
    g4 = allgather8([ds16[8:16]], "gather_cond_grad")[0]
    d_scc = g4[0, 0] + g4[2, 0] + g4[4, 0] + g4[6, 0]
    sg = jax.nn.sigmoid(c_ctx)
    grads["c_ctx"] = d_scc * (sg * (1.0 + c_ctx * (1.0 - sg)))

    keys = [(l, n) for l in range(depth) for n in proj]
    mine = [sum_leading(quads[(l, n)], F32, f"reduce_chip_sum_{n}_{l}") for l, n in keys]

    delta, new_m, new_v = {}, {}, {}
    smalls = [n for n in WEIGHTS if n not in BIG]
    pw, lay_s = _flat_rows([W[n] for n in smalls], F32)
    pg, _ = _flat_rows([grads[n] for n in smalls], F32)
    pm, _ = _flat_rows([Mo[n] for n in smalls], F32)
    pvv, _ = _flat_rows([Vo[n] for n in smalls], F32)
    _, d_, m_, v_, *other = adamw(pw, [pg], pm, pvv, "adamw_small", comm=share_halves_comm(mine))
    for n, dd, mm_, vv in zip(smalls, _unflat(d_, lay_s), _unflat(m_, lay_s), _unflat(v_, lay_s)):
        delta[n], new_m[n], new_v[n] = dd, mm_, vv
    mine_by, other_by = {k: [None] * depth for k in proj}, {k: [None] * depth for k in proj}
    for (l, k), a, b in zip(keys, mine, other):
        mine_by[k][l], other_by[k][l] = a, b
    for n in BIG:
        shp = W[n].shape
        two = lambda a: a.reshape(-1, shp[-1])
        if n == "w_ada":
            g_, d_, m_, v_ = adamw(two(W[n]), gw_ada, two(Mo[n]), two(Vo[n]), f"adamw_{n}")
        else:
            g_, d_, m_, v_ = adamw_halves(two(W[n]), mine_by[n], other_by[n], two(Mo[n]), two(Vo[n]), core, f"adamw_{n}")
        grads[n], delta[n], new_m[n], new_v[n] = g_.reshape(shp), d_.reshape(shp), m_.reshape(shp), v_.reshape(shp)

    return (loss, grad_x, *[grads[n] for n in WEIGHTS], *[delta[n] for n in WEIGHTS],
            *[new_m[n] for n in WEIGHTS], *[new_v[n] for n in WEIGHTS])
```

```python
import functools
import math

import numpy as np
import jax
import jax.numpy as jnp
from jax import lax
from jax.experimental import pallas as pl
from jax.experimental.pallas import tpu as pltpu

F32 = jnp.float32
BF16 = jnp.bfloat16
MESH = pl.DeviceIdType.MESH
ANY = pl.BlockSpec(memory_space=pl.ANY)
VMEM_SPEC = pl.BlockSpec(memory_space=pltpu.VMEM)

LANE = 128
CHUNK = 64
SUB = 16
GRID_W = 64
WIN_R = 8
WIN_C = 16
EPS = 1e-6
F_FLOOR = 1e-30
NEG_INF = -1e30
EXP_CLAMP = 80.0
ATTN_SCALE = LANE ** -0.5
VMEM_LIMIT = 56 * 1024 * 1024
ADAM_LR, ADAM_B1, ADAM_B2, ADAM_EPS, ADAM_WD, ADAM_STEP = 0.001, 0.9, 0.999, 1e-08, 0.01, 10


def _cp(*sem):
    return pltpu.CompilerParams(dimension_semantics=sem or None, vmem_limit_bytes=VMEM_LIMIT)


def _me():
    return lax.axis_index("x"), lax.axis_index("y"), lax.axis_index("c")


def allgather8(blocks, name, hbm=False):
    na = len(blocks)
    comm = allgather8_comm(blocks)

    def body(*refs):
        comm["start"](refs[:na], refs[na:2 * na], refs[2 * na:])
        comm["finish"](refs[:na], refs[na:2 * na], refs[2 * na:])

    spec = ANY if hbm else VMEM_SPEC
    return pl.pallas_call(
        body, name=name, out_shape=comm["outs"], in_specs=[spec] * na, out_specs=[spec] * na,
        scratch_shapes=comm["scratch"], compiler_params=pltpu.CompilerParams(vmem_limit_bytes=VMEM_LIMIT),
    )(*blocks)


def allgather8_comm(blocks):
    na = len(blocks)

    def parts(x_refs, out_refs, sems):
        send_sems, recv_sems, local_sems = sems
        x, y, c = _me()
        me, sibling = (x, y, c), (x, y, 1 - c)
        chips = [(1 - x, y), (x, 1 - y), (1 - x, 1 - y)]

        def rows(a, px, py, pc):
            return out_refs[a].at[4 * px + 2 * py + pc]

        def copy(a, k, blk, to, src=None):
            return pltpu.make_async_remote_copy(
                src_ref=rows(a, *blk) if src is None else src, dst_ref=rows(a, *blk),
                send_sem=send_sems.at[a, k], recv_sem=recv_sems.at[a, k], device_id=to, device_id_type=MESH)

        mine = [pltpu.make_async_copy(x_refs[a], rows(a, *me), local_sems.at[a]) for a in range(na)]
        first = []
        for a in range(na):
            first.append(copy(a, 0, me, sibling, src=x_refs[a]))
            first += [copy(a, 1 + j, me, (*chip, c), src=x_refs[a]) for j, chip in enumerate(chips)]
        return c, me, sibling, chips, copy, mine, first

    def start(x_refs, out_refs, sems):
        _, _, _, _, _, mine, first = parts(x_refs, out_refs, sems)
        for cp in mine + first:
            cp.start()

    def finish(x_refs, out_refs, sems):
        c, me, sibling, chips, copy, mine, first = parts(x_refs, out_refs, sems)
        passed = []
        for j, chip in enumerate(chips):
            for a in range(na):
                copy(a, 1 + j, (*chip, c), me).wait_recv()
                passed.append(copy(a, 4 + j, (*chip, c), sibling))
                passed[-1].start()
        for a in range(na):
            copy(a, 0, sibling, me).wait_recv()
            for j, chip in enumerate(chips):
                copy(a, 4 + j, (*chip, 1 - c), me).wait_recv()
        for cp in first + passed:
            cp.wait_send()
        for cp in mine:
            cp.wait()

    return dict(ins=list(blocks), outs=[jax.ShapeDtypeStruct((8,) + b.shape, b.dtype) for b in blocks],
                scratch=[pltpu.SemaphoreType.DMA((na, 7)), pltpu.SemaphoreType.DMA((na, 7)),
                         pltpu.SemaphoreType.DMA((na,))], start=start, finish=finish)


def _run_comm(comm, name):
    na = len(comm["ins"])

    def body(*refs):
        comm["start"](refs[:na], refs[na:2 * na], refs[2 * na:])
        comm["finish"](refs[:na], refs[na:2 * na], refs[2 * na:])

    return pl.pallas_call(body, name=name, out_shape=comm["outs"], in_specs=[ANY] * na, out_specs=[ANY] * na,
                          scratch_shapes=comm["scratch"])(*comm["ins"])


def swap_halves_comm(gs):
    na = len(gs)
    hrs = [g.shape[1] // 2 for g in gs]

    def copies(g_refs, o_refs, sems):
        send_sems, recv_sems = sems
        x, y, c = _me()
        cps = []
        for a in range(na):
            for s in range(4):
                src = g_refs[a].at[s, pl.ds(pl.multiple_of((1 - c) * hrs[a], 16), hrs[a]), :]
                cps.append(pltpu.make_async_remote_copy(
                    src_ref=src, dst_ref=o_refs[a].at[s], send_sem=send_sems.at[a, s], recv_sem=recv_sems.at[a, s],
                    device_id=(x, y, 1 - c), device_id_type=MESH))
        return cps

    def start(g_refs, o_refs, sems):
        for cp in copies(g_refs, o_refs, sems):
            cp.start()

    def finish(g_refs, o_refs, sems):
        for cp in copies(g_refs, o_refs, sems):
            cp.wait()

    return dict(ins=list(gs), outs=[jax.ShapeDtypeStruct((4, hrs[a], gs[a].shape[2]), gs[a].dtype) for a in range(na)],
                scratch=[pltpu.SemaphoreType.DMA((na, 4)), pltpu.SemaphoreType.DMA((na, 4))], start=start, finish=finish)


def chip_alltoall_comm(gs):
    na = len(gs)

    def copies(g_refs, o_refs, sems):
        send_sems, recv_sems, local_sems = sems
        x, y, c = _me()
        mine = 2 * x + y
        cps = []
        for a in range(na):
            cps.append(pltpu.make_async_copy(g_refs[a].at[mine], o_refs[a].at[mine], local_sems.at[a]))
            for k, (px, py) in enumerate([(1 - x, y), (x, 1 - y), (1 - x, 1 - y)]):
                cps.append(pltpu.make_async_remote_copy(
                    src_ref=g_refs[a].at[2 * px + py], dst_ref=o_refs[a].at[mine], send_sem=send_sems.at[a, k],
                    recv_sem=recv_sems.at[a, k], device_id=(px, py, c), device_id_type=MESH))
        return cps

    def start(g_refs, o_refs, sems):
        for cp in copies(g_refs, o_refs, sems):
            cp.start()

    def finish(g_refs, o_refs, sems):
        for cp in copies(g_refs, o_refs, sems):
            cp.wait()

    return dict(ins=list(gs), outs=[jax.ShapeDtypeStruct(g.shape, g.dtype) for g in gs],
                scratch=[pltpu.SemaphoreType.DMA((na, 3)), pltpu.SemaphoreType.DMA((na, 3)),
                         pltpu.SemaphoreType.DMA((na,))], start=start, finish=finish)


def share_halves_comm(vs):
    na = len(vs)

    def copies(v_refs, o_refs, sems):
        send_sems, recv_sems = sems
        x, y, c = _me()
        return [pltpu.make_async_remote_copy(
            src_ref=v_refs[a], dst_ref=o_refs[a], send_sem=send_sems.at[a], recv_sem=recv_sems.at[a],
            device_id=(x, y, 1 - c), device_id_type=MESH) for a in range(na)]

    def start(v_refs, o_refs, sems):
        for cp in copies(v_refs, o_refs, sems):
            cp.start()

    def finish(v_refs, o_refs, sems):
        for cp in copies(v_refs, o_refs, sems):
            cp.wait()

    return dict(ins=list(vs), outs=[jax.ShapeDtypeStruct(v.shape, v.dtype) for v in vs],
                scratch=[pltpu.SemaphoreType.DMA((na,)), pltpu.SemaphoreType.DMA((na,))], start=start, finish=finish)


def _row_block(rows, cap):
    rb = math.gcd(rows, cap)
    return rb if rb % 8 == 0 else rows


def sum_leading(x, out_dtype, name):
    n, r, c = x.shape
    rb = _row_block(r, 1024)

    def body(x_ref, o_ref):
        acc = x_ref[0].astype(F32)
        for k in range(1, n):
            acc = acc + x_ref[k].astype(F32)
        o_ref[...] = acc.astype(o_ref.dtype)

    return pl.pallas_call(
        body, name=name, grid=(r // rb,), out_shape=jax.ShapeDtypeStruct((r, c), out_dtype),
        in_specs=[pl.BlockSpec((n, rb, c), lambda i: (0, i, 0))], out_specs=pl.BlockSpec((rb, c), lambda i: (i, 0)),
        compiler_params=_cp("parallel"),
    )(x)


def pair_sum(g, got, core, name):
    _, r2, n = g.shape
    hr = r2 // 2
    rb = math.gcd(hr, 512)
    nb = hr // rb

    def body(c_ref, a_ref, b_ref, o_ref):
        o_ref[...] = (a_ref[...].astype(F32) + b_ref[...].astype(F32)).astype(o_ref.dtype)

    spec = pl.BlockSpec((None, rb, n), lambda s, i, c_ref: (s, i, 0))
    return pl.pallas_call(
        body, name=name, out_shape=jax.ShapeDtypeStruct((4, hr, n), g.dtype),
        grid_spec=pltpu.PrefetchScalarGridSpec(
            num_scalar_prefetch=1, grid=(4, nb),
            in_specs=[pl.BlockSpec((None, rb, n), lambda s, i, c_ref: (s, c_ref[0] * nb + i, 0)), spec],
            out_specs=spec),
        compiler_params=_cp("parallel", "parallel"),
    )(core, g, got)


def adamw(w, gs, m, v, name, comm=None):
    rows, c = w.shape
    ng = len(gs)
    r = rows // ng
    rb = _row_block(r, 128 if c > 2048 else 256 if c > 1024 else 1024)
    nb = r // rb
    bc1 = 1.0 - ADAM_B1 ** ADAM_STEP
    bc2 = 1.0 - ADAM_B2 ** ADAM_STEP

    def body(w_ref, *refs):
        g_refs, (m_ref, v_ref, g_out, d_ref, nm_ref, nv_ref) = refs[:ng], refs[ng:]
        part = pl.program_id(0) // nb
        gg = g_refs[0][...]
        for k in range(1, ng):
            gg = jnp.where(part == k, g_refs[k][...], gg)
        nm = ADAM_B1 * m_ref[...] + (1.0 - ADAM_B1) * gg
        nv = ADAM_B2 * v_ref[...] + (1.0 - ADAM_B2) * (gg * gg)
        g_out[...] = gg
        d_ref[...] = -ADAM_LR * ((nm / bc1) / (jnp.sqrt(nv / bc2) + ADAM_EPS) + ADAM_WD * w_ref[...])
        nm_ref[...] = nm
        nv_ref[...] = nv

    spec = pl.BlockSpec((rb, c), lambda i: (i, 0))
    gspecs = [pl.BlockSpec((rb, c), functools.partial(lambda k, i: (jnp.clip(i - k * nb, 0, nb - 1), 0), k))
              for k in range(ng)]
    sds = jax.ShapeDtypeStruct((rows, c), F32)
    return _pcall(body, name=name, grid=(ng * nb,), out_shape=(sds,) * 4, in_specs=[spec] + gspecs + [spec, spec],
                  out_specs=(spec,) * 4, args=(w, *gs, m, v), sem=("parallel",), comm=comm)


def adamw_halves(w, mine, other, m, v, core, name):
    rows, c = w.shape
    nl = len(mine)
    hr = rows // (2 * nl)
    rb = _row_block(hr, 128 if c > 2048 else 256 if c > 1024 else 1024)
    nb = hr // rb
    bc1 = 1.0 - ADAM_B1 ** ADAM_STEP
    bc2 = 1.0 - ADAM_B2 ** ADAM_STEP

    def body(c_ref, w_ref, *refs):
        mine_refs, other_refs = refs[:nl], refs[nl:2 * nl]
        m_ref, v_ref, g_out, d_ref, nm_ref, nv_ref = refs[2 * nl:]
        part = pl.program_id(0) // nb
        layer, half = part // 2, part % 2
        gg = jnp.where(half == c_ref[0], mine_refs[0][...], other_refs[0][...])
        for k in range(1, nl):
            gg = jnp.where(layer == k, jnp.where(half == c_ref[0], mine_refs[k][...], other_refs[k][...]), gg)
        nm = ADAM_B1 * m_ref[...] + (1.0 - ADAM_B1) * gg
        nv = ADAM_B2 * v_ref[...] + (1.0 - ADAM_B2) * (gg * gg)
        g_out[...] = gg
        d_ref[...] = -ADAM_LR * ((nm / bc1) / (jnp.sqrt(nv / bc2) + ADAM_EPS) + ADAM_WD * w_ref[...])
        nm_ref[...] = nm
        nv_ref[...] = nv

    spec = pl.BlockSpec((rb, c), lambda i, c_ref: (i, 0))
    gspecs = [pl.BlockSpec((rb, c), functools.partial(
        lambda k, i, c_ref: (jnp.clip(i - 2 * k * nb, 0, 2 * nb - 1) % nb, 0), k)) for k in range(nl)]
    sds = jax.ShapeDtypeStruct((rows, c), F32)
    return pl.pallas_call(
        body, name=name, out_shape=(sds,) * 4,
        grid_spec=pltpu.PrefetchScalarGridSpec(
            num_scalar_prefetch=1, grid=(2 * nl * nb,), in_specs=[spec] + gspecs + gspecs + [spec, spec],
            out_specs=(spec,) * 4),
        compiler_params=_cp("parallel"),
    )(core, w, *mine, *other, m, v)


def _pick(n, prefs):
    for p in prefs:
        if n % p == 0:
            return p
    return n


def _hosted(body, n_in, n_out, comm, first, last):
    if comm is None:
        return body
    k, ns = len(comm["ins"]), len(comm["scratch"])

    def wrapped(*refs):
        ins, cins = refs[:n_in], refs[n_in:n_in + k]
        outs, couts = refs[n_in + k:n_in + k + n_out], refs[n_in + k + n_out:n_in + 2 * k + n_out]
        rest = refs[n_in + 2 * k + n_out:]
        scratch, sems = rest[:len(rest) - ns], rest[len(rest) - ns:]

        @pl.when(first())
        def _():
            comm["start"](cins, couts, sems)

        body(*ins, *outs, *scratch)

        @pl.when(last())
        def _():
            comm["finish"](cins, couts, sems)

    return wrapped


def _comm_extras(comm):
    if comm is None:
        return [], [], [], []
    return list(comm["ins"]), [ANY] * len(comm["ins"]), list(comm["outs"]), list(comm["scratch"])


def _mm_body(dims, nk, out_dtype):
    def body(a_ref, b_ref, o_ref, acc=None):
        kk = pl.program_id(2)
        part = lax.dot_general(a_ref[...].astype(BF16), b_ref[...].astype(BF16), (dims, ((), ())),
                               preferred_element_type=F32)
        if nk == 1:
            o_ref[...] = part.astype(out_dtype)
        else:
            @pl.when(kk == 0)
            def _():
                acc[...] = part

            @pl.when(kk > 0)
            def _():
                acc[...] += part

            @pl.when(kk == nk - 1)
            def _():
                o_ref[...] = acc[...].astype(out_dtype)
    return body


def _acc(nk, shape):
    return [pltpu.VMEM(shape, F32)] if nk > 1 else []


def mm_nn(a, w, out_dtype, name, comm=None):
    M, K = a.shape
    S, _, Ns = w.shape
    tm = _pick(M, (1088, 1024, 512, 256, 128))
    tn = _pick(Ns, (1024, 896, 1408, 512, 256, 128))
    tk = _pick(K, (2816, 2048, 1408, 1024, 512, 256, 128))
    nps, nk = Ns // tn, K // tk
    grid = (S * nps, M // tm, nk)
    ids = lambda: [pl.program_id(d) for d in range(3)]
    first = lambda: functools.reduce(jnp.logical_and, [p == 0 for p in ids()])
    last = lambda: functools.reduce(jnp.logical_and, [p == g - 1 for p, g in zip(ids(), grid)])
    cin, cspec, cout, csem = _comm_extras(comm)
    out = pl.pallas_call(
        _hosted(_mm_body(((1,), (0,)), nk, out_dtype), 2, 1, comm, first, last), name=name, grid=grid,
        out_shape=[jax.ShapeDtypeStruct((M, S * Ns), out_dtype)] + cout,
        in_specs=[pl.BlockSpec((tm, tk), lambda j, i, k: (i, k)),
                  pl.BlockSpec((None, tk, tn), lambda j, i, k: (j // nps, k, j % nps))] + cspec,
        out_specs=[pl.BlockSpec((tm, tn), lambda j, i, k: (i, j))] + cspec,
        scratch_shapes=_acc(nk, (tm, tn)) + csem,
        compiler_params=_cp(*(("arbitrary",) * 3 if comm else ("parallel", "parallel", "arbitrary"))),
    )(a, w, *cin)
    return out[0] if comm is None else out


def mm_nt(dy, w, out_dtype, name, comm=None):
    M, N = dy.shape
    S, K, Ns = w.shape
    tm = _pick(M, (1088, 1024, 512, 256, 128))
    tn = _pick(K, (1408, 1024, 512, 256, 128))
    tk = _pick(Ns, (2816, 2048, 1792, 1408, 1024, 896, 512, 256, 128))
    kps, nk = Ns // tk, N // tk
    out = _pcall(
        _mm_body(((1,), (1,)), nk, out_dtype), name=name, grid=(K // tn, M // tm, nk),
        out_shape=[jax.ShapeDtypeStruct((M, K), out_dtype)],
        in_specs=[pl.BlockSpec((tm, tk), lambda j, i, k: (i, k)),
                  pl.BlockSpec((None, tn, tk), lambda j, i, k: (k // kps, j, k % kps))],
        out_specs=[pl.BlockSpec((tm, tn), lambda j, i, k: (i, j))], args=(dy, w),
        scratch=_acc(nk, (tm, tn)), sem=("parallel", "parallel", "arbitrary"), comm=comm)
    return out[0] if comm is None else out


def mm_tn(a, dy, S, out_dtype, name):
    M, K = a.shape
    N = dy.shape[1]
    Ns = N // S
    to = _pick(K, (1024, 512, 256, 128))
    tn = _pick(Ns, (2816, 2048, 1792, 1408, 1024, 896, 512, 256, 128))
    tk = _pick(M, (1088, 1024, 512, 256, 128))
    nps, nk = Ns // tn, M // tk
    return pl.pallas_call(
        _mm_body(((0,), (0,)), nk, out_dtype), name=name, grid=(K // to, S * nps, nk),
        out_shape=jax.ShapeDtypeStruct((S, K, Ns), out_dtype),
        in_specs=[pl.BlockSpec((tk, to), lambda i, j, k: (k, i)),
                  pl.BlockSpec((tk, tn), lambda i, j, k: (k, j))],
        out_specs=pl.BlockSpec((None, to, tn), lambda i, j, k: (j // nps, i, j % nps)),
        scratch_shapes=_acc(nk, (to, tn)), compiler_params=_cp("parallel", "parallel", "arbitrary"),
    )(a, dy)


_DIMS = {"nn": ((1,), (0,)), "nt": ((1,), (1,)), "tn": ((0,), (0,))}


def _dot(a, b, mode):
    return lax.dot_general(a.astype(BF16), b.astype(BF16), (_DIMS[mode], ((), ())), preferred_element_type=F32)


@functools.partial(jax.custom_vjp, nondiff_argnums=(2,))
def mmf(a, b, mode):
    return _dot(a, b, mode)


def _mmf_fwd(a, b, mode):
    return _dot(a, b, mode), (a, b)


def _mmf_bwd(mode, res, ct):
    a, b = res
    if mode == "nn":
        return _dot(ct, b, "nt"), _dot(a, ct, "tn")
    if mode == "nt":
        return _dot(ct, b, "nn"), _dot(ct, a, "tn")
    return _dot(b, ct, "nt"), _dot(a, ct, "nn")


mmf.defvjp(_mmf_fwd, _mmf_bwd)


def _dot_hi(m, g):
    return jnp.dot(m, g, precision=lax.Precision.HIGHEST, preferred_element_type=F32)


@jax.custom_vjp
def cumdot(m, mt, g):
    return _dot_hi(m, g)


def _cumdot_fwd(m, mt, g):
    return _dot_hi(m, g), (m, mt)


def _cumdot_bwd(res, ct):
    m, mt = res
    return jnp.zeros_like(m), jnp.zeros_like(mt), _dot_hi(mt, ct)


cumdot.defvjp(_cumdot_fwd, _cumdot_bwd)


def _rms(x, w):
    return x * lax.rsqrt(jnp.mean(x * x, axis=-1, keepdims=True) + EPS) * w


def _silu(x):
    return x * jax.nn.sigmoid(x)


RT = 16


def _gn_math(has_gate, x, m, gate, lnw, shift, scale):
    xn = x + gate * m if has_gate else x
    h = _rms(xn, lnw) * (1.0 + scale) + shift
    return xn, h


def _seg_spec(width, ncb):
    return pl.BlockSpec((None, RT, width), lambda i: (jnp.minimum(i // ncb, 1), 0, 0))


def gate_norm(x, m, gate, lnw, shift, scale, nc, R, name, comm=None):
    T, D = x.shape
    has_gate = m is not None
    ncb = nc // R

    def body(*refs):
        if has_gate:
            x_ref, m_ref, g_ref, w_ref, sh_ref, sc_ref, xn_ref, h_ref = refs
        else:
            x_ref, w_ref, sh_ref, sc_ref, h_ref = refs

        def step(t, carry):
            rows = pl.ds(pl.multiple_of(t * RT, RT), RT)
            xn, h = _gn_math(has_gate, x_ref[rows, :], m_ref[rows, :] if has_gate else None,
                             g_ref[...] if has_gate else None, w_ref[...], sh_ref[...], sc_ref[...])
            if has_gate:
                xn_ref[rows, :] = xn
            h_ref[rows, :] = h.astype(BF16)
            return carry

        lax.fori_loop(0, R // RT, step, 0)

    row = pl.BlockSpec((R, D), lambda i: (i, 0))
    seg = _seg_spec(D, ncb)
    shared = pl.BlockSpec((None, RT, D), lambda i: (0, 0, 0))
    if has_gate:
        ins, in_specs = (x, m, gate, lnw, shift, scale), [row, row, seg, shared, seg, seg]
        out_shape = (jax.ShapeDtypeStruct((T, D), F32), jax.ShapeDtypeStruct((T, D), BF16))
        out_specs = (row, row)
    else:
        ins, in_specs = (x, lnw, shift, scale), [row, shared, seg, seg]
        out_shape, out_specs = (jax.ShapeDtypeStruct((T, D), BF16),), (row,)
    out = _pcall(body, name=name, grid=(T // R,), out_shape=out_shape, in_specs=in_specs, out_specs=out_specs,
                 args=ins, sem=("parallel",), comm=comm)
    own = tuple(out[:2]) if has_gate else (None, out[0])
    return own if comm is None else own + tuple(out[2 if has_gate else 1:])


def gate_norm_bwd(x, m, gate, lnw, shift, scale, dxn, dh, nc, R, name):
    T, D = x.shape
    has_gate = m is not None
    ncb = nc // R

    def body(*refs):
        if has_gate:
            (x_ref, m_ref, g_ref, w_ref, sh_ref, sc_ref, dxn_ref, dh_ref,
             dx_ref, dm_ref, dg_ref, dw_ref, dsh_ref, dsc_ref) = refs
        else:
            x_ref, w_ref, sh_ref, sc_ref, dxn_ref, dh_ref, dx_ref, dw_ref, dsh_ref, dsc_ref = refs
        i = pl.program_id(0)

        @pl.when(i == 0)
        def _():
            dw_ref[...] = jnp.zeros_like(dw_ref)

        @pl.when((i == 0) | (i == ncb))
        def _():
            dsh_ref[...] = jnp.zeros_like(dsh_ref)
            dsc_ref[...] = jnp.zeros_like(dsc_ref)
            if has_gate:
                dg_ref[...] = jnp.zeros_like(dg_ref)

        def step(t, carry):
            rows = pl.ds(pl.multiple_of(t * RT, RT), RT)
            ct = (dxn_ref[rows, :], dh_ref[rows, :])
            if has_gate:
                _, vjp = jax.vjp(functools.partial(_gn_math, True), x_ref[rows, :], m_ref[rows, :], g_ref[...],
                                 w_ref[...], sh_ref[...], sc_ref[...])
                dx, dm, dg, dw, dsh, dsc = vjp(ct)
                dm_ref[rows, :] = dm.astype(BF16)
                dg_ref[...] += dg
            else:
                f = lambda x_, w_, sh_, sc_: _gn_math(False, x_, None, None, w_, sh_, sc_)[1]
                _, vjp = jax.vjp(f, x_ref[rows, :], w_ref[...], sh_ref[...], sc_ref[...])
                dx, dw, dsh, dsc = vjp(ct[1])
                dx = dx + ct[0]
            dx_ref[rows, :] = dx
            dw_ref[...] += dw
            dsh_ref[...] += dsh
            dsc_ref[...] += dsc
            return carry

        lax.fori_loop(0, R // RT, step, 0)

    row = pl.BlockSpec((R, D), lambda i: (i, 0))
    seg = _seg_spec(D, ncb)
    shared = pl.BlockSpec((None, RT, D), lambda i: (0, 0, 0))
    full, segs, one = jax.ShapeDtypeStruct((T, D), F32), jax.ShapeDtypeStruct((2, RT, D), F32), \
        jax.ShapeDtypeStruct((1, RT, D), F32)
    if has_gate:
        ins = (x, m, gate, lnw, shift, scale, dxn, dh)
        in_specs = [row, row, seg, shared, seg, seg, row, row]
        out_shape = (full, jax.ShapeDtypeStruct((T, D), BF16), segs, one, segs, segs)
        out_specs = (row, row, seg, shared, seg, seg)
    else:
        ins = (x, lnw, shift, scale, dxn, dh)
        in_specs = [row, shared, seg, seg, row, row]
        out_shape = (full, one, segs, segs)
        out_specs = (row, shared, seg, seg)
    out = pl.pallas_call(body, name=name, grid=(T // R,), out_shape=out_shape, in_specs=in_specs,
                         out_specs=out_specs, compiler_params=_cp("arbitrary"))(*ins)
    if has_gate:
        return out
    dx, dw, dsh, dsc = out
    return dx, None, None, dw, dsh, dsc


def gate_loss(x, m, gate, target, nc, R, name):
    T, D = x.shape
    ncb = nc // R

    def body(x_ref, m_ref, g_ref, t_ref, loss_ref, dx_ref, dm_ref, dg_ref):
        i = pl.program_id(0)

        @pl.when(i == 0)
        def _():
            loss_ref[...] = jnp.zeros_like(loss_ref)

        @pl.when((i == 0) | (i == ncb))
        def _():
            dg_ref[...] = jnp.zeros_like(dg_ref)

        live = jnp.where(i >= ncb, 1.0, 0.0).astype(F32)

        def step(t, carry):
            rows = pl.ds(pl.multiple_of(t * RT, RT), RT)
            mm_ = m_ref[rows, :]
            g = g_ref[...]
            e = (x_ref[rows, :] + g * mm_ - t_ref[rows, :]) * live
            dy = e * (1.0 / D)
            loss_ref[...] += 0.5 * e * dy
            dx_ref[rows, :] = dy
            dm_ref[rows, :] = (dy * g).astype(BF16)
            dg_ref[...] += dy * mm_
            return carry

        lax.fori_loop(0, R // RT, step, 0)

    row = pl.BlockSpec((R, D), lambda i: (i, 0))
    seg = _seg_spec(D, ncb)
    return pl.pallas_call(
        body, name=name, grid=(T // R,),
        out_shape=(jax.ShapeDtypeStruct((RT, D), F32), jax.ShapeDtypeStruct((T, D), F32),
                   jax.ShapeDtypeStruct((T, D), BF16), jax.ShapeDtypeStruct((2, RT, D), F32)),
        in_specs=[row, row, seg, pl.BlockSpec((R, D), lambda i: (jnp.maximum(i - ncb, 0), 0))],
        out_specs=(pl.BlockSpec((RT, D), lambda i: (0, 0)), row, row, seg),
        compiler_params=_cp("arbitrary"),
    )(x, m, gate, target)


def _hg_chunk(rev, lb, z, iv, hq, st):
    f = lb + (1.0 - lb) * jax.nn.sigmoid(z)
    g = jnp.log(jnp.maximum(f, F_FLOOR))
    k = (1.0 - lb) * jax.nn.sigmoid(-z)
    q = _silu(hq)
    ri = lax.broadcasted_iota(jnp.int32, (CHUNK, CHUNK), 0)
    ci = lax.broadcasted_iota(jnp.int32, (CHUNK, CHUNK), 1)
    r1 = lax.broadcasted_iota(jnp.int32, (CHUNK, 1), 0)
    seen = (ci >= ri) if rev else (ci <= ri)
    seen_t = (ci <= ri) if rev else (ci >= ri)
    cum = cumdot(seen.astype(F32), seen_t.astype(F32), g)
    tot = jnp.sum(g, axis=0, keepdims=True)
    att = jnp.zeros((CHUNK, CHUNK), F32)
    ref_rows = jnp.zeros_like(g)
    refs = []
    for b in range(CHUNK // SUB):
        before = (r1 >= SUB * (b + 1)) if rev else (r1 < SUB * b)
        r_b = jnp.sum(jnp.where(before, g, 0.0), axis=0, keepdims=True)
        in_b = (r1 >= SUB * b) & (r1 < SUB * (b + 1))
        ref_rows = ref_rows + jnp.where(in_b, r_b, 0.0)
        refs.append(r_b)
    qd = q * jnp.exp(cum - ref_rows)
    for b in range(CHUNK // SUB):
        kd = k * jnp.exp(jnp.minimum(refs[b] - cum, EXP_CLAMP))
        in_b = (ri >= SUB * b) & (ri < SUB * (b + 1))
        att = att + jnp.where(in_b, mmf(qd, kd, "nt"), 0.0)
    att = jnp.where(seen, att, 0.0)
    o = mmf(att, iv, "nn") + mmf(q * jnp.exp(cum), st, "nt")
    st_new = st * jnp.exp(tot) + mmf(iv, k * jnp.exp(tot - cum), "tn")
    return st_new, o


def _hg_cid(rev, i, ncs, n):
    if not rev:
        return i
    return jnp.where(i < ncs, ncs - 1 - i, ncs + n - 1 - i)


def hgrn_fwd(u, lb, rev, zcol, nc, hgw, name):
    T = u.shape[0]
    n, ncs, nh = T // CHUNK, nc // CHUNK, hgw // LANE

    def body(z_ref, v_ref, q_ref, lb_ref, o_ref, s_ref, st):
        i = pl.program_id(0)

        @pl.when(i == 0)
        def _():
            st[...] = jnp.zeros_like(st)

        for h in range(nh):
            cols = slice(h * LANE, (h + 1) * LANE)
            s_ref[h] = st[h]
            s_new, o = _hg_chunk(rev, lb_ref[:, cols], z_ref[:, cols], v_ref[:, cols], q_ref[:, cols], st[h])
            st[h] = s_new
            o_ref[:, cols] = o

    def col(cb):
        return pl.BlockSpec((CHUNK, hgw), lambda i: (_hg_cid(rev, i, ncs, n), cb))

    return pl.pallas_call(
        body, name=name, grid=(n,),
        out_shape=(jax.ShapeDtypeStruct((T, hgw), F32), jax.ShapeDtypeStruct((n, nh, LANE, LANE), F32)),
        in_specs=[col(zcol), col(2), col(7), pl.BlockSpec((1, hgw), lambda i: (0, 0))],
        out_specs=(pl.BlockSpec((CHUNK, hgw), lambda i: (_hg_cid(rev, i, ncs, n), 0)),
                   pl.BlockSpec((None, nh, LANE, LANE), lambda i: (i, 0, 0, 0))),
        scratch_shapes=[pltpu.VMEM((nh, LANE, LANE), F32)], compiler_params=_cp("arbitrary"),
    )(u, u, u, lb)


def hgrn_bwd(u, lb, states, do, rev, zcol, nc, hgw, name):
    T = u.shape[0]
    n, ncs, nh = T // CHUNK, nc // CHUNK, hgw // LANE

    def body(z_ref, v_ref, q_ref, lb_ref, s_ref, do_ref, dz_ref, dv_ref, dq_ref, dlb_ref, dst):
        j = pl.program_id(0)

        @pl.when(j == 0)
        def _():
            dst[...] = jnp.zeros_like(dst)
            dlb_ref[...] = jnp.zeros_like(dlb_ref)

        for h in range(nh):
            cols = slice(h * LANE, (h + 1) * LANE)
            _, vjp = jax.vjp(functools.partial(_hg_chunk, rev), lb_ref[:, cols], z_ref[:, cols], v_ref[:, cols],
                             q_ref[:, cols], s_ref[h])
            dlb, dz, dv, dq, ds = vjp((dst[h], do_ref[:, cols]))
            dst[h] = ds
            dz_ref[:, cols] = dz
            dv_ref[:, cols] = dv
            dq_ref[:, cols] = dq
            dlb_ref[:, cols] += dlb

    def cid(j):
        return _hg_cid(rev, n - 1 - j, ncs, n)

    def col(cb):
        return pl.BlockSpec((CHUNK, hgw), lambda j: (cid(j), cb))

    out = pl.BlockSpec((CHUNK, hgw), lambda j: (cid(j), 0))
    full = jax.ShapeDtypeStruct((T, hgw), F32)
    return pl.pallas_call(
        body, name=name, grid=(n,),
        out_shape=(full, full, full, jax.ShapeDtypeStruct((1, hgw), F32)),
        in_specs=[col(zcol), col(2), col(7), pl.BlockSpec((1, hgw), lambda j: (0, 0)),
                  pl.BlockSpec((None, nh, LANE, LANE), lambda j: (n - 1 - j, 0, 0, 0)), out],
        out_specs=(out, out, out, pl.BlockSpec((1, hgw), lambda j: (0, 0))),
        scratch_shapes=[pltpu.VMEM((nh, LANE, LANE), F32)], compiler_params=_cp("arbitrary"),
    )(u, u, u, lb, states, do)


HT = 128


def _head_group(nh, *col_offsets):
    for g in (4, 2):
        if nh % g == 0 and all(c % g == 0 for c in col_offsets):
            return g
    return 1


def _read_math(ofw, obw, g, w):
    return _rms(ofw + obw, w) * _silu(g)


def hg_read(ofw, obw, u, w, gcol, R, name):
    T, hgw = ofw.shape
    nh = hgw // LANE
    g = _head_group(nh, gcol)

    def body(a_ref, b_ref, g_ref, w_ref, o_ref):
        for j in range(g):
            cols = slice(j * LANE, (j + 1) * LANE)
            for t in range(R // HT):
                rows = slice(t * HT, (t + 1) * HT)
                o_ref[rows, cols] = _read_math(a_ref[rows, cols], b_ref[rows, cols], g_ref[rows, cols],
                                               w_ref[...]).astype(BF16)

    blk = pl.BlockSpec((R, g * LANE), lambda i, h: (i, h))
    return pl.pallas_call(
        body, name=name, grid=(T // R, nh // g), out_shape=jax.ShapeDtypeStruct((T, hgw), BF16),
        in_specs=[blk, blk, pl.BlockSpec((R, g * LANE), lambda i, h: (i, gcol // g + h)),
                  pl.BlockSpec((1, LANE), lambda i, h: (0, 0))],
        out_specs=blk, compiler_params=_cp("parallel", "parallel"),
    )(ofw, obw, u, w)


def hg_read_bwd(ofw, obw, u, w, dout, gcol, ocol, R, name):
    T, hgw = ofw.shape
    nh = hgw // LANE
    g = _head_group(nh, gcol, ocol)

    def body(a_ref, b_ref, g_ref, w_ref, d_ref, do_ref, dg_ref, dw_ref):
        @pl.when(pl.program_id(1) == 0)
        def _():
            dw_ref[...] = jnp.zeros_like(dw_ref)

        for j in range(g):
            cols = slice(j * LANE, (j + 1) * LANE)
            for t in range(R // HT):
                rows = slice(t * HT, (t + 1) * HT)
                _, vjp = jax.vjp(_read_math, a_ref[rows, cols], b_ref[rows, cols], g_ref[rows, cols], w_ref[...])
                da, _, dg, dw = vjp(d_ref[rows, cols])
                do_ref[rows, cols] = da
                dg_ref[rows, cols] = dg
                dw_ref[j] += dw

    blk = pl.BlockSpec((R, g * LANE), lambda h, i: (i, h))
    full = jax.ShapeDtypeStruct((T, hgw), F32)
    return pl.pallas_call(
        body, name=name, grid=(nh // g, T // R), out_shape=(full, full, jax.ShapeDtypeStruct((nh, 1, LANE), F32)),
        in_specs=[blk, blk, pl.BlockSpec((R, g * LANE), lambda h, i: (i, gcol // g + h)),
                  pl.BlockSpec((1, LANE), lambda h, i: (0, 0)),
                  pl.BlockSpec((R, g * LANE), lambda h, i: (i, ocol // g + h))],
        out_specs=(blk, blk, pl.BlockSpec((g, 1, LANE), lambda h, i: (h, 0, 0))),
        compiler_params=_cp("parallel", "arbitrary"),
    )(ofw, obw, u, w, dout)


def _na_step(qw, ow, bias, qraw, kl, vl, kc, vc):
    q = _rms(qraw, qw)
    s_loc = mmf(q, kl, "nt") * ATTN_SCALE + bias
    s_ctx = mmf(q, kc, "nt") * ATTN_SCALE
    m = lax.stop_gradient(jnp.maximum(jnp.max(s_loc, axis=-1, keepdims=True), jnp.max(s_ctx, axis=-1, keepdims=True)))
    p_loc = jnp.exp(s_loc - m)
    p_ctx = jnp.exp(s_ctx - m)
    inv = 1.0 / (jnp.sum(p_loc, axis=-1, keepdims=True) + jnp.sum(p_ctx, axis=-1, keepdims=True))
    return _rms(mmf(p_loc * inv, vl, "nn") + mmf(p_ctx * inv, vc, "nn"), ow)


def _na_geometry(nc, rows):
    ncs = nc // GRID_W
    win_r = min(WIN_R, rows)
    nloc = win_r * GRID_W

    def row_start(s):
        r = jnp.maximum(s - ncs, 0)
        return jnp.clip(r - win_r // 2, 0, rows - win_r)

    def bias_idx(s):
        r = s - ncs
        return jnp.where(s < ncs, win_r, r - jnp.clip(r - win_r // 2, 0, rows - win_r))

    return ncs, win_r, nloc, row_start, bias_idx


def na_bias_tables(rpb, rows):
    win_r = min(WIN_R, rows)
    nh = rpb.shape[0]
    sel_r = np.zeros((win_r, win_r, 2 * WIN_R - 1), np.float32)
    for off in range(win_r):
        for jr in range(win_r):
            sel_r[off, jr, jr - off + WIN_R - 1] = 1.0
    qc = np.arange(GRID_W)[:, None]
    kc = np.arange(GRID_W)[None, :]
    wstart = np.clip(qc - WIN_C // 2, 0, GRID_W - WIN_C)
    ok = (kc >= wstart) & (kc < wstart + WIN_C)
    sel_c = np.zeros((GRID_W, GRID_W, 2 * WIN_C - 1), np.float32)
    sel_c[np.broadcast_to(qc, ok.shape)[ok], np.broadcast_to(kc, ok.shape)[ok], (kc - qc + WIN_C - 1)[ok]] = 1.0
    hi = lax.Precision.HIGHEST
    t = jnp.einsum("hab,oja->hojb", rpb, sel_r, precision=hi)
    t = jnp.einsum("hojb,qkb->hoqjk", t, sel_c, precision=hi)
    t = jnp.where(ok[None, None, :, None, :], t, NEG_INF)
    t = jnp.concatenate([t, jnp.full((nh, 1, GRID_W, win_r, GRID_W), NEG_INF, F32)], axis=1)
    return t.reshape(nh, win_r + 1, GRID_W, win_r * GRID_W)


def kv_prep(u, kw, kcol, vcol, naw, R, name):
    T = u.shape[0]
    g = _head_group(naw // LANE, kcol, vcol)

    def body(k_ref, v_ref, w_ref, kn_ref, vb_ref):
        for j in range(g):
            cols = slice(j * LANE, (j + 1) * LANE)
            kn_ref[:, cols] = _rms(k_ref[:, cols], w_ref[...]).astype(BF16)
        vb_ref[...] = v_ref[...].astype(BF16)

    blk = pl.BlockSpec((R, g * LANE), lambda i, h: (i, h))
    sds = jax.ShapeDtypeStruct((T, naw), BF16)
    return pl.pallas_call(
        body, name=name, grid=(T // R, naw // LANE // g), out_shape=(sds, sds),
        in_specs=[pl.BlockSpec((R, g * LANE), lambda i, h: (i, kcol // g + h)),
                  pl.BlockSpec((R, g * LANE), lambda i, h: (i, vcol // g + h)),
                  pl.BlockSpec((1, LANE), lambda i, h: (0, 0))],
        out_specs=(blk, blk), compiler_params=_cp("parallel", "parallel"),
    )(u, u, kw)


def kv_prep_bwd(u, kw, dkn, kcol, naw, R, name):
    T = u.shape[0]
    nh = naw // LANE
    g = _head_group(nh, kcol)

    def body(k_ref, w_ref, d_ref, dk_ref, dw_ref):
        @pl.when(pl.program_id(1) == 0)
        def _():
            dw_ref[...] = jnp.zeros_like(dw_ref)

        for j in range(g):
            cols = slice(j * LANE, (j + 1) * LANE)
            for t in range(R // HT):
                rows = slice(t * HT, (t + 1) * HT)
                _, vjp = jax.vjp(_rms, k_ref[rows, cols], w_ref[...])
                dk, dw = vjp(d_ref[rows, cols])
                dk_ref[rows, cols] = dk
                dw_ref[j] += dw

    blk = pl.BlockSpec((R, g * LANE), lambda h, i: (i, h))
    return pl.pallas_call(
        body, name=name, grid=(nh // g, T // R),
        out_shape=(jax.ShapeDtypeStruct((T, naw), F32), jax.ShapeDtypeStruct((nh, 1, LANE), F32)),
        in_specs=[pl.BlockSpec((R, g * LANE), lambda h, i: (i, kcol // g + h)),
                  pl.BlockSpec((1, LANE), lambda h, i: (0, 0)), blk],
        out_specs=(blk, pl.BlockSpec((g, 1, LANE), lambda h, i: (h, 0, 0))),
        compiler_params=_cp("parallel", "arbitrary"),
    )(u, kw, dkn)


NA_HB = 4


def _na_operands(j, s, nc, nloc, row_start, q_refs, k_ref, v_ref, qw_ref, ow_ref, b_ref):
    cols = slice(j * LANE, (j + 1) * LANE)
    loc = pl.ds(pl.multiple_of(nc + row_start(s) * GRID_W, GRID_W), nloc)
    ops = (qw_ref[...], ow_ref[:, cols], b_ref[j], q_refs[j][...], k_ref[loc, cols].astype(F32),
           v_ref[loc, cols].astype(F32), k_ref[0:nc, cols].astype(F32), v_ref[0:nc, cols].astype(F32))
    return cols, loc, ops


def _grid_ends(grid):
    ids = lambda: [pl.program_id(d) for d in range(len(grid))]
    first = lambda: functools.reduce(jnp.logical_and, [p == 0 for p in ids()])
    last = lambda: functools.reduce(jnp.logical_and, [p == g - 1 for p, g in zip(ids(), grid)])
    return first, last


def na_fwd(u, kn, vb, qw, ow, bias, qcol, nc, name, comm=None):
    T, naw = kn.shape
    nh, rows = naw // LANE, (T - nc) // GRID_W
    hb = NA_HB if nh % NA_HB == 0 else 1
    ncs, win_r, nloc, row_start, bias_idx = _na_geometry(nc, rows)

    def body(*refs):
        q_refs, (k_ref, v_ref, qw_ref, ow_ref, b_ref, o_ref) = refs[:hb], refs[hb:]
        s = pl.program_id(1)
        for j in range(hb):
            cols, _, ops = _na_operands(j, s, nc, nloc, row_start, q_refs, k_ref, v_ref, qw_ref, ow_ref, b_ref)
            o_ref[:, cols] = _na_step(*ops).astype(BF16)

    wide = pl.BlockSpec((T, hb * LANE), lambda g, s: (0, g), pipeline_mode=pl.Buffered(1))
    grid = (nh // hb, T // GRID_W)
    cin, cspec, cout, csem = _comm_extras(comm)
    out = pl.pallas_call(
        _hosted(body, hb + 5, 1, comm, *_grid_ends(grid)), name=name, grid=grid,
        out_shape=[jax.ShapeDtypeStruct((T, naw), BF16)] + cout,
        in_specs=[pl.BlockSpec((GRID_W, LANE), functools.partial(lambda j, g, s: (s, qcol + g * hb + j), j))
                  for j in range(hb)]
        + [wide, wide, pl.BlockSpec((1, LANE), lambda g, s: (0, 0)), pl.BlockSpec((1, hb * LANE), lambda g, s: (0, g)),
           pl.BlockSpec((hb, None, GRID_W, nloc), lambda g, s: (g, bias_idx(s), 0, 0))] + cspec,
        out_specs=[pl.BlockSpec((GRID_W, hb * LANE), lambda g, s: (s, g))] + cspec, scratch_shapes=csem,
        compiler_params=_cp("arbitrary", "arbitrary"),
    )(*([u] * hb), kn, vb, qw, ow, bias, *cin)
    return out[0] if comm is None else out


def na_bwd(u, kn, vb, qw, ow, bias, dout, qcol, ocol, nc, name, comm=None):
    T, naw = kn.shape
    nh, rows = naw // LANE, (T - nc) // GRID_W
    hb = NA_HB if nh % NA_HB == 0 else 1
    ncs, win_r, nloc, row_start, bias_idx = _na_geometry(nc, rows)
    fresh = [0] + [ncs + r for r in range(rows) if r == 0 or r - np.clip(r - win_r // 2, 0, rows - win_r)
                   != (r - 1) - np.clip(r - 1 - win_r // 2, 0, rows - win_r)]

    def body(*refs):
        q_refs, d_refs = refs[:hb], refs[hb:2 * hb]
        k_ref, v_ref, qw_ref, ow_ref, b_ref, dq_ref, dk_ref, dv_ref, db_ref, dqw_ref, dow_ref = refs[2 * hb:]
        s = pl.program_id(1)

        @pl.when(s == 0)
        def _():
            dk_ref[...] = jnp.zeros_like(dk_ref)
            dv_ref[...] = jnp.zeros_like(dv_ref)
            dqw_ref[...] = jnp.zeros_like(dqw_ref)
            dow_ref[...] = jnp.zeros_like(dow_ref)

        first = functools.reduce(lambda a, b: a | b, [s == f for f in fresh])

        @pl.when(first)
        def _():
            db_ref[...] = jnp.zeros_like(db_ref)

        for j in range(hb):
            cols, loc, ops = _na_operands(j, s, nc, nloc, row_start, q_refs, k_ref, v_ref, qw_ref, ow_ref, b_ref)
            _, vjp = jax.vjp(_na_step, *ops)
            dqw, dow, db, dq, dkl, dvl, dkc, dvc = vjp(d_refs[j][...])
            dq_ref[:, cols] = dq
            dk_ref[loc, cols] += dkl
            dv_ref[loc, cols] += dvl
            dk_ref[0:nc, cols] += dkc
            dv_ref[0:nc, cols] += dvc
            db_ref[j] += db
            dqw_ref[j] += dqw
            dow_ref[j] += dow

    wide = pl.BlockSpec((T, hb * LANE), lambda g, s: (0, g), pipeline_mode=pl.Buffered(1))
    hvec = pl.BlockSpec((hb, 1, LANE), lambda g, s: (g, 0, 0))
    full = jax.ShapeDtypeStruct((T, naw), F32)
    hv = jax.ShapeDtypeStruct((nh, 1, LANE), F32)
    bspec = pl.BlockSpec((hb, None, GRID_W, nloc), lambda g, s: (g, bias_idx(s), 0, 0))
    grid = (nh // hb, T // GRID_W)
    cin, cspec, cout, csem = _comm_extras(comm)
    return pl.pallas_call(
        _hosted(body, 2 * hb + 5, 6, comm, *_grid_ends(grid)), name=name, grid=grid,
        out_shape=[full, full, full, jax.ShapeDtypeStruct(bias.shape, F32), hv, hv] + cout,
        in_specs=[pl.BlockSpec((GRID_W, LANE), functools.partial(lambda j, g, s: (s, qcol + g * hb + j), j))
                  for j in range(hb)]
        + [pl.BlockSpec((GRID_W, LANE), functools.partial(lambda j, g, s: (s, ocol + g * hb + j), j))
           for j in range(hb)]
        + [wide, wide, pl.BlockSpec((1, LANE), lambda g, s: (0, 0)), pl.BlockSpec((1, hb * LANE), lambda g, s: (0, g)),
           bspec] + cspec,
        out_specs=[pl.BlockSpec((GRID_W, hb * LANE), lambda g, s: (s, g)), wide, wide, bspec, hvec, hvec] + cspec,
        scratch_shapes=csem, compiler_params=_cp("arbitrary", "arbitrary"),
    )(*([u] * hb), *([dout] * hb), kn, vb, qw, ow, bias, *cin)


def _halo_specs(R, width, T, col):
    hb = R // 8
    prev = pl.BlockSpec((8, width), lambda j, i: (jnp.maximum(i * hb - 1, 0), col(j, i)))
    nxt = pl.BlockSpec((8, width), lambda j, i: (jnp.minimum((i + 1) * hb, T // 8 - 1), col(j, i)))
    return prev, nxt


def _edge_flags(i, ncb, nblk):
    has_prev = jnp.where((i == 0) | (i == ncb), 0.0, 1.0).astype(F32)
    has_next = jnp.where((i == ncb - 1) | (i == nblk - 1), 0.0, 1.0).astype(F32)
    return has_prev, has_next


def _shift_up(a, prev_row):
    r0 = lax.broadcasted_iota(jnp.int32, a.shape, 0) == 0
    return jnp.where(r0, prev_row, pltpu.roll(a, 1, 0))


def _shift_dn(a, next_row):
    n = a.shape[0]
    rl = lax.broadcasted_iota(jnp.int32, a.shape, 0) == n - 1
    return jnp.where(rl, next_row, pltpu.roll(a, n - 1, 0))


def _conv3(a, prev_row, next_row, w_ref):
    return w_ref[0:1, :] * _shift_up(a, prev_row) + w_ref[1:2, :] * a + w_ref[2:3, :] * _shift_dn(a, next_row)


def _cv_post(b, y, w):
    return _rms(b * y, w)


def short_conv(u, cw, ow, bcol, nc, cvw, R, bwd_dout=None, ocol=0, name=""):
    T = u.shape[0]
    nh, nblk, ncb = cvw // LANE, T // R, nc // R
    bwd = bwd_dout is not None
    g = _head_group(nh, bcol, bcol + nh, bcol + 2 * nh, ocol)
    gw = g * LANE

    def body(b_ref, c_ref, v_ref, cp_ref, vp_ref, cn_ref, vn_ref, cw_ref, ow_ref, *rest):
        i = pl.program_id(1)
        has_prev, has_next = _edge_flags(i, ncb, nblk)
        p = c_ref[...] * v_ref[...]
        y = _conv3(p, cp_ref[7:8, :] * vp_ref[7:8, :] * has_prev, cn_ref[0:1, :] * vn_ref[0:1, :] * has_next, cw_ref)
        if bwd:
            d_ref, db_ref, dy_ref, dow_ref = rest

            @pl.when(i == 0)
            def _():
                dow_ref[...] = jnp.zeros_like(dow_ref)

        for j in range(g):
            cols = slice(j * LANE, (j + 1) * LANE)
            if not bwd:
                rest[0][:, cols] = _cv_post(b_ref[:, cols], y[:, cols], ow_ref[:, cols]).astype(BF16)
            else:
                _, vjp = jax.vjp(_cv_post, b_ref[:, cols], y[:, cols], ow_ref[:, cols])
                db, dy, dow = vjp(d_ref[:, cols])
                db_ref[:, cols] = db
                dy_ref[:, cols] = dy
                dow_ref[j] += dow

    def main(k):
        return pl.BlockSpec((R, gw), lambda h, i: (i, (bcol + k * nh) // g + h))

    cprev, cnext = _halo_specs(R, gw, T, lambda h, i: (bcol + nh) // g + h)
    vprev, vnext = _halo_specs(R, gw, T, lambda h, i: (bcol + 2 * nh) // g + h)
    in_specs = [main(0), main(1), main(2), cprev, vprev, cnext, vnext,
                pl.BlockSpec((3, gw), lambda h, i: (0, h)), pl.BlockSpec((1, gw), lambda h, i: (0, h))]
    ins = [u] * 7 + [cw, ow]
    blk = pl.BlockSpec((R, gw), lambda h, i: (i, h))
    if not bwd:
        out_shape, out_specs = jax.ShapeDtypeStruct((T, cvw), BF16), blk
    else:
        in_specs.append(pl.BlockSpec((R, gw), lambda h, i: (i, ocol // g + h)))
        ins.append(bwd_dout)
        full = jax.ShapeDtypeStruct((T, cvw), F32)
        out_shape = (full, full, jax.ShapeDtypeStruct((nh, 1, LANE), F32))
        out_specs = (blk, blk, pl.BlockSpec((g, 1, LANE), lambda h, i: (h, 0, 0)))
    return pl.pallas_call(body, name=name, grid=(nh // g, nblk), out_shape=out_shape, in_specs=in_specs,
                          out_specs=out_specs, compiler_params=_cp("parallel", "arbitrary"))(*ins)


def conv3_bwd(dy, src, cw, nc, R, W, prod_cols=None, col0=0, out_dtype=F32, name=""):
    T, C = dy.shape
    nblk, ncb = T // R, nc // R
    prod = prod_cols is not None

    def body(*refs):
        if prod:
            (d_ref, dp_ref, dn_ref, c_ref, v_ref, cp_ref, vp_ref, cn_ref, vn_ref, w_ref,
             dc_ref, dv_ref, dw_ref) = refs
        else:
            d_ref, dp_ref, dn_ref, p_ref, pp_ref, pn_ref, w_ref, o_ref, dw_ref = refs
        i = pl.program_id(1)
        has_prev, has_next = _edge_flags(i, ncb, nblk)

        @pl.when(i == 0)
        def _():
            dw_ref[...] = jnp.zeros_like(dw_ref)

        d = d_ref[...]
        d_up = _shift_up(d, dp_ref[7:8, :] * has_prev)
        d_dn = _shift_dn(d, dn_ref[0:1, :] * has_next)
        dp = w_ref[0:1, :] * d_dn + w_ref[1:2, :] * d + w_ref[2:3, :] * d_up
        if prod:
            c, v = c_ref[...], v_ref[...]
            p = c * v
            p_prev, p_next = cp_ref[7:8, :] * vp_ref[7:8, :] * has_prev, cn_ref[0:1, :] * vn_ref[0:1, :] * has_next
            dc_ref[...] = dp * v
            dv_ref[...] = dp * c
        else:
            p = p_ref[...]
            p_prev, p_next = pp_ref[7:8, :] * has_prev, pn_ref[0:1, :] * has_next
            o_ref[...] = dp.astype(out_dtype)
        dw_ref[0:1, :] += jnp.sum(_shift_up(p, p_prev) * d, axis=0, keepdims=True)
        dw_ref[1:2, :] += jnp.sum(p * d, axis=0, keepdims=True)
        dw_ref[2:3, :] += jnp.sum(_shift_dn(p, p_next) * d, axis=0, keepdims=True)

    blk = pl.BlockSpec((R, W), lambda j, i: (i, j))
    dprev, dnext = _halo_specs(R, W, T, lambda j, i: j)
    wspec = pl.BlockSpec((3, W), lambda j, i: (0, j))
    dwspec = pl.BlockSpec((8, W), lambda j, i: (0, j))
    dwshape = jax.ShapeDtypeStruct((8, C), F32)
    if prod:
        ccol, vcol = prod_cols
        cprev, cnext = _halo_specs(R, W, T, lambda j, i: ccol + j)
        vprev, vnext = _halo_specs(R, W, T, lambda j, i: vcol + j)
        in_specs = [blk, dprev, dnext, pl.BlockSpec((R, W), lambda j, i: (i, ccol + j)),
                    pl.BlockSpec((R, W), lambda j, i: (i, vcol + j)), cprev, vprev, cnext, vnext, wspec]
        ins = [dy, dy, dy] + [src] * 6 + [cw]
        full = jax.ShapeDtypeStruct((T, C), F32)
        out_shape, out_specs = (full, full, dwshape), (blk, blk, dwspec)
    else:
        sprev, snext = _halo_specs(R, W, T, lambda j, i: col0 + j)
        in_specs = [blk, dprev, dnext, pl.BlockSpec((R, W), lambda j, i: (i, col0 + j)), sprev, snext,
                    pl.BlockSpec((3, W), lambda j, i: (0, col0 + j))]
        ins = [dy, dy, dy, src, src, src, cw]
        out_shape, out_specs = (jax.ShapeDtypeStruct((T, C), out_dtype), dwshape), (blk, dwspec)
    return pl.pallas_call(body, name=name, grid=(C // W, nblk), out_shape=out_shape, in_specs=in_specs,
                          out_specs=out_specs, compiler_params=_cp("parallel", "arbitrary"))(*ins)


def _pcall(body, *, name, grid, in_specs, out_specs, out_shape, args, scratch=(), sem=None, comm=None):
    cin, cspec, cout, csem = _comm_extras(comm)
    if comm is not None:
        sem = ("arbitrary",) * len(grid)
    return pl.pallas_call(
        _hosted(body, len(in_specs), len(out_specs), comm, *_grid_ends(grid)), name=name, grid=grid,
        out_shape=list(out_shape) + cout, in_specs=list(in_specs) + cspec, out_specs=list(out_specs) + cspec,
        scratch_shapes=list(scratch) + csem, compiler_params=_cp(*sem))(*args, *cin)


def ffn_mid(uf, cw, cb, nc, R, W, name, comm=None):
    T, C = uf.shape
    F = C // 2
    nblk, ncb, nj = T // R, nc // R, F // W

    def body(g_ref, v_ref, gp_ref, vp_ref, gn_ref, vn_ref, wg_ref, wv_ref, bg_ref, bv_ref, a_ref):
        i = pl.program_id(1)
        has_prev, has_next = _edge_flags(i, ncb, nblk)
        yg = _conv3(g_ref[...], gp_ref[7:8, :] * has_prev, gn_ref[0:1, :] * has_next, wg_ref) + bg_ref[...]
        yv = _conv3(v_ref[...], vp_ref[7:8, :] * has_prev, vn_ref[0:1, :] * has_next, wv_ref) + bv_ref[...]
        a_ref[...] = (yg * jax.nn.sigmoid(yg) * yv).astype(BF16)

    gblk = pl.BlockSpec((R, W), lambda j, i: (i, j))
    vblk = pl.BlockSpec((R, W), lambda j, i: (i, nj + j))
    gprev, gnext = _halo_specs(R, W, T, lambda j, i: j)
    vprev, vnext = _halo_specs(R, W, T, lambda j, i: nj + j)
    in_specs = [gblk, vblk, gprev, vprev, gnext, vnext,
                pl.BlockSpec((3, W), lambda j, i: (0, j)), pl.BlockSpec((3, W), lambda j, i: (0, nj + j)),
                pl.BlockSpec((1, W), lambda j, i: (0, j)), pl.BlockSpec((1, W), lambda j, i: (0, nj + j))]
    return _pcall(body, name=name, grid=(nj, nblk), in_specs=in_specs, out_specs=[gblk],
                  out_shape=[jax.ShapeDtypeStruct((T, F), BF16)], args=[uf] * 6 + [cw, cw, cb, cb],
                  sem=("parallel", "arbitrary"), comm=comm)


def ffn_mid_bwd(uf, cw, cb, da, nc, R, W, name, comm=None):
    T, C = uf.shape
    F = C // 2
    nblk, ncb, nj = T // R, nc // R, F // W

    def body(g_ref, v_ref, gp_ref, vp_ref, gn_ref, vn_ref, d_ref, dp_ref, dn_ref, wg_ref, wv_ref, bg_ref, bv_ref,
             dug_ref, duv_ref, dwg_ref, dwv_ref, dbg_ref, dbv_ref):
        i = pl.program_id(1)
        has_prev, has_next = _edge_flags(i, ncb, nblk)

        @pl.when(i == 0)
        def _():
            for r in (dwg_ref, dwv_ref, dbg_ref, dbv_ref):
                r[...] = jnp.zeros_like(r)

        def taps(w_ref):
            return w_ref[0:1, :], w_ref[1:2, :], w_ref[2:3, :]

        def dy_of(yg, yv, d):
            sg = jax.nn.sigmoid(yg)
            return d * yv * (sg * (1.0 + yg * (1.0 - sg))), d * (yg * sg)

        g, v, d = g_ref[...], v_ref[...], d_ref[...]
        (wg0, wg1, wg2), (wv0, wv1, wv2) = taps(wg_ref), taps(wv_ref)
        bg, bv = bg_ref[...], bv_ref[...]
        g_up, g_dn = _shift_up(g, gp_ref[7:8, :] * has_prev), _shift_dn(g, gn_ref[0:1, :] * has_next)
        v_up, v_dn = _shift_up(v, vp_ref[7:8, :] * has_prev), _shift_dn(v, vn_ref[0:1, :] * has_next)
        dyg, dyv = dy_of(wg0 * g_up + wg1 * g + wg2 * g_dn + bg, wv0 * v_up + wv1 * v + wv2 * v_dn + bv, d)
        dyg_p, dyv_p = dy_of(wg0 * gp_ref[6:7, :] + wg1 * gp_ref[7:8, :] + wg2 * g_ref[0:1, :] + bg,
                             wv0 * vp_ref[6:7, :] + wv1 * vp_ref[7:8, :] + wv2 * v_ref[0:1, :] + bv, dp_ref[7:8, :])
        dyg_n, dyv_n = dy_of(wg0 * g_ref[R - 1:R, :] + wg1 * gn_ref[0:1, :] + wg2 * gn_ref[1:2, :] + bg,
                             wv0 * v_ref[R - 1:R, :] + wv1 * vn_ref[0:1, :] + wv2 * vn_ref[1:2, :] + bv, dn_ref[0:1, :])
        dug_ref[...] = (wg0 * _shift_dn(dyg, dyg_n * has_next) + wg1 * dyg
                        + wg2 * _shift_up(dyg, dyg_p * has_prev)).astype(BF16)
        duv_ref[...] = (wv0 * _shift_dn(dyv, dyv_n * has_next) + wv1 * dyv
                        + wv2 * _shift_up(dyv, dyv_p * has_prev)).astype(BF16)
        for ref, ups, mid, dns, dy in ((dwg_ref, g_up, g, g_dn, dyg), (dwv_ref, v_up, v, v_dn, dyv)):
            ref[0:1, :] += jnp.sum(ups * dy, axis=0, keepdims=True)
            ref[1:2, :] += jnp.sum(mid * dy, axis=0, keepdims=True)
            ref[2:3, :] += jnp.sum(dns * dy, axis=0, keepdims=True)
        dbg_ref[...] += jnp.sum(dyg, axis=0, keepdims=True)
        dbv_ref[...] += jnp.sum(dyv, axis=0, keepdims=True)

    gblk = pl.BlockSpec((R, W), lambda j, i: (i, j))
    vblk = pl.BlockSpec((R, W), lambda j, i: (i, nj + j))
    gprev, gnext = _halo_specs(R, W, T, lambda j, i: j)
    vprev, vnext = _halo_specs(R, W, T, lambda j, i: nj + j)
    half = jax.ShapeDtypeStruct((T, F), BF16)
    taps8, bias1 = jax.ShapeDtypeStruct((8, F), F32), jax.ShapeDtypeStruct((1, F), F32)
    return _pcall(
        body, name=name, grid=(nj, nblk), out_shape=(half, half, taps8, taps8, bias1, bias1),
        in_specs=[gblk, vblk, gprev, vprev, gnext, vnext, gblk, gprev, gnext,
                  pl.BlockSpec((3, W), lambda j, i: (0, j)), pl.BlockSpec((3, W), lambda j, i: (0, nj + j)),
                  pl.BlockSpec((1, W), lambda j, i: (0, j)), pl.BlockSpec((1, W), lambda j, i: (0, nj + j))],
        out_specs=(gblk, gblk, pl.BlockSpec((8, W), lambda j, i: (0, j)), pl.BlockSpec((8, W), lambda j, i: (0, j)),
                   pl.BlockSpec((1, W), lambda j, i: (0, j)), pl.BlockSpec((1, W), lambda j, i: (0, j))),
        args=(uf, uf, uf, uf, uf, uf, da, da, da, cw, cw, cb, cb), sem=("parallel", "arbitrary"), comm=comm)


WEIGHTS = ("c_ctx", "w_ada", "b_ada", "ln1_w", "ln2_w", "w_in", "hg_lb_logits", "hg_norm_w", "na_q_norm_w",
           "na_k_norm_w", "na_rpb", "na_out_norm_w", "cv_w", "cv_out_norm_w", "w_out", "w_up", "ffn_conv_w",
           "ffn_conv_b", "w_down")
BIG = ("w_ada", "w_in", "w_out", "w_up", "w_down")
SHARDED_SMALL = ("hg_lb_logits", "cv_w", "ffn_conv_w")


def _flat_rows(parts, dtype):
    flat, layout, off = [], [], 0
    for p in parts:
        layout.append((off, p.shape))
        flat.append(p.reshape(-1).astype(dtype))
        off += p.size
    pad = (-off) % (8 * LANE)
    if pad:
        flat.append(jnp.zeros((pad,), dtype))
    return jnp.concatenate(flat).reshape(-1, LANE), layout


def _unflat(buf, layout):
    v = buf.reshape(-1)
    return [v[off:off + int(np.prod(shape))].reshape(shape) for off, shape in layout]


def _lb_all(logits):
    sm = jax.nn.softmax(logits.astype(F32), axis=1)
    return jnp.cumsum(sm, axis=1) - sm[:, :1]


def _seg(ctx_vec, lat_vec):
    return jnp.broadcast_to(jnp.stack([ctx_vec, lat_vec])[:, None, :], (2, RT, ctx_vec.shape[0]))


def _shared(vec):
    return jnp.broadcast_to(vec[None, None, :], (1, RT, vec.shape[0]))


def kernel(x, c, ctx, c_ctx, w_ada, b_ada, ln1_w, ln2_w, w_in, hg_lb_logits, hg_norm_w, na_q_norm_w, na_k_norm_w, na_rpb, na_out_norm_w, cv_w, cv_out_norm_w, w_out, w_up, ffn_conv_w, ffn_conv_b, w_down, loss_target, m_c_ctx, m_w_ada, m_b_ada, m_ln1_w, m_ln2_w, m_w_in, m_hg_lb_logits, m_hg_norm_w, m_na_q_norm_w, m_na_k_norm_w, m_na_rpb, m_na_out_norm_w, m_cv_w, m_cv_out_norm_w, m_w_out, m_w_up, m_ffn_conv_w, m_ffn_conv_b, m_w_down, v_c_ctx, v_w_ada, v_b_ada, v_ln1_w, v_ln2_w, v_w_in, v_hg_lb_logits, v_hg_norm_w, v_na_q_norm_w, v_na_k_norm_w, v_na_rpb, v_na_out_norm_w, v_cv_w, v_cv_out_norm_w, v_w_out, v_w_up, v_ffn_conv_w, v_ffn_conv_b, v_w_down):
    W = dict(c_ctx=c_ctx, w_ada=w_ada, b_ada=b_ada, ln1_w=ln1_w, ln2_w=ln2_w, w_in=w_in, hg_lb_logits=hg_lb_logits,
             hg_norm_w=hg_norm_w, na_q_norm_w=na_q_norm_w, na_k_norm_w=na_k_norm_w, na_rpb=na_rpb,
             na_out_norm_w=na_out_norm_w, cv_w=cv_w, cv_out_norm_w=cv_out_norm_w, w_out=w_out, w_up=w_up,
             ffn_conv_w=ffn_conv_w, ffn_conv_b=ffn_conv_b, w_down=w_down)
    Mo = dict(c_ctx=m_c_ctx, w_ada=m_w_ada, b_ada=m_b_ada, ln1_w=m_ln1_w, ln2_w=m_ln2_w, w_in=m_w_in,
              hg_lb_logits=m_hg_lb_logits, hg_norm_w=m_hg_norm_w, na_q_norm_w=m_na_q_norm_w,
              na_k_norm_w=m_na_k_norm_w, na_rpb=m_na_rpb, na_out_norm_w=m_na_out_norm_w, cv_w=m_cv_w,
              cv_out_norm_w=m_cv_out_norm_w, w_out=m_w_out, w_up=m_w_up, ffn_conv_w=m_ffn_conv_w,
              ffn_conv_b=m_ffn_conv_b, w_down=m_w_down)
    Vo = dict(c_ctx=v_c_ctx, w_ada=v_w_ada, b_ada=v_b_ada, ln1_w=v_ln1_w, ln2_w=v_ln2_w, w_in=v_w_in,
              hg_lb_logits=v_hg_lb_logits, hg_norm_w=v_hg_norm_w, na_q_norm_w=v_na_q_norm_w,
              na_k_norm_w=v_na_k_norm_w, na_rpb=v_na_rpb, na_out_norm_w=v_na_out_norm_w, cv_w=v_cv_w,
              cv_out_norm_w=v_cv_out_norm_w, w_out=v_w_out, w_up=v_w_up, ffn_conv_w=v_ffn_conv_w,
              ffn_conv_b=v_ffn_conv_b, w_down=v_w_down)

    xi, yi, ci = _me()
    chip = 2 * xi + yi
    dev = 2 * chip + ci
    L, D = x.shape[1], x.shape[2]
    NC = ctx.shape[1]
    T = NC + L
    depth = w_in.shape[0]
    HGW, NAW, CVW = 4 * hg_lb_logits.shape[-1], na_out_norm_w.shape[-1], cv_out_norm_w.shape[-1]
    MIX = HGW + NAW + CVW
    INW, FF2 = 4 * w_in.shape[-1], 4 * w_up.shape[-1]
    F = FF2 // 2
    ADA = 4 * w_ada.shape[-1]
    assert NAW == 2 * HGW and INW == 5 * HGW + 3 * NAW + 3 * CVW and ADA == 6 * D and NC % 128 == 0
    assert L % GRID_W == 0 and T % CHUNK == 0 and depth == 2
    R = math.gcd(NC, 256)
    FW = 512 if F % 512 == 0 else LANE
    rows = L // GRID_W
    nh_hg, nh_na, nh_cv = HGW // LANE, NAW // LANE, CVW // LANE
    kcol = 3 * nh_hg
    vcol = kcol + nh_na
    gcol = vcol + nh_na + nh_hg
    qcol = gcol + nh_hg
    bcol = qcol + nh_na
    mix_na, mix_cv = nh_hg, nh_hg + nh_na

    small1, lay1 = _flat_rows([c[0], hg_lb_logits, cv_w, ffn_conv_w], F32)
    g1 = allgather8([small1], "gather_cond")[0]
    per_dev = [_unflat(g1[d], lay1) for d in range(8)]
    c_all = jnp.stack([p[0] for p in per_dev])
    lb_logits = jnp.concatenate([per_dev[2 * s][1] for s in range(4)], axis=-1)
    cvw_full = jnp.concatenate([per_dev[2 * s][2] for s in range(4)], axis=-1)
    fcw_full = jnp.concatenate([per_dev[2 * s][3] for s in range(4)], axis=-1)
    lb_all, lb_pull = jax.vjp(_lb_all, lb_logits)

    a16 = jnp.concatenate([c_all, c_ctx[None], jnp.zeros((7, D), F32)])
    s16 = _silu(a16)
    wcols = ADA // 4
    b_mine = lax.dynamic_slice_in_dim(b_ada, chip * wcols, wcols, axis=1)
    p_ada = mm_nn(s16, w_ada, F32, "ada_fwd").reshape(16, depth, wcols).transpose(1, 0, 2) + b_mine[:, None, :]
    g2 = allgather8([p_ada.reshape(depth * 16, wcols)], "gather_ada")[0].reshape(8, depth, 16, wcols)
    ada_rows = jnp.concatenate([g2[2 * s] for s in range(4)], axis=-1)
    ada = lax.dynamic_index_in_dim(ada_rows, dev, axis=1, keepdims=False)
    ada_c = ada_rows[:, 8]

    def half_rows(a):
        h = a.shape[0] // 2
        return lax.dynamic_slice_in_dim(a, ci * h, h, axis=0)

    proj = ("w_in", "w_out", "w_up", "w_down")
    wparts = [{n: half_rows(W[n][l]).astype(BF16) for n in proj} for l in range(depth)]

    def stacked(n, g):
        g = g.reshape(4, -1, g.shape[-1])
        return g.reshape(1, -1, g.shape[-1]) if n in ("w_out", "w_down") else g

    Wg = [{}, {}]
    fetch_plan = {"norm_in": [(0, "w_in")], "proj_in_0": [(0, "w_out"), (0, "w_down")], "na_fwd_0": [(0, "w_up")],
                  "ffn_up_0": [(1, "w_up")], "ffn_mid_0": [(1, "w_in"), (1, "w_out")], "ffn_down_0": [(1, "w_down")]}

    def fetching(name, call):
        items = fetch_plan.get(name, [])
        res = call(allgather8_comm([wparts[l][n] for l, n in items]) if items else None)
        res = list(res) if isinstance(res, (list, tuple)) else [res]
        for (l, n), g in zip(items, res[len(res) - len(items):]):
            Wg[l][n] = stacked(n, g)
        own = res[:len(res) - len(items)]
        return own[0] if len(own) == 1 else own

    xcat = jnp.concatenate([ctx[0], x[0]], axis=0)
    mods = []
    for l in range(depth):
        lat, con = jnp.split(ada[l], 6), jnp.split(ada_c[l], 6)
        mods.append(dict(sh1=_seg(con[0], lat[0]), sc1=_seg(con[1], lat[1]), g1=_seg(con[2], lat[2]),
                         sh2=_seg(con[3], lat[3]), sc2=_seg(con[4], lat[4]), g2=_seg(con[5], lat[5]),
                         ln1=_shared(ln1_w[l]), ln2=_shared(ln2_w[l])))
    bias_pull, saved = [], []
    x0 = xcat
    h = fetching("norm_in", lambda cm: gate_norm(x0, None, None, mods[0]["ln1"], mods[0]["sh1"], mods[0]["sc1"], NC, R,
                                                 "norm_in", comm=cm)[1:])
    for l in range(depth):
        md, wl = mods[l], Wg[l]
        u = fetching(f"proj_in_{l}", lambda cm: mm_nn(h, wl["w_in"], F32, f"proj_in_{l}", comm=cm))
        lbf, lbb = lb_all[0, l][None], lb_all[1, l][None]
        o_fw, st_fw = hgrn_fwd(u, lbf, False, 0, NC, HGW, f"hgrn_fw_{l}")
        o_bw, st_bw = hgrn_fwd(u, lbb, True, 1, NC, HGW, f"hgrn_bw_{l}")
        hgn = hg_norm_w[l][None]
        hg = hg_read(o_fw, o_bw, u, hgn, gcol, R, f"hg_read_{l}")
        bias, pull = jax.vjp(lambda r: na_bias_tables(r, rows), na_rpb[l])
        bias_pull.append(pull)
        qn, kn, on = na_q_norm_w[l][None], na_k_norm_w[l][None], na_out_norm_w[l][None]
        keys_n, vals_b = kv_prep(u, kn, kcol, vcol, NAW, R, f"kv_prep_{l}")
        na = fetching(f"na_fwd_{l}", lambda cm: na_fwd(u, keys_n, vals_b, qn, on, bias, qcol, NC, f"na_fwd_{l}", comm=cm))
        cvw_l, cvo = cvw_full[l], cv_out_norm_w[l][None]
        cv = short_conv(u, cvw_l, cvo, bcol, NC, CVW, R, name=f"short_conv_{l}")
        mix = jnp.concatenate([hg, na, cv], axis=1)
        m1 = mm_nn(mix, wl["w_out"], F32, f"proj_out_{l}")
        x1, h2 = gate_norm(x0, m1, md["g1"], md["ln2"], md["sh2"], md["sc2"], NC, R, f"gate_norm_mid_{l}")
        uf = fetching(f"ffn_up_{l}", lambda cm: mm_nn(h2, wl["w_up"], F32, f"ffn_up_{l}", comm=cm))
        fcw_l, fcb_l = fcw_full[l], ffn_conv_b[l][None]
        a = fetching(f"ffn_mid_{l}", lambda cm: ffn_mid(uf, fcw_l, fcb_l, NC, R, FW, f"ffn_mid_{l}", comm=cm))
        m2 = fetching(f"ffn_down_{l}", lambda cm: mm_nn(a, wl["w_down"], F32, f"ffn_down_{l}", comm=cm))
        saved.append(dict(x0=x0, h=h, u=u, o_fw=o_fw, o_bw=o_bw, st_fw=st_fw, st_bw=st_bw, bias=bias, mix=mix,
                          m1=m1, x1=x1, h2=h2, uf=uf, a=a, m2=m2, lbf=lbf, lbb=lbb, keys_n=keys_n, vals_b=vals_b))
        if l + 1 < depth:
            nx = mods[l + 1]
            x0, h = gate_norm(x1, m2, md["g2"], nx["ln1"], nx["sh1"], nx["sc1"], NC, R, f"gate_norm_end_{l}")
    sv, md = saved[-1], mods[-1]
    loss_terms, d_x1, d_m2, d_g2 = gate_loss(sv["x1"], sv["m2"], md["g2"], loss_target[0], NC, R, "gate_loss")
    loss = lax.psum(jnp.sum(loss_terms), ("x", "y", "c"))

    big_grads = [dict() for _ in range(depth)]
    core = ci.astype(jnp.int32).reshape(1)
    pairs, quads = {}, {}
    reduce_plan = {"ffn_down_bwd_0": [(1, "w_down")], "ffn_mid_bwd_0": [(1, "w_up")],
                   "ffn_up_bwd_0": [(1, "w_in"), (1, "w_out")], "na_bwd_0": [(0, "w_down"), (0, "w_up"), (0, "w_out")],
                   "proj_in_bwd_0": [(0, "w_in")]}

    def swapping(l, names, call):
        parts = [big_grads[l][n] for n in names]
        res = list(call(swap_halves_comm(parts)))
        for n, p, g in zip(names, parts, res[len(res) - len(names):]):
            pairs[(l, n)] = pair_sum(p, g, core, f"reduce_pair_sum_{n}_{l}")
        own = res[:len(res) - len(names)]
        return own[0] if len(own) == 1 else own

    def stage(l, names):
        swapping(l, names, lambda cm: _run_comm(cm, f"reduce_sibling_{l}_{names[0]}"))

    def reducing(name, call):
        items = reduce_plan.get(name, [])
        res = call(chip_alltoall_comm([pairs[k] for k in items]) if items else None)
        res = list(res) if isinstance(res, (list, tuple)) else [res]
        for k, q in zip(items, res[len(res) - len(items):]):
            quads[k] = q
        own = res[:len(res) - len(items)]
        return own[0] if len(own) == 1 else own

    small = [dict() for _ in range(depth)]
    d_ada = [None] * depth
    d_lb = [None] * depth
    for l in reversed(range(depth)):
        sv, md, wl = saved[l], mods[l], Wg[l]
        u, uf = sv["u"], sv["uf"]
        big_grads[l]["w_down"] = mm_tn(sv["a"], d_m2, 1, BF16, f"grad_w_down_{l}").reshape(4, F // 4, D)
        d_a = reducing(f"ffn_down_bwd_{l}", lambda cm: mm_nt(d_m2, wl["w_down"], F32, f"ffn_down_bwd_{l}", comm=cm))
        fcw_l, fcb_l = fcw_full[l], ffn_conv_b[l][None]
        dug, duv, dwg, dwv, dbg, dbv = reducing(
            f"ffn_mid_bwd_{l}", lambda cm: ffn_mid_bwd(uf, fcw_l, fcb_l, d_a, NC, R, FW, f"ffn_mid_bwd_{l}", comm=cm))
        d_uf = jnp.concatenate([dug, duv], axis=1)
        small[l]["ffn_conv_w"] = jnp.concatenate([dwg[:3], dwv[:3]], axis=1)
        small[l]["ffn_conv_b"] = jnp.concatenate([dbg[0], dbv[0]])
        big_grads[l]["w_up"] = mm_tn(sv["h2"], d_uf, 4, BF16, f"grad_w_up_{l}")
        d_h2 = reducing(f"ffn_up_bwd_{l}", lambda cm: mm_nt(d_uf, wl["w_up"], F32, f"ffn_up_bwd_{l}", comm=cm))
        d_x0, d_m1, dg1, dln2, dsh2, dsc2 = gate_norm_bwd(sv["x0"], sv["m1"], md["g1"], md["ln2"], md["sh2"], md["sc2"],
                                                          d_x1, d_h2, NC, R, f"gate_norm_mid_bwd_{l}")
        big_grads[l]["w_out"] = mm_tn(sv["mix"], d_m1, 1, BF16, f"grad_w_out_{l}").reshape(4, MIX // 4, D)
        if l == 0:
            d_mix = swapping(0, ("w_down", "w_up", "w_out"),
                             lambda cm: mm_nt(d_m1, wl["w_out"], F32, f"proj_out_bwd_{l}", comm=cm))
        else:
            d_mix = mm_nt(d_m1, wl["w_out"], F32, f"proj_out_bwd_{l}")
        hgn = hg_norm_w[l][None]
        d_o, d_hgg, d_hgn = hg_read_bwd(sv["o_fw"], sv["o_bw"], u, hgn, d_mix, gcol, 0, R, f"hg_read_bwd_{l}")
        dzf, dvf, dqf, dlbf = hgrn_bwd(u, sv["lbf"], sv["st_fw"], d_o, False, 0, NC, HGW, f"hgrn_fw_bwd_{l}")
        dzb, dvb, dqb, dlbb = hgrn_bwd(u, sv["lbb"], sv["st_bw"], d_o, True, 1, NC, HGW, f"hgrn_bw_bwd_{l}")
        d_lb[l] = (dlbf[0], dlbb[0])
        qn, kn, on = na_q_norm_w[l][None], na_k_norm_w[l][None], na_out_norm_w[l][None]
        d_nq, d_keys_n, d_nv, d_bias, d_qn, d_on = reducing(
            f"na_bwd_{l}", lambda cm: na_bwd(u, sv["keys_n"], sv["vals_b"], qn, on, sv["bias"], d_mix, qcol, mix_na, NC,
                                            f"na_bwd_{l}", comm=cm))
        d_nk, d_kn = kv_prep_bwd(u, kn, d_keys_n, kcol, NAW, R, f"kv_prep_bwd_{l}")
        cvw_l, cvo = cvw_full[l], cv_out_norm_w[l][None]
        d_cb, d_cy, d_cvo = short_conv(u, cvw_l, cvo, bcol, NC, CVW, R, bwd_dout=d_mix, ocol=mix_cv,
                                       name=f"short_conv_bwd_{l}")
        gcv = _head_group(nh_cv, bcol + nh_cv, bcol + 2 * nh_cv)
        d_cc, d_cvv, d_cvw = conv3_bwd(d_cy, u, cvw_l, NC, R, gcv * LANE,
                                       prod_cols=((bcol + nh_cv) // gcv, (bcol + 2 * nh_cv) // gcv),
                                       name=f"short_conv_taps_bwd_{l}")
        d_u = jnp.concatenate([dzf, dzb, dvf + dvb, d_nk, d_nv, dqf + dqb, d_hgg, d_nq, d_cb, d_cc, d_cvv],
                              axis=1).astype(BF16)
        big_grads[l]["w_in"] = mm_tn(sv["h"], d_u, 4, BF16, f"grad_w_in_{l}")
        if l > 0:
            d_h = swapping(l, proj, lambda cm: mm_nt(d_u, wl["w_in"], F32, f"proj_in_bwd_{l}", comm=cm))
        else:
            stage(0, ("w_in",))
            d_h = reducing(f"proj_in_bwd_{l}", lambda cm: mm_nt(d_u, wl["w_in"], F32, f"proj_in_bwd_{l}", comm=cm))
        small[l].update(hg_norm_w=d_hgn.sum(0)[0], na_q_norm_w=d_qn.sum(0)[0], na_k_norm_w=d_kn.sum(0)[0],
                        na_out_norm_w=d_on.reshape(-1), na_rpb=bias_pull[l](d_bias)[0], cv_w=d_cvw[:3],
                        cv_out_norm_w=d_cvo.reshape(-1), ln2_w=dln2.sum((0, 1)))
        if l > 0:
            pv, pm = saved[l - 1], mods[l - 1]
            d_x1, d_m2, dg2_prev, dln1, dsh1, dsc1 = gate_norm_bwd(pv["x1"], pv["m2"], pm["g2"], md["ln1"], md["sh1"],
                                                                   md["sc1"], d_x0, d_h, NC, R, f"gate_norm_end_bwd_{l - 1}")
        else:
            d_xin, _, _, dln1, dsh1, dsc1 = gate_norm_bwd(sv["x0"], None, None, md["ln1"], md["sh1"], md["sc1"], d_x0, d_h,
                                                          NC, R, "norm_in_bwd")
        small[l]["ln1_w"] = dln1.sum((0, 1))
        this_g2 = d_g2
        vecs = [v.sum(1) for v in (dsh1, dsc1, dg1, dsh2, dsc2, this_g2)]
        d_ada[l] = jnp.stack([jnp.concatenate([v[s] for v in vecs]) for s in (0, 1)])
        if l > 0:
            d_g2 = dg2_prev
    grad_x = d_xin[NC:][None]
    d_logits = lb_pull(jnp.stack([jnp.stack([d_lb[l][k] for l in range(depth)]) for k in (0, 1)]))[0]

    rep_names = ("ln1_w", "ln2_w", "hg_norm_w", "na_q_norm_w", "na_k_norm_w", "na_rpb", "na_out_norm_w",
                 "cv_out_norm_w", "ffn_conv_b", "cv_w", "ffn_conv_w")
    parts3 = [jnp.stack([small[l][n] for l in range(depth)]) for n in rep_names]
    parts3 += [d_logits, jnp.stack([d_ada[l][0] for l in range(depth)]), jnp.stack([d_ada[l][1] for l in range(depth)])]
    buf3, lay3 = _flat_rows(parts3, F32)
    g3 = allgather8([buf3], "gather_small_grads")[0]
    tot3 = _unflat(sum_leading(g3, F32, "sum_small_grads"), lay3)
    gsm = dict(zip(rep_names, tot3[:len(rep_names)]))
    gsm["hg_lb_logits"] = tot3[len(rep_names)]
    dctx_tot, dlat_tot = tot3[-2], tot3[-1]
    dlat_each = jnp.stack([_unflat(g3[d], lay3)[-1] for d in range(8)], axis=1)
    grads = {n: gsm[n].reshape(W[n].shape) for n in rep_names if n not in SHARDED_SMALL}
    for n in SHARDED_SMALL:
        wl_ = W[n].shape[-1]
        grads[n] = lax.dynamic_slice_in_dim(gsm[n], chip * wl_, wl_, axis=gsm[n].ndim - 1)
    grads["b_ada"] = dctx_tot + dlat_tot

    gw_ada, dms = [], []
    for l in range(depth):
        dm = jnp.concatenate([dlat_each[l], dctx_tot[l][None], jnp.zeros((7, ADA), F32)])
        dms.append(lax.dynamic_slice_in_dim(dm, chip * wcols, wcols, axis=1))
        gw_ada.append(mm_tn(s16, dms[l], 1, F32, f"grad_w_ada_{l}")[0])
    ds16 = mm_nt(jnp.concatenate(dms, axis=1), w_ada, F32, "ada_bwd")
    g4 = allgather8([ds16[8:16]], "gather_cond_grad")[0]
    d_scc = g4[0, 0] + g4[2, 0] + g4[4, 0] + g4[6, 0]
    sg = jax.nn.sigmoid(c_ctx)
    grads["c_ctx"] = d_scc * (sg * (1.0 + c_ctx * (1.0 - sg)))

    keys = [(l, n) for l in range(depth) for n in proj]
    mine = [sum_leading(quads[(l, n)], F32, f"reduce_chip_sum_{n}_{l}") for l, n in keys]

    delta, new_m, new_v = {}, {}, {}
    smalls = [n for n in WEIGHTS if n not in BIG]
    pw, lay_s = _flat_rows([W[n] for n in smalls], F32)
    pg, _ = _flat_rows([grads[n] for n in smalls], F32)
    pm, _ = _flat_rows([Mo[n] for n in smalls], F32)
    pvv, _ = _flat_rows([Vo[n] for n in smalls], F32)
    _, d_, m_, v_, *other = adamw(pw, [pg], pm, pvv, "adamw_small", comm=share_halves_comm(mine))
    for n, dd, mm_, vv in zip(smalls, _unflat(d_, lay_s), _unflat(m_, lay_s), _unflat(v_, lay_s)):
        delta[n], new_m[n], new_v[n] = dd, mm_, vv
    mine_by, other_by = {k: [None] * depth for k in proj}, {k: [None] * depth for k in proj}
    for (l, k), a, b in zip(keys, mine, other):
        mine_by[k][l], other_by[k][l] = a, b
    for n in BIG:
        shp = W[n].shape
        two = lambda a: a.reshape(-1, shp[-1])
        if n == "w_ada":
            g_, d_, m_, v_ = adamw(two(W[n]), gw_ada, two(Mo[n]), two(Vo[n]), f"adamw_{n}")
        else:
            g_, d_, m_, v_ = adamw_halves(two(W[n]), mine_by[n], other_by[n], two(Mo[n]), two(Vo[n]), core, f"adamw_{n}")
        grads[n], delta[n], new_m[n], new_v[n] = g_.reshape(shp), d_.reshape(shp), m_.reshape(shp), v_.reshape(shp)

    return (loss, grad_x, *[grads[n] for n in WEIGHTS], *[delta[n] for n in WEIGHTS],
            *[new_m[n] for n in WEIGHTS], *[new_v[n] for n in WEIGHTS])
```

```python
import functools
import math

import numpy as np
import jax
import jax.numpy as jnp
from jax import lax
from jax.experimental import pallas as pl
from jax.experimental.pallas import tpu as pltpu

F32 = jnp.float32
BF16 = jnp.bfloat16
MESH = pl.DeviceIdType.MESH
ANY = pl.BlockSpec(memory_space=pl.ANY)
VMEM_SPEC = pl.BlockSpec(memory_space=pltpu.VMEM)

LANE = 128
CHUNK = 64
SUB = 16
GRID_W = 64
WIN_R = 8
WIN_C = 16
EPS = 1e-6
F_FLOOR = 1e-30
NEG_INF = -1e30
EXP_CLAMP = 80.0
ATTN_SCALE = LANE ** -0.5
VMEM_LIMIT = 56 * 1024 * 1024
ADAM_LR, ADAM_B1, ADAM_B2, ADAM_EPS, ADAM_WD, ADAM_STEP = 0.001, 0.9, 0.999, 1e-08, 0.01, 10


def _cp(*sem):
    return pltpu.CompilerParams(dimension_semantics=sem or None, vmem_limit_bytes=VMEM_LIMIT)


def _me():
    return lax.axis_index("x"), lax.axis_index("y"), lax.axis_index("c")


def allgather8(blocks, name, hbm=False):
    na = len(blocks)
    comm = allgather8_comm(blocks)

    def body(*refs):
        comm["start"](refs[:na], refs[na:2 * na], refs[2 * na:])
        comm["finish"](refs[:na], refs[na:2 * na], refs[2 * na:])

    spec = ANY if hbm else VMEM_SPEC
    return pl.pallas_call(
        body, name=name, out_shape=comm["outs"], in_specs=[spec] * na, out_specs=[spec] * na,
        scratch_shapes=comm["scratch"], compiler_params=pltpu.CompilerParams(vmem_limit_bytes=VMEM_LIMIT),
    )(*blocks)


def allgather8_comm(blocks):
    na = len(blocks)

    def parts(x_refs, out_refs, sems):
        send_sems, recv_sems, local_sems = sems
        x, y, c = _me()
        me, sibling = (x, y, c), (x, y, 1 - c)
        chips = [(1 - x, y), (x, 1 - y), (1 - x, 1 - y)]

        def rows(a, px, py, pc):
            return out_refs[a].at[4 * px + 2 * py + pc]

        def copy(a, k, blk, to, src=None):
            return pltpu.make_async_remote_copy(
                src_ref=rows(a, *blk) if src is None else src, dst_ref=rows(a, *blk),
                send_sem=send_sems.at[a, k], recv_sem=recv_sems.at[a, k], device_id=to, device_id_type=MESH)

        mine = [pltpu.make_async_copy(x_refs[a], rows(a, *me), local_sems.at[a]) for a in range(na)]
        first = []
        for a in range(na):
            first.append(copy(a, 0, me, sibling, src=x_refs[a]))
            first += [copy(a, 1 + j, me, (*chip, c), src=x_refs[a]) for j, chip in enumerate(chips)]
        return c, me, sibling, chips, copy, mine, first

    def start(x_refs, out_refs, sems):
        _, _, _, _, _, mine, first = parts(x_refs, out_refs, sems)
        for cp in mine + first:
            cp.start()

    def finish(x_refs, out_refs, sems):
        c, me, sibling, chips, copy, mine, first = parts(x_refs, out_refs, sems)
        passed = []
        for j, chip in enumerate(chips):
            for a in range(na):
                copy(a, 1 + j, (*chip, c), me).wait_recv()
                passed.append(copy(a, 4 + j, (*chip, c), sibling))
                passed[-1].start()
        for a in range(na):
            copy(a, 0, sibling, me).wait_recv()
            for j, chip in enumerate(chips):
                copy(a, 4 + j, (*chip, 1 - c), me).wait_recv()
        for cp in first + passed:
            cp.wait_send()
        for cp in mine:
            cp.wait()

    return dict(ins=list(blocks), outs=[jax.ShapeDtypeStruct((8,) + b.shape, b.dtype) for b in blocks],
                scratch=[pltpu.SemaphoreType.DMA((na, 7)), pltpu.SemaphoreType.DMA((na, 7)),
                         pltpu.SemaphoreType.DMA((na,))], start=start, finish=finish)


def _run_comm(comm, name):
    na = len(comm["ins"])

    def body(*refs):
        comm["start"](refs[:na], refs[na:2 * na], refs[2 * na:])
        comm["finish"](refs[:na], refs[na:2 * na], refs[2 * na:])

    return pl.pallas_call(body, name=name, out_shape=comm["outs"], in_specs=[ANY] * na, out_specs=[ANY] * na,
                          scratch_shapes=comm["scratch"])(*comm["ins"])


def swap_halves_comm(gs):
    na = len(gs)
    hrs = [g.shape[1] // 2 for g in gs]

    def copies(g_refs, o_refs, sems):
        send_sems, recv_sems = sems
        x, y, c = _me()
        cps = []
        for a in range(na):
            for s in range(4):
                src = g_refs[a].at[s, pl.ds(pl.multiple_of((1 - c) * hrs[a], 16), hrs[a]), :]
                cps.append(pltpu.make_async_remote_copy(
                    src_ref=src, dst_ref=o_refs[a].at[s], send_sem=send_sems.at[a, s], recv_sem=recv_sems.at[a, s],
                    device_id=(x, y, 1 - c), device_id_type=MESH))
        return cps

    def start(g_refs, o_refs, sems):
        for cp in copies(g_refs, o_refs, sems):
            cp.start()

    def finish(g_refs, o_refs, sems):
        for cp in copies(g_refs, o_refs, sems):
            cp.wait()

    return dict(ins=list(gs), outs=[jax.ShapeDtypeStruct((4, hrs[a], gs[a].shape[2]), gs[a].dtype) for a in range(na)],
                scratch=[pltpu.SemaphoreType.DMA((na, 4)), pltpu.SemaphoreType.DMA((na, 4))], start=start, finish=finish)


def chip_alltoall_comm(gs):
    na = len(gs)

    def copies(g_refs, o_refs, sems):
        send_sems, recv_sems, local_sems = sems
        x, y, c = _me()
        mine = 2 * x + y
        cps = []
        for a in range(na):
            cps.append(pltpu.make_async_copy(g_refs[a].at[mine], o_refs[a].at[mine], local_sems.at[a]))
            for k, (px, py) in enumerate([(1 - x, y), (x, 1 - y), (1 - x, 1 - y)]):
                cps.append(pltpu.make_async_remote_copy(
                    src_ref=g_refs[a].at[2 * px + py], dst_ref=o_refs[a].at[mine], send_sem=send_sems.at[a, k],
                    recv_sem=recv_sems.at[a, k], device_id=(px, py, c), device_id_type=MESH))
        return cps

    def start(g_refs, o_refs, sems):
        for cp in copies(g_refs, o_refs, sems):
            cp.start()

    def finish(g_refs, o_refs, sems):
        for cp in copies(g_refs, o_refs, sems):
            cp.wait()

    return dict(ins=list(gs), outs=[jax.ShapeDtypeStruct(g.shape, g.dtype) for g in gs],
                scratch=[pltpu.SemaphoreType.DMA((na, 3)), pltpu.SemaphoreType.DMA((na, 3)),
                         pltpu.SemaphoreType.DMA((na,))], start=start, finish=finish)


def share_halves_comm(vs):
    na = len(vs)

    def copies(v_refs, o_refs, sems):
        send_sems, recv_sems = sems
        x, y, c = _me()
        return [pltpu.make_async_remote_copy(
            src_ref=v_refs[a], dst_ref=o_refs[a], send_sem=send_sems.at[a], recv_sem=recv_sems.at[a],
            device_id=(x, y, 1 - c), device_id_type=MESH) for a in range(na)]

    def start(v_refs, o_refs, sems):
        for cp in copies(v_refs, o_refs, sems):
            cp.start()

    def finish(v_refs, o_refs, sems):
        for cp in copies(v_refs, o_refs, sems):
            cp.wait()

    return dict(ins=list(vs), outs=[jax.ShapeDtypeStruct(v.shape, v.dtype) for v in vs],
                scratch=[pltpu.SemaphoreType.DMA((na,)), pltpu.SemaphoreType.DMA((na,))], start=start, finish=finish)


def _row_block(rows, cap):
    rb = math.gcd(rows, cap)
    return rb if rb % 8 == 0 else rows


def sum_leading(x, out_dtype, name):
    n, r, c = x.shape
    rb = _row_block(r, 1024)

    def body(x_ref, o_ref):
        acc = x_ref[0].astype(F32)
        for k in range(1, n):
            acc = acc + x_ref[k].astype(F32)
        o_ref[...] = acc.astype(o_ref.dtype)

    return pl.pallas_call(
        body, name=name, grid=(r // rb,), out_shape=jax.ShapeDtypeStruct((r, c), out_dtype),
        in_specs=[pl.BlockSpec((n, rb, c), lambda i: (0, i, 0))], out_specs=pl.BlockSpec((rb, c), lambda i: (i, 0)),
        compiler_params=_cp("parallel"),
    )(x)


def pair_sum(g, got, core, name):
    _, r2, n = g.shape
    hr = r2 // 2
    rb = math.gcd(hr, 512)
    nb = hr // rb

    def body(c_ref, a_ref, b_ref, o_ref):
        o_ref[...] = (a_ref[...].astype(F32) + b_ref[...].astype(F32)).astype(o_ref.dtype)

    spec = pl.BlockSpec((None, rb, n), lambda s, i, c_ref: (s, i, 0))
    return pl.pallas_call(
        body, name=name, out_shape=jax.ShapeDtypeStruct((4, hr, n), g.dtype),
        grid_spec=pltpu.PrefetchScalarGridSpec(
            num_scalar_prefetch=1, grid=(4, nb),
            in_specs=[pl.BlockSpec((None, rb, n), lambda s, i, c_ref: (s, c_ref[0] * nb + i, 0)), spec],
            out_specs=spec),
        compiler_params=_cp("parallel", "parallel"),
    )(core, g, got)


def adamw(w, gs, m, v, name, comm=None):
    rows, c = w.shape
    ng = len(gs)
    r = rows // ng
    rb = _row_block(r, 128 if c > 2048 else 256 if c > 1024 else 1024)
    nb = r // rb
    bc1 = 1.0 - ADAM_B1 ** ADAM_STEP
    bc2 = 1.0 - ADAM_B2 ** ADAM_STEP

    def body(w_ref, *refs):
        g_refs, (m_ref, v_ref, g_out, d_ref, nm_ref, nv_ref) = refs[:ng], refs[ng:]
        part = pl.program_id(0) // nb
        gg = g_refs[0][...]
        for k in range(1, ng):
            gg = jnp.where(part == k, g_refs[k][...], gg)
        nm = ADAM_B1 * m_ref[...] + (1.0 - ADAM_B1) * gg
        nv = ADAM_B2 * v_ref[...] + (1.0 - ADAM_B2) * (gg * gg)
        g_out[...] = gg
        d_ref[...] = -ADAM_LR * ((nm / bc1) / (jnp.sqrt(nv / bc2) + ADAM_EPS) + ADAM_WD * w_ref[...])
        nm_ref[...] = nm
        nv_ref[...] = nv

    spec = pl.BlockSpec((rb, c), lambda i: (i, 0))
    gspecs = [pl.BlockSpec((rb, c), functools.partial(lambda k, i: (jnp.clip(i - k * nb, 0, nb - 1), 0), k))
              for k in range(ng)]
    sds = jax.ShapeDtypeStruct((rows, c), F32)
    return _pcall(body, name=name, grid=(ng * nb,), out_shape=(sds,) * 4, in_specs=[spec] + gspecs + [spec, spec],
                  out_specs=(spec,) * 4, args=(w, *gs, m, v), sem=("parallel",), comm=comm)


def adamw_halves(w, mine, other, m, v, core, name):
    rows, c = w.shape
    nl = len(mine)
    hr = rows // (2 * nl)
    rb = _row_block(hr, 128 if c > 2048 else 256 if c > 1024 else 1024)
    nb = hr // rb
    bc1 = 1.0 - ADAM_B1 ** ADAM_STEP
    bc2 = 1.0 - ADAM_B2 ** ADAM_STEP

    def body(c_ref, w_ref, *refs):
        mine_refs, other_refs = refs[:nl], refs[nl:2 * nl]
        m_ref, v_ref, g_out, d_ref, nm_ref, nv_ref = refs[2 * nl:]
        part = pl.program_id(0) // nb
        layer, half = part // 2, part % 2
        gg = jnp.where(half == c_ref[0], mine_refs[0][...], other_refs[0][...])
        for k in range(1, nl):
            gg = jnp.where(layer == k, jnp.where(half == c_ref[0], mine_refs[k][...], other_refs[k][...]), gg)
        nm = ADAM_B1 * m_ref[...] + (1.0 - ADAM_B1) * gg
        nv = ADAM_B2 * v_ref[...] + (1.0 - ADAM_B2) * (gg * gg)
        g_out[...] = gg
        d_ref[...] = -ADAM_LR * ((nm / bc1) / (jnp.sqrt(nv / bc2) + ADAM_EPS) + ADAM_WD * w_ref[...])
        nm_ref[...] = nm
        nv_ref[...] = nv

    spec = pl.BlockSpec((rb, c), lambda i, c_ref: (i, 0))
    gspecs = [pl.BlockSpec((rb, c), functools.partial(
        lambda k, i, c_ref: (jnp.clip(i - 2 * k * nb, 0, 2 * nb - 1) % nb, 0), k)) for k in range(nl)]
    sds = jax.ShapeDtypeStruct((rows, c), F32)
    return pl.pallas_call(
        body, name=name, out_shape=(sds,) * 4,
        grid_spec=pltpu.PrefetchScalarGridSpec(
            num_scalar_prefetch=1, grid=(2 * nl * nb,), in_specs=[spec] + gspecs + gspecs + [spec, spec],
            out_specs=(spec,) * 4),
        compiler_params=_cp("parallel"),
    )(core, w, *mine, *other, m, v)


def _pick(n, prefs):
    for p in prefs:
        if n % p == 0:
            return p
    return n


def _hosted(body, n_in, n_out, comm, first, last):
    if comm is None:
        return body
    k, ns = len(comm["ins"]), len(comm["scratch"])

    def wrapped(*refs):
        ins, cins = refs[:n_in], refs[n_in:n_in + k]
        outs, couts = refs[n_in + k:n_in + k + n_out], refs[n_in + k + n_out:n_in + 2 * k + n_out]
        rest = refs[n_in + 2 * k + n_out:]
        scratch, sems = rest[:len(rest) - ns], rest[len(rest) - ns:]

        @pl.when(first())
        def _():
            comm["start"](cins, couts, sems)

        body(*ins, *outs, *scratch)

        @pl.when(last())
        def _():
            comm["finish"](cins, couts, sems)

    return wrapped


def _comm_extras(comm):
    if comm is None:
        return [], [], [], []
    return list(comm["ins"]), [ANY] * len(comm["ins"]), list(comm["outs"]), list(comm["scratch"])


def _mm_body(dims, nk, out_dtype, split=None):
    def body(*refs):
        if split is None:
            (a_ref, b_ref), rest = refs[:2], refs[2:]
            a, b = a_ref[...], b_ref[...]
        else:
            operand, axis, bound = split
            low = pl.program_id(axis) < bound
            if operand == 0:
                (a0_ref, a1_ref, b_ref), rest = refs[:3], refs[3:]
                a, b = jnp.where(low, a0_ref[...], a1_ref[...]), b_ref[...]
            else:
                (a_ref, b0_ref, b1_ref), rest = refs[:3], refs[3:]
                a, b = a_ref[...], jnp.where(low, b0_ref[...], b1_ref[...])
        o_ref, acc = rest[0], (rest[1] if len(rest) > 1 else None)
        kk = pl.program_id(2)
        part = lax.dot_general(a.astype(BF16), b.astype(BF16), (dims, ((), ())), preferred_element_type=F32)
        if nk == 1:
            o_ref[...] = part.astype(out_dtype)
        else:
            @pl.when(kk == 0)
            def _():
                acc[...] = part

            @pl.when(kk > 0)
            def _():
                acc[...] += part

            @pl.when(kk == nk - 1)
            def _():
                o_ref[...] = acc[...].astype(out_dtype)
    return body


def _acc(nk, shape):
    return [pltpu.VMEM(shape, F32)] if nk > 1 else []


def mm_nn(a, w, out_dtype, name, comm=None):
    M, K = a.shape
    S, _, Ns = w.shape
    tm = _pick(M, (1088, 1024, 512, 256, 128))
    tn = _pick(Ns, (1024, 896, 1408, 512, 256, 128))
    tk = _pick(K, (2816, 2048, 1408, 1024, 512, 256, 128))
    nps, nk = Ns // tn, K // tk
    grid = (S * nps, M // tm, nk)
    ids = lambda: [pl.program_id(d) for d in range(3)]
    first = lambda: functools.reduce(jnp.logical_and, [p == 0 for p in ids()])
    last = lambda: functools.reduce(jnp.logical_and, [p == g - 1 for p, g in zip(ids(), grid)])
    cin, cspec, cout, csem = _comm_extras(comm)
    out = pl.pallas_call(
        _hosted(_mm_body(((1,), (0,)), nk, out_dtype), 2, 1, comm, first, last), name=name, grid=grid,
        out_shape=[jax.ShapeDtypeStruct((M, S * Ns), out_dtype)] + cout,
        in_specs=[pl.BlockSpec((tm, tk), lambda j, i, k: (i, k)),
                  pl.BlockSpec((None, tk, tn), lambda j, i, k: (j // nps, k, j % nps))] + cspec,
        out_specs=[pl.BlockSpec((tm, tn), lambda j, i, k: (i, j))] + cspec,
        scratch_shapes=_acc(nk, (tm, tn)) + csem,
        compiler_params=_cp(*(("arbitrary",) * 3 if comm else ("parallel", "parallel", "arbitrary"))),
    )(a, w, *cin)
    return out[0] if comm is None else out


def mm_nt(dy, w, out_dtype, name, comm=None):
    pair = isinstance(dy, (tuple, list))
    M = dy[0].shape[0] if pair else dy.shape[0]
    S, K, Ns = w.shape
    N = S * Ns
    tm = _pick(M, (544, 512, 256, 128) if pair else (1088, 1024, 512, 256, 128))
    tn = _pick(K, (1408, 1024, 512, 256, 128))
    tk = _pick(Ns, (2816, 2048, 1792, 1408, 1024, 896, 512, 256, 128))
    kps, nk = Ns // tk, N // tk
    half = nk // 2
    if pair:
        dy_specs = [pl.BlockSpec((tm, tk), lambda j, i, k: (i, jnp.minimum(k, half - 1))),
                    pl.BlockSpec((tm, tk), lambda j, i, k: (i, jnp.maximum(k - half, 0)))]
    else:
        dy_specs = [pl.BlockSpec((tm, tk), lambda j, i, k: (i, k))]
    out = _pcall(
        _mm_body(((1,), (1,)), nk, out_dtype, split=(0, 2, half) if pair else None), name=name,
        grid=(K // tn, M // tm, nk), out_shape=[jax.ShapeDtypeStruct((M, K), out_dtype)],
        in_specs=dy_specs + [pl.BlockSpec((None, tn, tk), lambda j, i, k: (k // kps, j, k % kps))],
        out_specs=[pl.BlockSpec((tm, tn), lambda j, i, k: (i, j))], args=(*(dy if pair else (dy,)), w),
        scratch=_acc(nk, (tm, tn)), sem=("parallel", "parallel", "arbitrary"), comm=comm)
    return out[0] if comm is None else out


def mm_tn(a, dy, S, out_dtype, name):
    M, K = a.shape
    pair = isinstance(dy, (tuple, list))
    N = 2 * dy[0].shape[1] if pair else dy.shape[1]
    Ns = N // S
    to = _pick(K, (512, 256, 128) if pair else (1024, 512, 256, 128))
    tn = _pick(Ns, (2816, 2048, 1792, 1408, 1024, 896, 512, 256, 128))
    tk = _pick(M, (1088, 1024, 512, 256, 128))
    nps, nk = Ns // tn, M // tk
    half = S * nps // 2
    if pair:
        dy_specs = [pl.BlockSpec((tk, tn), lambda i, j, k: (k, jnp.minimum(j, half - 1))),
                    pl.BlockSpec((tk, tn), lambda i, j, k: (k, jnp.maximum(j - half, 0)))]
    else:
        dy_specs = [pl.BlockSpec((tk, tn), lambda i, j, k: (k, j))]
    return pl.pallas_call(
        _mm_body(((0,), (0,)), nk, out_dtype, split=(1, 1, half) if pair else None), name=name,
        grid=(K // to, S * nps, nk), out_shape=jax.ShapeDtypeStruct((S, K, Ns), out_dtype),
        in_specs=[pl.BlockSpec((tk, to), lambda i, j, k: (k, i))] + dy_specs,
        out_specs=pl.BlockSpec((None, to, tn), lambda i, j, k: (j // nps, i, j % nps)),
        scratch_shapes=_acc(nk, (to, tn)), compiler_params=_cp("parallel", "parallel", "arbitrary"),
    )(a, *(dy if pair else (dy,)))


_DIMS = {"nn": ((1,), (0,)), "nt": ((1,), (1,)), "tn": ((0,), (0,))}


def _dot(a, b, mode):
    return lax.dot_general(a.astype(BF16), b.astype(BF16), (_DIMS[mode], ((), ())), preferred_element_type=F32)


@functools.partial(jax.custom_vjp, nondiff_argnums=(2,))
def mmf(a, b, mode):
    return _dot(a, b, mode)


def _mmf_fwd(a, b, mode):
    return _dot(a, b, mode), (a, b)


def _mmf_bwd(mode, res, ct):
    a, b = res
    if mode == "nn":
        return _dot(ct, b, "nt"), _dot(a, ct, "tn")
    if mode == "nt":
        return _dot(ct, b, "nn"), _dot(ct, a, "tn")
    return _dot(b, ct, "nt"), _dot(a, ct, "nn")


mmf.defvjp(_mmf_fwd, _mmf_bwd)


def _dot_hi(m, g):
    return jnp.dot(m, g, precision=lax.Precision.HIGHEST, preferred_element_type=F32)


@jax.custom_vjp
def cumdot(m, mt, g):
    return _dot_hi(m, g)


def _cumdot_fwd(m, mt, g):
    return _dot_hi(m, g), (m, mt)


def _cumdot_bwd(res, ct):
    m, mt = res
    return jnp.zeros_like(m), jnp.zeros_like(mt), _dot_hi(mt, ct)


cumdot.defvjp(_cumdot_fwd, _cumdot_bwd)


def _rms(x, w):
    return x * lax.rsqrt(jnp.mean(x * x, axis=-1, keepdims=True) + EPS) * w


def _silu(x):
    return x * jax.nn.sigmoid(x)


RT = 16


def _gn_math(has_gate, x, m, gate, lnw, shift, scale):
    xn = x + gate * m if has_gate else x
    h = _rms(xn, lnw) * (1.0 + scale) + shift
    return xn, h


def _seg_spec(width, ncb):
    return pl.BlockSpec((None, RT, width), lambda i: (jnp.minimum(i // ncb, 1), 0, 0))


def gate_norm(x, m, gate, lnw, shift, scale, nc, R, name, comm=None):
    T, D = x.shape
    has_gate = m is not None
    ncb = nc // R

    def body(*refs):
        if has_gate:
            x_ref, m_ref, g_ref, w_ref, sh_ref, sc_ref, xn_ref, h_ref = refs
        else:
            x_ref, w_ref, sh_ref, sc_ref, h_ref = refs

        def step(t, carry):
            rows = pl.ds(pl.multiple_of(t * RT, RT), RT)
            xn, h = _gn_math(has_gate, x_ref[rows, :], m_ref[rows, :] if has_gate else None,
                             g_ref[...] if has_gate else None, w_ref[...], sh_ref[...], sc_ref[...])
            if has_gate:
                xn_ref[rows, :] = xn
            h_ref[rows, :] = h.astype(BF16)
            return carry

        lax.fori_loop(0, R // RT, step, 0)

    row = pl.BlockSpec((R, D), lambda i: (i, 0))
    seg = _seg_spec(D, ncb)
    shared = pl.BlockSpec((None, RT, D), lambda i: (0, 0, 0))
    if has_gate:
        ins, in_specs = (x, m, gate, lnw, shift, scale), [row, row, seg, shared, seg, seg]
        out_shape = (jax.ShapeDtypeStruct((T, D), F32), jax.ShapeDtypeStruct((T, D), BF16))
        out_specs = (row, row)
    else:
        ins, in_specs = (x, lnw, shift, scale), [row, shared, seg, seg]
        out_shape, out_specs = (jax.ShapeDtypeStruct((T, D), BF16),), (row,)
    out = _pcall(body, name=name, grid=(T // R,), out_shape=out_shape, in_specs=in_specs, out_specs=out_specs,
                 args=ins, sem=("parallel",), comm=comm)
    own = tuple(out[:2]) if has_gate else (None, out[0])
    return own if comm is None else own + tuple(out[2 if has_gate else 1:])


def gate_norm_bwd(x, m, gate, lnw, shift, scale, dxn, dh, nc, R, name):
    T, D = x.shape
    has_gate = m is not None
    ncb = nc // R

    def body(*refs):
        if has_gate:
            (x_ref, m_ref, g_ref, w_ref, sh_ref, sc_ref, dxn_ref, dh_ref,
             dx_ref, dm_ref, dg_ref, dw_ref, dsh_ref, dsc_ref) = refs
        else:
            x_ref, w_ref, sh_ref, sc_ref, dxn_ref, dh_ref, dx_ref, dw_ref, dsh_ref, dsc_ref = refs
        i = pl.program_id(0)

        @pl.when(i == 0)
        def _():
            dw_ref[...] = jnp.zeros_like(dw_ref)

        @pl.when((i == 0) | (i == ncb))
        def _():
            dsh_ref[...] = jnp.zeros_like(dsh_ref)
            dsc_ref[...] = jnp.zeros_like(dsc_ref)
            if has_gate:
                dg_ref[...] = jnp.zeros_like(dg_ref)

        def step(t, carry):
            rows = pl.ds(pl.multiple_of(t * RT, RT), RT)
            ct = (dxn_ref[rows, :], dh_ref[rows, :])
            if has_gate:
                _, vjp = jax.vjp(functools.partial(_gn_math, True), x_ref[rows, :], m_ref[rows, :], g_ref[...],
                                 w_ref[...], sh_ref[...], sc_ref[...])
                dx, dm, dg, dw, dsh, dsc = vjp(ct)
                dm_ref[rows, :] = dm.astype(BF16)
                dg_ref[...] += dg
            else:
                f = lambda x_, w_, sh_, sc_: _gn_math(False, x_, None, None, w_, sh_, sc_)[1]
                _, vjp = jax.vjp(f, x_ref[rows, :], w_ref[...], sh_ref[...], sc_ref[...])
                dx, dw, dsh, dsc = vjp(ct[1])
                dx = dx + ct[0]
            dx_ref[rows, :] = dx
            dw_ref[...] += dw
            dsh_ref[...] += dsh
            dsc_ref[...] += dsc
            return carry

        lax.fori_loop(0, R // RT, step, 0)

    row = pl.BlockSpec((R, D), lambda i: (i, 0))
    seg = _seg_spec(D, ncb)
    shared = pl.BlockSpec((None, RT, D), lambda i: (0, 0, 0))
    full, segs, one = jax.ShapeDtypeStruct((T, D), F32), jax.ShapeDtypeStruct((2, RT, D), F32), \
        jax.ShapeDtypeStruct((1, RT, D), F32)
    if has_gate:
        ins = (x, m, gate, lnw, shift, scale, dxn, dh)
        in_specs = [row, row, seg, shared, seg, seg, row, row]
        out_shape = (full, jax.ShapeDtypeStruct((T, D), BF16), segs, one, segs, segs)
        out_specs = (row, row, seg, shared, seg, seg)
    else:
        ins = (x, lnw, shift, scale, dxn, dh)
        in_specs = [row, shared, seg, seg, row, row]
        out_shape = (full, one, segs, segs)
        out_specs = (row, shared, seg, seg)
    out = pl.pallas_call(body, name=name, grid=(T // R,), out_shape=out_shape, in_specs=in_specs,
                         out_specs=out_specs, compiler_params=_cp("arbitrary"))(*ins)
    if has_gate:
        return out
    dx, dw, dsh, dsc = out
    return dx, None, None, dw, dsh, dsc


def gate_loss(x, m, gate, target, nc, R, name):
    T, D = x.shape
    ncb = nc // R

    def body(x_ref, m_ref, g_ref, t_ref, loss_ref, dx_ref, dm_ref, dg_ref):
        i = pl.program_id(0)

        @pl.when(i == 0)
        def _():
            loss_ref[...] = jnp.zeros_like(loss_ref)

        @pl.when((i == 0) | (i == ncb))
        def _():
            dg_ref[...] = jnp.zeros_like(dg_ref)

        live = jnp.where(i >= ncb, 1.0, 0.0).astype(F32)

        def step(t, carry):
            rows = pl.ds(pl.multiple_of(t * RT, RT), RT)
            mm_ = m_ref[rows, :]
            g = g_ref[...]
            e = (x_ref[rows, :] + g * mm_ - t_ref[rows, :]) * live
            dy = e * (1.0 / D)
            loss_ref[...] += 0.5 * e * dy
            dx_ref[rows, :] = dy
            dm_ref[rows, :] = (dy * g).astype(BF16)
            dg_ref[...] += dy * mm_
            return carry

        lax.fori_loop(0, R // RT, step, 0)

    row = pl.BlockSpec((R, D), lambda i: (i, 0))
    seg = _seg_spec(D, ncb)
    return pl.pallas_call(
        body, name=name, grid=(T // R,),
        out_shape=(jax.ShapeDtypeStruct((RT, D), F32), jax.ShapeDtypeStruct((T, D), F32),
                   jax.ShapeDtypeStruct((T, D), BF16), jax.ShapeDtypeStruct((2, RT, D), F32)),
        in_specs=[row, row, seg, pl.BlockSpec((R, D), lambda i: (jnp.maximum(i - ncb, 0), 0))],
        out_specs=(pl.BlockSpec((RT, D), lambda i: (0, 0)), row, row, seg),
        compiler_params=_cp("arbitrary"),
    )(x, m, gate, target)


def _hg_chunk(rev, lb, z, iv, hq, st):
    f = lb + (1.0 - lb) * jax.nn.sigmoid(z)
    g = jnp.log(jnp.maximum(f, F_FLOOR))
    k = (1.0 - lb) * jax.nn.sigmoid(-z)
    q = _silu(hq)
    ri = lax.broadcasted_iota(jnp.int32, (CHUNK, CHUNK), 0)
    ci = lax.broadcasted_iota(jnp.int32, (CHUNK, CHUNK), 1)
    r1 = lax.broadcasted_iota(jnp.int32, (CHUNK, 1), 0)
    seen = (ci >= ri) if rev else (ci <= ri)
    seen_t = (ci <= ri) if rev else (ci >= ri)
    cum = cumdot(seen.astype(F32), seen_t.astype(F32), g)
    tot = jnp.sum(g, axis=0, keepdims=True)
    att = jnp.zeros((CHUNK, CHUNK), F32)
    ref_rows = jnp.zeros_like(g)
    refs = []
    for b in range(CHUNK // SUB):
        before = (r1 >= SUB * (b + 1)) if rev else (r1 < SUB * b)
        r_b = jnp.sum(jnp.where(before, g, 0.0), axis=0, keepdims=True)
        in_b = (r1 >= SUB * b) & (r1 < SUB * (b + 1))
        ref_rows = ref_rows + jnp.where(in_b, r_b, 0.0)
        refs.append(r_b)
    qd = q * jnp.exp(cum - ref_rows)
    for b in range(CHUNK // SUB):
        kd = k * jnp.exp(jnp.minimum(refs[b] - cum, EXP_CLAMP))
        in_b = (ri >= SUB * b) & (ri < SUB * (b + 1))
        att = att + jnp.where(in_b, mmf(qd, kd, "nt"), 0.0)
    att = jnp.where(seen, att, 0.0)
    o = mmf(att, iv, "nn") + mmf(q * jnp.exp(cum), st, "nt")
    st_new = st * jnp.exp(tot) + mmf(iv, k * jnp.exp(tot - cum), "tn")
    return st_new, o


def _hg_cid(rev, i, ncs, n):
    if not rev:
        return i
    return jnp.where(i < ncs, ncs - 1 - i, ncs + n - 1 - i)


def hgrn_fwd(u, lb, rev, zcol, nc, hgw, name):
    T = u.shape[0]
    n, ncs, nh = T // CHUNK, nc // CHUNK, hgw // LANE

    def body(z_ref, v_ref, q_ref, lb_ref, o_ref, s_ref, st):
        i = pl.program_id(0)

        @pl.when(i == 0)
        def _():
            st[...] = jnp.zeros_like(st)

        for h in range(nh):
            cols = slice(h * LANE, (h + 1) * LANE)
            s_ref[h] = st[h]
            s_new, o = _hg_chunk(rev, lb_ref[:, cols], z_ref[:, cols], v_ref[:, cols], q_ref[:, cols], st[h])
            st[h] = s_new
            o_ref[:, cols] = o

    def col(cb):
        return pl.BlockSpec((CHUNK, hgw), lambda i: (_hg_cid(rev, i, ncs, n), cb))

    return pl.pallas_call(
        body, name=name, grid=(n,),
        out_shape=(jax.ShapeDtypeStruct((T, hgw), F32), jax.ShapeDtypeStruct((n, nh, LANE, LANE), F32)),
        in_specs=[col(zcol), col(2), col(7), pl.BlockSpec((1, hgw), lambda i: (0, 0))],
        out_specs=(pl.BlockSpec((CHUNK, hgw), lambda i: (_hg_cid(rev, i, ncs, n), 0)),
                   pl.BlockSpec((None, nh, LANE, LANE), lambda i: (i, 0, 0, 0))),
        scratch_shapes=[pltpu.VMEM((nh, LANE, LANE), F32)], compiler_params=_cp("arbitrary"),
    )(u, u, u, lb)


def hgrn_bwd(u, lb, states, do, rev, zcol, nc, hgw, name):
    T = u.shape[0]
    n, ncs, nh = T // CHUNK, nc // CHUNK, hgw // LANE

    def body(z_ref, v_ref, q_ref, lb_ref, s_ref, do_ref, dz_ref, dv_ref, dq_ref, dlb_ref, dst):
        j = pl.program_id(0)

        @pl.when(j == 0)
        def _():
            dst[...] = jnp.zeros_like(dst)
            dlb_ref[...] = jnp.zeros_like(dlb_ref)

        for h in range(nh):
            cols = slice(h * LANE, (h + 1) * LANE)
            _, vjp = jax.vjp(functools.partial(_hg_chunk, rev), lb_ref[:, cols], z_ref[:, cols], v_ref[:, cols],
                             q_ref[:, cols], s_ref[h])
            dlb, dz, dv, dq, ds = vjp((dst[h], do_ref[:, cols]))
            dst[h] = ds
            dz_ref[:, cols] = dz
            dv_ref[:, cols] = dv
            dq_ref[:, cols] = dq
            dlb_ref[:, cols] += dlb

    def cid(j):
        return _hg_cid(rev, n - 1 - j, ncs, n)

    def col(cb):
        return pl.BlockSpec((CHUNK, hgw), lambda j: (cid(j), cb))

    out = pl.BlockSpec((CHUNK, hgw), lambda j: (cid(j), 0))
    full = jax.ShapeDtypeStruct((T, hgw), F32)
    return pl.pallas_call(
        body, name=name, grid=(n,),
        out_shape=(full, full, full, jax.ShapeDtypeStruct((1, hgw), F32)),
        in_specs=[col(zcol), col(2), col(7), pl.BlockSpec((1, hgw), lambda j: (0, 0)),
                  pl.BlockSpec((None, nh, LANE, LANE), lambda j: (n - 1 - j, 0, 0, 0)), out],
        out_specs=(out, out, out, pl.BlockSpec((1, hgw), lambda j: (0, 0))),
        scratch_shapes=[pltpu.VMEM((nh, LANE, LANE), F32)], compiler_params=_cp("arbitrary"),
    )(u, u, u, lb, states, do)


HT = 128


def _head_group(nh, *col_offsets):
    for g in (4, 2):
        if nh % g == 0 and all(c % g == 0 for c in col_offsets):
            return g
    return 1


def _read_math(ofw, obw, g, w):
    return _rms(ofw + obw, w) * _silu(g)


def hg_read(ofw, obw, u, w, gcol, R, name):
    T, hgw = ofw.shape
    nh = hgw // LANE
    g = _head_group(nh, gcol)

    def body(a_ref, b_ref, g_ref, w_ref, o_ref):
        for j in range(g):
            cols = slice(j * LANE, (j + 1) * LANE)
            for t in range(R // HT):
                rows = slice(t * HT, (t + 1) * HT)
                o_ref[rows, cols] = _read_math(a_ref[rows, cols], b_ref[rows, cols], g_ref[rows, cols],
                                               w_ref[...]).astype(BF16)

    blk = pl.BlockSpec((R, g * LANE), lambda i, h: (i, h))
    return pl.pallas_call(
        body, name=name, grid=(T // R, nh // g), out_shape=jax.ShapeDtypeStruct((T, hgw), BF16),
        in_specs=[blk, blk, pl.BlockSpec((R, g * LANE), lambda i, h: (i, gcol // g + h)),
                  pl.BlockSpec((1, LANE), lambda i, h: (0, 0))],
        out_specs=blk, compiler_params=_cp("parallel", "parallel"),
    )(ofw, obw, u, w)


def hg_read_bwd(ofw, obw, u, w, dout, gcol, ocol, R, name):
    T, hgw = ofw.shape
    nh = hgw // LANE
    g = _head_group(nh, gcol, ocol)

    def body(a_ref, b_ref, g_ref, w_ref, d_ref, do_ref, dg_ref, dw_ref):
        @pl.when(pl.program_id(1) == 0)
        def _():
            dw_ref[...] = jnp.zeros_like(dw_ref)

        for j in range(g):
            cols = slice(j * LANE, (j + 1) * LANE)
            for t in range(R // HT):
                rows = slice(t * HT, (t + 1) * HT)
                _, vjp = jax.vjp(_read_math, a_ref[rows, cols], b_ref[rows, cols], g_ref[rows, cols], w_ref[...])
                da, _, dg, dw = vjp(d_ref[rows, cols])
                do_ref[rows, cols] = da
                dg_ref[rows, cols] = dg
                dw_ref[j] += dw

    blk = pl.BlockSpec((R, g * LANE), lambda h, i: (i, h))
    full = jax.ShapeDtypeStruct((T, hgw), F32)
    return pl.pallas_call(
        body, name=name, grid=(nh // g, T // R), out_shape=(full, full, jax.ShapeDtypeStruct((nh, 1, LANE), F32)),
        in_specs=[blk, blk, pl.BlockSpec((R, g * LANE), lambda h, i: (i, gcol // g + h)),
                  pl.BlockSpec((1, LANE), lambda h, i: (0, 0)),
                  pl.BlockSpec((R, g * LANE), lambda h, i: (i, ocol // g + h))],
        out_specs=(blk, blk, pl.BlockSpec((g, 1, LANE), lambda h, i: (h, 0, 0))),
        compiler_params=_cp("parallel", "arbitrary"),
    )(ofw, obw, u, w, dout)


def _na_step(qw, ow, bias, qraw, kl, vl, kc, vc):
    q = _rms(qraw, qw)
    s_loc = mmf(q, kl, "nt") * ATTN_SCALE + bias
    s_ctx = mmf(q, kc, "nt") * ATTN_SCALE
    m = lax.stop_gradient(jnp.maximum(jnp.max(s_loc, axis=-1, keepdims=True), jnp.max(s_ctx, axis=-1, keepdims=True)))
    p_loc = jnp.exp(s_loc - m)
    p_ctx = jnp.exp(s_ctx - m)
    inv = 1.0 / (jnp.sum(p_loc, axis=-1, keepdims=True) + jnp.sum(p_ctx, axis=-1, keepdims=True))
    return _rms(mmf(p_loc * inv, vl, "nn") + mmf(p_ctx * inv, vc, "nn"), ow)


def _na_geometry(nc, rows):
    ncs = nc // GRID_W
    win_r = min(WIN_R, rows)
    nloc = win_r * GRID_W

    def row_start(s):
        r = jnp.maximum(s - ncs, 0)
        return jnp.clip(r - win_r // 2, 0, rows - win_r)

    def bias_idx(s):
        r = s - ncs
        return jnp.where(s < ncs, win_r, r - jnp.clip(r - win_r // 2, 0, rows - win_r))

    return ncs, win_r, nloc, row_start, bias_idx


def na_bias_tables(rpb, rows):
    win_r = min(WIN_R, rows)
    nh = rpb.shape[0]
    sel_r = np.zeros((win_r, win_r, 2 * WIN_R - 1), np.float32)
    for off in range(win_r):
        for jr in range(win_r):
            sel_r[off, jr, jr - off + WIN_R - 1] = 1.0
    qc = np.arange(GRID_W)[:, None]
    kc = np.arange(GRID_W)[None, :]
    wstart = np.clip(qc - WIN_C // 2, 0, GRID_W - WIN_C)
    ok = (kc >= wstart) & (kc < wstart + WIN_C)
    sel_c = np.zeros((GRID_W, GRID_W, 2 * WIN_C - 1), np.float32)
    sel_c[np.broadcast_to(qc, ok.shape)[ok], np.broadcast_to(kc, ok.shape)[ok], (kc - qc + WIN_C - 1)[ok]] = 1.0
    hi = lax.Precision.HIGHEST
    t = jnp.einsum("hab,oja->hojb", rpb, sel_r, precision=hi)
    t = jnp.einsum("hojb,qkb->hoqjk", t, sel_c, precision=hi)
    t = jnp.where(ok[None, None, :, None, :], t, NEG_INF)
    t = jnp.concatenate([t, jnp.full((nh, 1, GRID_W, win_r, GRID_W), NEG_INF, F32)], axis=1)
    return t.reshape(nh, win_r + 1, GRID_W, win_r * GRID_W)


def kv_prep(u, kw, kcol, vcol, naw, R, name):
    T = u.shape[0]
    g = _head_group(naw // LANE, kcol, vcol)

    def body(k_ref, v_ref, w_ref, kn_ref, vb_ref):
        for j in range(g):
            cols = slice(j * LANE, (j + 1) * LANE)
            kn_ref[:, cols] = _rms(k_ref[:, cols], w_ref[...]).astype(BF16)
        vb_ref[...] = v_ref[...].astype(BF16)

    blk = pl.BlockSpec((R, g * LANE), lambda i, h: (i, h))
    sds = jax.ShapeDtypeStruct((T, naw), BF16)
    return pl.pallas_call(
        body, name=name, grid=(T // R, naw // LANE // g), out_shape=(sds, sds),
        in_specs=[pl.BlockSpec((R, g * LANE), lambda i, h: (i, kcol // g + h)),
                  pl.BlockSpec((R, g * LANE), lambda i, h: (i, vcol // g + h)),
                  pl.BlockSpec((1, LANE), lambda i, h: (0, 0))],
        out_specs=(blk, blk), compiler_params=_cp("parallel", "parallel"),
    )(u, u, kw)


def kv_prep_bwd(u, kw, dkn, kcol, naw, R, name):
    T = u.shape[0]
    nh = naw // LANE
    g = _head_group(nh, kcol)

    def body(k_ref, w_ref, d_ref, dk_ref, dw_ref):
        @pl.when(pl.program_id(1) == 0)
        def _():
            dw_ref[...] = jnp.zeros_like(dw_ref)

        for j in range(g):
            cols = slice(j * LANE, (j + 1) * LANE)
            for t in range(R // HT):
                rows = slice(t * HT, (t + 1) * HT)
                _, vjp = jax.vjp(_rms, k_ref[rows, cols], w_ref[...])
                dk, dw = vjp(d_ref[rows, cols])
                dk_ref[rows, cols] = dk
                dw_ref[j] += dw

    blk = pl.BlockSpec((R, g * LANE), lambda h, i: (i, h))
    return pl.pallas_call(
        body, name=name, grid=(nh // g, T // R),
        out_shape=(jax.ShapeDtypeStruct((T, naw), F32), jax.ShapeDtypeStruct((nh, 1, LANE), F32)),
        in_specs=[pl.BlockSpec((R, g * LANE), lambda h, i: (i, kcol // g + h)),
                  pl.BlockSpec((1, LANE), lambda h, i: (0, 0)), blk],
        out_specs=(blk, pl.BlockSpec((g, 1, LANE), lambda h, i: (h, 0, 0))),
        compiler_params=_cp("parallel", "arbitrary"),
    )(u, kw, dkn)


NA_HB = 4


def _na_operands(j, s, nc, nloc, row_start, q_refs, k_ref, v_ref, qw_ref, ow_ref, b_ref):
    cols = slice(j * LANE, (j + 1) * LANE)
    loc = pl.ds(pl.multiple_of(nc + row_start(s) * GRID_W, GRID_W), nloc)
    ops = (qw_ref[...], ow_ref[:, cols], b_ref[j], q_refs[j][...], k_ref[loc, cols].astype(F32),
           v_ref[loc, cols].astype(F32), k_ref[0:nc, cols].astype(F32), v_ref[0:nc, cols].astype(F32))
    return cols, loc, ops


def _grid_ends(grid):
    ids = lambda: [pl.program_id(d) for d in range(len(grid))]
    first = lambda: functools.reduce(jnp.logical_and, [p == 0 for p in ids()])
    last = lambda: functools.reduce(jnp.logical_and, [p == g - 1 for p, g in zip(ids(), grid)])
    return first, last


def na_fwd(u, kn, vb, qw, ow, bias, qcol, nc, name, comm=None):
    T, naw = kn.shape
    nh, rows = naw // LANE, (T - nc) // GRID_W
    hb = NA_HB if nh % NA_HB == 0 else 1
    ncs, win_r, nloc, row_start, bias_idx = _na_geometry(nc, rows)

    def body(*refs):
        q_refs, (k_ref, v_ref, qw_ref, ow_ref, b_ref, o_ref) = refs[:hb], refs[hb:]
        s = pl.program_id(1)
        for j in range(hb):
            cols, _, ops = _na_operands(j, s, nc, nloc, row_start, q_refs, k_ref, v_ref, qw_ref, ow_ref, b_ref)
            o_ref[:, cols] = _na_step(*ops).astype(BF16)

    wide = pl.BlockSpec((T, hb * LANE), lambda g, s: (0, g), pipeline_mode=pl.Buffered(1))
    grid = (nh // hb, T // GRID_W)
    cin, cspec, cout, csem = _comm_extras(comm)
    out = pl.pallas_call(
        _hosted(body, hb + 5, 1, comm, *_grid_ends(grid)), name=name, grid=grid,
        out_shape=[jax.ShapeDtypeStruct((T, naw), BF16)] + cout,
        in_specs=[pl.BlockSpec((GRID_W, LANE), functools.partial(lambda j, g, s: (s, qcol + g * hb + j), j))
                  for j in range(hb)]
        + [wide, wide, pl.BlockSpec((1, LANE), lambda g, s: (0, 0)), pl.BlockSpec((1, hb * LANE), lambda g, s: (0, g)),
           pl.BlockSpec((hb, None, GRID_W, nloc), lambda g, s: (g, bias_idx(s), 0, 0))] + cspec,
        out_specs=[pl.BlockSpec((GRID_W, hb * LANE), lambda g, s: (s, g))] + cspec, scratch_shapes=csem,
        compiler_params=_cp("arbitrary", "arbitrary"),
    )(*([u] * hb), kn, vb, qw, ow, bias, *cin)
    return out[0] if comm is None else out


def na_bwd(u, kn, vb, qw, ow, bias, dout, qcol, ocol, nc, name, comm=None):
    T, naw = kn.shape
    nh, rows = naw // LANE, (T - nc) // GRID_W
    hb = NA_HB if nh % NA_HB == 0 else 1
    ncs, win_r, nloc, row_start, bias_idx = _na_geometry(nc, rows)
    fresh = [0] + [ncs + r for r in range(rows) if r == 0 or r - np.clip(r - win_r // 2, 0, rows - win_r)
                   != (r - 1) - np.clip(r - 1 - win_r // 2, 0, rows - win_r)]

    def body(*refs):
        q_refs, d_refs = refs[:hb], refs[hb:2 * hb]
        k_ref, v_ref, qw_ref, ow_ref, b_ref, dq_ref, dk_ref, dv_ref, db_ref, dqw_ref, dow_ref = refs[2 * hb:]
        s = pl.program_id(1)

        @pl.when(s == 0)
        def _():
            dk_ref[...] = jnp.zeros_like(dk_ref)
            dv_ref[...] = jnp.zeros_like(dv_ref)
            dqw_ref[...] = jnp.zeros_like(dqw_ref)
            dow_ref[...] = jnp.zeros_like(dow_ref)

        first = functools.reduce(lambda a, b: a | b, [s == f for f in fresh])

        @pl.when(first)
        def _():
            db_ref[...] = jnp.zeros_like(db_ref)

        for j in range(hb):
            cols, loc, ops = _na_operands(j, s, nc, nloc, row_start, q_refs, k_ref, v_ref, qw_ref, ow_ref, b_ref)
            _, vjp = jax.vjp(_na_step, *ops)
            dqw, dow, db, dq, dkl, dvl, dkc, dvc = vjp(d_refs[j][...])
            dq_ref[:, cols] = dq
            dk_ref[loc, cols] += dkl
            dv_ref[loc, cols] += dvl
            dk_ref[0:nc, cols] += dkc
            dv_ref[0:nc, cols] += dvc
            db_ref[j] += db
            dqw_ref[j] += dqw
            dow_ref[j] += dow

    wide = pl.BlockSpec((T, hb * LANE), lambda g, s: (0, g), pipeline_mode=pl.Buffered(1))
    hvec = pl.BlockSpec((hb, 1, LANE), lambda g, s: (g, 0, 0))
    full = jax.ShapeDtypeStruct((T, naw), F32)
    hv = jax.ShapeDtypeStruct((nh, 1, LANE), F32)
    bspec = pl.BlockSpec((hb, None, GRID_W, nloc), lambda g, s: (g, bias_idx(s), 0, 0))
    grid = (nh // hb, T // GRID_W)
    cin, cspec, cout, csem = _comm_extras(comm)
    return pl.pallas_call(
        _hosted(body, 2 * hb + 5, 6, comm, *_grid_ends(grid)), name=name, grid=grid,
        out_shape=[full, full, full, jax.ShapeDtypeStruct(bias.shape, F32), hv, hv] + cout,
        in_specs=[pl.BlockSpec((GRID_W, LANE), functools.partial(lambda j, g, s: (s, qcol + g * hb + j), j))
                  for j in range(hb)]
        + [pl.BlockSpec((GRID_W, LANE), functools.partial(lambda j, g, s: (s, ocol + g * hb + j), j))
           for j in range(hb)]
        + [wide, wide, pl.BlockSpec((1, LANE), lambda g, s: (0, 0)), pl.BlockSpec((1, hb * LANE), lambda g, s: (0, g)),
           bspec] + cspec,
        out_specs=[pl.BlockSpec((GRID_W, hb * LANE), lambda g, s: (s, g)), wide, wide, bspec, hvec, hvec] + cspec,
        scratch_shapes=csem, compiler_params=_cp("arbitrary", "arbitrary"),
    )(*([u] * hb), *([dout] * hb), kn, vb, qw, ow, bias, *cin)


def _halo_specs(R, width, T, col):
    hb = R // 8
    prev = pl.BlockSpec((8, width), lambda j, i: (jnp.maximum(i * hb - 1, 0), col(j, i)))
    nxt = pl.BlockSpec((8, width), lambda j, i: (jnp.minimum((i + 1) * hb, T // 8 - 1), col(j, i)))
    return prev, nxt


def _edge_flags(i, ncb, nblk):
    has_prev = jnp.where((i == 0) | (i == ncb), 0.0, 1.0).astype(F32)
    has_next = jnp.where((i == ncb - 1) | (i == nblk - 1), 0.0, 1.0).astype(F32)
    return has_prev, has_next


def _shift_up(a, prev_row):
    r0 = lax.broadcasted_iota(jnp.int32, a.shape, 0) == 0
    return jnp.where(r0, prev_row, pltpu.roll(a, 1, 0))


def _shift_dn(a, next_row):
    n = a.shape[0]
    rl = lax.broadcasted_iota(jnp.int32, a.shape, 0) == n - 1
    return jnp.where(rl, next_row, pltpu.roll(a, n - 1, 0))


def _conv3(a, prev_row, next_row, w_ref):
    return w_ref[0:1, :] * _shift_up(a, prev_row) + w_ref[1:2, :] * a + w_ref[2:3, :] * _shift_dn(a, next_row)


def _cv_post(b, y, w):
    return _rms(b * y, w)


def short_conv(u, cw, ow, bcol, nc, cvw, R, bwd_dout=None, ocol=0, name=""):
    T = u.shape[0]
    nh, nblk, ncb = cvw // LANE, T // R, nc // R
    bwd = bwd_dout is not None
    g = _head_group(nh, bcol, bcol + nh, bcol + 2 * nh, ocol)
    gw = g * LANE

    def body(b_ref, c_ref, v_ref, cp_ref, vp_ref, cn_ref, vn_ref, cw_ref, ow_ref, *rest):
        i = pl.program_id(1)
        has_prev, has_next = _edge_flags(i, ncb, nblk)
        p = c_ref[...] * v_ref[...]
        y = _conv3(p, cp_ref[7:8, :] * vp_ref[7:8, :] * has_prev, cn_ref[0:1, :] * vn_ref[0:1, :] * has_next, cw_ref)
        if bwd:
            d_ref, db_ref, dy_ref, dow_ref = rest

            @pl.when(i == 0)
            def _():
                dow_ref[...] = jnp.zeros_like(dow_ref)

        for j in range(g):
            cols = slice(j * LANE, (j + 1) * LANE)
            if not bwd:
                rest[0][:, cols] = _cv_post(b_ref[:, cols], y[:, cols], ow_ref[:, cols]).astype(BF16)
            else:
                _, vjp = jax.vjp(_cv_post, b_ref[:, cols], y[:, cols], ow_ref[:, cols])
                db, dy, dow = vjp(d_ref[:, cols])
                db_ref[:, cols] = db
                dy_ref[:, cols] = dy
                dow_ref[j] += dow

    def main(k):
        return pl.BlockSpec((R, gw), lambda h, i: (i, (bcol + k * nh) // g + h))

    cprev, cnext = _halo_specs(R, gw, T, lambda h, i: (bcol + nh) // g + h)
    vprev, vnext = _halo_specs(R, gw, T, lambda h, i: (bcol + 2 * nh) // g + h)
    in_specs = [main(0), main(1), main(2), cprev, vprev, cnext, vnext,
                pl.BlockSpec((3, gw), lambda h, i: (0, h)), pl.BlockSpec((1, gw), lambda h, i: (0, h))]
    ins = [u] * 7 + [cw, ow]
    blk = pl.BlockSpec((R, gw), lambda h, i: (i, h))
    if not bwd:
        out_shape, out_specs = jax.ShapeDtypeStruct((T, cvw), BF16), blk
    else:
        in_specs.append(pl.BlockSpec((R, gw), lambda h, i: (i, ocol // g + h)))
        ins.append(bwd_dout)
        full = jax.ShapeDtypeStruct((T, cvw), F32)
        out_shape = (full, full, jax.ShapeDtypeStruct((nh, 1, LANE), F32))
        out_specs = (blk, blk, pl.BlockSpec((g, 1, LANE), lambda h, i: (h, 0, 0)))
    return pl.pallas_call(body, name=name, grid=(nh // g, nblk), out_shape=out_shape, in_specs=in_specs,
                          out_specs=out_specs, compiler_params=_cp("parallel", "arbitrary"))(*ins)


def conv3_bwd(dy, src, cw, nc, R, W, prod_cols=None, col0=0, out_dtype=F32, name=""):
    T, C = dy.shape
    nblk, ncb = T // R, nc // R
    prod = prod_cols is not None

    def body(*refs):
        if prod:
            (d_ref, dp_ref, dn_ref, c_ref, v_ref, cp_ref, vp_ref, cn_ref, vn_ref, w_ref,
             dc_ref, dv_ref, dw_ref) = refs
        else:
            d_ref, dp_ref, dn_ref, p_ref, pp_ref, pn_ref, w_ref, o_ref, dw_ref = refs
        i = pl.program_id(1)
        has_prev, has_next = _edge_flags(i, ncb, nblk)

        @pl.when(i == 0)
        def _():
            dw_ref[...] = jnp.zeros_like(dw_ref)

        d = d_ref[...]
        d_up = _shift_up(d, dp_ref[7:8, :] * has_prev)
        d_dn = _shift_dn(d, dn_ref[0:1, :] * has_next)
        dp = w_ref[0:1, :] * d_dn + w_ref[1:2, :] * d + w_ref[2:3, :] * d_up
        if prod:
            c, v = c_ref[...], v_ref[...]
            p = c * v
            p_prev, p_next = cp_ref[7:8, :] * vp_ref[7:8, :] * has_prev, cn_ref[0:1, :] * vn_ref[0:1, :] * has_next
            dc_ref[...] = dp * v
            dv_ref[...] = dp * c
        else:
            p = p_ref[...]
            p_prev, p_next = pp_ref[7:8, :] * has_prev, pn_ref[0:1, :] * has_next
            o_ref[...] = dp.astype(out_dtype)
        dw_ref[0:1, :] += jnp.sum(_shift_up(p, p_prev) * d, axis=0, keepdims=True)
        dw_ref[1:2, :] += jnp.sum(p * d, axis=0, keepdims=True)
        dw_ref[2:3, :] += jnp.sum(_shift_dn(p, p_next) * d, axis=0, keepdims=True)

    blk = pl.BlockSpec((R, W), lambda j, i: (i, j))
    dprev, dnext = _halo_specs(R, W, T, lambda j, i: j)
    wspec = pl.BlockSpec((3, W), lambda j, i: (0, j))
    dwspec = pl.BlockSpec((8, W), lambda j, i: (0, j))
    dwshape = jax.ShapeDtypeStruct((8, C), F32)
    if prod:
        ccol, vcol = prod_cols
        cprev, cnext = _halo_specs(R, W, T, lambda j, i: ccol + j)
        vprev, vnext = _halo_specs(R, W, T, lambda j, i: vcol + j)
        in_specs = [blk, dprev, dnext, pl.BlockSpec((R, W), lambda j, i: (i, ccol + j)),
                    pl.BlockSpec((R, W), lambda j, i: (i, vcol + j)), cprev, vprev, cnext, vnext, wspec]
        ins = [dy, dy, dy] + [src] * 6 + [cw]
        full = jax.ShapeDtypeStruct((T, C), F32)
        out_shape, out_specs = (full, full, dwshape), (blk, blk, dwspec)
    else:
        sprev, snext = _halo_specs(R, W, T, lambda j, i: col0 + j)
        in_specs = [blk, dprev, dnext, pl.BlockSpec((R, W), lambda j, i: (i, col0 + j)), sprev, snext,
                    pl.BlockSpec((3, W), lambda j, i: (0, col0 + j))]
        ins = [dy, dy, dy, src, src, src, cw]
        out_shape, out_specs = (jax.ShapeDtypeStruct((T, C), out_dtype), dwshape), (blk, dwspec)
    return pl.pallas_call(body, name=name, grid=(C // W, nblk), out_shape=out_shape, in_specs=in_specs,
                          out_specs=out_specs, compiler_params=_cp("parallel", "arbitrary"))(*ins)


def _pcall(body, *, name, grid, in_specs, out_specs, out_shape, args, scratch=(), sem=None, comm=None):
    cin, cspec, cout, csem = _comm_extras(comm)
    if comm is not None:
        sem = ("arbitrary",) * len(grid)
    return pl.pallas_call(
        _hosted(body, len(in_specs), len(out_specs), comm, *_grid_ends(grid)), name=name, grid=grid,
        out_shape=list(out_shape) + cout, in_specs=list(in_specs) + cspec, out_specs=list(out_specs) + cspec,
        scratch_shapes=list(scratch) + csem, compiler_params=_cp(*sem))(*args, *cin)


def ffn_mid(uf, cw, cb, nc, R, W, name, comm=None):
    T, C = uf.shape
    F = C // 2
    nblk, ncb, nj = T // R, nc // R, F // W

    def body(g_ref, v_ref, gp_ref, vp_ref, gn_ref, vn_ref, wg_ref, wv_ref, bg_ref, bv_ref, a_ref):
        i = pl.program_id(1)
        has_prev, has_next = _edge_flags(i, ncb, nblk)
        yg = _conv3(g_ref[...], gp_ref[7:8, :] * has_prev, gn_ref[0:1, :] * has_next, wg_ref) + bg_ref[...]
        yv = _conv3(v_ref[...], vp_ref[7:8, :] * has_prev, vn_ref[0:1, :] * has_next, wv_ref) + bv_ref[...]
        a_ref[...] = (yg * jax.nn.sigmoid(yg) * yv).astype(BF16)

    gblk = pl.BlockSpec((R, W), lambda j, i: (i, j))
    vblk = pl.BlockSpec((R, W), lambda j, i: (i, nj + j))
    gprev, gnext = _halo_specs(R, W, T, lambda j, i: j)
    vprev, vnext = _halo_specs(R, W, T, lambda j, i: nj + j)
    in_specs = [gblk, vblk, gprev, vprev, gnext, vnext,
                pl.BlockSpec((3, W), lambda j, i: (0, j)), pl.BlockSpec((3, W), lambda j, i: (0, nj + j)),
                pl.BlockSpec((1, W), lambda j, i: (0, j)), pl.BlockSpec((1, W), lambda j, i: (0, nj + j))]
    return _pcall(body, name=name, grid=(nj, nblk), in_specs=in_specs, out_specs=[gblk],
                  out_shape=[jax.ShapeDtypeStruct((T, F), BF16)], args=[uf] * 6 + [cw, cw, cb, cb],
                  sem=("parallel", "arbitrary"), comm=comm)


def ffn_mid_bwd(uf, cw, cb, da, nc, R, W, name, comm=None):
    T, C = uf.shape
    F = C // 2
    nblk, ncb, nj = T // R, nc // R, F // W

    def body(g_ref, v_ref, gp_ref, vp_ref, gn_ref, vn_ref, d_ref, dp_ref, dn_ref, wg_ref, wv_ref, bg_ref, bv_ref,
             dug_ref, duv_ref, dwg_ref, dwv_ref, dbg_ref, dbv_ref):
        i = pl.program_id(1)
        has_prev, has_next = _edge_flags(i, ncb, nblk)

        @pl.when(i == 0)
        def _():
            for r in (dwg_ref, dwv_ref, dbg_ref, dbv_ref):
                r[...] = jnp.zeros_like(r)

        def taps(w_ref):
            return w_ref[0:1, :], w_ref[1:2, :], w_ref[2:3, :]

        def dy_of(yg, yv, d):
            sg = jax.nn.sigmoid(yg)
            return d * yv * (sg * (1.0 + yg * (1.0 - sg))), d * (yg * sg)

        g, v, d = g_ref[...], v_ref[...], d_ref[...]
        (wg0, wg1, wg2), (wv0, wv1, wv2) = taps(wg_ref), taps(wv_ref)
        bg, bv = bg_ref[...], bv_ref[...]
        g_up, g_dn = _shift_up(g, gp_ref[7:8, :] * has_prev), _shift_dn(g, gn_ref[0:1, :] * has_next)
        v_up, v_dn = _shift_up(v, vp_ref[7:8, :] * has_prev), _shift_dn(v, vn_ref[0:1, :] * has_next)
        dyg, dyv = dy_of(wg0 * g_up + wg1 * g + wg2 * g_dn + bg, wv0 * v_up + wv1 * v + wv2 * v_dn + bv, d)
        dyg_p, dyv_p = dy_of(wg0 * gp_ref[6:7, :] + wg1 * gp_ref[7:8, :] + wg2 * g_ref[0:1, :] + bg,
                             wv0 * vp_ref[6:7, :] + wv1 * vp_ref[7:8, :] + wv2 * v_ref[0:1, :] + bv, dp_ref[7:8, :])
        dyg_n, dyv_n = dy_of(wg0 * g_ref[R - 1:R, :] + wg1 * gn_ref[0:1, :] + wg2 * gn_ref[1:2, :] + bg,
                             wv0 * v_ref[R - 1:R, :] + wv1 * vn_ref[0:1, :] + wv2 * vn_ref[1:2, :] + bv, dn_ref[0:1, :])
        dug_ref[...] = (wg0 * _shift_dn(dyg, dyg_n * has_next) + wg1 * dyg
                        + wg2 * _shift_up(dyg, dyg_p * has_prev)).astype(BF16)
        duv_ref[...] = (wv0 * _shift_dn(dyv, dyv_n * has_next) + wv1 * dyv
                        + wv2 * _shift_up(dyv, dyv_p * has_prev)).astype(BF16)
        for ref, ups, mid, dns, dy in ((dwg_ref, g_up, g, g_dn, dyg), (dwv_ref, v_up, v, v_dn, dyv)):
            ref[0:1, :] += jnp.sum(ups * dy, axis=0, keepdims=True)
            ref[1:2, :] += jnp.sum(mid * dy, axis=0, keepdims=True)
            ref[2:3, :] += jnp.sum(dns * dy, axis=0, keepdims=True)
        dbg_ref[...] += jnp.sum(dyg, axis=0, keepdims=True)
        dbv_ref[...] += jnp.sum(dyv, axis=0, keepdims=True)

    gblk = pl.BlockSpec((R, W), lambda j, i: (i, j))
    vblk = pl.BlockSpec((R, W), lambda j, i: (i, nj + j))
    gprev, gnext = _halo_specs(R, W, T, lambda j, i: j)
    vprev, vnext = _halo_specs(R, W, T, lambda j, i: nj + j)
    half = jax.ShapeDtypeStruct((T, F), BF16)
    taps8, bias1 = jax.ShapeDtypeStruct((8, F), F32), jax.ShapeDtypeStruct((1, F), F32)
    return _pcall(
        body, name=name, grid=(nj, nblk), out_shape=(half, half, taps8, taps8, bias1, bias1),
        in_specs=[gblk, vblk, gprev, vprev, gnext, vnext, gblk, gprev, gnext,
                  pl.BlockSpec((3, W), lambda j, i: (0, j)), pl.BlockSpec((3, W), lambda j, i: (0, nj + j)),
                  pl.BlockSpec((1, W), lambda j, i: (0, j)), pl.BlockSpec((1, W), lambda j, i: (0, nj + j))],
        out_specs=(gblk, gblk, pl.BlockSpec((8, W), lambda j, i: (0, j)), pl.BlockSpec((8, W), lambda j, i: (0, j)),
                   pl.BlockSpec((1, W), lambda j, i: (0, j)), pl.BlockSpec((1, W), lambda j, i: (0, j))),
        args=(uf, uf, uf, uf, uf, uf, da, da, da, cw, cw, cb, cb), sem=("parallel", "arbitrary"), comm=comm)


WEIGHTS = ("c_ctx", "w_ada", "b_ada", "ln1_w", "ln2_w", "w_in", "hg_lb_logits", "hg_norm_w", "na_q_norm_w",
           "na_k_norm_w", "na_rpb", "na_out_norm_w", "cv_w", "cv_out_norm_w", "w_out", "w_up", "ffn_conv_w",
           "ffn_conv_b", "w_down")
BIG = ("w_ada", "w_in", "w_out", "w_up", "w_down")
SHARDED_SMALL = ("hg_lb_logits", "cv_w", "ffn_conv_w")


def _flat_rows(parts, dtype):
    flat, layout, off = [], [], 0
    for p in parts:
        layout.append((off, p.shape))
        flat.append(p.reshape(-1).astype(dtype))
        off += p.size
    pad = (-off) % (8 * LANE)
    if pad:
        flat.append(jnp.zeros((pad,), dtype))
    return jnp.concatenate(flat).reshape(-1, LANE), layout


def _unflat(buf, layout):
    v = buf.reshape(-1)
    return [v[off:off + int(np.prod(shape))].reshape(shape) for off, shape in layout]


def _lb_all(logits):
    sm = jax.nn.softmax(logits.astype(F32), axis=1)
    return jnp.cumsum(sm, axis=1) - sm[:, :1]


def _seg(ctx_vec, lat_vec):
    return jnp.broadcast_to(jnp.stack([ctx_vec, lat_vec])[:, None, :], (2, RT, ctx_vec.shape[0]))


def _shared(vec):
    return jnp.broadcast_to(vec[None, None, :], (1, RT, vec.shape[0]))


def kernel(x, c, ctx, c_ctx, w_ada, b_ada, ln1_w, ln2_w, w_in, hg_lb_logits, hg_norm_w, na_q_norm_w, na_k_norm_w, na_rpb, na_out_norm_w, cv_w, cv_out_norm_w, w_out, w_up, ffn_conv_w, ffn_conv_b, w_down, loss_target, m_c_ctx, m_w_ada, m_b_ada, m_ln1_w, m_ln2_w, m_w_in, m_hg_lb_logits, m_hg_norm_w, m_na_q_norm_w, m_na_k_norm_w, m_na_rpb, m_na_out_norm_w, m_cv_w, m_cv_out_norm_w, m_w_out, m_w_up, m_ffn_conv_w, m_ffn_conv_b, m_w_down, v_c_ctx, v_w_ada, v_b_ada, v_ln1_w, v_ln2_w, v_w_in, v_hg_lb_logits, v_hg_norm_w, v_na_q_norm_w, v_na_k_norm_w, v_na_rpb, v_na_out_norm_w, v_cv_w, v_cv_out_norm_w, v_w_out, v_w_up, v_ffn_conv_w, v_ffn_conv_b, v_w_down):
    W = dict(c_ctx=c_ctx, w_ada=w_ada, b_ada=b_ada, ln1_w=ln1_w, ln2_w=ln2_w, w_in=w_in, hg_lb_logits=hg_lb_logits,
             hg_norm_w=hg_norm_w, na_q_norm_w=na_q_norm_w, na_k_norm_w=na_k_norm_w, na_rpb=na_rpb,
             na_out_norm_w=na_out_norm_w, cv_w=cv_w, cv_out_norm_w=cv_out_norm_w, w_out=w_out, w_up=w_up,
             ffn_conv_w=ffn_conv_w, ffn_conv_b=ffn_conv_b, w_down=w_down)
    Mo = dict(c_ctx=m_c_ctx, w_ada=m_w_ada, b_ada=m_b_ada, ln1_w=m_ln1_w, ln2_w=m_ln2_w, w_in=m_w_in,
              hg_lb_logits=m_hg_lb_logits, hg_norm_w=m_hg_norm_w, na_q_norm_w=m_na_q_norm_w,
              na_k_norm_w=m_na_k_norm_w, na_rpb=m_na_rpb, na_out_norm_w=m_na_out_norm_w, cv_w=m_cv_w,
              cv_out_norm_w=m_cv_out_norm_w, w_out=m_w_out, w_up=m_w_up, ffn_conv_w=m_ffn_conv_w,
              ffn_conv_b=m_ffn_conv_b, w_down=m_w_down)
    Vo = dict(c_ctx=v_c_ctx, w_ada=v_w_ada, b_ada=v_b_ada, ln1_w=v_ln1_w, ln2_w=v_ln2_w, w_in=v_w_in,
              hg_lb_logits=v_hg_lb_logits, hg_norm_w=v_hg_norm_w, na_q_norm_w=v_na_q_norm_w,
              na_k_norm_w=v_na_k_norm_w, na_rpb=v_na_rpb, na_out_norm_w=v_na_out_norm_w, cv_w=v_cv_w,
              cv_out_norm_w=v_cv_out_norm_w, w_out=v_w_out, w_up=v_w_up, ffn_conv_w=v_ffn_conv_w,
              ffn_conv_b=v_ffn_conv_b, w_down=v_w_down)

    xi, yi, ci = _me()
    chip = 2 * xi + yi
    dev = 2 * chip + ci
    L, D = x.shape[1], x.shape[2]
    NC = ctx.shape[1]
    T = NC + L
    depth = w_in.shape[0]
    HGW, NAW, CVW = 4 * hg_lb_logits.shape[-1], na_out_norm_w.shape[-1], cv_out_norm_w.shape[-1]
    MIX = HGW + NAW + CVW
    INW, FF2 = 4 * w_in.shape[-1], 4 * w_up.shape[-1]
    F = FF2 // 2
    ADA = 4 * w_ada.shape[-1]
    assert NAW == 2 * HGW and INW == 5 * HGW + 3 * NAW + 3 * CVW and ADA == 6 * D and NC % 128 == 0
    assert L % GRID_W == 0 and T % CHUNK == 0 and depth == 2
    R = math.gcd(NC, 256)
    FW = 512 if F % 512 == 0 else LANE
    rows = L // GRID_W
    nh_hg, nh_na, nh_cv = HGW // LANE, NAW // LANE, CVW // LANE
    kcol = 3 * nh_hg
    vcol = kcol + nh_na
    gcol = vcol + nh_na + nh_hg
    qcol = gcol + nh_hg
    bcol = qcol + nh_na
    mix_na, mix_cv = nh_hg, nh_hg + nh_na

    small1, lay1 = _flat_rows([c[0], hg_lb_logits, cv_w, ffn_conv_w], F32)
    g1 = allgather8([small1], "gather_cond")[0]
    per_dev = [_unflat(g1[d], lay1) for d in range(8)]
    c_all = jnp.stack([p[0] for p in per_dev])
    lb_logits = jnp.concatenate([per_dev[2 * s][1] for s in range(4)], axis=-1)
    cvw_full = jnp.concatenate([per_dev[2 * s][2] for s in range(4)], axis=-1)
    fcw_full = jnp.concatenate([per_dev[2 * s][3] for s in range(4)], axis=-1)
    lb_all, lb_pull = jax.vjp(_lb_all, lb_logits)

    a16 = jnp.concatenate([c_all, c_ctx[None], jnp.zeros((7, D), F32)])
    s16 = _silu(a16)
    wcols = ADA // 4
    b_mine = lax.dynamic_slice_in_dim(b_ada, chip * wcols, wcols, axis=1)
    p_ada = mm_nn(s16, w_ada, F32, "ada_fwd").reshape(16, depth, wcols).transpose(1, 0, 2) + b_mine[:, None, :]
    g2 = allgather8([p_ada.reshape(depth * 16, wcols)], "gather_ada")[0].reshape(8, depth, 16, wcols)
    ada_rows = jnp.concatenate([g2[2 * s] for s in range(4)], axis=-1)
    ada = lax.dynamic_index_in_dim(ada_rows, dev, axis=1, keepdims=False)
    ada_c = ada_rows[:, 8]

    def half_rows(a):
        h = a.shape[0] // 2
        return lax.dynamic_slice_in_dim(a, ci * h, h, axis=0)

    proj = ("w_in", "w_out", "w_up", "w_down")
    wparts = [{n: half_rows(W[n][l]).astype(BF16) for n in proj} for l in range(depth)]

    def stacked(n, g):
        g = g.reshape(4, -1, g.shape[-1])
        return g.reshape(1, -1, g.shape[-1]) if n in ("w_out", "w_down") else g

    Wg = [{}, {}]
    fetch_plan = {"norm_in": [(0, "w_in")], "proj_in_0": [(0, "w_out"), (0, "w_down")], "na_fwd_0": [(0, "w_up")],
                  "ffn_up_0": [(1, "w_up")], "ffn_mid_0": [(1, "w_in"), (1, "w_out")], "ffn_down_0": [(1, "w_down")]}

    def fetching(name, call):
        items = fetch_plan.get(name, [])
        res = call(allgather8_comm([wparts[l][n] for l, n in items]) if items else None)
        res = list(res) if isinstance(res, (list, tuple)) else [res]
        for (l, n), g in zip(items, res[len(res) - len(items):]):
            Wg[l][n] = stacked(n, g)
        own = res[:len(res) - len(items)]
        return own[0] if len(own) == 1 else own

    xcat = jnp.concatenate([ctx[0], x[0]], axis=0)
    mods = []
    for l in range(depth):
        lat, con = jnp.split(ada[l], 6), jnp.split(ada_c[l], 6)
        mods.append(dict(sh1=_seg(con[0], lat[0]), sc1=_seg(con[1], lat[1]), g1=_seg(con[2], lat[2]),
                         sh2=_seg(con[3], lat[3]), sc2=_seg(con[4], lat[4]), g2=_seg(con[5], lat[5]),
                         ln1=_shared(ln1_w[l]), ln2=_shared(ln2_w[l])))
    bias_pull, saved = [], []
    x0 = xcat
    h = fetching("norm_in", lambda cm: gate_norm(x0, None, None, mods[0]["ln1"], mods[0]["sh1"], mods[0]["sc1"], NC, R,
                                                 "norm_in", comm=cm)[1:])
    for l in range(depth):
        md, wl = mods[l], Wg[l]
        u = fetching(f"proj_in_{l}", lambda cm: mm_nn(h, wl["w_in"], F32, f"proj_in_{l}", comm=cm))
        lbf, lbb = lb_all[0, l][None], lb_all[1, l][None]
        o_fw, st_fw = hgrn_fwd(u, lbf, False, 0, NC, HGW, f"hgrn_fw_{l}")
        o_bw, st_bw = hgrn_fwd(u, lbb, True, 1, NC, HGW, f"hgrn_bw_{l}")
        hgn = hg_norm_w[l][None]
        hg = hg_read(o_fw, o_bw, u, hgn, gcol, R, f"hg_read_{l}")
        bias, pull = jax.vjp(lambda r: na_bias_tables(r, rows), na_rpb[l])
        bias_pull.append(pull)
        qn, kn, on = na_q_norm_w[l][None], na_k_norm_w[l][None], na_out_norm_w[l][None]
        keys_n, vals_b = kv_prep(u, kn, kcol, vcol, NAW, R, f"kv_prep_{l}")
        na = fetching(f"na_fwd_{l}", lambda cm: na_fwd(u, keys_n, vals_b, qn, on, bias, qcol, NC, f"na_fwd_{l}", comm=cm))
        cvw_l, cvo = cvw_full[l], cv_out_norm_w[l][None]
        cv = short_conv(u, cvw_l, cvo, bcol, NC, CVW, R, name=f"short_conv_{l}")
        mix = jnp.concatenate([hg, na, cv], axis=1)
        m1 = mm_nn(mix, wl["w_out"], F32, f"proj_out_{l}")
        x1, h2 = gate_norm(x0, m1, md["g1"], md["ln2"], md["sh2"], md["sc2"], NC, R, f"gate_norm_mid_{l}")
        uf = fetching(f"ffn_up_{l}", lambda cm: mm_nn(h2, wl["w_up"], F32, f"ffn_up_{l}", comm=cm))
        fcw_l, fcb_l = fcw_full[l], ffn_conv_b[l][None]
        a = fetching(f"ffn_mid_{l}", lambda cm: ffn_mid(uf, fcw_l, fcb_l, NC, R, FW, f"ffn_mid_{l}", comm=cm))
        m2 = fetching(f"ffn_down_{l}", lambda cm: mm_nn(a, wl["w_down"], F32, f"ffn_down_{l}", comm=cm))
        saved.append(dict(x0=x0, h=h, u=u, o_fw=o_fw, o_bw=o_bw, st_fw=st_fw, st_bw=st_bw, bias=bias, mix=mix,
                          m1=m1, x1=x1, h2=h2, uf=uf, a=a, m2=m2, lbf=lbf, lbb=lbb, keys_n=keys_n, vals_b=vals_b))
        if l + 1 < depth:
            nx = mods[l + 1]
            x0, h = gate_norm(x1, m2, md["g2"], nx["ln1"], nx["sh1"], nx["sc1"], NC, R, f"gate_norm_end_{l}")
    sv, md = saved[-1], mods[-1]
    loss_terms, d_x1, d_m2, d_g2 = gate_loss(sv["x1"], sv["m2"], md["g2"], loss_target[0], NC, R, "gate_loss")
    loss = lax.psum(jnp.sum(loss_terms), ("x", "y", "c"))

    big_grads = [dict() for _ in range(depth)]
    core = ci.astype(jnp.int32).reshape(1)
    pairs, quads = {}, {}
    reduce_plan = {"ffn_down_bwd_0": [(1, "w_down")], "ffn_mid_bwd_0": [(1, "w_up")],
                   "ffn_up_bwd_0": [(1, "w_in"), (1, "w_out")], "na_bwd_0": [(0, "w_down"), (0, "w_up"), (0, "w_out")],
                   "proj_in_bwd_0": [(0, "w_in")]}

    def swapping(l, names, call):
        parts = [big_grads[l][n] for n in names]
        res = list(call(swap_halves_comm(parts)))
        for n, p, g in zip(names, parts, res[len(res) - len(names):]):
            pairs[(l, n)] = pair_sum(p, g, core, f"reduce_pair_sum_{n}_{l}")
        own = res[:len(res) - len(names)]
        return own[0] if len(own) == 1 else own

    def stage(l, names):
        swapping(l, names, lambda cm: _run_comm(cm, f"reduce_sibling_{l}_{names[0]}"))

    def reducing(name, call):
        items = reduce_plan.get(name, [])
        res = call(chip_alltoall_comm([pairs[k] for k in items]) if items else None)
        res = list(res) if isinstance(res, (list, tuple)) else [res]
        for k, q in zip(items, res[len(res) - len(items):]):
            quads[k] = q
        own = res[:len(res) - len(items)]
        return own[0] if len(own) == 1 else own

    small = [dict() for _ in range(depth)]
    d_ada = [None] * depth
    d_lb = [None] * depth
    for l in reversed(range(depth)):
        sv, md, wl = saved[l], mods[l], Wg[l]
        u, uf = sv["u"], sv["uf"]
        big_grads[l]["w_down"] = mm_tn(sv["a"], d_m2, 1, BF16, f"grad_w_down_{l}").reshape(4, F // 4, D)
        d_a = reducing(f"ffn_down_bwd_{l}", lambda cm: mm_nt(d_m2, wl["w_down"], F32, f"ffn_down_bwd_{l}", comm=cm))
        fcw_l, fcb_l = fcw_full[l], ffn_conv_b[l][None]
        dug, duv, dwg, dwv, dbg, dbv = reducing(
            f"ffn_mid_bwd_{l}", lambda cm: ffn_mid_bwd(uf, fcw_l, fcb_l, d_a, NC, R, FW, f"ffn_mid_bwd_{l}", comm=cm))
        d_uf = (dug, duv)
        small[l]["ffn_conv_w"] = jnp.concatenate([dwg[:3], dwv[:3]], axis=1)
        small[l]["ffn_conv_b"] = jnp.concatenate([dbg[0], dbv[0]])
        big_grads[l]["w_up"] = mm_tn(sv["h2"], d_uf, 4, BF16, f"grad_w_up_{l}")
        d_h2 = reducing(f"ffn_up_bwd_{l}", lambda cm: mm_nt(d_uf, wl["w_up"], F32, f"ffn_up_bwd_{l}", comm=cm))
        d_x0, d_m1, dg1, dln2, dsh2, dsc2 = gate_norm_bwd(sv["x0"], sv["m1"], md["g1"], md["ln2"], md["sh2"], md["sc2"],
                                                          d_x1, d_h2, NC, R, f"gate_norm_mid_bwd_{l}")
        big_grads[l]["w_out"] = mm_tn(sv["mix"], d_m1, 1, BF16, f"grad_w_out_{l}").reshape(4, MIX // 4, D)
        if l == 0:
            d_mix = swapping(0, ("w_down", "w_up", "w_out"),
                             lambda cm: mm_nt(d_m1, wl["w_out"], F32, f"proj_out_bwd_{l}", comm=cm))
        else:
            d_mix = mm_nt(d_m1, wl["w_out"], F32, f"proj_out_bwd_{l}")
        hgn = hg_norm_w[l][None]
        d_o, d_hgg, d_hgn = hg_read_bwd(sv["o_fw"], sv["o_bw"], u, hgn, d_mix, gcol, 0, R, f"hg_read_bwd_{l}")
        dzf, dvf, dqf, dlbf = hgrn_bwd(u, sv["lbf"], sv["st_fw"], d_o, False, 0, NC, HGW, f"hgrn_fw_bwd_{l}")
        dzb, dvb, dqb, dlbb = hgrn_bwd(u, sv["lbb"], sv["st_bw"], d_o, True, 1, NC, HGW, f"hgrn_bw_bwd_{l}")
        d_lb[l] = (dlbf[0], dlbb[0])
        qn, kn, on = na_q_norm_w[l][None], na_k_norm_w[l][None], na_out_norm_w[l][None]
        d_nq, d_keys_n, d_nv, d_bias, d_qn, d_on = reducing(
            f"na_bwd_{l}", lambda cm: na_bwd(u, sv["keys_n"], sv["vals_b"], qn, on, sv["bias"], d_mix, qcol, mix_na, NC,
                                            f"na_bwd_{l}", comm=cm))
        d_nk, d_kn = kv_prep_bwd(u, kn, d_keys_n, kcol, NAW, R, f"kv_prep_bwd_{l}")
        cvw_l, cvo = cvw_full[l], cv_out_norm_w[l][None]
        d_cb, d_cy, d_cvo = short_conv(u, cvw_l, cvo, bcol, NC, CVW, R, bwd_dout=d_mix, ocol=mix_cv,
                                       name=f"short_conv_bwd_{l}")
        gcv = _head_group(nh_cv, bcol + nh_cv, bcol + 2 * nh_cv)
        d_cc, d_cvv, d_cvw = conv3_bwd(d_cy, u, cvw_l, NC, R, gcv * LANE,
                                       prod_cols=((bcol + nh_cv) // gcv, (bcol + 2 * nh_cv) // gcv),
                                       name=f"short_conv_taps_bwd_{l}")
        d_u = jnp.concatenate([dzf, dzb, dvf + dvb, d_nk, d_nv, dqf + dqb, d_hgg, d_nq, d_cb, d_cc, d_cvv],
                              axis=1).astype(BF16)
        big_grads[l]["w_in"] = mm_tn(sv["h"], d_u, 4, BF16, f"grad_w_in_{l}")
        if l > 0:
            d_h = swapping(l, proj, lambda cm: mm_nt(d_u, wl["w_in"], F32, f"proj_in_bwd_{l}", comm=cm))
        else:
            stage(0, ("w_in",))
            d_h = reducing(f"proj_in_bwd_{l}", lambda cm: mm_nt(d_u, wl["w_in"], F32, f"proj_in_bwd_{l}", comm=cm))
        small[l].update(hg_norm_w=d_hgn.sum(0)[0], na_q_norm_w=d_qn.sum(0)[0], na_k_norm_w=d_kn.sum(0)[0],
                        na_out_norm_w=d_on.reshape(-1), na_rpb=bias_pull[l](d_bias)[0], cv_w=d_cvw[:3],
                        cv_out_norm_w=d_cvo.reshape(-1), ln2_w=dln2.sum((0, 1)))
        if l > 0:
            pv, pm = saved[l - 1], mods[l - 1]
            d_x1, d_m2, dg2_prev, dln1, dsh1, dsc1 = gate_norm_bwd(pv["x1"], pv["m2"], pm["g2"], md["ln1"], md["sh1"],
                                                                   md["sc1"], d_x0, d_h, NC, R, f"gate_norm_end_bwd_{l - 1}")
        else:
            d_xin, _, _, dln1, dsh1, dsc1 = gate_norm_bwd(sv["x0"], None, None, md["ln1"], md["sh1"], md["sc1"], d_x0, d_h,
                                                          NC, R, "norm_in_bwd")
        small[l]["ln1_w"] = dln1.sum((0, 1))
        this_g2 = d_g2
        vecs = [v.sum(1) for v in (dsh1, dsc1, dg1, dsh2, dsc2, this_g2)]
        d_ada[l] = jnp.stack([jnp.concatenate([v[s] for v in vecs]) for s in (0, 1)])
        if l > 0:
            d_g2 = dg2_prev
    grad_x = d_xin[NC:][None]
    d_logits = lb_pull(jnp.stack([jnp.stack([d_lb[l][k] for l in range(depth)]) for k in (0, 1)]))[0]

    rep_names = ("ln1_w", "ln2_w", "hg_norm_w", "na_q_norm_w", "na_k_norm_w", "na_rpb", "na_out_norm_w",
                 "cv_out_norm_w", "ffn_conv_b", "cv_w", "ffn_conv_w")
    parts3 = [jnp.stack([small[l][n] for l in range(depth)]) for n in rep_names]
    parts3 += [d_logits, jnp.stack([d_ada[l][0] for l in range(depth)]), jnp.stack([d_ada[l][1] for l in range(depth)])]
    buf3, lay3 = _flat_rows(parts3, F32)
    g3 = allgather8([buf3], "gather_small_grads")[0]
    tot3 = _unflat(sum_leading(g3, F32, "sum_small_grads"), lay3)
    gsm = dict(zip(rep_names, tot3[:len(rep_names)]))
    gsm["hg_lb_logits"] = tot3[len(rep_names)]
    dctx_tot, dlat_tot = tot3[-2], tot3[-1]
    dlat_each = jnp.stack([_unflat(g3[d], lay3)[-1] for d in range(8)], axis=1)
    grads = {n: gsm[n].reshape(W[n].shape) for n in rep_names if n not in SHARDED_SMALL}
    for n in SHARDED_SMALL:
        wl_ = W[n].shape[-1]
        grads[n] = lax.dynamic_slice_in_dim(gsm[n], chip * wl_, wl_, axis=gsm[n].ndim - 1)
    grads["b_ada"] = dctx_tot + dlat_tot

    gw_ada, dms = [], []
    for l in range(depth):
        dm = jnp.concatenate([dlat_each[l], dctx_tot[l][None], jnp.zeros((7, ADA), F32)])
        dms.append(lax.dynamic_slice_in_dim(dm, chip * wcols, wcols, axis=1))
        gw_ada.append(mm_tn(s16, dms[l], 1, F32, f"grad_w_ada_{l}")[0])
    ds16 = mm_nt(jnp.concatenate(dms, axis=1), w_ada, F32, "ada_bwd")
    g4 = allgather8([ds16[8:16]], "gather_cond_grad")[0]
    d_scc = g4[0, 0] + g4[2, 0] + g4[4, 0] + g4[6, 0]
    sg = jax.nn.sigmoid(c_ctx)
    grads["c_ctx"] = d_scc * (sg * (1.0 + c_ctx * (1.0 - sg)))

    keys = [(l, n) for l in range(depth) for n in proj]
    mine = [sum_leading(quads[(l, n)], F32, f"reduce_chip_sum_{n}_{l}") for l, n in keys]

    delta, new_m, new_v = {}, {}, {}
    smalls = [n for n in WEIGHTS if n not in BIG]
    pw, lay_s = _flat_rows([W[n] for n in smalls], F32)
    pg, _ = _flat_rows([grads[n] for n in smalls], F32)
    pm, _ = _flat_rows([Mo[n] for n in smalls], F32)
    pvv, _ = _flat_rows([Vo[n] for n in smalls], F32)
    _, d_, m_, v_, *other = adamw(pw, [pg], pm, pvv, "adamw_small", comm=share_halves_comm(mine))
    for n, dd, mm_, vv in zip(smalls, _unflat(d_, lay_s), _unflat(m_, lay_s), _unflat(v_, lay_s)):
        delta[n], new_m[n], new_v[n] = dd, mm_, vv
    mine_by, other_by = {k: [None] * depth for k in proj}, {k: [None] * depth for k in proj}
    for (l, k), a, b in zip(keys, mine, other):
        mine_by[k][l], other_by[k][l] = a, b
    for n in BIG:
        shp = W[n].shape
        two = lambda a: a.reshape(-1, shp[-1])
        if n == "w_ada":
            g_, d_, m_, v_ = adamw(two(W[n]), gw_ada, two(Mo[n]), two(Vo[n]), f"adamw_{n}")
        else:
            g_, d_, m_, v_ = adamw_halves(two(W[n]), mine_by[n], other_by[n], two(Mo[n]), two(Vo[n]), core, f"adamw_{n}")
        grads[n], delta[n], new_m[n], new_v[n] = g_.reshape(shp), d_.reshape(shp), m_.reshape(shp), v_.reshape(shp)

    return (loss, grad_x, *[grads[n] for n in WEIGHTS], *[delta[n] for n in WEIGHTS],
            *[new_m[n] for n in WEIGHTS], *[new_v[n] for n in WEIGHTS])
```

```python
import functools
import math

import numpy as np
import jax
import jax.numpy as jnp
from jax import lax
from jax.experimental import pallas as pl
from jax.experimental.pallas import tpu as pltpu

F32 = jnp.float32
BF16 = jnp.bfloat16
MESH = pl.DeviceIdType.MESH
ANY = pl.BlockSpec(memory_space=pl.ANY)
VMEM_SPEC = pl.BlockSpec(memory_space=pltpu.VMEM)

LANE = 128
CHUNK = 64
SUB = 16
GRID_W = 64
WIN_R = 8
WIN_C = 16
EPS = 1e-6
F_FLOOR = 1e-30
NEG_INF = -1e30
EXP_CLAMP = 80.0
ATTN_SCALE = LANE ** -0.5
VMEM_LIMIT = 56 * 1024 * 1024
ADAM_LR, ADAM_B1, ADAM_B2, ADAM_EPS, ADAM_WD, ADAM_STEP = 0.001, 0.9, 0.999, 1e-08, 0.01, 10


def _cp(*sem):
    return pltpu.CompilerParams(dimension_semantics=sem or None, vmem_limit_bytes=VMEM_LIMIT)


def _me():
    return lax.axis_index("x"), lax.axis_index("y"), lax.axis_index("c")


def allgather8(blocks, name, hbm=False):
    na = len(blocks)
    comm = allgather8_comm(blocks)

    def body(*refs):
        comm["start"](refs[:na], refs[na:2 * na], refs[2 * na:])
        comm["finish"](refs[:na], refs[na:2 * na], refs[2 * na:])

    spec = ANY if hbm else VMEM_SPEC
    return pl.pallas_call(
        body, name=name, out_shape=comm["outs"], in_specs=[spec] * na, out_specs=[spec] * na,
        scratch_shapes=comm["scratch"], compiler_params=pltpu.CompilerParams(vmem_limit_bytes=VMEM_LIMIT),
    )(*blocks)


def allgather8_comm(blocks):
    na = len(blocks)

    def parts(x_refs, out_refs, sems):
        send_sems, recv_sems, local_sems = sems
        x, y, c = _me()
        me, sibling = (x, y, c), (x, y, 1 - c)
        chips = [(1 - x, y), (x, 1 - y), (1 - x, 1 - y)]

        def rows(a, px, py, pc):
            return out_refs[a].at[4 * px + 2 * py + pc]

        def copy(a, k, blk, to, src=None):
            return pltpu.make_async_remote_copy(
                src_ref=rows(a, *blk) if src is None else src, dst_ref=rows(a, *blk),
                send_sem=send_sems.at[a, k], recv_sem=recv_sems.at[a, k], device_id=to, device_id_type=MESH)

        mine = [pltpu.make_async_copy(x_refs[a], rows(a, *me), local_sems.at[a]) for a in range(na)]
        first = []
        for a in range(na):
            first.append(copy(a, 0, me, sibling, src=x_refs[a]))
            first += [copy(a, 1 + j, me, (*chip, c), src=x_refs[a]) for j, chip in enumerate(chips)]
        return c, me, sibling, chips, copy, mine, first

    def start(x_refs, out_refs, sems):
        _, _, _, _, _, mine, first = parts(x_refs, out_refs, sems)
        for cp in mine + first:
            cp.start()

    def finish(x_refs, out_refs, sems):
        c, me, sibling, chips, copy, mine, first = parts(x_refs, out_refs, sems)
        passed = []
        for j, chip in enumerate(chips):
            for a in range(na):
                copy(a, 1 + j, (*chip, c), me).wait_recv()
                passed.append(copy(a, 4 + j, (*chip, c), sibling))
                passed[-1].start()
        for a in range(na):
            copy(a, 0, sibling, me).wait_recv()
            for j, chip in enumerate(chips):
                copy(a, 4 + j, (*chip, 1 - c), me).wait_recv()
        for cp in first + passed:
            cp.wait_send()
        for cp in mine:
            cp.wait()

    return dict(ins=list(blocks), outs=[jax.ShapeDtypeStruct((8,) + b.shape, b.dtype) for b in blocks],
                scratch=[pltpu.SemaphoreType.DMA((na, 7)), pltpu.SemaphoreType.DMA((na, 7)),
                         pltpu.SemaphoreType.DMA((na,))], start=start, finish=finish)


def _run_comm(comm, name):
    na = len(comm["ins"])

    def body(*refs):
        comm["start"](refs[:na], refs[na:2 * na], refs[2 * na:])
        comm["finish"](refs[:na], refs[na:2 * na], refs[2 * na:])

    return pl.pallas_call(body, name=name, out_shape=comm["outs"], in_specs=[ANY] * na, out_specs=[ANY] * na,
                          scratch_shapes=comm["scratch"])(*comm["ins"])


def swap_halves_comm(gs):
    na = len(gs)
    hrs = [g.shape[1] // 2 for g in gs]

    def copies(g_refs, o_refs, sems):
        send_sems, recv_sems = sems
        x, y, c = _me()
        cps = []
        for a in range(na):
            for s in range(4):
                src = g_refs[a].at[s, pl.ds(pl.multiple_of((1 - c) * hrs[a], 16), hrs[a]), :]
                cps.append(pltpu.make_async_remote_copy(
                    src_ref=src, dst_ref=o_refs[a].at[s], send_sem=send_sems.at[a, s], recv_sem=recv_sems.at[a, s],
                    device_id=(x, y, 1 - c), device_id_type=MESH))
        return cps

    def start(g_refs, o_refs, sems):
        for cp in copies(g_refs, o_refs, sems):
            cp.start()

    def finish(g_refs, o_refs, sems):
        for cp in copies(g_refs, o_refs, sems):
            cp.wait()

    return dict(ins=list(gs), outs=[jax.ShapeDtypeStruct((4, hrs[a], gs[a].shape[2]), gs[a].dtype) for a in range(na)],
                scratch=[pltpu.SemaphoreType.DMA((na, 4)), pltpu.SemaphoreType.DMA((na, 4))], start=start, finish=finish)


def chip_alltoall_comm(gs):
    na = len(gs)

    def copies(g_refs, o_refs, sems):
        send_sems, recv_sems, local_sems = sems
        x, y, c = _me()
        mine = 2 * x + y
        cps = []
        for a in range(na):
            cps.append(pltpu.make_async_copy(g_refs[a].at[mine], o_refs[a].at[mine], local_sems.at[a]))
            for k, (px, py) in enumerate([(1 - x, y), (x, 1 - y), (1 - x, 1 - y)]):
                cps.append(pltpu.make_async_remote_copy(
                    src_ref=g_refs[a].at[2 * px + py], dst_ref=o_refs[a].at[mine], send_sem=send_sems.at[a, k],
                    recv_sem=recv_sems.at[a, k], device_id=(px, py, c), device_id_type=MESH))
        return cps

    def start(g_refs, o_refs, sems):
        for cp in copies(g_refs, o_refs, sems):
            cp.start()

    def finish(g_refs, o_refs, sems):
        for cp in copies(g_refs, o_refs, sems):
            cp.wait()

    return dict(ins=list(gs), outs=[jax.ShapeDtypeStruct(g.shape, g.dtype) for g in gs],
                scratch=[pltpu.SemaphoreType.DMA((na, 3)), pltpu.SemaphoreType.DMA((na, 3)),
                         pltpu.SemaphoreType.DMA((na,))], start=start, finish=finish)


def share_halves_comm(vs):
    na = len(vs)

    def copies(v_refs, o_refs, sems):
        send_sems, recv_sems = sems
        x, y, c = _me()
        return [pltpu.make_async_remote_copy(
            src_ref=v_refs[a], dst_ref=o_refs[a], send_sem=send_sems.at[a], recv_sem=recv_sems.at[a],
            device_id=(x, y, 1 - c), device_id_type=MESH) for a in range(na)]

    def start(v_refs, o_refs, sems):
        for cp in copies(v_refs, o_refs, sems):
            cp.start()

    def finish(v_refs, o_refs, sems):
        for cp in copies(v_refs, o_refs, sems):
            cp.wait()

    return dict(ins=list(vs), outs=[jax.ShapeDtypeStruct(v.shape, v.dtype) for v in vs],
                scratch=[pltpu.SemaphoreType.DMA((na,)), pltpu.SemaphoreType.DMA((na,))], start=start, finish=finish)


def _row_block(rows, cap):
    rb = math.gcd(rows, cap)
    return rb if rb % 8 == 0 else rows


def sum_leading(x, out_dtype, name):
    n, r, c = x.shape
    rb = _row_block(r, 1024)

    def body(x_ref, o_ref):
        acc = x_ref[0].astype(F32)
        for k in range(1, n):
            acc = acc + x_ref[k].astype(F32)
        o_ref[...] = acc.astype(o_ref.dtype)

    return pl.pallas_call(
        body, name=name, grid=(r // rb,), out_shape=jax.ShapeDtypeStruct((r, c), out_dtype),
        in_specs=[pl.BlockSpec((n, rb, c), lambda i: (0, i, 0))], out_specs=pl.BlockSpec((rb, c), lambda i: (i, 0)),
        compiler_params=_cp("parallel"),
    )(x)


def pair_sum(g, got, core, name):
    _, r2, n = g.shape
    hr = r2 // 2
    rb = math.gcd(hr, 512)
    nb = hr // rb

    def body(c_ref, a_ref, b_ref, o_ref):
        o_ref[...] = (a_ref[...].astype(F32) + b_ref[...].astype(F32)).astype(o_ref.dtype)

    spec = pl.BlockSpec((None, rb, n), lambda s, i, c_ref: (s, i, 0))
    return pl.pallas_call(
        body, name=name, out_shape=jax.ShapeDtypeStruct((4, hr, n), g.dtype),
        grid_spec=pltpu.PrefetchScalarGridSpec(
            num_scalar_prefetch=1, grid=(4, nb),
            in_specs=[pl.BlockSpec((None, rb, n), lambda s, i, c_ref: (s, c_ref[0] * nb + i, 0)), spec],
            out_specs=spec),
        compiler_params=_cp("parallel", "parallel"),
    )(core, g, got)


def adamw(w, gs, m, v, name, comm=None):
    rows, c = w.shape
    ng = len(gs)
    r = rows // ng
    rb = _row_block(r, 128 if c > 2048 else 256 if c > 1024 else 1024)
    nb = r // rb
    bc1 = 1.0 - ADAM_B1 ** ADAM_STEP
    bc2 = 1.0 - ADAM_B2 ** ADAM_STEP

    def body(w_ref, *refs):
        g_refs, (m_ref, v_ref, g_out, d_ref, nm_ref, nv_ref) = refs[:ng], refs[ng:]
        part = pl.program_id(0) // nb
        gg = g_refs[0][...]
        for k in range(1, ng):
            gg = jnp.where(part == k, g_refs[k][...], gg)
        nm = ADAM_B1 * m_ref[...] + (1.0 - ADAM_B1) * gg
        nv = ADAM_B2 * v_ref[...] + (1.0 - ADAM_B2) * (gg * gg)
        g_out[...] = gg
        d_ref[...] = -ADAM_LR * ((nm / bc1) / (jnp.sqrt(nv / bc2) + ADAM_EPS) + ADAM_WD * w_ref[...])
        nm_ref[...] = nm
        nv_ref[...] = nv

    spec = pl.BlockSpec((rb, c), lambda i: (i, 0))
    gspecs = [pl.BlockSpec((rb, c), functools.partial(lambda k, i: (jnp.clip(i - k * nb, 0, nb - 1), 0), k))
              for k in range(ng)]
    sds = jax.ShapeDtypeStruct((rows, c), F32)
    return _pcall(body, name=name, grid=(ng * nb,), out_shape=(sds,) * 4, in_specs=[spec] + gspecs + [spec, spec],
                  out_specs=(spec,) * 4, args=(w, *gs, m, v), sem=("parallel",), comm=comm)


def adamw_halves(w, mine, other, m, v, core, name):
    rows, c = w.shape
    nl = len(mine)
    hr = rows // (2 * nl)
    rb = _row_block(hr, 128 if c > 2048 else 256 if c > 1024 else 1024)
    nb = hr // rb
    bc1 = 1.0 - ADAM_B1 ** ADAM_STEP
    bc2 = 1.0 - ADAM_B2 ** ADAM_STEP

    def body(c_ref, w_ref, *refs):
        mine_refs, other_refs = refs[:nl], refs[nl:2 * nl]
        m_ref, v_ref, g_out, d_ref, nm_ref, nv_ref = refs[2 * nl:]
        part = pl.program_id(0) // nb
        layer, half = part // 2, part % 2
        gg = jnp.where(half == c_ref[0], mine_refs[0][...], other_refs[0][...])
        for k in range(1, nl):
            gg = jnp.where(layer == k, jnp.where(half == c_ref[0], mine_refs[k][...], other_refs[k][...]), gg)
        nm = ADAM_B1 * m_ref[...] + (1.0 - ADAM_B1) * gg
        nv = ADAM_B2 * v_ref[...] + (1.0 - ADAM_B2) * (gg * gg)
        g_out[...] = gg
        d_ref[...] = -ADAM_LR * ((nm / bc1) / (jnp.sqrt(nv / bc2) + ADAM_EPS) + ADAM_WD * w_ref[...])
        nm_ref[...] = nm
        nv_ref[...] = nv

    spec = pl.BlockSpec((rb, c), lambda i, c_ref: (i, 0))
    gspecs = [pl.BlockSpec((rb, c), functools.partial(
        lambda k, i, c_ref: (jnp.clip(i - 2 * k * nb, 0, 2 * nb - 1) % nb, 0), k)) for k in range(nl)]
    sds = jax.ShapeDtypeStruct((rows, c), F32)
    return pl.pallas_call(
        body, name=name, out_shape=(sds,) * 4,
        grid_spec=pltpu.PrefetchScalarGridSpec(
            num_scalar_prefetch=1, grid=(2 * nl * nb,), in_specs=[spec] + gspecs + gspecs + [spec, spec],
            out_specs=(spec,) * 4),
        compiler_params=_cp("parallel"),
    )(core, w, *mine, *other, m, v)


def _pick(n, prefs):
    for p in prefs:
        if n % p == 0:
            return p
    return n


def _hosted(body, n_in, n_out, comm, first, last):
    if comm is None:
        return body
    k, ns = len(comm["ins"]), len(comm["scratch"])

    def wrapped(*refs):
        ins, cins = refs[:n_in], refs[n_in:n_in + k]
        outs, couts = refs[n_in + k:n_in + k + n_out], refs[n_in + k + n_out:n_in + 2 * k + n_out]
        rest = refs[n_in + 2 * k + n_out:]
        scratch, sems = rest[:len(rest) - ns], rest[len(rest) - ns:]

        @pl.when(first())
        def _():
            comm["start"](cins, couts, sems)

        body(*ins, *outs, *scratch)

        @pl.when(last())
        def _():
            comm["finish"](cins, couts, sems)

    return wrapped


def _comm_extras(comm):
    if comm is None:
        return [], [], [], []
    return list(comm["ins"]), [ANY] * len(comm["ins"]), list(comm["outs"]), list(comm["scratch"])


def _mm_body(dims, nk, out_dtype):
    def body(a_ref, b_ref, o_ref, acc=None):
        kk = pl.program_id(2)
        part = lax.dot_general(a_ref[...].astype(BF16), b_ref[...].astype(BF16), (dims, ((), ())),
                               preferred_element_type=F32)
        if nk == 1:
            o_ref[...] = part.astype(out_dtype)
        else:
            @pl.when(kk == 0)
            def _():
                acc[...] = part

            @pl.when(kk > 0)
            def _():
                acc[...] += part

            @pl.when(kk == nk - 1)
            def _():
                o_ref[...] = acc[...].astype(out_dtype)
    return body


def _acc(nk, shape):
    return [pltpu.VMEM(shape, F32)] if nk > 1 else []


def mm_nn(a, w, out_dtype, name, comm=None):
    M, K = a.shape
    S, _, Ns = w.shape
    tm = _pick(M, (1088, 1024, 512, 256, 128))
    tn = _pick(Ns, (1024, 896, 1408, 512, 256, 128))
    tk = _pick(K, (2816, 2048, 1408, 1024, 512, 256, 128))
    nps, nk = Ns // tn, K // tk
    grid = (S * nps, M // tm, nk)
    ids = lambda: [pl.program_id(d) for d in range(3)]
    first = lambda: functools.reduce(jnp.logical_and, [p == 0 for p in ids()])
    last = lambda: functools.reduce(jnp.logical_and, [p == g - 1 for p, g in zip(ids(), grid)])
    cin, cspec, cout, csem = _comm_extras(comm)
    out = pl.pallas_call(
        _hosted(_mm_body(((1,), (0,)), nk, out_dtype), 2, 1, comm, first, last), name=name, grid=grid,
        out_shape=[jax.ShapeDtypeStruct((M, S * Ns), out_dtype)] + cout,
        in_specs=[pl.BlockSpec((tm, tk), lambda j, i, k: (i, k)),
                  pl.BlockSpec((None, tk, tn), lambda j, i, k: (j // nps, k, j % nps))] + cspec,
        out_specs=[pl.BlockSpec((tm, tn), lambda j, i, k: (i, j))] + cspec,
        scratch_shapes=_acc(nk, (tm, tn)) + csem,
        compiler_params=_cp(*(("arbitrary",) * 3 if comm else ("parallel", "parallel", "arbitrary"))),
    )(a, w, *cin)
    return out[0] if comm is None else out


def mm_nt(dy, w, out_dtype, name, comm=None):
    M, N = dy.shape
    S, K, Ns = w.shape
    tm = _pick(M, (1088, 1024, 512, 256, 128))
    tn = _pick(K, (1408, 1024, 512, 256, 128))
    tk = _pick(Ns, (2816, 2048, 1792, 1408, 1024, 896, 512, 256, 128))
    kps, nk = Ns // tk, N // tk
    out = _pcall(
        _mm_body(((1,), (1,)), nk, out_dtype), name=name, grid=(K // tn, M // tm, nk),
        out_shape=[jax.ShapeDtypeStruct((M, K), out_dtype)],
        in_specs=[pl.BlockSpec((tm, tk), lambda j, i, k: (i, k)),
                  pl.BlockSpec((None, tn, tk), lambda j, i, k: (k // kps, j, k % kps))],
        out_specs=[pl.BlockSpec((tm, tn), lambda j, i, k: (i, j))], args=(dy, w),
        scratch=_acc(nk, (tm, tn)), sem=("parallel", "parallel", "arbitrary"), comm=comm)
    return out[0] if comm is None else out


def mm_tn(a, dy, S, out_dtype, name):
    M, K = a.shape
    N = dy.shape[1]
    Ns = N // S
    to = _pick(K, (1024, 512, 256, 128))
    tn = _pick(Ns, (2816, 2048, 1792, 1408, 1024, 896, 512, 256, 128))
    tk = _pick(M, (1088, 1024, 512, 256, 128))
    nps, nk = Ns // tn, M // tk
    return pl.pallas_call(
        _mm_body(((0,), (0,)), nk, out_dtype), name=name, grid=(K // to, S * nps, nk),
        out_shape=jax.ShapeDtypeStruct((S, K, Ns), out_dtype),
        in_specs=[pl.BlockSpec((tk, to), lambda i, j, k: (k, i)),
                  pl.BlockSpec((tk, tn), lambda i, j, k: (k, j))],
        out_specs=pl.BlockSpec((None, to, tn), lambda i, j, k: (j // nps, i, j % nps)),
        scratch_shapes=_acc(nk, (to, tn)), compiler_params=_cp("parallel", "parallel", "arbitrary"),
    )(a, dy)


_DIMS = {"nn": ((1,), (0,)), "nt": ((1,), (1,)), "tn": ((0,), (0,))}


def _dot(a, b, mode):
    return lax.dot_general(a.astype(BF16), b.astype(BF16), (_DIMS[mode], ((), ())), preferred_element_type=F32)


@functools.partial(jax.custom_vjp, nondiff_argnums=(2,))
def mmf(a, b, mode):
    return _dot(a, b, mode)


def _mmf_fwd(a, b, mode):
    return _dot(a, b, mode), (a, b)


def _mmf_bwd(mode, res, ct):
    a, b = res
    if mode == "nn":
        return _dot(ct, b, "nt"), _dot(a, ct, "tn")
    if mode == "nt":
        return _dot(ct, b, "nn"), _dot(ct, a, "tn")
    return _dot(b, ct, "nt"), _dot(a, ct, "nn")


mmf.defvjp(_mmf_fwd, _mmf_bwd)


def _dot_hi(m, g):
    return jnp.dot(m, g, precision=lax.Precision.HIGHEST, preferred_element_type=F32)


@jax.custom_vjp
def cumdot(m, mt, g):
    return _dot_hi(m, g)


def _cumdot_fwd(m, mt, g):
    return _dot_hi(m, g), (m, mt)


def _cumdot_bwd(res, ct):
    m, mt = res
    return jnp.zeros_like(m), jnp.zeros_like(mt), _dot_hi(mt, ct)


cumdot.defvjp(_cumdot_fwd, _cumdot_bwd)


def _rms(x, w):
    return x * lax.rsqrt(jnp.mean(x * x, axis=-1, keepdims=True) + EPS) * w


def _silu(x):
    return x * jax.nn.sigmoid(x)


RT = 16


def _gn_math(has_gate, x, m, gate, lnw, shift, scale):
    xn = x + gate * m if has_gate else x
    h = _rms(xn, lnw) * (1.0 + scale) + shift
    return xn, h


def _seg_spec(width, ncb):
    return pl.BlockSpec((None, RT, width), lambda i: (jnp.minimum(i // ncb, 1), 0, 0))


def gate_norm(x, m, gate, lnw, shift, scale, nc, R, name, comm=None):
    T, D = x.shape
    has_gate = m is not None
    ncb = nc // R

    def body(*refs):
        if has_gate:
            x_ref, m_ref, g_ref, w_ref, sh_ref, sc_ref, xn_ref, h_ref = refs
        else:
            x_ref, w_ref, sh_ref, sc_ref, h_ref = refs

        def step(t, carry):
            rows = pl.ds(pl.multiple_of(t * RT, RT), RT)
            xn, h = _gn_math(has_gate, x_ref[rows, :], m_ref[rows, :] if has_gate else None,
                             g_ref[...] if has_gate else None, w_ref[...], sh_ref[...], sc_ref[...])
            if has_gate:
                xn_ref[rows, :] = xn
            h_ref[rows, :] = h.astype(BF16)
            return carry

        lax.fori_loop(0, R // RT, step, 0)

    row = pl.BlockSpec((R, D), lambda i: (i, 0))
    seg = _seg_spec(D, ncb)
    shared = pl.BlockSpec((None, RT, D), lambda i: (0, 0, 0))
    if has_gate:
        ins, in_specs = (x, m, gate, lnw, shift, scale), [row, row, seg, shared, seg, seg]
        out_shape = (jax.ShapeDtypeStruct((T, D), F32), jax.ShapeDtypeStruct((T, D), BF16))
        out_specs = (row, row)
    else:
        ins, in_specs = (x, lnw, shift, scale), [row, shared, seg, seg]
        out_shape, out_specs = (jax.ShapeDtypeStruct((T, D), BF16),), (row,)
    out = _pcall(body, name=name, grid=(T // R,), out_shape=out_shape, in_specs=in_specs, out_specs=out_specs,
                 args=ins, sem=("parallel",), comm=comm)
    own = tuple(out[:2]) if has_gate else (None, out[0])
    return own if comm is None else own + tuple(out[2 if has_gate else 1:])


def gate_norm_bwd(x, m, gate, lnw, shift, scale, dxn, dh, nc, R, name):
    T, D = x.shape
    has_gate = m is not None
    ncb = nc // R

    def body(*refs):
        if has_gate:
            (x_ref, m_ref, g_ref, w_ref, sh_ref, sc_ref, dxn_ref, dh_ref,
             dx_ref, dm_ref, dg_ref, dw_ref, dsh_ref, dsc_ref) = refs
        else:
            x_ref, w_ref, sh_ref, sc_ref, dxn_ref, dh_ref, dx_ref, dw_ref, dsh_ref, dsc_ref = refs
        i = pl.program_id(0)

        @pl.when(i == 0)
        def _():
            dw_ref[...] = jnp.zeros_like(dw_ref)

        @pl.when((i == 0) | (i == ncb))
        def _():
            dsh_ref[...] = jnp.zeros_like(dsh_ref)
            dsc_ref[...] = jnp.zeros_like(dsc_ref)
            if has_gate:
                dg_ref[...] = jnp.zeros_like(dg_ref)

        def step(t, carry):
            rows = pl.ds(pl.multiple_of(t * RT, RT), RT)
            ct = (dxn_ref[rows, :], dh_ref[rows, :])
            if has_gate:
                _, vjp = jax.vjp(functools.partial(_gn_math, True), x_ref[rows, :], m_ref[rows, :], g_ref[...],
                                 w_ref[...], sh_ref[...], sc_ref[...])
                dx, dm, dg, dw, dsh, dsc = vjp(ct)
                dm_ref[rows, :] = dm.astype(BF16)
                dg_ref[...] += dg
            else:
                f = lambda x_, w_, sh_, sc_: _gn_math(False, x_, None, None, w_, sh_, sc_)[1]
                _, vjp = jax.vjp(f, x_ref[rows, :], w_ref[...], sh_ref[...], sc_ref[...])
                dx, dw, dsh, dsc = vjp(ct[1])
                dx = dx + ct[0]
            dx_ref[rows, :] = dx
            dw_ref[...] += dw
            dsh_ref[...] += dsh
            dsc_ref[...] += dsc
            return carry

        lax.fori_loop(0, R // RT, step, 0)

    row = pl.BlockSpec((R, D), lambda i: (i, 0))
    seg = _seg_spec(D, ncb)
    shared = pl.BlockSpec((None, RT, D), lambda i: (0, 0, 0))
    full, segs, one = jax.ShapeDtypeStruct((T, D), F32), jax.ShapeDtypeStruct((2, RT, D), F32), \
        jax.ShapeDtypeStruct((1, RT, D), F32)
    if has_gate:
        ins = (x, m, gate, lnw, shift, scale, dxn, dh)
        in_specs = [row, row, seg, shared, seg, seg, row, row]
        out_shape = (full, jax.ShapeDtypeStruct((T, D), BF16), segs, one, segs, segs)
        out_specs = (row, row, seg, shared, seg, seg)
    else:
        ins = (x, lnw, shift, scale, dxn, dh)
        in_specs = [row, shared, seg, seg, row, row]
        out_shape = (full, one, segs, segs)
        out_specs = (row, shared, seg, seg)
    out = pl.pallas_call(body, name=name, grid=(T // R,), out_shape=out_shape, in_specs=in_specs,
                         out_specs=out_specs, compiler_params=_cp("arbitrary"))(*ins)
    if has_gate:
        return out
    dx, dw, dsh, dsc = out
    return dx, None, None, dw, dsh, dsc


def gate_loss(x, m, gate, target, nc, R, name):
    T, D = x.shape
    ncb = nc // R

    def body(x_ref, m_ref, g_ref, t_ref, loss_ref, dx_ref, dm_ref, dg_ref):
        i = pl.program_id(0)

        @pl.when(i == 0)
        def _():
            loss_ref[...] = jnp.zeros_like(loss_ref)

        @pl.when((i == 0) | (i == ncb))
        def _():
            dg_ref[...] = jnp.zeros_like(dg_ref)

        live = jnp.where(i >= ncb, 1.0, 0.0).astype(F32)

        def step(t, carry):
            rows = pl.ds(pl.multiple_of(t * RT, RT), RT)
            mm_ = m_ref[rows, :]
            g = g_ref[...]
            e = (x_ref[rows, :] + g * mm_ - t_ref[rows, :]) * live
            dy = e * (1.0 / D)
            loss_ref[...] += 0.5 * e * dy
            dx_ref[rows, :] = dy
            dm_ref[rows, :] = (dy * g).astype(BF16)
            dg_ref[...] += dy * mm_
            return carry

        lax.fori_loop(0, R // RT, step, 0)

    row = pl.BlockSpec((R, D), lambda i: (i, 0))
    seg = _seg_spec(D, ncb)
    return pl.pallas_call(
        body, name=name, grid=(T // R,),
        out_shape=(jax.ShapeDtypeStruct((RT, D), F32), jax.ShapeDtypeStruct((T, D), F32),
                   jax.ShapeDtypeStruct((T, D), BF16), jax.ShapeDtypeStruct((2, RT, D), F32)),
        in_specs=[row, row, seg, pl.BlockSpec((R, D), lambda i: (jnp.maximum(i - ncb, 0), 0))],
        out_specs=(pl.BlockSpec((RT, D), lambda i: (0, 0)), row, row, seg),
        compiler_params=_cp("arbitrary"),
    )(x, m, gate, target)


def _hg_chunk(rev, lb, z, iv, hq, st):
    f = lb + (1.0 - lb) * jax.nn.sigmoid(z)
    g = jnp.log(jnp.maximum(f, F_FLOOR))
    k = (1.0 - lb) * jax.nn.sigmoid(-z)
    q = _silu(hq)
    ri = lax.broadcasted_iota(jnp.int32, (CHUNK, CHUNK), 0)
    ci = lax.broadcasted_iota(jnp.int32, (CHUNK, CHUNK), 1)
    r1 = lax.broadcasted_iota(jnp.int32, (CHUNK, 1), 0)
    seen = (ci >= ri) if rev else (ci <= ri)
    seen_t = (ci <= ri) if rev else (ci >= ri)
    cum = cumdot(seen.astype(F32), seen_t.astype(F32), g)
    tot = jnp.sum(g, axis=0, keepdims=True)
    att = jnp.zeros((CHUNK, CHUNK), F32)
    ref_rows = jnp.zeros_like(g)
    refs = []
    for b in range(CHUNK // SUB):
        before = (r1 >= SUB * (b + 1)) if rev else (r1 < SUB * b)
        r_b = jnp.sum(jnp.where(before, g, 0.0), axis=0, keepdims=True)
        in_b = (r1 >= SUB * b) & (r1 < SUB * (b + 1))
        ref_rows = ref_rows + jnp.where(in_b, r_b, 0.0)
        refs.append(r_b)
    qd = q * jnp.exp(cum - ref_rows)
    for b in range(CHUNK // SUB):
        kd = k * jnp.exp(jnp.minimum(refs[b] - cum, EXP_CLAMP))
        in_b = (ri >= SUB * b) & (ri < SUB * (b + 1))
        att = att + jnp.where(in_b, mmf(qd, kd, "nt"), 0.0)
    att = jnp.where(seen, att, 0.0)
    o = mmf(att, iv, "nn") + mmf(q * jnp.exp(cum), st, "nt")
    st_new = st * jnp.exp(tot) + mmf(iv, k * jnp.exp(tot - cum), "tn")
    return st_new, o


def _hg_cid(rev, i, ncs, n):
    if not rev:
        return i
    return jnp.where(i < ncs, ncs - 1 - i, ncs + n - 1 - i)


def hgrn_fwd(u, lb, rev, zcol, nc, hgw, name):
    T = u.shape[0]
    n, ncs, nh = T // CHUNK, nc // CHUNK, hgw // LANE

    def body(z_ref, v_ref, q_ref, lb_ref, o_ref, s_ref, st):
        i = pl.program_id(0)

        @pl.when(i == 0)
        def _():
            st[...] = jnp.zeros_like(st)

        for h in range(nh):
            cols = slice(h * LANE, (h + 1) * LANE)
            s_ref[h] = st[h]
            s_new, o = _hg_chunk(rev, lb_ref[:, cols], z_ref[:, cols], v_ref[:, cols], q_ref[:, cols], st[h])
            st[h] = s_new
            o_ref[:, cols] = o

    def col(cb):
        return pl.BlockSpec((CHUNK, hgw), lambda i: (_hg_cid(rev, i, ncs, n), cb))

    return pl.pallas_call(
        body, name=name, grid=(n,),
        out_shape=(jax.ShapeDtypeStruct((T, hgw), F32), jax.ShapeDtypeStruct((n, nh, LANE, LANE), F32)),
        in_specs=[col(zcol), col(2), col(7), pl.BlockSpec((1, hgw), lambda i: (0, 0))],
        out_specs=(pl.BlockSpec((CHUNK, hgw), lambda i: (_hg_cid(rev, i, ncs, n), 0)),
                   pl.BlockSpec((None, nh, LANE, LANE), lambda i: (i, 0, 0, 0))),
        scratch_shapes=[pltpu.VMEM((nh, LANE, LANE), F32)], compiler_params=_cp("arbitrary"),
    )(u, u, u, lb)


def hgrn_bwd(u, lb, states, do, rev, zcol, nc, hgw, name):
    T = u.shape[0]
    n, ncs, nh = T // CHUNK, nc // CHUNK, hgw // LANE

    def body(z_ref, v_ref, q_ref, lb_ref, s_ref, do_ref, dz_ref, dv_ref, dq_ref, dlb_ref, dst):
        j = pl.program_id(0)

        @pl.when(j == 0)
        def _():
            dst[...] = jnp.zeros_like(dst)
            dlb_ref[...] = jnp.zeros_like(dlb_ref)

        for h in range(nh):
            cols = slice(h * LANE, (h + 1) * LANE)
            _, vjp = jax.vjp(functools.partial(_hg_chunk, rev), lb_ref[:, cols], z_ref[:, cols], v_ref[:, cols],
                             q_ref[:, cols], s_ref[h])
            dlb, dz, dv, dq, ds = vjp((dst[h], do_ref[:, cols]))
            dst[h] = ds
            dz_ref[:, cols] = dz
            dv_ref[:, cols] = dv
            dq_ref[:, cols] = dq
            dlb_ref[:, cols] += dlb

    def cid(j):
        return _hg_cid(rev, n - 1 - j, ncs, n)

    def col(cb):
        return pl.BlockSpec((CHUNK, hgw), lambda j: (cid(j), cb))

    out = pl.BlockSpec((CHUNK, hgw), lambda j: (cid(j), 0))
    full = jax.ShapeDtypeStruct((T, hgw), F32)
    return pl.pallas_call(
        body, name=name, grid=(n,),
        out_shape=(full, full, full, jax.ShapeDtypeStruct((1, hgw), F32)),
        in_specs=[col(zcol), col(2), col(7), pl.BlockSpec((1, hgw), lambda j: (0, 0)),
                  pl.BlockSpec((None, nh, LANE, LANE), lambda j: (n - 1 - j, 0, 0, 0)), out],
        out_specs=(out, out, out, pl.BlockSpec((1, hgw), lambda j: (0, 0))),
        scratch_shapes=[pltpu.VMEM((nh, LANE, LANE), F32)], compiler_params=_cp("arbitrary"),
    )(u, u, u, lb, states, do)


HT = 128


def _head_group(nh, *col_offsets):
    for g in (4, 2):
        if nh % g == 0 and all(c % g == 0 for c in col_offsets):
            return g
    return 1


def _read_math(ofw, obw, g, w):
    return _rms(ofw + obw, w) * _silu(g)


def hg_read(ofw, obw, u, w, gcol, R, name):
    T, hgw = ofw.shape
    nh = hgw // LANE
    g = _head_group(nh, gcol)

    def body(a_ref, b_ref, g_ref, w_ref, o_ref):
        for j in range(g):
            cols = slice(j * LANE, (j + 1) * LANE)
            for t in range(R // HT):
                rows = slice(t * HT, (t + 1) * HT)
                o_ref[rows, cols] = _read_math(a_ref[rows, cols], b_ref[rows, cols], g_ref[rows, cols],
                                               w_ref[...]).astype(BF16)

    blk = pl.BlockSpec((R, g * LANE), lambda i, h: (i, h))
    return pl.pallas_call(
        body, name=name, grid=(T // R, nh // g), out_shape=jax.ShapeDtypeStruct((T, hgw), BF16),
        in_specs=[blk, blk, pl.BlockSpec((R, g * LANE), lambda i, h: (i, gcol // g + h)),
                  pl.BlockSpec((1, LANE), lambda i, h: (0, 0))],
        out_specs=blk, compiler_params=_cp("parallel", "parallel"),
    )(ofw, obw, u, w)


def hg_read_bwd(ofw, obw, u, w, dout, gcol, ocol, R, name):
    T, hgw = ofw.shape
    nh = hgw // LANE
    g = _head_group(nh, gcol, ocol)

    def body(a_ref, b_ref, g_ref, w_ref, d_ref, do_ref, dg_ref, dw_ref):
        @pl.when(pl.program_id(1) == 0)
        def _():
            dw_ref[...] = jnp.zeros_like(dw_ref)

        for j in range(g):
            cols = slice(j * LANE, (j + 1) * LANE)
            for t in range(R // HT):
                rows = slice(t * HT, (t + 1) * HT)
                _, vjp = jax.vjp(_read_math, a_ref[rows, cols], b_ref[rows, cols], g_ref[rows, cols], w_ref[...])
                da, _, dg, dw = vjp(d_ref[rows, cols])
                do_ref[rows, cols] = da
                dg_ref[rows, cols] = dg
                dw_ref[j] += dw

    blk = pl.BlockSpec((R, g * LANE), lambda h, i: (i, h))
    full = jax.ShapeDtypeStruct((T, hgw), F32)
    return pl.pallas_call(
        body, name=name, grid=(nh // g, T // R), out_shape=(full, full, jax.ShapeDtypeStruct((nh, 1, LANE), F32)),
        in_specs=[blk, blk, pl.BlockSpec((R, g * LANE), lambda h, i: (i, gcol // g + h)),
                  pl.BlockSpec((1, LANE), lambda h, i: (0, 0)),
                  pl.BlockSpec((R, g * LANE), lambda h, i: (i, ocol // g + h))],
        out_specs=(blk, blk, pl.BlockSpec((g, 1, LANE), lambda h, i: (h, 0, 0))),
        compiler_params=_cp("parallel", "arbitrary"),
    )(ofw, obw, u, w, dout)


NA_SPLIT = 2


def _na_step(qw, ow, qraw, kc, vc, *local):
    q = _rms(qraw, qw)
    pieces = [local[i:i + 3] for i in range(0, len(local), 3)]
    scores = [mmf(q, k, "nt") * ATTN_SCALE + b for b, k, _ in pieces] + [mmf(q, kc, "nt") * ATTN_SCALE]
    values = [v for _, _, v in pieces] + [vc]
    m = lax.stop_gradient(functools.reduce(jnp.maximum, [jnp.max(s, axis=-1, keepdims=True) for s in scores]))
    ps = [jnp.exp(s - m) for s in scores]
    inv = 1.0 / sum(jnp.sum(p, axis=-1, keepdims=True) for p in ps)
    return _rms(sum(mmf(p * inv, v, "nn") for p, v in zip(ps, values)), ow)


def _na_geometry(nc, rows):
    ncs = nc // GRID_W
    win_r = min(WIN_R, rows)
    nloc = win_r * GRID_W

    def row_start(s):
        r = jnp.maximum(s - ncs, 0)
        return jnp.clip(r - win_r // 2, 0, rows - win_r)

    def bias_idx(s):
        r = s - ncs
        return jnp.where(s < ncs, win_r, r - jnp.clip(r - win_r // 2, 0, rows - win_r))

    return ncs, win_r, nloc, row_start, bias_idx


def na_bias_tables(rpb, rows):
    win_r = min(WIN_R, rows)
    nh = rpb.shape[0]
    sel_r = np.zeros((win_r, win_r, 2 * WIN_R - 1), np.float32)
    for off in range(win_r):
        for jr in range(win_r):
            sel_r[off, jr, jr - off + WIN_R - 1] = 1.0
    qc = np.arange(GRID_W)[:, None]
    kc = np.arange(GRID_W)[None, :]
    wstart = np.clip(qc - WIN_C // 2, 0, GRID_W - WIN_C)
    ok = (kc >= wstart) & (kc < wstart + WIN_C)
    sel_c = np.zeros((GRID_W, GRID_W, 2 * WIN_C - 1), np.float32)
    sel_c[np.broadcast_to(qc, ok.shape)[ok], np.broadcast_to(kc, ok.shape)[ok], (kc - qc + WIN_C - 1)[ok]] = 1.0
    hi = lax.Precision.HIGHEST
    t = jnp.einsum("hab,oja->hojb", rpb, sel_r, precision=hi)
    t = jnp.einsum("hojb,qkb->hoqjk", t, sel_c, precision=hi)
    t = jnp.where(ok[None, None, :, None, :], t, NEG_INF)
    t = jnp.concatenate([t, jnp.full((nh, 1, GRID_W, win_r, GRID_W), NEG_INF, F32)], axis=1)
    return t.reshape(nh, win_r + 1, GRID_W, win_r * GRID_W)


def kv_prep(u, kw, kcol, vcol, naw, R, name):
    T = u.shape[0]
    g = _head_group(naw // LANE, kcol, vcol)

    def body(k_ref, v_ref, w_ref, kn_ref, vb_ref):
        for j in range(g):
            cols = slice(j * LANE, (j + 1) * LANE)
            kn_ref[:, cols] = _rms(k_ref[:, cols], w_ref[...]).astype(BF16)
        vb_ref[...] = v_ref[...].astype(BF16)

    blk = pl.BlockSpec((R, g * LANE), lambda i, h: (i, h))
    sds = jax.ShapeDtypeStruct((T, naw), BF16)
    return pl.pallas_call(
        body, name=name, grid=(T // R, naw // LANE // g), out_shape=(sds, sds),
        in_specs=[pl.BlockSpec((R, g * LANE), lambda i, h: (i, kcol // g + h)),
                  pl.BlockSpec((R, g * LANE), lambda i, h: (i, vcol // g + h)),
                  pl.BlockSpec((1, LANE), lambda i, h: (0, 0))],
        out_specs=(blk, blk), compiler_params=_cp("parallel", "parallel"),
    )(u, u, kw)


def kv_prep_bwd(u, kw, dkn, kcol, naw, R, name):
    T = u.shape[0]
    nh = naw // LANE
    g = _head_group(nh, kcol)

    def body(k_ref, w_ref, d_ref, dk_ref, dw_ref):
        @pl.when(pl.program_id(1) == 0)
        def _():
            dw_ref[...] = jnp.zeros_like(dw_ref)

        for j in range(g):
            cols = slice(j * LANE, (j + 1) * LANE)
            for t in range(R // HT):
                rows = slice(t * HT, (t + 1) * HT)
                _, vjp = jax.vjp(_rms, k_ref[rows, cols], w_ref[...])
                dk, dw = vjp(d_ref[rows, cols])
                dk_ref[rows, cols] = dk
                dw_ref[j] += dw

    blk = pl.BlockSpec((R, g * LANE), lambda h, i: (i, h))
    return pl.pallas_call(
        body, name=name, grid=(nh // g, T // R),
        out_shape=(jax.ShapeDtypeStruct((T, naw), F32), jax.ShapeDtypeStruct((nh, 1, LANE), F32)),
        in_specs=[pl.BlockSpec((R, g * LANE), lambda h, i: (i, kcol // g + h)),
                  pl.BlockSpec((1, LANE), lambda h, i: (0, 0)), blk],
        out_specs=(blk, pl.BlockSpec((g, 1, LANE), lambda h, i: (h, 0, 0))),
        compiler_params=_cp("parallel", "arbitrary"),
    )(u, kw, dkn)


NA_HB = 4


def _na_operands(j, s, nc, nloc, row_start, q_refs, k_ref, v_ref, qw_ref, ow_ref, b_ref):
    cols = slice(j * LANE, (j + 1) * LANE)
    piece = nloc // NA_SPLIT
    locs = [pl.ds(pl.multiple_of(nc + row_start(s) * GRID_W + p * piece, GRID_W), piece) for p in range(NA_SPLIT)]
    lanes = [slice(p * piece, (p + 1) * piece) for p in range(NA_SPLIT)]
    ops = [qw_ref[...], ow_ref[:, cols], q_refs[j][...], k_ref[0:nc, cols].astype(F32), v_ref[0:nc, cols].astype(F32)]
    for loc, ln in zip(locs, lanes):
        ops += [b_ref[j, :, ln], k_ref[loc, cols].astype(F32), v_ref[loc, cols].astype(F32)]
    return cols, locs, lanes, ops


def _grid_ends(grid):
    ids = lambda: [pl.program_id(d) for d in range(len(grid))]
    first = lambda: functools.reduce(jnp.logical_and, [p == 0 for p in ids()])
    last = lambda: functools.reduce(jnp.logical_and, [p == g - 1 for p, g in zip(ids(), grid)])
    return first, last


def na_fwd(u, kn, vb, qw, ow, bias, qcol, nc, name, comm=None):
    T, naw = kn.shape
    nh, rows = naw // LANE, (T - nc) // GRID_W
    hb = NA_HB if nh % NA_HB == 0 else 1
    ncs, win_r, nloc, row_start, bias_idx = _na_geometry(nc, rows)

    def body(*refs):
        q_refs, (k_ref, v_ref, qw_ref, ow_ref, b_ref, o_ref) = refs[:hb], refs[hb:]
        s = pl.program_id(1)
        for j in range(hb):
            cols, _, _, ops = _na_operands(j, s, nc, nloc, row_start, q_refs, k_ref, v_ref, qw_ref, ow_ref, b_ref)
            o_ref[:, cols] = _na_step(*ops).astype(BF16)

    wide = pl.BlockSpec((T, hb * LANE), lambda g, s: (0, g), pipeline_mode=pl.Buffered(1))
    grid = (nh // hb, T // GRID_W)
    cin, cspec, cout, csem = _comm_extras(comm)
    out = pl.pallas_call(
        _hosted(body, hb + 5, 1, comm, *_grid_ends(grid)), name=name, grid=grid,
        out_shape=[jax.ShapeDtypeStruct((T, naw), BF16)] + cout,
        in_specs=[pl.BlockSpec((GRID_W, LANE), functools.partial(lambda j, g, s: (s, qcol + g * hb + j), j))
                  for j in range(hb)]
        + [wide, wide, pl.BlockSpec((1, LANE), lambda g, s: (0, 0)), pl.BlockSpec((1, hb * LANE), lambda g, s: (0, g)),
           pl.BlockSpec((hb, None, GRID_W, nloc), lambda g, s: (g, bias_idx(s), 0, 0))] + cspec,
        out_specs=[pl.BlockSpec((GRID_W, hb * LANE), lambda g, s: (s, g))] + cspec, scratch_shapes=csem,
        compiler_params=_cp("arbitrary", "arbitrary"),
    )(*([u] * hb), kn, vb, qw, ow, bias, *cin)
    return out[0] if comm is None else out


def na_bwd(u, kn, vb, qw, ow, bias, dout, qcol, ocol, nc, name, comm=None):
    T, naw = kn.shape
    nh, rows = naw // LANE, (T - nc) // GRID_W
    hb = NA_HB if nh % NA_HB == 0 else 1
    ncs, win_r, nloc, row_start, bias_idx = _na_geometry(nc, rows)
    fresh = [0] + [ncs + r for r in range(rows) if r == 0 or r - np.clip(r - win_r // 2, 0, rows - win_r)
                   != (r - 1) - np.clip(r - 1 - win_r // 2, 0, rows - win_r)]

    def body(*refs):
        q_refs, d_refs = refs[:hb], refs[hb:2 * hb]
        k_ref, v_ref, qw_ref, ow_ref, b_ref, dq_ref, dk_ref, dv_ref, db_ref, dqw_ref, dow_ref = refs[2 * hb:]
        s = pl.program_id(1)

        @pl.when(s == 0)
        def _():
            dk_ref[...] = jnp.zeros_like(dk_ref)
            dv_ref[...] = jnp.zeros_like(dv_ref)
            dqw_ref[...] = jnp.zeros_like(dqw_ref)
            dow_ref[...] = jnp.zeros_like(dow_ref)

        first = functools.reduce(lambda a, b: a | b, [s == f for f in fresh])

        @pl.when(first)
        def _():
            db_ref[...] = jnp.zeros_like(db_ref)

        for j in range(hb):
            cols, locs, lanes, ops = _na_operands(j, s, nc, nloc, row_start, q_refs, k_ref, v_ref, qw_ref, ow_ref, b_ref)
            _, vjp = jax.vjp(_na_step, *ops)
            dqw, dow, dq, dkc, dvc, *dlocal = vjp(d_refs[j][...])
            dq_ref[:, cols] = dq
            for p, (loc, ln) in enumerate(zip(locs, lanes)):
                db, dkl, dvl = dlocal[3 * p:3 * p + 3]
                dk_ref[loc, cols] += dkl
                dv_ref[loc, cols] += dvl
                db_ref[j, :, ln] += db
            dk_ref[0:nc, cols] += dkc
            dv_ref[0:nc, cols] += dvc
            dqw_ref[j] += dqw
            dow_ref[j] += dow

    wide = pl.BlockSpec((T, hb * LANE), lambda g, s: (0, g), pipeline_mode=pl.Buffered(1))
    hvec = pl.BlockSpec((hb, 1, LANE), lambda g, s: (g, 0, 0))
    full = jax.ShapeDtypeStruct((T, naw), F32)
    hv = jax.ShapeDtypeStruct((nh, 1, LANE), F32)
    bspec = pl.BlockSpec((hb, None, GRID_W, nloc), lambda g, s: (g, bias_idx(s), 0, 0))
    grid = (nh // hb, T // GRID_W)
    cin, cspec, cout, csem = _comm_extras(comm)
    return pl.pallas_call(
        _hosted(body, 2 * hb + 5, 6, comm, *_grid_ends(grid)), name=name, grid=grid,
        out_shape=[full, full, full, jax.ShapeDtypeStruct(bias.shape, F32), hv, hv] + cout,
        in_specs=[pl.BlockSpec((GRID_W, LANE), functools.partial(lambda j, g, s: (s, qcol + g * hb + j), j))
                  for j in range(hb)]
        + [pl.BlockSpec((GRID_W, LANE), functools.partial(lambda j, g, s: (s, ocol + g * hb + j), j))
           for j in range(hb)]
        + [wide, wide, pl.BlockSpec((1, LANE), lambda g, s: (0, 0)), pl.BlockSpec((1, hb * LANE), lambda g, s: (0, g)),
           bspec] + cspec,
        out_specs=[pl.BlockSpec((GRID_W, hb * LANE), lambda g, s: (s, g)), wide, wide, bspec, hvec, hvec] + cspec,
        scratch_shapes=csem, compiler_params=_cp("arbitrary", "arbitrary"),
    )(*([u] * hb), *([dout] * hb), kn, vb, qw, ow, bias, *cin)


def _halo_specs(R, width, T, col):
    hb = R // 8
    prev = pl.BlockSpec((8, width), lambda j, i: (jnp.maximum(i * hb - 1, 0), col(j, i)))
    nxt = pl.BlockSpec((8, width), lambda j, i: (jnp.minimum((i + 1) * hb, T // 8 - 1), col(j, i)))
    return prev, nxt


def _edge_flags(i, ncb, nblk):
    has_prev = jnp.where((i == 0) | (i == ncb), 0.0, 1.0).astype(F32)
    has_next = jnp.where((i == ncb - 1) | (i == nblk - 1), 0.0, 1.0).astype(F32)
    return has_prev, has_next


def _shift_up(a, prev_row):
    r0 = lax.broadcasted_iota(jnp.int32, a.shape, 0) == 0
    return jnp.where(r0, prev_row, pltpu.roll(a, 1, 0))


def _shift_dn(a, next_row):
    n = a.shape[0]
    rl = lax.broadcasted_iota(jnp.int32, a.shape, 0) == n - 1
    return jnp.where(rl, next_row, pltpu.roll(a, n - 1, 0))


def _conv3(a, prev_row, next_row, w_ref):
    return w_ref[0:1, :] * _shift_up(a, prev_row) + w_ref[1:2, :] * a + w_ref[2:3, :] * _shift_dn(a, next_row)


def _cv_post(b, y, w):
    return _rms(b * y, w)


def short_conv(u, cw, ow, bcol, nc, cvw, R, bwd_dout=None, ocol=0, name=""):
    T = u.shape[0]
    nh, nblk, ncb = cvw // LANE, T // R, nc // R
    bwd = bwd_dout is not None
    g = _head_group(nh, bcol, bcol + nh, bcol + 2 * nh, ocol)
    gw = g * LANE

    def body(b_ref, c_ref, v_ref, cp_ref, vp_ref, cn_ref, vn_ref, cw_ref, ow_ref, *rest):
        i = pl.program_id(1)
        has_prev, has_next = _edge_flags(i, ncb, nblk)
        p = c_ref[...] * v_ref[...]
        y = _conv3(p, cp_ref[7:8, :] * vp_ref[7:8, :] * has_prev, cn_ref[0:1, :] * vn_ref[0:1, :] * has_next, cw_ref)
        if bwd:
            d_ref, db_ref, dy_ref, dow_ref = rest

            @pl.when(i == 0)
            def _():
                dow_ref[...] = jnp.zeros_like(dow_ref)

        for j in range(g):
            cols = slice(j * LANE, (j + 1) * LANE)
            if not bwd:
                rest[0][:, cols] = _cv_post(b_ref[:, cols], y[:, cols], ow_ref[:, cols]).astype(BF16)
            else:
                _, vjp = jax.vjp(_cv_post, b_ref[:, cols], y[:, cols], ow_ref[:, cols])
                db, dy, dow = vjp(d_ref[:, cols])
                db_ref[:, cols] = db
                dy_ref[:, cols] = dy
                dow_ref[j] += dow

    def main(k):
        return pl.BlockSpec((R, gw), lambda h, i: (i, (bcol + k * nh) // g + h))

    cprev, cnext = _halo_specs(R, gw, T, lambda h, i: (bcol + nh) // g + h)
    vprev, vnext = _halo_specs(R, gw, T, lambda h, i: (bcol + 2 * nh) // g + h)
    in_specs = [main(0), main(1), main(2), cprev, vprev, cnext, vnext,
                pl.BlockSpec((3, gw), lambda h, i: (0, h)), pl.BlockSpec((1, gw), lambda h, i: (0, h))]
    ins = [u] * 7 + [cw, ow]
    blk = pl.BlockSpec((R, gw), lambda h, i: (i, h))
    if not bwd:
        out_shape, out_specs = jax.ShapeDtypeStruct((T, cvw), BF16), blk
    else:
        in_specs.append(pl.BlockSpec((R, gw), lambda h, i: (i, ocol // g + h)))
        ins.append(bwd_dout)
        full = jax.ShapeDtypeStruct((T, cvw), F32)
        out_shape = (full, full, jax.ShapeDtypeStruct((nh, 1, LANE), F32))
        out_specs = (blk, blk, pl.BlockSpec((g, 1, LANE), lambda h, i: (h, 0, 0)))
    return pl.pallas_call(body, name=name, grid=(nh // g, nblk), out_shape=out_shape, in_specs=in_specs,
                          out_specs=out_specs, compiler_params=_cp("parallel", "arbitrary"))(*ins)


def conv3_bwd(dy, src, cw, nc, R, W, prod_cols=None, col0=0, out_dtype=F32, name=""):
    T, C = dy.shape
    nblk, ncb = T // R, nc // R
    prod = prod_cols is not None

    def body(*refs):
        if prod:
            (d_ref, dp_ref, dn_ref, c_ref, v_ref, cp_ref, vp_ref, cn_ref, vn_ref, w_ref,
             dc_ref, dv_ref, dw_ref) = refs
        else:
            d_ref, dp_ref, dn_ref, p_ref, pp_ref, pn_ref, w_ref, o_ref, dw_ref = refs
        i = pl.program_id(1)
        has_prev, has_next = _edge_flags(i, ncb, nblk)

        @pl.when(i == 0)
        def _():
            dw_ref[...] = jnp.zeros_like(dw_ref)

        d = d_ref[...]
        d_up = _shift_up(d, dp_ref[7:8, :] * has_prev)
        d_dn = _shift_dn(d, dn_ref[0:1, :] * has_next)
        dp = w_ref[0:1, :] * d_dn + w_ref[1:2, :] * d + w_ref[2:3, :] * d_up
        if prod:
            c, v = c_ref[...], v_ref[...]
            p = c * v
            p_prev, p_next = cp_ref[7:8, :] * vp_ref[7:8, :] * has_prev, cn_ref[0:1, :] * vn_ref[0:1, :] * has_next
            dc_ref[...] = dp * v
            dv_ref[...] = dp * c
        else:
            p = p_ref[...]
            p_prev, p_next = pp_ref[7:8, :] * has_prev, pn_ref[0:1, :] * has_next
            o_ref[...] = dp.astype(out_dtype)
        dw_ref[0:1, :] += jnp.sum(_shift_up(p, p_prev) * d, axis=0, keepdims=True)
        dw_ref[1:2, :] += jnp.sum(p * d, axis=0, keepdims=True)
        dw_ref[2:3, :] += jnp.sum(_shift_dn(p, p_next) * d, axis=0, keepdims=True)

    blk = pl.BlockSpec((R, W), lambda j, i: (i, j))
    dprev, dnext = _halo_specs(R, W, T, lambda j, i: j)
    wspec = pl.BlockSpec((3, W), lambda j, i: (0, j))
    dwspec = pl.BlockSpec((8, W), lambda j, i: (0, j))
    dwshape = jax.ShapeDtypeStruct((8, C), F32)
    if prod:
        ccol, vcol = prod_cols
        cprev, cnext = _halo_specs(R, W, T, lambda j, i: ccol + j)
        vprev, vnext = _halo_specs(R, W, T, lambda j, i: vcol + j)
        in_specs = [blk, dprev, dnext, pl.BlockSpec((R, W), lambda j, i: (i, ccol + j)),
                    pl.BlockSpec((R, W), lambda j, i: (i, vcol + j)), cprev, vprev, cnext, vnext, wspec]
        ins = [dy, dy, dy] + [src] * 6 + [cw]
        full = jax.ShapeDtypeStruct((T, C), F32)
        out_shape, out_specs = (full, full, dwshape), (blk, blk, dwspec)
    else:
        sprev, snext = _halo_specs(R, W, T, lambda j, i: col0 + j)
        in_specs = [blk, dprev, dnext, pl.BlockSpec((R, W), lambda j, i: (i, col0 + j)), sprev, snext,
                    pl.BlockSpec((3, W), lambda j, i: (0, col0 + j))]
        ins = [dy, dy, dy, src, src, src, cw]
        out_shape, out_specs = (jax.ShapeDtypeStruct((T, C), out_dtype), dwshape), (blk, dwspec)
    return pl.pallas_call(body, name=name, grid=(C // W, nblk), out_shape=out_shape, in_specs=in_specs,
                          out_specs=out_specs, compiler_params=_cp("parallel", "arbitrary"))(*ins)


def _pcall(body, *, name, grid, in_specs, out_specs, out_shape, args, scratch=(), sem=None, comm=None):
    cin, cspec, cout, csem = _comm_extras(comm)
    if comm is not None:
        sem = ("arbitrary",) * len(grid)
    return pl.pallas_call(
        _hosted(body, len(in_specs), len(out_specs), comm, *_grid_ends(grid)), name=name, grid=grid,
        out_shape=list(out_shape) + cout, in_specs=list(in_specs) + cspec, out_specs=list(out_specs) + cspec,
        scratch_shapes=list(scratch) + csem, compiler_params=_cp(*sem))(*args, *cin)


def ffn_mid(uf, cw, cb, nc, R, W, name, comm=None):
    T, C = uf.shape
    F = C // 2
    nblk, ncb, nj = T // R, nc // R, F // W

    def body(g_ref, v_ref, gp_ref, vp_ref, gn_ref, vn_ref, wg_ref, wv_ref, bg_ref, bv_ref, a_ref):
        i = pl.program_id(1)
        has_prev, has_next = _edge_flags(i, ncb, nblk)
        yg = _conv3(g_ref[...], gp_ref[7:8, :] * has_prev, gn_ref[0:1, :] * has_next, wg_ref) + bg_ref[...]
        yv = _conv3(v_ref[...], vp_ref[7:8, :] * has_prev, vn_ref[0:1, :] * has_next, wv_ref) + bv_ref[...]
        a_ref[...] = (yg * jax.nn.sigmoid(yg) * yv).astype(BF16)

    gblk = pl.BlockSpec((R, W), lambda j, i: (i, j))
    vblk = pl.BlockSpec((R, W), lambda j, i: (i, nj + j))
    gprev, gnext = _halo_specs(R, W, T, lambda j, i: j)
    vprev, vnext = _halo_specs(R, W, T, lambda j, i: nj + j)
    in_specs = [gblk, vblk, gprev, vprev, gnext, vnext,
                pl.BlockSpec((3, W), lambda j, i: (0, j)), pl.BlockSpec((3, W), lambda j, i: (0, nj + j)),
                pl.BlockSpec((1, W), lambda j, i: (0, j)), pl.BlockSpec((1, W), lambda j, i: (0, nj + j))]
    return _pcall(body, name=name, grid=(nj, nblk), in_specs=in_specs, out_specs=[gblk],
                  out_shape=[jax.ShapeDtypeStruct((T, F), BF16)], args=[uf] * 6 + [cw, cw, cb, cb],
                  sem=("parallel", "arbitrary"), comm=comm)


def ffn_mid_bwd(uf, cw, cb, da, nc, R, W, name, comm=None):
    T, C = uf.shape
    F = C // 2
    nblk, ncb, nj = T // R, nc // R, F // W

    def body(g_ref, v_ref, gp_ref, vp_ref, gn_ref, vn_ref, d_ref, dp_ref, dn_ref, wg_ref, wv_ref, bg_ref, bv_ref,
             dug_ref, duv_ref, dwg_ref, dwv_ref, dbg_ref, dbv_ref):
        i = pl.program_id(1)
        has_prev, has_next = _edge_flags(i, ncb, nblk)

        @pl.when(i == 0)
        def _():
            for r in (dwg_ref, dwv_ref, dbg_ref, dbv_ref):
                r[...] = jnp.zeros_like(r)

        def taps(w_ref):
            return w_ref[0:1, :], w_ref[1:2, :], w_ref[2:3, :]

        def dy_of(yg, yv, d):
            sg = jax.nn.sigmoid(yg)
            return d * yv * (sg * (1.0 + yg * (1.0 - sg))), d * (yg * sg)

        g, v, d = g_ref[...], v_ref[...], d_ref[...]
        (wg0, wg1, wg2), (wv0, wv1, wv2) = taps(wg_ref), taps(wv_ref)
        bg, bv = bg_ref[...], bv_ref[...]
        g_up, g_dn = _shift_up(g, gp_ref[7:8, :] * has_prev), _shift_dn(g, gn_ref[0:1, :] * has_next)
        v_up, v_dn = _shift_up(v, vp_ref[7:8, :] * has_prev), _shift_dn(v, vn_ref[0:1, :] * has_next)
        dyg, dyv = dy_of(wg0 * g_up + wg1 * g + wg2 * g_dn + bg, wv0 * v_up + wv1 * v + wv2 * v_dn + bv, d)
        dyg_p, dyv_p = dy_of(wg0 * gp_ref[6:7, :] + wg1 * gp_ref[7:8, :] + wg2 * g_ref[0:1, :] + bg,
                             wv0 * vp_ref[6:7, :] + wv1 * vp_ref[7:8, :] + wv2 * v_ref[0:1, :] + bv, dp_ref[7:8, :])
        dyg_n, dyv_n = dy_of(wg0 * g_ref[R - 1:R, :] + wg1 * gn_ref[0:1, :] + wg2 * gn_ref[1:2, :] + bg,
                             wv0 * v_ref[R - 1:R, :] + wv1 * vn_ref[0:1, :] + wv2 * vn_ref[1:2, :] + bv, dn_ref[0:1, :])
        dug_ref[...] = (wg0 * _shift_dn(dyg, dyg_n * has_next) + wg1 * dyg
                        + wg2 * _shift_up(dyg, dyg_p * has_prev)).astype(BF16)
        duv_ref[...] = (wv0 * _shift_dn(dyv, dyv_n * has_next) + wv1 * dyv
                        + wv2 * _shift_up(dyv, dyv_p * has_prev)).astype(BF16)
        for ref, ups, mid, dns, dy in ((dwg_ref, g_up, g, g_dn, dyg), (dwv_ref, v_up, v, v_dn, dyv)):
            ref[0:1, :] += jnp.sum(ups * dy, axis=0, keepdims=True)
            ref[1:2, :] += jnp.sum(mid * dy, axis=0, keepdims=True)
            ref[2:3, :] += jnp.sum(dns * dy, axis=0, keepdims=True)
        dbg_ref[...] += jnp.sum(dyg, axis=0, keepdims=True)
        dbv_ref[...] += jnp.sum(dyv, axis=0, keepdims=True)

    gblk = pl.BlockSpec((R, W), lambda j, i: (i, j))
    vblk = pl.BlockSpec((R, W), lambda j, i: (i, nj + j))
    gprev, gnext = _halo_specs(R, W, T, lambda j, i: j)
    vprev, vnext = _halo_specs(R, W, T, lambda j, i: nj + j)
    half = jax.ShapeDtypeStruct((T, F), BF16)
    taps8, bias1 = jax.ShapeDtypeStruct((8, F), F32), jax.ShapeDtypeStruct((1, F), F32)
    return _pcall(
        body, name=name, grid=(nj, nblk), out_shape=(half, half, taps8, taps8, bias1, bias1),
        in_specs=[gblk, vblk, gprev, vprev, gnext, vnext, gblk, gprev, gnext,
                  pl.BlockSpec((3, W), lambda j, i: (0, j)), pl.BlockSpec((3, W), lambda j, i: (0, nj + j)),
                  pl.BlockSpec((1, W), lambda j, i: (0, j)), pl.BlockSpec((1, W), lambda j, i: (0, nj + j))],
        out_specs=(gblk, gblk, pl.BlockSpec((8, W), lambda j, i: (0, j)), pl.BlockSpec((8, W), lambda j, i: (0, j)),
                   pl.BlockSpec((1, W), lambda j, i: (0, j)), pl.BlockSpec((1, W), lambda j, i: (0, j))),
        args=(uf, uf, uf, uf, uf, uf, da, da, da, cw, cw, cb, cb), sem=("parallel", "arbitrary"), comm=comm)


WEIGHTS = ("c_ctx", "w_ada", "b_ada", "ln1_w", "ln2_w", "w_in", "hg_lb_logits", "hg_norm_w", "na_q_norm_w",
           "na_k_norm_w", "na_rpb", "na_out_norm_w", "cv_w", "cv_out_norm_w", "w_out", "w_up", "ffn_conv_w",
           "ffn_conv_b", "w_down")
BIG = ("w_ada", "w_in", "w_out", "w_up", "w_down")
SHARDED_SMALL = ("hg_lb_logits", "cv_w", "ffn_conv_w")


def _flat_rows(parts, dtype):
    flat, layout, off = [], [], 0
    for p in parts:
        layout.append((off, p.shape))
        flat.append(p.reshape(-1).astype(dtype))
        off += p.size
    pad = (-off) % (8 * LANE)
    if pad:
        flat.append(jnp.zeros((pad,), dtype))
    return jnp.concatenate(flat).reshape(-1, LANE), layout


def _unflat(buf, layout):
    v = buf.reshape(-1)
    return [v[off:off + int(np.prod(shape))].reshape(shape) for off, shape in layout]


def _lb_all(logits):
    sm = jax.nn.softmax(logits.astype(F32), axis=1)
    return jnp.cumsum(sm, axis=1) - sm[:, :1]


def _seg(ctx_vec, lat_vec):
    return jnp.broadcast_to(jnp.stack([ctx_vec, lat_vec])[:, None, :], (2, RT, ctx_vec.shape[0]))


def _shared(vec):
    return jnp.broadcast_to(vec[None, None, :], (1, RT, vec.shape[0]))


def kernel(x, c, ctx, c_ctx, w_ada, b_ada, ln1_w, ln2_w, w_in, hg_lb_logits, hg_norm_w, na_q_norm_w, na_k_norm_w, na_rpb, na_out_norm_w, cv_w, cv_out_norm_w, w_out, w_up, ffn_conv_w, ffn_conv_b, w_down, loss_target, m_c_ctx, m_w_ada, m_b_ada, m_ln1_w, m_ln2_w, m_w_in, m_hg_lb_logits, m_hg_norm_w, m_na_q_norm_w, m_na_k_norm_w, m_na_rpb, m_na_out_norm_w, m_cv_w, m_cv_out_norm_w, m_w_out, m_w_up, m_ffn_conv_w, m_ffn_conv_b, m_w_down, v_c_ctx, v_w_ada, v_b_ada, v_ln1_w, v_ln2_w, v_w_in, v_hg_lb_logits, v_hg_norm_w, v_na_q_norm_w, v_na_k_norm_w, v_na_rpb, v_na_out_norm_w, v_cv_w, v_cv_out_norm_w, v_w_out, v_w_up, v_ffn_conv_w, v_ffn_conv_b, v_w_down):
    W = dict(c_ctx=c_ctx, w_ada=w_ada, b_ada=b_ada, ln1_w=ln1_w, ln2_w=ln2_w, w_in=w_in, hg_lb_logits=hg_lb_logits,
             hg_norm_w=hg_norm_w, na_q_norm_w=na_q_norm_w, na_k_norm_w=na_k_norm_w, na_rpb=na_rpb,
             na_out_norm_w=na_out_norm_w, cv_w=cv_w, cv_out_norm_w=cv_out_norm_w, w_out=w_out, w_up=w_up,
             ffn_conv_w=ffn_conv_w, ffn_conv_b=ffn_conv_b, w_down=w_down)
    Mo = dict(c_ctx=m_c_ctx, w_ada=m_w_ada, b_ada=m_b_ada, ln1_w=m_ln1_w, ln2_w=m_ln2_w, w_in=m_w_in,
              hg_lb_logits=m_hg_lb_logits, hg_norm_w=m_hg_norm_w, na_q_norm_w=m_na_q_norm_w,
              na_k_norm_w=m_na_k_norm_w, na_rpb=m_na_rpb, na_out_norm_w=m_na_out_norm_w, cv_w=m_cv_w,
              cv_out_norm_w=m_cv_out_norm_w, w_out=m_w_out, w_up=m_w_up, ffn_conv_w=m_ffn_conv_w,
              ffn_conv_b=m_ffn_conv_b, w_down=m_w_down)
    Vo = dict(c_ctx=v_c_ctx, w_ada=v_w_ada, b_ada=v_b_ada, ln1_w=v_ln1_w, ln2_w=v_ln2_w, w_in=v_w_in,
              hg_lb_logits=v_hg_lb_logits, hg_norm_w=v_hg_norm_w, na_q_norm_w=v_na_q_norm_w,
              na_k_norm_w=v_na_k_norm_w, na_rpb=v_na_rpb, na_out_norm_w=v_na_out_norm_w, cv_w=v_cv_w,
              cv_out_norm_w=v_cv_out_norm_w, w_out=v_w_out, w_up=v_w_up, ffn_conv_w=v_ffn_conv_w,
              ffn_conv_b=v_ffn_conv_b, w_down=v_w_down)

    xi, yi, ci = _me()
    chip = 2 * xi + yi
    dev = 2 * chip + ci
    L, D = x.shape[1], x.shape[2]
    NC = ctx.shape[1]
    T = NC + L
    depth = w_in.shape[0]
    HGW, NAW, CVW = 4 * hg_lb_logits.shape[-1], na_out_norm_w.shape[-1], cv_out_norm_w.shape[-1]
    MIX = HGW + NAW + CVW
    INW, FF2 = 4 * w_in.shape[-1], 4 * w_up.shape[-1]
    F = FF2 // 2
    ADA = 4 * w_ada.shape[-1]
    assert NAW == 2 * HGW and INW == 5 * HGW + 3 * NAW + 3 * CVW and ADA == 6 * D and NC % 128 == 0
    assert L % GRID_W == 0 and T % CHUNK == 0 and depth == 2
    R = math.gcd(NC, 256)
    FW = 512 if F % 512 == 0 else LANE
    rows = L // GRID_W
    nh_hg, nh_na, nh_cv = HGW // LANE, NAW // LANE, CVW // LANE
    kcol = 3 * nh_hg
    vcol = kcol + nh_na
    gcol = vcol + nh_na + nh_hg
    qcol = gcol + nh_hg
    bcol = qcol + nh_na
    mix_na, mix_cv = nh_hg, nh_hg + nh_na

    small1, lay1 = _flat_rows([c[0], hg_lb_logits, cv_w, ffn_conv_w], F32)
    g1 = allgather8([small1], "gather_cond")[0]
    per_dev = [_unflat(g1[d], lay1) for d in range(8)]
    c_all = jnp.stack([p[0] for p in per_dev])
    lb_logits = jnp.concatenate([per_dev[2 * s][1] for s in range(4)], axis=-1)
    cvw_full = jnp.concatenate([per_dev[2 * s][2] for s in range(4)], axis=-1)
    fcw_full = jnp.concatenate([per_dev[2 * s][3] for s in range(4)], axis=-1)
    lb_all, lb_pull = jax.vjp(_lb_all, lb_logits)

    a16 = jnp.concatenate([c_all, c_ctx[None], jnp.zeros((7, D), F32)])
    s16 = _silu(a16)
    wcols = ADA // 4
    b_mine = lax.dynamic_slice_in_dim(b_ada, chip * wcols, wcols, axis=1)
    p_ada = mm_nn(s16, w_ada, F32, "ada_fwd").reshape(16, depth, wcols).transpose(1, 0, 2) + b_mine[:, None, :]
    g2 = allgather8([p_ada.reshape(depth * 16, wcols)], "gather_ada")[0].reshape(8, depth, 16, wcols)
    ada_rows = jnp.concatenate([g2[2 * s] for s in range(4)], axis=-1)
    ada = lax.dynamic_index_in_dim(ada_rows, dev, axis=1, keepdims=False)
    ada_c = ada_rows[:, 8]

    def half_rows(a):
        h = a.shape[0] // 2
        return lax.dynamic_slice_in_dim(a, ci * h, h, axis=0)

    proj = ("w_in", "w_out", "w_up", "w_down")
    wparts = [{n: half_rows(W[n][l]).astype(BF16) for n in proj} for l in range(depth)]

    def stacked(n, g):
        g = g.reshape(4, -1, g.shape[-1])
        return g.reshape(1, -1, g.shape[-1]) if n in ("w_out", "w_down") else g

    Wg = [{}, {}]
    fetch_plan = {"norm_in": [(0, "w_in")], "proj_in_0": [(0, "w_out"), (0, "w_down")], "na_fwd_0": [(0, "w_up")],
                  "ffn_up_0": [(1, "w_up")], "ffn_mid_0": [(1, "w_in"), (1, "w_out")], "ffn_down_0": [(1, "w_down")]}

    def fetching(name, call):
        items = fetch_plan.get(name, [])
        res = call(allgather8_comm([wparts[l][n] for l, n in items]) if items else None)
        res = list(res) if isinstance(res, (list, tuple)) else [res]
        for (l, n), g in zip(items, res[len(res) - len(items):]):
            Wg[l][n] = stacked(n, g)
        own = res[:len(res) - len(items)]
        return own[0] if len(own) == 1 else own

    xcat = jnp.concatenate([ctx[0], x[0]], axis=0)
    mods = []
    for l in range(depth):
        lat, con = jnp.split(ada[l], 6), jnp.split(ada_c[l], 6)
        mods.append(dict(sh1=_seg(con[0], lat[0]), sc1=_seg(con[1], lat[1]), g1=_seg(con[2], lat[2]),
                         sh2=_seg(con[3], lat[3]), sc2=_seg(con[4], lat[4]), g2=_seg(con[5], lat[5]),
                         ln1=_shared(ln1_w[l]), ln2=_shared(ln2_w[l])))
    bias_pull, saved = [], []
    x0 = xcat
    h = fetching("norm_in", lambda cm: gate_norm(x0, None, None, mods[0]["ln1"], mods[0]["sh1"], mods[0]["sc1"], NC, R,
                                                 "norm_in", comm=cm)[1:])
    for l in range(depth):
        md, wl = mods[l], Wg[l]
        u = fetching(f"proj_in_{l}", lambda cm: mm_nn(h, wl["w_in"], F32, f"proj_in_{l}", comm=cm))
        lbf, lbb = lb_all[0, l][None], lb_all[1, l][None]
        o_fw, st_fw = hgrn_fwd(u, lbf, False, 0, NC, HGW, f"hgrn_fw_{l}")
        o_bw, st_bw = hgrn_fwd(u, lbb, True, 1, NC, HGW, f"hgrn_bw_{l}")
        hgn = hg_norm_w[l][None]
        hg = hg_read(o_fw, o_bw, u, hgn, gcol, R, f"hg_read_{l}")
        bias, pull = jax.vjp(lambda r: na_bias_tables(r, rows), na_rpb[l])
        bias_pull.append(pull)
        qn, kn, on = na_q_norm_w[l][None], na_k_norm_w[l][None], na_out_norm_w[l][None]
        keys_n, vals_b = kv_prep(u, kn, kcol, vcol, NAW, R, f"kv_prep_{l}")
        na = fetching(f"na_fwd_{l}", lambda cm: na_fwd(u, keys_n, vals_b, qn, on, bias, qcol, NC, f"na_fwd_{l}", comm=cm))
        cvw_l, cvo = cvw_full[l], cv_out_norm_w[l][None]
        cv = short_conv(u, cvw_l, cvo, bcol, NC, CVW, R, name=f"short_conv_{l}")
        mix = jnp.concatenate([hg, na, cv], axis=1)
        m1 = mm_nn(mix, wl["w_out"], F32, f"proj_out_{l}")
        x1, h2 = gate_norm(x0, m1, md["g1"], md["ln2"], md["sh2"], md["sc2"], NC, R, f"gate_norm_mid_{l}")
        uf = fetching(f"ffn_up_{l}", lambda cm: mm_nn(h2, wl["w_up"], F32, f"ffn_up_{l}", comm=cm))
        fcw_l, fcb_l = fcw_full[l], ffn_conv_b[l][None]
        a = fetching(f"ffn_mid_{l}", lambda cm: ffn_mid(uf, fcw_l, fcb_l, NC, R, FW, f"ffn_mid_{l}", comm=cm))
        m2 = fetching(f"ffn_down_{l}", lambda cm: mm_nn(a, wl["w_down"], F32, f"ffn_down_{l}", comm=cm))
        saved.append(dict(x0=x0, h=h, u=u, o_fw=o_fw, o_bw=o_bw, st_fw=st_fw, st_bw=st_bw, bias=bias, mix=mix,
                          m1=m1, x1=x1, h2=h2, uf=uf, a=a, m2=m2, lbf=lbf, lbb=lbb, keys_n=keys_n, vals_b=vals_b))
        if l + 1 < depth:
            nx = mods[l + 1]
            x0, h = gate_norm(x1, m2, md["g2"], nx["ln1"], nx["sh1"], nx["sc1"], NC, R, f"gate_norm_end_{l}")
    sv, md = saved[-1], mods[-1]
    loss_terms, d_x1, d_m2, d_g2 = gate_loss(sv["x1"], sv["m2"], md["g2"], loss_target[0], NC, R, "gate_loss")
    loss = lax.psum(jnp.sum(loss_terms), ("x", "y", "c"))

    big_grads = [dict() for _ in range(depth)]
    core = ci.astype(jnp.int32).reshape(1)
    pairs, quads = {}, {}
    reduce_plan = {"ffn_down_bwd_0": [(1, "w_down")], "ffn_mid_bwd_0": [(1, "w_up")],
                   "ffn_up_bwd_0": [(1, "w_in"), (1, "w_out")], "na_bwd_0": [(0, "w_down"), (0, "w_up"), (0, "w_out")],
                   "proj_in_bwd_0": [(0, "w_in")]}

    def swapping(l, names, call):
        parts = [big_grads[l][n] for n in names]
        res = list(call(swap_halves_comm(parts)))
        for n, p, g in zip(names, parts, res[len(res) - len(names):]):
            pairs[(l, n)] = pair_sum(p, g, core, f"reduce_pair_sum_{n}_{l}")
        own = res[:len(res) - len(names)]
        return own[0] if len(own) == 1 else own

    def stage(l, names):
        swapping(l, names, lambda cm: _run_comm(cm, f"reduce_sibling_{l}_{names[0]}"))

    def reducing(name, call):
        items = reduce_plan.get(name, [])
        res = call(chip_alltoall_comm([pairs[k] for k in items]) if items else None)
        res = list(res) if isinstance(res, (list, tuple)) else [res]
        for k, q in zip(items, res[len(res) - len(items):]):
            quads[k] = q
        own = res[:len(res) - len(items)]
        return own[0] if len(own) == 1 else own

    small = [dict() for _ in range(depth)]
    d_ada = [None] * depth
    d_lb = [None] * depth
    for l in reversed(range(depth)):
        sv, md, wl = saved[l], mods[l], Wg[l]
        u, uf = sv["u"], sv["uf"]
        big_grads[l]["w_down"] = mm_tn(sv["a"], d_m2, 1, BF16, f"grad_w_down_{l}").reshape(4, F // 4, D)
        d_a = reducing(f"ffn_down_bwd_{l}", lambda cm: mm_nt(d_m2, wl["w_down"], F32, f"ffn_down_bwd_{l}", comm=cm))
        fcw_l, fcb_l = fcw_full[l], ffn_conv_b[l][None]
        dug, duv, dwg, dwv, dbg, dbv = reducing(
            f"ffn_mid_bwd_{l}", lambda cm: ffn_mid_bwd(uf, fcw_l, fcb_l, d_a, NC, R, FW, f"ffn_mid_bwd_{l}", comm=cm))
        d_uf = jnp.concatenate([dug, duv], axis=1)
        small[l]["ffn_conv_w"] = jnp.concatenate([dwg[:3], dwv[:3]], axis=1)
        small[l]["ffn_conv_b"] = jnp.concatenate([dbg[0], dbv[0]])
        big_grads[l]["w_up"] = mm_tn(sv["h2"], d_uf, 4, BF16, f"grad_w_up_{l}")
        d_h2 = reducing(f"ffn_up_bwd_{l}", lambda cm: mm_nt(d_uf, wl["w_up"], F32, f"ffn_up_bwd_{l}", comm=cm))
        d_x0, d_m1, dg1, dln2, dsh2, dsc2 = gate_norm_bwd(sv["x0"], sv["m1"], md["g1"], md["ln2"], md["sh2"], md["sc2"],
                                                          d_x1, d_h2, NC, R, f"gate_norm_mid_bwd_{l}")
        big_grads[l]["w_out"] = mm_tn(sv["mix"], d_m1, 1, BF16, f"grad_w_out_{l}").reshape(4, MIX // 4, D)
        if l == 0:
            d_mix = swapping(0, ("w_down", "w_up", "w_out"),
                             lambda cm: mm_nt(d_m1, wl["w_out"], F32, f"proj_out_bwd_{l}", comm=cm))
        else:
            d_mix = mm_nt(d_m1, wl["w_out"], F32, f"proj_out_bwd_{l}")
        hgn = hg_norm_w[l][None]
        d_o, d_hgg, d_hgn = hg_read_bwd(sv["o_fw"], sv["o_bw"], u, hgn, d_mix, gcol, 0, R, f"hg_read_bwd_{l}")
        dzf, dvf, dqf, dlbf = hgrn_bwd(u, sv["lbf"], sv["st_fw"], d_o, False, 0, NC, HGW, f"hgrn_fw_bwd_{l}")
        dzb, dvb, dqb, dlbb = hgrn_bwd(u, sv["lbb"], sv["st_bw"], d_o, True, 1, NC, HGW, f"hgrn_bw_bwd_{l}")
        d_lb[l] = (dlbf[0], dlbb[0])
        qn, kn, on = na_q_norm_w[l][None], na_k_norm_w[l][None], na_out_norm_w[l][None]
        d_nq, d_keys_n, d_nv, d_bias, d_qn, d_on = reducing(
            f"na_bwd_{l}", lambda cm: na_bwd(u, sv["keys_n"], sv["vals_b"], qn, on, sv["bias"], d_mix, qcol, mix_na, NC,
                                            f"na_bwd_{l}", comm=cm))
        d_nk, d_kn = kv_prep_bwd(u, kn, d_keys_n, kcol, NAW, R, f"kv_prep_bwd_{l}")
        cvw_l, cvo = cvw_full[l], cv_out_norm_w[l][None]
        d_cb, d_cy, d_cvo = short_conv(u, cvw_l, cvo, bcol, NC, CVW, R, bwd_dout=d_mix, ocol=mix_cv,
                                       name=f"short_conv_bwd_{l}")
        gcv = _head_group(nh_cv, bcol + nh_cv, bcol + 2 * nh_cv)
        d_cc, d_cvv, d_cvw = conv3_bwd(d_cy, u, cvw_l, NC, R, gcv * LANE,
                                       prod_cols=((bcol + nh_cv) // gcv, (bcol + 2 * nh_cv) // gcv),
                                       name=f"short_conv_taps_bwd_{l}")
        d_u = jnp.concatenate([dzf, dzb, dvf + dvb, d_nk, d_nv, dqf + dqb, d_hgg, d_nq, d_cb, d_cc, d_cvv],
                              axis=1).astype(BF16)
        big_grads[l]["w_in"] = mm_tn(sv["h"], d_u, 4, BF16, f"grad_w_in_{l}")
        if l > 0:
            d_h = swapping(l, proj, lambda cm: mm_nt(d_u, wl["w_in"], F32, f"proj_in_bwd_{l}", comm=cm))
        else:
            stage(0, ("w_in",))
            d_h = reducing(f"proj_in_bwd_{l}", lambda cm: mm_nt(d_u, wl["w_in"], F32, f"proj_in_bwd_{l}", comm=cm))
        small[l].update(hg_norm_w=d_hgn.sum(0)[0], na_q_norm_w=d_qn.sum(0)[0], na_k_norm_w=d_kn.sum(0)[0],
                        na_out_norm_w=d_on.reshape(-1), na_rpb=bias_pull[l](d_bias)[0], cv_w=d_cvw[:3],
                        cv_out_norm_w=d_cvo.reshape(-1), ln2_w=dln2.sum((0, 1)))
        if l > 0:
            pv, pm = saved[l - 1], mods[l - 1]
            d_x1, d_m2, dg2_prev, dln1, dsh1, dsc1 = gate_norm_bwd(pv["x1"], pv["m2"], pm["g2"], md["ln1"], md["sh1"],
                                                                   md["sc1"], d_x0, d_h, NC, R, f"gate_norm_end_bwd_{l - 1}")
        else:
            d_xin, _, _, dln1, dsh1, dsc1 = gate_norm_bwd(sv["x0"], None, None, md["ln1"], md["sh1"], md["sc1"], d_x0, d_h,
                                                          NC, R, "norm_in_bwd")
        small[l]["ln1_w"] = dln1.sum((0, 1))
        this_g2 = d_g2
        vecs = [v.sum(1) for v in (dsh1, dsc1, dg1, dsh2, dsc2, this_g2)]
        d_ada[l] = jnp.stack([jnp.concatenate([v[s] for v in vecs]) for s in (0, 1)])
        if l > 0:
            d_g2 = dg2_prev
    grad_x = d_xin[NC:][None]
    d_logits = lb_pull(jnp.stack([jnp.stack([d_lb[l][k] for l in range(depth)]) for k in (0, 1)]))[0]

    rep_names = ("ln1_w", "ln2_w", "hg_norm_w", "na_q_norm_w", "na_k_norm_w", "na_rpb", "na_out_norm_w",
                 "cv_out_norm_w", "ffn_conv_b", "cv_w", "ffn_conv_w")
    parts3 = [jnp.stack([small[l][n] for l in range(depth)]) for n in rep_names]
    parts3 += [d_logits, jnp.stack([d_ada[l][0] for l in range(depth)]), jnp.stack([d_ada[l][1] for l in range(depth)])]
    buf3, lay3 = _flat_rows(parts3, F32)
    g3 = allgather8([buf3], "gather_small_grads")[0]
    tot3 = _unflat(sum_leading(g3, F32, "sum_small_grads"), lay3)
    gsm = dict(zip(rep_names, tot3[:len(rep_names)]))
    gsm["hg_lb_logits"] = tot3[len(rep_names)]
    dctx_tot, dlat_tot = tot3[-2], tot3[-1]
    dlat_each = jnp.stack([_unflat(g3[d], lay3)[-1] for d in range(8)], axis=1)
    grads = {n: gsm[n].reshape(W[n].shape) for n in rep_names if n not in SHARDED_SMALL}
    for n in SHARDED_SMALL:
        wl_ = W[n].shape[-1]
        grads[n] = lax.dynamic_slice_in_dim(gsm[n], chip * wl_, wl_, axis=gsm[n].ndim - 1)
    grads["b_ada"] = dctx_tot + dlat_tot

    gw_ada, dms = [], []
    for l in range(depth):
        dm = jnp.concatenate([dlat_each[l], dctx_tot[l][None], jnp.zeros((7, ADA), F32)])
        dms.append(lax.dynamic_slice_in_dim(dm, chip * wcols, wcols, axis=1))
        gw_ada.append(mm_tn(s16, dms[l], 1, F32, f"grad_w_ada_{l}")[0])
    ds16 = mm_nt(jnp.concatenate(dms, axis=1), w_ada, F32, "ada_bwd")
    g4 = allgather8([ds16[8:16]], "gather_cond_grad")[0]
    d_scc = g4[0, 0] + g4[2, 0] + g4[4, 0] + g4[6, 0]
    sg = jax.nn.sigmoid(c_ctx)
    grads["c_ctx"] = d_scc * (sg * (1.0 + c_ctx * (1.0 - sg)))

    keys = [(l, n) for l in range(depth) for n in proj]
    mine = [sum_leading(quads[(l, n)], F32, f"reduce_chip_sum_{n}_{l}") for l, n in keys]

    delta, new_m, new_v = {}, {}, {}
    smalls = [n for n in WEIGHTS if n not in BIG]
    pw, lay_s = _flat_rows([W[n] for n in smalls], F32)
    pg, _ = _flat_rows([grads[n] for n in smalls], F32)
    pm, _ = _flat_rows([Mo[n] for n in smalls], F32)
    pvv, _ = _flat_rows([Vo[n] for n in smalls], F32)
    _, d_, m_, v_, *other = adamw(pw, [pg], pm, pvv, "adamw_small", comm=share_halves_comm(mine))
    for n, dd, mm_, vv in zip(smalls, _unflat(d_, lay_s), _unflat(m_, lay_s), _unflat(v_, lay_s)):
        delta[n], new_m[n], new_v[n] = dd, mm_, vv
    mine_by, other_by = {k: [None] * depth for k in proj}, {k: [None] * depth for k in proj}
    for (l, k), a, b in zip(keys, mine, other):
        mine_by[k][l], other_by[k][l] = a, b
    for n in BIG:
        shp = W[n].shape
        two = lambda a: a.reshape(-1, shp[-1])
        if n == "w_ada":
            g_, d_, m_, v_ = adamw(two(W[n]), gw_ada, two(Mo[n]), two(Vo[n]), f"adamw_{n}")
        else:
            g_, d_, m_, v_ = adamw_halves(two(W[n]), mine_by[n], other_by[n], two(Mo[n]), two(Vo[n]), core, f"adamw_{n}")
        grads[n], delta[n], new_m[n], new_v[n] = g_.reshape(shp), d_.reshape(shp), m_.reshape(shp), v_.reshape(shp)

    return (loss, grad_x, *[grads[n] for n in WEIGHTS], *[delta[n] for n in WEIGHTS],
            *[new_m[n] for n in WEIGHTS], *[new_v[n] for n in WEIGHTS])
```
